```python
import jax, jax.numpy as jnp
from jax import lax
import numpy as np

D_MODEL = 1024
BATCH = 8
SEQ = 8192
DEPTH = 2

CONV_CH = D_MODEL // 2
CONV_K = 3
SGU_WIDTH = D_MODEL // 2
SGU_GROUPS = 4
SGU_GROUP_CH = SGU_WIDTH // SGU_GROUPS
CHUNK = 128
ATT_HEADS = 8
HEAD_DIM = 64
ATT_WIDTH = ATT_HEADS * HEAD_DIM
Q_BLOCK = 128
N_BRANCH = 3
D_FF = ((8 * D_MODEL + 3 * 256 - 1) // (3 * 256)) * 256
IN_COLS = 3 * CONV_CH + 2 * SGU_WIDTH + 3 * ATT_WIDTH + N_BRANCH * D_MODEL
EPS = 1e-6

kernel_name = "hybrid_gated_conv_sgu_stickbreaking"


def _rmsnorm(x, g):
    xf = x.astype(jnp.float32)
    y = xf * lax.rsqrt(jnp.mean(xf * xf, axis=-1, keepdims=True) + EPS)
    return (y * g.astype(jnp.float32)).astype(x.dtype)


def _layernorm(x, g, b):
    xf = x.astype(jnp.float32)
    mu = jnp.mean(xf, axis=-1, keepdims=True)
    xc = xf - mu
    y = xc * lax.rsqrt(jnp.mean(xc * xc, axis=-1, keepdims=True) + EPS)
    return (y * g.astype(jnp.float32) + b.astype(jnp.float32)).astype(x.dtype)


def _split_cols(p):
    widths = (CONV_CH, CONV_CH, CONV_CH, SGU_WIDTH, SGU_WIDTH,
              ATT_WIDTH, ATT_WIDTH, ATT_WIDTH, N_BRANCH * D_MODEL)
    out, off = [], 0
    for w in widths:
        out.append(p[..., off:off + w])
        off += w
    return out


def _short_conv(bg, cg, xa, w, b):
    u = cg * xa
    S = u.shape[1]
    up = jnp.pad(u, ((0, 0), (CONV_K - 1, 0), (0, 0)))
    y = b
    for k in range(CONV_K):
        y = y + w[k] * up[:, k:k + S]
    return bg * y


def _sgu(u, v, ln_g, ln_b, w_s, b_s):
    u = jax.nn.gelu(u, approximate=False)
    v = _layernorm(jax.nn.gelu(v, approximate=False), ln_g, ln_b)
    B, S, _ = v.shape
    vc = v.reshape(B, S // CHUNK, CHUNK, SGU_GROUPS, SGU_GROUP_CH)
    causal = jnp.tril(jnp.ones((CHUNK, CHUNK), dtype=bool))
    w = jnp.where(causal[None], w_s, 0.0).astype(v.dtype)
    mixed = jnp.einsum('gts,bnsgc->bntgc', w, vc) + b_s.T[None, None, :, :, None]
    return u * mixed.reshape(B, S, SGU_WIDTH)


def _stick_breaking(q, k, v):
    B, H, S, d = q.shape
    nb = S // Q_BLOCK
    scale = 1.0 / float(np.sqrt(d))
    qb = q.reshape(B, H, nb, Q_BLOCK, d).transpose(2, 0, 1, 3, 4)
    key_pos = jnp.arange(S)

    def block(args):
        qi, i = args
        z = jnp.einsum('bhqd,bhkd->bhqk', qi, k).astype(jnp.float32) * scale
        qpos = i * Q_BLOCK + jnp.arange(Q_BLOCK)
        mask = key_pos[None, :] < qpos[:, None]
        log_beta = jax.nn.log_sigmoid(z)
        log_1m = jnp.where(mask, jax.nn.log_sigmoid(-z), 0.0)
        after = lax.cumsum(log_1m, axis=3, reverse=True) - log_1m
        a = jnp.where(mask, jnp.exp(log_beta + after), 0.0)
        return jnp.einsum('bhqk,bhkd->bhqd', a.astype(v.dtype), v)

    out = lax.map(block, (qb, jnp.arange(nb)))
    return out.transpose(1, 2, 0, 3, 4).reshape(B, H, S, d)


def _fwd_setup_inputs(seed: int = 0) -> dict:
    key = jax.random.key(seed)
    ks = jax.random.split(key, 17)
    L, D = DEPTH, D_MODEL
    nrm = lambda k, shape, s: jax.random.normal(k, shape, jnp.float32) * s
    return {
        "x": jax.random.normal(ks[0], (BATCH, SEQ, D), jnp.float32),
        "mix_norm_g": 1.0 + nrm(ks[1], (L, D), 0.02),
        "w_in": nrm(ks[2], (L, D, IN_COLS), D ** -0.5),
        "b_gate": nrm(ks[3], (L, N_BRANCH * D), 0.02),
        "conv_w": nrm(ks[4], (L, CONV_K, CONV_CH), CONV_K ** -0.5),
        "conv_b": nrm(ks[5], (L, CONV_CH), 0.02),
        "sgu_ln_g": 1.0 + nrm(ks[6], (L, SGU_WIDTH), 0.02),
        "sgu_ln_b": nrm(ks[7], (L, SGU_WIDTH), 0.02),
        "sgu_w": nrm(ks[8], (L, SGU_GROUPS, CHUNK, CHUNK), CHUNK ** -0.5),
        "sgu_b": nrm(ks[9], (L, SGU_GROUPS, CHUNK), 0.02),
        "q_norm_g": 1.0 + nrm(ks[10], (L, HEAD_DIM), 0.02),
        "k_norm_g": 1.0 + nrm(ks[11], (L, HEAD_DIM), 0.02),
        "w_branch_out": nrm(ks[12], (L, N_BRANCH, CONV_CH, D), CONV_CH ** -0.5),
        "w_o": nrm(ks[13], (L, D, D), D ** -0.5),
        "ffn_norm_g": 1.0 + nrm(ks[14], (L, D), 0.02),
        "w_gate_up": nrm(ks[15], (L, D, 2 * D_FF), D ** -0.5),
        "w_down": nrm(ks[16], (L, D_FF, D), D_FF ** -0.5),
    }


def _fwd_reference(x, mix_norm_g, w_in, b_gate, conv_w, conv_b, sgu_ln_g, sgu_ln_b, sgu_w, sgu_b,
              q_norm_g, k_norm_g, w_branch_out, w_o, ffn_norm_g, w_gate_up, w_down):
    B, S, D = x.shape
    for l in range(DEPTH):
        h = _rmsnorm(x, mix_norm_g[l])
        a_b, a_c, a_x, s_u, s_v, q, k, v, gates = _split_cols(h @ w_in[l])

        ya = _short_conv(a_b, a_c, a_x, conv_w[l], conv_b[l])
        yb = _sgu(s_u, s_v, sgu_ln_g[l], sgu_ln_b[l], sgu_w[l], sgu_b[l])

        qh = _rmsnorm(q.reshape(B, S, ATT_HEADS, HEAD_DIM), q_norm_g[l]).transpose(0, 2, 1, 3)
        kh = _rmsnorm(k.reshape(B, S, ATT_HEADS, HEAD_DIM), k_norm_g[l]).transpose(0, 2, 1, 3)
        vh = v.reshape(B, S, ATT_HEADS, HEAD_DIM).transpose(0, 2, 1, 3)
        yc = _stick_breaking(qh, kh, vh).transpose(0, 2, 1, 3).reshape(B, S, ATT_WIDTH)

        ys = jnp.stack([ya, yb, yc], axis=2)
        yd = jnp.einsum('bsnc,ncd->bsnd', ys, w_branch_out[l])
        g = jax.nn.sigmoid(gates + b_gate[l]).reshape(B, S, N_BRANCH, D)
        merged = jnp.sum(g * yd, axis=2)
        x = x + merged @ w_o[l]

        h2 = _rmsnorm(x, ffn_norm_g[l])
        gu = h2 @ w_gate_up[l]
        x = x + (jax.nn.silu(gu[..., :D_FF]) * gu[..., D_FF:]) @ w_down[l]
    return x


import jax as _jax
import jax.numpy as _jnp

TWIN_FORMAT = 'train_step'
FWD_PARAMS = ['x', 'mix_norm_g', 'w_in', 'b_gate', 'conv_w', 'conv_b', 'sgu_ln_g', 'sgu_ln_b', 'sgu_w', 'sgu_b', 'q_norm_g', 'k_norm_g', 'w_branch_out', 'w_o', 'ffn_norm_g', 'w_gate_up', 'w_down']
TWIN_WEIGHTS = ['mix_norm_g', 'w_in', 'b_gate', 'conv_w', 'conv_b', 'sgu_ln_g', 'sgu_ln_b', 'sgu_w', 'sgu_b', 'q_norm_g', 'k_norm_g', 'w_branch_out', 'w_o', 'ffn_norm_g', 'w_gate_up', 'w_down']
TWIN_DIFF_INPUT = 'x'
TWIN_INPUTS = ['x', 'mix_norm_g', 'w_in', 'b_gate', 'conv_w', 'conv_b', 'sgu_ln_g', 'sgu_ln_b', 'sgu_w', 'sgu_b', 'q_norm_g', 'k_norm_g', 'w_branch_out', 'w_o', 'ffn_norm_g', 'w_gate_up', 'w_down', 'loss_target', 'm_mix_norm_g', 'm_w_in', 'm_b_gate', 'm_conv_w', 'm_conv_b', 'm_sgu_ln_g', 'm_sgu_ln_b', 'm_sgu_w', 'm_sgu_b', 'm_q_norm_g', 'm_k_norm_g', 'm_w_branch_out', 'm_w_o', 'm_ffn_norm_g', 'm_w_gate_up', 'm_w_down', 'v_mix_norm_g', 'v_w_in', 'v_b_gate', 'v_conv_w', 'v_conv_b', 'v_sgu_ln_g', 'v_sgu_ln_b', 'v_sgu_w', 'v_sgu_b', 'v_q_norm_g', 'v_k_norm_g', 'v_w_branch_out', 'v_w_o', 'v_ffn_norm_g', 'v_w_gate_up', 'v_w_down']
TWIN_OUTPUTS = ['loss', 'grad_x', 'grad_mix_norm_g', 'grad_w_in', 'grad_b_gate', 'grad_conv_w', 'grad_conv_b', 'grad_sgu_ln_g', 'grad_sgu_ln_b', 'grad_sgu_w', 'grad_sgu_b', 'grad_q_norm_g', 'grad_k_norm_g', 'grad_w_branch_out', 'grad_w_o', 'grad_ffn_norm_g', 'grad_w_gate_up', 'grad_w_down', 'delta_mix_norm_g', 'delta_w_in', 'delta_b_gate', 'delta_conv_w', 'delta_conv_b', 'delta_sgu_ln_g', 'delta_sgu_ln_b', 'delta_sgu_w', 'delta_sgu_b', 'delta_q_norm_g', 'delta_k_norm_g', 'delta_w_branch_out', 'delta_w_o', 'delta_ffn_norm_g', 'delta_w_gate_up', 'delta_w_down', 'new_m_mix_norm_g', 'new_m_w_in', 'new_m_b_gate', 'new_m_conv_w', 'new_m_conv_b', 'new_m_sgu_ln_g', 'new_m_sgu_ln_b', 'new_m_sgu_w', 'new_m_sgu_b', 'new_m_q_norm_g', 'new_m_k_norm_g', 'new_m_w_branch_out', 'new_m_w_o', 'new_m_ffn_norm_g', 'new_m_w_gate_up', 'new_m_w_down', 'new_v_mix_norm_g', 'new_v_w_in', 'new_v_b_gate', 'new_v_conv_w', 'new_v_conv_b', 'new_v_sgu_ln_g', 'new_v_sgu_ln_b', 'new_v_sgu_w', 'new_v_sgu_b', 'new_v_q_norm_g', 'new_v_k_norm_g', 'new_v_w_branch_out', 'new_v_w_o', 'new_v_ffn_norm_g', 'new_v_w_gate_up', 'new_v_w_down']
TWIN_LEAF_KINDS = {'loss': 'loss', 'grad_x': 'grad_x', 'grad_mix_norm_g': 'grad_w', 'grad_w_in': 'grad_w', 'grad_b_gate': 'grad_w', 'grad_conv_w': 'grad_w', 'grad_conv_b': 'grad_w', 'grad_sgu_ln_g': 'grad_w', 'grad_sgu_ln_b': 'grad_w', 'grad_sgu_w': 'grad_w', 'grad_sgu_b': 'grad_w', 'grad_q_norm_g': 'grad_w', 'grad_k_norm_g': 'grad_w', 'grad_w_branch_out': 'grad_w', 'grad_w_o': 'grad_w', 'grad_ffn_norm_g': 'grad_w', 'grad_w_gate_up': 'grad_w', 'grad_w_down': 'grad_w', 'delta_mix_norm_g': 'delta_w', 'delta_w_in': 'delta_w', 'delta_b_gate': 'delta_w', 'delta_conv_w': 'delta_w', 'delta_conv_b': 'delta_w', 'delta_sgu_ln_g': 'delta_w', 'delta_sgu_ln_b': 'delta_w', 'delta_sgu_w': 'delta_w', 'delta_sgu_b': 'delta_w', 'delta_q_norm_g': 'delta_w', 'delta_k_norm_g': 'delta_w', 'delta_w_branch_out': 'delta_w', 'delta_w_o': 'delta_w', 'delta_ffn_norm_g': 'delta_w', 'delta_w_gate_up': 'delta_w', 'delta_w_down': 'delta_w', 'new_m_mix_norm_g': 'new_m', 'new_m_w_in': 'new_m', 'new_m_b_gate': 'new_m', 'new_m_conv_w': 'new_m', 'new_m_conv_b': 'new_m', 'new_m_sgu_ln_g': 'new_m', 'new_m_sgu_ln_b': 'new_m', 'new_m_sgu_w': 'new_m', 'new_m_sgu_b': 'new_m', 'new_m_q_norm_g': 'new_m', 'new_m_k_norm_g': 'new_m', 'new_m_w_branch_out': 'new_m', 'new_m_w_o': 'new_m', 'new_m_ffn_norm_g': 'new_m', 'new_m_w_gate_up': 'new_m', 'new_m_w_down': 'new_m', 'new_v_mix_norm_g': 'new_v', 'new_v_w_in': 'new_v', 'new_v_b_gate': 'new_v', 'new_v_conv_w': 'new_v', 'new_v_conv_b': 'new_v', 'new_v_sgu_ln_g': 'new_v', 'new_v_sgu_ln_b': 'new_v', 'new_v_sgu_w': 'new_v', 'new_v_sgu_b': 'new_v', 'new_v_q_norm_g': 'new_v', 'new_v_k_norm_g': 'new_v', 'new_v_w_branch_out': 'new_v', 'new_v_w_o': 'new_v', 'new_v_ffn_norm_g': 'new_v', 'new_v_w_gate_up': 'new_v', 'new_v_w_down': 'new_v'}


def _forward(args):
    return _fwd_reference(*[args[k] for k in FWD_PARAMS])


def _output_shape():
    def fwd():
        inp = _fwd_setup_inputs(0)
        return _fwd_reference(*[inp[k] for k in FWD_PARAMS])
    out = _jax.eval_shape(fwd)
    return out.shape, out.dtype

N_MICROBATCH = 1
ADAM_LR = 0.001
ADAM_B1 = 0.9
ADAM_B2 = 0.999
ADAM_EPS = 1e-08
ADAM_WD = 0.01
ADAM_STEP = 10
PER_EXAMPLE_BATCH_AXIS = {'x': 0, 'loss_target': 0}
SHARED_INPUTS = []
_WEIGHT_DTYPES = {'mix_norm_g': _jnp.float32, 'w_in': _jnp.float32, 'b_gate': _jnp.float32, 'conv_w': _jnp.float32, 'conv_b': _jnp.float32, 'sgu_ln_g': _jnp.float32, 'sgu_ln_b': _jnp.float32, 'sgu_w': _jnp.float32, 'sgu_b': _jnp.float32, 'q_norm_g': _jnp.float32, 'k_norm_g': _jnp.float32, 'w_branch_out': _jnp.float32, 'w_o': _jnp.float32, 'ffn_norm_g': _jnp.float32, 'w_gate_up': _jnp.float32, 'w_down': _jnp.float32}
MOMENT_SCALE = {'mix_norm_g': 7.188872e+01, 'w_in': 6.374298e-01, 'b_gate': 4.186200e+00, 'conv_w': 2.127496e+01, 'conv_b': 1.668256e+00, 'sgu_ln_g': 7.872364e+00, 'sgu_ln_b': 3.813892e-01, 'sgu_w': 7.137087e-01, 'sgu_b': 5.168901e-01, 'q_norm_g': 1.857442e+01, 'k_norm_g': 1.853561e+01, 'w_branch_out': 6.813736e-01, 'w_o': 1.135664e+00, 'ffn_norm_g': 4.962822e+01, 'w_gate_up': 3.735885e-01, 'w_down': 6.492140e-01}


def _to_microbatches(a, axis):
    t = _jnp.moveaxis(a, axis, 0)
    t = t.reshape((N_MICROBATCH, t.shape[0] // N_MICROBATCH) + t.shape[1:])
    return _jnp.moveaxis(t, 1, axis + 1)


def setup_inputs(seed: int = 0) -> dict:
    inp = _fwd_setup_inputs(seed)
    key = _jax.random.fold_in(_jax.random.key(seed), 7919)
    shape, _ = _output_shape()
    out = dict(inp)
    out["loss_target"] = _jax.random.normal(_jax.random.fold_in(key, 0), shape, _jnp.float32)
    for i, name in enumerate(TWIN_WEIGHTS):
        w = inp[name].astype(_jnp.float32)
        if MOMENT_SCALE is None:
            s = _jnp.sqrt(_jnp.mean(_jnp.square(w)) + 1e-30)
        else:
            s = MOMENT_SCALE[name]
        km, kv = _jax.random.split(_jax.random.fold_in(key, i + 1))
        out[name] = w
        out["m_" + name] = s * _jax.random.normal(km, w.shape, _jnp.float32)
        out["v_" + name] = (s * s) * _jax.random.uniform(kv, w.shape, _jnp.float32, 0.5, 1.5)
    if N_MICROBATCH > 1:
        for name, axis in PER_EXAMPLE_BATCH_AXIS.items():
            out[name] = _to_microbatches(out[name], axis)
    return {'x': out['x'], 'mix_norm_g': out['mix_norm_g'], 'w_in': out['w_in'], 'b_gate': out['b_gate'], 'conv_w': out['conv_w'], 'conv_b': out['conv_b'], 'sgu_ln_g': out['sgu_ln_g'], 'sgu_ln_b': out['sgu_ln_b'], 'sgu_w': out['sgu_w'], 'sgu_b': out['sgu_b'], 'q_norm_g': out['q_norm_g'], 'k_norm_g': out['k_norm_g'], 'w_branch_out': out['w_branch_out'], 'w_o': out['w_o'], 'ffn_norm_g': out['ffn_norm_g'], 'w_gate_up': out['w_gate_up'], 'w_down': out['w_down'], 'loss_target': out['loss_target'], 'm_mix_norm_g': out['m_mix_norm_g'], 'm_w_in': out['m_w_in'], 'm_b_gate': out['m_b_gate'], 'm_conv_w': out['m_conv_w'], 'm_conv_b': out['m_conv_b'], 'm_sgu_ln_g': out['m_sgu_ln_g'], 'm_sgu_ln_b': out['m_sgu_ln_b'], 'm_sgu_w': out['m_sgu_w'], 'm_sgu_b': out['m_sgu_b'], 'm_q_norm_g': out['m_q_norm_g'], 'm_k_norm_g': out['m_k_norm_g'], 'm_w_branch_out': out['m_w_branch_out'], 'm_w_o': out['m_w_o'], 'm_ffn_norm_g': out['m_ffn_norm_g'], 'm_w_gate_up': out['m_w_gate_up'], 'm_w_down': out['m_w_down'], 'v_mix_norm_g': out['v_mix_norm_g'], 'v_w_in': out['v_w_in'], 'v_b_gate': out['v_b_gate'], 'v_conv_w': out['v_conv_w'], 'v_conv_b': out['v_conv_b'], 'v_sgu_ln_g': out['v_sgu_ln_g'], 'v_sgu_ln_b': out['v_sgu_ln_b'], 'v_sgu_w': out['v_sgu_w'], 'v_sgu_b': out['v_sgu_b'], 'v_q_norm_g': out['v_q_norm_g'], 'v_k_norm_g': out['v_k_norm_g'], 'v_w_branch_out': out['v_w_branch_out'], 'v_w_o': out['v_w_o'], 'v_ffn_norm_g': out['v_ffn_norm_g'], 'v_w_gate_up': out['v_w_gate_up'], 'v_w_down': out['v_w_down']}


def _loss(weights, diff, rest, loss_target):
    with _jax.named_scope("forward"):
        args = {**rest, TWIN_DIFF_INPUT: diff, **{k: w.astype(_WEIGHT_DTYPES[k]) for k, w in weights.items()}}
        y = _forward(args)
    with _jax.named_scope("loss_head"):
        err = _jnp.square(y.astype(_jnp.float32) - loss_target)
        return 0.5 * _jnp.sum(_jnp.mean(err, axis=-1)) if err.ndim else 0.5 * err


def _adamw(w, g, m, v):
    m = ADAM_B1 * m + (1.0 - ADAM_B1) * g
    v = ADAM_B2 * v + (1.0 - ADAM_B2) * _jnp.square(g)
    m_hat = m / (1.0 - ADAM_B1 ** ADAM_STEP)
    v_hat = v / (1.0 - ADAM_B2 ** ADAM_STEP)
    delta = -ADAM_LR * (m_hat / (_jnp.sqrt(v_hat) + ADAM_EPS) + ADAM_WD * w)
    return delta, m, v


def reference(x, mix_norm_g, w_in, b_gate, conv_w, conv_b, sgu_ln_g, sgu_ln_b, sgu_w, sgu_b, q_norm_g, k_norm_g, w_branch_out, w_o, ffn_norm_g, w_gate_up, w_down, loss_target, m_mix_norm_g, m_w_in, m_b_gate, m_conv_w, m_conv_b, m_sgu_ln_g, m_sgu_ln_b, m_sgu_w, m_sgu_b, m_q_norm_g, m_k_norm_g, m_w_branch_out, m_w_o, m_ffn_norm_g, m_w_gate_up, m_w_down, v_mix_norm_g, v_w_in, v_b_gate, v_conv_w, v_conv_b, v_sgu_ln_g, v_sgu_ln_b, v_sgu_w, v_sgu_b, v_q_norm_g, v_k_norm_g, v_w_branch_out, v_w_o, v_ffn_norm_g, v_w_gate_up, v_w_down):
    given = dict(x=x, mix_norm_g=mix_norm_g, w_in=w_in, b_gate=b_gate, conv_w=conv_w, conv_b=conv_b, sgu_ln_g=sgu_ln_g, sgu_ln_b=sgu_ln_b, sgu_w=sgu_w, sgu_b=sgu_b, q_norm_g=q_norm_g, k_norm_g=k_norm_g, w_branch_out=w_branch_out, w_o=w_o, ffn_norm_g=ffn_norm_g, w_gate_up=w_gate_up, w_down=w_down, loss_target=loss_target, m_mix_norm_g=m_mix_norm_g, m_w_in=m_w_in, m_b_gate=m_b_gate, m_conv_w=m_conv_w, m_conv_b=m_conv_b, m_sgu_ln_g=m_sgu_ln_g, m_sgu_ln_b=m_sgu_ln_b, m_sgu_w=m_sgu_w, m_sgu_b=m_sgu_b, m_q_norm_g=m_q_norm_g, m_k_norm_g=m_k_norm_g, m_w_branch_out=m_w_branch_out, m_w_o=m_w_o, m_ffn_norm_g=m_ffn_norm_g, m_w_gate_up=m_w_gate_up, m_w_down=m_w_down, v_mix_norm_g=v_mix_norm_g, v_w_in=v_w_in, v_b_gate=v_b_gate, v_conv_w=v_conv_w, v_conv_b=v_conv_b, v_sgu_ln_g=v_sgu_ln_g, v_sgu_ln_b=v_sgu_ln_b, v_sgu_w=v_sgu_w, v_sgu_b=v_sgu_b, v_q_norm_g=v_q_norm_g, v_k_norm_g=v_k_norm_g, v_w_branch_out=v_w_branch_out, v_w_o=v_w_o, v_ffn_norm_g=v_ffn_norm_g, v_w_gate_up=v_w_gate_up, v_w_down=v_w_down)
    weights = {n: given[n] for n in TWIN_WEIGHTS}
    shared = {n: given[n] for n in SHARED_INPUTS}
    per_example = {n: given[n] for n in ['x']}
    grad_fn = _jax.value_and_grad(_loss, argnums=(0, 1))

    def one_microbatch(ex, loss_target):
        ex = dict(ex)
        diff = ex.pop(TWIN_DIFF_INPUT)
        return grad_fn(weights, diff, {**shared, **ex}, loss_target)

    if N_MICROBATCH == 1:
        loss, (grad_w, grad_x) = one_microbatch(per_example, given["loss_target"])
    else:
        def body(carry, xs):
            loss_sum, grad_sum = carry
            l_k, (gw_k, gx_k) = one_microbatch(xs[0], xs[1])
            with _jax.named_scope("update"):
                return (loss_sum + l_k, _jax.tree.map(_jnp.add, grad_sum, gw_k)), gx_k

        init = (_jnp.zeros((), _jnp.float32), _jax.tree.map(_jnp.zeros_like, weights))
        (loss, grad_w), grad_x = _jax.lax.scan(body, init, (per_example, given["loss_target"]))
    with _jax.named_scope("update"):
        delta_w, new_m, new_v = {}, {}, {}
        for n in TWIN_WEIGHTS:
            delta_w[n], new_m[n], new_v[n] = _adamw(weights[n], grad_w[n], given["m_" + n], given["v_" + n])
    return (loss, grad_x, *[grad_w[n] for n in TWIN_WEIGHTS], *[delta_w[n] for n in TWIN_WEIGHTS],
            *[new_m[n] for n in TWIN_WEIGHTS], *[new_v[n] for n in TWIN_WEIGHTS])
```

```python
import functools
import math

import jax
import jax.numpy as jnp
from jax import lax
from jax.experimental import pallas as pl
from jax.experimental.pallas import tpu as pltpu

F32 = jnp.float32
BF16 = jnp.bfloat16
MESH = pl.DeviceIdType.MESH

N_DEV = 8
LANES = 128
VMEM_LIMIT_BYTES = 56 * 1024 * 1024
EPS = 1e-6
ADAM_LR, ADAM_B1, ADAM_B2, ADAM_EPS, ADAM_WD, ADAM_STEP = 0.001, 0.9, 0.999, 1e-08, 0.01, 10
BIG = ("w_in", "w_branch_out", "w_o", "w_gate_up", "w_down")
SMALL = ("mix_norm_g", "b_gate", "conv_w", "conv_b", "sgu_ln_g", "sgu_ln_b", "sgu_w", "sgu_b",
         "q_norm_g", "k_norm_g", "ffn_norm_g")
WEIGHTS = ("mix_norm_g", "w_in", "b_gate", "conv_w", "conv_b", "sgu_ln_g", "sgu_ln_b", "sgu_w", "sgu_b",
           "q_norm_g", "k_norm_g", "w_branch_out", "w_o", "ffn_norm_g", "w_gate_up", "w_down")
SHARD_AXIS = {"w_in": 2, "w_branch_out": 3, "w_o": 1, "w_gate_up": 2, "w_down": 1}


def _tile(n, cap, mult):
    best = None
    for t in range(mult, min(n, cap) + 1, mult):
        if n % t == 0:
            best = t
    return best if best is not None else n


def _params(*sem):
    return pltpu.CompilerParams(dimension_semantics=sem if sem else None, vmem_limit_bytes=VMEM_LIMIT_BYTES)


def _erf(x):
    return lax.erf(x)


def _gelu(x):
    return 0.5 * x * (1.0 + _erf(x * (1.0 / math.sqrt(2.0))))


def _gelu_grad(x):
    return 0.5 * (1.0 + _erf(x * (1.0 / math.sqrt(2.0)))) + x * jnp.exp(-0.5 * x * x) * (1.0 / math.sqrt(2.0 * math.pi))


def _sigmoid(x):
    return 1.0 / (1.0 + jnp.exp(-x))


def _matmul(a, b, *, name, res=None, out_dtype=F32):
    M, K = a.shape
    _, N = b.shape
    tm, tn, tk = _tile(M, 512, 8), _tile(N, 1536, LANES), _tile(K, 1536, LANES)
    nk = K // tk
    has_res = res is not None

    def body(*refs):
        refs = list(refs)
        acc = refs.pop() if nk > 1 else None
        if has_res:
            a_ref, b_ref, r_ref, o_ref = refs
        else:
            a_ref, b_ref, o_ref = refs
        k = pl.program_id(2)
        part = jnp.dot(a_ref[...], b_ref[...], preferred_element_type=F32)

        def finish(v):
            if has_res:
                v = v + r_ref[...]
            o_ref[...] = v.astype(out_dtype)

        if nk == 1:
            finish(part)
        else:
            @pl.when(k == 0)
            def _():
                acc[...] = part

            @pl.when(jnp.logical_and(k > 0, k < nk - 1))
            def _():
                acc[...] += part

            @pl.when(k == nk - 1)
            def _():
                finish(acc[...] + part)

    in_specs = [pl.BlockSpec((tm, tk), lambda i, j, k: (i, k)), pl.BlockSpec((tk, tn), lambda i, j, k: (k, j))]
    args = [a, b]
    if has_res:
        in_specs.append(pl.BlockSpec((tm, tn), lambda i, j, k: (i, j)))
        args.append(res)
    return pl.pallas_call(
        body, name=name, grid=(M // tm, N // tn, nk), in_specs=in_specs,
        out_specs=pl.BlockSpec((tm, tn), lambda i, j, k: (i, j)),
        out_shape=jax.ShapeDtypeStruct((M, N), out_dtype),
        scratch_shapes=[pltpu.VMEM((tm, tn), F32)] if nk > 1 else [],
        compiler_params=_params("parallel", "parallel", "arbitrary"),
    )(*args)


def _matmul_tn(x, y, *, name):
    S, A = x.shape
    _, B = y.shape
    ta, tb, ts = _tile(A, 512, LANES), _tile(B, 1536, LANES), _tile(S, 512, 8)
    ns = S // ts

    def body(x_ref, y_ref, o_ref):
        s = pl.program_id(2)
        part = lax.dot_general(x_ref[...], y_ref[...], (((0,), (0,)), ((), ())), preferred_element_type=F32)

        @pl.when(s == 0)
        def _():
            o_ref[...] = part

        @pl.when(s > 0)
        def _():
            o_ref[...] += part

    return pl.pallas_call(
        body, name=name, grid=(A // ta, B // tb, ns),
        in_specs=[pl.BlockSpec((ts, ta), lambda i, j, s: (s, i)), pl.BlockSpec((ts, tb), lambda i, j, s: (s, j))],
        out_specs=pl.BlockSpec((ta, tb), lambda i, j, s: (i, j)),
        out_shape=jax.ShapeDtypeStruct((A, B), F32),
        compiler_params=_params("parallel", "parallel", "arbitrary"),
    )(x, y)


def _rmsnorm_fwd(x, g, *, scale, out_dtype, name):
    R, W = x.shape
    tr = _tile(R, 512 if W >= 512 else 4096, 8)

    def body(x_ref, g_ref, o_ref):
        xv = x_ref[...]
        r = lax.rsqrt(jnp.mean(xv * xv, axis=1, keepdims=True) + EPS)
        o_ref[...] = (xv * r * (g_ref[...] * scale)).astype(out_dtype)

    return pl.pallas_call(
        body, name=name, grid=(R // tr,),
        in_specs=[pl.BlockSpec((tr, W), lambda i: (i, 0)), pl.BlockSpec((1, W), lambda i: (0, 0))],
        out_specs=pl.BlockSpec((tr, W), lambda i: (i, 0)),
        out_shape=jax.ShapeDtypeStruct((R, W), out_dtype),
        compiler_params=_params("parallel"),
    )(x, g)


def _rmsnorm_bwd(x, g, dy, *, scale, name, dres=None, out_dtype=F32):
    R, W = x.shape
    tr = _tile(R, 512 if W >= 512 else 4096, 8)
    has_res = dres is not None

    def body(*refs):
        if has_res:
            x_ref, g_ref, dy_ref, dres_ref, dx_ref, dg_ref = refs
        else:
            x_ref, g_ref, dy_ref, dx_ref, dg_ref = refs
        i = pl.program_id(0)
        xv = x_ref[...]
        dyv = dy_ref[...].astype(F32) * scale
        r = lax.rsqrt(jnp.mean(xv * xv, axis=1, keepdims=True) + EPS)
        u = dyv * g_ref[...]
        dx = r * u - xv * (r * r * r * jnp.mean(u * xv, axis=1, keepdims=True))
        if has_res:
            dx = dx + dres_ref[...]
        dx_ref[...] = dx.astype(out_dtype)
        part = jnp.sum(dyv * xv * r, axis=0, keepdims=True)

        @pl.when(i == 0)
        def _():
            dg_ref[...] = part

        @pl.when(i > 0)
        def _():
            dg_ref[...] += part

    row = pl.BlockSpec((tr, W), lambda i: (i, 0))
    one = pl.BlockSpec((1, W), lambda i: (0, 0))
    in_specs = [row, one, row] + ([row] if has_res else [])
    args = [x, g, dy] + ([dres] if has_res else [])
    return pl.pallas_call(
        body, name=name, grid=(R // tr,), in_specs=in_specs, out_specs=[row, one],
        out_shape=[jax.ShapeDtypeStruct((R, W), out_dtype), jax.ShapeDtypeStruct((1, W), F32)],
        compiler_params=_params("arbitrary"),
    )(*args)


def _shift_down(u, prev, n):
    ts = u.shape[0]
    out = pltpu.roll(u, n, 0)
    row = lax.broadcasted_iota(jnp.int32, u.shape, 0)
    for r in range(n):
        out = jnp.where(row == r, prev[8 - n + r:8 - n + r + 1, :], out)
    return out


def _shift_up(u, nxt, n):
    ts = u.shape[0]
    out = pltpu.roll(u, ts - n, 0)
    row = lax.broadcasted_iota(jnp.int32, u.shape, 0)
    for r in range(n):
        out = jnp.where(row == ts - n + r, nxt[r:r + 1, :], out)
    return out


def _conv_fwd(P, conv_w, conv_b, *, C, name):
    S = P.shape[0]
    ts = _tile(S, 512, 8)
    hb = ts // 8

    def body(ab_ref, ac_ref, ax_ref, pc_ref, px_ref, w_ref, b_ref, o_ref):
        i = pl.program_id(0)
        u = ac_ref[...] * ax_ref[...]
        prev = pc_ref[...] * px_ref[...] * (i > 0).astype(F32)
        w = w_ref[...]
        y = b_ref[...] + w[0:1, :] * _shift_down(u, prev, 2) + w[1:2, :] * _shift_down(u, prev, 1) + w[2:3, :] * u
        o_ref[...] = (ab_ref[...] * y).astype(BF16)

    cur = lambda c: pl.BlockSpec((ts, C), lambda i: (i, c))
    prv = lambda c: pl.BlockSpec((8, C), lambda i: (jnp.maximum(i * hb - 1, 0), c))
    return pl.pallas_call(
        body, name=name, grid=(S // ts,),
        in_specs=[cur(0), cur(1), cur(2), prv(1), prv(2),
                  pl.BlockSpec((8, C), lambda i: (0, 0)), pl.BlockSpec((1, C), lambda i: (0, 0))],
        out_specs=pl.BlockSpec((ts, C), lambda i: (i, 0)),
        out_shape=jax.ShapeDtypeStruct((S, C), BF16),
        compiler_params=_params("parallel"),
    )(P, P, P, P, P, conv_w, conv_b)


def _conv_bwd(P, dya, conv_w, conv_b, *, C, name):
    S = P.shape[0]
    ts = _tile(S, 512, 8)
    hb = ts // 8
    last = S // 8 - 1
    n = S // ts

    def body(ab_ref, ac_ref, ax_ref, pc_ref, px_ref, dy_ref, nab_ref, ndy_ref, w_ref, b_ref, o_ref, dw_ref):
        i = pl.program_id(0)
        ab, ac, ax = ab_ref[...], ac_ref[...], ax_ref[...]
        u = ac * ax
        prev = pc_ref[...] * px_ref[...] * (i > 0).astype(F32)
        w = w_ref[...]
        u1, u2 = _shift_down(u, prev, 1), _shift_down(u, prev, 2)
        y = b_ref[...] + w[0:1, :] * u2 + w[1:2, :] * u1 + w[2:3, :] * u
        dya_v = dy_ref[...]
        dyp = dya_v * ab
        nxt = ndy_ref[...] * nab_ref[...] * (i < n - 1).astype(F32)
        du = w[2:3, :] * dyp + w[1:2, :] * _shift_up(dyp, nxt, 1) + w[0:1, :] * _shift_up(dyp, nxt, 2)
        o_ref[:, 0:C] = (dya_v * y).astype(BF16)
        o_ref[:, C:2 * C] = (du * ax).astype(BF16)
        o_ref[:, 2 * C:3 * C] = (du * ac).astype(BF16)
        part = jnp.concatenate([
            jnp.sum(dyp * u2, axis=0, keepdims=True), jnp.sum(dyp * u1, axis=0, keepdims=True),
            jnp.sum(dyp * u, axis=0, keepdims=True), jnp.sum(dyp, axis=0, keepdims=True),
            jnp.zeros((4, C), F32)], axis=0)

        @pl.when(i == 0)
        def _():
            dw_ref[...] = part

        @pl.when(i > 0)
        def _():
            dw_ref[...] += part

    cur = lambda c: pl.BlockSpec((ts, C), lambda i: (i, c))
    prv = lambda c: pl.BlockSpec((8, C), lambda i: (jnp.maximum(i * hb - 1, 0), c))
    nxt = lambda c: pl.BlockSpec((8, C), lambda i: (jnp.minimum((i + 1) * hb, last), c))
    return pl.pallas_call(
        body, name=name, grid=(n,),
        in_specs=[cur(0), cur(1), cur(2), prv(1), prv(2), cur(0), nxt(0), nxt(0),
                  pl.BlockSpec((8, C), lambda i: (0, 0)), pl.BlockSpec((1, C), lambda i: (0, 0))],
        out_specs=[pl.BlockSpec((ts, 3 * C), lambda i: (i, 0)), pl.BlockSpec((8, C), lambda i: (0, 0))],
        out_shape=[jax.ShapeDtypeStruct((S, 3 * C), BF16), jax.ShapeDtypeStruct((8, C), F32)],
        compiler_params=_params("arbitrary"),
    )(P, P, P, P, P, dya, P, dya, conv_w, conv_b)


def _sgu_fwd(P, ln_g, ln_b, wm, bT, *, W, cu, cv, name):
    S = P.shape[0]
    G, CH, _ = wm.shape
    gw = W // G
    ts = _tile(S, 512, CH)

    def body(u_ref, v_ref, g_ref, b_ref, wm_ref, bT_ref, o_ref):
        gv = _gelu(v_ref[...])
        mu = jnp.mean(gv, axis=1, keepdims=True)
        xc = gv - mu
        vn = (xc * lax.rsqrt(jnp.mean(xc * xc, axis=1, keepdims=True) + EPS) * g_ref[...] + b_ref[...]).astype(BF16)
        bT_v = bT_ref[...]
        for c in range(ts // CH):
            rows = slice(c * CH, (c + 1) * CH)
            for g in range(G):
                cols = slice(g * gw, (g + 1) * gw)
                mixed = jnp.dot(wm_ref[g], vn[rows, cols], preferred_element_type=F32) + bT_v[:, g:g + 1]
                o_ref[rows, cols] = (_gelu(u_ref[rows, cols]) * mixed).astype(BF16)

    full = lambda shp: pl.BlockSpec(shp, lambda i: (0,) * len(shp))
    return pl.pallas_call(
        body, name=name, grid=(S // ts,),
        in_specs=[pl.BlockSpec((ts, W), lambda i: (i, cu)), pl.BlockSpec((ts, W), lambda i: (i, cv)),
                  full((1, W)), full((1, W)), full((G, CH, CH)), full((CH, G))],
        out_specs=pl.BlockSpec((ts, W), lambda i: (i, 0)),
        out_shape=jax.ShapeDtypeStruct((S, W), BF16),
        compiler_params=_params("parallel"),
    )(P, P, ln_g, ln_b, wm, bT)


def _sgu_bwd(P, dyb, ln_g, ln_b, wm, wmT, bT, *, W, cu, cv, name):
    S = P.shape[0]
    G, CH, _ = wm.shape
    gw = W // G
    ts = _tile(S, 512, CH)

    def body(u_ref, v_ref, dy_ref, g_ref, b_ref, wm_ref, wmT_ref, bT_ref, o_ref, dw_ref, db_ref, dln_ref, dvn_ref):
        i = pl.program_id(0)

        @pl.when(i == 0)
        def _():
            dw_ref[...] = jnp.zeros_like(dw_ref)
            db_ref[...] = jnp.zeros_like(db_ref)
            dln_ref[...] = jnp.zeros_like(dln_ref)

        sv = v_ref[...]
        gv = _gelu(sv)
        mu = jnp.mean(gv, axis=1, keepdims=True)
        xc = gv - mu
        rstd = lax.rsqrt(jnp.mean(xc * xc, axis=1, keepdims=True) + EPS)
        xhat = xc * rstd
        lg = g_ref[...]
        vn = (xhat * lg + b_ref[...]).astype(BF16)
        bT_v = bT_ref[...]
        for c in range(ts // CH):
            rows = slice(c * CH, (c + 1) * CH)
            for g in range(G):
                cols = slice(g * gw, (g + 1) * gw)
                vn_cg = vn[rows, cols]
                mixed = jnp.dot(wm_ref[g], vn_cg, preferred_element_type=F32) + bT_v[:, g:g + 1]
                su = u_ref[rows, cols]
                dyv = dy_ref[rows, cols]
                dmix = dyv * _gelu(su)
                o_ref[rows, cols] = (dyv * mixed * _gelu_grad(su)).astype(BF16)
                dmix_b = dmix.astype(BF16)
                dw_ref[g] += lax.dot_general(dmix_b, vn_cg, (((1,), (1,)), ((), ())), preferred_element_type=F32)
                db_ref[g] += jnp.broadcast_to(jnp.sum(dmix, axis=1, keepdims=True), (CH, CH))
                dvn_ref[rows, cols] = jnp.dot(wmT_ref[g], dmix_b, preferred_element_type=F32)
        dvn = dvn_ref[...]
        dxh = dvn * lg
        dgv = rstd * (dxh - jnp.mean(dxh, axis=1, keepdims=True) - xhat * jnp.mean(dxh * xhat, axis=1, keepdims=True))
        o_ref[:, W:2 * W] = (dgv * _gelu_grad(sv)).astype(BF16)
        dln_ref[0:1, :] += jnp.sum(dvn * xhat, axis=0, keepdims=True)
        dln_ref[1:2, :] += jnp.sum(dvn, axis=0, keepdims=True)

    full = lambda shp: pl.BlockSpec(shp, lambda i: (0,) * len(shp))
    return pl.pallas_call(
        body, name=name, grid=(S // ts,),
        in_specs=[pl.BlockSpec((ts, W), lambda i: (i, cu)), pl.BlockSpec((ts, W), lambda i: (i, cv)),
                  pl.BlockSpec((ts, W), lambda i: (i, 0)),
                  full((1, W)), full((1, W)), full((G, CH, CH)), full((G, CH, CH)), full((CH, G))],
        out_specs=[pl.BlockSpec((ts, 2 * W), lambda i: (i, 0)), full((G, CH, CH)), full((G, CH, CH)), full((8, W))],
        out_shape=[jax.ShapeDtypeStruct((S, 2 * W), BF16), jax.ShapeDtypeStruct((G, CH, CH), F32),
                   jax.ShapeDtypeStruct((G, CH, CH), F32), jax.ShapeDtypeStruct((8, W), F32)],
        scratch_shapes=[pltpu.VMEM((ts, W), F32)],
        compiler_params=_params("arbitrary"),
    )(P, P, dyb, ln_g, ln_b, wm, wmT, bT)


def _split_dot(x, u):
    hi = x.astype(BF16)
    lo = (x - hi.astype(F32)).astype(BF16)
    return jnp.dot(hi, u, preferred_element_type=F32) + jnp.dot(lo, u, preferred_element_type=F32)


_NT = (((1,), (1,)), ((), ()))
_TN = (((0,), (0,)), ((), ()))


def _attn_fwd(q, k, v, umat, *, tq, tk, name):
    H, S, hd = q.shape

    def body(q_ref, k_ref, v_ref, u_ref, o_ref):
        i = pl.program_id(1)
        qb = q_ref[0]
        um = u_ref[...]
        qpos = lax.broadcasted_iota(jnp.int32, (tq, tk), 0) + i * tq
        kloc = lax.broadcasted_iota(jnp.int32, (tq, tk), 1)

        def step(j, carry, masked):
            r, acc = carry
            ks = pl.multiple_of(j * tk, tk)
            kb = k_ref[0, pl.ds(ks, tk), :]
            vb = v_ref[0, pl.ds(ks, tk), :]
            z = lax.dot_general(qb, kb, _NT, preferred_element_type=F32)
            sp = jnp.log(1.0 + jnp.exp(-jnp.abs(z)))
            lb = jnp.minimum(z, 0.0) - sp
            lm = jnp.minimum(-z, 0.0) - sp
            if masked:
                m = (kloc + j * tk) < qpos
                lm = jnp.where(m, lm, 0.0)
            a = jnp.exp(lb + _split_dot(lm, um) + r)
            if masked:
                a = jnp.where(m, a, 0.0)
            acc = acc + _split_dot(a, vb)
            return r + jnp.sum(lm, axis=1, keepdims=True), acc

        jd = (i * tq) // tk
        carry = step(jd, (jnp.zeros((tq, 1), F32), jnp.zeros((tq, hd), F32)), True)
        carry = lax.fori_loop(0, jd, lambda t, c: step(jd - 1 - t, c, False), carry)
        o_ref[0] = carry[1]

    blk = pl.BlockSpec((1, tq, hd), lambda h, i: (h, i, 0))
    whole = pl.BlockSpec((1, S, hd), lambda h, i: (h, 0, 0))
    return pl.pallas_call(
        body, name=name, grid=(H, S // tq),
        in_specs=[blk, whole, whole, pl.BlockSpec((tk, tk), lambda h, i: (0, 0))],
        out_specs=blk, out_shape=jax.ShapeDtypeStruct((H, S, hd), F32),
        compiler_params=_params("parallel", "arbitrary"),
    )(q, k, v, umat)


def _attn_bwd(q, k, v, o, do, umat, *, tq, tk, name):
    H, S, hd = q.shape

    def body(q_ref, k_ref, v_ref, o_ref, do_ref, u_ref, dq_ref, dk_ref, dv_ref):
        i = pl.program_id(1)

        @pl.when(i == 0)
        def _():
            dk_ref[...] = jnp.zeros_like(dk_ref)
            dv_ref[...] = jnp.zeros_like(dv_ref)

        qb = q_ref[0]
        do32 = do_ref[0]
        dob = do32.astype(BF16)
        tot = jnp.sum(dob.astype(F32) * o_ref[0], axis=1, keepdims=True)
        um = u_ref[...]
        qpos = lax.broadcasted_iota(jnp.int32, (tq, tk), 0) + i * tq
        kloc = lax.broadcasted_iota(jnp.int32, (tq, tk), 1)

        def step(j, carry, masked):
            r, gs, dq = carry
            ks = pl.multiple_of(j * tk, tk)
            kb = k_ref[0, pl.ds(ks, tk), :]
            vb = v_ref[0, pl.ds(ks, tk), :]
            z = lax.dot_general(qb, kb, _NT, preferred_element_type=F32)
            e = jnp.exp(-jnp.abs(z))
            d = 1.0 + e
            sp = jnp.log(d)
            inv = 1.0 / d
            einv = e * inv
            pos = z >= 0.0
            sig = jnp.where(pos, inv, einv)
            nsig = jnp.where(pos, einv, inv)
            lb = jnp.minimum(z, 0.0) - sp
            lm = jnp.minimum(-z, 0.0) - sp
            if masked:
                m = (kloc + j * tk) < qpos
                lm = jnp.where(m, lm, 0.0)
            a = jnp.exp(lb + _split_dot(lm, um) + r)
            if masked:
                a = jnp.where(m, a, 0.0)
            g = lax.dot_general(dob, vb, _NT, preferred_element_type=F32) * a
            c = tot - (_split_dot(g, um) + g + gs)
            dz = g * nsig - c * sig
            if masked:
                dz = jnp.where(m, dz, 0.0)
            dzb = dz.astype(BF16)
            dq = dq + jnp.dot(dzb, kb, preferred_element_type=F32)
            dk_ref[0, pl.ds(ks, tk), :] += lax.dot_general(dzb, qb, _TN, preferred_element_type=F32)
            dv_ref[0, pl.ds(ks, tk), :] += lax.dot_general(a.astype(BF16), dob, _TN, preferred_element_type=F32)
            return r + jnp.sum(lm, axis=1, keepdims=True), gs + jnp.sum(g, axis=1, keepdims=True), dq

        jd = (i * tq) // tk
        zero = jnp.zeros((tq, 1), F32)
        carry = step(jd, (zero, zero, jnp.zeros((tq, hd), F32)), True)
        carry = lax.fori_loop(0, jd, lambda t, c: step(jd - 1 - t, c, False), carry)
        dq_ref[0] = carry[2]

    blk = pl.BlockSpec((1, tq, hd), lambda h, i: (h, i, 0))
    whole = pl.BlockSpec((1, S, hd), lambda h, i: (h, 0, 0))
    shp = jax.ShapeDtypeStruct((H, S, hd), F32)
    return pl.pallas_call(
        body, name=name, grid=(H, S // tq),
        in_specs=[blk, whole, whole, blk, blk, pl.BlockSpec((tk, tk), lambda h, i: (0, 0))],
        out_specs=[blk, whole, whole], out_shape=[shp, shp, shp],
        compiler_params=_params("parallel", "arbitrary"),
    )(q, k, v, o, do, umat)


def _merge_fwd(ys, wb, P, b_gate, *, gate_col0, name):
    S, C = ys[0].shape
    D = wb.shape[2]
    ts = _tile(S, 512, 8)

    def body(y0, y1, y2, wb_ref, g0, g1, g2, bg_ref, o_ref):
        acc = jnp.zeros((ts, D), F32)
        for n, (y_ref, g_ref) in enumerate(((y0, g0), (y1, g1), (y2, g2))):
            yd = jnp.dot(y_ref[...], wb_ref[n], preferred_element_type=F32)
            acc = acc + _sigmoid(g_ref[...] + bg_ref[:, n * D:(n + 1) * D]) * yd
        o_ref[...] = acc.astype(BF16)

    ysp = pl.BlockSpec((ts, C), lambda i: (i, 0))
    gsp = lambda n: pl.BlockSpec((ts, D), lambda i: (i, gate_col0 + n))
    return pl.pallas_call(
        body, name=name, grid=(S // ts,),
        in_specs=[ysp, ysp, ysp, pl.BlockSpec((3, C, D), lambda i: (0, 0, 0)), gsp(0), gsp(1), gsp(2),
                  pl.BlockSpec((1, 3 * D), lambda i: (0, 0))],
        out_specs=pl.BlockSpec((ts, D), lambda i: (i, 0)),
        out_shape=jax.ShapeDtypeStruct((S, D), BF16),
        compiler_params=_params("parallel"),
    )(*ys, wb, P, P, P, b_gate)


def _merge_bwd(ys, wb, wbT, P, b_gate, dmerged, *, gate_col0, name):
    S, C = ys[0].shape
    D = wb.shape[2]
    ts = _tile(S, 256, 8)

    def body(y0, y1, y2, wb_ref, wbT_ref, g0, g1, g2, bg_ref, dm_ref, dg_ref, dyd_ref, dy_ref, dbg_ref):
        i = pl.program_id(0)
        dm = dm_ref[...]
        parts = []
        for n, (y_ref, g_ref) in enumerate(((y0, g0), (y1, g1), (y2, g2))):
            yd = jnp.dot(y_ref[...], wb_ref[n], preferred_element_type=F32)
            sg = _sigmoid(g_ref[...] + bg_ref[:, n * D:(n + 1) * D])
            dgate = dm * yd * sg * (1.0 - sg)
            dg_ref[:, n * D:(n + 1) * D] = dgate.astype(BF16)
            parts.append(jnp.sum(dgate, axis=0, keepdims=True))
            dyd = (dm * sg).astype(BF16)
            dyd_ref[n] = dyd
            dy_ref[n] = jnp.dot(dyd, wbT_ref[n], preferred_element_type=F32)
        part = jnp.concatenate(parts, axis=1)

        @pl.when(i == 0)
        def _():
            dbg_ref[...] = part

        @pl.when(i > 0)
        def _():
            dbg_ref[...] += part

    ysp = pl.BlockSpec((ts, C), lambda i: (i, 0))
    gsp = lambda n: pl.BlockSpec((ts, D), lambda i: (i, gate_col0 + n))
    return pl.pallas_call(
        body, name=name, grid=(S // ts,),
        in_specs=[ysp, ysp, ysp, pl.BlockSpec((3, C, D), lambda i: (0, 0, 0)),
                  pl.BlockSpec((3, D, C), lambda i: (0, 0, 0)), gsp(0), gsp(1), gsp(2),
                  pl.BlockSpec((1, 3 * D), lambda i: (0, 0)), pl.BlockSpec((ts, D), lambda i: (i, 0))],
        out_specs=[pl.BlockSpec((ts, 3 * D), lambda i: (i, 0)), pl.BlockSpec((3, ts, D), lambda i: (0, i, 0)),
                   pl.BlockSpec((3, ts, C), lambda i: (0, i, 0)), pl.BlockSpec((1, 3 * D), lambda i: (0, 0))],
        out_shape=[jax.ShapeDtypeStruct((S, 3 * D), BF16), jax.ShapeDtypeStruct((3, S, D), BF16),
                   jax.ShapeDtypeStruct((3, S, C), F32), jax.ShapeDtypeStruct((1, 3 * D), F32)],
        compiler_params=_params("arbitrary"),
    )(*ys, wb, wbT, P, P, P, b_gate, dmerged)


def _swiglu_fwd(gu, *, name):
    S, F2 = gu.shape
    F = F2 // 2
    ts, tf = _tile(S, 512, 8), _tile(F, 1536, LANES)
    nf = F // tf

    def body(g_ref, u_ref, o_ref):
        gt = g_ref[...]
        o_ref[...] = (gt * _sigmoid(gt) * u_ref[...]).astype(BF16)

    return pl.pallas_call(
        body, name=name, grid=(S // ts, nf),
        in_specs=[pl.BlockSpec((ts, tf), lambda i, j: (i, j)), pl.BlockSpec((ts, tf), lambda i, j: (i, j + nf))],
        out_specs=pl.BlockSpec((ts, tf), lambda i, j: (i, j)),
        out_shape=jax.ShapeDtypeStruct((S, F), BF16),
        compiler_params=_params("parallel", "parallel"),
    )(gu, gu)


def _swiglu_bwd(gu, dact, *, name):
    S, F2 = gu.shape
    F = F2 // 2
    ts, tf = _tile(S, 512, 8), _tile(F, 1536, LANES)
    nf = F // tf

    def body(g_ref, u_ref, d_ref, o_ref):
        j = pl.program_id(1)
        gt, da = g_ref[...], d_ref[...]
        sg = _sigmoid(gt)

        @pl.when(j < nf)
        def _():
            o_ref[...] = (da * u_ref[...] * sg * (1.0 + gt * (1.0 - sg))).astype(BF16)

        @pl.when(j >= nf)
        def _():
            o_ref[...] = (da * gt * sg).astype(BF16)

    return pl.pallas_call(
        body, name=name, grid=(S // ts, 2 * nf),
        in_specs=[pl.BlockSpec((ts, tf), lambda i, j: (i, j % nf)),
                  pl.BlockSpec((ts, tf), lambda i, j: (i, j % nf + nf)),
                  pl.BlockSpec((ts, tf), lambda i, j: (i, j % nf))],
        out_specs=pl.BlockSpec((ts, tf), lambda i, j: (i, j)),
        out_shape=jax.ShapeDtypeStruct((S, F2), BF16),
        compiler_params=_params("parallel", "parallel"),
    )(gu, gu, dact)


def _loss_grad(y, target, *, name):
    S, D = y.shape
    ts = _tile(S, 512, 8)

    def body(y_ref, t_ref, dy_ref, l_ref):
        i = pl.program_id(0)
        err = y_ref[...] - t_ref[...]
        dy_ref[...] = err * (1.0 / D)
        part = jnp.broadcast_to(jnp.sum(jnp.sum(err * err, axis=1, keepdims=True), axis=0, keepdims=True) * (0.5 / D),
                                (1, LANES))

        @pl.when(i == 0)
        def _():
            l_ref[...] = part

        @pl.when(i > 0)
        def _():
            l_ref[...] += part

    row = pl.BlockSpec((ts, D), lambda i: (i, 0))
    return pl.pallas_call(
        body, name=name, grid=(S // ts,), in_specs=[row, row],
        out_specs=[row, pl.BlockSpec((1, LANES), lambda i: (0, 0))],
        out_shape=[jax.ShapeDtypeStruct((S, D), F32), jax.ShapeDtypeStruct((1, LANES), F32)],
        compiler_params=_params("arbitrary"),
    )(y, target)


def _adamw(w, m, v, gs, *, name):
    R = w.shape[0]
    ns = gs.shape[0]
    tr = _tile(R, 2048, 16)
    c1 = 1.0 / (1.0 - ADAM_B1 ** ADAM_STEP)
    c2 = 1.0 / (1.0 - ADAM_B2 ** ADAM_STEP)

    def body(w_ref, m_ref, v_ref, gs_ref, g_ref, d_ref, nm_ref, nv_ref):
        g = gs_ref[0].astype(F32)
        for s in range(1, ns):
            g = g + gs_ref[s].astype(F32)
        nm = ADAM_B1 * m_ref[...] + (1.0 - ADAM_B1) * g
        nv = ADAM_B2 * v_ref[...] + (1.0 - ADAM_B2) * (g * g)
        g_ref[...] = g
        nm_ref[...] = nm
        nv_ref[...] = nv
        d_ref[...] = -ADAM_LR * ((nm * c1) / (jnp.sqrt(nv * c2) + ADAM_EPS) + ADAM_WD * w_ref[...])

    row = pl.BlockSpec((tr, LANES), lambda i: (i, 0))
    shp = jax.ShapeDtypeStruct((R, LANES), F32)
    return pl.pallas_call(
        body, name=name, grid=(R // tr,),
        in_specs=[row, row, row, pl.BlockSpec((ns, tr, LANES), lambda i: (0, i, 0))],
        out_specs=[row, row, row, row], out_shape=[shp, shp, shp, shp],
        compiler_params=_params("parallel"),
    )(w, m, v, gs)


def _slot_sum(gs, *, name):
    ns, R, _ = gs.shape
    tr = _tile(R, 2048, 16)

    def body(gs_ref, o_ref):
        g = gs_ref[0]
        for s in range(1, ns):
            g = g + gs_ref[s]
        o_ref[...] = g

    return pl.pallas_call(
        body, name=name, grid=(R // tr,),
        in_specs=[pl.BlockSpec((ns, tr, LANES), lambda i: (0, i, 0))],
        out_specs=pl.BlockSpec((tr, LANES), lambda i: (i, 0)),
        out_shape=jax.ShapeDtypeStruct((R, LANES), F32),
        compiler_params=_params("parallel"),
    )(gs)


def _all_gather(xs, *, name):
    R, L = xs.shape

    def body(x_ref, out_ref, send_sems, recv_sems, local_sem):
        x, y, c = lax.axis_index("x"), lax.axis_index("y"), lax.axis_index("c")
        me, sibling = (x, y, c), (x, y, 1 - c)
        chips = [(1 - x, y), (x, 1 - y), (1 - x, 1 - y)]

        def slot(px, py, pc):
            return out_ref.at[4 * px + 2 * py + pc]

        def copy(k, block, to, src=None):
            return pltpu.make_async_remote_copy(
                src_ref=slot(*block) if src is None else src, dst_ref=slot(*block),
                send_sem=send_sems.at[k], recv_sem=recv_sems.at[k], device_id=to, device_id_type=MESH)

        mine = pltpu.make_async_copy(x_ref, slot(*me), local_sem)
        mine.start()
        first = [copy(0, me, sibling, src=x_ref)]
        first += [copy(1 + j, me, (*chip, c), src=x_ref) for j, chip in enumerate(chips)]
        for cp in first:
            cp.start()
        passed = [copy(4 + j, (*chip, c), sibling) for j, chip in enumerate(chips)]
        for j, chip in enumerate(chips):
            copy(1 + j, (*chip, c), me).wait_recv()
            passed[j].start()
        copy(0, sibling, me).wait_recv()
        for j, chip in enumerate(chips):
            copy(4 + j, (*chip, 1 - c), me).wait_recv()
        for cp in first + passed:
            cp.wait_send()
        mine.wait()

    return pl.pallas_call(
        body, name=name,
        in_specs=[pl.BlockSpec(memory_space=pl.ANY)], out_specs=pl.BlockSpec(memory_space=pl.ANY),
        out_shape=jax.ShapeDtypeStruct((N_DEV, R, L), xs.dtype),
        scratch_shapes=[pltpu.SemaphoreType.DMA((7,)), pltpu.SemaphoreType.DMA((7,)), pltpu.SemaphoreType.DMA],
        compiler_params=pltpu.CompilerParams(has_side_effects=True),
    )(xs)


def _all_to_all(xs, *, name):
    _, R, L = xs.shape

    def body(x_ref, out_ref, send_sems, recv_sems, local_sem):
        x, y, c = lax.axis_index("x"), lax.axis_index("y"), lax.axis_index("c")
        me = 4 * x + 2 * y + c
        mine = pltpu.make_async_copy(x_ref.at[me], out_ref.at[me], local_sem)
        mine.start()
        copies = []
        for k in range(1, N_DEV):
            px = 1 - x if k & 4 else x
            py = 1 - y if k & 2 else y
            pc = 1 - c if k & 1 else c
            peer = 4 * px + 2 * py + pc
            copies.append((
                pltpu.make_async_remote_copy(
                    src_ref=x_ref.at[peer], dst_ref=out_ref.at[me], send_sem=send_sems.at[k - 1],
                    recv_sem=recv_sems.at[k - 1], device_id=(px, py, pc), device_id_type=MESH),
                pltpu.make_async_remote_copy(
                    src_ref=x_ref.at[me], dst_ref=out_ref.at[peer], send_sem=send_sems.at[k - 1],
                    recv_sem=recv_sems.at[k - 1], device_id=(px, py, pc), device_id_type=MESH)))
        for send, _ in copies:
            send.start()
        for _, recv in copies:
            recv.wait_recv()
        for send, _ in copies:
            send.wait_send()
        mine.wait()

    return pl.pallas_call(
        body, name=name,
        in_specs=[pl.BlockSpec(memory_space=pl.ANY)], out_specs=pl.BlockSpec(memory_space=pl.ANY),
        out_shape=jax.ShapeDtypeStruct(xs.shape, xs.dtype),
        scratch_shapes=[pltpu.SemaphoreType.DMA((7,)), pltpu.SemaphoreType.DMA((7,)), pltpu.SemaphoreType.DMA],
        compiler_params=pltpu.CompilerParams(has_side_effects=True),
    )(xs)


def _pack_rows(parts, dtype):
    flat = jnp.concatenate([p.reshape(-1).astype(dtype) for p in parts])
    pad = (-flat.shape[0]) % (8 * LANES)
    if pad:
        flat = jnp.concatenate([flat, jnp.zeros((pad,), dtype)])
    return flat.reshape(-1, LANES)


def _unpack_rows(packed, shapes):
    flat = packed.reshape(-1)
    out, off = [], 0
    for shp in shapes:
        n = math.prod(shp)
        out.append(flat[off:off + n].reshape(shp))
        off += n
    return out


def _unshard(gathered, axis):
    g = jnp.moveaxis(gathered, 0, axis)
    shp = list(g.shape)
    shp[axis:axis + 2] = [shp[axis] * shp[axis + 1]]
    return g.reshape(shp)


def _reshard(full, axis):
    shp = list(full.shape)
    shp[axis:axis + 1] = [N_DEV, shp[axis] // N_DEV]
    return jnp.moveaxis(full.reshape(shp), axis, 0)


def _heads(a, H):
    S = a.shape[0]
    return a.reshape(S, H, -1).transpose(1, 0, 2)


def _unheads(a):
    H, S, hd = a.shape
    return a.transpose(1, 0, 2).reshape(S, H * hd)


def _local_step(x, target, wf, sm, *, tq, tk):
    S, D = x.shape
    L = wf["w_in"].shape[0]
    C = sm["conv_b"].shape[1]
    W = sm["sgu_ln_g"].shape[1]
    hd = sm["q_norm_g"].shape[1]
    G, CH = sm["sgu_w"].shape[1], sm["sgu_w"].shape[2]
    F = wf["w_down"].shape[1]
    A = wf["w_in"].shape[2] - (3 * C + 2 * W + 3 * D)
    A = A // 3
    H = A // hd
    col_q = 3 * C + 2 * W
    qscale = 1.0 / math.sqrt(hd)
    tril = jnp.tril(jnp.ones((CH, CH), F32))
    umat = (lax.broadcasted_iota(jnp.int32, (tk, tk), 0) > lax.broadcasted_iota(jnp.int32, (tk, tk), 1)).astype(BF16)

    saved = []
    for l in range(L):
        n = f"l{l}_"
        g1 = sm["mix_norm_g"][l][None]
        h = _rmsnorm_fwd(x, g1, scale=1.0, out_dtype=BF16, name=n + "mixnorm")
        P = _matmul(h, wf["w_in"][l], name=n + "w_in")
        cw = jnp.concatenate([sm["conv_w"][l], jnp.zeros((5, C), F32)], axis=0)
        cb = sm["conv_b"][l][None]
        ya = _conv_fwd(P, cw, cb, C=C, name=n + "conv")
        wm = (sm["sgu_w"][l] * tril).astype(BF16)
        bT = sm["sgu_b"][l].T
        lng, lnb = sm["sgu_ln_g"][l][None], sm["sgu_ln_b"][l][None]
        yb = _sgu_fwd(P, lng, lnb, wm, bT, W=W, cu=(3 * C) // W, cv=(3 * C) // W + 1, name=n + "sgu")
        qkv = P[:, col_q:col_q + 3 * A].reshape(S, 3, H, hd).transpose(1, 2, 0, 3)
        q_raw, k_raw = qkv[0].reshape(H * S, hd), qkv[1].reshape(H * S, hd)
        gq, gk = sm["q_norm_g"][l][None], sm["k_norm_g"][l][None]
        qn = _rmsnorm_fwd(q_raw, gq, scale=qscale, out_dtype=BF16, name=n + "qnorm").reshape(H, S, hd)
        kn = _rmsnorm_fwd(k_raw, gk, scale=1.0, out_dtype=BF16, name=n + "knorm").reshape(H, S, hd)
        vh = qkv[2].astype(BF16)
        o = _attn_fwd(qn, kn, vh, umat, tq=tq, tk=tk, name=n + "attn")
        yc = _unheads(o).astype(BF16)
        bg = sm["b_gate"][l][None]
        gate_col0 = (col_q + 3 * A) // D
        merged = _merge_fwd((ya, yb, yc), wf["w_branch_out"][l], P, bg, gate_col0=gate_col0, name=n + "merge")
        x1 = _matmul(merged, wf["w_o"][l], res=x, name=n + "w_o")
        g2 = sm["ffn_norm_g"][l][None]
        h2 = _rmsnorm_fwd(x1, g2, scale=1.0, out_dtype=BF16, name=n + "ffnnorm")
        gu = _matmul(h2, wf["w_gate_up"][l], name=n + "w_gate_up")
        act = _swiglu_fwd(gu, name=n + "swiglu")
        x2 = _matmul(act, wf["w_down"][l], res=x1, name=n + "w_down")
        saved.append(dict(x=x, h=h, P=P, cw=cw, cb=cb, ya=ya, wm=wm, bT=bT, lng=lng, lnb=lnb, yb=yb,
                          q_raw=q_raw, k_raw=k_raw, gq=gq, gk=gk, qn=qn, kn=kn, vh=vh, o=o, yc=yc, bg=bg,
                          gate_col0=gate_col0, merged=merged, x1=x1, g1=g1, g2=g2, h2=h2, gu=gu, act=act))
        x = x2

    dx, lpart = _loss_grad(x, target, name="loss")
    grads = {k: [None] * L for k in WEIGHTS}
    for l in reversed(range(L)):
        n = f"l{l}_b_"
        sv = saved[l]
        dxb = dx.astype(BF16)
        grads["w_down"][l] = _matmul_tn(sv["act"], dxb, name=n + "g_w_down")
        dact = _matmul(dxb, wf["w_down"][l].T, name=n + "d_act")
        dgu = _swiglu_bwd(sv["gu"], dact, name=n + "swiglu")
        grads["w_gate_up"][l] = _matmul_tn(sv["h2"], dgu, name=n + "g_w_gate_up")
        dh2 = _matmul(dgu, wf["w_gate_up"][l].T, name=n + "d_h2")
        dx1, dg2 = _rmsnorm_bwd(sv["x1"], sv["g2"], dh2, scale=1.0, dres=dx, name=n + "ffnnorm")
        grads["ffn_norm_g"][l] = dg2[0]
        dx1b = dx1.astype(BF16)
        grads["w_o"][l] = _matmul_tn(sv["merged"], dx1b, name=n + "g_w_o")
        dmerged = _matmul(dx1b, wf["w_o"][l].T, name=n + "d_merged")
        ys = (sv["ya"], sv["yb"], sv["yc"])
        wb = wf["w_branch_out"][l]
        dgates, dyd, dy, dbg = _merge_bwd(ys, wb, wb.transpose(0, 2, 1), sv["P"], sv["bg"], dmerged,
                                          gate_col0=sv["gate_col0"], name=n + "merge")
        grads["b_gate"][l] = dbg[0]
        grads["w_branch_out"][l] = jnp.stack(
            [_matmul_tn(ys[i], dyd[i], name=n + f"g_w_branch{i}") for i in range(3)])
        dconv, dcw = _conv_bwd(sv["P"], dy[0], sv["cw"], sv["cb"], C=C, name=n + "conv")
        grads["conv_w"][l], grads["conv_b"][l] = dcw[0:3], dcw[3]
        wmT = sv["wm"].transpose(0, 2, 1)
        dsgu, dsw, dsb, dln = _sgu_bwd(sv["P"], dy[1], sv["lng"], sv["lnb"], sv["wm"], wmT, sv["bT"], W=W,
                                       cu=(3 * C) // W, cv=(3 * C) // W + 1, name=n + "sgu")
        grads["sgu_w"][l], grads["sgu_b"][l] = dsw * tril, dsb[:, :, 0]
        grads["sgu_ln_g"][l], grads["sgu_ln_b"][l] = dln[0], dln[1]
        do = _heads(dy[2], H)
        dqn, dkn, dvh = _attn_bwd(sv["qn"], sv["kn"], sv["vh"], sv["o"], do, umat, tq=tq, tk=tk, name=n + "attn")
        dq_raw, dgq = _rmsnorm_bwd(sv["q_raw"], sv["gq"], dqn.reshape(H * S, hd), scale=qscale, name=n + "qnorm",
                                   out_dtype=BF16)
        dk_raw, dgk = _rmsnorm_bwd(sv["k_raw"], sv["gk"], dkn.reshape(H * S, hd), scale=1.0, name=n + "knorm",
                                   out_dtype=BF16)
        grads["q_norm_g"][l], grads["k_norm_g"][l] = dgq[0], dgk[0]
        dP = jnp.concatenate([dconv, dsgu, _unheads(dq_raw.reshape(H, S, hd)), _unheads(dk_raw.reshape(H, S, hd)),
                              _unheads(dvh).astype(BF16), dgates], axis=1)
        grads["w_in"][l] = _matmul_tn(sv["h"], dP, name=n + "g_w_in")
        dh = _matmul(dP, wf["w_in"][l].T, name=n + "d_h")
        dx, dg1 = _rmsnorm_bwd(sv["x"], sv["g1"], dh, scale=1.0, dres=dx1, name=n + "mixnorm")
        grads["mix_norm_g"][l] = dg1[0]
    return lpart[0, 0], dx, {k: jnp.stack(v) for k, v in grads.items()}


def kernel(x, mix_norm_g, w_in, b_gate, conv_w, conv_b, sgu_ln_g, sgu_ln_b, sgu_w, sgu_b, q_norm_g, k_norm_g, w_branch_out, w_o, ffn_norm_g, w_gate_up, w_down, loss_target, m_mix_norm_g, m_w_in, m_b_gate, m_conv_w, m_conv_b, m_sgu_ln_g, m_sgu_ln_b, m_sgu_w, m_sgu_b, m_q_norm_g, m_k_norm_g, m_w_branch_out, m_w_o, m_ffn_norm_g, m_w_gate_up, m_w_down, v_mix_norm_g, v_w_in, v_b_gate, v_conv_w, v_conv_b, v_sgu_ln_g, v_sgu_ln_b, v_sgu_w, v_sgu_b, v_q_norm_g, v_k_norm_g, v_w_branch_out, v_w_o, v_ffn_norm_g, v_w_gate_up, v_w_down):
    w = dict(mix_norm_g=mix_norm_g, w_in=w_in, b_gate=b_gate, conv_w=conv_w, conv_b=conv_b, sgu_ln_g=sgu_ln_g,
             sgu_ln_b=sgu_ln_b, sgu_w=sgu_w, sgu_b=sgu_b, q_norm_g=q_norm_g, k_norm_g=k_norm_g,
             w_branch_out=w_branch_out, w_o=w_o, ffn_norm_g=ffn_norm_g, w_gate_up=w_gate_up, w_down=w_down)
    m = dict(mix_norm_g=m_mix_norm_g, w_in=m_w_in, b_gate=m_b_gate, conv_w=m_conv_w, conv_b=m_conv_b,
             sgu_ln_g=m_sgu_ln_g, sgu_ln_b=m_sgu_ln_b, sgu_w=m_sgu_w, sgu_b=m_sgu_b, q_norm_g=m_q_norm_g,
             k_norm_g=m_k_norm_g, w_branch_out=m_w_branch_out, w_o=m_w_o, ffn_norm_g=m_ffn_norm_g,
             w_gate_up=m_w_gate_up, w_down=m_w_down)
    v = dict(mix_norm_g=v_mix_norm_g, w_in=v_w_in, b_gate=v_b_gate, conv_w=v_conv_w, conv_b=v_conv_b,
             sgu_ln_g=v_sgu_ln_g, sgu_ln_b=v_sgu_ln_b, sgu_w=v_sgu_w, sgu_b=v_sgu_b, q_norm_g=v_q_norm_g,
             k_norm_g=v_k_norm_g, w_branch_out=v_w_branch_out, w_o=v_w_o, ffn_norm_g=v_ffn_norm_g,
             w_gate_up=v_w_gate_up, w_down=v_w_down)
    me = 4 * lax.axis_index("x") + 2 * lax.axis_index("y") + lax.axis_index("c")
    S = x.shape[1]

    big_shapes = [w[k].shape for k in BIG]
    gathered = _all_gather(_pack_rows([w[k] for k in BIG], BF16), name="gather_weights")
    parts = _unpack_rows_dev(gathered, big_shapes)
    wf = {k: _unshard(p, SHARD_AXIS[k]) for k, p in zip(BIG, parts)}
    conv_g = _all_gather(_pack_rows([conv_w], F32), name="gather_conv_w")
    conv_full = _unshard(_unpack_rows_dev(conv_g, [conv_w.shape])[0], 2)
    sm = {k: w[k] for k in SMALL}
    sm["conv_w"] = conv_full

    tq = _tile(S, 128, 8)
    lpart, dx, grads = _local_step(x[0], loss_target[0], wf, sm, tq=tq, tk=tq)
    loss = lax.psum(lpart, ("x", "y", "c"))

    chunks = jnp.concatenate(
        [_reshard(grads[k], SHARD_AXIS[k]).reshape(N_DEV, -1).astype(BF16) for k in BIG], axis=1)
    recv = _all_to_all(chunks.reshape(N_DEV, -1, LANES), name="exchange_grads")
    gb, db, mb, vb = _adamw(_pack_rows([w[k] for k in BIG], F32), _pack_rows([m[k] for k in BIG], F32),
                            _pack_rows([v[k] for k in BIG], F32), recv, name="adamw_big")
    out = {}
    for nm, packed in (("grad", gb), ("delta", db), ("new_m", mb), ("new_v", vb)):
        for k, a in zip(BIG, _unpack_rows(packed, big_shapes)):
            out[nm, k] = a

    small_shapes = [grads[k].shape for k in SMALL]
    sg = _all_gather(_pack_rows([grads[k] for k in SMALL], F32), name="gather_small_grads")
    gsum = _slot_sum(sg, name="sum_small_grads")
    gsmall = dict(zip(SMALL, _unpack_rows(gsum, small_shapes)))
    cshard = conv_w.shape[2]
    gsmall["conv_w"] = lax.dynamic_slice_in_dim(gsmall["conv_w"], me * cshard, cshard, axis=2)
    own_shapes = [w[k].shape for k in SMALL]
    gs_, ds_, ms_, vs_ = _adamw(_pack_rows([w[k] for k in SMALL], F32), _pack_rows([m[k] for k in SMALL], F32),
                                _pack_rows([v[k] for k in SMALL], F32),
                                _pack_rows([gsmall[k] for k in SMALL], F32)[None], name="adamw_small")
    for nm, packed in (("grad", gs_), ("delta", ds_), ("new_m", ms_), ("new_v", vs_)):
        for k, a in zip(SMALL, _unpack_rows(packed, own_shapes)):
            out[nm, k] = a

    res = [loss, dx[None]]
    for nm in ("grad", "delta", "new_m", "new_v"):
        res += [out[nm, k] for k in WEIGHTS]
    return tuple(res)


def _unpack_rows_dev(gathered, shapes):
    flat = gathered.reshape(N_DEV, -1)
    out, off = [], 0
    for shp in shapes:
        n = math.prod(shp)
        out.append(flat[:, off:off + n].reshape((N_DEV,) + tuple(shp)))
        off += n
    return out
```

```python
import functools
import math

import jax
import jax.numpy as jnp
from jax import lax
from jax.experimental import pallas as pl
from jax.experimental.pallas import tpu as pltpu

F32 = jnp.float32
BF16 = jnp.bfloat16
MESH = pl.DeviceIdType.MESH

N_DEV = 8
LANES = 128
VMEM_LIMIT_BYTES = 56 * 1024 * 1024
EPS = 1e-6
ADAM_LR, ADAM_B1, ADAM_B2, ADAM_EPS, ADAM_WD, ADAM_STEP = 0.001, 0.9, 0.999, 1e-08, 0.01, 10
ATTN_FWD_TILES = (1024, 256)
ATTN_BWD_TILES = (1024, 256)
BIG = ("w_in", "w_branch_out", "w_o", "w_gate_up", "w_down")
SMALL = ("mix_norm_g", "b_gate", "conv_w", "conv_b", "sgu_ln_g", "sgu_ln_b", "sgu_w", "sgu_b",
         "q_norm_g", "k_norm_g", "ffn_norm_g")
WEIGHTS = ("mix_norm_g", "w_in", "b_gate", "conv_w", "conv_b", "sgu_ln_g", "sgu_ln_b", "sgu_w", "sgu_b",
           "q_norm_g", "k_norm_g", "w_branch_out", "w_o", "ffn_norm_g", "w_gate_up", "w_down")
SHARD_AXIS = {"w_in": 2, "w_branch_out": 3, "w_o": 1, "w_gate_up": 2, "w_down": 1}


def _tile(n, cap, mult):
    best = None
    for t in range(mult, min(n, cap) + 1, mult):
        if n % t == 0:
            best = t
    return best if best is not None else n


def _params(*sem):
    return pltpu.CompilerParams(dimension_semantics=sem if sem else None, vmem_limit_bytes=VMEM_LIMIT_BYTES)


def _erf(x):
    return lax.erf(x)


def _gelu(x):
    return 0.5 * x * (1.0 + _erf(x * (1.0 / math.sqrt(2.0))))


def _gelu_grad(x):
    return 0.5 * (1.0 + _erf(x * (1.0 / math.sqrt(2.0)))) + x * jnp.exp(-0.5 * x * x) * (1.0 / math.sqrt(2.0 * math.pi))


def _sigmoid(x):
    return 1.0 / (1.0 + jnp.exp(-x))


def _matmul(a, b, *, name, res=None, out_dtype=F32):
    M, K = a.shape
    _, N = b.shape
    tm, tn, tk = _tile(M, 512, 8), _tile(N, 1536, LANES), _tile(K, 1536, LANES)
    nk = K // tk
    has_res = res is not None

    def body(*refs):
        refs = list(refs)
        acc = refs.pop() if nk > 1 else None
        if has_res:
            a_ref, b_ref, r_ref, o_ref = refs
        else:
            a_ref, b_ref, o_ref = refs
        k = pl.program_id(2)
        part = jnp.dot(a_ref[...], b_ref[...], preferred_element_type=F32)

        def finish(v):
            if has_res:
                v = v + r_ref[...]
            o_ref[...] = v.astype(out_dtype)

        if nk == 1:
            finish(part)
        else:
            @pl.when(k == 0)
            def _():
                acc[...] = part

            @pl.when(jnp.logical_and(k > 0, k < nk - 1))
            def _():
                acc[...] += part

            @pl.when(k == nk - 1)
            def _():
                finish(acc[...] + part)

    in_specs = [pl.BlockSpec((tm, tk), lambda i, j, k: (i, k)), pl.BlockSpec((tk, tn), lambda i, j, k: (k, j))]
    args = [a, b]
    if has_res:
        in_specs.append(pl.BlockSpec((tm, tn), lambda i, j, k: (i, j)))
        args.append(res)
    return pl.pallas_call(
        body, name=name, grid=(M // tm, N // tn, nk), in_specs=in_specs,
        out_specs=pl.BlockSpec((tm, tn), lambda i, j, k: (i, j)),
        out_shape=jax.ShapeDtypeStruct((M, N), out_dtype),
        scratch_shapes=[pltpu.VMEM((tm, tn), F32)] if nk > 1 else [],
        compiler_params=_params("parallel", "parallel", "arbitrary"),
    )(*args)


def _matmul_tn(x, y, *, name):
    S, A = x.shape
    _, B = y.shape
    ta, tb, ts = _tile(A, 512, LANES), _tile(B, 1536, LANES), _tile(S, 512, 8)
    ns = S // ts

    def body(x_ref, y_ref, o_ref):
        s = pl.program_id(2)
        part = lax.dot_general(x_ref[...], y_ref[...], (((0,), (0,)), ((), ())), preferred_element_type=F32)

        @pl.when(s == 0)
        def _():
            o_ref[...] = part

        @pl.when(s > 0)
        def _():
            o_ref[...] += part

    return pl.pallas_call(
        body, name=name, grid=(A // ta, B // tb, ns),
        in_specs=[pl.BlockSpec((ts, ta), lambda i, j, s: (s, i)), pl.BlockSpec((ts, tb), lambda i, j, s: (s, j))],
        out_specs=pl.BlockSpec((ta, tb), lambda i, j, s: (i, j)),
        out_shape=jax.ShapeDtypeStruct((A, B), F32),
        compiler_params=_params("parallel", "parallel", "arbitrary"),
    )(x, y)


def _rmsnorm_fwd(x, g, *, scale, out_dtype, name):
    R, W = x.shape
    tr = _tile(R, 512 if W >= 512 else 4096, 8)

    def body(x_ref, g_ref, o_ref):
        xv = x_ref[...]
        r = lax.rsqrt(jnp.mean(xv * xv, axis=1, keepdims=True) + EPS)
        o_ref[...] = (xv * r * (g_ref[...] * scale)).astype(out_dtype)

    return pl.pallas_call(
        body, name=name, grid=(R // tr,),
        in_specs=[pl.BlockSpec((tr, W), lambda i: (i, 0)), pl.BlockSpec((1, W), lambda i: (0, 0))],
        out_specs=pl.BlockSpec((tr, W), lambda i: (i, 0)),
        out_shape=jax.ShapeDtypeStruct((R, W), out_dtype),
        compiler_params=_params("parallel"),
    )(x, g)


def _rmsnorm_bwd(x, g, dy, *, scale, name, dres=None, out_dtype=F32):
    R, W = x.shape
    tr = _tile(R, 512 if W >= 512 else 4096, 8)
    has_res = dres is not None

    def body(*refs):
        if has_res:
            x_ref, g_ref, dy_ref, dres_ref, dx_ref, dg_ref = refs
        else:
            x_ref, g_ref, dy_ref, dx_ref, dg_ref = refs
        i = pl.program_id(0)
        xv = x_ref[...]
        dyv = dy_ref[...].astype(F32) * scale
        r = lax.rsqrt(jnp.mean(xv * xv, axis=1, keepdims=True) + EPS)
        u = dyv * g_ref[...]
        dx = r * u - xv * (r * r * r * jnp.mean(u * xv, axis=1, keepdims=True))
        if has_res:
            dx = dx + dres_ref[...]
        dx_ref[...] = dx.astype(out_dtype)
        part = jnp.sum(dyv * xv * r, axis=0, keepdims=True)

        @pl.when(i == 0)
        def _():
            dg_ref[...] = part

        @pl.when(i > 0)
        def _():
            dg_ref[...] += part

    row = pl.BlockSpec((tr, W), lambda i: (i, 0))
    one = pl.BlockSpec((1, W), lambda i: (0, 0))
    in_specs = [row, one, row] + ([row] if has_res else [])
    args = [x, g, dy] + ([dres] if has_res else [])
    return pl.pallas_call(
        body, name=name, grid=(R // tr,), in_specs=in_specs, out_specs=[row, one],
        out_shape=[jax.ShapeDtypeStruct((R, W), out_dtype), jax.ShapeDtypeStruct((1, W), F32)],
        compiler_params=_params("arbitrary"),
    )(*args)


def _shift_down(u, prev, n):
    ts = u.shape[0]
    out = pltpu.roll(u, n, 0)
    row = lax.broadcasted_iota(jnp.int32, u.shape, 0)
    for r in range(n):
        out = jnp.where(row == r, prev[8 - n + r:8 - n + r + 1, :], out)
    return out


def _shift_up(u, nxt, n):
    ts = u.shape[0]
    out = pltpu.roll(u, ts - n, 0)
    row = lax.broadcasted_iota(jnp.int32, u.shape, 0)
    for r in range(n):
        out = jnp.where(row == ts - n + r, nxt[r:r + 1, :], out)
    return out


def _conv_fwd(P, conv_w, conv_b, *, C, name):
    S = P.shape[0]
    ts = _tile(S, 512, 8)
    hb = ts // 8

    def body(ab_ref, ac_ref, ax_ref, pc_ref, px_ref, w_ref, b_ref, o_ref):
        i = pl.program_id(0)
        u = ac_ref[...] * ax_ref[...]
        prev = pc_ref[...] * px_ref[...] * (i > 0).astype(F32)
        w = w_ref[...]
        y = b_ref[...] + w[0:1, :] * _shift_down(u, prev, 2) + w[1:2, :] * _shift_down(u, prev, 1) + w[2:3, :] * u
        o_ref[...] = (ab_ref[...] * y).astype(BF16)

    cur = lambda c: pl.BlockSpec((ts, C), lambda i: (i, c))
    prv = lambda c: pl.BlockSpec((8, C), lambda i: (jnp.maximum(i * hb - 1, 0), c))
    return pl.pallas_call(
        body, name=name, grid=(S // ts,),
        in_specs=[cur(0), cur(1), cur(2), prv(1), prv(2),
                  pl.BlockSpec((8, C), lambda i: (0, 0)), pl.BlockSpec((1, C), lambda i: (0, 0))],
        out_specs=pl.BlockSpec((ts, C), lambda i: (i, 0)),
        out_shape=jax.ShapeDtypeStruct((S, C), BF16),
        compiler_params=_params("parallel"),
    )(P, P, P, P, P, conv_w, conv_b)


def _conv_bwd(P, dya, conv_w, conv_b, *, C, name):
    S = P.shape[0]
    ts = _tile(S, 512, 8)
    hb = ts // 8
    last = S // 8 - 1
    n = S // ts

    def body(ab_ref, ac_ref, ax_ref, pc_ref, px_ref, dy_ref, nab_ref, ndy_ref, w_ref, b_ref, o_ref, dw_ref):
        i = pl.program_id(0)
        ab, ac, ax = ab_ref[...], ac_ref[...], ax_ref[...]
        u = ac * ax
        prev = pc_ref[...] * px_ref[...] * (i > 0).astype(F32)
        w = w_ref[...]
        u1, u2 = _shift_down(u, prev, 1), _shift_down(u, prev, 2)
        y = b_ref[...] + w[0:1, :] * u2 + w[1:2, :] * u1 + w[2:3, :] * u
        dya_v = dy_ref[...]
        dyp = dya_v * ab
        nxt = ndy_ref[...] * nab_ref[...] * (i < n - 1).astype(F32)
        du = w[2:3, :] * dyp + w[1:2, :] * _shift_up(dyp, nxt, 1) + w[0:1, :] * _shift_up(dyp, nxt, 2)
        o_ref[:, 0:C] = (dya_v * y).astype(BF16)
        o_ref[:, C:2 * C] = (du * ax).astype(BF16)
        o_ref[:, 2 * C:3 * C] = (du * ac).astype(BF16)
        part = jnp.concatenate([
            jnp.sum(dyp * u2, axis=0, keepdims=True), jnp.sum(dyp * u1, axis=0, keepdims=True),
            jnp.sum(dyp * u, axis=0, keepdims=True), jnp.sum(dyp, axis=0, keepdims=True),
            jnp.zeros((4, C), F32)], axis=0)

        @pl.when(i == 0)
        def _():
            dw_ref[...] = part

        @pl.when(i > 0)
        def _():
            dw_ref[...] += part

    cur = lambda c: pl.BlockSpec((ts, C), lambda i: (i, c))
    prv = lambda c: pl.BlockSpec((8, C), lambda i: (jnp.maximum(i * hb - 1, 0), c))
    nxt = lambda c: pl.BlockSpec((8, C), lambda i: (jnp.minimum((i + 1) * hb, last), c))
    return pl.pallas_call(
        body, name=name, grid=(n,),
        in_specs=[cur(0), cur(1), cur(2), prv(1), prv(2), cur(0), nxt(0), nxt(0),
                  pl.BlockSpec((8, C), lambda i: (0, 0)), pl.BlockSpec((1, C), lambda i: (0, 0))],
        out_specs=[pl.BlockSpec((ts, 3 * C), lambda i: (i, 0)), pl.BlockSpec((8, C), lambda i: (0, 0))],
        out_shape=[jax.ShapeDtypeStruct((S, 3 * C), BF16), jax.ShapeDtypeStruct((8, C), F32)],
        compiler_params=_params("arbitrary"),
    )(P, P, P, P, P, dya, P, dya, conv_w, conv_b)


def _sgu_fwd(P, ln_g, ln_b, wm, bT, *, W, cu, cv, name):
    S = P.shape[0]
    G, CH, _ = wm.shape
    gw = W // G
    ts = _tile(S, 512, CH)

    def body(u_ref, v_ref, g_ref, b_ref, wm_ref, bT_ref, o_ref):
        gv = _gelu(v_ref[...])
        mu = jnp.mean(gv, axis=1, keepdims=True)
        xc = gv - mu
        vn = (xc * lax.rsqrt(jnp.mean(xc * xc, axis=1, keepdims=True) + EPS) * g_ref[...] + b_ref[...]).astype(BF16)
        bT_v = bT_ref[...]
        for c in range(ts // CH):
            rows = slice(c * CH, (c + 1) * CH)
            for g in range(G):
                cols = slice(g * gw, (g + 1) * gw)
                mixed = jnp.dot(wm_ref[g], vn[rows, cols], preferred_element_type=F32) + bT_v[:, g:g + 1]
                o_ref[rows, cols] = (_gelu(u_ref[rows, cols]) * mixed).astype(BF16)

    full = lambda shp: pl.BlockSpec(shp, lambda i: (0,) * len(shp))
    return pl.pallas_call(
        body, name=name, grid=(S // ts,),
        in_specs=[pl.BlockSpec((ts, W), lambda i: (i, cu)), pl.BlockSpec((ts, W), lambda i: (i, cv)),
                  full((1, W)), full((1, W)), full((G, CH, CH)), full((CH, G))],
        out_specs=pl.BlockSpec((ts, W), lambda i: (i, 0)),
        out_shape=jax.ShapeDtypeStruct((S, W), BF16),
        compiler_params=_params("parallel"),
    )(P, P, ln_g, ln_b, wm, bT)


def _sgu_bwd(P, dyb, ln_g, ln_b, wm, wmT, bT, *, W, cu, cv, name):
    S = P.shape[0]
    G, CH, _ = wm.shape
    gw = W // G
    ts = _tile(S, 512, CH)

    def body(u_ref, v_ref, dy_ref, g_ref, b_ref, wm_ref, wmT_ref, bT_ref, o_ref, dw_ref, db_ref, dln_ref, dvn_ref):
        i = pl.program_id(0)

        @pl.when(i == 0)
        def _():
            dw_ref[...] = jnp.zeros_like(dw_ref)
            db_ref[...] = jnp.zeros_like(db_ref)
            dln_ref[...] = jnp.zeros_like(dln_ref)

        sv = v_ref[...]
        gv = _gelu(sv)
        mu = jnp.mean(gv, axis=1, keepdims=True)
        xc = gv - mu
        rstd = lax.rsqrt(jnp.mean(xc * xc, axis=1, keepdims=True) + EPS)
        xhat = xc * rstd
        lg = g_ref[...]
        vn = (xhat * lg + b_ref[...]).astype(BF16)
        bT_v = bT_ref[...]
        for c in range(ts // CH):
            rows = slice(c * CH, (c + 1) * CH)
            for g in range(G):
                cols = slice(g * gw, (g + 1) * gw)
                vn_cg = vn[rows, cols]
                mixed = jnp.dot(wm_ref[g], vn_cg, preferred_element_type=F32) + bT_v[:, g:g + 1]
                su = u_ref[rows, cols]
                dyv = dy_ref[rows, cols]
                dmix = dyv * _gelu(su)
                o_ref[rows, cols] = (dyv * mixed * _gelu_grad(su)).astype(BF16)
                dmix_b = dmix.astype(BF16)
                dw_ref[g] += lax.dot_general(dmix_b, vn_cg, (((1,), (1,)), ((), ())), preferred_element_type=F32)
                db_ref[g] += jnp.broadcast_to(jnp.sum(dmix, axis=1, keepdims=True), (CH, CH))
                dvn_ref[rows, cols] = jnp.dot(wmT_ref[g], dmix_b, preferred_element_type=F32)
        dvn = dvn_ref[...]
        dxh = dvn * lg
        dgv = rstd * (dxh - jnp.mean(dxh, axis=1, keepdims=True) - xhat * jnp.mean(dxh * xhat, axis=1, keepdims=True))
        o_ref[:, W:2 * W] = (dgv * _gelu_grad(sv)).astype(BF16)
        dln_ref[0:1, :] += jnp.sum(dvn * xhat, axis=0, keepdims=True)
        dln_ref[1:2, :] += jnp.sum(dvn, axis=0, keepdims=True)

    full = lambda shp: pl.BlockSpec(shp, lambda i: (0,) * len(shp))
    return pl.pallas_call(
        body, name=name, grid=(S // ts,),
        in_specs=[pl.BlockSpec((ts, W), lambda i: (i, cu)), pl.BlockSpec((ts, W), lambda i: (i, cv)),
                  pl.BlockSpec((ts, W), lambda i: (i, 0)),
                  full((1, W)), full((1, W)), full((G, CH, CH)), full((G, CH, CH)), full((CH, G))],
        out_specs=[pl.BlockSpec((ts, 2 * W), lambda i: (i, 0)), full((G, CH, CH)), full((G, CH, CH)), full((8, W))],
        out_shape=[jax.ShapeDtypeStruct((S, 2 * W), BF16), jax.ShapeDtypeStruct((G, CH, CH), F32),
                   jax.ShapeDtypeStruct((G, CH, CH), F32), jax.ShapeDtypeStruct((8, W), F32)],
        scratch_shapes=[pltpu.VMEM((ts, W), F32)],
        compiler_params=_params("arbitrary"),
    )(P, P, dyb, ln_g, ln_b, wm, wmT, bT)


def _split_dot(x, u):
    hi = x.astype(BF16)
    lo = (x - hi.astype(F32)).astype(BF16)
    return jnp.dot(hi, u, preferred_element_type=F32) + jnp.dot(lo, u, preferred_element_type=F32)


_NT = (((1,), (1,)), ((), ()))
_TN = (((0,), (0,)), ((), ()))


def _attn_fwd(q, k, v, umat, *, tq, tk, name):
    H, S, hd = q.shape

    def body(q_ref, k_ref, v_ref, u_ref, o_ref):
        i = pl.program_id(1)
        qb = q_ref[0]
        um = u_ref[...]
        qpos = lax.broadcasted_iota(jnp.int32, (tq, tk), 0) + i * tq
        kloc = lax.broadcasted_iota(jnp.int32, (tq, tk), 1)

        def step(j, carry, masked):
            r, acc = carry
            ks = pl.multiple_of(j * tk, tk)
            kb = k_ref[0, pl.ds(ks, tk), :]
            vb = v_ref[0, pl.ds(ks, tk), :]
            z = lax.dot_general(qb, kb, _NT, preferred_element_type=F32)
            sp = jnp.log(1.0 + jnp.exp(-jnp.abs(z)))
            lb = jnp.minimum(z, 0.0) - sp
            lm = jnp.minimum(-z, 0.0) - sp
            if masked:
                m = (kloc + j * tk) < qpos
                lm = jnp.where(m, lm, 0.0)
            a = jnp.exp(lb + _split_dot(lm, um) + r)
            if masked:
                a = jnp.where(m, a, 0.0)
            acc = acc + _split_dot(a, vb)
            return r + jnp.sum(lm, axis=1, keepdims=True), acc

        jd = (i * tq) // tk
        carry = (jnp.zeros((tq, 1), F32), jnp.zeros((tq, hd), F32))
        for dd in reversed(range(max(1, tq // tk))):
            carry = step(jd + dd, carry, True)
        unroll = 2 if (tq // tk) % 2 == 0 else 1

        def trip(t, c):
            for s in range(unroll):
                c = step(jd - 1 - s - unroll * t, c, False)
            return c

        carry = lax.fori_loop(0, jd // unroll, trip, carry)
        o_ref[0] = carry[1]

    blk = pl.BlockSpec((1, tq, hd), lambda h, i: (h, i, 0))
    whole = pl.BlockSpec((1, S, hd), lambda h, i: (h, 0, 0))
    return pl.pallas_call(
        body, name=name, grid=(H, S // tq),
        in_specs=[blk, whole, whole, pl.BlockSpec((tk, tk), lambda h, i: (0, 0))],
        out_specs=blk, out_shape=jax.ShapeDtypeStruct((H, S, hd), F32),
        compiler_params=_params("parallel", "arbitrary"),
    )(q, k, v, umat)


def _attn_bwd(q, k, v, o, do, umat, *, tq, tk, name):
    H, S, hd = q.shape

    def body(q_ref, k_ref, v_ref, o_ref, do_ref, u_ref, dq_ref, dk_ref, dv_ref):
        i = pl.program_id(1)

        @pl.when(i == 0)
        def _():
            dk_ref[...] = jnp.zeros_like(dk_ref)
            dv_ref[...] = jnp.zeros_like(dv_ref)

        qb = q_ref[0]
        do32 = do_ref[0]
        dob = do32.astype(BF16)
        tot = jnp.sum(dob.astype(F32) * o_ref[0], axis=1, keepdims=True)
        um = u_ref[...]
        qpos = lax.broadcasted_iota(jnp.int32, (tq, tk), 0) + i * tq
        kloc = lax.broadcasted_iota(jnp.int32, (tq, tk), 1)

        def step(j, carry, masked):
            r, gs, dq = carry
            ks = pl.multiple_of(j * tk, tk)
            kb = k_ref[0, pl.ds(ks, tk), :]
            vb = v_ref[0, pl.ds(ks, tk), :]
            z = lax.dot_general(qb, kb, _NT, preferred_element_type=F32)
            e = jnp.exp(-jnp.abs(z))
            d = 1.0 + e
            sp = jnp.log(d)
            inv = 1.0 / d
            einv = e * inv
            pos = z >= 0.0
            sig = jnp.where(pos, inv, einv)
            nsig = jnp.where(pos, einv, inv)
            lb = jnp.minimum(z, 0.0) - sp
            lm = jnp.minimum(-z, 0.0) - sp
            if masked:
                m = (kloc + j * tk) < qpos
                lm = jnp.where(m, lm, 0.0)
            a = jnp.exp(lb + _split_dot(lm, um) + r)
            if masked:
                a = jnp.where(m, a, 0.0)
            g = lax.dot_general(dob, vb, _NT, preferred_element_type=F32) * a
            c = tot - (_split_dot(g, um) + g + gs)
            dz = g * nsig - c * sig
            if masked:
                dz = jnp.where(m, dz, 0.0)
            dzb = dz.astype(BF16)
            dq = dq + jnp.dot(dzb, kb, preferred_element_type=F32)
            dk_ref[0, pl.ds(ks, tk), :] += lax.dot_general(dzb, qb, _TN, preferred_element_type=F32)
            dv_ref[0, pl.ds(ks, tk), :] += lax.dot_general(a.astype(BF16), dob, _TN, preferred_element_type=F32)
            return r + jnp.sum(lm, axis=1, keepdims=True), gs + jnp.sum(g, axis=1, keepdims=True), dq

        jd = (i * tq) // tk
        zero = jnp.zeros((tq, 1), F32)
        carry = (zero, zero, jnp.zeros((tq, hd), F32))
        for dd in reversed(range(max(1, tq // tk))):
            carry = step(jd + dd, carry, True)
        unroll = 2 if (tq // tk) % 2 == 0 else 1

        def trip(t, c):
            for s in range(unroll):
                c = step(jd - 1 - s - unroll * t, c, False)
            return c

        carry = lax.fori_loop(0, jd // unroll, trip, carry)
        dq_ref[0] = carry[2]

    blk = pl.BlockSpec((1, tq, hd), lambda h, i: (h, i, 0))
    whole = pl.BlockSpec((1, S, hd), lambda h, i: (h, 0, 0))
    shp = jax.ShapeDtypeStruct((H, S, hd), F32)
    return pl.pallas_call(
        body, name=name, grid=(H, S // tq),
        in_specs=[blk, whole, whole, blk, blk, pl.BlockSpec((tk, tk), lambda h, i: (0, 0))],
        out_specs=[blk, whole, whole], out_shape=[shp, shp, shp],
        compiler_params=_params("parallel", "arbitrary"),
    )(q, k, v, o, do, umat)


def _merge_fwd(ys, wb, P, b_gate, *, gate_col0, name):
    S, C = ys[0].shape
    D = wb.shape[2]
    ts = _tile(S, 512, 8)

    def body(y0, y1, y2, wb_ref, g0, g1, g2, bg_ref, o_ref):
        acc = jnp.zeros((ts, D), F32)
        for n, (y_ref, g_ref) in enumerate(((y0, g0), (y1, g1), (y2, g2))):
            yd = jnp.dot(y_ref[...], wb_ref[n], preferred_element_type=F32)
            acc = acc + _sigmoid(g_ref[...] + bg_ref[:, n * D:(n + 1) * D]) * yd
        o_ref[...] = acc.astype(BF16)

    ysp = pl.BlockSpec((ts, C), lambda i: (i, 0))
    gsp = lambda n: pl.BlockSpec((ts, D), lambda i: (i, gate_col0 + n))
    return pl.pallas_call(
        body, name=name, grid=(S // ts,),
        in_specs=[ysp, ysp, ysp, pl.BlockSpec((3, C, D), lambda i: (0, 0, 0)), gsp(0), gsp(1), gsp(2),
                  pl.BlockSpec((1, 3 * D), lambda i: (0, 0))],
        out_specs=pl.BlockSpec((ts, D), lambda i: (i, 0)),
        out_shape=jax.ShapeDtypeStruct((S, D), BF16),
        compiler_params=_params("parallel"),
    )(*ys, wb, P, P, P, b_gate)


def _merge_bwd(ys, wb, wbT, P, b_gate, dmerged, *, gate_col0, name):
    S, C = ys[0].shape
    D = wb.shape[2]
    ts = _tile(S, 256, 8)

    def body(y0, y1, y2, wb_ref, wbT_ref, g0, g1, g2, bg_ref, dm_ref, dg_ref, dyd_ref, dy_ref, dbg_ref):
        i = pl.program_id(0)
        dm = dm_ref[...]
        parts = []
        for n, (y_ref, g_ref) in enumerate(((y0, g0), (y1, g1), (y2, g2))):
            yd = jnp.dot(y_ref[...], wb_ref[n], preferred_element_type=F32)
            sg = _sigmoid(g_ref[...] + bg_ref[:, n * D:(n + 1) * D])
            dgate = dm * yd * sg * (1.0 - sg)
            dg_ref[:, n * D:(n + 1) * D] = dgate.astype(BF16)
            parts.append(jnp.sum(dgate, axis=0, keepdims=True))
            dyd = (dm * sg).astype(BF16)
            dyd_ref[n] = dyd
            dy_ref[n] = jnp.dot(dyd, wbT_ref[n], preferred_element_type=F32)
        part = jnp.concatenate(parts, axis=1)

        @pl.when(i == 0)
        def _():
            dbg_ref[...] = part

        @pl.when(i > 0)
        def _():
            dbg_ref[...] += part

    ysp = pl.BlockSpec((ts, C), lambda i: (i, 0))
    gsp = lambda n: pl.BlockSpec((ts, D), lambda i: (i, gate_col0 + n))
    return pl.pallas_call(
        body, name=name, grid=(S // ts,),
        in_specs=[ysp, ysp, ysp, pl.BlockSpec((3, C, D), lambda i: (0, 0, 0)),
                  pl.BlockSpec((3, D, C), lambda i: (0, 0, 0)), gsp(0), gsp(1), gsp(2),
                  pl.BlockSpec((1, 3 * D), lambda i: (0, 0)), pl.BlockSpec((ts, D), lambda i: (i, 0))],
        out_specs=[pl.BlockSpec((ts, 3 * D), lambda i: (i, 0)), pl.BlockSpec((3, ts, D), lambda i: (0, i, 0)),
                   pl.BlockSpec((3, ts, C), lambda i: (0, i, 0)), pl.BlockSpec((1, 3 * D), lambda i: (0, 0))],
        out_shape=[jax.ShapeDtypeStruct((S, 3 * D), BF16), jax.ShapeDtypeStruct((3, S, D), BF16),
                   jax.ShapeDtypeStruct((3, S, C), F32), jax.ShapeDtypeStruct((1, 3 * D), F32)],
        compiler_params=_params("arbitrary"),
    )(*ys, wb, wbT, P, P, P, b_gate, dmerged)


def _swiglu_fwd(gu, *, name):
    S, F2 = gu.shape
    F = F2 // 2
    ts, tf = _tile(S, 512, 8), _tile(F, 1536, LANES)
    nf = F // tf

    def body(g_ref, u_ref, o_ref):
        gt = g_ref[...]
        o_ref[...] = (gt * _sigmoid(gt) * u_ref[...]).astype(BF16)

    return pl.pallas_call(
        body, name=name, grid=(S // ts, nf),
        in_specs=[pl.BlockSpec((ts, tf), lambda i, j: (i, j)), pl.BlockSpec((ts, tf), lambda i, j: (i, j + nf))],
        out_specs=pl.BlockSpec((ts, tf), lambda i, j: (i, j)),
        out_shape=jax.ShapeDtypeStruct((S, F), BF16),
        compiler_params=_params("parallel", "parallel"),
    )(gu, gu)


def _swiglu_bwd(gu, dact, *, name):
    S, F2 = gu.shape
    F = F2 // 2
    ts, tf = _tile(S, 512, 8), _tile(F, 1536, LANES)
    nf = F // tf

    def body(g_ref, u_ref, d_ref, o_ref):
        j = pl.program_id(1)
        gt, da = g_ref[...], d_ref[...]
        sg = _sigmoid(gt)

        @pl.when(j < nf)
        def _():
            o_ref[...] = (da * u_ref[...] * sg * (1.0 + gt * (1.0 - sg))).astype(BF16)

        @pl.when(j >= nf)
        def _():
            o_ref[...] = (da * gt * sg).astype(BF16)

    return pl.pallas_call(
        body, name=name, grid=(S // ts, 2 * nf),
        in_specs=[pl.BlockSpec((ts, tf), lambda i, j: (i, j % nf)),
                  pl.BlockSpec((ts, tf), lambda i, j: (i, j % nf + nf)),
                  pl.BlockSpec((ts, tf), lambda i, j: (i, j % nf))],
        out_specs=pl.BlockSpec((ts, tf), lambda i, j: (i, j)),
        out_shape=jax.ShapeDtypeStruct((S, F2), BF16),
        compiler_params=_params("parallel", "parallel"),
    )(gu, gu, dact)


def _loss_grad(y, target, *, name):
    S, D = y.shape
    ts = _tile(S, 512, 8)

    def body(y_ref, t_ref, dy_ref, l_ref):
        i = pl.program_id(0)
        err = y_ref[...] - t_ref[...]
        dy_ref[...] = err * (1.0 / D)
        part = jnp.broadcast_to(jnp.sum(jnp.sum(err * err, axis=1, keepdims=True), axis=0, keepdims=True) * (0.5 / D),
                                (1, LANES))

        @pl.when(i == 0)
        def _():
            l_ref[...] = part

        @pl.when(i > 0)
        def _():
            l_ref[...] += part

    row = pl.BlockSpec((ts, D), lambda i: (i, 0))
    return pl.pallas_call(
        body, name=name, grid=(S // ts,), in_specs=[row, row],
        out_specs=[row, pl.BlockSpec((1, LANES), lambda i: (0, 0))],
        out_shape=[jax.ShapeDtypeStruct((S, D), F32), jax.ShapeDtypeStruct((1, LANES), F32)],
        compiler_params=_params("arbitrary"),
    )(y, target)


def _adamw(w, m, v, gs, *, name):
    R = w.shape[0]
    ns = gs.shape[0]
    tr = _tile(R, 2048, 16)
    c1 = 1.0 / (1.0 - ADAM_B1 ** ADAM_STEP)
    c2 = 1.0 / (1.0 - ADAM_B2 ** ADAM_STEP)

    def body(w_ref, m_ref, v_ref, gs_ref, g_ref, d_ref, nm_ref, nv_ref):
        g = gs_ref[0].astype(F32)
        for s in range(1, ns):
            g = g + gs_ref[s].astype(F32)
        nm = ADAM_B1 * m_ref[...] + (1.0 - ADAM_B1) * g
        nv = ADAM_B2 * v_ref[...] + (1.0 - ADAM_B2) * (g * g)
        g_ref[...] = g
        nm_ref[...] = nm
        nv_ref[...] = nv
        d_ref[...] = -ADAM_LR * ((nm * c1) / (jnp.sqrt(nv * c2) + ADAM_EPS) + ADAM_WD * w_ref[...])

    row = pl.BlockSpec((tr, LANES), lambda i: (i, 0))
    shp = jax.ShapeDtypeStruct((R, LANES), F32)
    return pl.pallas_call(
        body, name=name, grid=(R // tr,),
        in_specs=[row, row, row, pl.BlockSpec((ns, tr, LANES), lambda i: (0, i, 0))],
        out_specs=[row, row, row, row], out_shape=[shp, shp, shp, shp],
        compiler_params=_params("parallel"),
    )(w, m, v, gs)


def _slot_sum(gs, *, name):
    ns, R, _ = gs.shape
    tr = _tile(R, 2048, 16)

    def body(gs_ref, o_ref):
        g = gs_ref[0]
        for s in range(1, ns):
            g = g + gs_ref[s]
        o_ref[...] = g

    return pl.pallas_call(
        body, name=name, grid=(R // tr,),
        in_specs=[pl.BlockSpec((ns, tr, LANES), lambda i: (0, i, 0))],
        out_specs=pl.BlockSpec((tr, LANES), lambda i: (i, 0)),
        out_shape=jax.ShapeDtypeStruct((R, LANES), F32),
        compiler_params=_params("parallel"),
    )(gs)


def _all_gather(xs, *, name):
    R, L = xs.shape

    def body(x_ref, out_ref, send_sems, recv_sems, local_sem):
        x, y, c = lax.axis_index("x"), lax.axis_index("y"), lax.axis_index("c")
        me, sibling = (x, y, c), (x, y, 1 - c)
        chips = [(1 - x, y), (x, 1 - y), (1 - x, 1 - y)]

        def slot(px, py, pc):
            return out_ref.at[4 * px + 2 * py + pc]

        def copy(k, block, to, src=None):
            return pltpu.make_async_remote_copy(
                src_ref=slot(*block) if src is None else src, dst_ref=slot(*block),
                send_sem=send_sems.at[k], recv_sem=recv_sems.at[k], device_id=to, device_id_type=MESH)

        mine = pltpu.make_async_copy(x_ref, slot(*me), local_sem)
        mine.start()
        first = [copy(0, me, sibling, src=x_ref)]
        first += [copy(1 + j, me, (*chip, c), src=x_ref) for j, chip in enumerate(chips)]
        for cp in first:
            cp.start()
        passed = [copy(4 + j, (*chip, c), sibling) for j, chip in enumerate(chips)]
        for j, chip in enumerate(chips):
            copy(1 + j, (*chip, c), me).wait_recv()
            passed[j].start()
        copy(0, sibling, me).wait_recv()
        for j, chip in enumerate(chips):
            copy(4 + j, (*chip, 1 - c), me).wait_recv()
        for cp in first + passed:
            cp.wait_send()
        mine.wait()

    return pl.pallas_call(
        body, name=name,
        in_specs=[pl.BlockSpec(memory_space=pl.ANY)], out_specs=pl.BlockSpec(memory_space=pl.ANY),
        out_shape=jax.ShapeDtypeStruct((N_DEV, R, L), xs.dtype),
        scratch_shapes=[pltpu.SemaphoreType.DMA((7,)), pltpu.SemaphoreType.DMA((7,)), pltpu.SemaphoreType.DMA],
        compiler_params=pltpu.CompilerParams(has_side_effects=True),
    )(xs)


def _all_to_all(xs, *, name):
    _, R, L = xs.shape

    def body(x_ref, out_ref, send_sems, recv_sems, local_sem):
        x, y, c = lax.axis_index("x"), lax.axis_index("y"), lax.axis_index("c")
        me = 4 * x + 2 * y + c
        mine = pltpu.make_async_copy(x_ref.at[me], out_ref.at[me], local_sem)
        mine.start()
        copies = []
        for k in range(1, N_DEV):
            px = 1 - x if k & 4 else x
            py = 1 - y if k & 2 else y
            pc = 1 - c if k & 1 else c
            peer = 4 * px + 2 * py + pc
            copies.append((
                pltpu.make_async_remote_copy(
                    src_ref=x_ref.at[peer], dst_ref=out_ref.at[me], send_sem=send_sems.at[k - 1],
                    recv_sem=recv_sems.at[k - 1], device_id=(px, py, pc), device_id_type=MESH),
                pltpu.make_async_remote_copy(
                    src_ref=x_ref.at[me], dst_ref=out_ref.at[peer], send_sem=send_sems.at[k - 1],
                    recv_sem=recv_sems.at[k - 1], device_id=(px, py, pc), device_id_type=MESH)))
        for send, _ in copies:
            send.start()
        for _, recv in copies:
            recv.wait_recv()
        for send, _ in copies:
            send.wait_send()
        mine.wait()

    return pl.pallas_call(
        body, name=name,
        in_specs=[pl.BlockSpec(memory_space=pl.ANY)], out_specs=pl.BlockSpec(memory_space=pl.ANY),
        out_shape=jax.ShapeDtypeStruct(xs.shape, xs.dtype),
        scratch_shapes=[pltpu.SemaphoreType.DMA((7,)), pltpu.SemaphoreType.DMA((7,)), pltpu.SemaphoreType.DMA],
        compiler_params=pltpu.CompilerParams(has_side_effects=True),
    )(xs)


PACK_ROWS = 16


def _pack_rows(parts, dtype, lead=()):
    rows = []
    for p in parts:
        r = p.reshape(lead + (-1, LANES)).astype(dtype)
        pad = (-r.shape[-2]) % PACK_ROWS
        rows.append(jnp.pad(r, [(0, 0)] * len(lead) + [(0, pad), (0, 0)]) if pad else r)
    return jnp.concatenate(rows, axis=len(lead))


def _unpack_rows(packed, shapes, lead=()):
    out, off = [], 0
    for shp in shapes:
        n = math.prod(shp) // LANES
        out.append(lax.slice_in_dim(packed, off, off + n, axis=len(lead)).reshape(lead + tuple(shp)))
        off += n + (-n) % PACK_ROWS
    return out


def _unshard(gathered, axis):
    g = jnp.moveaxis(gathered, 0, axis)
    shp = list(g.shape)
    shp[axis:axis + 2] = [shp[axis] * shp[axis + 1]]
    return g.reshape(shp)


def _reshard(full, axis):
    shp = list(full.shape)
    shp[axis:axis + 1] = [N_DEV, shp[axis] // N_DEV]
    return jnp.moveaxis(full.reshape(shp), axis, 0)


def _heads(a, H):
    S = a.shape[0]
    return a.reshape(S, H, -1).transpose(1, 0, 2)


def _unheads(a):
    H, S, hd = a.shape
    return a.transpose(1, 0, 2).reshape(S, H * hd)


def _local_step(x, target, wf, sm):
    S, D = x.shape
    L = wf["w_in"].shape[0]
    C = sm["conv_b"].shape[1]
    W = sm["sgu_ln_g"].shape[1]
    hd = sm["q_norm_g"].shape[1]
    G, CH = sm["sgu_w"].shape[1], sm["sgu_w"].shape[2]
    F = wf["w_down"].shape[1]
    A = wf["w_in"].shape[2] - (3 * C + 2 * W + 3 * D)
    A = A // 3
    H = A // hd
    col_q = 3 * C + 2 * W
    qscale = 1.0 / math.sqrt(hd)
    tril = jnp.tril(jnp.ones((CH, CH), F32))
    (ftq, ftk), (btq, btk) = [(_tile(S, a, LANES), _tile(S, b, LANES)) for a, b in (ATTN_FWD_TILES, ATTN_BWD_TILES)]
    umat = lambda t: (lax.broadcasted_iota(jnp.int32, (t, t), 0) > lax.broadcasted_iota(jnp.int32, (t, t), 1)).astype(BF16)

    saved = []
    for l in range(L):
        n = f"l{l}_"
        g1 = sm["mix_norm_g"][l][None]
        h = _rmsnorm_fwd(x, g1, scale=1.0, out_dtype=BF16, name=n + "mixnorm")
        P = _matmul(h, wf["w_in"][l], name=n + "w_in")
        cw = jnp.pad(sm["conv_w"][l], ((0, 5), (0, 0)))
        cb = sm["conv_b"][l][None]
        ya = _conv_fwd(P, cw, cb, C=C, name=n + "conv")
        wm = (sm["sgu_w"][l] * tril).astype(BF16)
        bT = sm["sgu_b"][l].T
        lng, lnb = sm["sgu_ln_g"][l][None], sm["sgu_ln_b"][l][None]
        yb = _sgu_fwd(P, lng, lnb, wm, bT, W=W, cu=(3 * C) // W, cv=(3 * C) // W + 1, name=n + "sgu")
        qkv = P[:, col_q:col_q + 3 * A].reshape(S, 3, H, hd).transpose(1, 2, 0, 3)
        q_raw, k_raw = qkv[0].reshape(H * S, hd), qkv[1].reshape(H * S, hd)
        gq, gk = sm["q_norm_g"][l][None], sm["k_norm_g"][l][None]
        qn = _rmsnorm_fwd(q_raw, gq, scale=qscale, out_dtype=BF16, name=n + "qnorm").reshape(H, S, hd)
        kn = _rmsnorm_fwd(k_raw, gk, scale=1.0, out_dtype=BF16, name=n + "knorm").reshape(H, S, hd)
        vh = qkv[2].astype(BF16)
        o = _attn_fwd(qn, kn, vh, umat(ftk), tq=ftq, tk=ftk, name=n + "attn")
        yc = _unheads(o).astype(BF16)
        bg = sm["b_gate"][l][None]
        gate_col0 = (col_q + 3 * A) // D
        merged = _merge_fwd((ya, yb, yc), wf["w_branch_out"][l], P, bg, gate_col0=gate_col0, name=n + "merge")
        x1 = _matmul(merged, wf["w_o"][l], res=x, name=n + "w_o")
        g2 = sm["ffn_norm_g"][l][None]
        h2 = _rmsnorm_fwd(x1, g2, scale=1.0, out_dtype=BF16, name=n + "ffnnorm")
        gu = _matmul(h2, wf["w_gate_up"][l], name=n + "w_gate_up")
        act = _swiglu_fwd(gu, name=n + "swiglu")
        x2 = _matmul(act, wf["w_down"][l], res=x1, name=n + "w_down")
        saved.append(dict(x=x, h=h, P=P, cw=cw, cb=cb, ya=ya, wm=wm, bT=bT, lng=lng, lnb=lnb, yb=yb,
                          q_raw=q_raw, k_raw=k_raw, gq=gq, gk=gk, qn=qn, kn=kn, vh=vh, o=o, yc=yc, bg=bg,
                          gate_col0=gate_col0, merged=merged, x1=x1, g1=g1, g2=g2, h2=h2, gu=gu, act=act))
        x = x2

    dx, lpart = _loss_grad(x, target, name="loss")
    grads = {k: [None] * L for k in WEIGHTS}
    for l in reversed(range(L)):
        n = f"l{l}_b_"
        sv = saved[l]
        dxb = dx.astype(BF16)
        grads["w_down"][l] = _matmul_tn(sv["act"], dxb, name=n + "g_w_down")
        dact = _matmul(dxb, wf["w_down"][l].T, name=n + "d_act")
        dgu = _swiglu_bwd(sv["gu"], dact, name=n + "swiglu")
        grads["w_gate_up"][l] = _matmul_tn(sv["h2"], dgu, name=n + "g_w_gate_up")
        dh2 = _matmul(dgu, wf["w_gate_up"][l].T, name=n + "d_h2")
        dx1, dg2 = _rmsnorm_bwd(sv["x1"], sv["g2"], dh2, scale=1.0, dres=dx, name=n + "ffnnorm")
        grads["ffn_norm_g"][l] = dg2[0]
        dx1b = dx1.astype(BF16)
        grads["w_o"][l] = _matmul_tn(sv["merged"], dx1b, name=n + "g_w_o")
        dmerged = _matmul(dx1b, wf["w_o"][l].T, name=n + "d_merged")
        ys = (sv["ya"], sv["yb"], sv["yc"])
        wb = wf["w_branch_out"][l]
        dgates, dyd, dy, dbg = _merge_bwd(ys, wb, wb.transpose(0, 2, 1), sv["P"], sv["bg"], dmerged,
                                          gate_col0=sv["gate_col0"], name=n + "merge")
        grads["b_gate"][l] = dbg[0]
        grads["w_branch_out"][l] = jnp.stack(
            [_matmul_tn(ys[i], dyd[i], name=n + f"g_w_branch{i}") for i in range(3)])
        dconv, dcw = _conv_bwd(sv["P"], dy[0], sv["cw"], sv["cb"], C=C, name=n + "conv")
        grads["conv_w"][l], grads["conv_b"][l] = dcw[0:3], dcw[3]
        wmT = sv["wm"].transpose(0, 2, 1)
        dsgu, dsw, dsb, dln = _sgu_bwd(sv["P"], dy[1], sv["lng"], sv["lnb"], sv["wm"], wmT, sv["bT"], W=W,
                                       cu=(3 * C) // W, cv=(3 * C) // W + 1, name=n + "sgu")
        grads["sgu_w"][l], grads["sgu_b"][l] = dsw * tril, dsb[:, :, 0]
        grads["sgu_ln_g"][l], grads["sgu_ln_b"][l] = dln[0], dln[1]
        do = _heads(dy[2], H)
        dqn, dkn, dvh = _attn_bwd(sv["qn"], sv["kn"], sv["vh"], sv["o"], do, umat(btk), tq=btq, tk=btk,
                                   name=n + "attn")
        dq_raw, dgq = _rmsnorm_bwd(sv["q_raw"], sv["gq"], dqn.reshape(H * S, hd), scale=qscale, name=n + "qnorm",
                                   out_dtype=BF16)
        dk_raw, dgk = _rmsnorm_bwd(sv["k_raw"], sv["gk"], dkn.reshape(H * S, hd), scale=1.0, name=n + "knorm",
                                   out_dtype=BF16)
        grads["q_norm_g"][l], grads["k_norm_g"][l] = dgq[0], dgk[0]
        dP = jnp.concatenate([dconv, dsgu, _unheads(dq_raw.reshape(H, S, hd)), _unheads(dk_raw.reshape(H, S, hd)),
                              _unheads(dvh).astype(BF16), dgates], axis=1)
        grads["w_in"][l] = _matmul_tn(sv["h"], dP, name=n + "g_w_in")
        dh = _matmul(dP, wf["w_in"][l].T, name=n + "d_h")
        dx, dg1 = _rmsnorm_bwd(sv["x"], sv["g1"], dh, scale=1.0, dres=dx1, name=n + "mixnorm")
        grads["mix_norm_g"][l] = dg1[0]
    return lpart[0, 0], dx, {k: jnp.stack(v) for k, v in grads.items()}


def kernel(x, mix_norm_g, w_in, b_gate, conv_w, conv_b, sgu_ln_g, sgu_ln_b, sgu_w, sgu_b, q_norm_g, k_norm_g, w_branch_out, w_o, ffn_norm_g, w_gate_up, w_down, loss_target, m_mix_norm_g, m_w_in, m_b_gate, m_conv_w, m_conv_b, m_sgu_ln_g, m_sgu_ln_b, m_sgu_w, m_sgu_b, m_q_norm_g, m_k_norm_g, m_w_branch_out, m_w_o, m_ffn_norm_g, m_w_gate_up, m_w_down, v_mix_norm_g, v_w_in, v_b_gate, v_conv_w, v_conv_b, v_sgu_ln_g, v_sgu_ln_b, v_sgu_w, v_sgu_b, v_q_norm_g, v_k_norm_g, v_w_branch_out, v_w_o, v_ffn_norm_g, v_w_gate_up, v_w_down):
    w = dict(mix_norm_g=mix_norm_g, w_in=w_in, b_gate=b_gate, conv_w=conv_w, conv_b=conv_b, sgu_ln_g=sgu_ln_g,
             sgu_ln_b=sgu_ln_b, sgu_w=sgu_w, sgu_b=sgu_b, q_norm_g=q_norm_g, k_norm_g=k_norm_g,
             w_branch_out=w_branch_out, w_o=w_o, ffn_norm_g=ffn_norm_g, w_gate_up=w_gate_up, w_down=w_down)
    m = dict(mix_norm_g=m_mix_norm_g, w_in=m_w_in, b_gate=m_b_gate, conv_w=m_conv_w, conv_b=m_conv_b,
             sgu_ln_g=m_sgu_ln_g, sgu_ln_b=m_sgu_ln_b, sgu_w=m_sgu_w, sgu_b=m_sgu_b, q_norm_g=m_q_norm_g,
             k_norm_g=m_k_norm_g, w_branch_out=m_w_branch_out, w_o=m_w_o, ffn_norm_g=m_ffn_norm_g,
             w_gate_up=m_w_gate_up, w_down=m_w_down)
    v = dict(mix_norm_g=v_mix_norm_g, w_in=v_w_in, b_gate=v_b_gate, conv_w=v_conv_w, conv_b=v_conv_b,
             sgu_ln_g=v_sgu_ln_g, sgu_ln_b=v_sgu_ln_b, sgu_w=v_sgu_w, sgu_b=v_sgu_b, q_norm_g=v_q_norm_g,
             k_norm_g=v_k_norm_g, w_branch_out=v_w_branch_out, w_o=v_w_o, ffn_norm_g=v_ffn_norm_g,
             w_gate_up=v_w_gate_up, w_down=v_w_down)
    me = 4 * lax.axis_index("x") + 2 * lax.axis_index("y") + lax.axis_index("c")
    S = x.shape[1]

    big_shapes = [w[k].shape for k in BIG]
    gathered = _all_gather(_pack_rows([w[k] for k in BIG], BF16), name="gather_weights")
    parts = _unpack_rows(gathered, big_shapes, lead=(N_DEV,))
    wf = {k: _unshard(p, SHARD_AXIS[k]) for k, p in zip(BIG, parts)}
    conv_g = _all_gather(_pack_rows([conv_w], F32), name="gather_conv_w")
    conv_full = _unshard(_unpack_rows(conv_g, [conv_w.shape], lead=(N_DEV,))[0], 2)
    sm = {k: w[k] for k in SMALL}
    sm["conv_w"] = conv_full

    lpart, dx, grads = _local_step(x[0], loss_target[0], wf, sm)
    loss = lax.psum(lpart, ("x", "y", "c"))

    chunks = _pack_rows([_reshard(grads[k], SHARD_AXIS[k]) for k in BIG], BF16, lead=(N_DEV,))
    recv = _all_to_all(chunks, name="exchange_grads")
    gb, db, mb, vb = _adamw(_pack_rows([w[k] for k in BIG], F32), _pack_rows([m[k] for k in BIG], F32),
                            _pack_rows([v[k] for k in BIG], F32), recv, name="adamw_big")
    out = {}
    for nm, packed in (("grad", gb), ("delta", db), ("new_m", mb), ("new_v", vb)):
        for k, a in zip(BIG, _unpack_rows(packed, big_shapes)):
            out[nm, k] = a

    small_shapes = [grads[k].shape for k in SMALL]
    sg = _all_gather(_pack_rows([grads[k] for k in SMALL], F32), name="gather_small_grads")
    gsum = _slot_sum(sg, name="sum_small_grads")
    gsmall = dict(zip(SMALL, _unpack_rows(gsum, small_shapes)))
    cshard = conv_w.shape[2]
    gsmall["conv_w"] = lax.dynamic_slice_in_dim(gsmall["conv_w"], me * cshard, cshard, axis=2)
    own_shapes = [w[k].shape for k in SMALL]
    gs_, ds_, ms_, vs_ = _adamw(_pack_rows([w[k] for k in SMALL], F32), _pack_rows([m[k] for k in SMALL], F32),
                                _pack_rows([v[k] for k in SMALL], F32),
                                _pack_rows([gsmall[k] for k in SMALL], F32)[None], name="adamw_small")
    for nm, packed in (("grad", gs_), ("delta", ds_), ("new_m", ms_), ("new_v", vs_)):
        for k, a in zip(SMALL, _unpack_rows(packed, own_shapes)):
            out[nm, k] = a

    res = [loss, dx[None]]
    for nm in ("grad", "delta", "new_m", "new_v"):
        res += [out[nm, k] for k in WEIGHTS]
    return tuple(res)
```

```python
import functools
import math

import jax
import jax.numpy as jnp
from jax import lax
from jax.experimental import pallas as pl
from jax.experimental.pallas import tpu as pltpu

F32 = jnp.float32
BF16 = jnp.bfloat16
MESH = pl.DeviceIdType.MESH

N_DEV = 8
LANES = 128
VMEM_LIMIT_BYTES = 56 * 1024 * 1024
EPS = 1e-6
ADAM_LR, ADAM_B1, ADAM_B2, ADAM_EPS, ADAM_WD, ADAM_STEP = 0.001, 0.9, 0.999, 1e-08, 0.01, 10
ATTN_FWD_TILES = (1024, 256)
ATTN_BWD_TILES = (1024, 256)
BIG = ("w_in", "w_branch_out", "w_o", "w_gate_up", "w_down")
SMALL = ("mix_norm_g", "b_gate", "conv_w", "conv_b", "sgu_ln_g", "sgu_ln_b", "sgu_w", "sgu_b",
         "q_norm_g", "k_norm_g", "ffn_norm_g")
WEIGHTS = ("mix_norm_g", "w_in", "b_gate", "conv_w", "conv_b", "sgu_ln_g", "sgu_ln_b", "sgu_w", "sgu_b",
           "q_norm_g", "k_norm_g", "w_branch_out", "w_o", "ffn_norm_g", "w_gate_up", "w_down")
SHARD_AXIS = {"w_in": 2, "w_branch_out": 3, "w_o": 1, "w_gate_up": 2, "w_down": 1}


def _tile(n, cap, mult):
    best = None
    for t in range(mult, min(n, cap) + 1, mult):
        if n % t == 0:
            best = t
    return best if best is not None else n


def _params(*sem):
    return pltpu.CompilerParams(dimension_semantics=sem if sem else None, vmem_limit_bytes=VMEM_LIMIT_BYTES)


def _erf(x):
    return lax.erf(x)


def _gelu(x):
    return 0.5 * x * (1.0 + _erf(x * (1.0 / math.sqrt(2.0))))


def _gelu_grad(x):
    return 0.5 * (1.0 + _erf(x * (1.0 / math.sqrt(2.0)))) + x * jnp.exp(-0.5 * x * x) * (1.0 / math.sqrt(2.0 * math.pi))


def _sigmoid(x):
    return 1.0 / (1.0 + jnp.exp(-x))


def _matmul(a, b, *, name, res=None, out_dtype=F32):
    M, K = a.shape
    _, N = b.shape
    tm, tn, tk = _tile(M, 1024, 8), _tile(N, 1536, LANES), _tile(K, 1536, LANES)
    nk = K // tk
    has_res = res is not None

    def body(*refs):
        refs = list(refs)
        acc = refs.pop() if nk > 1 else None
        if has_res:
            a_ref, b_ref, r_ref, o_ref = refs
        else:
            a_ref, b_ref, o_ref = refs
        k = pl.program_id(2)
        part = jnp.dot(a_ref[...], b_ref[...], preferred_element_type=F32)

        def finish(v):
            if has_res:
                v = v + r_ref[...]
            o_ref[...] = v.astype(out_dtype)

        if nk == 1:
            finish(part)
        else:
            @pl.when(k == 0)
            def _():
                acc[...] = part

            @pl.when(jnp.logical_and(k > 0, k < nk - 1))
            def _():
                acc[...] += part

            @pl.when(k == nk - 1)
            def _():
                finish(acc[...] + part)

    in_specs = [pl.BlockSpec((tm, tk), lambda i, j, k: (i, k)), pl.BlockSpec((tk, tn), lambda i, j, k: (k, j))]
    args = [a, b]
    if has_res:
        in_specs.append(pl.BlockSpec((tm, tn), lambda i, j, k: (i, j)))
        args.append(res)
    return pl.pallas_call(
        body, name=name, grid=(M // tm, N // tn, nk), in_specs=in_specs,
        out_specs=pl.BlockSpec((tm, tn), lambda i, j, k: (i, j)),
        out_shape=jax.ShapeDtypeStruct((M, N), out_dtype),
        scratch_shapes=[pltpu.VMEM((tm, tn), F32)] if nk > 1 else [],
        compiler_params=_params("parallel", "parallel", "arbitrary"),
    )(*args)


def _matmul_tn(x, y, *, name):
    S, A = x.shape
    _, B = y.shape
    ta, tb, ts = _tile(A, 1024, LANES), _tile(B, 1536, LANES), _tile(S, 1024, 8)
    ns = S // ts

    def body(x_ref, y_ref, o_ref):
        s = pl.program_id(2)
        part = lax.dot_general(x_ref[...], y_ref[...], (((0,), (0,)), ((), ())), preferred_element_type=F32)

        @pl.when(s == 0)
        def _():
            o_ref[...] = part

        @pl.when(s > 0)
        def _():
            o_ref[...] += part

    return pl.pallas_call(
        body, name=name, grid=(A // ta, B // tb, ns),
        in_specs=[pl.BlockSpec((ts, ta), lambda i, j, s: (s, i)), pl.BlockSpec((ts, tb), lambda i, j, s: (s, j))],
        out_specs=pl.BlockSpec((ta, tb), lambda i, j, s: (i, j)),
        out_shape=jax.ShapeDtypeStruct((A, B), F32),
        compiler_params=_params("parallel", "parallel", "arbitrary"),
    )(x, y)


def _rmsnorm_fwd(x, g, *, scale, out_dtype, name):
    R, W = x.shape
    tr = _tile(R, 512 if W >= 512 else 4096, 8)

    def body(x_ref, g_ref, o_ref):
        xv = x_ref[...]
        r = lax.rsqrt(jnp.mean(xv * xv, axis=1, keepdims=True) + EPS)
        o_ref[...] = (xv * r * (g_ref[...] * scale)).astype(out_dtype)

    return pl.pallas_call(
        body, name=name, grid=(R // tr,),
        in_specs=[pl.BlockSpec((tr, W), lambda i: (i, 0)), pl.BlockSpec((1, W), lambda i: (0, 0))],
        out_specs=pl.BlockSpec((tr, W), lambda i: (i, 0)),
        out_shape=jax.ShapeDtypeStruct((R, W), out_dtype),
        compiler_params=_params("parallel"),
    )(x, g)


def _rmsnorm_bwd(x, g, dy, *, scale, name, dres=None, out_dtype=F32, bf16_copy=False):
    R, W = x.shape
    tr = _tile(R, 512 if W >= 512 else 4096, 8)
    has_res = dres is not None

    def body(*refs):
        refs = list(refs)
        dxb_ref = refs.pop() if bf16_copy else None
        if has_res:
            x_ref, g_ref, dy_ref, dres_ref, dx_ref, dg_ref = refs
        else:
            x_ref, g_ref, dy_ref, dx_ref, dg_ref = refs
        i = pl.program_id(0)
        xv = x_ref[...]
        dyv = dy_ref[...].astype(F32) * scale
        r = lax.rsqrt(jnp.mean(xv * xv, axis=1, keepdims=True) + EPS)
        u = dyv * g_ref[...]
        dx = r * u - xv * (r * r * r * jnp.mean(u * xv, axis=1, keepdims=True))
        if has_res:
            dx = dx + dres_ref[...]
        dx_ref[...] = dx.astype(out_dtype)
        if bf16_copy:
            dxb_ref[...] = dx.astype(BF16)
        part = jnp.sum(dyv * xv * r, axis=0, keepdims=True)

        @pl.when(i == 0)
        def _():
            dg_ref[...] = part

        @pl.when(i > 0)
        def _():
            dg_ref[...] += part

    row = pl.BlockSpec((tr, W), lambda i: (i, 0))
    one = pl.BlockSpec((1, W), lambda i: (0, 0))
    in_specs = [row, one, row] + ([row] if has_res else [])
    args = [x, g, dy] + ([dres] if has_res else [])
    extra = bool(bf16_copy)
    return pl.pallas_call(
        body, name=name, grid=(R // tr,), in_specs=in_specs, out_specs=[row, one] + [row] * extra,
        out_shape=[jax.ShapeDtypeStruct((R, W), out_dtype), jax.ShapeDtypeStruct((1, W), F32)]
        + [jax.ShapeDtypeStruct((R, W), BF16)] * extra,
        compiler_params=_params("arbitrary"),
    )(*args)


def _shift_down(u, prev, n):
    ts = u.shape[0]
    out = pltpu.roll(u, n, 0)
    row = lax.broadcasted_iota(jnp.int32, u.shape, 0)
    for r in range(n):
        out = jnp.where(row == r, prev[8 - n + r:8 - n + r + 1, :], out)
    return out


def _shift_up(u, nxt, n):
    ts = u.shape[0]
    out = pltpu.roll(u, ts - n, 0)
    row = lax.broadcasted_iota(jnp.int32, u.shape, 0)
    for r in range(n):
        out = jnp.where(row == ts - n + r, nxt[r:r + 1, :], out)
    return out


def _conv_fwd(P, conv_w, conv_b, *, C, name):
    S = P.shape[0]
    ts = _tile(S, 512, 8)
    hb = ts // 8

    def body(ab_ref, ac_ref, ax_ref, pc_ref, px_ref, w_ref, b_ref, o_ref):
        i = pl.program_id(0)
        u = ac_ref[...] * ax_ref[...]
        prev = pc_ref[...] * px_ref[...] * (i > 0).astype(F32)
        w = w_ref[...]
        y = b_ref[...] + w[0:1, :] * _shift_down(u, prev, 2) + w[1:2, :] * _shift_down(u, prev, 1) + w[2:3, :] * u
        o_ref[...] = (ab_ref[...] * y).astype(BF16)

    cur = lambda c: pl.BlockSpec((ts, C), lambda i: (i, c))
    prv = lambda c: pl.BlockSpec((8, C), lambda i: (jnp.maximum(i * hb - 1, 0), c))
    return pl.pallas_call(
        body, name=name, grid=(S // ts,),
        in_specs=[cur(0), cur(1), cur(2), prv(1), prv(2),
                  pl.BlockSpec((8, C), lambda i: (0, 0)), pl.BlockSpec((1, C), lambda i: (0, 0))],
        out_specs=pl.BlockSpec((ts, C), lambda i: (i, 0)),
        out_shape=jax.ShapeDtypeStruct((S, C), BF16),
        compiler_params=_params("parallel"),
    )(P, P, P, P, P, conv_w, conv_b)


def _conv_bwd(P, dya, conv_w, conv_b, *, C, name):
    S = P.shape[0]
    ts = _tile(S, 512, 8)
    hb = ts // 8
    last = S // 8 - 1
    n = S // ts

    def body(ab_ref, ac_ref, ax_ref, pc_ref, px_ref, dy_ref, nab_ref, ndy_ref, w_ref, b_ref, o_ref, dw_ref):
        i = pl.program_id(0)
        ab, ac, ax = ab_ref[...], ac_ref[...], ax_ref[...]
        u = ac * ax
        prev = pc_ref[...] * px_ref[...] * (i > 0).astype(F32)
        w = w_ref[...]
        u1, u2 = _shift_down(u, prev, 1), _shift_down(u, prev, 2)
        y = b_ref[...] + w[0:1, :] * u2 + w[1:2, :] * u1 + w[2:3, :] * u
        dya_v = dy_ref[...]
        dyp = dya_v * ab
        nxt = ndy_ref[...] * nab_ref[...] * (i < n - 1).astype(F32)
        du = w[2:3, :] * dyp + w[1:2, :] * _shift_up(dyp, nxt, 1) + w[0:1, :] * _shift_up(dyp, nxt, 2)
        o_ref[:, 0:C] = (dya_v * y).astype(BF16)
        o_ref[:, C:2 * C] = (du * ax).astype(BF16)
        o_ref[:, 2 * C:3 * C] = (du * ac).astype(BF16)
        part = jnp.concatenate([
            jnp.sum(dyp * u2, axis=0, keepdims=True), jnp.sum(dyp * u1, axis=0, keepdims=True),
            jnp.sum(dyp * u, axis=0, keepdims=True), jnp.sum(dyp, axis=0, keepdims=True),
            jnp.zeros((4, C), F32)], axis=0)

        @pl.when(i == 0)
        def _():
            dw_ref[...] = part

        @pl.when(i > 0)
        def _():
            dw_ref[...] += part

    cur = lambda c: pl.BlockSpec((ts, C), lambda i: (i, c))
    prv = lambda c: pl.BlockSpec((8, C), lambda i: (jnp.maximum(i * hb - 1, 0), c))
    nxt = lambda c: pl.BlockSpec((8, C), lambda i: (jnp.minimum((i + 1) * hb, last), c))
    return pl.pallas_call(
        body, name=name, grid=(n,),
        in_specs=[cur(0), cur(1), cur(2), prv(1), prv(2), cur(0), nxt(0), nxt(0),
                  pl.BlockSpec((8, C), lambda i: (0, 0)), pl.BlockSpec((1, C), lambda i: (0, 0))],
        out_specs=[pl.BlockSpec((ts, 3 * C), lambda i: (i, 0)), pl.BlockSpec((8, C), lambda i: (0, 0))],
        out_shape=[jax.ShapeDtypeStruct((S, 3 * C), BF16), jax.ShapeDtypeStruct((8, C), F32)],
        compiler_params=_params("arbitrary"),
    )(P, P, P, P, P, dya, P, dya, conv_w, conv_b)


def _sgu_fwd(P, ln_g, ln_b, wm, bT, *, W, cu, cv, name):
    S = P.shape[0]
    G, CH, _ = wm.shape
    gw = W // G
    ts = _tile(S, 512, CH)

    def body(u_ref, v_ref, g_ref, b_ref, wm_ref, bT_ref, o_ref):
        gv = _gelu(v_ref[...])
        mu = jnp.mean(gv, axis=1, keepdims=True)
        xc = gv - mu
        vn = (xc * lax.rsqrt(jnp.mean(xc * xc, axis=1, keepdims=True) + EPS) * g_ref[...] + b_ref[...]).astype(BF16)
        bT_v = bT_ref[...]
        for c in range(ts // CH):
            rows = slice(c * CH, (c + 1) * CH)
            for g in range(G):
                cols = slice(g * gw, (g + 1) * gw)
                mixed = jnp.dot(wm_ref[g], vn[rows, cols], preferred_element_type=F32) + bT_v[:, g:g + 1]
                o_ref[rows, cols] = (_gelu(u_ref[rows, cols]) * mixed).astype(BF16)

    full = lambda shp: pl.BlockSpec(shp, lambda i: (0,) * len(shp))
    return pl.pallas_call(
        body, name=name, grid=(S // ts,),
        in_specs=[pl.BlockSpec((ts, W), lambda i: (i, cu)), pl.BlockSpec((ts, W), lambda i: (i, cv)),
                  full((1, W)), full((1, W)), full((G, CH, CH)), full((CH, G))],
        out_specs=pl.BlockSpec((ts, W), lambda i: (i, 0)),
        out_shape=jax.ShapeDtypeStruct((S, W), BF16),
        compiler_params=_params("parallel"),
    )(P, P, ln_g, ln_b, wm, bT)


def _sgu_bwd(P, dyb, ln_g, ln_b, wm, wmT, bT, *, W, cu, cv, name):
    S = P.shape[0]
    G, CH, _ = wm.shape
    gw = W // G
    ts = _tile(S, 512, CH)

    def body(u_ref, v_ref, dy_ref, g_ref, b_ref, wm_ref, wmT_ref, bT_ref, o_ref, dw_ref, db_ref, dln_ref, dvn_ref):
        i = pl.program_id(0)

        @pl.when(i == 0)
        def _():
            dw_ref[...] = jnp.zeros_like(dw_ref)
            db_ref[...] = jnp.zeros_like(db_ref)
            dln_ref[...] = jnp.zeros_like(dln_ref)

        sv = v_ref[...]
        gv = _gelu(sv)
        mu = jnp.mean(gv, axis=1, keepdims=True)
        xc = gv - mu
        rstd = lax.rsqrt(jnp.mean(xc * xc, axis=1, keepdims=True) + EPS)
        xhat = xc * rstd
        lg = g_ref[...]
        vn = (xhat * lg + b_ref[...]).astype(BF16)
        bT_v = bT_ref[...]
        for c in range(ts // CH):
            rows = slice(c * CH, (c + 1) * CH)
            for g in range(G):
                cols = slice(g * gw, (g + 1) * gw)
                vn_cg = vn[rows, cols]
                mixed = jnp.dot(wm_ref[g], vn_cg, preferred_element_type=F32) + bT_v[:, g:g + 1]
                su = u_ref[rows, cols]
                dyv = dy_ref[rows, cols]
                dmix = dyv * _gelu(su)
                o_ref[rows, cols] = (dyv * mixed * _gelu_grad(su)).astype(BF16)
                dmix_b = dmix.astype(BF16)
                dw_ref[g] += lax.dot_general(dmix_b, vn_cg, (((1,), (1,)), ((), ())), preferred_element_type=F32)
                db_ref[g] += jnp.broadcast_to(jnp.sum(dmix, axis=1, keepdims=True), (CH, CH))
                dvn_ref[rows, cols] = jnp.dot(wmT_ref[g], dmix_b, preferred_element_type=F32)
        dvn = dvn_ref[...]
        dxh = dvn * lg
        dgv = rstd * (dxh - jnp.mean(dxh, axis=1, keepdims=True) - xhat * jnp.mean(dxh * xhat, axis=1, keepdims=True))
        o_ref[:, W:2 * W] = (dgv * _gelu_grad(sv)).astype(BF16)
        dln_ref[0:1, :] += jnp.sum(dvn * xhat, axis=0, keepdims=True)
        dln_ref[1:2, :] += jnp.sum(dvn, axis=0, keepdims=True)

    full = lambda shp: pl.BlockSpec(shp, lambda i: (0,) * len(shp))
    return pl.pallas_call(
        body, name=name, grid=(S // ts,),
        in_specs=[pl.BlockSpec((ts, W), lambda i: (i, cu)), pl.BlockSpec((ts, W), lambda i: (i, cv)),
                  pl.BlockSpec((ts, W), lambda i: (i, 0)),
                  full((1, W)), full((1, W)), full((G, CH, CH)), full((G, CH, CH)), full((CH, G))],
        out_specs=[pl.BlockSpec((ts, 2 * W), lambda i: (i, 0)), full((G, CH, CH)), full((G, CH, CH)), full((8, W))],
        out_shape=[jax.ShapeDtypeStruct((S, 2 * W), BF16), jax.ShapeDtypeStruct((G, CH, CH), F32),
                   jax.ShapeDtypeStruct((G, CH, CH), F32), jax.ShapeDtypeStruct((8, W), F32)],
        scratch_shapes=[pltpu.VMEM((ts, W), F32)],
        compiler_params=_params("arbitrary"),
    )(P, P, dyb, ln_g, ln_b, wm, wmT, bT)


def _suffix_sums(x, u, parts=1):
    hi = x.astype(BF16)
    out = jnp.dot(hi, u, preferred_element_type=F32)
    if parts == 2:
        lo = (x - hi.astype(F32)).astype(BF16)
        out = out + jnp.dot(lo, u, preferred_element_type=F32)
    return out


_NT = (((1,), (1,)), ((), ()))
_TN = (((0,), (0,)), ((), ()))


def _attn_fwd(q, k, v, umat, *, tq, tk, name):
    H, S, hd = q.shape

    def body(q_ref, k_ref, v_ref, u_ref, o_ref):
        i = pl.program_id(1)
        qb = q_ref[0]
        um = u_ref[...]
        qpos = lax.broadcasted_iota(jnp.int32, (tq, tk), 0) + i * tq
        kloc = lax.broadcasted_iota(jnp.int32, (tq, tk), 1)

        def step(j, carry, masked):
            r, acc = carry
            ks = pl.multiple_of(j * tk, tk)
            kb = k_ref[0, pl.ds(ks, tk), :]
            vb = v_ref[0, pl.ds(ks, tk), :]
            z = lax.dot_general(qb, kb, _NT, preferred_element_type=F32)
            sp = jnp.log(1.0 + jnp.exp(-jnp.abs(z)))
            lb = jnp.minimum(z, 0.0) - sp
            lm = jnp.minimum(-z, 0.0) - sp
            if masked:
                m = (kloc + j * tk) < qpos
                lm = jnp.where(m, lm, 0.0)
            a = jnp.exp(lb + _suffix_sums(lm, um) + r)
            if masked:
                a = jnp.where(m, a, 0.0)
            acc = acc + jnp.dot(a.astype(BF16), vb, preferred_element_type=F32)
            return r + jnp.sum(lm, axis=1, keepdims=True), acc

        jd = (i * tq) // tk
        carry = (jnp.zeros((tq, 1), F32), jnp.zeros((tq, hd), F32))
        for dd in reversed(range(max(1, tq // tk))):
            carry = step(jd + dd, carry, True)
        unroll = 2 if (tq // tk) % 2 == 0 else 1

        def trip(t, c):
            for s in range(unroll):
                c = step(jd - 1 - s - unroll * t, c, False)
            return c

        carry = lax.fori_loop(0, jd // unroll, trip, carry)
        o_ref[0] = carry[1]

    blk = pl.BlockSpec((1, tq, hd), lambda h, i: (h, i, 0))
    whole = pl.BlockSpec((1, S, hd), lambda h, i: (h, 0, 0))
    return pl.pallas_call(
        body, name=name, grid=(H, S // tq),
        in_specs=[blk, whole, whole, pl.BlockSpec((tk, tk), lambda h, i: (0, 0))],
        out_specs=blk, out_shape=jax.ShapeDtypeStruct((H, S, hd), F32),
        compiler_params=_params("parallel", "arbitrary"),
    )(q, k, v, umat)


def _attn_bwd(q, k, v, o, do, umat, *, tq, tk, name):
    H, S, hd = q.shape

    def body(q_ref, k_ref, v_ref, o_ref, do_ref, u_ref, dq_ref, dk_ref, dv_ref):
        i = pl.program_id(1)

        @pl.when(i == 0)
        def _():
            dk_ref[...] = jnp.zeros_like(dk_ref)
            dv_ref[...] = jnp.zeros_like(dv_ref)

        qb = q_ref[0]
        do32 = do_ref[0]
        dob = do32.astype(BF16)
        tot = jnp.sum(dob.astype(F32) * o_ref[0], axis=1, keepdims=True)
        um = u_ref[...]
        qpos = lax.broadcasted_iota(jnp.int32, (tq, tk), 0) + i * tq
        kloc = lax.broadcasted_iota(jnp.int32, (tq, tk), 1)

        def step(j, carry, masked):
            r, gs, dq = carry
            ks = pl.multiple_of(j * tk, tk)
            kb = k_ref[0, pl.ds(ks, tk), :]
            vb = v_ref[0, pl.ds(ks, tk), :]
            z = lax.dot_general(qb, kb, _NT, preferred_element_type=F32)
            e = jnp.exp(-jnp.abs(z))
            d = 1.0 + e
            sp = jnp.log(d)
            inv = 1.0 / d
            einv = e * inv
            pos = z >= 0.0
            sig = jnp.where(pos, inv, einv)
            nsig = jnp.where(pos, einv, inv)
            lb = jnp.minimum(z, 0.0) - sp
            lm = jnp.minimum(-z, 0.0) - sp
            if masked:
                m = (kloc + j * tk) < qpos
                lm = jnp.where(m, lm, 0.0)
            a = jnp.exp(lb + _suffix_sums(lm, um) + r)
            if masked:
                a = jnp.where(m, a, 0.0)
            ab = a.astype(BF16)
            g = lax.dot_general(dob, vb, _NT, preferred_element_type=F32) * ab.astype(F32)
            c = tot - (_suffix_sums(g, um, parts=2) + g + gs)
            dz = g * nsig - c * sig
            if masked:
                dz = jnp.where(m, dz, 0.0)
            dzb = dz.astype(BF16)
            dq = dq + jnp.dot(dzb, kb, preferred_element_type=F32)
            dk_ref[0, pl.ds(ks, tk), :] += lax.dot_general(dzb, qb, _TN, preferred_element_type=F32)
            dv_ref[0, pl.ds(ks, tk), :] += lax.dot_general(ab, dob, _TN, preferred_element_type=F32)
            return r + jnp.sum(lm, axis=1, keepdims=True), gs + jnp.sum(g, axis=1, keepdims=True), dq

        jd = (i * tq) // tk
        zero = jnp.zeros((tq, 1), F32)
        carry = (zero, zero, jnp.zeros((tq, hd), F32))
        for dd in reversed(range(max(1, tq // tk))):
            carry = step(jd + dd, carry, True)
        unroll = 2 if (tq // tk) % 2 == 0 else 1

        def trip(t, c):
            for s in range(unroll):
                c = step(jd - 1 - s - unroll * t, c, False)
            return c

        carry = lax.fori_loop(0, jd // unroll, trip, carry)
        dq_ref[0] = carry[2]

    blk = pl.BlockSpec((1, tq, hd), lambda h, i: (h, i, 0))
    whole = pl.BlockSpec((1, S, hd), lambda h, i: (h, 0, 0))
    shp = jax.ShapeDtypeStruct((H, S, hd), F32)
    return pl.pallas_call(
        body, name=name, grid=(H, S // tq),
        in_specs=[blk, whole, whole, blk, blk, pl.BlockSpec((tk, tk), lambda h, i: (0, 0))],
        out_specs=[blk, whole, whole], out_shape=[shp, shp, shp],
        compiler_params=_params("parallel", "arbitrary"),
    )(q, k, v, o, do, umat)


def _merge_fwd(ys, wb, P, b_gate, *, gate_col0, name):
    S, C = ys[0].shape
    D = wb.shape[2]
    ts = _tile(S, 512, 8)

    def body(y0, y1, y2, wb_ref, g0, g1, g2, bg_ref, o_ref):
        acc = jnp.zeros((ts, D), F32)
        for n, (y_ref, g_ref) in enumerate(((y0, g0), (y1, g1), (y2, g2))):
            yd = jnp.dot(y_ref[...], wb_ref[n], preferred_element_type=F32)
            acc = acc + _sigmoid(g_ref[...] + bg_ref[:, n * D:(n + 1) * D]) * yd
        o_ref[...] = acc.astype(BF16)

    ysp = pl.BlockSpec((ts, C), lambda i: (i, 0))
    gsp = lambda n: pl.BlockSpec((ts, D), lambda i: (i, gate_col0 + n))
    return pl.pallas_call(
        body, name=name, grid=(S // ts,),
        in_specs=[ysp, ysp, ysp, pl.BlockSpec((3, C, D), lambda i: (0, 0, 0)), gsp(0), gsp(1), gsp(2),
                  pl.BlockSpec((1, 3 * D), lambda i: (0, 0))],
        out_specs=pl.BlockSpec((ts, D), lambda i: (i, 0)),
        out_shape=jax.ShapeDtypeStruct((S, D), BF16),
        compiler_params=_params("parallel"),
    )(*ys, wb, P, P, P, b_gate)


def _merge_bwd(ys, wb, wbT, P, b_gate, dmerged, *, gate_col0, name):
    S, C = ys[0].shape
    D = wb.shape[2]
    ts = _tile(S, 256, 8)

    def body(y0, y1, y2, wb_ref, wbT_ref, g0, g1, g2, bg_ref, dm_ref, dg_ref, dyd_ref, dy_ref, dbg_ref):
        i = pl.program_id(0)
        dm = dm_ref[...]
        parts = []
        for n, (y_ref, g_ref) in enumerate(((y0, g0), (y1, g1), (y2, g2))):
            yd = jnp.dot(y_ref[...], wb_ref[n], preferred_element_type=F32)
            sg = _sigmoid(g_ref[...] + bg_ref[:, n * D:(n + 1) * D])
            dgate = dm * yd * sg * (1.0 - sg)
            dg_ref[:, n * D:(n + 1) * D] = dgate.astype(BF16)
            parts.append(jnp.sum(dgate, axis=0, keepdims=True))
            dyd = (dm * sg).astype(BF16)
            dyd_ref[n] = dyd
            dy_ref[n] = jnp.dot(dyd, wbT_ref[n], preferred_element_type=F32)
        part = jnp.concatenate(parts, axis=1)

        @pl.when(i == 0)
        def _():
            dbg_ref[...] = part

        @pl.when(i > 0)
        def _():
            dbg_ref[...] += part

    ysp = pl.BlockSpec((ts, C), lambda i: (i, 0))
    gsp = lambda n: pl.BlockSpec((ts, D), lambda i: (i, gate_col0 + n))
    return pl.pallas_call(
        body, name=name, grid=(S // ts,),
        in_specs=[ysp, ysp, ysp, pl.BlockSpec((3, C, D), lambda i: (0, 0, 0)),
                  pl.BlockSpec((3, D, C), lambda i: (0, 0, 0)), gsp(0), gsp(1), gsp(2),
                  pl.BlockSpec((1, 3 * D), lambda i: (0, 0)), pl.BlockSpec((ts, D), lambda i: (i, 0))],
        out_specs=[pl.BlockSpec((ts, 3 * D), lambda i: (i, 0)), pl.BlockSpec((3, ts, D), lambda i: (0, i, 0)),
                   pl.BlockSpec((3, ts, C), lambda i: (0, i, 0)), pl.BlockSpec((1, 3 * D), lambda i: (0, 0))],
        out_shape=[jax.ShapeDtypeStruct((S, 3 * D), BF16), jax.ShapeDtypeStruct((3, S, D), BF16),
                   jax.ShapeDtypeStruct((3, S, C), F32), jax.ShapeDtypeStruct((1, 3 * D), F32)],
        compiler_params=_params("arbitrary"),
    )(*ys, wb, wbT, P, P, P, b_gate, dmerged)


def _swiglu_fwd(gu, *, name):
    S, F2 = gu.shape
    F = F2 // 2
    ts, tf = _tile(S, 512, 8), _tile(F, 1536, LANES)
    nf = F // tf

    def body(g_ref, u_ref, o_ref):
        gt = g_ref[...]
        o_ref[...] = (gt * _sigmoid(gt) * u_ref[...]).astype(BF16)

    return pl.pallas_call(
        body, name=name, grid=(S // ts, nf),
        in_specs=[pl.BlockSpec((ts, tf), lambda i, j: (i, j)), pl.BlockSpec((ts, tf), lambda i, j: (i, j + nf))],
        out_specs=pl.BlockSpec((ts, tf), lambda i, j: (i, j)),
        out_shape=jax.ShapeDtypeStruct((S, F), BF16),
        compiler_params=_params("parallel", "parallel"),
    )(gu, gu)


def _swiglu_bwd(gu, dact, *, name):
    S, F2 = gu.shape
    F = F2 // 2
    ts = _tile(S, 256, 8)

    def body(gu_ref, d_ref, o_ref):
        gt, up, da = gu_ref[:, 0:F], gu_ref[:, F:F2], d_ref[...]
        sg = _sigmoid(gt)
        o_ref[:, 0:F] = (da * up * sg * (1.0 + gt * (1.0 - sg))).astype(BF16)
        o_ref[:, F:F2] = (da * gt * sg).astype(BF16)

    return pl.pallas_call(
        body, name=name, grid=(S // ts,),
        in_specs=[pl.BlockSpec((ts, F2), lambda i: (i, 0)), pl.BlockSpec((ts, F), lambda i: (i, 0))],
        out_specs=pl.BlockSpec((ts, F2), lambda i: (i, 0)),
        out_shape=jax.ShapeDtypeStruct((S, F2), BF16),
        compiler_params=_params("parallel"),
    )(gu, dact)


def _loss_grad(y, target, *, name):
    S, D = y.shape
    ts = _tile(S, 512, 8)

    def body(y_ref, t_ref, dy_ref, l_ref, dyb_ref):
        i = pl.program_id(0)
        err = y_ref[...] - t_ref[...]
        dy_ref[...] = err * (1.0 / D)
        dyb_ref[...] = (err * (1.0 / D)).astype(BF16)
        part = jnp.broadcast_to(jnp.sum(jnp.sum(err * err, axis=1, keepdims=True), axis=0, keepdims=True) * (0.5 / D),
                                (1, LANES))

        @pl.when(i == 0)
        def _():
            l_ref[...] = part

        @pl.when(i > 0)
        def _():
            l_ref[...] += part

    row = pl.BlockSpec((ts, D), lambda i: (i, 0))
    return pl.pallas_call(
        body, name=name, grid=(S // ts,), in_specs=[row, row],
        out_specs=[row, pl.BlockSpec((1, LANES), lambda i: (0, 0)), row],
        out_shape=[jax.ShapeDtypeStruct((S, D), F32), jax.ShapeDtypeStruct((1, LANES), F32),
                   jax.ShapeDtypeStruct((S, D), BF16)],
        compiler_params=_params("arbitrary"),
    )(y, target)


def _adamw(w, m, v, gs, *, name):
    R = w.shape[0]
    ns = gs.shape[0]
    tr = _tile(R, 2048, 16)
    c1 = 1.0 / (1.0 - ADAM_B1 ** ADAM_STEP)
    c2 = 1.0 / (1.0 - ADAM_B2 ** ADAM_STEP)

    def body(w_ref, m_ref, v_ref, gs_ref, g_ref, d_ref, nm_ref, nv_ref):
        g = gs_ref[0].astype(F32)
        for s in range(1, ns):
            g = g + gs_ref[s].astype(F32)
        nm = ADAM_B1 * m_ref[...] + (1.0 - ADAM_B1) * g
        nv = ADAM_B2 * v_ref[...] + (1.0 - ADAM_B2) * (g * g)
        g_ref[...] = g
        nm_ref[...] = nm
        nv_ref[...] = nv
        d_ref[...] = -ADAM_LR * ((nm * c1) / (jnp.sqrt(nv * c2) + ADAM_EPS) + ADAM_WD * w_ref[...])

    row = pl.BlockSpec((tr, LANES), lambda i: (i, 0))
    shp = jax.ShapeDtypeStruct((R, LANES), F32)
    return pl.pallas_call(
        body, name=name, grid=(R // tr,),
        in_specs=[row, row, row, pl.BlockSpec((ns, tr, LANES), lambda i: (0, i, 0))],
        out_specs=[row, row, row, row], out_shape=[shp, shp, shp, shp],
        compiler_params=_params("parallel"),
    )(w, m, v, gs)


def _slot_sum(gs, *, name):
    ns, R, _ = gs.shape
    tr = _tile(R, 2048, 16)

    def body(gs_ref, o_ref):
        g = gs_ref[0]
        for s in range(1, ns):
            g = g + gs_ref[s]
        o_ref[...] = g

    return pl.pallas_call(
        body, name=name, grid=(R // tr,),
        in_specs=[pl.BlockSpec((ns, tr, LANES), lambda i: (0, i, 0))],
        out_specs=pl.BlockSpec((tr, LANES), lambda i: (i, 0)),
        out_shape=jax.ShapeDtypeStruct((R, LANES), F32),
        compiler_params=_params("parallel"),
    )(gs)


def _all_gather(xs, *, name):
    R, L = xs.shape

    def body(x_ref, out_ref, send_sems, recv_sems, local_sem):
        x, y, c = lax.axis_index("x"), lax.axis_index("y"), lax.axis_index("c")
        me, sibling = (x, y, c), (x, y, 1 - c)
        chips = [(1 - x, y), (x, 1 - y), (1 - x, 1 - y)]

        def slot(px, py, pc):
            return out_ref.at[4 * px + 2 * py + pc]

        def copy(k, block, to, src=None):
            return pltpu.make_async_remote_copy(
                src_ref=slot(*block) if src is None else src, dst_ref=slot(*block),
                send_sem=send_sems.at[k], recv_sem=recv_sems.at[k], device_id=to, device_id_type=MESH)

        mine = pltpu.make_async_copy(x_ref, slot(*me), local_sem)
        mine.start()
        first = [copy(0, me, sibling, src=x_ref)]
        first += [copy(1 + j, me, (*chip, c), src=x_ref) for j, chip in enumerate(chips)]
        for cp in first:
            cp.start()
        passed = [copy(4 + j, (*chip, c), sibling) for j, chip in enumerate(chips)]
        for j, chip in enumerate(chips):
            copy(1 + j, (*chip, c), me).wait_recv()
            passed[j].start()
        copy(0, sibling, me).wait_recv()
        for j, chip in enumerate(chips):
            copy(4 + j, (*chip, 1 - c), me).wait_recv()
        for cp in first + passed:
            cp.wait_send()
        mine.wait()

    return pl.pallas_call(
        body, name=name,
        in_specs=[pl.BlockSpec(memory_space=pl.ANY)], out_specs=pl.BlockSpec(memory_space=pl.ANY),
        out_shape=jax.ShapeDtypeStruct((N_DEV, R, L), xs.dtype),
        scratch_shapes=[pltpu.SemaphoreType.DMA((7,)), pltpu.SemaphoreType.DMA((7,)), pltpu.SemaphoreType.DMA],
        compiler_params=pltpu.CompilerParams(has_side_effects=True),
    )(xs)


def _all_to_all(xs, *, name):
    _, R, L = xs.shape

    def body(x_ref, out_ref, send_sems, recv_sems, local_sem):
        x, y, c = lax.axis_index("x"), lax.axis_index("y"), lax.axis_index("c")
        me = 4 * x + 2 * y + c
        mine = pltpu.make_async_copy(x_ref.at[me], out_ref.at[me], local_sem)
        mine.start()
        copies = []
        for k in range(1, N_DEV):
            px = 1 - x if k & 4 else x
            py = 1 - y if k & 2 else y
            pc = 1 - c if k & 1 else c
            peer = 4 * px + 2 * py + pc
            copies.append((
                pltpu.make_async_remote_copy(
                    src_ref=x_ref.at[peer], dst_ref=out_ref.at[me], send_sem=send_sems.at[k - 1],
                    recv_sem=recv_sems.at[k - 1], device_id=(px, py, pc), device_id_type=MESH),
                pltpu.make_async_remote_copy(
                    src_ref=x_ref.at[me], dst_ref=out_ref.at[peer], send_sem=send_sems.at[k - 1],
                    recv_sem=recv_sems.at[k - 1], device_id=(px, py, pc), device_id_type=MESH)))
        for send, _ in copies:
            send.start()
        for _, recv in copies:
            recv.wait_recv()
        for send, _ in copies:
            send.wait_send()
        mine.wait()

    return pl.pallas_call(
        body, name=name,
        in_specs=[pl.BlockSpec(memory_space=pl.ANY)], out_specs=pl.BlockSpec(memory_space=pl.ANY),
        out_shape=jax.ShapeDtypeStruct(xs.shape, xs.dtype),
        scratch_shapes=[pltpu.SemaphoreType.DMA((7,)), pltpu.SemaphoreType.DMA((7,)), pltpu.SemaphoreType.DMA],
        compiler_params=pltpu.CompilerParams(has_side_effects=True),
    )(xs)


PACK_ROWS = 16


def _pack_rows(parts, dtype, lead=()):
    rows = []
    for p in parts:
        r = p.reshape(lead + (-1, LANES)).astype(dtype)
        pad = (-r.shape[-2]) % PACK_ROWS
        rows.append(jnp.pad(r, [(0, 0)] * len(lead) + [(0, pad), (0, 0)]) if pad else r)
    return jnp.concatenate(rows, axis=len(lead))


def _unpack_rows(packed, shapes, lead=()):
    out, off = [], 0
    for shp in shapes:
        n = math.prod(shp) // LANES
        out.append(lax.slice_in_dim(packed, off, off + n, axis=len(lead)).reshape(lead + tuple(shp)))
        off += n + (-n) % PACK_ROWS
    return out


def _unshard(gathered, axis):
    g = jnp.moveaxis(gathered, 0, axis)
    shp = list(g.shape)
    shp[axis:axis + 2] = [shp[axis] * shp[axis + 1]]
    return g.reshape(shp)


def _reshard(full, axis):
    shp = list(full.shape)
    shp[axis:axis + 1] = [N_DEV, shp[axis] // N_DEV]
    return jnp.moveaxis(full.reshape(shp), axis, 0)


def _heads(a, H):
    S = a.shape[0]
    return a.reshape(S, H, -1).transpose(1, 0, 2)


def _unheads(a):
    H, S, hd = a.shape
    return a.transpose(1, 0, 2).reshape(S, H * hd)


def _local_step(x, target, wf, sm):
    S, D = x.shape
    L = wf["w_in"].shape[0]
    C = sm["conv_b"].shape[1]
    W = sm["sgu_ln_g"].shape[1]
    hd = sm["q_norm_g"].shape[1]
    G, CH = sm["sgu_w"].shape[1], sm["sgu_w"].shape[2]
    F = wf["w_down"].shape[1]
    A = wf["w_in"].shape[2] - (3 * C + 2 * W + 3 * D)
    A = A // 3
    H = A // hd
    col_q = 3 * C + 2 * W
    qscale = 1.0 / math.sqrt(hd)
    tril = jnp.tril(jnp.ones((CH, CH), F32))
    (ftq, ftk), (btq, btk) = [(_tile(S, a, LANES), _tile(S, b, LANES)) for a, b in (ATTN_FWD_TILES, ATTN_BWD_TILES)]
    umat = lambda t: (lax.broadcasted_iota(jnp.int32, (t, t), 0) > lax.broadcasted_iota(jnp.int32, (t, t), 1)).astype(BF16)

    saved = []
    for l in range(L):
        n = f"l{l}_"
        g1 = sm["mix_norm_g"][l][None]
        h = _rmsnorm_fwd(x, g1, scale=1.0, out_dtype=BF16, name=n + "mixnorm")
        P = _matmul(h, wf["w_in"][l], name=n + "w_in")
        cw = jnp.pad(sm["conv_w"][l], ((0, 5), (0, 0)))
        cb = sm["conv_b"][l][None]
        ya = _conv_fwd(P, cw, cb, C=C, name=n + "conv")
        wm = (sm["sgu_w"][l] * tril).astype(BF16)
        bT = sm["sgu_b"][l].T
        lng, lnb = sm["sgu_ln_g"][l][None], sm["sgu_ln_b"][l][None]
        yb = _sgu_fwd(P, lng, lnb, wm, bT, W=W, cu=(3 * C) // W, cv=(3 * C) // W + 1, name=n + "sgu")
        qkv = P[:, col_q:col_q + 3 * A].reshape(S, 3, H, hd).transpose(1, 2, 0, 3)
        q_raw, k_raw = qkv[0].reshape(H * S, hd), qkv[1].reshape(H * S, hd)
        gq, gk = sm["q_norm_g"][l][None], sm["k_norm_g"][l][None]
        qn = _rmsnorm_fwd(q_raw, gq, scale=qscale, out_dtype=BF16, name=n + "qnorm").reshape(H, S, hd)
        kn = _rmsnorm_fwd(k_raw, gk, scale=1.0, out_dtype=BF16, name=n + "knorm").reshape(H, S, hd)
        vh = qkv[2].astype(BF16)
        o = _attn_fwd(qn, kn, vh, umat(ftk), tq=ftq, tk=ftk, name=n + "attn")
        yc = _unheads(o).astype(BF16)
        bg = sm["b_gate"][l][None]
        gate_col0 = (col_q + 3 * A) // D
        merged = _merge_fwd((ya, yb, yc), wf["w_branch_out"][l], P, bg, gate_col0=gate_col0, name=n + "merge")
        x1 = _matmul(merged, wf["w_o"][l], res=x, name=n + "w_o")
        g2 = sm["ffn_norm_g"][l][None]
        h2 = _rmsnorm_fwd(x1, g2, scale=1.0, out_dtype=BF16, name=n + "ffnnorm")
        gu = _matmul(h2, wf["w_gate_up"][l], name=n + "w_gate_up")
        act = _swiglu_fwd(gu, name=n + "swiglu")
        x2 = _matmul(act, wf["w_down"][l], res=x1, name=n + "w_down")
        saved.append(dict(x=x, h=h, P=P, cw=cw, cb=cb, ya=ya, wm=wm, bT=bT, lng=lng, lnb=lnb, yb=yb,
                          q_raw=q_raw, k_raw=k_raw, gq=gq, gk=gk, qn=qn, kn=kn, vh=vh, o=o, yc=yc, bg=bg,
                          gate_col0=gate_col0, merged=merged, x1=x1, g1=g1, g2=g2, h2=h2, gu=gu, act=act))
        x = x2

    dx, lpart, dxb = _loss_grad(x, target, name="loss")
    grads = {k: [None] * L for k in WEIGHTS}
    for l in reversed(range(L)):
        n = f"l{l}_b_"
        sv = saved[l]
        grads["w_down"][l] = _matmul_tn(sv["act"], dxb, name=n + "g_w_down")
        dact = _matmul(dxb, wf["w_down"][l].T, name=n + "d_act")
        dgu = _swiglu_bwd(sv["gu"], dact, name=n + "swiglu")
        grads["w_gate_up"][l] = _matmul_tn(sv["h2"], dgu, name=n + "g_w_gate_up")
        dh2 = _matmul(dgu, wf["w_gate_up"][l].T, name=n + "d_h2")
        dx1, dg2, dx1b = _rmsnorm_bwd(sv["x1"], sv["g2"], dh2, scale=1.0, dres=dx, name=n + "ffnnorm",
                                      bf16_copy=True)
        grads["ffn_norm_g"][l] = dg2[0]
        grads["w_o"][l] = _matmul_tn(sv["merged"], dx1b, name=n + "g_w_o")
        dmerged = _matmul(dx1b, wf["w_o"][l].T, name=n + "d_merged")
        ys = (sv["ya"], sv["yb"], sv["yc"])
        wb = wf["w_branch_out"][l]
        dgates, dyd, dy, dbg = _merge_bwd(ys, wb, wb.transpose(0, 2, 1), sv["P"], sv["bg"], dmerged,
                                          gate_col0=sv["gate_col0"], name=n + "merge")
        grads["b_gate"][l] = dbg[0]
        grads["w_branch_out"][l] = jnp.stack(
            [_matmul_tn(ys[i], dyd[i], name=n + f"g_w_branch{i}") for i in range(3)])
        dconv, dcw = _conv_bwd(sv["P"], dy[0], sv["cw"], sv["cb"], C=C, name=n + "conv")
        grads["conv_w"][l], grads["conv_b"][l] = dcw[0:3], dcw[3]
        wmT = sv["wm"].transpose(0, 2, 1)
        dsgu, dsw, dsb, dln = _sgu_bwd(sv["P"], dy[1], sv["lng"], sv["lnb"], sv["wm"], wmT, sv["bT"], W=W,
                                       cu=(3 * C) // W, cv=(3 * C) // W + 1, name=n + "sgu")
        grads["sgu_w"][l], grads["sgu_b"][l] = dsw * tril, dsb[:, :, 0]
        grads["sgu_ln_g"][l], grads["sgu_ln_b"][l] = dln[0], dln[1]
        do = _heads(dy[2], H)
        dqn, dkn, dvh = _attn_bwd(sv["qn"], sv["kn"], sv["vh"], sv["o"], do, umat(btk), tq=btq, tk=btk,
                                   name=n + "attn")
        dq_raw, dgq = _rmsnorm_bwd(sv["q_raw"], sv["gq"], dqn.reshape(H * S, hd), scale=qscale, name=n + "qnorm",
                                   out_dtype=BF16)
        dk_raw, dgk = _rmsnorm_bwd(sv["k_raw"], sv["gk"], dkn.reshape(H * S, hd), scale=1.0, name=n + "knorm",
                                   out_dtype=BF16)
        grads["q_norm_g"][l], grads["k_norm_g"][l] = dgq[0], dgk[0]
        dP = jnp.concatenate([dconv, dsgu, _unheads(dq_raw.reshape(H, S, hd)), _unheads(dk_raw.reshape(H, S, hd)),
                              _unheads(dvh).astype(BF16), dgates], axis=1)
        grads["w_in"][l] = _matmul_tn(sv["h"], dP, name=n + "g_w_in")
        dh = _matmul(dP, wf["w_in"][l].T, name=n + "d_h")
        dx, dg1, dxb = _rmsnorm_bwd(sv["x"], sv["g1"], dh, scale=1.0, dres=dx1, name=n + "mixnorm", bf16_copy=True)
        grads["mix_norm_g"][l] = dg1[0]
    return lpart[0, 0], dx, grads


def kernel(x, mix_norm_g, w_in, b_gate, conv_w, conv_b, sgu_ln_g, sgu_ln_b, sgu_w, sgu_b, q_norm_g, k_norm_g, w_branch_out, w_o, ffn_norm_g, w_gate_up, w_down, loss_target, m_mix_norm_g, m_w_in, m_b_gate, m_conv_w, m_conv_b, m_sgu_ln_g, m_sgu_ln_b, m_sgu_w, m_sgu_b, m_q_norm_g, m_k_norm_g, m_w_branch_out, m_w_o, m_ffn_norm_g, m_w_gate_up, m_w_down, v_mix_norm_g, v_w_in, v_b_gate, v_conv_w, v_conv_b, v_sgu_ln_g, v_sgu_ln_b, v_sgu_w, v_sgu_b, v_q_norm_g, v_k_norm_g, v_w_branch_out, v_w_o, v_ffn_norm_g, v_w_gate_up, v_w_down):
    w = dict(mix_norm_g=mix_norm_g, w_in=w_in, b_gate=b_gate, conv_w=conv_w, conv_b=conv_b, sgu_ln_g=sgu_ln_g,
             sgu_ln_b=sgu_ln_b, sgu_w=sgu_w, sgu_b=sgu_b, q_norm_g=q_norm_g, k_norm_g=k_norm_g,
             w_branch_out=w_branch_out, w_o=w_o, ffn_norm_g=ffn_norm_g, w_gate_up=w_gate_up, w_down=w_down)
    m = dict(mix_norm_g=m_mix_norm_g, w_in=m_w_in, b_gate=m_b_gate, conv_w=m_conv_w, conv_b=m_conv_b,
             sgu_ln_g=m_sgu_ln_g, sgu_ln_b=m_sgu_ln_b, sgu_w=m_sgu_w, sgu_b=m_sgu_b, q_norm_g=m_q_norm_g,
             k_norm_g=m_k_norm_g, w_branch_out=m_w_branch_out, w_o=m_w_o, ffn_norm_g=m_ffn_norm_g,
             w_gate_up=m_w_gate_up, w_down=m_w_down)
    v = dict(mix_norm_g=v_mix_norm_g, w_in=v_w_in, b_gate=v_b_gate, conv_w=v_conv_w, conv_b=v_conv_b,
             sgu_ln_g=v_sgu_ln_g, sgu_ln_b=v_sgu_ln_b, sgu_w=v_sgu_w, sgu_b=v_sgu_b, q_norm_g=v_q_norm_g,
             k_norm_g=v_k_norm_g, w_branch_out=v_w_branch_out, w_o=v_w_o, ffn_norm_g=v_ffn_norm_g,
             w_gate_up=v_w_gate_up, w_down=v_w_down)
    me = 4 * lax.axis_index("x") + 2 * lax.axis_index("y") + lax.axis_index("c")
    S = x.shape[1]

    big_shapes = [w[k].shape for k in BIG]
    gathered = _all_gather(_pack_rows([w[k] for k in BIG], BF16), name="gather_weights")
    parts = _unpack_rows(gathered, big_shapes, lead=(N_DEV,))
    wf = {k: _unshard(p, SHARD_AXIS[k]) for k, p in zip(BIG, parts)}
    conv_g = _all_gather(_pack_rows([conv_w], F32), name="gather_conv_w")
    conv_full = _unshard(_unpack_rows(conv_g, [conv_w.shape], lead=(N_DEV,))[0], 2)
    sm = {k: w[k] for k in SMALL}
    sm["conv_w"] = conv_full

    lpart, dx, grads = _local_step(x[0], loss_target[0], wf, sm)
    loss = lax.psum(lpart, ("x", "y", "c"))

    chunks = _pack_rows([_reshard(gl, SHARD_AXIS[k] - 1) for k in BIG for gl in grads[k]], BF16, lead=(N_DEV,))
    recv = _all_to_all(chunks, name="exchange_grads")
    gb, db, mb, vb = _adamw(_pack_rows([w[k] for k in BIG], F32), _pack_rows([m[k] for k in BIG], F32),
                            _pack_rows([v[k] for k in BIG], F32), recv, name="adamw_big")
    out = {}
    for nm, packed in (("grad", gb), ("delta", db), ("new_m", mb), ("new_v", vb)):
        for k, a in zip(BIG, _unpack_rows(packed, big_shapes)):
            out[nm, k] = a

    small_grads = [jnp.stack(grads[k]) for k in SMALL]
    small_shapes = [g.shape for g in small_grads]
    sg = _all_gather(_pack_rows(small_grads, F32), name="gather_small_grads")
    gsum = _slot_sum(sg, name="sum_small_grads")
    gsmall = dict(zip(SMALL, _unpack_rows(gsum, small_shapes)))
    cshard = conv_w.shape[2]
    gsmall["conv_w"] = lax.dynamic_slice_in_dim(gsmall["conv_w"], me * cshard, cshard, axis=2)
    own_shapes = [w[k].shape for k in SMALL]
    gs_, ds_, ms_, vs_ = _adamw(_pack_rows([w[k] for k in SMALL], F32), _pack_rows([m[k] for k in SMALL], F32),
                                _pack_rows([v[k] for k in SMALL], F32),
                                _pack_rows([gsmall[k] for k in SMALL], F32)[None], name="adamw_small")
    for nm, packed in (("grad", gs_), ("delta", ds_), ("new_m", ms_), ("new_v", vs_)):
        for k, a in zip(SMALL, _unpack_rows(packed, own_shapes)):
            out[nm, k] = a

    res = [loss, dx[None]]
    for nm in ("grad", "delta", "new_m", "new_v"):
        res += [out[nm, k] for k in WEIGHTS]
    return tuple(res)
```

```python
import functools
import math

import jax
import jax.numpy as jnp
from jax import lax
from jax.experimental import pallas as pl
from jax.experimental.pallas import tpu as pltpu

F32 = jnp.float32
BF16 = jnp.bfloat16
MESH = pl.DeviceIdType.MESH

N_DEV = 8
LANES = 128
VMEM_LIMIT_BYTES = 56 * 1024 * 1024
EPS = 1e-6
ADAM_LR, ADAM_B1, ADAM_B2, ADAM_EPS, ADAM_WD, ADAM_STEP = 0.001, 0.9, 0.999, 1e-08, 0.01, 10
ATTN_FWD_TILES = (512, 256)
ATTN_BWD_TILES = (1024, 256)
BIG = ("w_in", "w_branch_out", "w_o", "w_gate_up", "w_down")
SMALL = ("mix_norm_g", "b_gate", "conv_w", "conv_b", "sgu_ln_g", "sgu_ln_b", "sgu_w", "sgu_b",
         "q_norm_g", "k_norm_g", "ffn_norm_g")
WEIGHTS = ("mix_norm_g", "w_in", "b_gate", "conv_w", "conv_b", "sgu_ln_g", "sgu_ln_b", "sgu_w", "sgu_b",
           "q_norm_g", "k_norm_g", "w_branch_out", "w_o", "ffn_norm_g", "w_gate_up", "w_down")
SHARD_AXIS = {"w_in": 2, "w_branch_out": 3, "w_o": 1, "w_gate_up": 2, "w_down": 1}


def _tile(n, cap, mult):
    best = None
    for t in range(mult, min(n, cap) + 1, mult):
        if n % t == 0:
            best = t
    return best if best is not None else n


def _params(*sem):
    return pltpu.CompilerParams(dimension_semantics=sem if sem else None, vmem_limit_bytes=VMEM_LIMIT_BYTES)


def _erf(x):
    return lax.erf(x)


def _gelu(x):
    return 0.5 * x * (1.0 + _erf(x * (1.0 / math.sqrt(2.0))))


def _gelu_grad(x):
    return 0.5 * (1.0 + _erf(x * (1.0 / math.sqrt(2.0)))) + x * jnp.exp(-0.5 * x * x) * (1.0 / math.sqrt(2.0 * math.pi))


def _sigmoid(x):
    return 1.0 / (1.0 + jnp.exp(-x))


def _matmul(a, b, *, name, res=None, out_dtype=F32):
    M, K = a.shape
    _, N = b.shape
    tm, tn, tk = _tile(M, 1024, 8), _tile(N, 1536, LANES), _tile(K, 1536, LANES)
    nk = K // tk
    has_res = res is not None

    def body(*refs):
        refs = list(refs)
        acc = refs.pop() if nk > 1 else None
        if has_res:
            a_ref, b_ref, r_ref, o_ref = refs
        else:
            a_ref, b_ref, o_ref = refs
        k = pl.program_id(2)
        part = jnp.dot(a_ref[...], b_ref[...], preferred_element_type=F32)

        def finish(v):
            if has_res:
                v = v + r_ref[...]
            o_ref[...] = v.astype(out_dtype)

        if nk == 1:
            finish(part)
        else:
            @pl.when(k == 0)
            def _():
                acc[...] = part

            @pl.when(jnp.logical_and(k > 0, k < nk - 1))
            def _():
                acc[...] += part

            @pl.when(k == nk - 1)
            def _():
                finish(acc[...] + part)

    in_specs = [pl.BlockSpec((tm, tk), lambda i, j, k: (i, k)), pl.BlockSpec((tk, tn), lambda i, j, k: (k, j))]
    args = [a, b]
    if has_res:
        in_specs.append(pl.BlockSpec((tm, tn), lambda i, j, k: (i, j)))
        args.append(res)
    return pl.pallas_call(
        body, name=name, grid=(M // tm, N // tn, nk), in_specs=in_specs,
        out_specs=pl.BlockSpec((tm, tn), lambda i, j, k: (i, j)),
        out_shape=jax.ShapeDtypeStruct((M, N), out_dtype),
        scratch_shapes=[pltpu.VMEM((tm, tn), F32)] if nk > 1 else [],
        compiler_params=_params("parallel", "parallel", "arbitrary"),
    )(*args)


def _matmul_tn(x, y, *, name):
    S, A = x.shape
    _, B = y.shape
    ta, tb, ts = _tile(A, 1024, LANES), _tile(B, 1536, LANES), _tile(S, 1024, 8)
    ns = S // ts

    def body(x_ref, y_ref, o_ref):
        s = pl.program_id(2)
        part = lax.dot_general(x_ref[...], y_ref[...], (((0,), (0,)), ((), ())), preferred_element_type=F32)

        @pl.when(s == 0)
        def _():
            o_ref[...] = part

        @pl.when(s > 0)
        def _():
            o_ref[...] += part

    return pl.pallas_call(
        body, name=name, grid=(A // ta, B // tb, ns),
        in_specs=[pl.BlockSpec((ts, ta), lambda i, j, s: (s, i)), pl.BlockSpec((ts, tb), lambda i, j, s: (s, j))],
        out_specs=pl.BlockSpec((ta, tb), lambda i, j, s: (i, j)),
        out_shape=jax.ShapeDtypeStruct((A, B), F32),
        compiler_params=_params("parallel", "parallel", "arbitrary"),
    )(x, y)


def _rmsnorm_fwd(x, g, *, scale, out_dtype, name):
    R, W = x.shape
    tr = _tile(R, 512 if W >= 512 else 4096, 8)

    def body(x_ref, g_ref, o_ref):
        xv = x_ref[...]
        r = lax.rsqrt(jnp.mean(xv * xv, axis=1, keepdims=True) + EPS)
        o_ref[...] = (xv * r * (g_ref[...] * scale)).astype(out_dtype)

    return pl.pallas_call(
        body, name=name, grid=(R // tr,),
        in_specs=[pl.BlockSpec((tr, W), lambda i: (i, 0)), pl.BlockSpec((1, W), lambda i: (0, 0))],
        out_specs=pl.BlockSpec((tr, W), lambda i: (i, 0)),
        out_shape=jax.ShapeDtypeStruct((R, W), out_dtype),
        compiler_params=_params("parallel"),
    )(x, g)


def _rmsnorm_bwd(x, g, dy, *, scale, name, dres=None, out_dtype=F32, bf16_copy=False):
    R, W = x.shape
    tr = _tile(R, 512 if W >= 512 else 4096, 8)
    has_res = dres is not None

    def body(*refs):
        refs = list(refs)
        dxb_ref = refs.pop() if bf16_copy else None
        if has_res:
            x_ref, g_ref, dy_ref, dres_ref, dx_ref, dg_ref = refs
        else:
            x_ref, g_ref, dy_ref, dx_ref, dg_ref = refs
        i = pl.program_id(0)
        xv = x_ref[...]
        dyv = dy_ref[...].astype(F32) * scale
        r = lax.rsqrt(jnp.mean(xv * xv, axis=1, keepdims=True) + EPS)
        u = dyv * g_ref[...]
        dx = r * u - xv * (r * r * r * jnp.mean(u * xv, axis=1, keepdims=True))
        if has_res:
            dx = dx + dres_ref[...]
        dx_ref[...] = dx.astype(out_dtype)
        if bf16_copy:
            dxb_ref[...] = dx.astype(BF16)
        part = jnp.sum(dyv * xv * r, axis=0, keepdims=True)

        @pl.when(i == 0)
        def _():
            dg_ref[...] = part

        @pl.when(i > 0)
        def _():
            dg_ref[...] += part

    row = pl.BlockSpec((tr, W), lambda i: (i, 0))
    one = pl.BlockSpec((1, W), lambda i: (0, 0))
    in_specs = [row, one, row] + ([row] if has_res else [])
    args = [x, g, dy] + ([dres] if has_res else [])
    extra = bool(bf16_copy)
    return pl.pallas_call(
        body, name=name, grid=(R // tr,), in_specs=in_specs, out_specs=[row, one] + [row] * extra,
        out_shape=[jax.ShapeDtypeStruct((R, W), out_dtype), jax.ShapeDtypeStruct((1, W), F32)]
        + [jax.ShapeDtypeStruct((R, W), BF16)] * extra,
        compiler_params=_params("arbitrary"),
    )(*args)


def _shift_down(u, prev, n):
    ts = u.shape[0]
    out = pltpu.roll(u, n, 0)
    row = lax.broadcasted_iota(jnp.int32, u.shape, 0)
    for r in range(n):
        out = jnp.where(row == r, prev[8 - n + r:8 - n + r + 1, :], out)
    return out


def _shift_up(u, nxt, n):
    ts = u.shape[0]
    out = pltpu.roll(u, ts - n, 0)
    row = lax.broadcasted_iota(jnp.int32, u.shape, 0)
    for r in range(n):
        out = jnp.where(row == ts - n + r, nxt[r:r + 1, :], out)
    return out


def _conv_fwd(P, conv_w, conv_b, *, C, name):
    S = P.shape[0]
    ts = _tile(S, 512, 8)
    hb = ts // 8

    def body(ab_ref, ac_ref, ax_ref, pc_ref, px_ref, w_ref, b_ref, o_ref):
        i = pl.program_id(0)
        u = ac_ref[...] * ax_ref[...]
        prev = pc_ref[...] * px_ref[...] * (i > 0).astype(F32)
        w = w_ref[...]
        y = b_ref[...] + w[0:1, :] * _shift_down(u, prev, 2) + w[1:2, :] * _shift_down(u, prev, 1) + w[2:3, :] * u
        o_ref[...] = (ab_ref[...] * y).astype(BF16)

    cur = lambda c: pl.BlockSpec((ts, C), lambda i: (i, c))
    prv = lambda c: pl.BlockSpec((8, C), lambda i: (jnp.maximum(i * hb - 1, 0), c))
    return pl.pallas_call(
        body, name=name, grid=(S // ts,),
        in_specs=[cur(0), cur(1), cur(2), prv(1), prv(2),
                  pl.BlockSpec((8, C), lambda i: (0, 0)), pl.BlockSpec((1, C), lambda i: (0, 0))],
        out_specs=pl.BlockSpec((ts, C), lambda i: (i, 0)),
        out_shape=jax.ShapeDtypeStruct((S, C), BF16),
        compiler_params=_params("parallel"),
    )(P, P, P, P, P, conv_w, conv_b)


def _conv_bwd(P, dya, conv_w, conv_b, *, C, name):
    S = P.shape[0]
    ts = _tile(S, 512, 8)
    hb = ts // 8
    last = S // 8 - 1
    n = S // ts

    def body(ab_ref, ac_ref, ax_ref, pc_ref, px_ref, dy_ref, nab_ref, ndy_ref, w_ref, b_ref, o_ref, dw_ref):
        i = pl.program_id(0)
        ab, ac, ax = ab_ref[...], ac_ref[...], ax_ref[...]
        u = ac * ax
        prev = pc_ref[...] * px_ref[...] * (i > 0).astype(F32)
        w = w_ref[...]
        u1, u2 = _shift_down(u, prev, 1), _shift_down(u, prev, 2)
        y = b_ref[...] + w[0:1, :] * u2 + w[1:2, :] * u1 + w[2:3, :] * u
        dya_v = dy_ref[...]
        dyp = dya_v * ab
        nxt = ndy_ref[...] * nab_ref[...] * (i < n - 1).astype(F32)
        du = w[2:3, :] * dyp + w[1:2, :] * _shift_up(dyp, nxt, 1) + w[0:1, :] * _shift_up(dyp, nxt, 2)
        o_ref[:, 0:C] = (dya_v * y).astype(BF16)
        o_ref[:, C:2 * C] = (du * ax).astype(BF16)
        o_ref[:, 2 * C:3 * C] = (du * ac).astype(BF16)
        part = jnp.concatenate([
            jnp.sum(dyp * u2, axis=0, keepdims=True), jnp.sum(dyp * u1, axis=0, keepdims=True),
            jnp.sum(dyp * u, axis=0, keepdims=True), jnp.sum(dyp, axis=0, keepdims=True),
            jnp.zeros((4, C), F32)], axis=0)

        @pl.when(i == 0)
        def _():
            dw_ref[...] = part

        @pl.when(i > 0)
        def _():
            dw_ref[...] += part

    cur = lambda c: pl.BlockSpec((ts, C), lambda i: (i, c))
    prv = lambda c: pl.BlockSpec((8, C), lambda i: (jnp.maximum(i * hb - 1, 0), c))
    nxt = lambda c: pl.BlockSpec((8, C), lambda i: (jnp.minimum((i + 1) * hb, last), c))
    return pl.pallas_call(
        body, name=name, grid=(n,),
        in_specs=[cur(0), cur(1), cur(2), prv(1), prv(2), cur(0), nxt(0), nxt(0),
                  pl.BlockSpec((8, C), lambda i: (0, 0)), pl.BlockSpec((1, C), lambda i: (0, 0))],
        out_specs=[pl.BlockSpec((ts, 3 * C), lambda i: (i, 0)), pl.BlockSpec((8, C), lambda i: (0, 0))],
        out_shape=[jax.ShapeDtypeStruct((S, 3 * C), BF16), jax.ShapeDtypeStruct((8, C), F32)],
        compiler_params=_params("arbitrary"),
    )(P, P, P, P, P, dya, P, dya, conv_w, conv_b)


def _sgu_fwd(P, ln_g, ln_b, wm, bT, *, W, cu, cv, name):
    S = P.shape[0]
    G, CH, _ = wm.shape
    gw = W // G
    ts = _tile(S, 512, CH)

    def body(u_ref, v_ref, g_ref, b_ref, wm_ref, bT_ref, o_ref):
        gv = _gelu(v_ref[...])
        mu = jnp.mean(gv, axis=1, keepdims=True)
        xc = gv - mu
        vn = (xc * lax.rsqrt(jnp.mean(xc * xc, axis=1, keepdims=True) + EPS) * g_ref[...] + b_ref[...]).astype(BF16)
        bT_v = bT_ref[...]
        for c in range(ts // CH):
            rows = slice(c * CH, (c + 1) * CH)
            for g in range(G):
                cols = slice(g * gw, (g + 1) * gw)
                mixed = jnp.dot(wm_ref[g], vn[rows, cols], preferred_element_type=F32) + bT_v[:, g:g + 1]
                o_ref[rows, cols] = (_gelu(u_ref[rows, cols]) * mixed).astype(BF16)

    full = lambda shp: pl.BlockSpec(shp, lambda i: (0,) * len(shp))
    return pl.pallas_call(
        body, name=name, grid=(S // ts,),
        in_specs=[pl.BlockSpec((ts, W), lambda i: (i, cu)), pl.BlockSpec((ts, W), lambda i: (i, cv)),
                  full((1, W)), full((1, W)), full((G, CH, CH)), full((CH, G))],
        out_specs=pl.BlockSpec((ts, W), lambda i: (i, 0)),
        out_shape=jax.ShapeDtypeStruct((S, W), BF16),
        compiler_params=_params("parallel"),
    )(P, P, ln_g, ln_b, wm, bT)


def _sgu_bwd(P, dyb, ln_g, ln_b, wm, wmT, bT, *, W, cu, cv, name):
    S = P.shape[0]
    G, CH, _ = wm.shape
    gw = W // G
    ts = _tile(S, 512, CH)

    def body(u_ref, v_ref, dy_ref, g_ref, b_ref, wm_ref, wmT_ref, bT_ref, o_ref, dw_ref, db_ref, dln_ref, dvn_ref):
        i = pl.program_id(0)

        @pl.when(i == 0)
        def _():
            dw_ref[...] = jnp.zeros_like(dw_ref)
            db_ref[...] = jnp.zeros_like(db_ref)
            dln_ref[...] = jnp.zeros_like(dln_ref)

        sv = v_ref[...]
        gv = _gelu(sv)
        mu = jnp.mean(gv, axis=1, keepdims=True)
        xc = gv - mu
        rstd = lax.rsqrt(jnp.mean(xc * xc, axis=1, keepdims=True) + EPS)
        xhat = xc * rstd
        lg = g_ref[...]
        vn = (xhat * lg + b_ref[...]).astype(BF16)
        bT_v = bT_ref[...]
        for c in range(ts // CH):
            rows = slice(c * CH, (c + 1) * CH)
            for g in range(G):
                cols = slice(g * gw, (g + 1) * gw)
                vn_cg = vn[rows, cols]
                mixed = jnp.dot(wm_ref[g], vn_cg, preferred_element_type=F32) + bT_v[:, g:g + 1]
                su = u_ref[rows, cols]
                dyv = dy_ref[rows, cols]
                dmix = dyv * _gelu(su)
                o_ref[rows, cols] = (dyv * mixed * _gelu_grad(su)).astype(BF16)
                dmix_b = dmix.astype(BF16)
                dw_ref[g] += lax.dot_general(dmix_b, vn_cg, (((1,), (1,)), ((), ())), preferred_element_type=F32)
                db_ref[g] += jnp.broadcast_to(jnp.sum(dmix, axis=1, keepdims=True), (CH, CH))
                dvn_ref[rows, cols] = jnp.dot(wmT_ref[g], dmix_b, preferred_element_type=F32)
        dvn = dvn_ref[...]
        dxh = dvn * lg
        dgv = rstd * (dxh - jnp.mean(dxh, axis=1, keepdims=True) - xhat * jnp.mean(dxh * xhat, axis=1, keepdims=True))
        o_ref[:, W:2 * W] = (dgv * _gelu_grad(sv)).astype(BF16)
        dln_ref[0:1, :] += jnp.sum(dvn * xhat, axis=0, keepdims=True)
        dln_ref[1:2, :] += jnp.sum(dvn, axis=0, keepdims=True)

    full = lambda shp: pl.BlockSpec(shp, lambda i: (0,) * len(shp))
    return pl.pallas_call(
        body, name=name, grid=(S // ts,),
        in_specs=[pl.BlockSpec((ts, W), lambda i: (i, cu)), pl.BlockSpec((ts, W), lambda i: (i, cv)),
                  pl.BlockSpec((ts, W), lambda i: (i, 0)),
                  full((1, W)), full((1, W)), full((G, CH, CH)), full((G, CH, CH)), full((CH, G))],
        out_specs=[pl.BlockSpec((ts, 2 * W), lambda i: (i, 0)), full((G, CH, CH)), full((G, CH, CH)), full((8, W))],
        out_shape=[jax.ShapeDtypeStruct((S, 2 * W), BF16), jax.ShapeDtypeStruct((G, CH, CH), F32),
                   jax.ShapeDtypeStruct((G, CH, CH), F32), jax.ShapeDtypeStruct((8, W), F32)],
        scratch_shapes=[pltpu.VMEM((ts, W), F32)],
        compiler_params=_params("arbitrary"),
    )(P, P, dyb, ln_g, ln_b, wm, wmT, bT)


def _block_sums(x, u, parts=1):
    hi = x.astype(BF16)
    out = jnp.dot(hi, u, preferred_element_type=F32)
    if parts == 2:
        lo = (x - hi.astype(F32)).astype(BF16)
        out = out + jnp.dot(lo, u, preferred_element_type=F32)
    return out


_NT = (((1,), (1,)), ((), ()))
_TN = (((0,), (0,)), ((), ()))


def _diag_step(step, j, carry, row0):
    if row0 == 0:
        return step(j, carry, True)
    tail = step(j, tuple(c[row0:] for c in carry), True, row0)
    return tuple(jnp.concatenate([c[:row0], t], axis=0) for c, t in zip(carry, tail))


def _attn_fwd(q, k, v, umat, *, tq, tk, name):
    H, S, hd = q.shape

    def body(q_ref, k_ref, v_ref, u_ref, o_ref):
        i = pl.program_id(1)
        qb = q_ref[0]
        um = u_ref[...]
        qpos = lax.broadcasted_iota(jnp.int32, (tq, tk), 0) + i * tq
        kloc = lax.broadcasted_iota(jnp.int32, (tq, tk), 1)

        def step(j, carry, masked, row0=0):
            r, acc = carry
            ks = pl.multiple_of(j * tk, tk)
            kb = k_ref[0, pl.ds(ks, tk), :]
            vb = v_ref[0, pl.ds(ks, tk), :]
            z = lax.dot_general(qb[row0:], kb, _NT, preferred_element_type=F32)
            lb = jnp.minimum(z, 0.0) - jnp.log(1.0 + jnp.exp(-jnp.abs(z)))
            lm = lb - z
            if masked:
                m = (kloc[row0:] + j * tk) < qpos[row0:]
                lm = jnp.where(m, lm, 0.0)
            a = jnp.exp(lb + _block_sums(lm, um) + r)
            if masked:
                a = jnp.where(m, a, 0.0)
            acc = acc + jnp.dot(a.astype(BF16), vb, preferred_element_type=F32)
            return r + jnp.sum(lm, axis=1, keepdims=True), acc

        jd = (i * tq) // tk
        carry = (jnp.zeros((tq, 1), F32), jnp.zeros((tq, hd), F32))
        for dd in reversed(range(max(1, tq // tk))):
            carry = _diag_step(step, jd + dd, carry, dd * tk if tq > tk else 0)
        unroll = 2 if (tq // tk) % 2 == 0 else 1

        def trip(t, c):
            for s in range(unroll):
                c = step(jd - 1 - s - unroll * t, c, False)
            return c

        carry = lax.fori_loop(0, jd // unroll, trip, carry)
        o_ref[0] = carry[1]

    blk = pl.BlockSpec((1, tq, hd), lambda h, i: (h, i, 0))
    whole = pl.BlockSpec((1, S, hd), lambda h, i: (h, 0, 0))
    return pl.pallas_call(
        body, name=name, grid=(H, S // tq),
        in_specs=[blk, whole, whole, pl.BlockSpec((tk, tk), lambda h, i: (0, 0))],
        out_specs=blk, out_shape=jax.ShapeDtypeStruct((H, S, hd), F32),
        compiler_params=_params("parallel", "arbitrary"),
    )(q, k, v, umat)


def _attn_bwd(q, k, v, o, do, umat, *, tq, tk, name):
    H, S, hd = q.shape

    def body(q_ref, k_ref, v_ref, o_ref, do_ref, u_ref, dq_ref, dk_ref, dv_ref):
        i = pl.program_id(1)

        @pl.when(i == 0)
        def _():
            dk_ref[...] = jnp.zeros_like(dk_ref)
            dv_ref[...] = jnp.zeros_like(dv_ref)

        qb = q_ref[0]
        do32 = do_ref[0]
        dob = do32.astype(BF16)
        tot = jnp.sum(dob.astype(F32) * o_ref[0], axis=1, keepdims=True)
        um = u_ref[...]
        qpos = lax.broadcasted_iota(jnp.int32, (tq, tk), 0) + i * tq
        kloc = lax.broadcasted_iota(jnp.int32, (tq, tk), 1)

        def step(j, carry, masked, row0=0):
            r, gs, dq = carry
            ks = pl.multiple_of(j * tk, tk)
            kb = k_ref[0, pl.ds(ks, tk), :]
            vb = v_ref[0, pl.ds(ks, tk), :]
            qs, dos = qb[row0:], dob[row0:]
            z = lax.dot_general(qs, kb, _NT, preferred_element_type=F32)
            lb = jnp.minimum(z, 0.0) - jnp.log(1.0 + jnp.exp(-jnp.abs(z)))
            sig = jnp.exp(lb)
            lm = lb - z
            if masked:
                m = (kloc[row0:] + j * tk) < qpos[row0:]
                lm = jnp.where(m, lm, 0.0)
            a = jnp.exp(lb + _block_sums(lm, um) + r)
            if masked:
                a = jnp.where(m, a, 0.0)
            ab = a.astype(BF16)
            g = lax.dot_general(dos, vb, _NT, preferred_element_type=F32) * ab.astype(F32)
            dz = g - sig * ((tot[row0:] - gs) - _block_sums(g, um, parts=2))
            if masked:
                dz = jnp.where(m, dz, 0.0)
            dzb = dz.astype(BF16)
            dq = dq + jnp.dot(dzb, kb, preferred_element_type=F32)
            dk_ref[0, pl.ds(ks, tk), :] += lax.dot_general(dzb, qs, _TN, preferred_element_type=F32)
            dv_ref[0, pl.ds(ks, tk), :] += lax.dot_general(ab, dos, _TN, preferred_element_type=F32)
            return r + jnp.sum(lm, axis=1, keepdims=True), gs + jnp.sum(g, axis=1, keepdims=True), dq

        jd = (i * tq) // tk
        zero = jnp.zeros((tq, 1), F32)
        carry = (zero, zero, jnp.zeros((tq, hd), F32))
        for dd in reversed(range(max(1, tq // tk))):
            carry = _diag_step(step, jd + dd, carry, dd * tk if tq > tk else 0)
        unroll = 2 if (tq // tk) % 2 == 0 else 1

        def trip(t, c):
            for s in range(unroll):
                c = step(jd - 1 - s - unroll * t, c, False)
            return c

        carry = lax.fori_loop(0, jd // unroll, trip, carry)
        dq_ref[0] = carry[2]

    blk = pl.BlockSpec((1, tq, hd), lambda h, i: (h, i, 0))
    whole = pl.BlockSpec((1, S, hd), lambda h, i: (h, 0, 0))
    shp = jax.ShapeDtypeStruct((H, S, hd), F32)
    return pl.pallas_call(
        body, name=name, grid=(H, S // tq),
        in_specs=[blk, whole, whole, blk, blk, pl.BlockSpec((tk, tk), lambda h, i: (0, 0))],
        out_specs=[blk, whole, whole], out_shape=[shp, shp, shp],
        compiler_params=_params("parallel", "arbitrary"),
    )(q, k, v, o, do, umat)


def _merge_fwd(ys, wb, P, b_gate, *, gate_col0, name):
    S, C = ys[0].shape
    D = wb.shape[2]
    ts = _tile(S, 512, 8)

    def body(y0, y1, y2, wb_ref, g0, g1, g2, bg_ref, o_ref):
        acc = jnp.zeros((ts, D), F32)
        for n, (y_ref, g_ref) in enumerate(((y0, g0), (y1, g1), (y2, g2))):
            yd = jnp.dot(y_ref[...], wb_ref[n], preferred_element_type=F32)
            acc = acc + _sigmoid(g_ref[...] + bg_ref[:, n * D:(n + 1) * D]) * yd
        o_ref[...] = acc.astype(BF16)

    ysp = pl.BlockSpec((ts, C), lambda i: (i, 0))
    gsp = lambda n: pl.BlockSpec((ts, D), lambda i: (i, gate_col0 + n))
    return pl.pallas_call(
        body, name=name, grid=(S // ts,),
        in_specs=[ysp, ysp, ysp, pl.BlockSpec((3, C, D), lambda i: (0, 0, 0)), gsp(0), gsp(1), gsp(2),
                  pl.BlockSpec((1, 3 * D), lambda i: (0, 0))],
        out_specs=pl.BlockSpec((ts, D), lambda i: (i, 0)),
        out_shape=jax.ShapeDtypeStruct((S, D), BF16),
        compiler_params=_params("parallel"),
    )(*ys, wb, P, P, P, b_gate)


def _merge_bwd(ys, wb, wbT, P, b_gate, dmerged, *, gate_col0, name):
    S, C = ys[0].shape
    D = wb.shape[2]
    ts = _tile(S, 256, 8)

    def body(y0, y1, y2, wb_ref, wbT_ref, g0, g1, g2, bg_ref, dm_ref, dg_ref, dyd_ref, dy_ref, dbg_ref):
        i = pl.program_id(0)
        dm = dm_ref[...]
        parts = []
        for n, (y_ref, g_ref) in enumerate(((y0, g0), (y1, g1), (y2, g2))):
            yd = jnp.dot(y_ref[...], wb_ref[n], preferred_element_type=F32)
            sg = _sigmoid(g_ref[...] + bg_ref[:, n * D:(n + 1) * D])
            dgate = dm * yd * sg * (1.0 - sg)
            dg_ref[:, n * D:(n + 1) * D] = dgate.astype(BF16)
            parts.append(jnp.sum(dgate, axis=0, keepdims=True))
            dyd = (dm * sg).astype(BF16)
            dyd_ref[n] = dyd
            dy_ref[n] = jnp.dot(dyd, wbT_ref[n], preferred_element_type=F32)
        part = jnp.concatenate(parts, axis=1)

        @pl.when(i == 0)
        def _():
            dbg_ref[...] = part

        @pl.when(i > 0)
        def _():
            dbg_ref[...] += part

    ysp = pl.BlockSpec((ts, C), lambda i: (i, 0))
    gsp = lambda n: pl.BlockSpec((ts, D), lambda i: (i, gate_col0 + n))
    return pl.pallas_call(
        body, name=name, grid=(S // ts,),
        in_specs=[ysp, ysp, ysp, pl.BlockSpec((3, C, D), lambda i: (0, 0, 0)),
                  pl.BlockSpec((3, D, C), lambda i: (0, 0, 0)), gsp(0), gsp(1), gsp(2),
                  pl.BlockSpec((1, 3 * D), lambda i: (0, 0)), pl.BlockSpec((ts, D), lambda i: (i, 0))],
        out_specs=[pl.BlockSpec((ts, 3 * D), lambda i: (i, 0)), pl.BlockSpec((3, ts, D), lambda i: (0, i, 0)),
                   pl.BlockSpec((3, ts, C), lambda i: (0, i, 0)), pl.BlockSpec((1, 3 * D), lambda i: (0, 0))],
        out_shape=[jax.ShapeDtypeStruct((S, 3 * D), BF16), jax.ShapeDtypeStruct((3, S, D), BF16),
                   jax.ShapeDtypeStruct((3, S, C), F32), jax.ShapeDtypeStruct((1, 3 * D), F32)],
        compiler_params=_params("arbitrary"),
    )(*ys, wb, wbT, P, P, P, b_gate, dmerged)


def _swiglu_fwd(gu, *, name):
    S, F2 = gu.shape
    F = F2 // 2
    ts, tf = _tile(S, 512, 8), _tile(F, 1536, LANES)
    nf = F // tf

    def body(g_ref, u_ref, o_ref):
        gt = g_ref[...]
        o_ref[...] = (gt * _sigmoid(gt) * u_ref[...]).astype(BF16)

    return pl.pallas_call(
        body, name=name, grid=(S // ts, nf),
        in_specs=[pl.BlockSpec((ts, tf), lambda i, j: (i, j)), pl.BlockSpec((ts, tf), lambda i, j: (i, j + nf))],
        out_specs=pl.BlockSpec((ts, tf), lambda i, j: (i, j)),
        out_shape=jax.ShapeDtypeStruct((S, F), BF16),
        compiler_params=_params("parallel", "parallel"),
    )(gu, gu)


def _swiglu_bwd(gu, dact, *, name):
    S, F2 = gu.shape
    F = F2 // 2
    ts = _tile(S, 256, 8)

    def body(gu_ref, d_ref, o_ref):
        gt, up, da = gu_ref[:, 0:F], gu_ref[:, F:F2], d_ref[...]
        sg = _sigmoid(gt)
        o_ref[:, 0:F] = (da * up * sg * (1.0 + gt * (1.0 - sg))).astype(BF16)
        o_ref[:, F:F2] = (da * gt * sg).astype(BF16)

    return pl.pallas_call(
        body, name=name, grid=(S // ts,),
        in_specs=[pl.BlockSpec((ts, F2), lambda i: (i, 0)), pl.BlockSpec((ts, F), lambda i: (i, 0))],
        out_specs=pl.BlockSpec((ts, F2), lambda i: (i, 0)),
        out_shape=jax.ShapeDtypeStruct((S, F2), BF16),
        compiler_params=_params("parallel"),
    )(gu, dact)


def _loss_grad(y, target, *, name):
    S, D = y.shape
    ts = _tile(S, 512, 8)

    def body(y_ref, t_ref, dy_ref, l_ref, dyb_ref):
        i = pl.program_id(0)
        err = y_ref[...] - t_ref[...]
        dy_ref[...] = err * (1.0 / D)
        dyb_ref[...] = (err * (1.0 / D)).astype(BF16)
        part = jnp.broadcast_to(jnp.sum(jnp.sum(err * err, axis=1, keepdims=True), axis=0, keepdims=True) * (0.5 / D),
                                (1, LANES))

        @pl.when(i == 0)
        def _():
            l_ref[...] = part

        @pl.when(i > 0)
        def _():
            l_ref[...] += part

    row = pl.BlockSpec((ts, D), lambda i: (i, 0))
    return pl.pallas_call(
        body, name=name, grid=(S // ts,), in_specs=[row, row],
        out_specs=[row, pl.BlockSpec((1, LANES), lambda i: (0, 0)), row],
        out_shape=[jax.ShapeDtypeStruct((S, D), F32), jax.ShapeDtypeStruct((1, LANES), F32),
                   jax.ShapeDtypeStruct((S, D), BF16)],
        compiler_params=_params("arbitrary"),
    )(y, target)


def _adamw(w, m, v, gs, *, name):
    R = w.shape[0]
    ns = gs.shape[0]
    tr = _tile(R, 2048, 16)
    c1 = 1.0 / (1.0 - ADAM_B1 ** ADAM_STEP)
    c2 = 1.0 / (1.0 - ADAM_B2 ** ADAM_STEP)

    def body(w_ref, m_ref, v_ref, gs_ref, g_ref, d_ref, nm_ref, nv_ref):
        g = gs_ref[0].astype(F32)
        for s in range(1, ns):
            g = g + gs_ref[s].astype(F32)
        nm = ADAM_B1 * m_ref[...] + (1.0 - ADAM_B1) * g
        nv = ADAM_B2 * v_ref[...] + (1.0 - ADAM_B2) * (g * g)
        g_ref[...] = g
        nm_ref[...] = nm
        nv_ref[...] = nv
        d_ref[...] = -ADAM_LR * ((nm * c1) / (jnp.sqrt(nv * c2) + ADAM_EPS) + ADAM_WD * w_ref[...])

    row = pl.BlockSpec((tr, LANES), lambda i: (i, 0))
    shp = jax.ShapeDtypeStruct((R, LANES), F32)
    return pl.pallas_call(
        body, name=name, grid=(R // tr,),
        in_specs=[row, row, row, pl.BlockSpec((ns, tr, LANES), lambda i: (0, i, 0))],
        out_specs=[row, row, row, row], out_shape=[shp, shp, shp, shp],
        compiler_params=_params("parallel"),
    )(w, m, v, gs)


def _slot_sum(gs, *, name):
    ns, R, _ = gs.shape
    tr = _tile(R, 2048, 16)

    def body(gs_ref, o_ref):
        g = gs_ref[0]
        for s in range(1, ns):
            g = g + gs_ref[s]
        o_ref[...] = g

    return pl.pallas_call(
        body, name=name, grid=(R // tr,),
        in_specs=[pl.BlockSpec((ns, tr, LANES), lambda i: (0, i, 0))],
        out_specs=pl.BlockSpec((tr, LANES), lambda i: (i, 0)),
        out_shape=jax.ShapeDtypeStruct((R, LANES), F32),
        compiler_params=_params("parallel"),
    )(gs)


def _all_gather(xs, *, name):
    R, L = xs.shape

    def body(x_ref, out_ref, send_sems, recv_sems, local_sem):
        x, y, c = lax.axis_index("x"), lax.axis_index("y"), lax.axis_index("c")
        me, sibling = (x, y, c), (x, y, 1 - c)
        chips = [(1 - x, y), (x, 1 - y), (1 - x, 1 - y)]

        def slot(px, py, pc):
            return out_ref.at[4 * px + 2 * py + pc]

        def copy(k, block, to, src=None):
            return pltpu.make_async_remote_copy(
                src_ref=slot(*block) if src is None else src, dst_ref=slot(*block),
                send_sem=send_sems.at[k], recv_sem=recv_sems.at[k], device_id=to, device_id_type=MESH)

        mine = pltpu.make_async_copy(x_ref, slot(*me), local_sem)
        mine.start()
        first = [copy(0, me, sibling, src=x_ref)]
        first += [copy(1 + j, me, (*chip, c), src=x_ref) for j, chip in enumerate(chips)]
        for cp in first:
            cp.start()
        passed = [copy(4 + j, (*chip, c), sibling) for j, chip in enumerate(chips)]
        for j, chip in enumerate(chips):
            copy(1 + j, (*chip, c), me).wait_recv()
            passed[j].start()
        copy(0, sibling, me).wait_recv()
        for j, chip in enumerate(chips):
            copy(4 + j, (*chip, 1 - c), me).wait_recv()
        for cp in first + passed:
            cp.wait_send()
        mine.wait()

    return pl.pallas_call(
        body, name=name,
        in_specs=[pl.BlockSpec(memory_space=pl.ANY)], out_specs=pl.BlockSpec(memory_space=pl.ANY),
        out_shape=jax.ShapeDtypeStruct((N_DEV, R, L), xs.dtype),
        scratch_shapes=[pltpu.SemaphoreType.DMA((7,)), pltpu.SemaphoreType.DMA((7,)), pltpu.SemaphoreType.DMA],
        compiler_params=pltpu.CompilerParams(has_side_effects=True),
    )(xs)


def _all_to_all(xs, *, name):
    _, R, L = xs.shape

    def body(x_ref, out_ref, send_sems, recv_sems, local_sem):
        x, y, c = lax.axis_index("x"), lax.axis_index("y"), lax.axis_index("c")
        me = 4 * x + 2 * y + c
        mine = pltpu.make_async_copy(x_ref.at[me], out_ref.at[me], local_sem)
        mine.start()
        copies = []
        for k in range(1, N_DEV):
            px = 1 - x if k & 4 else x
            py = 1 - y if k & 2 else y
            pc = 1 - c if k & 1 else c
            peer = 4 * px + 2 * py + pc
            copies.append((
                pltpu.make_async_remote_copy(
                    src_ref=x_ref.at[peer], dst_ref=out_ref.at[me], send_sem=send_sems.at[k - 1],
                    recv_sem=recv_sems.at[k - 1], device_id=(px, py, pc), device_id_type=MESH),
                pltpu.make_async_remote_copy(
                    src_ref=x_ref.at[me], dst_ref=out_ref.at[peer], send_sem=send_sems.at[k - 1],
                    recv_sem=recv_sems.at[k - 1], device_id=(px, py, pc), device_id_type=MESH)))
        for send, _ in copies:
            send.start()
        for _, recv in copies:
            recv.wait_recv()
        for send, _ in copies:
            send.wait_send()
        mine.wait()

    return pl.pallas_call(
        body, name=name,
        in_specs=[pl.BlockSpec(memory_space=pl.ANY)], out_specs=pl.BlockSpec(memory_space=pl.ANY),
        out_shape=jax.ShapeDtypeStruct(xs.shape, xs.dtype),
        scratch_shapes=[pltpu.SemaphoreType.DMA((7,)), pltpu.SemaphoreType.DMA((7,)), pltpu.SemaphoreType.DMA],
        compiler_params=pltpu.CompilerParams(has_side_effects=True),
    )(xs)


PACK_ROWS = 16


def _pack_rows(parts, dtype, lead=()):
    rows = []
    for p in parts:
        r = p.reshape(lead + (-1, LANES)).astype(dtype)
        pad = (-r.shape[-2]) % PACK_ROWS
        rows.append(jnp.pad(r, [(0, 0)] * len(lead) + [(0, pad), (0, 0)]) if pad else r)
    return jnp.concatenate(rows, axis=len(lead))


def _unpack_rows(packed, shapes, lead=()):
    out, off = [], 0
    for shp in shapes:
        n = math.prod(shp) // LANES
        out.append(lax.slice_in_dim(packed, off, off + n, axis=len(lead)).reshape(lead + tuple(shp)))
        off += n + (-n) % PACK_ROWS
    return out


def _unshard(gathered, axis):
    g = jnp.moveaxis(gathered, 0, axis)
    shp = list(g.shape)
    shp[axis:axis + 2] = [shp[axis] * shp[axis + 1]]
    return g.reshape(shp)


def _reshard(full, axis):
    shp = list(full.shape)
    shp[axis:axis + 1] = [N_DEV, shp[axis] // N_DEV]
    return jnp.moveaxis(full.reshape(shp), axis, 0)


def _heads(a, H):
    S = a.shape[0]
    return a.reshape(S, H, -1).transpose(1, 0, 2)


def _unheads(a):
    H, S, hd = a.shape
    return a.transpose(1, 0, 2).reshape(S, H * hd)


def _local_step(x, target, wf, sm):
    S, D = x.shape
    L = wf["w_in"].shape[0]
    C = sm["conv_b"].shape[1]
    W = sm["sgu_ln_g"].shape[1]
    hd = sm["q_norm_g"].shape[1]
    G, CH = sm["sgu_w"].shape[1], sm["sgu_w"].shape[2]
    F = wf["w_down"].shape[1]
    A = wf["w_in"].shape[2] - (3 * C + 2 * W + 3 * D)
    A = A // 3
    H = A // hd
    col_q = 3 * C + 2 * W
    qscale = 1.0 / math.sqrt(hd)
    tril = jnp.tril(jnp.ones((CH, CH), F32))
    (ftq, ftk), (btq, btk) = [(_tile(S, a, LANES), _tile(S, b, LANES)) for a, b in (ATTN_FWD_TILES, ATTN_BWD_TILES)]
    umat = lambda t: (lax.broadcasted_iota(jnp.int32, (t, t), 0) > lax.broadcasted_iota(jnp.int32, (t, t), 1)).astype(BF16)

    saved = []
    for l in range(L):
        n = f"l{l}_"
        g1 = sm["mix_norm_g"][l][None]
        h = _rmsnorm_fwd(x, g1, scale=1.0, out_dtype=BF16, name=n + "mixnorm")
        P = _matmul(h, wf["w_in"][l], name=n + "w_in")
        cw = jnp.pad(sm["conv_w"][l], ((0, 5), (0, 0)))
        cb = sm["conv_b"][l][None]
        ya = _conv_fwd(P, cw, cb, C=C, name=n + "conv")
        wm = (sm["sgu_w"][l] * tril).astype(BF16)
        bT = sm["sgu_b"][l].T
        lng, lnb = sm["sgu_ln_g"][l][None], sm["sgu_ln_b"][l][None]
        yb = _sgu_fwd(P, lng, lnb, wm, bT, W=W, cu=(3 * C) // W, cv=(3 * C) // W + 1, name=n + "sgu")
        qkv = P[:, col_q:col_q + 3 * A].reshape(S, 3, H, hd).transpose(1, 2, 0, 3)
        q_raw, k_raw = qkv[0].reshape(H * S, hd), qkv[1].reshape(H * S, hd)
        gq, gk = sm["q_norm_g"][l][None], sm["k_norm_g"][l][None]
        qn = _rmsnorm_fwd(q_raw, gq, scale=qscale, out_dtype=BF16, name=n + "qnorm").reshape(H, S, hd)
        kn = _rmsnorm_fwd(k_raw, gk, scale=1.0, out_dtype=BF16, name=n + "knorm").reshape(H, S, hd)
        vh = qkv[2].astype(BF16)
        o = _attn_fwd(qn, kn, vh, umat(ftk), tq=ftq, tk=ftk, name=n + "attn")
        yc = _unheads(o).astype(BF16)
        bg = sm["b_gate"][l][None]
        gate_col0 = (col_q + 3 * A) // D
        merged = _merge_fwd((ya, yb, yc), wf["w_branch_out"][l], P, bg, gate_col0=gate_col0, name=n + "merge")
        x1 = _matmul(merged, wf["w_o"][l], res=x, name=n + "w_o")
        g2 = sm["ffn_norm_g"][l][None]
        h2 = _rmsnorm_fwd(x1, g2, scale=1.0, out_dtype=BF16, name=n + "ffnnorm")
        gu = _matmul(h2, wf["w_gate_up"][l], name=n + "w_gate_up")
        act = _swiglu_fwd(gu, name=n + "swiglu")
        x2 = _matmul(act, wf["w_down"][l], res=x1, name=n + "w_down")
        saved.append(dict(x=x, h=h, P=P, cw=cw, cb=cb, ya=ya, wm=wm, bT=bT, lng=lng, lnb=lnb, yb=yb,
                          q_raw=q_raw, k_raw=k_raw, gq=gq, gk=gk, qn=qn, kn=kn, vh=vh, o=o, yc=yc, bg=bg,
                          gate_col0=gate_col0, merged=merged, x1=x1, g1=g1, g2=g2, h2=h2, gu=gu, act=act))
        x = x2

    dx, lpart, dxb = _loss_grad(x, target, name="loss")
    grads = {k: [None] * L for k in WEIGHTS}
    for l in reversed(range(L)):
        n = f"l{l}_b_"
        sv = saved[l]
        grads["w_down"][l] = _matmul_tn(sv["act"], dxb, name=n + "g_w_down")
        dact = _matmul(dxb, wf["w_down"][l].T, name=n + "d_act")
        dgu = _swiglu_bwd(sv["gu"], dact, name=n + "swiglu")
        grads["w_gate_up"][l] = _matmul_tn(sv["h2"], dgu, name=n + "g_w_gate_up")
        dh2 = _matmul(dgu, wf["w_gate_up"][l].T, name=n + "d_h2")
        dx1, dg2, dx1b = _rmsnorm_bwd(sv["x1"], sv["g2"], dh2, scale=1.0, dres=dx, name=n + "ffnnorm",
                                      bf16_copy=True)
        grads["ffn_norm_g"][l] = dg2[0]
        grads["w_o"][l] = _matmul_tn(sv["merged"], dx1b, name=n + "g_w_o")
        dmerged = _matmul(dx1b, wf["w_o"][l].T, name=n + "d_merged")
        ys = (sv["ya"], sv["yb"], sv["yc"])
        wb = wf["w_branch_out"][l]
        dgates, dyd, dy, dbg = _merge_bwd(ys, wb, wb.transpose(0, 2, 1), sv["P"], sv["bg"], dmerged,
                                          gate_col0=sv["gate_col0"], name=n + "merge")
        grads["b_gate"][l] = dbg[0]
        grads["w_branch_out"][l] = jnp.stack(
            [_matmul_tn(ys[i], dyd[i], name=n + f"g_w_branch{i}") for i in range(3)])
        dconv, dcw = _conv_bwd(sv["P"], dy[0], sv["cw"], sv["cb"], C=C, name=n + "conv")
        grads["conv_w"][l], grads["conv_b"][l] = dcw[0:3], dcw[3]
        wmT = sv["wm"].transpose(0, 2, 1)
        dsgu, dsw, dsb, dln = _sgu_bwd(sv["P"], dy[1], sv["lng"], sv["lnb"], sv["wm"], wmT, sv["bT"], W=W,
                                       cu=(3 * C) // W, cv=(3 * C) // W + 1, name=n + "sgu")
        grads["sgu_w"][l], grads["sgu_b"][l] = dsw * tril, dsb[:, :, 0]
        grads["sgu_ln_g"][l], grads["sgu_ln_b"][l] = dln[0], dln[1]
        do = _heads(dy[2], H)
        dqn, dkn, dvh = _attn_bwd(sv["qn"], sv["kn"], sv["vh"], sv["o"], do, umat(btk), tq=btq, tk=btk,
                                   name=n + "attn")
        dq_raw, dgq = _rmsnorm_bwd(sv["q_raw"], sv["gq"], dqn.reshape(H * S, hd), scale=qscale, name=n + "qnorm",
                                   out_dtype=BF16)
        dk_raw, dgk = _rmsnorm_bwd(sv["k_raw"], sv["gk"], dkn.reshape(H * S, hd), scale=1.0, name=n + "knorm",
                                   out_dtype=BF16)
        grads["q_norm_g"][l], grads["k_norm_g"][l] = dgq[0], dgk[0]
        dP = jnp.concatenate([dconv, dsgu, _unheads(dq_raw.reshape(H, S, hd)), _unheads(dk_raw.reshape(H, S, hd)),
                              _unheads(dvh).astype(BF16), dgates], axis=1)
        grads["w_in"][l] = _matmul_tn(sv["h"], dP, name=n + "g_w_in")
        dh = _matmul(dP, wf["w_in"][l].T, name=n + "d_h")
        dx, dg1, dxb = _rmsnorm_bwd(sv["x"], sv["g1"], dh, scale=1.0, dres=dx1, name=n + "mixnorm", bf16_copy=True)
        grads["mix_norm_g"][l] = dg1[0]
    return lpart[0, 0], dx, grads


def kernel(x, mix_norm_g, w_in, b_gate, conv_w, conv_b, sgu_ln_g, sgu_ln_b, sgu_w, sgu_b, q_norm_g, k_norm_g, w_branch_out, w_o, ffn_norm_g, w_gate_up, w_down, loss_target, m_mix_norm_g, m_w_in, m_b_gate, m_conv_w, m_conv_b, m_sgu_ln_g, m_sgu_ln_b, m_sgu_w, m_sgu_b, m_q_norm_g, m_k_norm_g, m_w_branch_out, m_w_o, m_ffn_norm_g, m_w_gate_up, m_w_down, v_mix_norm_g, v_w_in, v_b_gate, v_conv_w, v_conv_b, v_sgu_ln_g, v_sgu_ln_b, v_sgu_w, v_sgu_b, v_q_norm_g, v_k_norm_g, v_w_branch_out, v_w_o, v_ffn_norm_g, v_w_gate_up, v_w_down):
    w = dict(mix_norm_g=mix_norm_g, w_in=w_in, b_gate=b_gate, conv_w=conv_w, conv_b=conv_b, sgu_ln_g=sgu_ln_g,
             sgu_ln_b=sgu_ln_b, sgu_w=sgu_w, sgu_b=sgu_b, q_norm_g=q_norm_g, k_norm_g=k_norm_g,
             w_branch_out=w_branch_out, w_o=w_o, ffn_norm_g=ffn_norm_g, w_gate_up=w_gate_up, w_down=w_down)
    m = dict(mix_norm_g=m_mix_norm_g, w_in=m_w_in, b_gate=m_b_gate, conv_w=m_conv_w, conv_b=m_conv_b,
             sgu_ln_g=m_sgu_ln_g, sgu_ln_b=m_sgu_ln_b, sgu_w=m_sgu_w, sgu_b=m_sgu_b, q_norm_g=m_q_norm_g,
             k_norm_g=m_k_norm_g, w_branch_out=m_w_branch_out, w_o=m_w_o, ffn_norm_g=m_ffn_norm_g,
             w_gate_up=m_w_gate_up, w_down=m_w_down)
    v = dict(mix_norm_g=v_mix_norm_g, w_in=v_w_in, b_gate=v_b_gate, conv_w=v_conv_w, conv_b=v_conv_b,
             sgu_ln_g=v_sgu_ln_g, sgu_ln_b=v_sgu_ln_b, sgu_w=v_sgu_w, sgu_b=v_sgu_b, q_norm_g=v_q_norm_g,
             k_norm_g=v_k_norm_g, w_branch_out=v_w_branch_out, w_o=v_w_o, ffn_norm_g=v_ffn_norm_g,
             w_gate_up=v_w_gate_up, w_down=v_w_down)
    me = 4 * lax.axis_index("x") + 2 * lax.axis_index("y") + lax.axis_index("c")
    S = x.shape[1]

    big_shapes = [w[k].shape for k in BIG]
    gathered = _all_gather(_pack_rows([w[k] for k in BIG], BF16), name="gather_weights")
    parts = _unpack_rows(gathered, big_shapes, lead=(N_DEV,))
    wf = {k: _unshard(p, SHARD_AXIS[k]) for k, p in zip(BIG, parts)}
    conv_g = _all_gather(_pack_rows([conv_w], F32), name="gather_conv_w")
    conv_full = _unshard(_unpack_rows(conv_g, [conv_w.shape], lead=(N_DEV,))[0], 2)
    sm = {k: w[k] for k in SMALL}
    sm["conv_w"] = conv_full

    lpart, dx, grads = _local_step(x[0], loss_target[0], wf, sm)
    loss = lax.psum(lpart, ("x", "y", "c"))

    chunks = _pack_rows([_reshard(gl, SHARD_AXIS[k] - 1) for k in BIG for gl in grads[k]], BF16, lead=(N_DEV,))
    recv = _all_to_all(chunks, name="exchange_grads")
    gb, db, mb, vb = _adamw(_pack_rows([w[k] for k in BIG], F32), _pack_rows([m[k] for k in BIG], F32),
                            _pack_rows([v[k] for k in BIG], F32), recv, name="adamw_big")
    out = {}
    for nm, packed in (("grad", gb), ("delta", db), ("new_m", mb), ("new_v", vb)):
        for k, a in zip(BIG, _unpack_rows(packed, big_shapes)):
            out[nm, k] = a

    small_grads = [jnp.stack(grads[k]) for k in SMALL]
    small_shapes = [g.shape for g in small_grads]
    sg = _all_gather(_pack_rows(small_grads, F32), name="gather_small_grads")
    gsum = _slot_sum(sg, name="sum_small_grads")
    gsmall = dict(zip(SMALL, _unpack_rows(gsum, small_shapes)))
    cshard = conv_w.shape[2]
    gsmall["conv_w"] = lax.dynamic_slice_in_dim(gsmall["conv_w"], me * cshard, cshard, axis=2)
    own_shapes = [w[k].shape for k in SMALL]
    gs_, ds_, ms_, vs_ = _adamw(_pack_rows([w[k] for k in SMALL], F32), _pack_rows([m[k] for k in SMALL], F32),
                                _pack_rows([v[k] for k in SMALL], F32),
                                _pack_rows([gsmall[k] for k in SMALL], F32)[None], name="adamw_small")
    for nm, packed in (("grad", gs_), ("delta", ds_), ("new_m", ms_), ("new_v", vs_)):
        for k, a in zip(SMALL, _unpack_rows(packed, own_shapes)):
            out[nm, k] = a

    res = [loss, dx[None]]
    for nm in ("grad", "delta", "new_m", "new_v"):
        res += [out[nm, k] for k in WEIGHTS]
    return tuple(res)
```

```python
import functools
import math

import jax
import jax.numpy as jnp
from jax import lax
from jax.experimental import pallas as pl
from jax.experimental.pallas import tpu as pltpu

F32 = jnp.float32
BF16 = jnp.bfloat16
MESH = pl.DeviceIdType.MESH

N_DEV = 8
LANES = 128
VMEM_LIMIT_BYTES = 56 * 1024 * 1024
EPS = 1e-6
ADAM_LR, ADAM_B1, ADAM_B2, ADAM_EPS, ADAM_WD, ADAM_STEP = 0.001, 0.9, 0.999, 1e-08, 0.01, 10
ATTN_FWD_TILES = (512, 256)
ATTN_BWD_TILES = (1024, 256)
BIG = ("w_in", "w_branch_out", "w_o", "w_gate_up", "w_down")
SMALL = ("mix_norm_g", "b_gate", "conv_w", "conv_b", "sgu_ln_g", "sgu_ln_b", "sgu_w", "sgu_b",
         "q_norm_g", "k_norm_g", "ffn_norm_g")
WEIGHTS = ("mix_norm_g", "w_in", "b_gate", "conv_w", "conv_b", "sgu_ln_g", "sgu_ln_b", "sgu_w", "sgu_b",
           "q_norm_g", "k_norm_g", "w_branch_out", "w_o", "ffn_norm_g", "w_gate_up", "w_down")
SHARD_AXIS = {"w_in": 2, "w_branch_out": 3, "w_o": 1, "w_gate_up": 2, "w_down": 1}


def _tile(n, cap, mult):
    best = None
    for t in range(mult, min(n, cap) + 1, mult):
        if n % t == 0:
            best = t
    return best if best is not None else n


def _params(*sem):
    return pltpu.CompilerParams(dimension_semantics=sem if sem else None, vmem_limit_bytes=VMEM_LIMIT_BYTES)


def _erf(x):
    return lax.erf(x)


def _gelu(x):
    return 0.5 * x * (1.0 + _erf(x * (1.0 / math.sqrt(2.0))))


def _gelu_grad(x):
    return 0.5 * (1.0 + _erf(x * (1.0 / math.sqrt(2.0)))) + x * jnp.exp(-0.5 * x * x) * (1.0 / math.sqrt(2.0 * math.pi))


def _sigmoid(x):
    return 1.0 / (1.0 + jnp.exp(-x))


def _matmul(a, b, *, name, res=None, out_dtype=F32):
    M, K = a.shape
    _, N = b.shape
    tm, tn, tk = _tile(M, 1024, 8), _tile(N, 1536, LANES), _tile(K, 1536, LANES)
    nk = K // tk
    has_res = res is not None

    def body(*refs):
        refs = list(refs)
        acc = refs.pop() if nk > 1 else None
        if has_res:
            a_ref, b_ref, r_ref, o_ref = refs
        else:
            a_ref, b_ref, o_ref = refs
        k = pl.program_id(2)
        part = jnp.dot(a_ref[...], b_ref[...], preferred_element_type=F32)

        def finish(v):
            if has_res:
                v = v + r_ref[...]
            o_ref[...] = v.astype(out_dtype)

        if nk == 1:
            finish(part)
        else:
            @pl.when(k == 0)
            def _():
                acc[...] = part

            @pl.when(jnp.logical_and(k > 0, k < nk - 1))
            def _():
                acc[...] += part

            @pl.when(k == nk - 1)
            def _():
                finish(acc[...] + part)

    in_specs = [pl.BlockSpec((tm, tk), lambda i, j, k: (i, k)), pl.BlockSpec((tk, tn), lambda i, j, k: (k, j))]
    args = [a, b]
    if has_res:
        in_specs.append(pl.BlockSpec((tm, tn), lambda i, j, k: (i, j)))
        args.append(res)
    return pl.pallas_call(
        body, name=name, grid=(M // tm, N // tn, nk), in_specs=in_specs,
        out_specs=pl.BlockSpec((tm, tn), lambda i, j, k: (i, j)),
        out_shape=jax.ShapeDtypeStruct((M, N), out_dtype),
        scratch_shapes=[pltpu.VMEM((tm, tn), F32)] if nk > 1 else [],
        compiler_params=_params("parallel", "parallel", "arbitrary"),
    )(*args)


def _matmul_tn(x, y, *, name):
    S, A = x.shape
    _, B = y.shape
    ta, tb, ts = _tile(A, 1024, LANES), _tile(B, 1536, LANES), _tile(S, 1024, 8)
    ns = S // ts

    def body(x_ref, y_ref, o_ref):
        s = pl.program_id(2)
        part = lax.dot_general(x_ref[...], y_ref[...], (((0,), (0,)), ((), ())), preferred_element_type=F32)

        @pl.when(s == 0)
        def _():
            o_ref[...] = part

        @pl.when(s > 0)
        def _():
            o_ref[...] += part

    return pl.pallas_call(
        body, name=name, grid=(A // ta, B // tb, ns),
        in_specs=[pl.BlockSpec((ts, ta), lambda i, j, s: (s, i)), pl.BlockSpec((ts, tb), lambda i, j, s: (s, j))],
        out_specs=pl.BlockSpec((ta, tb), lambda i, j, s: (i, j)),
        out_shape=jax.ShapeDtypeStruct((A, B), F32),
        compiler_params=_params("parallel", "parallel", "arbitrary"),
    )(x, y)


def _rmsnorm_fwd(x, g, *, scale, out_dtype, name):
    R, W = x.shape
    tr = _tile(R, 512 if W >= 512 else 4096, 8)

    def body(x_ref, g_ref, o_ref):
        xv = x_ref[...]
        r = lax.rsqrt(jnp.mean(xv * xv, axis=1, keepdims=True) + EPS)
        o_ref[...] = (xv * r * (g_ref[...] * scale)).astype(out_dtype)

    return pl.pallas_call(
        body, name=name, grid=(R // tr,),
        in_specs=[pl.BlockSpec((tr, W), lambda i: (i, 0)), pl.BlockSpec((1, W), lambda i: (0, 0))],
        out_specs=pl.BlockSpec((tr, W), lambda i: (i, 0)),
        out_shape=jax.ShapeDtypeStruct((R, W), out_dtype),
        compiler_params=_params("parallel"),
    )(x, g)


def _rmsnorm_bwd(x, g, dy, *, scale, name, dres=None, out_dtype=F32, bf16_copy=False):
    R, W = x.shape
    tr = _tile(R, 512 if W >= 512 else 4096, 8)
    has_res = dres is not None

    def body(*refs):
        refs = list(refs)
        dxb_ref = refs.pop() if bf16_copy else None
        if has_res:
            x_ref, g_ref, dy_ref, dres_ref, dx_ref, dg_ref = refs
        else:
            x_ref, g_ref, dy_ref, dx_ref, dg_ref = refs
        i = pl.program_id(0)
        xv = x_ref[...]
        dyv = dy_ref[...].astype(F32) * scale
        r = lax.rsqrt(jnp.mean(xv * xv, axis=1, keepdims=True) + EPS)
        u = dyv * g_ref[...]
        dx = r * u - xv * (r * r * r * jnp.mean(u * xv, axis=1, keepdims=True))
        if has_res:
            dx = dx + dres_ref[...]
        dx_ref[...] = dx.astype(out_dtype)
        if bf16_copy:
            dxb_ref[...] = dx.astype(BF16)
        part = jnp.sum(dyv * xv * r, axis=0, keepdims=True)

        @pl.when(i == 0)
        def _():
            dg_ref[...] = part

        @pl.when(i > 0)
        def _():
            dg_ref[...] += part

    row = pl.BlockSpec((tr, W), lambda i: (i, 0))
    one = pl.BlockSpec((1, W), lambda i: (0, 0))
    in_specs = [row, one, row] + ([row] if has_res else [])
    args = [x, g, dy] + ([dres] if has_res else [])
    extra = bool(bf16_copy)
    return pl.pallas_call(
        body, name=name, grid=(R // tr,), in_specs=in_specs, out_specs=[row, one] + [row] * extra,
        out_shape=[jax.ShapeDtypeStruct((R, W), out_dtype), jax.ShapeDtypeStruct((1, W), F32)]
        + [jax.ShapeDtypeStruct((R, W), BF16)] * extra,
        compiler_params=_params("arbitrary"),
    )(*args)


def _shift_down(u, prev, n):
    ts = u.shape[0]
    out = pltpu.roll(u, n, 0)
    row = lax.broadcasted_iota(jnp.int32, u.shape, 0)
    for r in range(n):
        out = jnp.where(row == r, prev[8 - n + r:8 - n + r + 1, :], out)
    return out


def _shift_up(u, nxt, n):
    ts = u.shape[0]
    out = pltpu.roll(u, ts - n, 0)
    row = lax.broadcasted_iota(jnp.int32, u.shape, 0)
    for r in range(n):
        out = jnp.where(row == ts - n + r, nxt[r:r + 1, :], out)
    return out


def _conv_fwd(P, conv_w, conv_b, *, C, name):
    S = P.shape[0]
    ts = _tile(S, 512, 8)
    hb = ts // 8

    def body(ab_ref, ac_ref, ax_ref, pc_ref, px_ref, w_ref, b_ref, o_ref):
        i = pl.program_id(0)
        u = ac_ref[...] * ax_ref[...]
        prev = pc_ref[...] * px_ref[...] * (i > 0).astype(F32)
        w = w_ref[...]
        y = b_ref[...] + w[0:1, :] * _shift_down(u, prev, 2) + w[1:2, :] * _shift_down(u, prev, 1) + w[2:3, :] * u
        o_ref[...] = (ab_ref[...] * y).astype(BF16)

    cur = lambda c: pl.BlockSpec((ts, C), lambda i: (i, c))
    prv = lambda c: pl.BlockSpec((8, C), lambda i: (jnp.maximum(i * hb - 1, 0), c))
    return pl.pallas_call(
        body, name=name, grid=(S // ts,),
        in_specs=[cur(0), cur(1), cur(2), prv(1), prv(2),
                  pl.BlockSpec((8, C), lambda i: (0, 0)), pl.BlockSpec((1, C), lambda i: (0, 0))],
        out_specs=pl.BlockSpec((ts, C), lambda i: (i, 0)),
        out_shape=jax.ShapeDtypeStruct((S, C), BF16),
        compiler_params=_params("parallel"),
    )(P, P, P, P, P, conv_w, conv_b)


def _conv_bwd(P, dya, conv_w, conv_b, *, C, name):
    S = P.shape[0]
    ts = _tile(S, 512, 8)
    hb = ts // 8
    last = S // 8 - 1
    n = S // ts

    def body(ab_ref, ac_ref, ax_ref, pc_ref, px_ref, dy_ref, nab_ref, ndy_ref, w_ref, b_ref, o_ref, dw_ref):
        i = pl.program_id(0)
        ab, ac, ax = ab_ref[...], ac_ref[...], ax_ref[...]
        u = ac * ax
        prev = pc_ref[...] * px_ref[...] * (i > 0).astype(F32)
        w = w_ref[...]
        u1, u2 = _shift_down(u, prev, 1), _shift_down(u, prev, 2)
        y = b_ref[...] + w[0:1, :] * u2 + w[1:2, :] * u1 + w[2:3, :] * u
        dya_v = dy_ref[...]
        dyp = dya_v * ab
        nxt = ndy_ref[...] * nab_ref[...] * (i < n - 1).astype(F32)
        du = w[2:3, :] * dyp + w[1:2, :] * _shift_up(dyp, nxt, 1) + w[0:1, :] * _shift_up(dyp, nxt, 2)
        o_ref[:, 0:C] = (dya_v * y).astype(BF16)
        o_ref[:, C:2 * C] = (du * ax).astype(BF16)
        o_ref[:, 2 * C:3 * C] = (du * ac).astype(BF16)
        part = jnp.concatenate([
            jnp.sum(dyp * u2, axis=0, keepdims=True), jnp.sum(dyp * u1, axis=0, keepdims=True),
            jnp.sum(dyp * u, axis=0, keepdims=True), jnp.sum(dyp, axis=0, keepdims=True),
            jnp.zeros((4, C), F32)], axis=0)

        @pl.when(i == 0)
        def _():
            dw_ref[...] = part

        @pl.when(i > 0)
        def _():
            dw_ref[...] += part

    cur = lambda c: pl.BlockSpec((ts, C), lambda i: (i, c))
    prv = lambda c: pl.BlockSpec((8, C), lambda i: (jnp.maximum(i * hb - 1, 0), c))
    nxt = lambda c: pl.BlockSpec((8, C), lambda i: (jnp.minimum((i + 1) * hb, last), c))
    return pl.pallas_call(
        body, name=name, grid=(n,),
        in_specs=[cur(0), cur(1), cur(2), prv(1), prv(2), cur(0), nxt(0), nxt(0),
                  pl.BlockSpec((8, C), lambda i: (0, 0)), pl.BlockSpec((1, C), lambda i: (0, 0))],
        out_specs=[pl.BlockSpec((ts, 3 * C), lambda i: (i, 0)), pl.BlockSpec((8, C), lambda i: (0, 0))],
        out_shape=[jax.ShapeDtypeStruct((S, 3 * C), BF16), jax.ShapeDtypeStruct((8, C), F32)],
        compiler_params=_params("arbitrary"),
    )(P, P, P, P, P, dya, P, dya, conv_w, conv_b)


def _sgu_fwd(P, ln_g, ln_b, wm, bT, *, W, cu, cv, name):
    S = P.shape[0]
    G, CH, _ = wm.shape
    gw = W // G
    ts = _tile(S, 512, CH)

    def body(u_ref, v_ref, g_ref, b_ref, wm_ref, bT_ref, o_ref):
        gv = _gelu(v_ref[...])
        mu = jnp.mean(gv, axis=1, keepdims=True)
        xc = gv - mu
        vn = (xc * lax.rsqrt(jnp.mean(xc * xc, axis=1, keepdims=True) + EPS) * g_ref[...] + b_ref[...]).astype(BF16)
        bT_v = bT_ref[...]
        for c in range(ts // CH):
            rows = slice(c * CH, (c + 1) * CH)
            for g in range(G):
                cols = slice(g * gw, (g + 1) * gw)
                mixed = jnp.dot(wm_ref[g], vn[rows, cols], preferred_element_type=F32) + bT_v[:, g:g + 1]
                o_ref[rows, cols] = (_gelu(u_ref[rows, cols]) * mixed).astype(BF16)

    full = lambda shp: pl.BlockSpec(shp, lambda i: (0,) * len(shp))
    return pl.pallas_call(
        body, name=name, grid=(S // ts,),
        in_specs=[pl.BlockSpec((ts, W), lambda i: (i, cu)), pl.BlockSpec((ts, W), lambda i: (i, cv)),
                  full((1, W)), full((1, W)), full((G, CH, CH)), full((CH, G))],
        out_specs=pl.BlockSpec((ts, W), lambda i: (i, 0)),
        out_shape=jax.ShapeDtypeStruct((S, W), BF16),
        compiler_params=_params("parallel"),
    )(P, P, ln_g, ln_b, wm, bT)


def _sgu_bwd(P, dyb, ln_g, ln_b, wm, wmT, bT, *, W, cu, cv, name):
    S = P.shape[0]
    G, CH, _ = wm.shape
    gw = W // G
    ts = _tile(S, 512, CH)

    def body(u_ref, v_ref, dy_ref, g_ref, b_ref, wm_ref, wmT_ref, bT_ref, o_ref, dw_ref, db_ref, dln_ref, dvn_ref):
        i = pl.program_id(0)

        @pl.when(i == 0)
        def _():
            dw_ref[...] = jnp.zeros_like(dw_ref)
            db_ref[...] = jnp.zeros_like(db_ref)
            dln_ref[...] = jnp.zeros_like(dln_ref)

        sv = v_ref[...]
        gv = _gelu(sv)
        mu = jnp.mean(gv, axis=1, keepdims=True)
        xc = gv - mu
        rstd = lax.rsqrt(jnp.mean(xc * xc, axis=1, keepdims=True) + EPS)
        xhat = xc * rstd
        lg = g_ref[...]
        vn = (xhat * lg + b_ref[...]).astype(BF16)
        bT_v = bT_ref[...]
        for c in range(ts // CH):
            rows = slice(c * CH, (c + 1) * CH)
            for g in range(G):
                cols = slice(g * gw, (g + 1) * gw)
                vn_cg = vn[rows, cols]
                mixed = jnp.dot(wm_ref[g], vn_cg, preferred_element_type=F32) + bT_v[:, g:g + 1]
                su = u_ref[rows, cols]
                dyv = dy_ref[rows, cols]
                dmix = dyv * _gelu(su)
                o_ref[rows, cols] = (dyv * mixed * _gelu_grad(su)).astype(BF16)
                dmix_b = dmix.astype(BF16)
                dw_ref[g] += lax.dot_general(dmix_b, vn_cg, (((1,), (1,)), ((), ())), preferred_element_type=F32)
                db_ref[g] += jnp.broadcast_to(jnp.sum(dmix, axis=1, keepdims=True), (CH, CH))
                dvn_ref[rows, cols] = jnp.dot(wmT_ref[g], dmix_b, preferred_element_type=F32)
        dvn = dvn_ref[...]
        dxh = dvn * lg
        dgv = rstd * (dxh - jnp.mean(dxh, axis=1, keepdims=True) - xhat * jnp.mean(dxh * xhat, axis=1, keepdims=True))
        o_ref[:, W:2 * W] = (dgv * _gelu_grad(sv)).astype(BF16)
        dln_ref[0:1, :] += jnp.sum(dvn * xhat, axis=0, keepdims=True)
        dln_ref[1:2, :] += jnp.sum(dvn, axis=0, keepdims=True)

    full = lambda shp: pl.BlockSpec(shp, lambda i: (0,) * len(shp))
    return pl.pallas_call(
        body, name=name, grid=(S // ts,),
        in_specs=[pl.BlockSpec((ts, W), lambda i: (i, cu)), pl.BlockSpec((ts, W), lambda i: (i, cv)),
                  pl.BlockSpec((ts, W), lambda i: (i, 0)),
                  full((1, W)), full((1, W)), full((G, CH, CH)), full((G, CH, CH)), full((CH, G))],
        out_specs=[pl.BlockSpec((ts, 2 * W), lambda i: (i, 0)), full((G, CH, CH)), full((G, CH, CH)), full((8, W))],
        out_shape=[jax.ShapeDtypeStruct((S, 2 * W), BF16), jax.ShapeDtypeStruct((G, CH, CH), F32),
                   jax.ShapeDtypeStruct((G, CH, CH), F32), jax.ShapeDtypeStruct((8, W), F32)],
        scratch_shapes=[pltpu.VMEM((ts, W), F32)],
        compiler_params=_params("arbitrary"),
    )(P, P, dyb, ln_g, ln_b, wm, wmT, bT)


def _block_sums(x, u, parts=1):
    hi = x.astype(BF16)
    out = jnp.dot(hi, u, preferred_element_type=F32)
    if parts == 2:
        lo = (x - hi.astype(F32)).astype(BF16)
        out = out + jnp.dot(lo, u, preferred_element_type=F32)
    return out


_NT = (((1,), (1,)), ((), ()))
_TN = (((0,), (0,)), ((), ()))


def _diag_step(step, j, carry, row0):
    if row0 == 0:
        return step(j, carry, True)
    tail = step(j, tuple(c[row0:] for c in carry), True, row0)
    return tuple(jnp.concatenate([c[:row0], t], axis=0) for c, t in zip(carry, tail))


def _comm_begin(schedule, cin_ref, cout_ref, sems, grid):
    begin, end = schedule(cin_ref, cout_ref, *sems)
    h, i = pl.program_id(0), pl.program_id(1)
    pl.when(jnp.logical_and(h == 0, i == 0))(begin)

    def comm_end():
        pl.when(jnp.logical_and(h == grid[0] - 1, i == grid[1] - 1))(end)

    return comm_end


def _call_with_comm(body, name, grid, in_specs, out_specs, out_shape, args, comm):
    if comm is None:
        return pl.pallas_call(body, name=name, grid=grid, in_specs=in_specs, out_specs=out_specs, out_shape=out_shape,
                              compiler_params=_params("parallel", "arbitrary"))(*args)
    _, xs, cshape = comm
    anyspec = pl.BlockSpec(memory_space=pl.ANY)
    return pl.pallas_call(
        body, name=name, grid=grid, in_specs=in_specs + [anyspec], out_specs=out_specs + [anyspec],
        out_shape=out_shape + [jax.ShapeDtypeStruct(cshape, xs.dtype)], scratch_shapes=_comm_scratch(),
        compiler_params=_params("arbitrary", "arbitrary"),
    )(*args, xs)


def _attn_fwd(q, k, v, umat, *, tq, tk, name, comm=None):
    H, S, hd = q.shape

    def body(*refs):
        if comm is None:
            q_ref, k_ref, v_ref, u_ref, o_ref = refs
        else:
            q_ref, k_ref, v_ref, u_ref, cin_ref, o_ref, cout_ref, *sems = refs
            comm_end = _comm_begin(comm[0], cin_ref, cout_ref, sems, (H, S // tq))
        i = pl.program_id(1)
        qb = q_ref[0]
        um = u_ref[...]
        qpos = lax.broadcasted_iota(jnp.int32, (tq, tk), 0) + i * tq
        kloc = lax.broadcasted_iota(jnp.int32, (tq, tk), 1)

        def step(j, carry, masked, row0=0):
            r, acc = carry
            ks = pl.multiple_of(j * tk, tk)
            kb = k_ref[0, pl.ds(ks, tk), :]
            vb = v_ref[0, pl.ds(ks, tk), :]
            z = lax.dot_general(qb[row0:], kb, _NT, preferred_element_type=F32)
            lb = jnp.minimum(z, 0.0) - jnp.log(1.0 + jnp.exp(-jnp.abs(z)))
            lm = lb - z
            if masked:
                m = (kloc[row0:] + j * tk) < qpos[row0:]
                lm = jnp.where(m, lm, 0.0)
            a = jnp.exp(lb + _block_sums(lm, um) + r)
            if masked:
                a = jnp.where(m, a, 0.0)
            acc = acc + jnp.dot(a.astype(BF16), vb, preferred_element_type=F32)
            return r + jnp.sum(lm, axis=1, keepdims=True), acc

        jd = (i * tq) // tk
        carry = (jnp.zeros((tq, 1), F32), jnp.zeros((tq, hd), F32))
        for dd in reversed(range(max(1, tq // tk))):
            carry = _diag_step(step, jd + dd, carry, dd * tk if tq > tk else 0)
        unroll = 2 if (tq // tk) % 2 == 0 else 1

        def trip(t, c):
            for s in range(unroll):
                c = step(jd - 1 - s - unroll * t, c, False)
            return c

        carry = lax.fori_loop(0, jd // unroll, trip, carry)
        o_ref[0] = carry[1]
        if comm is not None:
            comm_end()

    blk = pl.BlockSpec((1, tq, hd), lambda h, i: (h, i, 0))
    whole = pl.BlockSpec((1, S, hd), lambda h, i: (h, 0, 0))
    in_specs = [blk, whole, whole, pl.BlockSpec((tk, tk), lambda h, i: (0, 0))]
    out_specs, out_shape = [blk], [jax.ShapeDtypeStruct((H, S, hd), F32)]
    res = _call_with_comm(body, name, (H, S // tq), in_specs, out_specs, out_shape, (q, k, v, umat), comm)
    return res[0] if comm is None else res


def _attn_bwd(q, k, v, o, do, umat, *, tq, tk, name, comm=None):
    H, S, hd = q.shape

    def body(*refs):
        if comm is None:
            q_ref, k_ref, v_ref, o_ref, do_ref, u_ref, dq_ref, dk_ref, dv_ref = refs
        else:
            q_ref, k_ref, v_ref, o_ref, do_ref, u_ref, cin_ref, dq_ref, dk_ref, dv_ref, cout_ref, *sems = refs
            comm_end = _comm_begin(comm[0], cin_ref, cout_ref, sems, (H, S // tq))
        i = pl.program_id(1)

        @pl.when(i == 0)
        def _():
            dk_ref[...] = jnp.zeros_like(dk_ref)
            dv_ref[...] = jnp.zeros_like(dv_ref)

        qb = q_ref[0]
        do32 = do_ref[0]
        dob = do32.astype(BF16)
        tot = jnp.sum(dob.astype(F32) * o_ref[0], axis=1, keepdims=True)
        um = u_ref[...]
        qpos = lax.broadcasted_iota(jnp.int32, (tq, tk), 0) + i * tq
        kloc = lax.broadcasted_iota(jnp.int32, (tq, tk), 1)

        def step(j, carry, masked, row0=0):
            r, gs, dq = carry
            ks = pl.multiple_of(j * tk, tk)
            kb = k_ref[0, pl.ds(ks, tk), :]
            vb = v_ref[0, pl.ds(ks, tk), :]
            qs, dos = qb[row0:], dob[row0:]
            z = lax.dot_general(qs, kb, _NT, preferred_element_type=F32)
            lb = jnp.minimum(z, 0.0) - jnp.log(1.0 + jnp.exp(-jnp.abs(z)))
            sig = jnp.exp(lb)
            lm = lb - z
            if masked:
                m = (kloc[row0:] + j * tk) < qpos[row0:]
                lm = jnp.where(m, lm, 0.0)
            a = jnp.exp(lb + _block_sums(lm, um) + r)
            if masked:
                a = jnp.where(m, a, 0.0)
            ab = a.astype(BF16)
            g = lax.dot_general(dos, vb, _NT, preferred_element_type=F32) * ab.astype(F32)
            dz = g - sig * ((tot[row0:] - gs) - _block_sums(g, um, parts=2))
            if masked:
                dz = jnp.where(m, dz, 0.0)
            dzb = dz.astype(BF16)
            dq = dq + jnp.dot(dzb, kb, preferred_element_type=F32)
            dk_ref[0, pl.ds(ks, tk), :] += lax.dot_general(dzb, qs, _TN, preferred_element_type=F32)
            dv_ref[0, pl.ds(ks, tk), :] += lax.dot_general(ab, dos, _TN, preferred_element_type=F32)
            return r + jnp.sum(lm, axis=1, keepdims=True), gs + jnp.sum(g, axis=1, keepdims=True), dq

        jd = (i * tq) // tk
        zero = jnp.zeros((tq, 1), F32)
        carry = (zero, zero, jnp.zeros((tq, hd), F32))
        for dd in reversed(range(max(1, tq // tk))):
            carry = _diag_step(step, jd + dd, carry, dd * tk if tq > tk else 0)
        unroll = 2 if (tq // tk) % 2 == 0 else 1

        def trip(t, c):
            for s in range(unroll):
                c = step(jd - 1 - s - unroll * t, c, False)
            return c

        carry = lax.fori_loop(0, jd // unroll, trip, carry)
        dq_ref[0] = carry[2]
        if comm is not None:
            comm_end()

    blk = pl.BlockSpec((1, tq, hd), lambda h, i: (h, i, 0))
    whole = pl.BlockSpec((1, S, hd), lambda h, i: (h, 0, 0))
    shp = jax.ShapeDtypeStruct((H, S, hd), F32)
    in_specs = [blk, whole, whole, blk, blk, pl.BlockSpec((tk, tk), lambda h, i: (0, 0))]
    return _call_with_comm(body, name, (H, S // tq), in_specs, [blk, whole, whole], [shp, shp, shp],
                           (q, k, v, o, do, umat), comm)


def _merge_fwd(ys, wb, P, b_gate, *, gate_col0, name):
    S, C = ys[0].shape
    D = wb.shape[2]
    ts = _tile(S, 512, 8)

    def body(y0, y1, y2, wb_ref, g0, g1, g2, bg_ref, o_ref):
        acc = jnp.zeros((ts, D), F32)
        for n, (y_ref, g_ref) in enumerate(((y0, g0), (y1, g1), (y2, g2))):
            yd = jnp.dot(y_ref[...], wb_ref[n], preferred_element_type=F32)
            acc = acc + _sigmoid(g_ref[...] + bg_ref[:, n * D:(n + 1) * D]) * yd
        o_ref[...] = acc.astype(BF16)

    ysp = pl.BlockSpec((ts, C), lambda i: (i, 0))
    gsp = lambda n: pl.BlockSpec((ts, D), lambda i: (i, gate_col0 + n))
    return pl.pallas_call(
        body, name=name, grid=(S // ts,),
        in_specs=[ysp, ysp, ysp, pl.BlockSpec((3, C, D), lambda i: (0, 0, 0)), gsp(0), gsp(1), gsp(2),
                  pl.BlockSpec((1, 3 * D), lambda i: (0, 0))],
        out_specs=pl.BlockSpec((ts, D), lambda i: (i, 0)),
        out_shape=jax.ShapeDtypeStruct((S, D), BF16),
        compiler_params=_params("parallel"),
    )(*ys, wb, P, P, P, b_gate)


def _merge_bwd(ys, wb, wbT, P, b_gate, dmerged, *, gate_col0, name):
    S, C = ys[0].shape
    D = wb.shape[2]
    ts = _tile(S, 256, 8)

    def body(y0, y1, y2, wb_ref, wbT_ref, g0, g1, g2, bg_ref, dm_ref, dg_ref, dyd_ref, dy_ref, dbg_ref):
        i = pl.program_id(0)
        dm = dm_ref[...]
        parts = []
        for n, (y_ref, g_ref) in enumerate(((y0, g0), (y1, g1), (y2, g2))):
            yd = jnp.dot(y_ref[...], wb_ref[n], preferred_element_type=F32)
            sg = _sigmoid(g_ref[...] + bg_ref[:, n * D:(n + 1) * D])
            dgate = dm * yd * sg * (1.0 - sg)
            dg_ref[:, n * D:(n + 1) * D] = dgate.astype(BF16)
            parts.append(jnp.sum(dgate, axis=0, keepdims=True))
            dyd = (dm * sg).astype(BF16)
            dyd_ref[n] = dyd
            dy_ref[n] = jnp.dot(dyd, wbT_ref[n], preferred_element_type=F32)
        part = jnp.concatenate(parts, axis=1)

        @pl.when(i == 0)
        def _():
            dbg_ref[...] = part

        @pl.when(i > 0)
        def _():
            dbg_ref[...] += part

    ysp = pl.BlockSpec((ts, C), lambda i: (i, 0))
    gsp = lambda n: pl.BlockSpec((ts, D), lambda i: (i, gate_col0 + n))
    return pl.pallas_call(
        body, name=name, grid=(S // ts,),
        in_specs=[ysp, ysp, ysp, pl.BlockSpec((3, C, D), lambda i: (0, 0, 0)),
                  pl.BlockSpec((3, D, C), lambda i: (0, 0, 0)), gsp(0), gsp(1), gsp(2),
                  pl.BlockSpec((1, 3 * D), lambda i: (0, 0)), pl.BlockSpec((ts, D), lambda i: (i, 0))],
        out_specs=[pl.BlockSpec((ts, 3 * D), lambda i: (i, 0)), pl.BlockSpec((3, ts, D), lambda i: (0, i, 0)),
                   pl.BlockSpec((3, ts, C), lambda i: (0, i, 0)), pl.BlockSpec((1, 3 * D), lambda i: (0, 0))],
        out_shape=[jax.ShapeDtypeStruct((S, 3 * D), BF16), jax.ShapeDtypeStruct((3, S, D), BF16),
                   jax.ShapeDtypeStruct((3, S, C), F32), jax.ShapeDtypeStruct((1, 3 * D), F32)],
        compiler_params=_params("arbitrary"),
    )(*ys, wb, wbT, P, P, P, b_gate, dmerged)


def _swiglu_fwd(gu, *, name):
    S, F2 = gu.shape
    F = F2 // 2
    ts, tf = _tile(S, 512, 8), _tile(F, 1536, LANES)
    nf = F // tf

    def body(g_ref, u_ref, o_ref):
        gt = g_ref[...]
        o_ref[...] = (gt * _sigmoid(gt) * u_ref[...]).astype(BF16)

    return pl.pallas_call(
        body, name=name, grid=(S // ts, nf),
        in_specs=[pl.BlockSpec((ts, tf), lambda i, j: (i, j)), pl.BlockSpec((ts, tf), lambda i, j: (i, j + nf))],
        out_specs=pl.BlockSpec((ts, tf), lambda i, j: (i, j)),
        out_shape=jax.ShapeDtypeStruct((S, F), BF16),
        compiler_params=_params("parallel", "parallel"),
    )(gu, gu)


def _swiglu_bwd(gu, dact, *, name):
    S, F2 = gu.shape
    F = F2 // 2
    ts = _tile(S, 256, 8)

    def body(gu_ref, d_ref, o_ref):
        gt, up, da = gu_ref[:, 0:F], gu_ref[:, F:F2], d_ref[...]
        sg = _sigmoid(gt)
        o_ref[:, 0:F] = (da * up * sg * (1.0 + gt * (1.0 - sg))).astype(BF16)
        o_ref[:, F:F2] = (da * gt * sg).astype(BF16)

    return pl.pallas_call(
        body, name=name, grid=(S // ts,),
        in_specs=[pl.BlockSpec((ts, F2), lambda i: (i, 0)), pl.BlockSpec((ts, F), lambda i: (i, 0))],
        out_specs=pl.BlockSpec((ts, F2), lambda i: (i, 0)),
        out_shape=jax.ShapeDtypeStruct((S, F2), BF16),
        compiler_params=_params("parallel"),
    )(gu, dact)


def _loss_grad(y, target, *, name):
    S, D = y.shape
    ts = _tile(S, 512, 8)

    def body(y_ref, t_ref, dy_ref, l_ref, dyb_ref):
        i = pl.program_id(0)
        err = y_ref[...] - t_ref[...]
        dy_ref[...] = err * (1.0 / D)
        dyb_ref[...] = (err * (1.0 / D)).astype(BF16)
        part = jnp.broadcast_to(jnp.sum(jnp.sum(err * err, axis=1, keepdims=True), axis=0, keepdims=True) * (0.5 / D),
                                (1, LANES))

        @pl.when(i == 0)
        def _():
            l_ref[...] = part

        @pl.when(i > 0)
        def _():
            l_ref[...] += part

    row = pl.BlockSpec((ts, D), lambda i: (i, 0))
    return pl.pallas_call(
        body, name=name, grid=(S // ts,), in_specs=[row, row],
        out_specs=[row, pl.BlockSpec((1, LANES), lambda i: (0, 0)), row],
        out_shape=[jax.ShapeDtypeStruct((S, D), F32), jax.ShapeDtypeStruct((1, LANES), F32),
                   jax.ShapeDtypeStruct((S, D), BF16)],
        compiler_params=_params("arbitrary"),
    )(y, target)


def _adamw(w, m, v, gs, *, name):
    R = w.shape[0]
    ns = gs.shape[0]
    tr = _tile(R, 2048, 16)
    c1 = 1.0 / (1.0 - ADAM_B1 ** ADAM_STEP)
    c2 = 1.0 / (1.0 - ADAM_B2 ** ADAM_STEP)

    def body(w_ref, m_ref, v_ref, gs_ref, g_ref, d_ref, nm_ref, nv_ref):
        g = gs_ref[0].astype(F32)
        for s in range(1, ns):
            g = g + gs_ref[s].astype(F32)
        nm = ADAM_B1 * m_ref[...] + (1.0 - ADAM_B1) * g
        nv = ADAM_B2 * v_ref[...] + (1.0 - ADAM_B2) * (g * g)
        g_ref[...] = g
        nm_ref[...] = nm
        nv_ref[...] = nv
        d_ref[...] = -ADAM_LR * ((nm * c1) / (jnp.sqrt(nv * c2) + ADAM_EPS) + ADAM_WD * w_ref[...])

    row = pl.BlockSpec((tr, LANES), lambda i: (i, 0))
    shp = jax.ShapeDtypeStruct((R, LANES), F32)
    return pl.pallas_call(
        body, name=name, grid=(R // tr,),
        in_specs=[row, row, row, pl.BlockSpec((ns, tr, LANES), lambda i: (0, i, 0))],
        out_specs=[row, row, row, row], out_shape=[shp, shp, shp, shp],
        compiler_params=_params("parallel"),
    )(w, m, v, gs)


def _slot_sum(gs, *, name):
    ns, R, _ = gs.shape
    tr = _tile(R, 2048, 16)

    def body(gs_ref, o_ref):
        g = gs_ref[0]
        for s in range(1, ns):
            g = g + gs_ref[s]
        o_ref[...] = g

    return pl.pallas_call(
        body, name=name, grid=(R // tr,),
        in_specs=[pl.BlockSpec((ns, tr, LANES), lambda i: (0, i, 0))],
        out_specs=pl.BlockSpec((tr, LANES), lambda i: (i, 0)),
        out_shape=jax.ShapeDtypeStruct((R, LANES), F32),
        compiler_params=_params("parallel"),
    )(gs)


def _comm_scratch():
    return [pltpu.SemaphoreType.DMA((7,)), pltpu.SemaphoreType.DMA((7,)), pltpu.SemaphoreType.DMA]


def _gather_schedule(x_ref, out_ref, send_sems, recv_sems, local_sem):
    x, y, c = lax.axis_index("x"), lax.axis_index("y"), lax.axis_index("c")
    me, sibling = (x, y, c), (x, y, 1 - c)
    chips = [(1 - x, y), (x, 1 - y), (1 - x, 1 - y)]

    def slot(px, py, pc):
        return out_ref.at[4 * px + 2 * py + pc]

    def copy(k, block, to, src=None):
        return pltpu.make_async_remote_copy(
            src_ref=slot(*block) if src is None else src, dst_ref=slot(*block),
            send_sem=send_sems.at[k], recv_sem=recv_sems.at[k], device_id=to, device_id_type=MESH)

    mine = pltpu.make_async_copy(x_ref, slot(*me), local_sem)
    first = [copy(0, me, sibling, src=x_ref)]
    first += [copy(1 + j, me, (*chip, c), src=x_ref) for j, chip in enumerate(chips)]
    passed = [copy(4 + j, (*chip, c), sibling) for j, chip in enumerate(chips)]

    def begin():
        mine.start()
        for cp in first:
            cp.start()

    def end():
        for j, chip in enumerate(chips):
            copy(1 + j, (*chip, c), me).wait_recv()
            passed[j].start()
        copy(0, sibling, me).wait_recv()
        for j, chip in enumerate(chips):
            copy(4 + j, (*chip, 1 - c), me).wait_recv()
        for cp in first + passed:
            cp.wait_send()
        mine.wait()

    return begin, end


def _comm_call(schedule, xs, out_shape, *, name):
    def body(x_ref, out_ref, send_sems, recv_sems, local_sem):
        begin, end = schedule(x_ref, out_ref, send_sems, recv_sems, local_sem)
        begin()
        end()

    return pl.pallas_call(
        body, name=name,
        in_specs=[pl.BlockSpec(memory_space=pl.ANY)], out_specs=pl.BlockSpec(memory_space=pl.ANY),
        out_shape=jax.ShapeDtypeStruct(out_shape, xs.dtype), scratch_shapes=_comm_scratch(),
        compiler_params=pltpu.CompilerParams(has_side_effects=True),
    )(xs)


def _all_gather(xs, *, name):
    return _comm_call(_gather_schedule, xs, (N_DEV,) + xs.shape, name=name)


def _exchange_schedule(x_ref, out_ref, send_sems, recv_sems, local_sem):
    x, y, c = lax.axis_index("x"), lax.axis_index("y"), lax.axis_index("c")
    me = 4 * x + 2 * y + c
    mine = pltpu.make_async_copy(x_ref.at[me], out_ref.at[me], local_sem)
    sends, recvs = [], []
    for k in range(1, N_DEV):
        px = 1 - x if k & 4 else x
        py = 1 - y if k & 2 else y
        pc = 1 - c if k & 1 else c
        peer = 4 * px + 2 * py + pc
        sends.append(pltpu.make_async_remote_copy(
            src_ref=x_ref.at[peer], dst_ref=out_ref.at[me], send_sem=send_sems.at[k - 1],
            recv_sem=recv_sems.at[k - 1], device_id=(px, py, pc), device_id_type=MESH))
        recvs.append(pltpu.make_async_remote_copy(
            src_ref=x_ref.at[me], dst_ref=out_ref.at[peer], send_sem=send_sems.at[k - 1],
            recv_sem=recv_sems.at[k - 1], device_id=(px, py, pc), device_id_type=MESH))

    def begin():
        mine.start()
        for cp in sends:
            cp.start()

    def end():
        for cp in recvs:
            cp.wait_recv()
        for cp in sends:
            cp.wait_send()
        mine.wait()

    return begin, end


def _all_to_all(xs, *, name):
    return _comm_call(_exchange_schedule, xs, xs.shape, name=name)


PACK_ROWS = 16


def _pack_rows(parts, dtype, lead=()):
    rows = []
    for p in parts:
        r = p.reshape(lead + (-1, LANES)).astype(dtype)
        pad = (-r.shape[-2]) % PACK_ROWS
        rows.append(jnp.pad(r, [(0, 0)] * len(lead) + [(0, pad), (0, 0)]) if pad else r)
    return jnp.concatenate(rows, axis=len(lead))


def _unpack_rows(packed, shapes, lead=()):
    out, off = [], 0
    for shp in shapes:
        n = math.prod(shp) // LANES
        out.append(lax.slice_in_dim(packed, off, off + n, axis=len(lead)).reshape(lead + tuple(shp)))
        off += n + (-n) % PACK_ROWS
    return out


def _unshard(gathered, axis):
    g = jnp.moveaxis(gathered, 0, axis)
    shp = list(g.shape)
    shp[axis:axis + 2] = [shp[axis] * shp[axis + 1]]
    return g.reshape(shp)


def _reshard(full, axis):
    shp = list(full.shape)
    shp[axis:axis + 1] = [N_DEV, shp[axis] // N_DEV]
    return jnp.moveaxis(full.reshape(shp), axis, 0)


def _heads(a, H):
    S = a.shape[0]
    return a.reshape(S, H, -1).transpose(1, 0, 2)


def _unheads(a):
    H, S, hd = a.shape
    return a.transpose(1, 0, 2).reshape(S, H * hd)


def _late_parts(L):
    return [(k, l) for k in BIG for l in range(L) if (k, l) != ("w_in", 0)]


def _local_step(x, target, w_in0, late_pack, late_shapes, sm):
    S, D = x.shape
    L = sm["mix_norm_g"].shape[0]
    late = _late_parts(L)
    wf = {k: [None] * L for k in BIG}
    wf["w_in"][0] = w_in0
    C = sm["conv_b"].shape[1]
    W = sm["sgu_ln_g"].shape[1]
    hd = sm["q_norm_g"].shape[1]
    G, CH = sm["sgu_w"].shape[1], sm["sgu_w"].shape[2]
    A = w_in0.shape[1] - (3 * C + 2 * W + 3 * D)
    A = A // 3
    H = A // hd
    col_q = 3 * C + 2 * W
    qscale = 1.0 / math.sqrt(hd)
    tril = jnp.tril(jnp.ones((CH, CH), F32))
    (ftq, ftk), (btq, btk) = [(_tile(S, a, LANES), _tile(S, b, LANES)) for a, b in (ATTN_FWD_TILES, ATTN_BWD_TILES)]
    umat = lambda t: (lax.broadcasted_iota(jnp.int32, (t, t), 0) > lax.broadcasted_iota(jnp.int32, (t, t), 1)).astype(BF16)

    saved = []
    for l in range(L):
        n = f"l{l}_"
        g1 = sm["mix_norm_g"][l][None]
        h = _rmsnorm_fwd(x, g1, scale=1.0, out_dtype=BF16, name=n + "mixnorm")
        P = _matmul(h, wf["w_in"][l], name=n + "w_in")
        cw = jnp.pad(sm["conv_w"][l], ((0, 5), (0, 0)))
        cb = sm["conv_b"][l][None]
        ya = _conv_fwd(P, cw, cb, C=C, name=n + "conv")
        wm = (sm["sgu_w"][l] * tril).astype(BF16)
        bT = sm["sgu_b"][l].T
        lng, lnb = sm["sgu_ln_g"][l][None], sm["sgu_ln_b"][l][None]
        yb = _sgu_fwd(P, lng, lnb, wm, bT, W=W, cu=(3 * C) // W, cv=(3 * C) // W + 1, name=n + "sgu")
        qkv = P[:, col_q:col_q + 3 * A].reshape(S, 3, H, hd).transpose(1, 2, 0, 3)
        q_raw, k_raw = qkv[0].reshape(H * S, hd), qkv[1].reshape(H * S, hd)
        gq, gk = sm["q_norm_g"][l][None], sm["k_norm_g"][l][None]
        qn = _rmsnorm_fwd(q_raw, gq, scale=qscale, out_dtype=BF16, name=n + "qnorm").reshape(H, S, hd)
        kn = _rmsnorm_fwd(k_raw, gk, scale=1.0, out_dtype=BF16, name=n + "knorm").reshape(H, S, hd)
        vh = qkv[2].astype(BF16)
        if l == 0:
            o, gathered = _attn_fwd(qn, kn, vh, umat(ftk), tq=ftq, tk=ftk, name=n + "attn",
                                    comm=(_gather_schedule, late_pack, (N_DEV,) + late_pack.shape))
            for (k, ll), part in zip(late, _unpack_rows(gathered, late_shapes, lead=(N_DEV,))):
                wf[k][ll] = _unshard(part, SHARD_AXIS[k] - 1)
        else:
            o = _attn_fwd(qn, kn, vh, umat(ftk), tq=ftq, tk=ftk, name=n + "attn")
        yc = _unheads(o).astype(BF16)
        bg = sm["b_gate"][l][None]
        gate_col0 = (col_q + 3 * A) // D
        merged = _merge_fwd((ya, yb, yc), wf["w_branch_out"][l], P, bg, gate_col0=gate_col0, name=n + "merge")
        x1 = _matmul(merged, wf["w_o"][l], res=x, name=n + "w_o")
        g2 = sm["ffn_norm_g"][l][None]
        h2 = _rmsnorm_fwd(x1, g2, scale=1.0, out_dtype=BF16, name=n + "ffnnorm")
        gu = _matmul(h2, wf["w_gate_up"][l], name=n + "w_gate_up")
        act = _swiglu_fwd(gu, name=n + "swiglu")
        x2 = _matmul(act, wf["w_down"][l], res=x1, name=n + "w_down")
        saved.append(dict(x=x, h=h, P=P, cw=cw, cb=cb, ya=ya, wm=wm, bT=bT, lng=lng, lnb=lnb, yb=yb,
                          q_raw=q_raw, k_raw=k_raw, gq=gq, gk=gk, qn=qn, kn=kn, vh=vh, o=o, yc=yc, bg=bg,
                          gate_col0=gate_col0, merged=merged, x1=x1, g1=g1, g2=g2, h2=h2, gu=gu, act=act))
        x = x2

    dx, lpart, dxb = _loss_grad(x, target, name="loss")
    grads = {k: [None] * L for k in WEIGHTS}
    for l in reversed(range(L)):
        n = f"l{l}_b_"
        sv = saved[l]
        grads["w_down"][l] = _matmul_tn(sv["act"], dxb, name=n + "g_w_down")
        dact = _matmul(dxb, wf["w_down"][l].T, name=n + "d_act")
        dgu = _swiglu_bwd(sv["gu"], dact, name=n + "swiglu")
        grads["w_gate_up"][l] = _matmul_tn(sv["h2"], dgu, name=n + "g_w_gate_up")
        dh2 = _matmul(dgu, wf["w_gate_up"][l].T, name=n + "d_h2")
        dx1, dg2, dx1b = _rmsnorm_bwd(sv["x1"], sv["g2"], dh2, scale=1.0, dres=dx, name=n + "ffnnorm",
                                      bf16_copy=True)
        grads["ffn_norm_g"][l] = dg2[0]
        grads["w_o"][l] = _matmul_tn(sv["merged"], dx1b, name=n + "g_w_o")
        dmerged = _matmul(dx1b, wf["w_o"][l].T, name=n + "d_merged")
        ys = (sv["ya"], sv["yb"], sv["yc"])
        wb = wf["w_branch_out"][l]
        dgates, dyd, dy, dbg = _merge_bwd(ys, wb, wb.transpose(0, 2, 1), sv["P"], sv["bg"], dmerged,
                                          gate_col0=sv["gate_col0"], name=n + "merge")
        grads["b_gate"][l] = dbg[0]
        grads["w_branch_out"][l] = jnp.stack(
            [_matmul_tn(ys[i], dyd[i], name=n + f"g_w_branch{i}") for i in range(3)])
        dconv, dcw = _conv_bwd(sv["P"], dy[0], sv["cw"], sv["cb"], C=C, name=n + "conv")
        grads["conv_w"][l], grads["conv_b"][l] = dcw[0:3], dcw[3]
        wmT = sv["wm"].transpose(0, 2, 1)
        dsgu, dsw, dsb, dln = _sgu_bwd(sv["P"], dy[1], sv["lng"], sv["lnb"], sv["wm"], wmT, sv["bT"], W=W,
                                       cu=(3 * C) // W, cv=(3 * C) // W + 1, name=n + "sgu")
        grads["sgu_w"][l], grads["sgu_b"][l] = dsw * tril, dsb[:, :, 0]
        grads["sgu_ln_g"][l], grads["sgu_ln_b"][l] = dln[0], dln[1]
        do = _heads(dy[2], H)
        if l == 0:
            chunks = _pack_rows([_reshard(grads[k][ll], SHARD_AXIS[k] - 1) for k, ll in late], BF16, lead=(N_DEV,))
            dqn, dkn, dvh, late_recv = _attn_bwd(sv["qn"], sv["kn"], sv["vh"], sv["o"], do, umat(btk), tq=btq, tk=btk,
                                                 name=n + "attn", comm=(_exchange_schedule, chunks, chunks.shape))
        else:
            dqn, dkn, dvh = _attn_bwd(sv["qn"], sv["kn"], sv["vh"], sv["o"], do, umat(btk), tq=btq, tk=btk,
                                      name=n + "attn")
        dq_raw, dgq = _rmsnorm_bwd(sv["q_raw"], sv["gq"], dqn.reshape(H * S, hd), scale=qscale, name=n + "qnorm",
                                   out_dtype=BF16)
        dk_raw, dgk = _rmsnorm_bwd(sv["k_raw"], sv["gk"], dkn.reshape(H * S, hd), scale=1.0, name=n + "knorm",
                                   out_dtype=BF16)
        grads["q_norm_g"][l], grads["k_norm_g"][l] = dgq[0], dgk[0]
        dP = jnp.concatenate([dconv, dsgu, _unheads(dq_raw.reshape(H, S, hd)), _unheads(dk_raw.reshape(H, S, hd)),
                              _unheads(dvh).astype(BF16), dgates], axis=1)
        grads["w_in"][l] = _matmul_tn(sv["h"], dP, name=n + "g_w_in")
        dh = _matmul(dP, wf["w_in"][l].T, name=n + "d_h")
        dx, dg1, dxb = _rmsnorm_bwd(sv["x"], sv["g1"], dh, scale=1.0, dres=dx1, name=n + "mixnorm", bf16_copy=True)
        grads["mix_norm_g"][l] = dg1[0]
    return lpart[0, 0], dx, grads, late_recv


def kernel(x, mix_norm_g, w_in, b_gate, conv_w, conv_b, sgu_ln_g, sgu_ln_b, sgu_w, sgu_b, q_norm_g, k_norm_g, w_branch_out, w_o, ffn_norm_g, w_gate_up, w_down, loss_target, m_mix_norm_g, m_w_in, m_b_gate, m_conv_w, m_conv_b, m_sgu_ln_g, m_sgu_ln_b, m_sgu_w, m_sgu_b, m_q_norm_g, m_k_norm_g, m_w_branch_out, m_w_o, m_ffn_norm_g, m_w_gate_up, m_w_down, v_mix_norm_g, v_w_in, v_b_gate, v_conv_w, v_conv_b, v_sgu_ln_g, v_sgu_ln_b, v_sgu_w, v_sgu_b, v_q_norm_g, v_k_norm_g, v_w_branch_out, v_w_o, v_ffn_norm_g, v_w_gate_up, v_w_down):
    w = dict(mix_norm_g=mix_norm_g, w_in=w_in, b_gate=b_gate, conv_w=conv_w, conv_b=conv_b, sgu_ln_g=sgu_ln_g,
             sgu_ln_b=sgu_ln_b, sgu_w=sgu_w, sgu_b=sgu_b, q_norm_g=q_norm_g, k_norm_g=k_norm_g,
             w_branch_out=w_branch_out, w_o=w_o, ffn_norm_g=ffn_norm_g, w_gate_up=w_gate_up, w_down=w_down)
    m = dict(mix_norm_g=m_mix_norm_g, w_in=m_w_in, b_gate=m_b_gate, conv_w=m_conv_w, conv_b=m_conv_b,
             sgu_ln_g=m_sgu_ln_g, sgu_ln_b=m_sgu_ln_b, sgu_w=m_sgu_w, sgu_b=m_sgu_b, q_norm_g=m_q_norm_g,
             k_norm_g=m_k_norm_g, w_branch_out=m_w_branch_out, w_o=m_w_o, ffn_norm_g=m_ffn_norm_g,
             w_gate_up=m_w_gate_up, w_down=m_w_down)
    v = dict(mix_norm_g=v_mix_norm_g, w_in=v_w_in, b_gate=v_b_gate, conv_w=v_conv_w, conv_b=v_conv_b,
             sgu_ln_g=v_sgu_ln_g, sgu_ln_b=v_sgu_ln_b, sgu_w=v_sgu_w, sgu_b=v_sgu_b, q_norm_g=v_q_norm_g,
             k_norm_g=v_k_norm_g, w_branch_out=v_w_branch_out, w_o=v_w_o, ffn_norm_g=v_ffn_norm_g,
             w_gate_up=v_w_gate_up, w_down=v_w_down)
    me = 4 * lax.axis_index("x") + 2 * lax.axis_index("y") + lax.axis_index("c")
    S = x.shape[1]

    L = w_in.shape[0]
    late = _late_parts(L)
    late_shapes = [w[k].shape[1:] for k, _ in late]
    w_in0 = _unshard(_all_gather(_pack_rows([w_in[0]], BF16), name="gather_w_in0").reshape((N_DEV,) + w_in.shape[1:]),
                     SHARD_AXIS["w_in"] - 1)
    late_pack = _pack_rows([w[k][l] for k, l in late], BF16)
    conv_g = _all_gather(_pack_rows([conv_w], F32), name="gather_conv_w")
    sm = {k: w[k] for k in SMALL}
    sm["conv_w"] = _unshard(_unpack_rows(conv_g, [conv_w.shape], lead=(N_DEV,))[0], 2)

    lpart, dx, grads, late_recv = _local_step(x[0], loss_target[0], w_in0, late_pack, late_shapes, sm)
    loss = lax.psum(lpart, ("x", "y", "c"))

    early_chunks = _pack_rows([_reshard(grads["w_in"][0], SHARD_AXIS["w_in"] - 1)], BF16, lead=(N_DEV,))
    early_recv = _all_to_all(early_chunks, name="exchange_w_in0")
    pk = lambda t, parts: _pack_rows([t[k][l] for k, l in parts], F32)
    early = [("w_in", 0)]
    res_e = _adamw(pk(w, early), pk(m, early), pk(v, early), early_recv, name="adamw_w_in0")
    res_l = _adamw(pk(w, late), pk(m, late), pk(v, late), late_recv, name="adamw_late")
    out = {}
    for i, nm in enumerate(("grad", "delta", "new_m", "new_v")):
        per = {("w_in", 0): res_e[i].reshape(w_in.shape[1:])}
        per.update(zip(late, _unpack_rows(res_l[i], late_shapes)))
        for k in BIG:
            out[nm, k] = jnp.stack([per[k, l] for l in range(L)])

    small_grads = [jnp.stack(grads[k]) for k in SMALL]
    small_shapes = [g.shape for g in small_grads]
    sg = _all_gather(_pack_rows(small_grads, F32), name="gather_small_grads")
    gsum = _slot_sum(sg, name="sum_small_grads")
    gsmall = dict(zip(SMALL, _unpack_rows(gsum, small_shapes)))
    cshard = conv_w.shape[2]
    gsmall["conv_w"] = lax.dynamic_slice_in_dim(gsmall["conv_w"], me * cshard, cshard, axis=2)
    own_shapes = [w[k].shape for k in SMALL]
    gs_, ds_, ms_, vs_ = _adamw(_pack_rows([w[k] for k in SMALL], F32), _pack_rows([m[k] for k in SMALL], F32),
                                _pack_rows([v[k] for k in SMALL], F32),
                                _pack_rows([gsmall[k] for k in SMALL], F32)[None], name="adamw_small")
    for nm, packed in (("grad", gs_), ("delta", ds_), ("new_m", ms_), ("new_v", vs_)):
        for k, a in zip(SMALL, _unpack_rows(packed, own_shapes)):
            out[nm, k] = a

    res = [loss, dx[None]]
    for nm in ("grad", "delta", "new_m", "new_v"):
        res += [out[nm, k] for k in WEIGHTS]
    return tuple(res)
```

```python
import functools
import math

import jax
import jax.numpy as jnp
from jax import lax
from jax.experimental import pallas as pl
from jax.experimental.pallas import tpu as pltpu

F32 = jnp.float32
BF16 = jnp.bfloat16
MESH = pl.DeviceIdType.MESH

N_DEV = 8
LANES = 128
VMEM_LIMIT_BYTES = 56 * 1024 * 1024
EPS = 1e-6
ADAM_LR, ADAM_B1, ADAM_B2, ADAM_EPS, ADAM_WD, ADAM_STEP = 0.001, 0.9, 0.999, 1e-08, 0.01, 10
ATTN_FWD_TILES = (512, 256)
ATTN_BWD_TILES = (1024, 256)
BIG = ("w_in", "w_branch_out", "w_o", "w_gate_up", "w_down")
SMALL = ("mix_norm_g", "b_gate", "conv_w", "conv_b", "sgu_ln_g", "sgu_ln_b", "sgu_w", "sgu_b",
         "q_norm_g", "k_norm_g", "ffn_norm_g")
WEIGHTS = ("mix_norm_g", "w_in", "b_gate", "conv_w", "conv_b", "sgu_ln_g", "sgu_ln_b", "sgu_w", "sgu_b",
           "q_norm_g", "k_norm_g", "w_branch_out", "w_o", "ffn_norm_g", "w_gate_up", "w_down")
SHARD_AXIS = {"w_in": 2, "w_branch_out": 3, "w_o": 1, "w_gate_up": 2, "w_down": 1}


def _tile(n, cap, mult):
    best = None
    for t in range(mult, min(n, cap) + 1, mult):
        if n % t == 0:
            best = t
    return best if best is not None else n


def _params(*sem):
    return pltpu.CompilerParams(dimension_semantics=sem if sem else None, vmem_limit_bytes=VMEM_LIMIT_BYTES)


def _erf(x):
    return lax.erf(x)


def _gelu(x):
    return 0.5 * x * (1.0 + _erf(x * (1.0 / math.sqrt(2.0))))


def _gelu_grad(x):
    return 0.5 * (1.0 + _erf(x * (1.0 / math.sqrt(2.0)))) + x * jnp.exp(-0.5 * x * x) * (1.0 / math.sqrt(2.0 * math.pi))


def _sigmoid(x):
    return 1.0 / (1.0 + jnp.exp(-x))


def _matmul(a, b, *, name, res=None, out_dtype=F32, comm=None):
    M, K = a.shape
    _, N = b.shape
    tm, tn, tk = _tile(M, 1024, 8), _tile(N, 1536, LANES), _tile(K, 1536, LANES)
    nk = K // tk
    grid = (M // tm, N // tn, nk)
    has_res = res is not None

    def body(*refs):
        refs = list(refs)
        a_ref, b_ref = refs[:2]
        r_ref = refs[2] if has_res else None
        pos = 2 + has_res
        if comm is not None:
            cin_ref, o_ref, cout_ref = refs[pos:pos + 3]
            pos += 3
        else:
            o_ref = refs[pos]
            pos += 1
        acc = refs[pos] if nk > 1 else None
        if comm is not None:
            comm_end = _comm_begin(comm[0], cin_ref, cout_ref, refs[pos + (nk > 1):], grid)
        k = pl.program_id(2)
        part = jnp.dot(a_ref[...], b_ref[...], preferred_element_type=F32)

        def finish(v):
            if has_res:
                v = v + r_ref[...]
            o_ref[...] = v.astype(out_dtype)

        if nk == 1:
            finish(part)
        else:
            @pl.when(k == 0)
            def _():
                acc[...] = part

            @pl.when(jnp.logical_and(k > 0, k < nk - 1))
            def _():
                acc[...] += part

            @pl.when(k == nk - 1)
            def _():
                finish(acc[...] + part)

        if comm is not None:
            comm_end()

    in_specs = [pl.BlockSpec((tm, tk), lambda i, j, k: (i, k)), pl.BlockSpec((tk, tn), lambda i, j, k: (k, j))]
    args = [a, b]
    if has_res:
        in_specs.append(pl.BlockSpec((tm, tn), lambda i, j, k: (i, j)))
        args.append(res)
    out_specs = [pl.BlockSpec((tm, tn), lambda i, j, k: (i, j))]
    out_shape = [jax.ShapeDtypeStruct((M, N), out_dtype)]
    scratch = [pltpu.VMEM((tm, tn), F32)] if nk > 1 else []
    sem = ("parallel", "parallel", "arbitrary")
    if comm is not None:
        anyspec = pl.BlockSpec(memory_space=pl.ANY)
        in_specs.append(anyspec)
        args.append(comm[1])
        out_specs.append(anyspec)
        out_shape.append(jax.ShapeDtypeStruct(comm[2], comm[1].dtype))
        scratch += _comm_scratch()
        sem = ("arbitrary",) * 3
    out = pl.pallas_call(body, name=name, grid=grid, in_specs=in_specs, out_specs=out_specs, out_shape=out_shape,
                         scratch_shapes=scratch, compiler_params=_params(*sem))(*args)
    return out[0] if comm is None else out


def _matmul_tn(x, y, *, name, out_dtype=F32, shard=None):
    S, A = x.shape
    _, B = y.shape
    ta, ts = _tile(A, 1024, LANES), _tile(S, 1024, 8)
    tb = shard if shard else _tile(B, 1536, LANES)
    ns = S // ts
    direct = out_dtype == F32
    view = (lambda r: r.at[0]) if shard else (lambda r: r)

    def body(x_ref, y_ref, o_ref, *scratch):
        s = pl.program_id(2)
        out = view(o_ref)
        acc = out if direct else scratch[0]
        part = lax.dot_general(x_ref[...], y_ref[...], (((0,), (0,)), ((), ())), preferred_element_type=F32)

        @pl.when(s == 0)
        def _():
            acc[...] = part

        @pl.when(s > 0)
        def _():
            acc[...] += part

        if not direct:
            @pl.when(s == ns - 1)
            def _():
                out[...] = acc[...].astype(out_dtype)

    if shard:
        out_spec, shape = pl.BlockSpec((1, ta, tb), lambda i, j, s: (j, i, 0)), (B // tb, A, tb)
    else:
        out_spec, shape = pl.BlockSpec((ta, tb), lambda i, j, s: (i, j)), (A, B)
    return pl.pallas_call(
        body, name=name, grid=(A // ta, B // tb, ns),
        in_specs=[pl.BlockSpec((ts, ta), lambda i, j, s: (s, i)), pl.BlockSpec((ts, tb), lambda i, j, s: (s, j))],
        out_specs=out_spec, out_shape=jax.ShapeDtypeStruct(shape, out_dtype),
        scratch_shapes=[] if direct else [pltpu.VMEM((ta, tb), F32)],
        compiler_params=_params("parallel", "parallel", "arbitrary"),
    )(x, y)


def _rmsnorm_fwd(x, g, *, scale, out_dtype, name):
    R, W = x.shape
    tr = _tile(R, 512 if W >= 512 else 4096, 8)

    def body(x_ref, g_ref, o_ref):
        xv = x_ref[...]
        r = lax.rsqrt(jnp.mean(xv * xv, axis=1, keepdims=True) + EPS)
        o_ref[...] = (xv * r * (g_ref[...] * scale)).astype(out_dtype)

    return pl.pallas_call(
        body, name=name, grid=(R // tr,),
        in_specs=[pl.BlockSpec((tr, W), lambda i: (i, 0)), pl.BlockSpec((1, W), lambda i: (0, 0))],
        out_specs=pl.BlockSpec((tr, W), lambda i: (i, 0)),
        out_shape=jax.ShapeDtypeStruct((R, W), out_dtype),
        compiler_params=_params("parallel"),
    )(x, g)


def _rmsnorm_bwd(x, g, dy, *, scale, name, dres=None, out_dtype=F32, bf16_copy=False):
    R, W = x.shape
    tr = _tile(R, 512 if W >= 512 else 4096, 8)
    has_res = dres is not None

    def body(*refs):
        refs = list(refs)
        dxb_ref = refs.pop() if bf16_copy else None
        if has_res:
            x_ref, g_ref, dy_ref, dres_ref, dx_ref, dg_ref = refs
        else:
            x_ref, g_ref, dy_ref, dx_ref, dg_ref = refs
        i = pl.program_id(0)
        xv = x_ref[...]
        dyv = dy_ref[...].astype(F32) * scale
        r = lax.rsqrt(jnp.mean(xv * xv, axis=1, keepdims=True) + EPS)
        u = dyv * g_ref[...]
        dx = r * u - xv * (r * r * r * jnp.mean(u * xv, axis=1, keepdims=True))
        if has_res:
            dx = dx + dres_ref[...]
        dx_ref[...] = dx.astype(out_dtype)
        if bf16_copy:
            dxb_ref[...] = dx.astype(BF16)
        part = jnp.sum(dyv * xv * r, axis=0, keepdims=True)

        @pl.when(i == 0)
        def _():
            dg_ref[...] = part

        @pl.when(i > 0)
        def _():
            dg_ref[...] += part

    row = pl.BlockSpec((tr, W), lambda i: (i, 0))
    one = pl.BlockSpec((1, W), lambda i: (0, 0))
    in_specs = [row, one, row] + ([row] if has_res else [])
    args = [x, g, dy] + ([dres] if has_res else [])
    extra = bool(bf16_copy)
    return pl.pallas_call(
        body, name=name, grid=(R // tr,), in_specs=in_specs, out_specs=[row, one] + [row] * extra,
        out_shape=[jax.ShapeDtypeStruct((R, W), out_dtype), jax.ShapeDtypeStruct((1, W), F32)]
        + [jax.ShapeDtypeStruct((R, W), BF16)] * extra,
        compiler_params=_params("arbitrary"),
    )(*args)


def _shift_down(u, prev, n):
    ts = u.shape[0]
    out = pltpu.roll(u, n, 0)
    row = lax.broadcasted_iota(jnp.int32, u.shape, 0)
    for r in range(n):
        out = jnp.where(row == r, prev[8 - n + r:8 - n + r + 1, :], out)
    return out


def _shift_up(u, nxt, n):
    ts = u.shape[0]
    out = pltpu.roll(u, ts - n, 0)
    row = lax.broadcasted_iota(jnp.int32, u.shape, 0)
    for r in range(n):
        out = jnp.where(row == ts - n + r, nxt[r:r + 1, :], out)
    return out


def _conv_fwd(P, conv_w, conv_b, *, C, name):
    S = P.shape[0]
    ts = _tile(S, 512, 8)
    hb = ts // 8

    def body(ab_ref, ac_ref, ax_ref, pc_ref, px_ref, w_ref, b_ref, o_ref):
        i = pl.program_id(0)
        u = ac_ref[...] * ax_ref[...]
        prev = pc_ref[...] * px_ref[...] * (i > 0).astype(F32)
        w = w_ref[...]
        y = b_ref[...] + w[0:1, :] * _shift_down(u, prev, 2) + w[1:2, :] * _shift_down(u, prev, 1) + w[2:3, :] * u
        o_ref[...] = (ab_ref[...] * y).astype(BF16)

    cur = lambda c: pl.BlockSpec((ts, C), lambda i: (i, c))
    prv = lambda c: pl.BlockSpec((8, C), lambda i: (jnp.maximum(i * hb - 1, 0), c))
    return pl.pallas_call(
        body, name=name, grid=(S // ts,),
        in_specs=[cur(0), cur(1), cur(2), prv(1), prv(2),
                  pl.BlockSpec((8, C), lambda i: (0, 0)), pl.BlockSpec((1, C), lambda i: (0, 0))],
        out_specs=pl.BlockSpec((ts, C), lambda i: (i, 0)),
        out_shape=jax.ShapeDtypeStruct((S, C), BF16),
        compiler_params=_params("parallel"),
    )(P, P, P, P, P, conv_w, conv_b)


def _conv_bwd(P, dya, conv_w, conv_b, *, C, name):
    S = P.shape[0]
    ts = _tile(S, 512, 8)
    hb = ts // 8
    last = S // 8 - 1
    n = S // ts

    def body(ab_ref, ac_ref, ax_ref, pc_ref, px_ref, dy_ref, nab_ref, ndy_ref, w_ref, b_ref, o_ref, dw_ref):
        i = pl.program_id(0)
        ab, ac, ax = ab_ref[...], ac_ref[...], ax_ref[...]
        u = ac * ax
        prev = pc_ref[...] * px_ref[...] * (i > 0).astype(F32)
        w = w_ref[...]
        u1, u2 = _shift_down(u, prev, 1), _shift_down(u, prev, 2)
        y = b_ref[...] + w[0:1, :] * u2 + w[1:2, :] * u1 + w[2:3, :] * u
        dya_v = dy_ref[...]
        dyp = dya_v * ab
        nxt = ndy_ref[...] * nab_ref[...] * (i < n - 1).astype(F32)
        du = w[2:3, :] * dyp + w[1:2, :] * _shift_up(dyp, nxt, 1) + w[0:1, :] * _shift_up(dyp, nxt, 2)
        o_ref[:, 0:C] = (dya_v * y).astype(BF16)
        o_ref[:, C:2 * C] = (du * ax).astype(BF16)
        o_ref[:, 2 * C:3 * C] = (du * ac).astype(BF16)
        part = jnp.concatenate([
            jnp.sum(dyp * u2, axis=0, keepdims=True), jnp.sum(dyp * u1, axis=0, keepdims=True),
            jnp.sum(dyp * u, axis=0, keepdims=True), jnp.sum(dyp, axis=0, keepdims=True),
            jnp.zeros((4, C), F32)], axis=0)

        @pl.when(i == 0)
        def _():
            dw_ref[...] = part

        @pl.when(i > 0)
        def _():
            dw_ref[...] += part

    cur = lambda c: pl.BlockSpec((ts, C), lambda i: (i, c))
    prv = lambda c: pl.BlockSpec((8, C), lambda i: (jnp.maximum(i * hb - 1, 0), c))
    nxt = lambda c: pl.BlockSpec((8, C), lambda i: (jnp.minimum((i + 1) * hb, last), c))
    return pl.pallas_call(
        body, name=name, grid=(n,),
        in_specs=[cur(0), cur(1), cur(2), prv(1), prv(2), cur(0), nxt(0), nxt(0),
                  pl.BlockSpec((8, C), lambda i: (0, 0)), pl.BlockSpec((1, C), lambda i: (0, 0))],
        out_specs=[pl.BlockSpec((ts, 3 * C), lambda i: (i, 0)), pl.BlockSpec((8, C), lambda i: (0, 0))],
        out_shape=[jax.ShapeDtypeStruct((S, 3 * C), BF16), jax.ShapeDtypeStruct((8, C), F32)],
        compiler_params=_params("arbitrary"),
    )(P, P, P, P, P, dya, P, dya, conv_w, conv_b)


def _sgu_fwd(P, ln_g, ln_b, wm, bT, *, W, cu, cv, name):
    S = P.shape[0]
    G, CH, _ = wm.shape
    gw = W // G
    ts = _tile(S, 512, CH)

    def body(u_ref, v_ref, g_ref, b_ref, wm_ref, bT_ref, o_ref):
        gv = _gelu(v_ref[...])
        mu = jnp.mean(gv, axis=1, keepdims=True)
        xc = gv - mu
        vn = (xc * lax.rsqrt(jnp.mean(xc * xc, axis=1, keepdims=True) + EPS) * g_ref[...] + b_ref[...]).astype(BF16)
        bT_v = bT_ref[...]
        for c in range(ts // CH):
            rows = slice(c * CH, (c + 1) * CH)
            for g in range(G):
                cols = slice(g * gw, (g + 1) * gw)
                mixed = jnp.dot(wm_ref[g], vn[rows, cols], preferred_element_type=F32) + bT_v[:, g:g + 1]
                o_ref[rows, cols] = (_gelu(u_ref[rows, cols]) * mixed).astype(BF16)

    full = lambda shp: pl.BlockSpec(shp, lambda i: (0,) * len(shp))
    return pl.pallas_call(
        body, name=name, grid=(S // ts,),
        in_specs=[pl.BlockSpec((ts, W), lambda i: (i, cu)), pl.BlockSpec((ts, W), lambda i: (i, cv)),
                  full((1, W)), full((1, W)), full((G, CH, CH)), full((CH, G))],
        out_specs=pl.BlockSpec((ts, W), lambda i: (i, 0)),
        out_shape=jax.ShapeDtypeStruct((S, W), BF16),
        compiler_params=_params("parallel"),
    )(P, P, ln_g, ln_b, wm, bT)


def _sgu_bwd(P, dyb, ln_g, ln_b, wm, wmT, bT, *, W, cu, cv, name):
    S = P.shape[0]
    G, CH, _ = wm.shape
    gw = W // G
    ts = _tile(S, 512, CH)

    def body(u_ref, v_ref, dy_ref, g_ref, b_ref, wm_ref, wmT_ref, bT_ref, o_ref, dw_ref, db_ref, dln_ref, dvn_ref):
        i = pl.program_id(0)

        @pl.when(i == 0)
        def _():
            dw_ref[...] = jnp.zeros_like(dw_ref)
            db_ref[...] = jnp.zeros_like(db_ref)
            dln_ref[...] = jnp.zeros_like(dln_ref)

        sv = v_ref[...]
        gv = _gelu(sv)
        mu = jnp.mean(gv, axis=1, keepdims=True)
        xc = gv - mu
        rstd = lax.rsqrt(jnp.mean(xc * xc, axis=1, keepdims=True) + EPS)
        xhat = xc * rstd
        lg = g_ref[...]
        vn = (xhat * lg + b_ref[...]).astype(BF16)
        bT_v = bT_ref[...]
        for c in range(ts // CH):
            rows = slice(c * CH, (c + 1) * CH)
            for g in range(G):
                cols = slice(g * gw, (g + 1) * gw)
                vn_cg = vn[rows, cols]
                mixed = jnp.dot(wm_ref[g], vn_cg, preferred_element_type=F32) + bT_v[:, g:g + 1]
                su = u_ref[rows, cols]
                dyv = dy_ref[rows, cols]
                dmix = dyv * _gelu(su)
                o_ref[rows, cols] = (dyv * mixed * _gelu_grad(su)).astype(BF16)
                dmix_b = dmix.astype(BF16)
                dw_ref[g] += lax.dot_general(dmix_b, vn_cg, (((1,), (1,)), ((), ())), preferred_element_type=F32)
                db_ref[g] += jnp.broadcast_to(jnp.sum(dmix, axis=1, keepdims=True), (CH, CH))
                dvn_ref[rows, cols] = jnp.dot(wmT_ref[g], dmix_b, preferred_element_type=F32)
        dvn = dvn_ref[...]
        dxh = dvn * lg
        dgv = rstd * (dxh - jnp.mean(dxh, axis=1, keepdims=True) - xhat * jnp.mean(dxh * xhat, axis=1, keepdims=True))
        o_ref[:, W:2 * W] = (dgv * _gelu_grad(sv)).astype(BF16)
        dln_ref[0:1, :] += jnp.sum(dvn * xhat, axis=0, keepdims=True)
        dln_ref[1:2, :] += jnp.sum(dvn, axis=0, keepdims=True)

    full = lambda shp: pl.BlockSpec(shp, lambda i: (0,) * len(shp))
    return pl.pallas_call(
        body, name=name, grid=(S // ts,),
        in_specs=[pl.BlockSpec((ts, W), lambda i: (i, cu)), pl.BlockSpec((ts, W), lambda i: (i, cv)),
                  pl.BlockSpec((ts, W), lambda i: (i, 0)),
                  full((1, W)), full((1, W)), full((G, CH, CH)), full((G, CH, CH)), full((CH, G))],
        out_specs=[pl.BlockSpec((ts, 2 * W), lambda i: (i, 0)), full((G, CH, CH)), full((G, CH, CH)), full((8, W))],
        out_shape=[jax.ShapeDtypeStruct((S, 2 * W), BF16), jax.ShapeDtypeStruct((G, CH, CH), F32),
                   jax.ShapeDtypeStruct((G, CH, CH), F32), jax.ShapeDtypeStruct((8, W), F32)],
        scratch_shapes=[pltpu.VMEM((ts, W), F32)],
        compiler_params=_params("arbitrary"),
    )(P, P, dyb, ln_g, ln_b, wm, wmT, bT)


def _block_sums(x, u, parts=1):
    hi = x.astype(BF16)
    out = jnp.dot(hi, u, preferred_element_type=F32)
    if parts == 2:
        lo = (x - hi.astype(F32)).astype(BF16)
        out = out + jnp.dot(lo, u, preferred_element_type=F32)
    return out


_NT = (((1,), (1,)), ((), ()))
_TN = (((0,), (0,)), ((), ()))


def _diag_step(step, j, carry, row0):
    if row0 == 0:
        return step(j, carry, True)
    tail = step(j, tuple(c[row0:] for c in carry), True, row0)
    return tuple(jnp.concatenate([c[:row0], t], axis=0) for c, t in zip(carry, tail))


def _comm_begin(schedule, cin_ref, cout_ref, sems, grid):
    begin, end = schedule(cin_ref, cout_ref, *sems)
    ids = [pl.program_id(d) for d in range(len(grid))]
    pl.when(functools.reduce(jnp.logical_and, [p == 0 for p in ids]))(begin)

    def comm_end():
        pl.when(functools.reduce(jnp.logical_and, [p == g - 1 for p, g in zip(ids, grid)]))(end)

    return comm_end


def _call_with_comm(body, name, grid, in_specs, out_specs, out_shape, args, comm):
    if comm is None:
        return pl.pallas_call(body, name=name, grid=grid, in_specs=in_specs, out_specs=out_specs, out_shape=out_shape,
                              compiler_params=_params("parallel", "arbitrary"))(*args)
    _, xs, cshape = comm
    anyspec = pl.BlockSpec(memory_space=pl.ANY)
    return pl.pallas_call(
        body, name=name, grid=grid, in_specs=in_specs + [anyspec], out_specs=out_specs + [anyspec],
        out_shape=out_shape + [jax.ShapeDtypeStruct(cshape, xs.dtype)], scratch_shapes=_comm_scratch(),
        compiler_params=_params("arbitrary", "arbitrary"),
    )(*args, xs)


def _attn_fwd(q, k, v, umat, *, tq, tk, name, comm=None):
    H, S, hd = q.shape

    def body(*refs):
        if comm is None:
            q_ref, k_ref, v_ref, u_ref, o_ref = refs
        else:
            q_ref, k_ref, v_ref, u_ref, cin_ref, o_ref, cout_ref, *sems = refs
            comm_end = _comm_begin(comm[0], cin_ref, cout_ref, sems, (H, S // tq))
        i = pl.program_id(1)
        qb = q_ref[0]
        um = u_ref[...]
        qpos = lax.broadcasted_iota(jnp.int32, (tq, tk), 0) + i * tq
        kloc = lax.broadcasted_iota(jnp.int32, (tq, tk), 1)

        def step(j, carry, masked, row0=0):
            r, acc = carry
            ks = pl.multiple_of(j * tk, tk)
            kb = k_ref[0, pl.ds(ks, tk), :]
            vb = v_ref[0, pl.ds(ks, tk), :]
            z = lax.dot_general(qb[row0:], kb, _NT, preferred_element_type=F32)
            lb = jnp.minimum(z, 0.0) - jnp.log(1.0 + jnp.exp(-jnp.abs(z)))
            lm = lb - z
            if masked:
                m = (kloc[row0:] + j * tk) < qpos[row0:]
                lm = jnp.where(m, lm, 0.0)
            a = jnp.exp(lb + _block_sums(lm, um) + r)
            if masked:
                a = jnp.where(m, a, 0.0)
            acc = acc + jnp.dot(a.astype(BF16), vb, preferred_element_type=F32)
            return r + jnp.sum(lm, axis=1, keepdims=True), acc

        jd = (i * tq) // tk
        carry = (jnp.zeros((tq, 1), F32), jnp.zeros((tq, hd), F32))
        for dd in reversed(range(max(1, tq // tk))):
            carry = _diag_step(step, jd + dd, carry, dd * tk if tq > tk else 0)
        unroll = 2 if (tq // tk) % 2 == 0 else 1

        def trip(t, c):
            for s in range(unroll):
                c = step(jd - 1 - s - unroll * t, c, False)
            return c

        carry = lax.fori_loop(0, jd // unroll, trip, carry)
        o_ref[0] = carry[1]
        if comm is not None:
            comm_end()

    blk = pl.BlockSpec((1, tq, hd), lambda h, i: (h, i, 0))
    whole = pl.BlockSpec((1, S, hd), lambda h, i: (h, 0, 0))
    in_specs = [blk, whole, whole, pl.BlockSpec((tk, tk), lambda h, i: (0, 0))]
    out_specs, out_shape = [blk], [jax.ShapeDtypeStruct((H, S, hd), F32)]
    res = _call_with_comm(body, name, (H, S // tq), in_specs, out_specs, out_shape, (q, k, v, umat), comm)
    return res[0] if comm is None else res


def _attn_bwd(q, k, v, o, do, umat, *, tq, tk, name, comm=None):
    H, S, hd = q.shape

    def body(*refs):
        if comm is None:
            q_ref, k_ref, v_ref, o_ref, do_ref, u_ref, dq_ref, dk_ref, dv_ref = refs
        else:
            q_ref, k_ref, v_ref, o_ref, do_ref, u_ref, cin_ref, dq_ref, dk_ref, dv_ref, cout_ref, *sems = refs
            comm_end = _comm_begin(comm[0], cin_ref, cout_ref, sems, (H, S // tq))
        i = pl.program_id(1)

        @pl.when(i == 0)
        def _():
            dk_ref[...] = jnp.zeros_like(dk_ref)
            dv_ref[...] = jnp.zeros_like(dv_ref)

        qb = q_ref[0]
        do32 = do_ref[0]
        dob = do32.astype(BF16)
        tot = jnp.sum(dob.astype(F32) * o_ref[0], axis=1, keepdims=True)
        um = u_ref[...]
        qpos = lax.broadcasted_iota(jnp.int32, (tq, tk), 0) + i * tq
        kloc = lax.broadcasted_iota(jnp.int32, (tq, tk), 1)

        def step(j, carry, masked, row0=0):
            r, gs, dq = carry
            ks = pl.multiple_of(j * tk, tk)
            kb = k_ref[0, pl.ds(ks, tk), :]
            vb = v_ref[0, pl.ds(ks, tk), :]
            qs, dos = qb[row0:], dob[row0:]
            z = lax.dot_general(qs, kb, _NT, preferred_element_type=F32)
            lb = jnp.minimum(z, 0.0) - jnp.log(1.0 + jnp.exp(-jnp.abs(z)))
            sig = jnp.exp(lb)
            lm = lb - z
            if masked:
                m = (kloc[row0:] + j * tk) < qpos[row0:]
                lm = jnp.where(m, lm, 0.0)
            a = jnp.exp(lb + _block_sums(lm, um) + r)
            if masked:
                a = jnp.where(m, a, 0.0)
            ab = a.astype(BF16)
            g = lax.dot_general(dos, vb, _NT, preferred_element_type=F32) * ab.astype(F32)
            dz = g - sig * ((tot[row0:] - gs) - _block_sums(g, um, parts=2))
            if masked:
                dz = jnp.where(m, dz, 0.0)
            dzb = dz.astype(BF16)
            dq = dq + jnp.dot(dzb, kb, preferred_element_type=F32)
            dk_ref[0, pl.ds(ks, tk), :] += lax.dot_general(dzb, qs, _TN, preferred_element_type=F32)
            dv_ref[0, pl.ds(ks, tk), :] += lax.dot_general(ab, dos, _TN, preferred_element_type=F32)
            return r + jnp.sum(lm, axis=1, keepdims=True), gs + jnp.sum(g, axis=1, keepdims=True), dq

        jd = (i * tq) // tk
        zero = jnp.zeros((tq, 1), F32)
        carry = (zero, zero, jnp.zeros((tq, hd), F32))
        for dd in reversed(range(max(1, tq // tk))):
            carry = _diag_step(step, jd + dd, carry, dd * tk if tq > tk else 0)
        unroll = 2 if (tq // tk) % 2 == 0 else 1

        def trip(t, c):
            for s in range(unroll):
                c = step(jd - 1 - s - unroll * t, c, False)
            return c

        carry = lax.fori_loop(0, jd // unroll, trip, carry)
        dq_ref[0] = carry[2]
        if comm is not None:
            comm_end()

    blk = pl.BlockSpec((1, tq, hd), lambda h, i: (h, i, 0))
    whole = pl.BlockSpec((1, S, hd), lambda h, i: (h, 0, 0))
    shp = jax.ShapeDtypeStruct((H, S, hd), F32)
    in_specs = [blk, whole, whole, blk, blk, pl.BlockSpec((tk, tk), lambda h, i: (0, 0))]
    return _call_with_comm(body, name, (H, S // tq), in_specs, [blk, whole, whole], [shp, shp, shp],
                           (q, k, v, o, do, umat), comm)


def _merge_fwd(ys, wb, P, b_gate, *, gate_col0, name):
    S, C = ys[0].shape
    D = wb.shape[2]
    ts = _tile(S, 512, 8)

    def body(y0, y1, y2, wb_ref, g0, g1, g2, bg_ref, o_ref):
        acc = jnp.zeros((ts, D), F32)
        for n, (y_ref, g_ref) in enumerate(((y0, g0), (y1, g1), (y2, g2))):
            yd = jnp.dot(y_ref[...], wb_ref[n], preferred_element_type=F32)
            acc = acc + _sigmoid(g_ref[...] + bg_ref[:, n * D:(n + 1) * D]) * yd
        o_ref[...] = acc.astype(BF16)

    ysp = pl.BlockSpec((ts, C), lambda i: (i, 0))
    gsp = lambda n: pl.BlockSpec((ts, D), lambda i: (i, gate_col0 + n))
    return pl.pallas_call(
        body, name=name, grid=(S // ts,),
        in_specs=[ysp, ysp, ysp, pl.BlockSpec((3, C, D), lambda i: (0, 0, 0)), gsp(0), gsp(1), gsp(2),
                  pl.BlockSpec((1, 3 * D), lambda i: (0, 0))],
        out_specs=pl.BlockSpec((ts, D), lambda i: (i, 0)),
        out_shape=jax.ShapeDtypeStruct((S, D), BF16),
        compiler_params=_params("parallel"),
    )(*ys, wb, P, P, P, b_gate)


def _merge_bwd(ys, wb, wbT, P, b_gate, dmerged, *, gate_col0, name):
    S, C = ys[0].shape
    D = wb.shape[2]
    ts = _tile(S, 256, 8)

    def body(y0, y1, y2, wb_ref, wbT_ref, g0, g1, g2, bg_ref, dm_ref, dg_ref, dyd_ref, dy_ref, dbg_ref):
        i = pl.program_id(0)
        dm = dm_ref[...]
        parts = []
        for n, (y_ref, g_ref) in enumerate(((y0, g0), (y1, g1), (y2, g2))):
            yd = jnp.dot(y_ref[...], wb_ref[n], preferred_element_type=F32)
            sg = _sigmoid(g_ref[...] + bg_ref[:, n * D:(n + 1) * D])
            dgate = dm * yd * sg * (1.0 - sg)
            dg_ref[:, n * D:(n + 1) * D] = dgate.astype(BF16)
            parts.append(jnp.sum(dgate, axis=0, keepdims=True))
            dyd = (dm * sg).astype(BF16)
            dyd_ref[n] = dyd
            dy_ref[n] = jnp.dot(dyd, wbT_ref[n], preferred_element_type=F32)
        part = jnp.concatenate(parts, axis=1)

        @pl.when(i == 0)
        def _():
            dbg_ref[...] = part

        @pl.when(i > 0)
        def _():
            dbg_ref[...] += part

    ysp = pl.BlockSpec((ts, C), lambda i: (i, 0))
    gsp = lambda n: pl.BlockSpec((ts, D), lambda i: (i, gate_col0 + n))
    return pl.pallas_call(
        body, name=name, grid=(S // ts,),
        in_specs=[ysp, ysp, ysp, pl.BlockSpec((3, C, D), lambda i: (0, 0, 0)),
                  pl.BlockSpec((3, D, C), lambda i: (0, 0, 0)), gsp(0), gsp(1), gsp(2),
                  pl.BlockSpec((1, 3 * D), lambda i: (0, 0)), pl.BlockSpec((ts, D), lambda i: (i, 0))],
        out_specs=[pl.BlockSpec((ts, 3 * D), lambda i: (i, 0)), pl.BlockSpec((3, ts, D), lambda i: (0, i, 0)),
                   pl.BlockSpec((3, ts, C), lambda i: (0, i, 0)), pl.BlockSpec((1, 3 * D), lambda i: (0, 0))],
        out_shape=[jax.ShapeDtypeStruct((S, 3 * D), BF16), jax.ShapeDtypeStruct((3, S, D), BF16),
                   jax.ShapeDtypeStruct((3, S, C), F32), jax.ShapeDtypeStruct((1, 3 * D), F32)],
        compiler_params=_params("arbitrary"),
    )(*ys, wb, wbT, P, P, P, b_gate, dmerged)


def _swiglu_fwd(gu, *, name):
    S, F2 = gu.shape
    F = F2 // 2
    ts, tf = _tile(S, 512, 8), _tile(F, 1536, LANES)
    nf = F // tf

    def body(g_ref, u_ref, o_ref):
        gt = g_ref[...]
        o_ref[...] = (gt * _sigmoid(gt) * u_ref[...]).astype(BF16)

    return pl.pallas_call(
        body, name=name, grid=(S // ts, nf),
        in_specs=[pl.BlockSpec((ts, tf), lambda i, j: (i, j)), pl.BlockSpec((ts, tf), lambda i, j: (i, j + nf))],
        out_specs=pl.BlockSpec((ts, tf), lambda i, j: (i, j)),
        out_shape=jax.ShapeDtypeStruct((S, F), BF16),
        compiler_params=_params("parallel", "parallel"),
    )(gu, gu)


def _swiglu_bwd(gu, dact, *, name):
    S, F2 = gu.shape
    F = F2 // 2
    ts = _tile(S, 256, 8)

    def body(gu_ref, d_ref, o_ref):
        gt, up, da = gu_ref[:, 0:F], gu_ref[:, F:F2], d_ref[...]
        sg = _sigmoid(gt)
        o_ref[:, 0:F] = (da * up * sg * (1.0 + gt * (1.0 - sg))).astype(BF16)
        o_ref[:, F:F2] = (da * gt * sg).astype(BF16)

    return pl.pallas_call(
        body, name=name, grid=(S // ts,),
        in_specs=[pl.BlockSpec((ts, F2), lambda i: (i, 0)), pl.BlockSpec((ts, F), lambda i: (i, 0))],
        out_specs=pl.BlockSpec((ts, F2), lambda i: (i, 0)),
        out_shape=jax.ShapeDtypeStruct((S, F2), BF16),
        compiler_params=_params("parallel"),
    )(gu, dact)


def _loss_grad(y, target, *, name):
    S, D = y.shape
    ts = _tile(S, 512, 8)

    def body(y_ref, t_ref, dy_ref, l_ref, dyb_ref):
        i = pl.program_id(0)
        err = y_ref[...] - t_ref[...]
        dy_ref[...] = err * (1.0 / D)
        dyb_ref[...] = (err * (1.0 / D)).astype(BF16)
        part = jnp.broadcast_to(jnp.sum(jnp.sum(err * err, axis=1, keepdims=True), axis=0, keepdims=True) * (0.5 / D),
                                (1, LANES))

        @pl.when(i == 0)
        def _():
            l_ref[...] = part

        @pl.when(i > 0)
        def _():
            l_ref[...] += part

    row = pl.BlockSpec((ts, D), lambda i: (i, 0))
    return pl.pallas_call(
        body, name=name, grid=(S // ts,), in_specs=[row, row],
        out_specs=[row, pl.BlockSpec((1, LANES), lambda i: (0, 0)), row],
        out_shape=[jax.ShapeDtypeStruct((S, D), F32), jax.ShapeDtypeStruct((1, LANES), F32),
                   jax.ShapeDtypeStruct((S, D), BF16)],
        compiler_params=_params("arbitrary"),
    )(y, target)


def _adamw(w, m, v, gs, *, name):
    R = w.shape[0]
    ns = gs.shape[0]
    tr = _tile(R, 2048, 16)
    c1 = 1.0 / (1.0 - ADAM_B1 ** ADAM_STEP)
    c2 = 1.0 / (1.0 - ADAM_B2 ** ADAM_STEP)

    def body(w_ref, m_ref, v_ref, gs_ref, g_ref, d_ref, nm_ref, nv_ref):
        g = gs_ref[0].astype(F32)
        for s in range(1, ns):
            g = g + gs_ref[s].astype(F32)
        nm = ADAM_B1 * m_ref[...] + (1.0 - ADAM_B1) * g
        nv = ADAM_B2 * v_ref[...] + (1.0 - ADAM_B2) * (g * g)
        g_ref[...] = g
        nm_ref[...] = nm
        nv_ref[...] = nv
        d_ref[...] = -ADAM_LR * ((nm * c1) / (jnp.sqrt(nv * c2) + ADAM_EPS) + ADAM_WD * w_ref[...])

    row = pl.BlockSpec((tr, LANES), lambda i: (i, 0))
    shp = jax.ShapeDtypeStruct((R, LANES), F32)
    return pl.pallas_call(
        body, name=name, grid=(R // tr,),
        in_specs=[row, row, row, pl.BlockSpec((ns, tr, LANES), lambda i: (0, i, 0))],
        out_specs=[row, row, row, row], out_shape=[shp, shp, shp, shp],
        compiler_params=_params("parallel"),
    )(w, m, v, gs)


def _slot_sum(gs, *, name):
    ns, R, _ = gs.shape
    tr = _tile(R, 2048, 16)

    def body(gs_ref, o_ref):
        g = gs_ref[0]
        for s in range(1, ns):
            g = g + gs_ref[s]
        o_ref[...] = g

    return pl.pallas_call(
        body, name=name, grid=(R // tr,),
        in_specs=[pl.BlockSpec((ns, tr, LANES), lambda i: (0, i, 0))],
        out_specs=pl.BlockSpec((tr, LANES), lambda i: (i, 0)),
        out_shape=jax.ShapeDtypeStruct((R, LANES), F32),
        compiler_params=_params("parallel"),
    )(gs)


def _comm_scratch():
    return [pltpu.SemaphoreType.DMA((7,)), pltpu.SemaphoreType.DMA((7,)), pltpu.SemaphoreType.DMA]


def _gather_schedule(x_ref, out_ref, send_sems, recv_sems, local_sem):
    x, y, c = lax.axis_index("x"), lax.axis_index("y"), lax.axis_index("c")
    me, sibling = (x, y, c), (x, y, 1 - c)
    chips = [(1 - x, y), (x, 1 - y), (1 - x, 1 - y)]

    def slot(px, py, pc):
        return out_ref.at[4 * px + 2 * py + pc]

    def copy(k, block, to, src=None):
        return pltpu.make_async_remote_copy(
            src_ref=slot(*block) if src is None else src, dst_ref=slot(*block),
            send_sem=send_sems.at[k], recv_sem=recv_sems.at[k], device_id=to, device_id_type=MESH)

    mine = pltpu.make_async_copy(x_ref, slot(*me), local_sem)
    first = [copy(0, me, sibling, src=x_ref)]
    first += [copy(1 + j, me, (*chip, c), src=x_ref) for j, chip in enumerate(chips)]
    passed = [copy(4 + j, (*chip, c), sibling) for j, chip in enumerate(chips)]

    def begin():
        mine.start()
        for cp in first:
            cp.start()

    def end():
        for j, chip in enumerate(chips):
            copy(1 + j, (*chip, c), me).wait_recv()
            passed[j].start()
        copy(0, sibling, me).wait_recv()
        for j, chip in enumerate(chips):
            copy(4 + j, (*chip, 1 - c), me).wait_recv()
        for cp in first + passed:
            cp.wait_send()
        mine.wait()

    return begin, end


def _comm_call(schedule, xs, out_shape, *, name):
    def body(x_ref, out_ref, send_sems, recv_sems, local_sem):
        begin, end = schedule(x_ref, out_ref, send_sems, recv_sems, local_sem)
        begin()
        end()

    return pl.pallas_call(
        body, name=name,
        in_specs=[pl.BlockSpec(memory_space=pl.ANY)], out_specs=pl.BlockSpec(memory_space=pl.ANY),
        out_shape=jax.ShapeDtypeStruct(out_shape, xs.dtype), scratch_shapes=_comm_scratch(),
        compiler_params=pltpu.CompilerParams(has_side_effects=True),
    )(xs)


def _all_gather(xs, *, name):
    return _comm_call(_gather_schedule, xs, (N_DEV,) + xs.shape, name=name)


def _exchange_schedule(x_ref, out_ref, send_sems, recv_sems, local_sem):
    x, y, c = lax.axis_index("x"), lax.axis_index("y"), lax.axis_index("c")
    me = 4 * x + 2 * y + c
    mine = pltpu.make_async_copy(x_ref.at[me], out_ref.at[me], local_sem)
    sends, recvs = [], []
    for k in range(1, N_DEV):
        px = 1 - x if k & 4 else x
        py = 1 - y if k & 2 else y
        pc = 1 - c if k & 1 else c
        peer = 4 * px + 2 * py + pc
        sends.append(pltpu.make_async_remote_copy(
            src_ref=x_ref.at[peer], dst_ref=out_ref.at[me], send_sem=send_sems.at[k - 1],
            recv_sem=recv_sems.at[k - 1], device_id=(px, py, pc), device_id_type=MESH))
        recvs.append(pltpu.make_async_remote_copy(
            src_ref=x_ref.at[me], dst_ref=out_ref.at[peer], send_sem=send_sems.at[k - 1],
            recv_sem=recv_sems.at[k - 1], device_id=(px, py, pc), device_id_type=MESH))

    def begin():
        mine.start()
        for cp in sends:
            cp.start()

    def end():
        for cp in recvs:
            cp.wait_recv()
        for cp in sends:
            cp.wait_send()
        mine.wait()

    return begin, end


PACK_ROWS = 16


def _pack_rows(parts, dtype, lead=()):
    rows = []
    for p in parts:
        r = p.reshape(lead + (-1, LANES)).astype(dtype)
        pad = (-r.shape[-2]) % PACK_ROWS
        rows.append(jnp.pad(r, [(0, 0)] * len(lead) + [(0, pad), (0, 0)]) if pad else r)
    return jnp.concatenate(rows, axis=len(lead))


def _unpack_rows(packed, shapes, lead=()):
    out, off = [], 0
    for shp in shapes:
        n = math.prod(shp) // LANES
        out.append(lax.slice_in_dim(packed, off, off + n, axis=len(lead)).reshape(lead + tuple(shp)))
        off += n + (-n) % PACK_ROWS
    return out


def _unshard(gathered, axis):
    g = jnp.moveaxis(gathered, 0, axis)
    shp = list(g.shape)
    shp[axis:axis + 2] = [shp[axis] * shp[axis + 1]]
    return g.reshape(shp)


def _reshard(full, axis):
    shp = list(full.shape)
    shp[axis:axis + 1] = [N_DEV, shp[axis] // N_DEV]
    return jnp.moveaxis(full.reshape(shp), axis, 0)


def _heads(a, H):
    S = a.shape[0]
    return a.reshape(S, H, -1).transpose(1, 0, 2)


def _unheads(a):
    H, S, hd = a.shape
    return a.transpose(1, 0, 2).reshape(S, H * hd)


def _late_parts(L):
    return [(k, l) for k in BIG for l in range(L) if (k, l) != ("w_in", 0)]


def _local_step(x, target, w_in0, late_pack, late_shapes, sm):
    S, D = x.shape
    L = sm["mix_norm_g"].shape[0]
    late = _late_parts(L)
    wf = {k: [None] * L for k in BIG}
    wf["w_in"][0] = w_in0
    C = sm["conv_b"].shape[1]
    W = sm["sgu_ln_g"].shape[1]
    hd = sm["q_norm_g"].shape[1]
    G, CH = sm["sgu_w"].shape[1], sm["sgu_w"].shape[2]
    A = w_in0.shape[1] - (3 * C + 2 * W + 3 * D)
    A = A // 3
    H = A // hd
    col_q = 3 * C + 2 * W
    qscale = 1.0 / math.sqrt(hd)
    tril = jnp.tril(jnp.ones((CH, CH), F32))
    (ftq, ftk), (btq, btk) = [(_tile(S, a, LANES), _tile(S, b, LANES)) for a, b in (ATTN_FWD_TILES, ATTN_BWD_TILES)]
    umat = lambda t: (lax.broadcasted_iota(jnp.int32, (t, t), 0) > lax.broadcasted_iota(jnp.int32, (t, t), 1)).astype(BF16)

    saved = []
    for l in range(L):
        n = f"l{l}_"
        g1 = sm["mix_norm_g"][l][None]
        h = _rmsnorm_fwd(x, g1, scale=1.0, out_dtype=BF16, name=n + "mixnorm")
        P = _matmul(h, wf["w_in"][l], name=n + "w_in")
        cw = jnp.pad(sm["conv_w"][l], ((0, 5), (0, 0)))
        cb = sm["conv_b"][l][None]
        ya = _conv_fwd(P, cw, cb, C=C, name=n + "conv")
        wm = (sm["sgu_w"][l] * tril).astype(BF16)
        bT = sm["sgu_b"][l].T
        lng, lnb = sm["sgu_ln_g"][l][None], sm["sgu_ln_b"][l][None]
        yb = _sgu_fwd(P, lng, lnb, wm, bT, W=W, cu=(3 * C) // W, cv=(3 * C) // W + 1, name=n + "sgu")
        qkv = P[:, col_q:col_q + 3 * A].reshape(S, 3, H, hd).transpose(1, 2, 0, 3)
        q_raw, k_raw = qkv[0].reshape(H * S, hd), qkv[1].reshape(H * S, hd)
        gq, gk = sm["q_norm_g"][l][None], sm["k_norm_g"][l][None]
        qn = _rmsnorm_fwd(q_raw, gq, scale=qscale, out_dtype=BF16, name=n + "qnorm").reshape(H, S, hd)
        kn = _rmsnorm_fwd(k_raw, gk, scale=1.0, out_dtype=BF16, name=n + "knorm").reshape(H, S, hd)
        vh = qkv[2].astype(BF16)
        if l == 0:
            o, gathered = _attn_fwd(qn, kn, vh, umat(ftk), tq=ftq, tk=ftk, name=n + "attn",
                                    comm=(_gather_schedule, late_pack, (N_DEV,) + late_pack.shape))
            for (k, ll), part in zip(late, _unpack_rows(gathered, late_shapes, lead=(N_DEV,))):
                wf[k][ll] = _unshard(part, SHARD_AXIS[k] - 1)
        else:
            o = _attn_fwd(qn, kn, vh, umat(ftk), tq=ftq, tk=ftk, name=n + "attn")
        yc = _unheads(o).astype(BF16)
        bg = sm["b_gate"][l][None]
        gate_col0 = (col_q + 3 * A) // D
        merged = _merge_fwd((ya, yb, yc), wf["w_branch_out"][l], P, bg, gate_col0=gate_col0, name=n + "merge")
        x1 = _matmul(merged, wf["w_o"][l], res=x, name=n + "w_o")
        g2 = sm["ffn_norm_g"][l][None]
        h2 = _rmsnorm_fwd(x1, g2, scale=1.0, out_dtype=BF16, name=n + "ffnnorm")
        gu = _matmul(h2, wf["w_gate_up"][l], name=n + "w_gate_up")
        act = _swiglu_fwd(gu, name=n + "swiglu")
        x2 = _matmul(act, wf["w_down"][l], res=x1, name=n + "w_down")
        saved.append(dict(x=x, h=h, P=P, cw=cw, cb=cb, ya=ya, wm=wm, bT=bT, lng=lng, lnb=lnb, yb=yb,
                          q_raw=q_raw, k_raw=k_raw, gq=gq, gk=gk, qn=qn, kn=kn, vh=vh, o=o, yc=yc, bg=bg,
                          gate_col0=gate_col0, merged=merged, x1=x1, g1=g1, g2=g2, h2=h2, gu=gu, act=act))
        x = x2

    dx, lpart, dxb = _loss_grad(x, target, name="loss")
    grads = {k: [None] * L for k in WEIGHTS}
    chunked = lambda k, g: g if k == "w_in" else _reshard(g, SHARD_AXIS[k] - 1)
    for l in reversed(range(L)):
        n = f"l{l}_b_"
        sv = saved[l]
        grads["w_down"][l] = _matmul_tn(sv["act"], dxb, name=n + "g_w_down", out_dtype=BF16)
        dact = _matmul(dxb, wf["w_down"][l].T, name=n + "d_act")
        dgu = _swiglu_bwd(sv["gu"], dact, name=n + "swiglu")
        grads["w_gate_up"][l] = _matmul_tn(sv["h2"], dgu, name=n + "g_w_gate_up", out_dtype=BF16)
        dh2 = _matmul(dgu, wf["w_gate_up"][l].T, name=n + "d_h2")
        dx1, dg2, dx1b = _rmsnorm_bwd(sv["x1"], sv["g2"], dh2, scale=1.0, dres=dx, name=n + "ffnnorm",
                                      bf16_copy=True)
        grads["ffn_norm_g"][l] = dg2[0]
        grads["w_o"][l] = _matmul_tn(sv["merged"], dx1b, name=n + "g_w_o", out_dtype=BF16)
        dmerged = _matmul(dx1b, wf["w_o"][l].T, name=n + "d_merged")
        ys = (sv["ya"], sv["yb"], sv["yc"])
        wb = wf["w_branch_out"][l]
        dgates, dyd, dy, dbg = _merge_bwd(ys, wb, wb.transpose(0, 2, 1), sv["P"], sv["bg"], dmerged,
                                          gate_col0=sv["gate_col0"], name=n + "merge")
        grads["b_gate"][l] = dbg[0]
        grads["w_branch_out"][l] = jnp.stack(
            [_matmul_tn(ys[i], dyd[i], name=n + f"g_w_branch{i}", out_dtype=BF16) for i in range(3)])
        dconv, dcw = _conv_bwd(sv["P"], dy[0], sv["cw"], sv["cb"], C=C, name=n + "conv")
        grads["conv_w"][l], grads["conv_b"][l] = dcw[0:3], dcw[3]
        wmT = sv["wm"].transpose(0, 2, 1)
        dsgu, dsw, dsb, dln = _sgu_bwd(sv["P"], dy[1], sv["lng"], sv["lnb"], sv["wm"], wmT, sv["bT"], W=W,
                                       cu=(3 * C) // W, cv=(3 * C) // W + 1, name=n + "sgu")
        grads["sgu_w"][l], grads["sgu_b"][l] = dsw * tril, dsb[:, :, 0]
        grads["sgu_ln_g"][l], grads["sgu_ln_b"][l] = dln[0], dln[1]
        do = _heads(dy[2], H)
        if l == 0:
            chunks = _pack_rows([chunked(k, grads[k][ll]) for k, ll in late], BF16, lead=(N_DEV,))
            dqn, dkn, dvh, late_recv = _attn_bwd(sv["qn"], sv["kn"], sv["vh"], sv["o"], do, umat(btk), tq=btq, tk=btk,
                                                 name=n + "attn", comm=(_exchange_schedule, chunks, chunks.shape))
        else:
            dqn, dkn, dvh = _attn_bwd(sv["qn"], sv["kn"], sv["vh"], sv["o"], do, umat(btk), tq=btq, tk=btk,
                                      name=n + "attn")
        dq_raw, dgq = _rmsnorm_bwd(sv["q_raw"], sv["gq"], dqn.reshape(H * S, hd), scale=qscale, name=n + "qnorm",
                                   out_dtype=BF16)
        dk_raw, dgk = _rmsnorm_bwd(sv["k_raw"], sv["gk"], dkn.reshape(H * S, hd), scale=1.0, name=n + "knorm",
                                   out_dtype=BF16)
        grads["q_norm_g"][l], grads["k_norm_g"][l] = dgq[0], dgk[0]
        dP = jnp.concatenate([dconv, dsgu, _unheads(dq_raw.reshape(H, S, hd)), _unheads(dk_raw.reshape(H, S, hd)),
                              _unheads(dvh).astype(BF16), dgates], axis=1)
        grads["w_in"][l] = _matmul_tn(sv["h"], dP, name=n + "g_w_in", out_dtype=BF16, shard=dP.shape[1] // N_DEV)
        if l == 0:
            chunks = _pack_rows([grads["w_in"][0]], BF16, lead=(N_DEV,))
            dh, early_recv = _matmul(dP, wf["w_in"][l].T, name=n + "d_h",
                                     comm=(_exchange_schedule, chunks, chunks.shape))
        else:
            dh = _matmul(dP, wf["w_in"][l].T, name=n + "d_h")
        dx, dg1, dxb = _rmsnorm_bwd(sv["x"], sv["g1"], dh, scale=1.0, dres=dx1, name=n + "mixnorm", bf16_copy=True)
        grads["mix_norm_g"][l] = dg1[0]
    return lpart[0, 0], dx, grads, early_recv, late_recv


def kernel(x, mix_norm_g, w_in, b_gate, conv_w, conv_b, sgu_ln_g, sgu_ln_b, sgu_w, sgu_b, q_norm_g, k_norm_g, w_branch_out, w_o, ffn_norm_g, w_gate_up, w_down, loss_target, m_mix_norm_g, m_w_in, m_b_gate, m_conv_w, m_conv_b, m_sgu_ln_g, m_sgu_ln_b, m_sgu_w, m_sgu_b, m_q_norm_g, m_k_norm_g, m_w_branch_out, m_w_o, m_ffn_norm_g, m_w_gate_up, m_w_down, v_mix_norm_g, v_w_in, v_b_gate, v_conv_w, v_conv_b, v_sgu_ln_g, v_sgu_ln_b, v_sgu_w, v_sgu_b, v_q_norm_g, v_k_norm_g, v_w_branch_out, v_w_o, v_ffn_norm_g, v_w_gate_up, v_w_down):
    w = dict(mix_norm_g=mix_norm_g, w_in=w_in, b_gate=b_gate, conv_w=conv_w, conv_b=conv_b, sgu_ln_g=sgu_ln_g,
             sgu_ln_b=sgu_ln_b, sgu_w=sgu_w, sgu_b=sgu_b, q_norm_g=q_norm_g, k_norm_g=k_norm_g,
             w_branch_out=w_branch_out, w_o=w_o, ffn_norm_g=ffn_norm_g, w_gate_up=w_gate_up, w_down=w_down)
    m = dict(mix_norm_g=m_mix_norm_g, w_in=m_w_in, b_gate=m_b_gate, conv_w=m_conv_w, conv_b=m_conv_b,
             sgu_ln_g=m_sgu_ln_g, sgu_ln_b=m_sgu_ln_b, sgu_w=m_sgu_w, sgu_b=m_sgu_b, q_norm_g=m_q_norm_g,
             k_norm_g=m_k_norm_g, w_branch_out=m_w_branch_out, w_o=m_w_o, ffn_norm_g=m_ffn_norm_g,
             w_gate_up=m_w_gate_up, w_down=m_w_down)
    v = dict(mix_norm_g=v_mix_norm_g, w_in=v_w_in, b_gate=v_b_gate, conv_w=v_conv_w, conv_b=v_conv_b,
             sgu_ln_g=v_sgu_ln_g, sgu_ln_b=v_sgu_ln_b, sgu_w=v_sgu_w, sgu_b=v_sgu_b, q_norm_g=v_q_norm_g,
             k_norm_g=v_k_norm_g, w_branch_out=v_w_branch_out, w_o=v_w_o, ffn_norm_g=v_ffn_norm_g,
             w_gate_up=v_w_gate_up, w_down=v_w_down)
    me = 4 * lax.axis_index("x") + 2 * lax.axis_index("y") + lax.axis_index("c")
    S = x.shape[1]

    L = w_in.shape[0]
    late = _late_parts(L)
    late_shapes = [w[k].shape[1:] for k, _ in late]
    w_in0 = _unshard(_all_gather(_pack_rows([w_in[0]], BF16), name="gather_w_in0").reshape((N_DEV,) + w_in.shape[1:]),
                     SHARD_AXIS["w_in"] - 1)
    late_pack = _pack_rows([w[k][l] for k, l in late], BF16)
    conv_g = _all_gather(_pack_rows([conv_w], F32), name="gather_conv_w")
    sm = {k: w[k] for k in SMALL}
    sm["conv_w"] = _unshard(_unpack_rows(conv_g, [conv_w.shape], lead=(N_DEV,))[0], 2)

    lpart, dx, grads, early_recv, late_recv = _local_step(x[0], loss_target[0], w_in0, late_pack, late_shapes, sm)
    loss = lax.psum(lpart, ("x", "y", "c"))

    pk = lambda t, parts: _pack_rows([t[k][l] for k, l in parts], F32)
    early = [("w_in", 0)]
    res_e = _adamw(pk(w, early), pk(m, early), pk(v, early), early_recv, name="adamw_w_in0")
    res_l = _adamw(pk(w, late), pk(m, late), pk(v, late), late_recv, name="adamw_late")
    out = {}
    whole = [k for k in BIG if k != "w_in"]
    for i, nm in enumerate(("grad", "delta", "new_m", "new_v")):
        w_in_rest, *rest = _unpack_rows(res_l[i], [(L - 1,) + w_in.shape[1:]] + [w[k].shape for k in whole])
        out[nm, "w_in"] = jnp.concatenate([res_e[i].reshape((1,) + w_in.shape[1:]), w_in_rest], axis=0)
        out.update({(nm, k): a for k, a in zip(whole, rest)})

    small_grads = [jnp.stack(grads[k]) for k in SMALL]
    small_shapes = [g.shape for g in small_grads]
    sg = _all_gather(_pack_rows(small_grads, F32), name="gather_small_grads")
    gsum = _slot_sum(sg, name="sum_small_grads")
    gsmall = dict(zip(SMALL, _unpack_rows(gsum, small_shapes)))
    cshard = conv_w.shape[2]
    gsmall["conv_w"] = lax.dynamic_slice_in_dim(gsmall["conv_w"], me * cshard, cshard, axis=2)
    own_shapes = [w[k].shape for k in SMALL]
    gs_, ds_, ms_, vs_ = _adamw(_pack_rows([w[k] for k in SMALL], F32), _pack_rows([m[k] for k in SMALL], F32),
                                _pack_rows([v[k] for k in SMALL], F32),
                                _pack_rows([gsmall[k] for k in SMALL], F32)[None], name="adamw_small")
    for nm, packed in (("grad", gs_), ("delta", ds_), ("new_m", ms_), ("new_v", vs_)):
        for k, a in zip(SMALL, _unpack_rows(packed, own_shapes)):
            out[nm, k] = a

    res = [loss, dx[None]]
    for nm in ("grad", "delta", "new_m", "new_v"):
        res += [out[nm, k] for k in WEIGHTS]
    return tuple(res)
```

```python
import functools
import math

import jax
import jax.numpy as jnp
from jax import lax
from jax.experimental import pallas as pl
from jax.experimental.pallas import tpu as pltpu

F32 = jnp.float32
BF16 = jnp.bfloat16
MESH = pl.DeviceIdType.MESH

N_DEV = 8
LANES = 128
VMEM_LIMIT_BYTES = 56 * 1024 * 1024
EPS = 1e-6
ADAM_LR, ADAM_B1, ADAM_B2, ADAM_EPS, ADAM_WD, ADAM_STEP = 0.001, 0.9, 0.999, 1e-08, 0.01, 10
ATTN_FWD_TILES = (512, 256)
ATTN_BWD_TILES = (1024, 256)
BIG = ("w_in", "w_branch_out", "w_o", "w_gate_up", "w_down")
SMALL = ("mix_norm_g", "b_gate", "conv_w", "conv_b", "sgu_ln_g", "sgu_ln_b", "sgu_w", "sgu_b",
         "q_norm_g", "k_norm_g", "ffn_norm_g")
WEIGHTS = ("mix_norm_g", "w_in", "b_gate", "conv_w", "conv_b", "sgu_ln_g", "sgu_ln_b", "sgu_w", "sgu_b",
           "q_norm_g", "k_norm_g", "w_branch_out", "w_o", "ffn_norm_g", "w_gate_up", "w_down")
SHARD_AXIS = {"w_in": 2, "w_branch_out": 3, "w_o": 1, "w_gate_up": 2, "w_down": 1}


def _tile(n, cap, mult):
    best = None
    for t in range(mult, min(n, cap) + 1, mult):
        if n % t == 0:
            best = t
    return best if best is not None else n


def _params(*sem):
    return pltpu.CompilerParams(dimension_semantics=sem if sem else None, vmem_limit_bytes=VMEM_LIMIT_BYTES)


def _erf(x):
    return lax.erf(x)


def _gelu(x):
    return 0.5 * x * (1.0 + _erf(x * (1.0 / math.sqrt(2.0))))


def _gelu_grad(x):
    return 0.5 * (1.0 + _erf(x * (1.0 / math.sqrt(2.0)))) + x * jnp.exp(-0.5 * x * x) * (1.0 / math.sqrt(2.0 * math.pi))


def _sigmoid(x):
    return 1.0 / (1.0 + jnp.exp(-x))


def _matmul(a, b, *, name, res=None, out_dtype=F32, comm=None):
    M, K = a.shape
    _, N = b.shape
    tm, tn, tk = _tile(M, 1024, 8), _tile(N, 1536, LANES), _tile(K, 1536, LANES)
    nk = K // tk
    grid = (M // tm, N // tn, nk)
    has_res = res is not None

    def body(*refs):
        refs = list(refs)
        a_ref, b_ref = refs[:2]
        r_ref = refs[2] if has_res else None
        pos = 2 + has_res
        if comm is not None:
            cin_ref, o_ref, cout_ref = refs[pos:pos + 3]
            pos += 3
        else:
            o_ref = refs[pos]
            pos += 1
        acc = refs[pos] if nk > 1 else None
        if comm is not None:
            comm_end = _comm_begin(comm[0], cin_ref, cout_ref, refs[pos + (nk > 1):], grid)
        k = pl.program_id(2)
        part = jnp.dot(a_ref[...], b_ref[...], preferred_element_type=F32)

        def finish(v):
            if has_res:
                v = v + r_ref[...]
            o_ref[...] = v.astype(out_dtype)

        if nk == 1:
            finish(part)
        else:
            @pl.when(k == 0)
            def _():
                acc[...] = part

            @pl.when(jnp.logical_and(k > 0, k < nk - 1))
            def _():
                acc[...] += part

            @pl.when(k == nk - 1)
            def _():
                finish(acc[...] + part)

        if comm is not None:
            comm_end()

    in_specs = [pl.BlockSpec((tm, tk), lambda i, j, k: (i, k)), pl.BlockSpec((tk, tn), lambda i, j, k: (k, j))]
    args = [a, b]
    if has_res:
        in_specs.append(pl.BlockSpec((tm, tn), lambda i, j, k: (i, j)))
        args.append(res)
    out_specs = [pl.BlockSpec((tm, tn), lambda i, j, k: (i, j))]
    out_shape = [jax.ShapeDtypeStruct((M, N), out_dtype)]
    scratch = [pltpu.VMEM((tm, tn), F32)] if nk > 1 else []
    sem = ("parallel", "parallel", "arbitrary")
    if comm is not None:
        anyspec = pl.BlockSpec(memory_space=pl.ANY)
        in_specs.append(anyspec)
        args.append(comm[1])
        out_specs.append(anyspec)
        out_shape.append(jax.ShapeDtypeStruct(comm[2], comm[1].dtype))
        scratch += _comm_scratch()
        sem = ("arbitrary",) * 3
    out = pl.pallas_call(body, name=name, grid=grid, in_specs=in_specs, out_specs=out_specs, out_shape=out_shape,
                         scratch_shapes=scratch, compiler_params=_params(*sem))(*args)
    return out[0] if comm is None else out


def _matmul_tn(x, y, *, name, out_dtype=F32, shard=None):
    S, A = x.shape
    _, B = y.shape
    ta, ts = _tile(A, 1024, LANES), _tile(S, 1024, 8)
    tb = shard if shard else _tile(B, 1536, LANES)
    ns = S // ts
    direct = out_dtype == F32
    view = (lambda r: r.at[0]) if shard else (lambda r: r)

    def body(x_ref, y_ref, o_ref, *scratch):
        s = pl.program_id(2)
        out = view(o_ref)
        acc = out if direct else scratch[0]
        part = lax.dot_general(x_ref[...], y_ref[...], (((0,), (0,)), ((), ())), preferred_element_type=F32)

        @pl.when(s == 0)
        def _():
            acc[...] = part

        @pl.when(s > 0)
        def _():
            acc[...] += part

        if not direct:
            @pl.when(s == ns - 1)
            def _():
                out[...] = acc[...].astype(out_dtype)

    if shard:
        out_spec, shape = pl.BlockSpec((1, ta, tb), lambda i, j, s: (j, i, 0)), (B // tb, A, tb)
    else:
        out_spec, shape = pl.BlockSpec((ta, tb), lambda i, j, s: (i, j)), (A, B)
    return pl.pallas_call(
        body, name=name, grid=(A // ta, B // tb, ns),
        in_specs=[pl.BlockSpec((ts, ta), lambda i, j, s: (s, i)), pl.BlockSpec((ts, tb), lambda i, j, s: (s, j))],
        out_specs=out_spec, out_shape=jax.ShapeDtypeStruct(shape, out_dtype),
        scratch_shapes=[] if direct else [pltpu.VMEM((ta, tb), F32)],
        compiler_params=_params("parallel", "parallel", "arbitrary"),
    )(x, y)


def _rmsnorm_fwd(x, g, *, scale, out_dtype, name):
    R, W = x.shape
    tr = _tile(R, 512 if W >= 512 else 4096, 8)

    def body(x_ref, g_ref, o_ref):
        xv = x_ref[...]
        r = lax.rsqrt(jnp.mean(xv * xv, axis=1, keepdims=True) + EPS)
        o_ref[...] = (xv * r * (g_ref[...] * scale)).astype(out_dtype)

    return pl.pallas_call(
        body, name=name, grid=(R // tr,),
        in_specs=[pl.BlockSpec((tr, W), lambda i: (i, 0)), pl.BlockSpec((1, W), lambda i: (0, 0))],
        out_specs=pl.BlockSpec((tr, W), lambda i: (i, 0)),
        out_shape=jax.ShapeDtypeStruct((R, W), out_dtype),
        compiler_params=_params("parallel"),
    )(x, g)


def _rmsnorm_bwd(x, g, dy, *, scale, name, dres=None, out_dtype=F32, bf16_copy=False):
    R, W = x.shape
    tr = _tile(R, 512 if W >= 512 else 4096, 8)
    has_res = dres is not None

    def body(*refs):
        refs = list(refs)
        dxb_ref = refs.pop() if bf16_copy else None
        if has_res:
            x_ref, g_ref, dy_ref, dres_ref, dx_ref, dg_ref = refs
        else:
            x_ref, g_ref, dy_ref, dx_ref, dg_ref = refs
        i = pl.program_id(0)
        xv = x_ref[...]
        dyv = dy_ref[...].astype(F32) * scale
        r = lax.rsqrt(jnp.mean(xv * xv, axis=1, keepdims=True) + EPS)
        u = dyv * g_ref[...]
        dx = r * u - xv * (r * r * r * jnp.mean(u * xv, axis=1, keepdims=True))
        if has_res:
            dx = dx + dres_ref[...]
        dx_ref[...] = dx.astype(out_dtype)
        if bf16_copy:
            dxb_ref[...] = dx.astype(BF16)
        part = jnp.sum(dyv * xv * r, axis=0, keepdims=True)

        @pl.when(i == 0)
        def _():
            dg_ref[...] = part

        @pl.when(i > 0)
        def _():
            dg_ref[...] += part

    row = pl.BlockSpec((tr, W), lambda i: (i, 0))
    one = pl.BlockSpec((1, W), lambda i: (0, 0))
    in_specs = [row, one, row] + ([row] if has_res else [])
    args = [x, g, dy] + ([dres] if has_res else [])
    extra = bool(bf16_copy)
    return pl.pallas_call(
        body, name=name, grid=(R // tr,), in_specs=in_specs, out_specs=[row, one] + [row] * extra,
        out_shape=[jax.ShapeDtypeStruct((R, W), out_dtype), jax.ShapeDtypeStruct((1, W), F32)]
        + [jax.ShapeDtypeStruct((R, W), BF16)] * extra,
        compiler_params=_params("arbitrary"),
    )(*args)


def _shift_down(u, prev, n):
    ts = u.shape[0]
    out = pltpu.roll(u, n, 0)
    row = lax.broadcasted_iota(jnp.int32, u.shape, 0)
    for r in range(n):
        out = jnp.where(row == r, prev[8 - n + r:8 - n + r + 1, :], out)
    return out


def _shift_up(u, nxt, n):
    ts = u.shape[0]
    out = pltpu.roll(u, ts - n, 0)
    row = lax.broadcasted_iota(jnp.int32, u.shape, 0)
    for r in range(n):
        out = jnp.where(row == ts - n + r, nxt[r:r + 1, :], out)
    return out


def _conv_fwd(P, conv_w, conv_b, *, C, name):
    S = P.shape[0]
    ts = _tile(S, 512, 8)
    hb = ts // 8

    def body(ab_ref, ac_ref, ax_ref, pc_ref, px_ref, w_ref, b_ref, o_ref):
        i = pl.program_id(0)
        u = ac_ref[...] * ax_ref[...]
        prev = pc_ref[...] * px_ref[...] * (i > 0).astype(F32)
        w = w_ref[...]
        y = b_ref[...] + w[0:1, :] * _shift_down(u, prev, 2) + w[1:2, :] * _shift_down(u, prev, 1) + w[2:3, :] * u
        o_ref[...] = (ab_ref[...] * y).astype(BF16)

    cur = lambda c: pl.BlockSpec((ts, C), lambda i: (i, c))
    prv = lambda c: pl.BlockSpec((8, C), lambda i: (jnp.maximum(i * hb - 1, 0), c))
    return pl.pallas_call(
        body, name=name, grid=(S // ts,),
        in_specs=[cur(0), cur(1), cur(2), prv(1), prv(2),
                  pl.BlockSpec((8, C), lambda i: (0, 0)), pl.BlockSpec((1, C), lambda i: (0, 0))],
        out_specs=pl.BlockSpec((ts, C), lambda i: (i, 0)),
        out_shape=jax.ShapeDtypeStruct((S, C), BF16),
        compiler_params=_params("parallel"),
    )(P, P, P, P, P, conv_w, conv_b)


def _conv_bwd(P, dya, conv_w, conv_b, *, C, name):
    S = P.shape[0]
    ts = _tile(S, 512, 8)
    hb = ts // 8
    last = S // 8 - 1
    n = S // ts

    def body(ab_ref, ac_ref, ax_ref, pc_ref, px_ref, dy_ref, nab_ref, ndy_ref, w_ref, b_ref, o_ref, dw_ref):
        i = pl.program_id(0)
        ab, ac, ax = ab_ref[...], ac_ref[...], ax_ref[...]
        u = ac * ax
        prev = pc_ref[...] * px_ref[...] * (i > 0).astype(F32)
        w = w_ref[...]
        u1, u2 = _shift_down(u, prev, 1), _shift_down(u, prev, 2)
        y = b_ref[...] + w[0:1, :] * u2 + w[1:2, :] * u1 + w[2:3, :] * u
        dya_v = dy_ref[...]
        dyp = dya_v * ab
        nxt = ndy_ref[...] * nab_ref[...] * (i < n - 1).astype(F32)
        du = w[2:3, :] * dyp + w[1:2, :] * _shift_up(dyp, nxt, 1) + w[0:1, :] * _shift_up(dyp, nxt, 2)
        o_ref[:, 0:C] = (dya_v * y).astype(BF16)
        o_ref[:, C:2 * C] = (du * ax).astype(BF16)
        o_ref[:, 2 * C:3 * C] = (du * ac).astype(BF16)
        part = jnp.concatenate([
            jnp.sum(dyp * u2, axis=0, keepdims=True), jnp.sum(dyp * u1, axis=0, keepdims=True),
            jnp.sum(dyp * u, axis=0, keepdims=True), jnp.sum(dyp, axis=0, keepdims=True),
            jnp.zeros((4, C), F32)], axis=0)

        @pl.when(i == 0)
        def _():
            dw_ref[...] = part

        @pl.when(i > 0)
        def _():
            dw_ref[...] += part

    cur = lambda c: pl.BlockSpec((ts, C), lambda i: (i, c))
    prv = lambda c: pl.BlockSpec((8, C), lambda i: (jnp.maximum(i * hb - 1, 0), c))
    nxt = lambda c: pl.BlockSpec((8, C), lambda i: (jnp.minimum((i + 1) * hb, last), c))
    return pl.pallas_call(
        body, name=name, grid=(n,),
        in_specs=[cur(0), cur(1), cur(2), prv(1), prv(2), cur(0), nxt(0), nxt(0),
                  pl.BlockSpec((8, C), lambda i: (0, 0)), pl.BlockSpec((1, C), lambda i: (0, 0))],
        out_specs=[pl.BlockSpec((ts, 3 * C), lambda i: (i, 0)), pl.BlockSpec((8, C), lambda i: (0, 0))],
        out_shape=[jax.ShapeDtypeStruct((S, 3 * C), BF16), jax.ShapeDtypeStruct((8, C), F32)],
        compiler_params=_params("arbitrary"),
    )(P, P, P, P, P, dya, P, dya, conv_w, conv_b)


def _sgu_fwd(P, ln_g, ln_b, wm, bT, *, W, cu, cv, name):
    S = P.shape[0]
    G, CH, _ = wm.shape
    gw = W // G
    ts = _tile(S, 512, CH)

    def body(u_ref, v_ref, g_ref, b_ref, wm_ref, bT_ref, o_ref):
        gv = _gelu(v_ref[...])
        mu = jnp.mean(gv, axis=1, keepdims=True)
        xc = gv - mu
        vn = (xc * lax.rsqrt(jnp.mean(xc * xc, axis=1, keepdims=True) + EPS) * g_ref[...] + b_ref[...]).astype(BF16)
        bT_v = bT_ref[...]
        for c in range(ts // CH):
            rows = slice(c * CH, (c + 1) * CH)
            for g in range(G):
                cols = slice(g * gw, (g + 1) * gw)
                mixed = jnp.dot(wm_ref[g], vn[rows, cols], preferred_element_type=F32) + bT_v[:, g:g + 1]
                o_ref[rows, cols] = (_gelu(u_ref[rows, cols]) * mixed).astype(BF16)

    full = lambda shp: pl.BlockSpec(shp, lambda i: (0,) * len(shp))
    return pl.pallas_call(
        body, name=name, grid=(S // ts,),
        in_specs=[pl.BlockSpec((ts, W), lambda i: (i, cu)), pl.BlockSpec((ts, W), lambda i: (i, cv)),
                  full((1, W)), full((1, W)), full((G, CH, CH)), full((CH, G))],
        out_specs=pl.BlockSpec((ts, W), lambda i: (i, 0)),
        out_shape=jax.ShapeDtypeStruct((S, W), BF16),
        compiler_params=_params("parallel"),
    )(P, P, ln_g, ln_b, wm, bT)


def _sgu_bwd(P, dyb, ln_g, ln_b, wm, wmT, bT, *, W, cu, cv, name):
    S = P.shape[0]
    G, CH, _ = wm.shape
    gw = W // G
    ts = _tile(S, 512, CH)

    def body(u_ref, v_ref, dy_ref, g_ref, b_ref, wm_ref, wmT_ref, bT_ref, o_ref, dw_ref, db_ref, dln_ref, dvn_ref):
        i = pl.program_id(0)

        @pl.when(i == 0)
        def _():
            dw_ref[...] = jnp.zeros_like(dw_ref)
            db_ref[...] = jnp.zeros_like(db_ref)
            dln_ref[...] = jnp.zeros_like(dln_ref)

        sv = v_ref[...]
        gv = _gelu(sv)
        mu = jnp.mean(gv, axis=1, keepdims=True)
        xc = gv - mu
        rstd = lax.rsqrt(jnp.mean(xc * xc, axis=1, keepdims=True) + EPS)
        xhat = xc * rstd
        lg = g_ref[...]
        vn = (xhat * lg + b_ref[...]).astype(BF16)
        bT_v = bT_ref[...]
        for c in range(ts // CH):
            rows = slice(c * CH, (c + 1) * CH)
            for g in range(G):
                cols = slice(g * gw, (g + 1) * gw)
                vn_cg = vn[rows, cols]
                mixed = jnp.dot(wm_ref[g], vn_cg, preferred_element_type=F32) + bT_v[:, g:g + 1]
                su = u_ref[rows, cols]
                dyv = dy_ref[rows, cols]
                dmix = dyv * _gelu(su)
                o_ref[rows, cols] = (dyv * mixed * _gelu_grad(su)).astype(BF16)
                dmix_b = dmix.astype(BF16)
                dw_ref[g] += lax.dot_general(dmix_b, vn_cg, (((1,), (1,)), ((), ())), preferred_element_type=F32)
                db_ref[g] += jnp.broadcast_to(jnp.sum(dmix, axis=1, keepdims=True), (CH, CH))
                dvn_ref[rows, cols] = jnp.dot(wmT_ref[g], dmix_b, preferred_element_type=F32)
        dvn = dvn_ref[...]
        dxh = dvn * lg
        dgv = rstd * (dxh - jnp.mean(dxh, axis=1, keepdims=True) - xhat * jnp.mean(dxh * xhat, axis=1, keepdims=True))
        o_ref[:, W:2 * W] = (dgv * _gelu_grad(sv)).astype(BF16)
        dln_ref[0:1, :] += jnp.sum(dvn * xhat, axis=0, keepdims=True)
        dln_ref[1:2, :] += jnp.sum(dvn, axis=0, keepdims=True)

    full = lambda shp: pl.BlockSpec(shp, lambda i: (0,) * len(shp))
    return pl.pallas_call(
        body, name=name, grid=(S // ts,),
        in_specs=[pl.BlockSpec((ts, W), lambda i: (i, cu)), pl.BlockSpec((ts, W), lambda i: (i, cv)),
                  pl.BlockSpec((ts, W), lambda i: (i, 0)),
                  full((1, W)), full((1, W)), full((G, CH, CH)), full((G, CH, CH)), full((CH, G))],
        out_specs=[pl.BlockSpec((ts, 2 * W), lambda i: (i, 0)), full((G, CH, CH)), full((G, CH, CH)), full((8, W))],
        out_shape=[jax.ShapeDtypeStruct((S, 2 * W), BF16), jax.ShapeDtypeStruct((G, CH, CH), F32),
                   jax.ShapeDtypeStruct((G, CH, CH), F32), jax.ShapeDtypeStruct((8, W), F32)],
        scratch_shapes=[pltpu.VMEM((ts, W), F32)],
        compiler_params=_params("arbitrary"),
    )(P, P, dyb, ln_g, ln_b, wm, wmT, bT)


def _block_sums(x, u, parts=1):
    hi = x.astype(BF16)
    out = jnp.dot(hi, u, preferred_element_type=F32)
    if parts == 2:
        lo = (x - hi.astype(F32)).astype(BF16)
        out = out + jnp.dot(lo, u, preferred_element_type=F32)
    return out


_NT = (((1,), (1,)), ((), ()))
_TN = (((0,), (0,)), ((), ()))


def _diag_step(step, j, carry, row0):
    if row0 == 0:
        return step(j, carry, True)
    tail = step(j, tuple(c[row0:] for c in carry), True, row0)
    return tuple(jnp.concatenate([c[:row0], t], axis=0) for c, t in zip(carry, tail))


def _qkv_heads_fwd(P, gq, gk, *, A, col0, hd, qscale, name):
    S = P.shape[0]
    H = A // hd
    ts = _tile(S, 512, 8)

    def body(q_ref, k_ref, v_ref, gq_ref, gk_ref, qn_ref, kn_ref, vh_ref):
        for h in range(H):
            cols = slice(h * hd, (h + 1) * hd)
            for x_ref, g_ref, sc, o_ref in ((q_ref, gq_ref, qscale, qn_ref), (k_ref, gk_ref, 1.0, kn_ref)):
                xh = x_ref[:, cols]
                r = lax.rsqrt(jnp.mean(xh * xh, axis=1, keepdims=True) + EPS)
                o_ref[h] = (xh * r * (g_ref[...] * sc)).astype(BF16)
            vh_ref[h] = v_ref[:, cols].astype(BF16)

    col = lambda n: pl.BlockSpec((ts, A), lambda i: (i, col0 + n))
    gsp = pl.BlockSpec((1, hd), lambda i: (0, 0))
    hsp = pl.BlockSpec((H, ts, hd), lambda i: (0, i, 0))
    shp = jax.ShapeDtypeStruct((H, S, hd), BF16)
    return pl.pallas_call(
        body, name=name, grid=(S // ts,), in_specs=[col(0), col(1), col(2), gsp, gsp],
        out_specs=[hsp, hsp, hsp], out_shape=[shp, shp, shp], compiler_params=_params("parallel"),
    )(P, P, P, gq, gk)


def _qkv_heads_bwd(P, gq, gk, dqn, dkn, dvh, *, A, col0, hd, qscale, name):
    S = P.shape[0]
    H = A // hd
    ts = _tile(S, 512, 8)

    def body(q_ref, k_ref, gq_ref, gk_ref, dq_ref, dk_ref, dv_ref, o_ref, dgq_ref, dgk_ref):
        i = pl.program_id(0)
        parts = [jnp.zeros((1, hd), F32), jnp.zeros((1, hd), F32)]
        for h in range(H):
            for n, (x_ref, g_ref, sc, d_ref) in enumerate(((q_ref, gq_ref, qscale, dq_ref), (k_ref, gk_ref, 1.0, dk_ref))):
                xh = x_ref[:, h * hd:(h + 1) * hd]
                dyv = d_ref[h] * sc
                r = lax.rsqrt(jnp.mean(xh * xh, axis=1, keepdims=True) + EPS)
                u = dyv * g_ref[...]
                dx = r * u - xh * (r * r * r * jnp.mean(u * xh, axis=1, keepdims=True))
                o_ref[:, n * A + h * hd:n * A + (h + 1) * hd] = dx.astype(BF16)
                parts[n] = parts[n] + jnp.sum(dyv * xh * r, axis=0, keepdims=True)
            o_ref[:, 2 * A + h * hd:2 * A + (h + 1) * hd] = dv_ref[h].astype(BF16)

        @pl.when(i == 0)
        def _():
            dgq_ref[...] = parts[0]
            dgk_ref[...] = parts[1]

        @pl.when(i > 0)
        def _():
            dgq_ref[...] += parts[0]
            dgk_ref[...] += parts[1]

    col = lambda n: pl.BlockSpec((ts, A), lambda i: (i, col0 + n))
    gsp = pl.BlockSpec((1, hd), lambda i: (0, 0))
    hsp = pl.BlockSpec((H, ts, hd), lambda i: (0, i, 0))
    return pl.pallas_call(
        body, name=name, grid=(S // ts,), in_specs=[col(0), col(1), gsp, gsp, hsp, hsp, hsp],
        out_specs=[pl.BlockSpec((ts, 3 * A), lambda i: (i, 0)), gsp, gsp],
        out_shape=[jax.ShapeDtypeStruct((S, 3 * A), BF16), jax.ShapeDtypeStruct((1, hd), F32),
                   jax.ShapeDtypeStruct((1, hd), F32)],
        compiler_params=_params("arbitrary"),
    )(P, P, gq, gk, dqn, dkn, dvh)


def _comm_begin(schedule, cin_ref, cout_ref, sems, grid):
    begin, end = schedule(cin_ref, cout_ref, *sems)
    ids = [pl.program_id(d) for d in range(len(grid))]
    pl.when(functools.reduce(jnp.logical_and, [p == 0 for p in ids]))(begin)

    def comm_end():
        pl.when(functools.reduce(jnp.logical_and, [p == g - 1 for p, g in zip(ids, grid)]))(end)

    return comm_end


def _call_with_comm(body, name, grid, in_specs, out_specs, out_shape, args, comm):
    if comm is None:
        return pl.pallas_call(body, name=name, grid=grid, in_specs=in_specs, out_specs=out_specs, out_shape=out_shape,
                              compiler_params=_params("parallel", "arbitrary"))(*args)
    _, xs, cshape = comm
    anyspec = pl.BlockSpec(memory_space=pl.ANY)
    return pl.pallas_call(
        body, name=name, grid=grid, in_specs=in_specs + [anyspec], out_specs=out_specs + [anyspec],
        out_shape=out_shape + [jax.ShapeDtypeStruct(cshape, xs.dtype)], scratch_shapes=_comm_scratch(),
        compiler_params=_params("arbitrary", "arbitrary"),
    )(*args, xs)


def _attn_fwd(q, k, v, umat, *, tq, tk, name, comm=None):
    H, S, hd = q.shape

    def body(*refs):
        if comm is None:
            q_ref, k_ref, v_ref, u_ref, o_ref = refs
        else:
            q_ref, k_ref, v_ref, u_ref, cin_ref, o_ref, cout_ref, *sems = refs
            comm_end = _comm_begin(comm[0], cin_ref, cout_ref, sems, (H, S // tq))
        i = pl.program_id(1)
        qb = q_ref[0]
        um = u_ref[...]
        qpos = lax.broadcasted_iota(jnp.int32, (tq, tk), 0) + i * tq
        kloc = lax.broadcasted_iota(jnp.int32, (tq, tk), 1)

        def step(j, carry, masked, row0=0):
            r, acc = carry
            ks = pl.multiple_of(j * tk, tk)
            kb = k_ref[0, pl.ds(ks, tk), :]
            vb = v_ref[0, pl.ds(ks, tk), :]
            z = lax.dot_general(qb[row0:], kb, _NT, preferred_element_type=F32)
            lb = jnp.minimum(z, 0.0) - jnp.log(1.0 + jnp.exp(-jnp.abs(z)))
            lm = lb - z
            if masked:
                m = (kloc[row0:] + j * tk) < qpos[row0:]
                lm = jnp.where(m, lm, 0.0)
            a = jnp.exp(lb + _block_sums(lm, um) + r)
            if masked:
                a = jnp.where(m, a, 0.0)
            acc = acc + jnp.dot(a.astype(BF16), vb, preferred_element_type=F32)
            return r + jnp.sum(lm, axis=1, keepdims=True), acc

        jd = (i * tq) // tk
        carry = (jnp.zeros((tq, 1), F32), jnp.zeros((tq, hd), F32))
        for dd in reversed(range(max(1, tq // tk))):
            carry = _diag_step(step, jd + dd, carry, dd * tk if tq > tk else 0)
        unroll = 2 if (tq // tk) % 2 == 0 else 1

        def trip(t, c):
            for s in range(unroll):
                c = step(jd - 1 - s - unroll * t, c, False)
            return c

        carry = lax.fori_loop(0, jd // unroll, trip, carry)
        o_ref[0] = carry[1]
        if comm is not None:
            comm_end()

    blk = pl.BlockSpec((1, tq, hd), lambda h, i: (h, i, 0))
    whole = pl.BlockSpec((1, S, hd), lambda h, i: (h, 0, 0))
    in_specs = [blk, whole, whole, pl.BlockSpec((tk, tk), lambda h, i: (0, 0))]
    out_specs, out_shape = [blk], [jax.ShapeDtypeStruct((H, S, hd), F32)]
    res = _call_with_comm(body, name, (H, S // tq), in_specs, out_specs, out_shape, (q, k, v, umat), comm)
    return res[0] if comm is None else res


def _attn_bwd(q, k, v, o, do, umat, *, tq, tk, name, comm=None):
    H, S, hd = q.shape

    def body(*refs):
        if comm is None:
            q_ref, k_ref, v_ref, o_ref, do_ref, u_ref, dq_ref, dk_ref, dv_ref = refs
        else:
            q_ref, k_ref, v_ref, o_ref, do_ref, u_ref, cin_ref, dq_ref, dk_ref, dv_ref, cout_ref, *sems = refs
            comm_end = _comm_begin(comm[0], cin_ref, cout_ref, sems, (H, S // tq))
        i = pl.program_id(1)

        @pl.when(i == 0)
        def _():
            dk_ref[...] = jnp.zeros_like(dk_ref)
            dv_ref[...] = jnp.zeros_like(dv_ref)

        qb = q_ref[0]
        do32 = do_ref[0]
        dob = do32.astype(BF16)
        tot = jnp.sum(dob.astype(F32) * o_ref[0], axis=1, keepdims=True)
        um = u_ref[...]
        qpos = lax.broadcasted_iota(jnp.int32, (tq, tk), 0) + i * tq
        kloc = lax.broadcasted_iota(jnp.int32, (tq, tk), 1)

        def step(j, carry, masked, row0=0):
            r, gs, dq = carry
            ks = pl.multiple_of(j * tk, tk)
            kb = k_ref[0, pl.ds(ks, tk), :]
            vb = v_ref[0, pl.ds(ks, tk), :]
            qs, dos = qb[row0:], dob[row0:]
            z = lax.dot_general(qs, kb, _NT, preferred_element_type=F32)
            lb = jnp.minimum(z, 0.0) - jnp.log(1.0 + jnp.exp(-jnp.abs(z)))
            sig = jnp.exp(lb)
            lm = lb - z
            if masked:
                m = (kloc[row0:] + j * tk) < qpos[row0:]
                lm = jnp.where(m, lm, 0.0)
            a = jnp.exp(lb + _block_sums(lm, um) + r)
            if masked:
                a = jnp.where(m, a, 0.0)
            ab = a.astype(BF16)
            g = lax.dot_general(dos, vb, _NT, preferred_element_type=F32) * ab.astype(F32)
            dz = g - sig * ((tot[row0:] - gs) - _block_sums(g, um, parts=2))
            if masked:
                dz = jnp.where(m, dz, 0.0)
            dzb = dz.astype(BF16)
            dq = dq + jnp.dot(dzb, kb, preferred_element_type=F32)
            dk_ref[0, pl.ds(ks, tk), :] += lax.dot_general(dzb, qs, _TN, preferred_element_type=F32)
            dv_ref[0, pl.ds(ks, tk), :] += lax.dot_general(ab, dos, _TN, preferred_element_type=F32)
            return r + jnp.sum(lm, axis=1, keepdims=True), gs + jnp.sum(g, axis=1, keepdims=True), dq

        jd = (i * tq) // tk
        zero = jnp.zeros((tq, 1), F32)
        carry = (zero, zero, jnp.zeros((tq, hd), F32))
        for dd in reversed(range(max(1, tq // tk))):
            carry = _diag_step(step, jd + dd, carry, dd * tk if tq > tk else 0)
        unroll = 2 if (tq // tk) % 2 == 0 else 1

        def trip(t, c):
            for s in range(unroll):
                c = step(jd - 1 - s - unroll * t, c, False)
            return c

        carry = lax.fori_loop(0, jd // unroll, trip, carry)
        dq_ref[0] = carry[2]
        if comm is not None:
            comm_end()

    blk = pl.BlockSpec((1, tq, hd), lambda h, i: (h, i, 0))
    whole = pl.BlockSpec((1, S, hd), lambda h, i: (h, 0, 0))
    shp = jax.ShapeDtypeStruct((H, S, hd), F32)
    in_specs = [blk, whole, whole, blk, blk, pl.BlockSpec((tk, tk), lambda h, i: (0, 0))]
    return _call_with_comm(body, name, (H, S // tq), in_specs, [blk, whole, whole], [shp, shp, shp],
                           (q, k, v, o, do, umat), comm)


def _merge_fwd(ya, yb, o, wb, P, b_gate, *, gate_col0, name):
    S, C = ya.shape
    H, _, hd = o.shape
    D = wb.shape[2]
    ts = _tile(S, 512, 8)

    def body(y0, y1, oh_ref, wb_ref, g0, g1, g2, bg_ref, m_ref, yc_ref):
        for h in range(H):
            yc_ref[:, h * hd:(h + 1) * hd] = oh_ref[h].astype(BF16)
        acc = jnp.zeros((ts, D), F32)
        for n, (y_ref, g_ref) in enumerate(((y0, g0), (y1, g1), (yc_ref, g2))):
            yd = jnp.dot(y_ref[...], wb_ref[n], preferred_element_type=F32)
            acc = acc + _sigmoid(g_ref[...] + bg_ref[:, n * D:(n + 1) * D]) * yd
        m_ref[...] = acc.astype(BF16)

    ysp = pl.BlockSpec((ts, C), lambda i: (i, 0))
    gsp = lambda n: pl.BlockSpec((ts, D), lambda i: (i, gate_col0 + n))
    return pl.pallas_call(
        body, name=name, grid=(S // ts,),
        in_specs=[ysp, ysp, pl.BlockSpec((H, ts, hd), lambda i: (0, i, 0)), pl.BlockSpec((3, C, D), lambda i: (0, 0, 0)),
                  gsp(0), gsp(1), gsp(2), pl.BlockSpec((1, 3 * D), lambda i: (0, 0))],
        out_specs=[pl.BlockSpec((ts, D), lambda i: (i, 0)), ysp],
        out_shape=[jax.ShapeDtypeStruct((S, D), BF16), jax.ShapeDtypeStruct((S, C), BF16)],
        compiler_params=_params("parallel"),
    )(ya, yb, o, wb, P, P, P, b_gate)


def _merge_bwd(ys, wb, wbT, P, b_gate, dmerged, *, gate_col0, hd, name):
    S, C = ys[0].shape
    D = wb.shape[2]
    H = C // hd
    ts = _tile(S, 256, 8)

    def body(y0, y1, y2, wb_ref, wbT_ref, g0, g1, g2, bg_ref, dm_ref,
             dg_ref, dyd0, dyd1, dyd2, dya_ref, dyb_ref, do_ref, dbg_ref):
        i = pl.program_id(0)
        dm = dm_ref[...]
        parts = []
        for n, (y_ref, g_ref, dyd_ref) in enumerate(((y0, g0, dyd0), (y1, g1, dyd1), (y2, g2, dyd2))):
            yd = jnp.dot(y_ref[...], wb_ref[n], preferred_element_type=F32)
            sg = _sigmoid(g_ref[...] + bg_ref[:, n * D:(n + 1) * D])
            dgate = dm * yd * sg * (1.0 - sg)
            dg_ref[:, n * D:(n + 1) * D] = dgate.astype(BF16)
            parts.append(jnp.sum(dgate, axis=0, keepdims=True))
            dyd = (dm * sg).astype(BF16)
            dyd_ref[...] = dyd
            dy = jnp.dot(dyd, wbT_ref[n], preferred_element_type=F32)
            if n == 0:
                dya_ref[...] = dy
            elif n == 1:
                dyb_ref[...] = dy
            else:
                for h in range(H):
                    do_ref[h] = dy[:, h * hd:(h + 1) * hd]
        part = jnp.concatenate(parts, axis=1)

        @pl.when(i == 0)
        def _():
            dbg_ref[...] = part

        @pl.when(i > 0)
        def _():
            dbg_ref[...] += part

    ysp = pl.BlockSpec((ts, C), lambda i: (i, 0))
    dsp = pl.BlockSpec((ts, D), lambda i: (i, 0))
    gsp = lambda n: pl.BlockSpec((ts, D), lambda i: (i, gate_col0 + n))
    dshp = jax.ShapeDtypeStruct((S, D), BF16)
    yshp = jax.ShapeDtypeStruct((S, C), F32)
    return pl.pallas_call(
        body, name=name, grid=(S // ts,),
        in_specs=[ysp, ysp, ysp, pl.BlockSpec((3, C, D), lambda i: (0, 0, 0)),
                  pl.BlockSpec((3, D, C), lambda i: (0, 0, 0)), gsp(0), gsp(1), gsp(2),
                  pl.BlockSpec((1, 3 * D), lambda i: (0, 0)), dsp],
        out_specs=[pl.BlockSpec((ts, 3 * D), lambda i: (i, 0)), dsp, dsp, dsp, ysp, ysp,
                   pl.BlockSpec((H, ts, hd), lambda i: (0, i, 0)), pl.BlockSpec((1, 3 * D), lambda i: (0, 0))],
        out_shape=[jax.ShapeDtypeStruct((S, 3 * D), BF16), dshp, dshp, dshp, yshp, yshp,
                   jax.ShapeDtypeStruct((H, S, hd), F32), jax.ShapeDtypeStruct((1, 3 * D), F32)],
        compiler_params=_params("arbitrary"),
    )(*ys, wb, wbT, P, P, P, b_gate, dmerged)


def _swiglu_fwd(gu, *, name):
    S, F2 = gu.shape
    F = F2 // 2
    ts, tf = _tile(S, 512, 8), _tile(F, 1536, LANES)
    nf = F // tf

    def body(g_ref, u_ref, o_ref):
        gt = g_ref[...]
        o_ref[...] = (gt * _sigmoid(gt) * u_ref[...]).astype(BF16)

    return pl.pallas_call(
        body, name=name, grid=(S // ts, nf),
        in_specs=[pl.BlockSpec((ts, tf), lambda i, j: (i, j)), pl.BlockSpec((ts, tf), lambda i, j: (i, j + nf))],
        out_specs=pl.BlockSpec((ts, tf), lambda i, j: (i, j)),
        out_shape=jax.ShapeDtypeStruct((S, F), BF16),
        compiler_params=_params("parallel", "parallel"),
    )(gu, gu)


def _swiglu_bwd(gu, dact, *, name):
    S, F2 = gu.shape
    F = F2 // 2
    ts = _tile(S, 256, 8)

    def body(gu_ref, d_ref, o_ref):
        gt, up, da = gu_ref[:, 0:F], gu_ref[:, F:F2], d_ref[...]
        sg = _sigmoid(gt)
        o_ref[:, 0:F] = (da * up * sg * (1.0 + gt * (1.0 - sg))).astype(BF16)
        o_ref[:, F:F2] = (da * gt * sg).astype(BF16)

    return pl.pallas_call(
        body, name=name, grid=(S // ts,),
        in_specs=[pl.BlockSpec((ts, F2), lambda i: (i, 0)), pl.BlockSpec((ts, F), lambda i: (i, 0))],
        out_specs=pl.BlockSpec((ts, F2), lambda i: (i, 0)),
        out_shape=jax.ShapeDtypeStruct((S, F2), BF16),
        compiler_params=_params("parallel"),
    )(gu, dact)


def _loss_grad(y, target, *, name):
    S, D = y.shape
    ts = _tile(S, 512, 8)

    def body(y_ref, t_ref, dy_ref, l_ref, dyb_ref):
        i = pl.program_id(0)
        err = y_ref[...] - t_ref[...]
        dy_ref[...] = err * (1.0 / D)
        dyb_ref[...] = (err * (1.0 / D)).astype(BF16)
        part = jnp.broadcast_to(jnp.sum(jnp.sum(err * err, axis=1, keepdims=True), axis=0, keepdims=True) * (0.5 / D),
                                (1, LANES))

        @pl.when(i == 0)
        def _():
            l_ref[...] = part

        @pl.when(i > 0)
        def _():
            l_ref[...] += part

    row = pl.BlockSpec((ts, D), lambda i: (i, 0))
    return pl.pallas_call(
        body, name=name, grid=(S // ts,), in_specs=[row, row],
        out_specs=[row, pl.BlockSpec((1, LANES), lambda i: (0, 0)), row],
        out_shape=[jax.ShapeDtypeStruct((S, D), F32), jax.ShapeDtypeStruct((1, LANES), F32),
                   jax.ShapeDtypeStruct((S, D), BF16)],
        compiler_params=_params("arbitrary"),
    )(y, target)


def _adamw(w, m, v, gs, *, name):
    R = w.shape[0]
    ns = gs.shape[0]
    tr = _tile(R, 2048, 16)
    c1 = 1.0 / (1.0 - ADAM_B1 ** ADAM_STEP)
    c2 = 1.0 / (1.0 - ADAM_B2 ** ADAM_STEP)

    def body(w_ref, m_ref, v_ref, gs_ref, g_ref, d_ref, nm_ref, nv_ref):
        g = gs_ref[0].astype(F32)
        for s in range(1, ns):
            g = g + gs_ref[s].astype(F32)
        nm = ADAM_B1 * m_ref[...] + (1.0 - ADAM_B1) * g
        nv = ADAM_B2 * v_ref[...] + (1.0 - ADAM_B2) * (g * g)
        g_ref[...] = g
        nm_ref[...] = nm
        nv_ref[...] = nv
        d_ref[...] = -ADAM_LR * ((nm * c1) / (jnp.sqrt(nv * c2) + ADAM_EPS) + ADAM_WD * w_ref[...])

    row = pl.BlockSpec((tr, LANES), lambda i: (i, 0))
    shp = jax.ShapeDtypeStruct((R, LANES), F32)
    return pl.pallas_call(
        body, name=name, grid=(R // tr,),
        in_specs=[row, row, row, pl.BlockSpec((ns, tr, LANES), lambda i: (0, i, 0))],
        out_specs=[row, row, row, row], out_shape=[shp, shp, shp, shp],
        compiler_params=_params("parallel"),
    )(w, m, v, gs)


def _slot_sum(gs, *, name):
    ns, R, _ = gs.shape
    tr = _tile(R, 2048, 16)

    def body(gs_ref, o_ref):
        g = gs_ref[0]
        for s in range(1, ns):
            g = g + gs_ref[s]
        o_ref[...] = g

    return pl.pallas_call(
        body, name=name, grid=(R // tr,),
        in_specs=[pl.BlockSpec((ns, tr, LANES), lambda i: (0, i, 0))],
        out_specs=pl.BlockSpec((tr, LANES), lambda i: (i, 0)),
        out_shape=jax.ShapeDtypeStruct((R, LANES), F32),
        compiler_params=_params("parallel"),
    )(gs)


def _comm_scratch():
    return [pltpu.SemaphoreType.DMA((7,)), pltpu.SemaphoreType.DMA((7,)), pltpu.SemaphoreType.DMA]


def _gather_schedule(x_ref, out_ref, send_sems, recv_sems, local_sem):
    x, y, c = lax.axis_index("x"), lax.axis_index("y"), lax.axis_index("c")
    me, sibling = (x, y, c), (x, y, 1 - c)
    chips = [(1 - x, y), (x, 1 - y), (1 - x, 1 - y)]

    def slot(px, py, pc):
        return out_ref.at[4 * px + 2 * py + pc]

    def copy(k, block, to, src=None):
        return pltpu.make_async_remote_copy(
            src_ref=slot(*block) if src is None else src, dst_ref=slot(*block),
            send_sem=send_sems.at[k], recv_sem=recv_sems.at[k], device_id=to, device_id_type=MESH)

    mine = pltpu.make_async_copy(x_ref, slot(*me), local_sem)
    first = [copy(0, me, sibling, src=x_ref)]
    first += [copy(1 + j, me, (*chip, c), src=x_ref) for j, chip in enumerate(chips)]
    passed = [copy(4 + j, (*chip, c), sibling) for j, chip in enumerate(chips)]

    def begin():
        mine.start()
        for cp in first:
            cp.start()

    def end():
        for j, chip in enumerate(chips):
            copy(1 + j, (*chip, c), me).wait_recv()
            passed[j].start()
        copy(0, sibling, me).wait_recv()
        for j, chip in enumerate(chips):
            copy(4 + j, (*chip, 1 - c), me).wait_recv()
        for cp in first + passed:
            cp.wait_send()
        mine.wait()

    return begin, end


def _comm_call(schedule, xs, out_shape, *, name):
    def body(x_ref, out_ref, send_sems, recv_sems, local_sem):
        begin, end = schedule(x_ref, out_ref, send_sems, recv_sems, local_sem)
        begin()
        end()

    return pl.pallas_call(
        body, name=name,
        in_specs=[pl.BlockSpec(memory_space=pl.ANY)], out_specs=pl.BlockSpec(memory_space=pl.ANY),
        out_shape=jax.ShapeDtypeStruct(out_shape, xs.dtype), scratch_shapes=_comm_scratch(),
        compiler_params=pltpu.CompilerParams(has_side_effects=True),
    )(xs)


def _all_gather(xs, *, name):
    return _comm_call(_gather_schedule, xs, (N_DEV,) + xs.shape, name=name)


def _exchange_schedule(x_ref, out_ref, send_sems, recv_sems, local_sem):
    x, y, c = lax.axis_index("x"), lax.axis_index("y"), lax.axis_index("c")
    me = 4 * x + 2 * y + c
    mine = pltpu.make_async_copy(x_ref.at[me], out_ref.at[me], local_sem)
    sends, recvs = [], []
    for k in range(1, N_DEV):
        px = 1 - x if k & 4 else x
        py = 1 - y if k & 2 else y
        pc = 1 - c if k & 1 else c
        peer = 4 * px + 2 * py + pc
        sends.append(pltpu.make_async_remote_copy(
            src_ref=x_ref.at[peer], dst_ref=out_ref.at[me], send_sem=send_sems.at[k - 1],
            recv_sem=recv_sems.at[k - 1], device_id=(px, py, pc), device_id_type=MESH))
        recvs.append(pltpu.make_async_remote_copy(
            src_ref=x_ref.at[me], dst_ref=out_ref.at[peer], send_sem=send_sems.at[k - 1],
            recv_sem=recv_sems.at[k - 1], device_id=(px, py, pc), device_id_type=MESH))

    def begin():
        mine.start()
        for cp in sends:
            cp.start()

    def end():
        for cp in recvs:
            cp.wait_recv()
        for cp in sends:
            cp.wait_send()
        mine.wait()

    return begin, end


PACK_ROWS = 16


def _pack_rows(parts, dtype, lead=()):
    rows = []
    for p in parts:
        r = p.reshape(lead + (-1, LANES)).astype(dtype)
        pad = (-r.shape[-2]) % PACK_ROWS
        rows.append(jnp.pad(r, [(0, 0)] * len(lead) + [(0, pad), (0, 0)]) if pad else r)
    return jnp.concatenate(rows, axis=len(lead))


def _unpack_rows(packed, shapes, lead=()):
    out, off = [], 0
    for shp in shapes:
        n = math.prod(shp) // LANES
        out.append(lax.slice_in_dim(packed, off, off + n, axis=len(lead)).reshape(lead + tuple(shp)))
        off += n + (-n) % PACK_ROWS
    return out


def _unshard(gathered, axis):
    g = jnp.moveaxis(gathered, 0, axis)
    shp = list(g.shape)
    shp[axis:axis + 2] = [shp[axis] * shp[axis + 1]]
    return g.reshape(shp)


def _reshard(full, axis):
    shp = list(full.shape)
    shp[axis:axis + 1] = [N_DEV, shp[axis] // N_DEV]
    return jnp.moveaxis(full.reshape(shp), axis, 0)


def _late_parts(L):
    return [(k, l) for k in BIG for l in range(L) if (k, l) != ("w_in", 0)]


def _local_step(x, target, w_in0, late_pack, late_shapes, sm):
    S, D = x.shape
    L = sm["mix_norm_g"].shape[0]
    late = _late_parts(L)
    wf = {k: [None] * L for k in BIG}
    wf["w_in"][0] = w_in0
    C = sm["conv_b"].shape[1]
    W = sm["sgu_ln_g"].shape[1]
    hd = sm["q_norm_g"].shape[1]
    G, CH = sm["sgu_w"].shape[1], sm["sgu_w"].shape[2]
    A = w_in0.shape[1] - (3 * C + 2 * W + 3 * D)
    A = A // 3
    H = A // hd
    col_q = 3 * C + 2 * W
    qscale = 1.0 / math.sqrt(hd)
    tril = jnp.tril(jnp.ones((CH, CH), F32))
    (ftq, ftk), (btq, btk) = [(_tile(S, a, LANES), _tile(S, b, LANES)) for a, b in (ATTN_FWD_TILES, ATTN_BWD_TILES)]
    umat = lambda t: (lax.broadcasted_iota(jnp.int32, (t, t), 0) > lax.broadcasted_iota(jnp.int32, (t, t), 1)).astype(BF16)

    saved = []
    for l in range(L):
        n = f"l{l}_"
        g1 = sm["mix_norm_g"][l][None]
        h = _rmsnorm_fwd(x, g1, scale=1.0, out_dtype=BF16, name=n + "mixnorm")
        P = _matmul(h, wf["w_in"][l], name=n + "w_in")
        cw = jnp.pad(sm["conv_w"][l], ((0, 5), (0, 0)))
        cb = sm["conv_b"][l][None]
        ya = _conv_fwd(P, cw, cb, C=C, name=n + "conv")
        wm = (sm["sgu_w"][l] * tril).astype(BF16)
        bT = sm["sgu_b"][l].T
        lng, lnb = sm["sgu_ln_g"][l][None], sm["sgu_ln_b"][l][None]
        yb = _sgu_fwd(P, lng, lnb, wm, bT, W=W, cu=(3 * C) // W, cv=(3 * C) // W + 1, name=n + "sgu")
        gq, gk = sm["q_norm_g"][l][None], sm["k_norm_g"][l][None]
        qn, kn, vh = _qkv_heads_fwd(P, gq, gk, A=A, col0=col_q // A, hd=hd, qscale=qscale, name=n + "qkv")
        if l == 0:
            o, gathered = _attn_fwd(qn, kn, vh, umat(ftk), tq=ftq, tk=ftk, name=n + "attn",
                                    comm=(_gather_schedule, late_pack, (N_DEV,) + late_pack.shape))
            for (k, ll), part in zip(late, _unpack_rows(gathered, late_shapes, lead=(N_DEV,))):
                wf[k][ll] = _unshard(part, SHARD_AXIS[k] - 1)
        else:
            o = _attn_fwd(qn, kn, vh, umat(ftk), tq=ftq, tk=ftk, name=n + "attn")
        bg = sm["b_gate"][l][None]
        gate_col0 = (col_q + 3 * A) // D
        merged, yc = _merge_fwd(ya, yb, o, wf["w_branch_out"][l], P, bg, gate_col0=gate_col0, name=n + "merge")
        x1 = _matmul(merged, wf["w_o"][l], res=x, name=n + "w_o")
        g2 = sm["ffn_norm_g"][l][None]
        h2 = _rmsnorm_fwd(x1, g2, scale=1.0, out_dtype=BF16, name=n + "ffnnorm")
        gu = _matmul(h2, wf["w_gate_up"][l], name=n + "w_gate_up")
        act = _swiglu_fwd(gu, name=n + "swiglu")
        x2 = _matmul(act, wf["w_down"][l], res=x1, name=n + "w_down")
        saved.append(dict(x=x, h=h, P=P, cw=cw, cb=cb, ya=ya, wm=wm, bT=bT, lng=lng, lnb=lnb, yb=yb,
                          gq=gq, gk=gk, qn=qn, kn=kn, vh=vh, o=o, yc=yc, bg=bg,
                          gate_col0=gate_col0, merged=merged, x1=x1, g1=g1, g2=g2, h2=h2, gu=gu, act=act))
        x = x2

    dx, lpart, dxb = _loss_grad(x, target, name="loss")
    grads = {k: [None] * L for k in WEIGHTS}
    chunked = lambda k, g: g if k == "w_in" else _reshard(g, SHARD_AXIS[k] - 1)
    for l in reversed(range(L)):
        n = f"l{l}_b_"
        sv = saved[l]
        grads["w_down"][l] = _matmul_tn(sv["act"], dxb, name=n + "g_w_down", out_dtype=BF16)
        dact = _matmul(dxb, wf["w_down"][l].T, name=n + "d_act")
        dgu = _swiglu_bwd(sv["gu"], dact, name=n + "swiglu")
        grads["w_gate_up"][l] = _matmul_tn(sv["h2"], dgu, name=n + "g_w_gate_up", out_dtype=BF16)
        dh2 = _matmul(dgu, wf["w_gate_up"][l].T, name=n + "d_h2")
        dx1, dg2, dx1b = _rmsnorm_bwd(sv["x1"], sv["g2"], dh2, scale=1.0, dres=dx, name=n + "ffnnorm",
                                      bf16_copy=True)
        grads["ffn_norm_g"][l] = dg2[0]
        grads["w_o"][l] = _matmul_tn(sv["merged"], dx1b, name=n + "g_w_o", out_dtype=BF16)
        dmerged = _matmul(dx1b, wf["w_o"][l].T, name=n + "d_merged")
        ys = (sv["ya"], sv["yb"], sv["yc"])
        wb = wf["w_branch_out"][l]
        dgates, *dyd, dya, dyb, do, dbg = _merge_bwd(ys, wb, wb.transpose(0, 2, 1), sv["P"], sv["bg"], dmerged,
                                                     gate_col0=sv["gate_col0"], hd=hd, name=n + "merge")
        grads["b_gate"][l] = dbg[0]
        grads["w_branch_out"][l] = jnp.stack(
            [_matmul_tn(ys[i], dyd[i], name=n + f"g_w_branch{i}", out_dtype=BF16) for i in range(3)])
        dconv, dcw = _conv_bwd(sv["P"], dya, sv["cw"], sv["cb"], C=C, name=n + "conv")
        grads["conv_w"][l], grads["conv_b"][l] = dcw[0:3], dcw[3]
        wmT = sv["wm"].transpose(0, 2, 1)
        dsgu, dsw, dsb, dln = _sgu_bwd(sv["P"], dyb, sv["lng"], sv["lnb"], sv["wm"], wmT, sv["bT"], W=W,
                                       cu=(3 * C) // W, cv=(3 * C) // W + 1, name=n + "sgu")
        grads["sgu_w"][l], grads["sgu_b"][l] = dsw * tril, dsb[:, :, 0]
        grads["sgu_ln_g"][l], grads["sgu_ln_b"][l] = dln[0], dln[1]
        if l == 0:
            chunks = _pack_rows([chunked(k, grads[k][ll]) for k, ll in late], BF16, lead=(N_DEV,))
            dqn, dkn, dvh, late_recv = _attn_bwd(sv["qn"], sv["kn"], sv["vh"], sv["o"], do, umat(btk), tq=btq, tk=btk,
                                                 name=n + "attn", comm=(_exchange_schedule, chunks, chunks.shape))
        else:
            dqn, dkn, dvh = _attn_bwd(sv["qn"], sv["kn"], sv["vh"], sv["o"], do, umat(btk), tq=btq, tk=btk,
                                      name=n + "attn")
        dqkv, dgq, dgk = _qkv_heads_bwd(sv["P"], sv["gq"], sv["gk"], dqn, dkn, dvh, A=A, col0=col_q // A, hd=hd,
                                        qscale=qscale, name=n + "qkv")
        grads["q_norm_g"][l], grads["k_norm_g"][l] = dgq[0], dgk[0]
        dP = jnp.concatenate([dconv, dsgu, dqkv, dgates], axis=1)
        grads["w_in"][l] = _matmul_tn(sv["h"], dP, name=n + "g_w_in", out_dtype=BF16, shard=dP.shape[1] // N_DEV)
        if l == 0:
            chunks = _pack_rows([grads["w_in"][0]], BF16, lead=(N_DEV,))
            dh, early_recv = _matmul(dP, wf["w_in"][l].T, name=n + "d_h",
                                     comm=(_exchange_schedule, chunks, chunks.shape))
        else:
            dh = _matmul(dP, wf["w_in"][l].T, name=n + "d_h")
        dx, dg1, dxb = _rmsnorm_bwd(sv["x"], sv["g1"], dh, scale=1.0, dres=dx1, name=n + "mixnorm", bf16_copy=True)
        grads["mix_norm_g"][l] = dg1[0]
    return lpart[0, 0], dx, grads, early_recv, late_recv


def kernel(x, mix_norm_g, w_in, b_gate, conv_w, conv_b, sgu_ln_g, sgu_ln_b, sgu_w, sgu_b, q_norm_g, k_norm_g, w_branch_out, w_o, ffn_norm_g, w_gate_up, w_down, loss_target, m_mix_norm_g, m_w_in, m_b_gate, m_conv_w, m_conv_b, m_sgu_ln_g, m_sgu_ln_b, m_sgu_w, m_sgu_b, m_q_norm_g, m_k_norm_g, m_w_branch_out, m_w_o, m_ffn_norm_g, m_w_gate_up, m_w_down, v_mix_norm_g, v_w_in, v_b_gate, v_conv_w, v_conv_b, v_sgu_ln_g, v_sgu_ln_b, v_sgu_w, v_sgu_b, v_q_norm_g, v_k_norm_g, v_w_branch_out, v_w_o, v_ffn_norm_g, v_w_gate_up, v_w_down):
    w = dict(mix_norm_g=mix_norm_g, w_in=w_in, b_gate=b_gate, conv_w=conv_w, conv_b=conv_b, sgu_ln_g=sgu_ln_g,
             sgu_ln_b=sgu_ln_b, sgu_w=sgu_w, sgu_b=sgu_b, q_norm_g=q_norm_g, k_norm_g=k_norm_g,
             w_branch_out=w_branch_out, w_o=w_o, ffn_norm_g=ffn_norm_g, w_gate_up=w_gate_up, w_down=w_down)
    m = dict(mix_norm_g=m_mix_norm_g, w_in=m_w_in, b_gate=m_b_gate, conv_w=m_conv_w, conv_b=m_conv_b,
             sgu_ln_g=m_sgu_ln_g, sgu_ln_b=m_sgu_ln_b, sgu_w=m_sgu_w, sgu_b=m_sgu_b, q_norm_g=m_q_norm_g,
             k_norm_g=m_k_norm_g, w_branch_out=m_w_branch_out, w_o=m_w_o, ffn_norm_g=m_ffn_norm_g,
             w_gate_up=m_w_gate_up, w_down=m_w_down)
    v = dict(mix_norm_g=v_mix_norm_g, w_in=v_w_in, b_gate=v_b_gate, conv_w=v_conv_w, conv_b=v_conv_b,
             sgu_ln_g=v_sgu_ln_g, sgu_ln_b=v_sgu_ln_b, sgu_w=v_sgu_w, sgu_b=v_sgu_b, q_norm_g=v_q_norm_g,
             k_norm_g=v_k_norm_g, w_branch_out=v_w_branch_out, w_o=v_w_o, ffn_norm_g=v_ffn_norm_g,
             w_gate_up=v_w_gate_up, w_down=v_w_down)
    me = 4 * lax.axis_index("x") + 2 * lax.axis_index("y") + lax.axis_index("c")
    S = x.shape[1]

    L = w_in.shape[0]
    late = _late_parts(L)
    late_shapes = [w[k].shape[1:] for k, _ in late]
    w_in0 = _unshard(_all_gather(_pack_rows([w_in[0]], BF16), name="gather_w_in0").reshape((N_DEV,) + w_in.shape[1:]),
                     SHARD_AXIS["w_in"] - 1)
    late_pack = _pack_rows([w[k][l] for k, l in late], BF16)
    conv_g = _all_gather(_pack_rows([conv_w], F32), name="gather_conv_w")
    sm = {k: w[k] for k in SMALL}
    sm["conv_w"] = _unshard(_unpack_rows(conv_g, [conv_w.shape], lead=(N_DEV,))[0], 2)

    lpart, dx, grads, early_recv, late_recv = _local_step(x[0], loss_target[0], w_in0, late_pack, late_shapes, sm)
    loss = lax.psum(lpart, ("x", "y", "c"))

    pk = lambda t, parts: _pack_rows([t[k][l] for k, l in parts], F32)
    early = [("w_in", 0)]
    res_e = _adamw(pk(w, early), pk(m, early), pk(v, early), early_recv, name="adamw_w_in0")
    res_l = _adamw(pk(w, late), pk(m, late), pk(v, late), late_recv, name="adamw_late")
    out = {}
    whole = [k for k in BIG if k != "w_in"]
    for i, nm in enumerate(("grad", "delta", "new_m", "new_v")):
        w_in_rest, *rest = _unpack_rows(res_l[i], [(L - 1,) + w_in.shape[1:]] + [w[k].shape for k in whole])
        out[nm, "w_in"] = jnp.concatenate([res_e[i].reshape((1,) + w_in.shape[1:]), w_in_rest], axis=0)
        out.update({(nm, k): a for k, a in zip(whole, rest)})

    small_grads = [jnp.stack(grads[k]) for k in SMALL]
    small_shapes = [g.shape for g in small_grads]
    sg = _all_gather(_pack_rows(small_grads, F32), name="gather_small_grads")
    gsum = _slot_sum(sg, name="sum_small_grads")
    gsmall = dict(zip(SMALL, _unpack_rows(gsum, small_shapes)))
    cshard = conv_w.shape[2]
    gsmall["conv_w"] = lax.dynamic_slice_in_dim(gsmall["conv_w"], me * cshard, cshard, axis=2)
    own_shapes = [w[k].shape for k in SMALL]
    gs_, ds_, ms_, vs_ = _adamw(_pack_rows([w[k] for k in SMALL], F32), _pack_rows([m[k] for k in SMALL], F32),
                                _pack_rows([v[k] for k in SMALL], F32),
                                _pack_rows([gsmall[k] for k in SMALL], F32)[None], name="adamw_small")
    for nm, packed in (("grad", gs_), ("delta", ds_), ("new_m", ms_), ("new_v", vs_)):
        for k, a in zip(SMALL, _unpack_rows(packed, own_shapes)):
            out[nm, k] = a

    res = [loss, dx[None]]
    for nm in ("grad", "delta", "new_m", "new_v"):
        res += [out[nm, k] for k in WEIGHTS]
    return tuple(res)
```

```python
import functools
import math

import jax
import jax.numpy as jnp
from jax import lax
from jax.experimental import pallas as pl
from jax.experimental.pallas import tpu as pltpu

F32 = jnp.float32
BF16 = jnp.bfloat16
MESH = pl.DeviceIdType.MESH

N_DEV = 8
LANES = 128
VMEM_LIMIT_BYTES = 56 * 1024 * 1024
EPS = 1e-6
ADAM_LR, ADAM_B1, ADAM_B2, ADAM_EPS, ADAM_WD, ADAM_STEP = 0.001, 0.9, 0.999, 1e-08, 0.01, 10
ATTN_FWD_TILES = (512, 512)
ATTN_BWD_TILES = (1024, 256)
ATTN_FWD_UNROLL = 2
ATTN_BWD_UNROLL = 4
BIG = ("w_in", "w_branch_out", "w_o", "w_gate_up", "w_down")
SMALL = ("mix_norm_g", "b_gate", "conv_w", "conv_b", "sgu_ln_g", "sgu_ln_b", "sgu_w", "sgu_b",
         "q_norm_g", "k_norm_g", "ffn_norm_g")
WEIGHTS = ("mix_norm_g", "w_in", "b_gate", "conv_w", "conv_b", "sgu_ln_g", "sgu_ln_b", "sgu_w", "sgu_b",
           "q_norm_g", "k_norm_g", "w_branch_out", "w_o", "ffn_norm_g", "w_gate_up", "w_down")
SHARD_AXIS = {"w_in": 2, "w_branch_out": 3, "w_o": 1, "w_gate_up": 2, "w_down": 1}


def _tile(n, cap, mult):
    best = None
    for t in range(mult, min(n, cap) + 1, mult):
        if n % t == 0:
            best = t
    return best if best is not None else n


def _params(*sem):
    return pltpu.CompilerParams(dimension_semantics=sem if sem else None, vmem_limit_bytes=VMEM_LIMIT_BYTES)


def _erf(x):
    return lax.erf(x)


def _gelu(x):
    return 0.5 * x * (1.0 + _erf(x * (1.0 / math.sqrt(2.0))))


def _gelu_grad(x):
    return 0.5 * (1.0 + _erf(x * (1.0 / math.sqrt(2.0)))) + x * jnp.exp(-0.5 * x * x) * (1.0 / math.sqrt(2.0 * math.pi))


def _sigmoid(x):
    return 1.0 / (1.0 + jnp.exp(-x))


def _matmul(a, b, *, name, res=None, out_dtype=F32, comm=None):
    M, K = a.shape
    _, N = b.shape
    tm, tn, tk = _tile(M, 1024, 8), _tile(N, 1536, LANES), _tile(K, 1536, LANES)
    nk = K // tk
    grid = (M // tm, N // tn, nk)
    has_res = res is not None

    def body(*refs):
        refs = list(refs)
        a_ref, b_ref = refs[:2]
        r_ref = refs[2] if has_res else None
        pos = 2 + has_res
        if comm is not None:
            cin_ref, o_ref, cout_ref = refs[pos:pos + 3]
            pos += 3
        else:
            o_ref = refs[pos]
            pos += 1
        acc = refs[pos] if nk > 1 else None
        if comm is not None:
            comm_end = _comm_begin(comm[0], cin_ref, cout_ref, refs[pos + (nk > 1):], grid)
        k = pl.program_id(2)
        part = jnp.dot(a_ref[...], b_ref[...], preferred_element_type=F32)

        def finish(v):
            if has_res:
                v = v + r_ref[...]
            o_ref[...] = v.astype(out_dtype)

        if nk == 1:
            finish(part)
        else:
            @pl.when(k == 0)
            def _():
                acc[...] = part

            @pl.when(jnp.logical_and(k > 0, k < nk - 1))
            def _():
                acc[...] += part

            @pl.when(k == nk - 1)
            def _():
                finish(acc[...] + part)

        if comm is not None:
            comm_end()

    in_specs = [pl.BlockSpec((tm, tk), lambda i, j, k: (i, k)), pl.BlockSpec((tk, tn), lambda i, j, k: (k, j))]
    args = [a, b]
    if has_res:
        in_specs.append(pl.BlockSpec((tm, tn), lambda i, j, k: (i, j)))
        args.append(res)
    out_specs = [pl.BlockSpec((tm, tn), lambda i, j, k: (i, j))]
    out_shape = [jax.ShapeDtypeStruct((M, N), out_dtype)]
    scratch = [pltpu.VMEM((tm, tn), F32)] if nk > 1 else []
    sem = ("parallel", "parallel", "arbitrary")
    if comm is not None:
        anyspec = pl.BlockSpec(memory_space=pl.ANY)
        in_specs.append(anyspec)
        args.append(comm[1])
        out_specs.append(anyspec)
        out_shape.append(jax.ShapeDtypeStruct(comm[2], comm[1].dtype))
        scratch += _comm_scratch()
        sem = ("arbitrary",) * 3
    out = pl.pallas_call(body, name=name, grid=grid, in_specs=in_specs, out_specs=out_specs, out_shape=out_shape,
                         scratch_shapes=scratch, compiler_params=_params(*sem))(*args)
    return out[0] if comm is None else out


def _matmul_tn(x, y, *, name, out_dtype=F32, shard=None):
    S, A = x.shape
    _, B = y.shape
    ta, ts = _tile(A, 1024, LANES), _tile(S, 1024, 8)
    tb = shard if shard else _tile(B, 1536, LANES)
    ns = S // ts
    direct = out_dtype == F32
    view = (lambda r: r.at[0]) if shard else (lambda r: r)

    def body(x_ref, y_ref, o_ref, *scratch):
        s = pl.program_id(2)
        out = view(o_ref)
        acc = out if direct else scratch[0]
        part = lax.dot_general(x_ref[...], y_ref[...], (((0,), (0,)), ((), ())), preferred_element_type=F32)

        @pl.when(s == 0)
        def _():
            acc[...] = part

        @pl.when(s > 0)
        def _():
            acc[...] += part

        if not direct:
            @pl.when(s == ns - 1)
            def _():
                out[...] = acc[...].astype(out_dtype)

    if shard:
        out_spec, shape = pl.BlockSpec((1, ta, tb), lambda i, j, s: (j, i, 0)), (B // tb, A, tb)
    else:
        out_spec, shape = pl.BlockSpec((ta, tb), lambda i, j, s: (i, j)), (A, B)
    return pl.pallas_call(
        body, name=name, grid=(A // ta, B // tb, ns),
        in_specs=[pl.BlockSpec((ts, ta), lambda i, j, s: (s, i)), pl.BlockSpec((ts, tb), lambda i, j, s: (s, j))],
        out_specs=out_spec, out_shape=jax.ShapeDtypeStruct(shape, out_dtype),
        scratch_shapes=[] if direct else [pltpu.VMEM((ta, tb), F32)],
        compiler_params=_params("parallel", "parallel", "arbitrary"),
    )(x, y)


def _rmsnorm_fwd(x, g, *, scale, out_dtype, name):
    R, W = x.shape
    tr = _tile(R, 512 if W >= 512 else 4096, 8)

    def body(x_ref, g_ref, o_ref):
        xv = x_ref[...]
        r = lax.rsqrt(jnp.mean(xv * xv, axis=1, keepdims=True) + EPS)
        o_ref[...] = (xv * r * (g_ref[...] * scale)).astype(out_dtype)

    return pl.pallas_call(
        body, name=name, grid=(R // tr,),
        in_specs=[pl.BlockSpec((tr, W), lambda i: (i, 0)), pl.BlockSpec((1, W), lambda i: (0, 0))],
        out_specs=pl.BlockSpec((tr, W), lambda i: (i, 0)),
        out_shape=jax.ShapeDtypeStruct((R, W), out_dtype),
        compiler_params=_params("parallel"),
    )(x, g)


def _rmsnorm_bwd(x, g, dy, *, scale, name, dres=None, out_dtype=F32, bf16_copy=False):
    R, W = x.shape
    tr = _tile(R, 512 if W >= 512 else 4096, 8)
    has_res = dres is not None

    def body(*refs):
        refs = list(refs)
        dxb_ref = refs.pop() if bf16_copy else None
        if has_res:
            x_ref, g_ref, dy_ref, dres_ref, dx_ref, dg_ref = refs
        else:
            x_ref, g_ref, dy_ref, dx_ref, dg_ref = refs
        i = pl.program_id(0)
        xv = x_ref[...]
        dyv = dy_ref[...].astype(F32) * scale
        r = lax.rsqrt(jnp.mean(xv * xv, axis=1, keepdims=True) + EPS)
        u = dyv * g_ref[...]
        dx = r * u - xv * (r * r * r * jnp.mean(u * xv, axis=1, keepdims=True))
        if has_res:
            dx = dx + dres_ref[...]
        dx_ref[...] = dx.astype(out_dtype)
        if bf16_copy:
            dxb_ref[...] = dx.astype(BF16)
        part = jnp.sum(dyv * xv * r, axis=0, keepdims=True)

        @pl.when(i == 0)
        def _():
            dg_ref[...] = part

        @pl.when(i > 0)
        def _():
            dg_ref[...] += part

    row = pl.BlockSpec((tr, W), lambda i: (i, 0))
    one = pl.BlockSpec((1, W), lambda i: (0, 0))
    in_specs = [row, one, row] + ([row] if has_res else [])
    args = [x, g, dy] + ([dres] if has_res else [])
    extra = bool(bf16_copy)
    return pl.pallas_call(
        body, name=name, grid=(R // tr,), in_specs=in_specs, out_specs=[row, one] + [row] * extra,
        out_shape=[jax.ShapeDtypeStruct((R, W), out_dtype), jax.ShapeDtypeStruct((1, W), F32)]
        + [jax.ShapeDtypeStruct((R, W), BF16)] * extra,
        compiler_params=_params("arbitrary"),
    )(*args)


def _shift_down(u, prev, n):
    ts = u.shape[0]
    out = pltpu.roll(u, n, 0)
    row = lax.broadcasted_iota(jnp.int32, u.shape, 0)
    for r in range(n):
        out = jnp.where(row == r, prev[8 - n + r:8 - n + r + 1, :], out)
    return out


def _shift_up(u, nxt, n):
    ts = u.shape[0]
    out = pltpu.roll(u, ts - n, 0)
    row = lax.broadcasted_iota(jnp.int32, u.shape, 0)
    for r in range(n):
        out = jnp.where(row == ts - n + r, nxt[r:r + 1, :], out)
    return out


def _conv_fwd(P, conv_w, conv_b, *, C, name):
    S = P.shape[0]
    ts = _tile(S, 512, 8)
    hb = ts // 8

    def body(ab_ref, ac_ref, ax_ref, pc_ref, px_ref, w_ref, b_ref, o_ref):
        i = pl.program_id(0)
        u = ac_ref[...] * ax_ref[...]
        prev = pc_ref[...] * px_ref[...] * (i > 0).astype(F32)
        w = w_ref[...]
        y = b_ref[...] + w[0:1, :] * _shift_down(u, prev, 2) + w[1:2, :] * _shift_down(u, prev, 1) + w[2:3, :] * u
        o_ref[...] = (ab_ref[...] * y).astype(BF16)

    cur = lambda c: pl.BlockSpec((ts, C), lambda i: (i, c))
    prv = lambda c: pl.BlockSpec((8, C), lambda i: (jnp.maximum(i * hb - 1, 0), c))
    return pl.pallas_call(
        body, name=name, grid=(S // ts,),
        in_specs=[cur(0), cur(1), cur(2), prv(1), prv(2),
                  pl.BlockSpec((8, C), lambda i: (0, 0)), pl.BlockSpec((1, C), lambda i: (0, 0))],
        out_specs=pl.BlockSpec((ts, C), lambda i: (i, 0)),
        out_shape=jax.ShapeDtypeStruct((S, C), BF16),
        compiler_params=_params("parallel"),
    )(P, P, P, P, P, conv_w, conv_b)


def _conv_bwd(P, dya, conv_w, conv_b, *, C, name):
    S = P.shape[0]
    ts = _tile(S, 512, 8)
    hb = ts // 8
    last = S // 8 - 1
    n = S // ts

    def body(ab_ref, ac_ref, ax_ref, pc_ref, px_ref, dy_ref, nab_ref, ndy_ref, w_ref, b_ref, o_ref, dw_ref):
        i = pl.program_id(0)
        ab, ac, ax = ab_ref[...], ac_ref[...], ax_ref[...]
        u = ac * ax
        prev = pc_ref[...] * px_ref[...] * (i > 0).astype(F32)
        w = w_ref[...]
        u1, u2 = _shift_down(u, prev, 1), _shift_down(u, prev, 2)
        y = b_ref[...] + w[0:1, :] * u2 + w[1:2, :] * u1 + w[2:3, :] * u
        dya_v = dy_ref[...]
        dyp = dya_v * ab
        nxt = ndy_ref[...] * nab_ref[...] * (i < n - 1).astype(F32)
        du = w[2:3, :] * dyp + w[1:2, :] * _shift_up(dyp, nxt, 1) + w[0:1, :] * _shift_up(dyp, nxt, 2)
        o_ref[:, 0:C] = (dya_v * y).astype(BF16)
        o_ref[:, C:2 * C] = (du * ax).astype(BF16)
        o_ref[:, 2 * C:3 * C] = (du * ac).astype(BF16)
        part = jnp.concatenate([
            jnp.sum(dyp * u2, axis=0, keepdims=True), jnp.sum(dyp * u1, axis=0, keepdims=True),
            jnp.sum(dyp * u, axis=0, keepdims=True), jnp.sum(dyp, axis=0, keepdims=True),
            jnp.zeros((4, C), F32)], axis=0)

        @pl.when(i == 0)
        def _():
            dw_ref[...] = part

        @pl.when(i > 0)
        def _():
            dw_ref[...] += part

    cur = lambda c: pl.BlockSpec((ts, C), lambda i: (i, c))
    prv = lambda c: pl.BlockSpec((8, C), lambda i: (jnp.maximum(i * hb - 1, 0), c))
    nxt = lambda c: pl.BlockSpec((8, C), lambda i: (jnp.minimum((i + 1) * hb, last), c))
    return pl.pallas_call(
        body, name=name, grid=(n,),
        in_specs=[cur(0), cur(1), cur(2), prv(1), prv(2), cur(0), nxt(0), nxt(0),
                  pl.BlockSpec((8, C), lambda i: (0, 0)), pl.BlockSpec((1, C), lambda i: (0, 0))],
        out_specs=[pl.BlockSpec((ts, 3 * C), lambda i: (i, 0)), pl.BlockSpec((8, C), lambda i: (0, 0))],
        out_shape=[jax.ShapeDtypeStruct((S, 3 * C), BF16), jax.ShapeDtypeStruct((8, C), F32)],
        compiler_params=_params("arbitrary"),
    )(P, P, P, P, P, dya, P, dya, conv_w, conv_b)


def _sgu_fwd(P, ln_g, ln_b, wm, bT, *, W, cu, cv, name):
    S = P.shape[0]
    G, CH, _ = wm.shape
    gw = W // G
    ts = _tile(S, 512, CH)

    def body(u_ref, v_ref, g_ref, b_ref, wm_ref, bT_ref, o_ref):
        gv = _gelu(v_ref[...])
        mu = jnp.mean(gv, axis=1, keepdims=True)
        xc = gv - mu
        vn = (xc * lax.rsqrt(jnp.mean(xc * xc, axis=1, keepdims=True) + EPS) * g_ref[...] + b_ref[...]).astype(BF16)
        bT_v = bT_ref[...]
        for c in range(ts // CH):
            rows = slice(c * CH, (c + 1) * CH)
            for g in range(G):
                cols = slice(g * gw, (g + 1) * gw)
                mixed = jnp.dot(wm_ref[g], vn[rows, cols], preferred_element_type=F32) + bT_v[:, g:g + 1]
                o_ref[rows, cols] = (_gelu(u_ref[rows, cols]) * mixed).astype(BF16)

    full = lambda shp: pl.BlockSpec(shp, lambda i: (0,) * len(shp))
    return pl.pallas_call(
        body, name=name, grid=(S // ts,),
        in_specs=[pl.BlockSpec((ts, W), lambda i: (i, cu)), pl.BlockSpec((ts, W), lambda i: (i, cv)),
                  full((1, W)), full((1, W)), full((G, CH, CH)), full((CH, G))],
        out_specs=pl.BlockSpec((ts, W), lambda i: (i, 0)),
        out_shape=jax.ShapeDtypeStruct((S, W), BF16),
        compiler_params=_params("parallel"),
    )(P, P, ln_g, ln_b, wm, bT)


def _sgu_bwd(P, dyb, ln_g, ln_b, wm, wmT, bT, *, W, cu, cv, name):
    S = P.shape[0]
    G, CH, _ = wm.shape
    gw = W // G
    ts = _tile(S, 512, CH)

    def body(u_ref, v_ref, dy_ref, g_ref, b_ref, wm_ref, wmT_ref, bT_ref, o_ref, dw_ref, db_ref, dln_ref, dvn_ref):
        i = pl.program_id(0)

        @pl.when(i == 0)
        def _():
            dw_ref[...] = jnp.zeros_like(dw_ref)
            db_ref[...] = jnp.zeros_like(db_ref)
            dln_ref[...] = jnp.zeros_like(dln_ref)

        sv = v_ref[...]
        gv = _gelu(sv)
        mu = jnp.mean(gv, axis=1, keepdims=True)
        xc = gv - mu
        rstd = lax.rsqrt(jnp.mean(xc * xc, axis=1, keepdims=True) + EPS)
        xhat = xc * rstd
        lg = g_ref[...]
        vn = (xhat * lg + b_ref[...]).astype(BF16)
        bT_v = bT_ref[...]
        for c in range(ts // CH):
            rows = slice(c * CH, (c + 1) * CH)
            for g in range(G):
                cols = slice(g * gw, (g + 1) * gw)
                vn_cg = vn[rows, cols]
                mixed = jnp.dot(wm_ref[g], vn_cg, preferred_element_type=F32) + bT_v[:, g:g + 1]
                su = u_ref[rows, cols]
                dyv = dy_ref[rows, cols]
                dmix = dyv * _gelu(su)
                o_ref[rows, cols] = (dyv * mixed * _gelu_grad(su)).astype(BF16)
                dmix_b = dmix.astype(BF16)
                dw_ref[g] += lax.dot_general(dmix_b, vn_cg, (((1,), (1,)), ((), ())), preferred_element_type=F32)
                db_ref[g] += jnp.broadcast_to(jnp.sum(dmix, axis=1, keepdims=True), (CH, CH))
                dvn_ref[rows, cols] = jnp.dot(wmT_ref[g], dmix_b, preferred_element_type=F32)
        dvn = dvn_ref[...]
        dxh = dvn * lg
        dgv = rstd * (dxh - jnp.mean(dxh, axis=1, keepdims=True) - xhat * jnp.mean(dxh * xhat, axis=1, keepdims=True))
        o_ref[:, W:2 * W] = (dgv * _gelu_grad(sv)).astype(BF16)
        dln_ref[0:1, :] += jnp.sum(dvn * xhat, axis=0, keepdims=True)
        dln_ref[1:2, :] += jnp.sum(dvn, axis=0, keepdims=True)

    full = lambda shp: pl.BlockSpec(shp, lambda i: (0,) * len(shp))
    return pl.pallas_call(
        body, name=name, grid=(S // ts,),
        in_specs=[pl.BlockSpec((ts, W), lambda i: (i, cu)), pl.BlockSpec((ts, W), lambda i: (i, cv)),
                  pl.BlockSpec((ts, W), lambda i: (i, 0)),
                  full((1, W)), full((1, W)), full((G, CH, CH)), full((G, CH, CH)), full((CH, G))],
        out_specs=[pl.BlockSpec((ts, 2 * W), lambda i: (i, 0)), full((G, CH, CH)), full((G, CH, CH)), full((8, W))],
        out_shape=[jax.ShapeDtypeStruct((S, 2 * W), BF16), jax.ShapeDtypeStruct((G, CH, CH), F32),
                   jax.ShapeDtypeStruct((G, CH, CH), F32), jax.ShapeDtypeStruct((8, W), F32)],
        scratch_shapes=[pltpu.VMEM((ts, W), F32)],
        compiler_params=_params("arbitrary"),
    )(P, P, dyb, ln_g, ln_b, wm, wmT, bT)


def _block_sums(x, u, parts=1):
    hi = x.astype(BF16)
    out = jnp.dot(hi, u, preferred_element_type=F32)
    if parts == 2:
        lo = (x - hi.astype(F32)).astype(BF16)
        out = out + jnp.dot(lo, u, preferred_element_type=F32)
    return out


_NT = (((1,), (1,)), ((), ()))
_TN = (((0,), (0,)), ((), ()))


def _left_blocks(step, jd, carry, unroll, jd_multiple):
    rem = 0
    if jd_multiple % unroll:
        rem = jd % unroll
        carry = lax.fori_loop(0, rem, lambda t, c: step(jd - 1 - t, c, False), carry)

    def trip(t, c):
        for s in range(unroll):
            c = step(jd - rem - 1 - s - unroll * t, c, False)
        return c

    return lax.fori_loop(0, jd // unroll, trip, carry)


def _diag_step(step, j, carry, row0):
    if row0 == 0:
        return step(j, carry, True)
    tail = step(j, tuple(c[row0:] for c in carry), True, row0)
    return tuple(jnp.concatenate([c[:row0], t], axis=0) for c, t in zip(carry, tail))


def _qkv_heads_fwd(P, gq, gk, *, A, col0, hd, qscale, name):
    S = P.shape[0]
    H = A // hd
    ts = _tile(S, 512, 8)

    def body(q_ref, k_ref, v_ref, gq_ref, gk_ref, qn_ref, kn_ref, vh_ref):
        for h in range(H):
            cols = slice(h * hd, (h + 1) * hd)
            for x_ref, g_ref, sc, o_ref in ((q_ref, gq_ref, qscale, qn_ref), (k_ref, gk_ref, 1.0, kn_ref)):
                xh = x_ref[:, cols]
                r = lax.rsqrt(jnp.mean(xh * xh, axis=1, keepdims=True) + EPS)
                o_ref[h] = (xh * r * (g_ref[...] * sc)).astype(BF16)
            vh_ref[h] = v_ref[:, cols].astype(BF16)

    col = lambda n: pl.BlockSpec((ts, A), lambda i: (i, col0 + n))
    gsp = pl.BlockSpec((1, hd), lambda i: (0, 0))
    hsp = pl.BlockSpec((H, ts, hd), lambda i: (0, i, 0))
    shp = jax.ShapeDtypeStruct((H, S, hd), BF16)
    return pl.pallas_call(
        body, name=name, grid=(S // ts,), in_specs=[col(0), col(1), col(2), gsp, gsp],
        out_specs=[hsp, hsp, hsp], out_shape=[shp, shp, shp], compiler_params=_params("parallel"),
    )(P, P, P, gq, gk)


def _qkv_heads_bwd(P, gq, gk, dqn, dkn, dvh, *, A, col0, hd, qscale, name):
    S = P.shape[0]
    H = A // hd
    ts = _tile(S, 512, 8)

    def body(q_ref, k_ref, gq_ref, gk_ref, dq_ref, dk_ref, dv_ref, o_ref, dgq_ref, dgk_ref):
        i = pl.program_id(0)
        parts = [jnp.zeros((1, hd), F32), jnp.zeros((1, hd), F32)]
        for h in range(H):
            for n, (x_ref, g_ref, sc, d_ref) in enumerate(((q_ref, gq_ref, qscale, dq_ref), (k_ref, gk_ref, 1.0, dk_ref))):
                xh = x_ref[:, h * hd:(h + 1) * hd]
                dyv = d_ref[h] * sc
                r = lax.rsqrt(jnp.mean(xh * xh, axis=1, keepdims=True) + EPS)
                u = dyv * g_ref[...]
                dx = r * u - xh * (r * r * r * jnp.mean(u * xh, axis=1, keepdims=True))
                o_ref[:, n * A + h * hd:n * A + (h + 1) * hd] = dx.astype(BF16)
                parts[n] = parts[n] + jnp.sum(dyv * xh * r, axis=0, keepdims=True)
            o_ref[:, 2 * A + h * hd:2 * A + (h + 1) * hd] = dv_ref[h].astype(BF16)

        @pl.when(i == 0)
        def _():
            dgq_ref[...] = parts[0]
            dgk_ref[...] = parts[1]

        @pl.when(i > 0)
        def _():
            dgq_ref[...] += parts[0]
            dgk_ref[...] += parts[1]

    col = lambda n: pl.BlockSpec((ts, A), lambda i: (i, col0 + n))
    gsp = pl.BlockSpec((1, hd), lambda i: (0, 0))
    hsp = pl.BlockSpec((H, ts, hd), lambda i: (0, i, 0))
    return pl.pallas_call(
        body, name=name, grid=(S // ts,), in_specs=[col(0), col(1), gsp, gsp, hsp, hsp, hsp],
        out_specs=[pl.BlockSpec((ts, 3 * A), lambda i: (i, 0)), gsp, gsp],
        out_shape=[jax.ShapeDtypeStruct((S, 3 * A), BF16), jax.ShapeDtypeStruct((1, hd), F32),
                   jax.ShapeDtypeStruct((1, hd), F32)],
        compiler_params=_params("arbitrary"),
    )(P, P, gq, gk, dqn, dkn, dvh)


def _comm_begin(schedule, cin_ref, cout_ref, sems, grid):
    begin, end = schedule(cin_ref, cout_ref, *sems)
    ids = [pl.program_id(d) for d in range(len(grid))]
    pl.when(functools.reduce(jnp.logical_and, [p == 0 for p in ids]))(begin)

    def comm_end():
        pl.when(functools.reduce(jnp.logical_and, [p == g - 1 for p, g in zip(ids, grid)]))(end)

    return comm_end


def _call_with_comm(body, name, grid, in_specs, out_specs, out_shape, args, comm):
    if comm is None:
        return pl.pallas_call(body, name=name, grid=grid, in_specs=in_specs, out_specs=out_specs, out_shape=out_shape,
                              compiler_params=_params("parallel", "arbitrary"))(*args)
    _, xs, cshape = comm
    anyspec = pl.BlockSpec(memory_space=pl.ANY)
    return pl.pallas_call(
        body, name=name, grid=grid, in_specs=in_specs + [anyspec], out_specs=out_specs + [anyspec],
        out_shape=out_shape + [jax.ShapeDtypeStruct(cshape, xs.dtype)], scratch_shapes=_comm_scratch(),
        compiler_params=_params("arbitrary", "arbitrary"),
    )(*args, xs)


def _attn_fwd(q, k, v, umat, *, tq, tk, name, comm=None):
    H, S, hd = q.shape

    def body(*refs):
        if comm is None:
            q_ref, k_ref, v_ref, u_ref, o_ref = refs
        else:
            q_ref, k_ref, v_ref, u_ref, cin_ref, o_ref, cout_ref, *sems = refs
            comm_end = _comm_begin(comm[0], cin_ref, cout_ref, sems, (H, S // tq))
        i = pl.program_id(1)
        qb = q_ref[0]
        um = u_ref[...]
        qpos = lax.broadcasted_iota(jnp.int32, (tq, tk), 0) + i * tq
        kloc = lax.broadcasted_iota(jnp.int32, (tq, tk), 1)

        def step(j, carry, masked, row0=0):
            r, acc = carry
            ks = pl.multiple_of(j * tk, tk)
            kb = k_ref[0, pl.ds(ks, tk), :]
            vb = v_ref[0, pl.ds(ks, tk), :]
            z = lax.dot_general(qb[row0:], kb, _NT, preferred_element_type=F32)
            lb = jnp.minimum(z, 0.0) - jnp.log(1.0 + jnp.exp(-jnp.abs(z)))
            lm = lb - z
            if masked:
                m = (kloc[row0:] + j * tk) < qpos[row0:]
                lm = jnp.where(m, lm, 0.0)
            a = jnp.exp(lb + _block_sums(lm, um) + r)
            if masked:
                a = jnp.where(m, a, 0.0)
            acc = acc + jnp.dot(a.astype(BF16), vb, preferred_element_type=F32)
            return r + jnp.sum(lm, axis=1, keepdims=True), acc

        jd = (i * tq) // tk
        carry = (jnp.zeros((tq, 1), F32), jnp.zeros((tq, hd), F32))
        for dd in reversed(range(max(1, tq // tk))):
            carry = _diag_step(step, jd + dd, carry, dd * tk if tq > tk else 0)
        carry = _left_blocks(step, jd, carry, ATTN_FWD_UNROLL, tq // tk)
        o_ref[0] = carry[1]
        if comm is not None:
            comm_end()

    blk = pl.BlockSpec((1, tq, hd), lambda h, i: (h, i, 0))
    whole = pl.BlockSpec((1, S, hd), lambda h, i: (h, 0, 0))
    in_specs = [blk, whole, whole, pl.BlockSpec((tk, tk), lambda h, i: (0, 0))]
    out_specs, out_shape = [blk], [jax.ShapeDtypeStruct((H, S, hd), F32)]
    res = _call_with_comm(body, name, (H, S // tq), in_specs, out_specs, out_shape, (q, k, v, umat), comm)
    return res[0] if comm is None else res


def _attn_bwd(q, k, v, o, do, umat, *, tq, tk, name, comm=None):
    H, S, hd = q.shape

    def body(*refs):
        if comm is None:
            q_ref, k_ref, v_ref, o_ref, do_ref, u_ref, dq_ref, dk_ref, dv_ref = refs
        else:
            q_ref, k_ref, v_ref, o_ref, do_ref, u_ref, cin_ref, dq_ref, dk_ref, dv_ref, cout_ref, *sems = refs
            comm_end = _comm_begin(comm[0], cin_ref, cout_ref, sems, (H, S // tq))
        i = pl.program_id(1)

        @pl.when(i == 0)
        def _():
            dk_ref[...] = jnp.zeros_like(dk_ref)
            dv_ref[...] = jnp.zeros_like(dv_ref)

        qb = q_ref[0]
        do32 = do_ref[0]
        dob = do32.astype(BF16)
        tot = jnp.sum(dob.astype(F32) * o_ref[0], axis=1, keepdims=True)
        um = u_ref[...]
        qpos = lax.broadcasted_iota(jnp.int32, (tq, tk), 0) + i * tq
        kloc = lax.broadcasted_iota(jnp.int32, (tq, tk), 1)

        def step(j, carry, masked, row0=0):
            r, gs, dq = carry
            ks = pl.multiple_of(j * tk, tk)
            kb = k_ref[0, pl.ds(ks, tk), :]
            vb = v_ref[0, pl.ds(ks, tk), :]
            qs, dos = qb[row0:], dob[row0:]
            z = lax.dot_general(qs, kb, _NT, preferred_element_type=F32)
            lb = jnp.minimum(z, 0.0) - jnp.log(1.0 + jnp.exp(-jnp.abs(z)))
            sig = jnp.exp(lb)
            lm = lb - z
            if masked:
                m = (kloc[row0:] + j * tk) < qpos[row0:]
                lm = jnp.where(m, lm, 0.0)
            a = jnp.exp(lb + _block_sums(lm, um) + r)
            if masked:
                a = jnp.where(m, a, 0.0)
            ab = a.astype(BF16)
            g = lax.dot_general(dos, vb, _NT, preferred_element_type=F32) * ab.astype(F32)
            dz = g - sig * ((tot[row0:] - gs) - _block_sums(g, um, parts=2))
            if masked:
                dz = jnp.where(m, dz, 0.0)
            dzb = dz.astype(BF16)
            dq = dq + jnp.dot(dzb, kb, preferred_element_type=F32)
            dk_ref[0, pl.ds(ks, tk), :] += lax.dot_general(dzb, qs, _TN, preferred_element_type=F32)
            dv_ref[0, pl.ds(ks, tk), :] += lax.dot_general(ab, dos, _TN, preferred_element_type=F32)
            return r + jnp.sum(lm, axis=1, keepdims=True), gs + jnp.sum(g, axis=1, keepdims=True), dq

        jd = (i * tq) // tk
        zero = jnp.zeros((tq, 1), F32)
        carry = (zero, zero, jnp.zeros((tq, hd), F32))
        for dd in reversed(range(max(1, tq // tk))):
            carry = _diag_step(step, jd + dd, carry, dd * tk if tq > tk else 0)
        carry = _left_blocks(step, jd, carry, ATTN_BWD_UNROLL, tq // tk)
        dq_ref[0] = carry[2]
        if comm is not None:
            comm_end()

    blk = pl.BlockSpec((1, tq, hd), lambda h, i: (h, i, 0))
    whole = pl.BlockSpec((1, S, hd), lambda h, i: (h, 0, 0))
    shp = jax.ShapeDtypeStruct((H, S, hd), F32)
    in_specs = [blk, whole, whole, blk, blk, pl.BlockSpec((tk, tk), lambda h, i: (0, 0))]
    return _call_with_comm(body, name, (H, S // tq), in_specs, [blk, whole, whole], [shp, shp, shp],
                           (q, k, v, o, do, umat), comm)


def _merge_fwd(ya, yb, o, wb, P, b_gate, *, gate_col0, name):
    S, C = ya.shape
    H, _, hd = o.shape
    D = wb.shape[2]
    ts = _tile(S, 512, 8)

    def body(y0, y1, oh_ref, wb_ref, g0, g1, g2, bg_ref, m_ref, yc_ref):
        for h in range(H):
            yc_ref[:, h * hd:(h + 1) * hd] = oh_ref[h].astype(BF16)
        acc = jnp.zeros((ts, D), F32)
        for n, (y_ref, g_ref) in enumerate(((y0, g0), (y1, g1), (yc_ref, g2))):
            yd = jnp.dot(y_ref[...], wb_ref[n], preferred_element_type=F32)
            acc = acc + _sigmoid(g_ref[...] + bg_ref[:, n * D:(n + 1) * D]) * yd
        m_ref[...] = acc.astype(BF16)

    ysp = pl.BlockSpec((ts, C), lambda i: (i, 0))
    gsp = lambda n: pl.BlockSpec((ts, D), lambda i: (i, gate_col0 + n))
    return pl.pallas_call(
        body, name=name, grid=(S // ts,),
        in_specs=[ysp, ysp, pl.BlockSpec((H, ts, hd), lambda i: (0, i, 0)), pl.BlockSpec((3, C, D), lambda i: (0, 0, 0)),
                  gsp(0), gsp(1), gsp(2), pl.BlockSpec((1, 3 * D), lambda i: (0, 0))],
        out_specs=[pl.BlockSpec((ts, D), lambda i: (i, 0)), ysp],
        out_shape=[jax.ShapeDtypeStruct((S, D), BF16), jax.ShapeDtypeStruct((S, C), BF16)],
        compiler_params=_params("parallel"),
    )(ya, yb, o, wb, P, P, P, b_gate)


def _merge_bwd(ys, wb, wbT, P, b_gate, dmerged, *, gate_col0, hd, name):
    S, C = ys[0].shape
    D = wb.shape[2]
    H = C // hd
    ts = _tile(S, 256, 8)

    def body(y0, y1, y2, wb_ref, wbT_ref, g0, g1, g2, bg_ref, dm_ref,
             dg_ref, dyd0, dyd1, dyd2, dya_ref, dyb_ref, do_ref, dbg_ref):
        i = pl.program_id(0)
        dm = dm_ref[...]
        parts = []
        for n, (y_ref, g_ref, dyd_ref) in enumerate(((y0, g0, dyd0), (y1, g1, dyd1), (y2, g2, dyd2))):
            yd = jnp.dot(y_ref[...], wb_ref[n], preferred_element_type=F32)
            sg = _sigmoid(g_ref[...] + bg_ref[:, n * D:(n + 1) * D])
            dgate = dm * yd * sg * (1.0 - sg)
            dg_ref[:, n * D:(n + 1) * D] = dgate.astype(BF16)
            parts.append(jnp.sum(dgate, axis=0, keepdims=True))
            dyd = (dm * sg).astype(BF16)
            dyd_ref[...] = dyd
            dy = jnp.dot(dyd, wbT_ref[n], preferred_element_type=F32)
            if n == 0:
                dya_ref[...] = dy
            elif n == 1:
                dyb_ref[...] = dy
            else:
                for h in range(H):
                    do_ref[h] = dy[:, h * hd:(h + 1) * hd]
        part = jnp.concatenate(parts, axis=1)

        @pl.when(i == 0)
        def _():
            dbg_ref[...] = part

        @pl.when(i > 0)
        def _():
            dbg_ref[...] += part

    ysp = pl.BlockSpec((ts, C), lambda i: (i, 0))
    dsp = pl.BlockSpec((ts, D), lambda i: (i, 0))
    gsp = lambda n: pl.BlockSpec((ts, D), lambda i: (i, gate_col0 + n))
    dshp = jax.ShapeDtypeStruct((S, D), BF16)
    yshp = jax.ShapeDtypeStruct((S, C), F32)
    return pl.pallas_call(
        body, name=name, grid=(S // ts,),
        in_specs=[ysp, ysp, ysp, pl.BlockSpec((3, C, D), lambda i: (0, 0, 0)),
                  pl.BlockSpec((3, D, C), lambda i: (0, 0, 0)), gsp(0), gsp(1), gsp(2),
                  pl.BlockSpec((1, 3 * D), lambda i: (0, 0)), dsp],
        out_specs=[pl.BlockSpec((ts, 3 * D), lambda i: (i, 0)), dsp, dsp, dsp, ysp, ysp,
                   pl.BlockSpec((H, ts, hd), lambda i: (0, i, 0)), pl.BlockSpec((1, 3 * D), lambda i: (0, 0))],
        out_shape=[jax.ShapeDtypeStruct((S, 3 * D), BF16), dshp, dshp, dshp, yshp, yshp,
                   jax.ShapeDtypeStruct((H, S, hd), F32), jax.ShapeDtypeStruct((1, 3 * D), F32)],
        compiler_params=_params("arbitrary"),
    )(*ys, wb, wbT, P, P, P, b_gate, dmerged)


def _swiglu_fwd(gu, *, name):
    S, F2 = gu.shape
    F = F2 // 2
    ts, tf = _tile(S, 512, 8), _tile(F, 1536, LANES)
    nf = F // tf

    def body(g_ref, u_ref, o_ref):
        gt = g_ref[...]
        o_ref[...] = (gt * _sigmoid(gt) * u_ref[...]).astype(BF16)

    return pl.pallas_call(
        body, name=name, grid=(S // ts, nf),
        in_specs=[pl.BlockSpec((ts, tf), lambda i, j: (i, j)), pl.BlockSpec((ts, tf), lambda i, j: (i, j + nf))],
        out_specs=pl.BlockSpec((ts, tf), lambda i, j: (i, j)),
        out_shape=jax.ShapeDtypeStruct((S, F), BF16),
        compiler_params=_params("parallel", "parallel"),
    )(gu, gu)


def _swiglu_bwd(gu, dact, *, name):
    S, F2 = gu.shape
    F = F2 // 2
    ts = _tile(S, 256, 8)

    def body(gu_ref, d_ref, o_ref):
        gt, up, da = gu_ref[:, 0:F], gu_ref[:, F:F2], d_ref[...]
        sg = _sigmoid(gt)
        o_ref[:, 0:F] = (da * up * sg * (1.0 + gt * (1.0 - sg))).astype(BF16)
        o_ref[:, F:F2] = (da * gt * sg).astype(BF16)

    return pl.pallas_call(
        body, name=name, grid=(S // ts,),
        in_specs=[pl.BlockSpec((ts, F2), lambda i: (i, 0)), pl.BlockSpec((ts, F), lambda i: (i, 0))],
        out_specs=pl.BlockSpec((ts, F2), lambda i: (i, 0)),
        out_shape=jax.ShapeDtypeStruct((S, F2), BF16),
        compiler_params=_params("parallel"),
    )(gu, dact)


def _loss_grad(y, target, *, name):
    S, D = y.shape
    ts = _tile(S, 512, 8)

    def body(y_ref, t_ref, dy_ref, l_ref, dyb_ref):
        i = pl.program_id(0)
        err = y_ref[...] - t_ref[...]
        dy_ref[...] = err * (1.0 / D)
        dyb_ref[...] = (err * (1.0 / D)).astype(BF16)
        part = jnp.broadcast_to(jnp.sum(jnp.sum(err * err, axis=1, keepdims=True), axis=0, keepdims=True) * (0.5 / D),
                                (1, LANES))

        @pl.when(i == 0)
        def _():
            l_ref[...] = part

        @pl.when(i > 0)
        def _():
            l_ref[...] += part

    row = pl.BlockSpec((ts, D), lambda i: (i, 0))
    return pl.pallas_call(
        body, name=name, grid=(S // ts,), in_specs=[row, row],
        out_specs=[row, pl.BlockSpec((1, LANES), lambda i: (0, 0)), row],
        out_shape=[jax.ShapeDtypeStruct((S, D), F32), jax.ShapeDtypeStruct((1, LANES), F32),
                   jax.ShapeDtypeStruct((S, D), BF16)],
        compiler_params=_params("arbitrary"),
    )(y, target)


def _adamw(w, m, v, gs, *, name):
    R = w.shape[0]
    ns = gs.shape[0]
    tr = _tile(R, 2048, 16)
    c1 = 1.0 / (1.0 - ADAM_B1 ** ADAM_STEP)
    c2 = 1.0 / (1.0 - ADAM_B2 ** ADAM_STEP)

    def body(w_ref, m_ref, v_ref, gs_ref, g_ref, d_ref, nm_ref, nv_ref):
        g = gs_ref[0].astype(F32)
        for s in range(1, ns):
            g = g + gs_ref[s].astype(F32)
        nm = ADAM_B1 * m_ref[...] + (1.0 - ADAM_B1) * g
        nv = ADAM_B2 * v_ref[...] + (1.0 - ADAM_B2) * (g * g)
        g_ref[...] = g
        nm_ref[...] = nm
        nv_ref[...] = nv
        d_ref[...] = -ADAM_LR * ((nm * c1) / (jnp.sqrt(nv * c2) + ADAM_EPS) + ADAM_WD * w_ref[...])

    row = pl.BlockSpec((tr, LANES), lambda i: (i, 0))
    shp = jax.ShapeDtypeStruct((R, LANES), F32)
    return pl.pallas_call(
        body, name=name, grid=(R // tr,),
        in_specs=[row, row, row, pl.BlockSpec((ns, tr, LANES), lambda i: (0, i, 0))],
        out_specs=[row, row, row, row], out_shape=[shp, shp, shp, shp],
        compiler_params=_params("parallel"),
    )(w, m, v, gs)


def _slot_sum(gs, *, name):
    ns, R, _ = gs.shape
    tr = _tile(R, 2048, 16)

    def body(gs_ref, o_ref):
        g = gs_ref[0]
        for s in range(1, ns):
            g = g + gs_ref[s]
        o_ref[...] = g

    return pl.pallas_call(
        body, name=name, grid=(R // tr,),
        in_specs=[pl.BlockSpec((ns, tr, LANES), lambda i: (0, i, 0))],
        out_specs=pl.BlockSpec((tr, LANES), lambda i: (i, 0)),
        out_shape=jax.ShapeDtypeStruct((R, LANES), F32),
        compiler_params=_params("parallel"),
    )(gs)


def _comm_scratch():
    return [pltpu.SemaphoreType.DMA((7,)), pltpu.SemaphoreType.DMA((7,)), pltpu.SemaphoreType.DMA]


def _gather_schedule(x_ref, out_ref, send_sems, recv_sems, local_sem):
    x, y, c = lax.axis_index("x"), lax.axis_index("y"), lax.axis_index("c")
    me, sibling = (x, y, c), (x, y, 1 - c)
    chips = [(1 - x, y), (x, 1 - y), (1 - x, 1 - y)]

    def slot(px, py, pc):
        return out_ref.at[4 * px + 2 * py + pc]

    def copy(k, block, to, src=None):
        return pltpu.make_async_remote_copy(
            src_ref=slot(*block) if src is None else src, dst_ref=slot(*block),
            send_sem=send_sems.at[k], recv_sem=recv_sems.at[k], device_id=to, device_id_type=MESH)

    mine = pltpu.make_async_copy(x_ref, slot(*me), local_sem)
    first = [copy(0, me, sibling, src=x_ref)]
    first += [copy(1 + j, me, (*chip, c), src=x_ref) for j, chip in enumerate(chips)]
    passed = [copy(4 + j, (*chip, c), sibling) for j, chip in enumerate(chips)]

    def begin():
        mine.start()
        for cp in first:
            cp.start()

    def end():
        for j, chip in enumerate(chips):
            copy(1 + j, (*chip, c), me).wait_recv()
            passed[j].start()
        copy(0, sibling, me).wait_recv()
        for j, chip in enumerate(chips):
            copy(4 + j, (*chip, 1 - c), me).wait_recv()
        for cp in first + passed:
            cp.wait_send()
        mine.wait()

    return begin, end


def _comm_call(schedule, xs, out_shape, *, name):
    def body(x_ref, out_ref, send_sems, recv_sems, local_sem):
        begin, end = schedule(x_ref, out_ref, send_sems, recv_sems, local_sem)
        begin()
        end()

    return pl.pallas_call(
        body, name=name,
        in_specs=[pl.BlockSpec(memory_space=pl.ANY)], out_specs=pl.BlockSpec(memory_space=pl.ANY),
        out_shape=jax.ShapeDtypeStruct(out_shape, xs.dtype), scratch_shapes=_comm_scratch(),
        compiler_params=pltpu.CompilerParams(has_side_effects=True),
    )(xs)


def _all_gather(xs, *, name):
    return _comm_call(_gather_schedule, xs, (N_DEV,) + xs.shape, name=name)


def _exchange_schedule(x_ref, out_ref, send_sems, recv_sems, local_sem):
    x, y, c = lax.axis_index("x"), lax.axis_index("y"), lax.axis_index("c")
    me = 4 * x + 2 * y + c
    mine = pltpu.make_async_copy(x_ref.at[me], out_ref.at[me], local_sem)
    sends, recvs = [], []
    for k in range(1, N_DEV):
        px = 1 - x if k & 4 else x
        py = 1 - y if k & 2 else y
        pc = 1 - c if k & 1 else c
        peer = 4 * px + 2 * py + pc
        sends.append(pltpu.make_async_remote_copy(
            src_ref=x_ref.at[peer], dst_ref=out_ref.at[me], send_sem=send_sems.at[k - 1],
            recv_sem=recv_sems.at[k - 1], device_id=(px, py, pc), device_id_type=MESH))
        recvs.append(pltpu.make_async_remote_copy(
            src_ref=x_ref.at[me], dst_ref=out_ref.at[peer], send_sem=send_sems.at[k - 1],
            recv_sem=recv_sems.at[k - 1], device_id=(px, py, pc), device_id_type=MESH))

    def begin():
        mine.start()
        for cp in sends:
            cp.start()

    def end():
        for cp in recvs:
            cp.wait_recv()
        for cp in sends:
            cp.wait_send()
        mine.wait()

    return begin, end


PACK_ROWS = 16


def _pack_rows(parts, dtype, lead=()):
    rows = []
    for p in parts:
        r = p.reshape(lead + (-1, LANES)).astype(dtype)
        pad = (-r.shape[-2]) % PACK_ROWS
        rows.append(jnp.pad(r, [(0, 0)] * len(lead) + [(0, pad), (0, 0)]) if pad else r)
    return jnp.concatenate(rows, axis=len(lead))


def _unpack_rows(packed, shapes, lead=()):
    out, off = [], 0
    for shp in shapes:
        n = math.prod(shp) // LANES
        out.append(lax.slice_in_dim(packed, off, off + n, axis=len(lead)).reshape(lead + tuple(shp)))
        off += n + (-n) % PACK_ROWS
    return out


def _unshard(gathered, axis):
    g = jnp.moveaxis(gathered, 0, axis)
    shp = list(g.shape)
    shp[axis:axis + 2] = [shp[axis] * shp[axis + 1]]
    return g.reshape(shp)


def _reshard(full, axis):
    shp = list(full.shape)
    shp[axis:axis + 1] = [N_DEV, shp[axis] // N_DEV]
    return jnp.moveaxis(full.reshape(shp), axis, 0)


def _late_parts(L):
    return [(k, l) for k in BIG for l in range(L) if (k, l) != ("w_in", 0)]


def _local_step(x, target, w_in0, late_pack, late_shapes, sm):
    S, D = x.shape
    L = sm["mix_norm_g"].shape[0]
    late = _late_parts(L)
    wf = {k: [None] * L for k in BIG}
    wf["w_in"][0] = w_in0
    C = sm["conv_b"].shape[1]
    W = sm["sgu_ln_g"].shape[1]
    hd = sm["q_norm_g"].shape[1]
    G, CH = sm["sgu_w"].shape[1], sm["sgu_w"].shape[2]
    A = w_in0.shape[1] - (3 * C + 2 * W + 3 * D)
    A = A // 3
    H = A // hd
    col_q = 3 * C + 2 * W
    qscale = 1.0 / math.sqrt(hd)
    tril = jnp.tril(jnp.ones((CH, CH), F32))
    (ftq, ftk), (btq, btk) = [(_tile(S, a, LANES), _tile(S, b, LANES)) for a, b in (ATTN_FWD_TILES, ATTN_BWD_TILES)]
    umat = lambda t: (lax.broadcasted_iota(jnp.int32, (t, t), 0) > lax.broadcasted_iota(jnp.int32, (t, t), 1)).astype(BF16)

    saved = []
    for l in range(L):
        n = f"l{l}_"
        g1 = sm["mix_norm_g"][l][None]
        h = _rmsnorm_fwd(x, g1, scale=1.0, out_dtype=BF16, name=n + "mixnorm")
        P = _matmul(h, wf["w_in"][l], name=n + "w_in")
        cw = jnp.pad(sm["conv_w"][l], ((0, 5), (0, 0)))
        cb = sm["conv_b"][l][None]
        ya = _conv_fwd(P, cw, cb, C=C, name=n + "conv")
        wm = (sm["sgu_w"][l] * tril).astype(BF16)
        bT = sm["sgu_b"][l].T
        lng, lnb = sm["sgu_ln_g"][l][None], sm["sgu_ln_b"][l][None]
        yb = _sgu_fwd(P, lng, lnb, wm, bT, W=W, cu=(3 * C) // W, cv=(3 * C) // W + 1, name=n + "sgu")
        gq, gk = sm["q_norm_g"][l][None], sm["k_norm_g"][l][None]
        qn, kn, vh = _qkv_heads_fwd(P, gq, gk, A=A, col0=col_q // A, hd=hd, qscale=qscale, name=n + "qkv")
        if l == 0:
            o, gathered = _attn_fwd(qn, kn, vh, umat(ftk), tq=ftq, tk=ftk, name=n + "attn",
                                    comm=(_gather_schedule, late_pack, (N_DEV,) + late_pack.shape))
            for (k, ll), part in zip(late, _unpack_rows(gathered, late_shapes, lead=(N_DEV,))):
                wf[k][ll] = _unshard(part, SHARD_AXIS[k] - 1)
        else:
            o = _attn_fwd(qn, kn, vh, umat(ftk), tq=ftq, tk=ftk, name=n + "attn")
        bg = sm["b_gate"][l][None]
        gate_col0 = (col_q + 3 * A) // D
        merged, yc = _merge_fwd(ya, yb, o, wf["w_branch_out"][l], P, bg, gate_col0=gate_col0, name=n + "merge")
        x1 = _matmul(merged, wf["w_o"][l], res=x, name=n + "w_o")
        g2 = sm["ffn_norm_g"][l][None]
        h2 = _rmsnorm_fwd(x1, g2, scale=1.0, out_dtype=BF16, name=n + "ffnnorm")
        gu = _matmul(h2, wf["w_gate_up"][l], name=n + "w_gate_up")
        act = _swiglu_fwd(gu, name=n + "swiglu")
        x2 = _matmul(act, wf["w_down"][l], res=x1, name=n + "w_down")
        saved.append(dict(x=x, h=h, P=P, cw=cw, cb=cb, ya=ya, wm=wm, bT=bT, lng=lng, lnb=lnb, yb=yb,
                          gq=gq, gk=gk, qn=qn, kn=kn, vh=vh, o=o, yc=yc, bg=bg,
                          gate_col0=gate_col0, merged=merged, x1=x1, g1=g1, g2=g2, h2=h2, gu=gu, act=act))
        x = x2

    dx, lpart, dxb = _loss_grad(x, target, name="loss")
    grads = {k: [None] * L for k in WEIGHTS}
    chunked = lambda k, g: g if k == "w_in" else _reshard(g, SHARD_AXIS[k] - 1)
    for l in reversed(range(L)):
        n = f"l{l}_b_"
        sv = saved[l]
        grads["w_down"][l] = _matmul_tn(sv["act"], dxb, name=n + "g_w_down", out_dtype=BF16)
        dact = _matmul(dxb, wf["w_down"][l].T, name=n + "d_act")
        dgu = _swiglu_bwd(sv["gu"], dact, name=n + "swiglu")
        grads["w_gate_up"][l] = _matmul_tn(sv["h2"], dgu, name=n + "g_w_gate_up", out_dtype=BF16)
        dh2 = _matmul(dgu, wf["w_gate_up"][l].T, name=n + "d_h2")
        dx1, dg2, dx1b = _rmsnorm_bwd(sv["x1"], sv["g2"], dh2, scale=1.0, dres=dx, name=n + "ffnnorm",
                                      bf16_copy=True)
        grads["ffn_norm_g"][l] = dg2[0]
        grads["w_o"][l] = _matmul_tn(sv["merged"], dx1b, name=n + "g_w_o", out_dtype=BF16)
        dmerged = _matmul(dx1b, wf["w_o"][l].T, name=n + "d_merged")
        ys = (sv["ya"], sv["yb"], sv["yc"])
        wb = wf["w_branch_out"][l]
        dgates, *dyd, dya, dyb, do, dbg = _merge_bwd(ys, wb, wb.transpose(0, 2, 1), sv["P"], sv["bg"], dmerged,
                                                     gate_col0=sv["gate_col0"], hd=hd, name=n + "merge")
        grads["b_gate"][l] = dbg[0]
        grads["w_branch_out"][l] = jnp.stack(
            [_matmul_tn(ys[i], dyd[i], name=n + f"g_w_branch{i}", out_dtype=BF16) for i in range(3)])
        dconv, dcw = _conv_bwd(sv["P"], dya, sv["cw"], sv["cb"], C=C, name=n + "conv")
        grads["conv_w"][l], grads["conv_b"][l] = dcw[0:3], dcw[3]
        wmT = sv["wm"].transpose(0, 2, 1)
        dsgu, dsw, dsb, dln = _sgu_bwd(sv["P"], dyb, sv["lng"], sv["lnb"], sv["wm"], wmT, sv["bT"], W=W,
                                       cu=(3 * C) // W, cv=(3 * C) // W + 1, name=n + "sgu")
        grads["sgu_w"][l], grads["sgu_b"][l] = dsw * tril, dsb[:, :, 0]
        grads["sgu_ln_g"][l], grads["sgu_ln_b"][l] = dln[0], dln[1]
        if l == 0:
            chunks = _pack_rows([chunked(k, grads[k][ll]) for k, ll in late], BF16, lead=(N_DEV,))
            dqn, dkn, dvh, late_recv = _attn_bwd(sv["qn"], sv["kn"], sv["vh"], sv["o"], do, umat(btk), tq=btq, tk=btk,
                                                 name=n + "attn", comm=(_exchange_schedule, chunks, chunks.shape))
        else:
            dqn, dkn, dvh = _attn_bwd(sv["qn"], sv["kn"], sv["vh"], sv["o"], do, umat(btk), tq=btq, tk=btk,
                                      name=n + "attn")
        dqkv, dgq, dgk = _qkv_heads_bwd(sv["P"], sv["gq"], sv["gk"], dqn, dkn, dvh, A=A, col0=col_q // A, hd=hd,
                                        qscale=qscale, name=n + "qkv")
        grads["q_norm_g"][l], grads["k_norm_g"][l] = dgq[0], dgk[0]
        dP = jnp.concatenate([dconv, dsgu, dqkv, dgates], axis=1)
        grads["w_in"][l] = _matmul_tn(sv["h"], dP, name=n + "g_w_in", out_dtype=BF16, shard=dP.shape[1] // N_DEV)
        if l == 0:
            chunks = _pack_rows([grads["w_in"][0]], BF16, lead=(N_DEV,))
            dh, early_recv = _matmul(dP, wf["w_in"][l].T, name=n + "d_h",
                                     comm=(_exchange_schedule, chunks, chunks.shape))
        else:
            dh = _matmul(dP, wf["w_in"][l].T, name=n + "d_h")
        dx, dg1, dxb = _rmsnorm_bwd(sv["x"], sv["g1"], dh, scale=1.0, dres=dx1, name=n + "mixnorm", bf16_copy=True)
        grads["mix_norm_g"][l] = dg1[0]
    return lpart[0, 0], dx, grads, early_recv, late_recv


def kernel(x, mix_norm_g, w_in, b_gate, conv_w, conv_b, sgu_ln_g, sgu_ln_b, sgu_w, sgu_b, q_norm_g, k_norm_g, w_branch_out, w_o, ffn_norm_g, w_gate_up, w_down, loss_target, m_mix_norm_g, m_w_in, m_b_gate, m_conv_w, m_conv_b, m_sgu_ln_g, m_sgu_ln_b, m_sgu_w, m_sgu_b, m_q_norm_g, m_k_norm_g, m_w_branch_out, m_w_o, m_ffn_norm_g, m_w_gate_up, m_w_down, v_mix_norm_g, v_w_in, v_b_gate, v_conv_w, v_conv_b, v_sgu_ln_g, v_sgu_ln_b, v_sgu_w, v_sgu_b, v_q_norm_g, v_k_norm_g, v_w_branch_out, v_w_o, v_ffn_norm_g, v_w_gate_up, v_w_down):
    w = dict(mix_norm_g=mix_norm_g, w_in=w_in, b_gate=b_gate, conv_w=conv_w, conv_b=conv_b, sgu_ln_g=sgu_ln_g,
             sgu_ln_b=sgu_ln_b, sgu_w=sgu_w, sgu_b=sgu_b, q_norm_g=q_norm_g, k_norm_g=k_norm_g,
             w_branch_out=w_branch_out, w_o=w_o, ffn_norm_g=ffn_norm_g, w_gate_up=w_gate_up, w_down=w_down)
    m = dict(mix_norm_g=m_mix_norm_g, w_in=m_w_in, b_gate=m_b_gate, conv_w=m_conv_w, conv_b=m_conv_b,
             sgu_ln_g=m_sgu_ln_g, sgu_ln_b=m_sgu_ln_b, sgu_w=m_sgu_w, sgu_b=m_sgu_b, q_norm_g=m_q_norm_g,
             k_norm_g=m_k_norm_g, w_branch_out=m_w_branch_out, w_o=m_w_o, ffn_norm_g=m_ffn_norm_g,
             w_gate_up=m_w_gate_up, w_down=m_w_down)
    v = dict(mix_norm_g=v_mix_norm_g, w_in=v_w_in, b_gate=v_b_gate, conv_w=v_conv_w, conv_b=v_conv_b,
             sgu_ln_g=v_sgu_ln_g, sgu_ln_b=v_sgu_ln_b, sgu_w=v_sgu_w, sgu_b=v_sgu_b, q_norm_g=v_q_norm_g,
             k_norm_g=v_k_norm_g, w_branch_out=v_w_branch_out, w_o=v_w_o, ffn_norm_g=v_ffn_norm_g,
             w_gate_up=v_w_gate_up, w_down=v_w_down)
    me = 4 * lax.axis_index("x") + 2 * lax.axis_index("y") + lax.axis_index("c")
    S = x.shape[1]

    L = w_in.shape[0]
    late = _late_parts(L)
    late_shapes = [w[k].shape[1:] for k, _ in late]
    w_in0 = _unshard(_all_gather(_pack_rows([w_in[0]], BF16), name="gather_w_in0").reshape((N_DEV,) + w_in.shape[1:]),
                     SHARD_AXIS["w_in"] - 1)
    late_pack = _pack_rows([w[k][l] for k, l in late], BF16)
    conv_g = _all_gather(_pack_rows([conv_w], F32), name="gather_conv_w")
    sm = {k: w[k] for k in SMALL}
    sm["conv_w"] = _unshard(_unpack_rows(conv_g, [conv_w.shape], lead=(N_DEV,))[0], 2)

    lpart, dx, grads, early_recv, late_recv = _local_step(x[0], loss_target[0], w_in0, late_pack, late_shapes, sm)
    loss = lax.psum(lpart, ("x", "y", "c"))

    pk = lambda t, parts: _pack_rows([t[k][l] for k, l in parts], F32)
    early = [("w_in", 0)]
    res_e = _adamw(pk(w, early), pk(m, early), pk(v, early), early_recv, name="adamw_w_in0")
    res_l = _adamw(pk(w, late), pk(m, late), pk(v, late), late_recv, name="adamw_late")
    out = {}
    whole = [k for k in BIG if k != "w_in"]
    for i, nm in enumerate(("grad", "delta", "new_m", "new_v")):
        w_in_rest, *rest = _unpack_rows(res_l[i], [(L - 1,) + w_in.shape[1:]] + [w[k].shape for k in whole])
        out[nm, "w_in"] = jnp.concatenate([res_e[i].reshape((1,) + w_in.shape[1:]), w_in_rest], axis=0)
        out.update({(nm, k): a for k, a in zip(whole, rest)})

    small_grads = [jnp.stack(grads[k]) for k in SMALL]
    small_shapes = [g.shape for g in small_grads]
    sg = _all_gather(_pack_rows(small_grads, F32), name="gather_small_grads")
    gsum = _slot_sum(sg, name="sum_small_grads")
    gsmall = dict(zip(SMALL, _unpack_rows(gsum, small_shapes)))
    cshard = conv_w.shape[2]
    gsmall["conv_w"] = lax.dynamic_slice_in_dim(gsmall["conv_w"], me * cshard, cshard, axis=2)
    own_shapes = [w[k].shape for k in SMALL]
    gs_, ds_, ms_, vs_ = _adamw(_pack_rows([w[k] for k in SMALL], F32), _pack_rows([m[k] for k in SMALL], F32),
                                _pack_rows([v[k] for k in SMALL], F32),
                                _pack_rows([gsmall[k] for k in SMALL], F32)[None], name="adamw_small")
    for nm, packed in (("grad", gs_), ("delta", ds_), ("new_m", ms_), ("new_v", vs_)):
        for k, a in zip(SMALL, _unpack_rows(packed, own_shapes)):
            out[nm, k] = a

    res = [loss, dx[None]]
    for nm in ("grad", "delta", "new_m", "new_v"):
        res += [out[nm, k] for k in WEIGHTS]
    return tuple(res)
```

```python
import functools
import math

import jax
import jax.numpy as jnp
from jax import lax
from jax.experimental import pallas as pl
from jax.experimental.pallas import tpu as pltpu

F32 = jnp.float32
BF16 = jnp.bfloat16
MESH = pl.DeviceIdType.MESH

N_DEV = 8
LANES = 128
VMEM_LIMIT_BYTES = 56 * 1024 * 1024
EPS = 1e-6
ADAM_LR, ADAM_B1, ADAM_B2, ADAM_EPS, ADAM_WD, ADAM_STEP = 0.001, 0.9, 0.999, 1e-08, 0.01, 10
ATTN_FWD_TILES = (512, 512)
ATTN_BWD_TILES = (1024, 256)
ATTN_FWD_UNROLL = 4
ATTN_BWD_UNROLL = 4
BIG = ("w_in", "w_branch_out", "w_o", "w_gate_up", "w_down")
SMALL = ("mix_norm_g", "b_gate", "conv_w", "conv_b", "sgu_ln_g", "sgu_ln_b", "sgu_w", "sgu_b",
         "q_norm_g", "k_norm_g", "ffn_norm_g")
WEIGHTS = ("mix_norm_g", "w_in", "b_gate", "conv_w", "conv_b", "sgu_ln_g", "sgu_ln_b", "sgu_w", "sgu_b",
           "q_norm_g", "k_norm_g", "w_branch_out", "w_o", "ffn_norm_g", "w_gate_up", "w_down")
SHARD_AXIS = {"w_in": 2, "w_branch_out": 3, "w_o": 1, "w_gate_up": 2, "w_down": 1}


def _tile(n, cap, mult):
    best = None
    for t in range(mult, min(n, cap) + 1, mult):
        if n % t == 0:
            best = t
    return best if best is not None else n


def _params(*sem):
    return pltpu.CompilerParams(dimension_semantics=sem if sem else None, vmem_limit_bytes=VMEM_LIMIT_BYTES)


def _erf(x):
    return lax.erf(x)


def _gelu(x):
    return 0.5 * x * (1.0 + _erf(x * (1.0 / math.sqrt(2.0))))


def _gelu_grad(x):
    return 0.5 * (1.0 + _erf(x * (1.0 / math.sqrt(2.0)))) + x * jnp.exp(-0.5 * x * x) * (1.0 / math.sqrt(2.0 * math.pi))


def _sigmoid(x):
    return 1.0 / (1.0 + jnp.exp(-x))


def _matmul(a, b, *, name, res=None, out_dtype=F32, comm=None):
    M, K = a.shape
    _, N = b.shape
    tm, tn, tk = _tile(M, 1024, 8), _tile(N, 1536, LANES), _tile(K, 1536, LANES)
    nk = K // tk
    grid = (M // tm, N // tn, nk)
    has_res = res is not None

    def body(*refs):
        refs = list(refs)
        a_ref, b_ref = refs[:2]
        r_ref = refs[2] if has_res else None
        pos = 2 + has_res
        if comm is not None:
            cin_ref, o_ref, cout_ref = refs[pos:pos + 3]
            pos += 3
        else:
            o_ref = refs[pos]
            pos += 1
        acc = refs[pos] if nk > 1 else None
        if comm is not None:
            comm_end = _comm_begin(comm[0], cin_ref, cout_ref, refs[pos + (nk > 1):], grid)
        k = pl.program_id(2)
        part = jnp.dot(a_ref[...], b_ref[...], preferred_element_type=F32)

        def finish(v):
            if has_res:
                v = v + r_ref[...]
            o_ref[...] = v.astype(out_dtype)

        if nk == 1:
            finish(part)
        else:
            @pl.when(k == 0)
            def _():
                acc[...] = part

            @pl.when(jnp.logical_and(k > 0, k < nk - 1))
            def _():
                acc[...] += part

            @pl.when(k == nk - 1)
            def _():
                finish(acc[...] + part)

        if comm is not None:
            comm_end()

    in_specs = [pl.BlockSpec((tm, tk), lambda i, j, k: (i, k)), pl.BlockSpec((tk, tn), lambda i, j, k: (k, j))]
    args = [a, b]
    if has_res:
        in_specs.append(pl.BlockSpec((tm, tn), lambda i, j, k: (i, j)))
        args.append(res)
    out_specs = [pl.BlockSpec((tm, tn), lambda i, j, k: (i, j))]
    out_shape = [jax.ShapeDtypeStruct((M, N), out_dtype)]
    scratch = [pltpu.VMEM((tm, tn), F32)] if nk > 1 else []
    sem = ("parallel", "parallel", "arbitrary")
    if comm is not None:
        anyspec = pl.BlockSpec(memory_space=pl.ANY)
        in_specs.append(anyspec)
        args.append(comm[1])
        out_specs.append(anyspec)
        out_shape.append(jax.ShapeDtypeStruct(comm[2], comm[1].dtype))
        scratch += _comm_scratch()
        sem = ("arbitrary",) * 3
    out = pl.pallas_call(body, name=name, grid=grid, in_specs=in_specs, out_specs=out_specs, out_shape=out_shape,
                         scratch_shapes=scratch, compiler_params=_params(*sem))(*args)
    return out[0] if comm is None else out


def _matmul_tn(x, y, *, name, out_dtype=F32, shard=None):
    S, A = x.shape
    _, B = y.shape
    ta, ts = _tile(A, 1536, LANES), _tile(S, 1024, 8)
    tb = shard if shard else _tile(B, 1536, LANES)
    ns = S // ts
    direct = out_dtype == F32
    view = (lambda r: r.at[0]) if shard else (lambda r: r)

    def body(x_ref, y_ref, o_ref, *scratch):
        s = pl.program_id(2)
        out = view(o_ref)
        acc = out if direct else scratch[0]
        part = lax.dot_general(x_ref[...], y_ref[...], (((0,), (0,)), ((), ())), preferred_element_type=F32)

        @pl.when(s == 0)
        def _():
            acc[...] = part

        @pl.when(s > 0)
        def _():
            acc[...] += part

        if not direct:
            @pl.when(s == ns - 1)
            def _():
                out[...] = acc[...].astype(out_dtype)

    if shard:
        out_spec, shape = pl.BlockSpec((1, ta, tb), lambda i, j, s: (j, i, 0)), (B // tb, A, tb)
    else:
        out_spec, shape = pl.BlockSpec((ta, tb), lambda i, j, s: (i, j)), (A, B)
    return pl.pallas_call(
        body, name=name, grid=(A // ta, B // tb, ns),
        in_specs=[pl.BlockSpec((ts, ta), lambda i, j, s: (s, i)), pl.BlockSpec((ts, tb), lambda i, j, s: (s, j))],
        out_specs=out_spec, out_shape=jax.ShapeDtypeStruct(shape, out_dtype),
        scratch_shapes=[] if direct else [pltpu.VMEM((ta, tb), F32)],
        compiler_params=_params("parallel", "parallel", "arbitrary"),
    )(x, y)


def _rmsnorm_fwd(x, g, *, scale, out_dtype, name):
    R, W = x.shape
    tr = _tile(R, 512 if W >= 512 else 4096, 8)

    def body(x_ref, g_ref, o_ref):
        xv = x_ref[...]
        r = lax.rsqrt(jnp.mean(xv * xv, axis=1, keepdims=True) + EPS)
        o_ref[...] = (xv * r * (g_ref[...] * scale)).astype(out_dtype)

    return pl.pallas_call(
        body, name=name, grid=(R // tr,),
        in_specs=[pl.BlockSpec((tr, W), lambda i: (i, 0)), pl.BlockSpec((1, W), lambda i: (0, 0))],
        out_specs=pl.BlockSpec((tr, W), lambda i: (i, 0)),
        out_shape=jax.ShapeDtypeStruct((R, W), out_dtype),
        compiler_params=_params("parallel"),
    )(x, g)


def _rmsnorm_bwd(x, g, dy, *, scale, name, dres=None, out_dtype=F32, bf16_copy=False):
    R, W = x.shape
    tr = _tile(R, 512 if W >= 512 else 4096, 8)
    has_res = dres is not None

    def body(*refs):
        refs = list(refs)
        dxb_ref = refs.pop() if bf16_copy else None
        if has_res:
            x_ref, g_ref, dy_ref, dres_ref, dx_ref, dg_ref = refs
        else:
            x_ref, g_ref, dy_ref, dx_ref, dg_ref = refs
        i = pl.program_id(0)
        xv = x_ref[...]
        dyv = dy_ref[...].astype(F32) * scale
        r = lax.rsqrt(jnp.mean(xv * xv, axis=1, keepdims=True) + EPS)
        u = dyv * g_ref[...]
        dx = r * u - xv * (r * r * r * jnp.mean(u * xv, axis=1, keepdims=True))
        if has_res:
            dx = dx + dres_ref[...]
        dx_ref[...] = dx.astype(out_dtype)
        if bf16_copy:
            dxb_ref[...] = dx.astype(BF16)
        part = jnp.sum(dyv * xv * r, axis=0, keepdims=True)

        @pl.when(i == 0)
        def _():
            dg_ref[...] = part

        @pl.when(i > 0)
        def _():
            dg_ref[...] += part

    row = pl.BlockSpec((tr, W), lambda i: (i, 0))
    one = pl.BlockSpec((1, W), lambda i: (0, 0))
    in_specs = [row, one, row] + ([row] if has_res else [])
    args = [x, g, dy] + ([dres] if has_res else [])
    extra = bool(bf16_copy)
    return pl.pallas_call(
        body, name=name, grid=(R // tr,), in_specs=in_specs, out_specs=[row, one] + [row] * extra,
        out_shape=[jax.ShapeDtypeStruct((R, W), out_dtype), jax.ShapeDtypeStruct((1, W), F32)]
        + [jax.ShapeDtypeStruct((R, W), BF16)] * extra,
        compiler_params=_params("arbitrary"),
    )(*args)


def _shift_down(u, prev, n):
    ts = u.shape[0]
    out = pltpu.roll(u, n, 0)
    row = lax.broadcasted_iota(jnp.int32, u.shape, 0)
    for r in range(n):
        out = jnp.where(row == r, prev[8 - n + r:8 - n + r + 1, :], out)
    return out


def _shift_up(u, nxt, n):
    ts = u.shape[0]
    out = pltpu.roll(u, ts - n, 0)
    row = lax.broadcasted_iota(jnp.int32, u.shape, 0)
    for r in range(n):
        out = jnp.where(row == ts - n + r, nxt[r:r + 1, :], out)
    return out


def _conv_fwd(P, conv_w, conv_b, *, C, name):
    S = P.shape[0]
    ts = _tile(S, 512, 8)
    hb = ts // 8

    def body(ab_ref, ac_ref, ax_ref, pc_ref, px_ref, w_ref, b_ref, o_ref):
        i = pl.program_id(0)
        u = ac_ref[...] * ax_ref[...]
        prev = pc_ref[...] * px_ref[...] * (i > 0).astype(F32)
        w = w_ref[...]
        y = b_ref[...] + w[0:1, :] * _shift_down(u, prev, 2) + w[1:2, :] * _shift_down(u, prev, 1) + w[2:3, :] * u
        o_ref[...] = (ab_ref[...] * y).astype(BF16)

    cur = lambda c: pl.BlockSpec((ts, C), lambda i: (i, c))
    prv = lambda c: pl.BlockSpec((8, C), lambda i: (jnp.maximum(i * hb - 1, 0), c))
    return pl.pallas_call(
        body, name=name, grid=(S // ts,),
        in_specs=[cur(0), cur(1), cur(2), prv(1), prv(2),
                  pl.BlockSpec((8, C), lambda i: (0, 0)), pl.BlockSpec((1, C), lambda i: (0, 0))],
        out_specs=pl.BlockSpec((ts, C), lambda i: (i, 0)),
        out_shape=jax.ShapeDtypeStruct((S, C), BF16),
        compiler_params=_params("parallel"),
    )(P, P, P, P, P, conv_w, conv_b)


def _conv_bwd(P, dya, conv_w, conv_b, *, C, name):
    S = P.shape[0]
    ts = _tile(S, 512, 8)
    hb = ts // 8
    last = S // 8 - 1
    n = S // ts

    def body(ab_ref, ac_ref, ax_ref, pc_ref, px_ref, dy_ref, nab_ref, ndy_ref, w_ref, b_ref, o_ref, dw_ref):
        i = pl.program_id(0)
        ab, ac, ax = ab_ref[...], ac_ref[...], ax_ref[...]
        u = ac * ax
        prev = pc_ref[...] * px_ref[...] * (i > 0).astype(F32)
        w = w_ref[...]
        u1, u2 = _shift_down(u, prev, 1), _shift_down(u, prev, 2)
        y = b_ref[...] + w[0:1, :] * u2 + w[1:2, :] * u1 + w[2:3, :] * u
        dya_v = dy_ref[...]
        dyp = dya_v * ab
        nxt = ndy_ref[...] * nab_ref[...] * (i < n - 1).astype(F32)
        du = w[2:3, :] * dyp + w[1:2, :] * _shift_up(dyp, nxt, 1) + w[0:1, :] * _shift_up(dyp, nxt, 2)
        o_ref[:, 0:C] = (dya_v * y).astype(BF16)
        o_ref[:, C:2 * C] = (du * ax).astype(BF16)
        o_ref[:, 2 * C:3 * C] = (du * ac).astype(BF16)
        part = jnp.concatenate([
            jnp.sum(dyp * u2, axis=0, keepdims=True), jnp.sum(dyp * u1, axis=0, keepdims=True),
            jnp.sum(dyp * u, axis=0, keepdims=True), jnp.sum(dyp, axis=0, keepdims=True),
            jnp.zeros((4, C), F32)], axis=0)

        @pl.when(i == 0)
        def _():
            dw_ref[...] = part

        @pl.when(i > 0)
        def _():
            dw_ref[...] += part

    cur = lambda c: pl.BlockSpec((ts, C), lambda i: (i, c))
    prv = lambda c: pl.BlockSpec((8, C), lambda i: (jnp.maximum(i * hb - 1, 0), c))
    nxt = lambda c: pl.BlockSpec((8, C), lambda i: (jnp.minimum((i + 1) * hb, last), c))
    return pl.pallas_call(
        body, name=name, grid=(n,),
        in_specs=[cur(0), cur(1), cur(2), prv(1), prv(2), cur(0), nxt(0), nxt(0),
                  pl.BlockSpec((8, C), lambda i: (0, 0)), pl.BlockSpec((1, C), lambda i: (0, 0))],
        out_specs=[pl.BlockSpec((ts, 3 * C), lambda i: (i, 0)), pl.BlockSpec((8, C), lambda i: (0, 0))],
        out_shape=[jax.ShapeDtypeStruct((S, 3 * C), BF16), jax.ShapeDtypeStruct((8, C), F32)],
        compiler_params=_params("arbitrary"),
    )(P, P, P, P, P, dya, P, dya, conv_w, conv_b)


def _sgu_fwd(P, ln_g, ln_b, wm, bT, *, W, cu, cv, name):
    S = P.shape[0]
    G, CH, _ = wm.shape
    gw = W // G
    ts = _tile(S, 512, CH)

    def body(u_ref, v_ref, g_ref, b_ref, wm_ref, bT_ref, o_ref):
        gv = _gelu(v_ref[...])
        mu = jnp.mean(gv, axis=1, keepdims=True)
        xc = gv - mu
        vn = (xc * lax.rsqrt(jnp.mean(xc * xc, axis=1, keepdims=True) + EPS) * g_ref[...] + b_ref[...]).astype(BF16)
        bT_v = bT_ref[...]
        for c in range(ts // CH):
            rows = slice(c * CH, (c + 1) * CH)
            for g in range(G):
                cols = slice(g * gw, (g + 1) * gw)
                mixed = jnp.dot(wm_ref[g], vn[rows, cols], preferred_element_type=F32) + bT_v[:, g:g + 1]
                o_ref[rows, cols] = (_gelu(u_ref[rows, cols]) * mixed).astype(BF16)

    full = lambda shp: pl.BlockSpec(shp, lambda i: (0,) * len(shp))
    return pl.pallas_call(
        body, name=name, grid=(S // ts,),
        in_specs=[pl.BlockSpec((ts, W), lambda i: (i, cu)), pl.BlockSpec((ts, W), lambda i: (i, cv)),
                  full((1, W)), full((1, W)), full((G, CH, CH)), full((CH, G))],
        out_specs=pl.BlockSpec((ts, W), lambda i: (i, 0)),
        out_shape=jax.ShapeDtypeStruct((S, W), BF16),
        compiler_params=_params("parallel"),
    )(P, P, ln_g, ln_b, wm, bT)


def _sgu_bwd(P, dyb, ln_g, ln_b, wm, wmT, bT, *, W, cu, cv, name):
    S = P.shape[0]
    G, CH, _ = wm.shape
    gw = W // G
    ts = _tile(S, 512, CH)

    def body(u_ref, v_ref, dy_ref, g_ref, b_ref, wm_ref, wmT_ref, bT_ref, o_ref, dw_ref, db_ref, dln_ref, dvn_ref):
        i = pl.program_id(0)

        @pl.when(i == 0)
        def _():
            dw_ref[...] = jnp.zeros_like(dw_ref)
            db_ref[...] = jnp.zeros_like(db_ref)
            dln_ref[...] = jnp.zeros_like(dln_ref)

        sv = v_ref[...]
        gv = _gelu(sv)
        mu = jnp.mean(gv, axis=1, keepdims=True)
        xc = gv - mu
        rstd = lax.rsqrt(jnp.mean(xc * xc, axis=1, keepdims=True) + EPS)
        xhat = xc * rstd
        lg = g_ref[...]
        vn = (xhat * lg + b_ref[...]).astype(BF16)
        bT_v = bT_ref[...]
        for c in range(ts // CH):
            rows = slice(c * CH, (c + 1) * CH)
            for g in range(G):
                cols = slice(g * gw, (g + 1) * gw)
                vn_cg = vn[rows, cols]
                mixed = jnp.dot(wm_ref[g], vn_cg, preferred_element_type=F32) + bT_v[:, g:g + 1]
                su = u_ref[rows, cols]
                dyv = dy_ref[rows, cols]
                dmix = dyv * _gelu(su)
                o_ref[rows, cols] = (dyv * mixed * _gelu_grad(su)).astype(BF16)
                dmix_b = dmix.astype(BF16)
                dw_ref[g] += lax.dot_general(dmix_b, vn_cg, (((1,), (1,)), ((), ())), preferred_element_type=F32)
                db_ref[g] += jnp.broadcast_to(jnp.sum(dmix, axis=1, keepdims=True), (CH, CH))
                dvn_ref[rows, cols] = jnp.dot(wmT_ref[g], dmix_b, preferred_element_type=F32)
        dvn = dvn_ref[...]
        dxh = dvn * lg
        dgv = rstd * (dxh - jnp.mean(dxh, axis=1, keepdims=True) - xhat * jnp.mean(dxh * xhat, axis=1, keepdims=True))
        o_ref[:, W:2 * W] = (dgv * _gelu_grad(sv)).astype(BF16)
        dln_ref[0:1, :] += jnp.sum(dvn * xhat, axis=0, keepdims=True)
        dln_ref[1:2, :] += jnp.sum(dvn, axis=0, keepdims=True)

    full = lambda shp: pl.BlockSpec(shp, lambda i: (0,) * len(shp))
    return pl.pallas_call(
        body, name=name, grid=(S // ts,),
        in_specs=[pl.BlockSpec((ts, W), lambda i: (i, cu)), pl.BlockSpec((ts, W), lambda i: (i, cv)),
                  pl.BlockSpec((ts, W), lambda i: (i, 0)),
                  full((1, W)), full((1, W)), full((G, CH, CH)), full((G, CH, CH)), full((CH, G))],
        out_specs=[pl.BlockSpec((ts, 2 * W), lambda i: (i, 0)), full((G, CH, CH)), full((G, CH, CH)), full((8, W))],
        out_shape=[jax.ShapeDtypeStruct((S, 2 * W), BF16), jax.ShapeDtypeStruct((G, CH, CH), F32),
                   jax.ShapeDtypeStruct((G, CH, CH), F32), jax.ShapeDtypeStruct((8, W), F32)],
        scratch_shapes=[pltpu.VMEM((ts, W), F32)],
        compiler_params=_params("arbitrary"),
    )(P, P, dyb, ln_g, ln_b, wm, wmT, bT)


def _block_sums(x, u, parts=1):
    hi = x.astype(BF16)
    out = jnp.dot(hi, u, preferred_element_type=F32)
    if parts == 2:
        lo = (x - hi.astype(F32)).astype(BF16)
        out = out + jnp.dot(lo, u, preferred_element_type=F32)
    return out


_NT = (((1,), (1,)), ((), ()))
_TN = (((0,), (0,)), ((), ()))


def _left_blocks(step, jd, carry, unroll, jd_multiple):
    rem = 0
    if jd_multiple % unroll:
        rem = jd % unroll
        carry = lax.fori_loop(0, rem, lambda t, c: step(jd - 1 - t, c, False), carry)

    def trip(t, c):
        for s in range(unroll):
            c = step(jd - rem - 1 - s - unroll * t, c, False)
        return c

    return lax.fori_loop(0, jd // unroll, trip, carry)


def _diag_step(step, j, carry, row0):
    if row0 == 0:
        return step(j, carry, True)
    tail = step(j, tuple(c[row0:] for c in carry), True, row0)
    return tuple(jnp.concatenate([c[:row0], t], axis=0) for c, t in zip(carry, tail))


def _qkv_heads_fwd(P, gq, gk, *, A, col0, hd, qscale, name):
    S = P.shape[0]
    H = A // hd
    ts = _tile(S, 512, 8)

    def body(q_ref, k_ref, v_ref, gq_ref, gk_ref, qn_ref, kn_ref, vh_ref):
        for h in range(H):
            cols = slice(h * hd, (h + 1) * hd)
            for x_ref, g_ref, sc, o_ref in ((q_ref, gq_ref, qscale, qn_ref), (k_ref, gk_ref, 1.0, kn_ref)):
                xh = x_ref[:, cols]
                r = lax.rsqrt(jnp.mean(xh * xh, axis=1, keepdims=True) + EPS)
                o_ref[h] = (xh * r * (g_ref[...] * sc)).astype(BF16)
            vh_ref[h] = v_ref[:, cols].astype(BF16)

    col = lambda n: pl.BlockSpec((ts, A), lambda i: (i, col0 + n))
    gsp = pl.BlockSpec((1, hd), lambda i: (0, 0))
    hsp = pl.BlockSpec((H, ts, hd), lambda i: (0, i, 0))
    shp = jax.ShapeDtypeStruct((H, S, hd), BF16)
    return pl.pallas_call(
        body, name=name, grid=(S // ts,), in_specs=[col(0), col(1), col(2), gsp, gsp],
        out_specs=[hsp, hsp, hsp], out_shape=[shp, shp, shp], compiler_params=_params("parallel"),
    )(P, P, P, gq, gk)


def _qkv_heads_bwd(P, gq, gk, dqn, dkn, dvh, *, A, col0, hd, qscale, name):
    S = P.shape[0]
    H = A // hd
    ts = _tile(S, 512, 8)

    def body(q_ref, k_ref, gq_ref, gk_ref, dq_ref, dk_ref, dv_ref, o_ref, dgq_ref, dgk_ref):
        i = pl.program_id(0)
        parts = [jnp.zeros((1, hd), F32), jnp.zeros((1, hd), F32)]
        for h in range(H):
            for n, (x_ref, g_ref, sc, d_ref) in enumerate(((q_ref, gq_ref, qscale, dq_ref), (k_ref, gk_ref, 1.0, dk_ref))):
                xh = x_ref[:, h * hd:(h + 1) * hd]
                dyv = d_ref[h] * sc
                r = lax.rsqrt(jnp.mean(xh * xh, axis=1, keepdims=True) + EPS)
                u = dyv * g_ref[...]
                dx = r * u - xh * (r * r * r * jnp.mean(u * xh, axis=1, keepdims=True))
                o_ref[:, n * A + h * hd:n * A + (h + 1) * hd] = dx.astype(BF16)
                parts[n] = parts[n] + jnp.sum(dyv * xh * r, axis=0, keepdims=True)
            o_ref[:, 2 * A + h * hd:2 * A + (h + 1) * hd] = dv_ref[h].astype(BF16)

        @pl.when(i == 0)
        def _():
            dgq_ref[...] = parts[0]
            dgk_ref[...] = parts[1]

        @pl.when(i > 0)
        def _():
            dgq_ref[...] += parts[0]
            dgk_ref[...] += parts[1]

    col = lambda n: pl.BlockSpec((ts, A), lambda i: (i, col0 + n))
    gsp = pl.BlockSpec((1, hd), lambda i: (0, 0))
    hsp = pl.BlockSpec((H, ts, hd), lambda i: (0, i, 0))
    return pl.pallas_call(
        body, name=name, grid=(S // ts,), in_specs=[col(0), col(1), gsp, gsp, hsp, hsp, hsp],
        out_specs=[pl.BlockSpec((ts, 3 * A), lambda i: (i, 0)), gsp, gsp],
        out_shape=[jax.ShapeDtypeStruct((S, 3 * A), BF16), jax.ShapeDtypeStruct((1, hd), F32),
                   jax.ShapeDtypeStruct((1, hd), F32)],
        compiler_params=_params("arbitrary"),
    )(P, P, gq, gk, dqn, dkn, dvh)


def _comm_begin(schedule, cin_ref, cout_ref, sems, grid):
    begin, end = schedule(cin_ref, cout_ref, *sems)
    ids = [pl.program_id(d) for d in range(len(grid))]
    pl.when(functools.reduce(jnp.logical_and, [p == 0 for p in ids]))(begin)

    def comm_end():
        pl.when(functools.reduce(jnp.logical_and, [p == g - 1 for p, g in zip(ids, grid)]))(end)

    return comm_end


def _call_with_comm(body, name, grid, in_specs, out_specs, out_shape, args, comm):
    if comm is None:
        return pl.pallas_call(body, name=name, grid=grid, in_specs=in_specs, out_specs=out_specs, out_shape=out_shape,
                              compiler_params=_params("parallel", "arbitrary"))(*args)
    _, xs, cshape = comm
    anyspec = pl.BlockSpec(memory_space=pl.ANY)
    return pl.pallas_call(
        body, name=name, grid=grid, in_specs=in_specs + [anyspec], out_specs=out_specs + [anyspec],
        out_shape=out_shape + [jax.ShapeDtypeStruct(cshape, xs.dtype)], scratch_shapes=_comm_scratch(),
        compiler_params=_params("arbitrary", "arbitrary"),
    )(*args, xs)


def _attn_fwd(q, k, v, umat, *, tq, tk, name, comm=None):
    H, S, hd = q.shape

    def body(*refs):
        if comm is None:
            q_ref, k_ref, v_ref, u_ref, o_ref = refs
        else:
            q_ref, k_ref, v_ref, u_ref, cin_ref, o_ref, cout_ref, *sems = refs
            comm_end = _comm_begin(comm[0], cin_ref, cout_ref, sems, (H, S // tq))
        i = pl.program_id(1)
        qb = q_ref[0]
        um = u_ref[...]
        qpos = lax.broadcasted_iota(jnp.int32, (tq, tk), 0) + i * tq
        kloc = lax.broadcasted_iota(jnp.int32, (tq, tk), 1)

        def step(j, carry, masked, row0=0):
            r, acc = carry
            ks = pl.multiple_of(j * tk, tk)
            kb = k_ref[0, pl.ds(ks, tk), :]
            vb = v_ref[0, pl.ds(ks, tk), :]
            z = lax.dot_general(qb[row0:], kb, _NT, preferred_element_type=F32)
            lb = jnp.minimum(z, 0.0) - jnp.log(1.0 + jnp.exp(-jnp.abs(z)))
            lm = lb - z
            if masked:
                m = (kloc[row0:] + j * tk) < qpos[row0:]
                lm = jnp.where(m, lm, 0.0)
            a = jnp.exp(lb + _block_sums(lm, um) + r)
            if masked:
                a = jnp.where(m, a, 0.0)
            acc = acc + jnp.dot(a.astype(BF16), vb, preferred_element_type=F32)
            return r + jnp.sum(lm, axis=1, keepdims=True), acc

        jd = (i * tq) // tk
        carry = (jnp.zeros((tq, 1), F32), jnp.zeros((tq, hd), F32))
        for dd in reversed(range(max(1, tq // tk))):
            carry = _diag_step(step, jd + dd, carry, dd * tk if tq > tk else 0)
        carry = _left_blocks(step, jd, carry, ATTN_FWD_UNROLL, tq // tk)
        o_ref[0] = carry[1]
        if comm is not None:
            comm_end()

    blk = pl.BlockSpec((1, tq, hd), lambda h, i: (h, i, 0))
    whole = pl.BlockSpec((1, S, hd), lambda h, i: (h, 0, 0))
    in_specs = [blk, whole, whole, pl.BlockSpec((tk, tk), lambda h, i: (0, 0))]
    out_specs, out_shape = [blk], [jax.ShapeDtypeStruct((H, S, hd), F32)]
    res = _call_with_comm(body, name, (H, S // tq), in_specs, out_specs, out_shape, (q, k, v, umat), comm)
    return res[0] if comm is None else res


def _attn_bwd(q, k, v, o, do, umat, *, tq, tk, name, comm=None):
    H, S, hd = q.shape

    def body(*refs):
        if comm is None:
            q_ref, k_ref, v_ref, o_ref, do_ref, u_ref, dq_ref, dk_ref, dv_ref = refs
        else:
            q_ref, k_ref, v_ref, o_ref, do_ref, u_ref, cin_ref, dq_ref, dk_ref, dv_ref, cout_ref, *sems = refs
            comm_end = _comm_begin(comm[0], cin_ref, cout_ref, sems, (H, S // tq))
        i = pl.program_id(1)

        @pl.when(i == 0)
        def _():
            dk_ref[...] = jnp.zeros_like(dk_ref)
            dv_ref[...] = jnp.zeros_like(dv_ref)

        qb = q_ref[0]
        do32 = do_ref[0]
        dob = do32.astype(BF16)
        tot = jnp.sum(dob.astype(F32) * o_ref[0], axis=1, keepdims=True)
        um = u_ref[...]
        qpos = lax.broadcasted_iota(jnp.int32, (tq, tk), 0) + i * tq
        kloc = lax.broadcasted_iota(jnp.int32, (tq, tk), 1)

        def step(j, carry, masked, row0=0):
            r, gs, dq = carry
            ks = pl.multiple_of(j * tk, tk)
            kb = k_ref[0, pl.ds(ks, tk), :]
            vb = v_ref[0, pl.ds(ks, tk), :]
            qs, dos = qb[row0:], dob[row0:]
            z = lax.dot_general(qs, kb, _NT, preferred_element_type=F32)
            lb = jnp.minimum(z, 0.0) - jnp.log(1.0 + jnp.exp(-jnp.abs(z)))
            sig = jnp.exp(lb)
            lm = lb - z
            if masked:
                m = (kloc[row0:] + j * tk) < qpos[row0:]
                lm = jnp.where(m, lm, 0.0)
            a = jnp.exp(lb + _block_sums(lm, um) + r)
            if masked:
                a = jnp.where(m, a, 0.0)
            ab = a.astype(BF16)
            g = lax.dot_general(dos, vb, _NT, preferred_element_type=F32) * ab.astype(F32)
            dz = g - sig * ((tot[row0:] - gs) - _block_sums(g, um, parts=2))
            if masked:
                dz = jnp.where(m, dz, 0.0)
            dzb = dz.astype(BF16)
            dq = dq + jnp.dot(dzb, kb, preferred_element_type=F32)
            dk_ref[0, pl.ds(ks, tk), :] += lax.dot_general(dzb, qs, _TN, preferred_element_type=F32)
            dv_ref[0, pl.ds(ks, tk), :] += lax.dot_general(ab, dos, _TN, preferred_element_type=F32)
            return r + jnp.sum(lm, axis=1, keepdims=True), gs + jnp.sum(g, axis=1, keepdims=True), dq

        jd = (i * tq) // tk
        zero = jnp.zeros((tq, 1), F32)
        carry = (zero, zero, jnp.zeros((tq, hd), F32))
        for dd in reversed(range(max(1, tq // tk))):
            carry = _diag_step(step, jd + dd, carry, dd * tk if tq > tk else 0)
        carry = _left_blocks(step, jd, carry, ATTN_BWD_UNROLL, tq // tk)
        dq_ref[0] = carry[2]
        if comm is not None:
            comm_end()

    blk = pl.BlockSpec((1, tq, hd), lambda h, i: (h, i, 0))
    whole = pl.BlockSpec((1, S, hd), lambda h, i: (h, 0, 0))
    shp = jax.ShapeDtypeStruct((H, S, hd), F32)
    in_specs = [blk, whole, whole, blk, blk, pl.BlockSpec((tk, tk), lambda h, i: (0, 0))]
    return _call_with_comm(body, name, (H, S // tq), in_specs, [blk, whole, whole], [shp, shp, shp],
                           (q, k, v, o, do, umat), comm)


def _merge_fwd(ya, yb, o, wb, P, b_gate, *, gate_col0, name):
    S, C = ya.shape
    H, _, hd = o.shape
    D = wb.shape[2]
    ts = _tile(S, 512, 8)

    def body(y0, y1, oh_ref, wb_ref, g0, g1, g2, bg_ref, m_ref, yc_ref):
        for h in range(H):
            yc_ref[:, h * hd:(h + 1) * hd] = oh_ref[h].astype(BF16)
        acc = jnp.zeros((ts, D), F32)
        for n, (y_ref, g_ref) in enumerate(((y0, g0), (y1, g1), (yc_ref, g2))):
            yd = jnp.dot(y_ref[...], wb_ref[n], preferred_element_type=F32)
            acc = acc + _sigmoid(g_ref[...] + bg_ref[:, n * D:(n + 1) * D]) * yd
        m_ref[...] = acc.astype(BF16)

    ysp = pl.BlockSpec((ts, C), lambda i: (i, 0))
    gsp = lambda n: pl.BlockSpec((ts, D), lambda i: (i, gate_col0 + n))
    return pl.pallas_call(
        body, name=name, grid=(S // ts,),
        in_specs=[ysp, ysp, pl.BlockSpec((H, ts, hd), lambda i: (0, i, 0)), pl.BlockSpec((3, C, D), lambda i: (0, 0, 0)),
                  gsp(0), gsp(1), gsp(2), pl.BlockSpec((1, 3 * D), lambda i: (0, 0))],
        out_specs=[pl.BlockSpec((ts, D), lambda i: (i, 0)), ysp],
        out_shape=[jax.ShapeDtypeStruct((S, D), BF16), jax.ShapeDtypeStruct((S, C), BF16)],
        compiler_params=_params("parallel"),
    )(ya, yb, o, wb, P, P, P, b_gate)


def _merge_bwd(ys, wb, wbT, P, b_gate, dmerged, *, gate_col0, hd, name):
    S, C = ys[0].shape
    D = wb.shape[2]
    H = C // hd
    ts = _tile(S, 256, 8)

    def body(y0, y1, y2, wb_ref, wbT_ref, g0, g1, g2, bg_ref, dm_ref,
             dg_ref, dyd0, dyd1, dyd2, dya_ref, dyb_ref, do_ref, dbg_ref):
        i = pl.program_id(0)
        dm = dm_ref[...]
        parts = []
        for n, (y_ref, g_ref, dyd_ref) in enumerate(((y0, g0, dyd0), (y1, g1, dyd1), (y2, g2, dyd2))):
            yd = jnp.dot(y_ref[...], wb_ref[n], preferred_element_type=F32)
            sg = _sigmoid(g_ref[...] + bg_ref[:, n * D:(n + 1) * D])
            dgate = dm * yd * sg * (1.0 - sg)
            dg_ref[:, n * D:(n + 1) * D] = dgate.astype(BF16)
            parts.append(jnp.sum(dgate, axis=0, keepdims=True))
            dyd = (dm * sg).astype(BF16)
            dyd_ref[...] = dyd
            dy = jnp.dot(dyd, wbT_ref[n], preferred_element_type=F32)
            if n == 0:
                dya_ref[...] = dy
            elif n == 1:
                dyb_ref[...] = dy
            else:
                for h in range(H):
                    do_ref[h] = dy[:, h * hd:(h + 1) * hd]
        part = jnp.concatenate(parts, axis=1)

        @pl.when(i == 0)
        def _():
            dbg_ref[...] = part

        @pl.when(i > 0)
        def _():
            dbg_ref[...] += part

    ysp = pl.BlockSpec((ts, C), lambda i: (i, 0))
    dsp = pl.BlockSpec((ts, D), lambda i: (i, 0))
    gsp = lambda n: pl.BlockSpec((ts, D), lambda i: (i, gate_col0 + n))
    dshp = jax.ShapeDtypeStruct((S, D), BF16)
    yshp = jax.ShapeDtypeStruct((S, C), F32)
    return pl.pallas_call(
        body, name=name, grid=(S // ts,),
        in_specs=[ysp, ysp, ysp, pl.BlockSpec((3, C, D), lambda i: (0, 0, 0)),
                  pl.BlockSpec((3, D, C), lambda i: (0, 0, 0)), gsp(0), gsp(1), gsp(2),
                  pl.BlockSpec((1, 3 * D), lambda i: (0, 0)), dsp],
        out_specs=[pl.BlockSpec((ts, 3 * D), lambda i: (i, 0)), dsp, dsp, dsp, ysp, ysp,
                   pl.BlockSpec((H, ts, hd), lambda i: (0, i, 0)), pl.BlockSpec((1, 3 * D), lambda i: (0, 0))],
        out_shape=[jax.ShapeDtypeStruct((S, 3 * D), BF16), dshp, dshp, dshp, yshp, yshp,
                   jax.ShapeDtypeStruct((H, S, hd), F32), jax.ShapeDtypeStruct((1, 3 * D), F32)],
        compiler_params=_params("arbitrary"),
    )(*ys, wb, wbT, P, P, P, b_gate, dmerged)


def _swiglu_fwd(gu, *, name):
    S, F2 = gu.shape
    F = F2 // 2
    ts, tf = _tile(S, 512, 8), _tile(F, 1536, LANES)
    nf = F // tf

    def body(g_ref, u_ref, o_ref):
        gt = g_ref[...]
        o_ref[...] = (gt * _sigmoid(gt) * u_ref[...]).astype(BF16)

    return pl.pallas_call(
        body, name=name, grid=(S // ts, nf),
        in_specs=[pl.BlockSpec((ts, tf), lambda i, j: (i, j)), pl.BlockSpec((ts, tf), lambda i, j: (i, j + nf))],
        out_specs=pl.BlockSpec((ts, tf), lambda i, j: (i, j)),
        out_shape=jax.ShapeDtypeStruct((S, F), BF16),
        compiler_params=_params("parallel", "parallel"),
    )(gu, gu)


def _swiglu_bwd(gu, dact, *, name):
    S, F2 = gu.shape
    F = F2 // 2
    ts = _tile(S, 256, 8)

    def body(gu_ref, d_ref, o_ref):
        gt, up, da = gu_ref[:, 0:F], gu_ref[:, F:F2], d_ref[...]
        sg = _sigmoid(gt)
        o_ref[:, 0:F] = (da * up * sg * (1.0 + gt * (1.0 - sg))).astype(BF16)
        o_ref[:, F:F2] = (da * gt * sg).astype(BF16)

    return pl.pallas_call(
        body, name=name, grid=(S // ts,),
        in_specs=[pl.BlockSpec((ts, F2), lambda i: (i, 0)), pl.BlockSpec((ts, F), lambda i: (i, 0))],
        out_specs=pl.BlockSpec((ts, F2), lambda i: (i, 0)),
        out_shape=jax.ShapeDtypeStruct((S, F2), BF16),
        compiler_params=_params("parallel"),
    )(gu, dact)


def _loss_grad(y, target, *, name):
    S, D = y.shape
    ts = _tile(S, 512, 8)

    def body(y_ref, t_ref, dy_ref, l_ref, dyb_ref):
        i = pl.program_id(0)
        err = y_ref[...] - t_ref[...]
        dy_ref[...] = err * (1.0 / D)
        dyb_ref[...] = (err * (1.0 / D)).astype(BF16)
        part = jnp.broadcast_to(jnp.sum(jnp.sum(err * err, axis=1, keepdims=True), axis=0, keepdims=True) * (0.5 / D),
                                (1, LANES))

        @pl.when(i == 0)
        def _():
            l_ref[...] = part

        @pl.when(i > 0)
        def _():
            l_ref[...] += part

    row = pl.BlockSpec((ts, D), lambda i: (i, 0))
    return pl.pallas_call(
        body, name=name, grid=(S // ts,), in_specs=[row, row],
        out_specs=[row, pl.BlockSpec((1, LANES), lambda i: (0, 0)), row],
        out_shape=[jax.ShapeDtypeStruct((S, D), F32), jax.ShapeDtypeStruct((1, LANES), F32),
                   jax.ShapeDtypeStruct((S, D), BF16)],
        compiler_params=_params("arbitrary"),
    )(y, target)


def _adamw(w, m, v, gs_list, *, name):
    R, W = w.shape
    rows = [g.shape[1] for g in gs_list]
    tr = _tile(math.gcd(*rows), max(16, (2048 * LANES // W) // 16 * 16), 16)
    first = [sum(rows[:t]) // tr for t in range(len(rows))]
    c1 = 1.0 / (1.0 - ADAM_B1 ** ADAM_STEP)
    c2 = 1.0 / (1.0 - ADAM_B2 ** ADAM_STEP)

    def body(w_ref, m_ref, v_ref, *refs):
        gs_refs, (g_ref, d_ref, nm_ref, nv_ref) = refs[:len(rows)], refs[len(rows):]
        i = pl.program_id(0)
        g = None
        for t, gs_ref in enumerate(gs_refs):
            gt = gs_ref[0].astype(F32)
            for s in range(1, gs_ref.shape[0]):
                gt = gt + gs_ref[s].astype(F32)
            g = gt if g is None else jnp.where(i >= first[t], gt, g)
        nm = ADAM_B1 * m_ref[...] + (1.0 - ADAM_B1) * g
        nv = ADAM_B2 * v_ref[...] + (1.0 - ADAM_B2) * (g * g)
        g_ref[...] = g
        nm_ref[...] = nm
        nv_ref[...] = nv
        d_ref[...] = -ADAM_LR * ((nm * c1) / (jnp.sqrt(nv * c2) + ADAM_EPS) + ADAM_WD * w_ref[...])

    row = pl.BlockSpec((tr, W), lambda i: (i, 0))
    slots = [pl.BlockSpec((g.shape[0], tr, W),
                          lambda i, b0=first[t], nb=rows[t] // tr: (0, jnp.clip(i - b0, 0, nb - 1), 0))
             for t, g in enumerate(gs_list)]
    shp = jax.ShapeDtypeStruct((R, W), F32)
    return pl.pallas_call(
        body, name=name, grid=(R // tr,), in_specs=[row, row, row] + slots,
        out_specs=[row, row, row, row], out_shape=[shp, shp, shp, shp],
        compiler_params=_params("parallel"),
    )(w, m, v, *gs_list)


def _slot_sum(gs, *, name):
    ns, R, _ = gs.shape
    tr = _tile(R, 2048, 16)

    def body(gs_ref, o_ref):
        g = gs_ref[0]
        for s in range(1, ns):
            g = g + gs_ref[s]
        o_ref[...] = g

    return pl.pallas_call(
        body, name=name, grid=(R // tr,),
        in_specs=[pl.BlockSpec((ns, tr, LANES), lambda i: (0, i, 0))],
        out_specs=pl.BlockSpec((tr, LANES), lambda i: (i, 0)),
        out_shape=jax.ShapeDtypeStruct((R, LANES), F32),
        compiler_params=_params("parallel"),
    )(gs)


def _comm_scratch():
    return [pltpu.SemaphoreType.DMA((7,)), pltpu.SemaphoreType.DMA((7,)), pltpu.SemaphoreType.DMA]


def _gather_schedule(x_ref, out_ref, send_sems, recv_sems, local_sem):
    x, y, c = lax.axis_index("x"), lax.axis_index("y"), lax.axis_index("c")
    me, sibling = (x, y, c), (x, y, 1 - c)
    chips = [(1 - x, y), (x, 1 - y), (1 - x, 1 - y)]

    def slot(px, py, pc):
        return out_ref.at[4 * px + 2 * py + pc]

    def copy(k, block, to, src=None):
        return pltpu.make_async_remote_copy(
            src_ref=slot(*block) if src is None else src, dst_ref=slot(*block),
            send_sem=send_sems.at[k], recv_sem=recv_sems.at[k], device_id=to, device_id_type=MESH)

    mine = pltpu.make_async_copy(x_ref, slot(*me), local_sem)
    first = [copy(0, me, sibling, src=x_ref)]
    first += [copy(1 + j, me, (*chip, c), src=x_ref) for j, chip in enumerate(chips)]
    passed = [copy(4 + j, (*chip, c), sibling) for j, chip in enumerate(chips)]

    def begin():
        mine.start()
        for cp in first:
            cp.start()

    def end():
        for j, chip in enumerate(chips):
            copy(1 + j, (*chip, c), me).wait_recv()
            passed[j].start()
        copy(0, sibling, me).wait_recv()
        for j, chip in enumerate(chips):
            copy(4 + j, (*chip, 1 - c), me).wait_recv()
        for cp in first + passed:
            cp.wait_send()
        mine.wait()

    return begin, end


def _comm_call(schedule, xs, out_shape, *, name):
    def body(x_ref, out_ref, send_sems, recv_sems, local_sem):
        begin, end = schedule(x_ref, out_ref, send_sems, recv_sems, local_sem)
        begin()
        end()

    return pl.pallas_call(
        body, name=name,
        in_specs=[pl.BlockSpec(memory_space=pl.ANY)], out_specs=pl.BlockSpec(memory_space=pl.ANY),
        out_shape=jax.ShapeDtypeStruct(out_shape, xs.dtype), scratch_shapes=_comm_scratch(),
        compiler_params=pltpu.CompilerParams(has_side_effects=True),
    )(xs)


def _all_gather(xs, *, name):
    return _comm_call(_gather_schedule, xs, (N_DEV,) + xs.shape, name=name)


def _exchange_schedule(x_ref, out_ref, send_sems, recv_sems, local_sem):
    x, y, c = lax.axis_index("x"), lax.axis_index("y"), lax.axis_index("c")
    me = 4 * x + 2 * y + c
    mine = pltpu.make_async_copy(x_ref.at[me], out_ref.at[me], local_sem)
    sends, recvs = [], []
    for k in range(1, N_DEV):
        px = 1 - x if k & 4 else x
        py = 1 - y if k & 2 else y
        pc = 1 - c if k & 1 else c
        peer = 4 * px + 2 * py + pc
        sends.append(pltpu.make_async_remote_copy(
            src_ref=x_ref.at[peer], dst_ref=out_ref.at[me], send_sem=send_sems.at[k - 1],
            recv_sem=recv_sems.at[k - 1], device_id=(px, py, pc), device_id_type=MESH))
        recvs.append(pltpu.make_async_remote_copy(
            src_ref=x_ref.at[me], dst_ref=out_ref.at[peer], send_sem=send_sems.at[k - 1],
            recv_sem=recv_sems.at[k - 1], device_id=(px, py, pc), device_id_type=MESH))

    def begin():
        mine.start()
        for cp in sends:
            cp.start()

    def end():
        for cp in recvs:
            cp.wait_recv()
        for cp in sends:
            cp.wait_send()
        mine.wait()

    return begin, end


PACK_ROWS = 16


def _pack_rows(parts, dtype, lead=()):
    rows = []
    for p in parts:
        r = p.reshape(lead + (-1, LANES)).astype(dtype)
        pad = (-r.shape[-2]) % PACK_ROWS
        rows.append(jnp.pad(r, [(0, 0)] * len(lead) + [(0, pad), (0, 0)]) if pad else r)
    return jnp.concatenate(rows, axis=len(lead))


def _unpack_rows(packed, shapes, lead=()):
    out, off = [], 0
    for shp in shapes:
        n = math.prod(shp) // LANES
        out.append(lax.slice_in_dim(packed, off, off + n, axis=len(lead)).reshape(lead + tuple(shp)))
        off += n + (-n) % PACK_ROWS
    return out


def _unshard(gathered, axis):
    g = jnp.moveaxis(gathered, 0, axis)
    shp = list(g.shape)
    shp[axis:axis + 2] = [shp[axis] * shp[axis + 1]]
    return g.reshape(shp)


def _reshard(full, axis):
    shp = list(full.shape)
    shp[axis:axis + 1] = [N_DEV, shp[axis] // N_DEV]
    return jnp.moveaxis(full.reshape(shp), axis, 0)


def _late_parts(L):
    return [(k, l) for k in BIG for l in range(L) if (k, l) != ("w_in", 0)]


def _local_step(x, target, w_in0, late_pack, late_shapes, sm):
    S, D = x.shape
    L = sm["mix_norm_g"].shape[0]
    late = _late_parts(L)
    wf = {k: [None] * L for k in BIG}
    wf["w_in"][0] = w_in0
    C = sm["conv_b"].shape[1]
    W = sm["sgu_ln_g"].shape[1]
    hd = sm["q_norm_g"].shape[1]
    G, CH = sm["sgu_w"].shape[1], sm["sgu_w"].shape[2]
    A = w_in0.shape[1] - (3 * C + 2 * W + 3 * D)
    A = A // 3
    H = A // hd
    col_q = 3 * C + 2 * W
    qscale = 1.0 / math.sqrt(hd)
    tril = jnp.tril(jnp.ones((CH, CH), F32))
    (ftq, ftk), (btq, btk) = [(_tile(S, a, LANES), _tile(S, b, LANES)) for a, b in (ATTN_FWD_TILES, ATTN_BWD_TILES)]
    umat = lambda t: (lax.broadcasted_iota(jnp.int32, (t, t), 0) > lax.broadcasted_iota(jnp.int32, (t, t), 1)).astype(BF16)

    saved = []
    for l in range(L):
        n = f"l{l}_"
        g1 = sm["mix_norm_g"][l][None]
        h = _rmsnorm_fwd(x, g1, scale=1.0, out_dtype=BF16, name=n + "mixnorm")
        P = _matmul(h, wf["w_in"][l], name=n + "w_in")
        cw = jnp.pad(sm["conv_w"][l], ((0, 5), (0, 0)))
        cb = sm["conv_b"][l][None]
        ya = _conv_fwd(P, cw, cb, C=C, name=n + "conv")
        wm = (sm["sgu_w"][l] * tril).astype(BF16)
        bT = sm["sgu_b"][l].T
        lng, lnb = sm["sgu_ln_g"][l][None], sm["sgu_ln_b"][l][None]
        yb = _sgu_fwd(P, lng, lnb, wm, bT, W=W, cu=(3 * C) // W, cv=(3 * C) // W + 1, name=n + "sgu")
        gq, gk = sm["q_norm_g"][l][None], sm["k_norm_g"][l][None]
        qn, kn, vh = _qkv_heads_fwd(P, gq, gk, A=A, col0=col_q // A, hd=hd, qscale=qscale, name=n + "qkv")
        if l == 0:
            o, gathered = _attn_fwd(qn, kn, vh, umat(ftk), tq=ftq, tk=ftk, name=n + "attn",
                                    comm=(_gather_schedule, late_pack, (N_DEV,) + late_pack.shape))
            for (k, ll), part in zip(late, _unpack_rows(gathered, late_shapes, lead=(N_DEV,))):
                wf[k][ll] = _unshard(part, SHARD_AXIS[k] - 1)
        else:
            o = _attn_fwd(qn, kn, vh, umat(ftk), tq=ftq, tk=ftk, name=n + "attn")
        bg = sm["b_gate"][l][None]
        gate_col0 = (col_q + 3 * A) // D
        merged, yc = _merge_fwd(ya, yb, o, wf["w_branch_out"][l], P, bg, gate_col0=gate_col0, name=n + "merge")
        x1 = _matmul(merged, wf["w_o"][l], res=x, name=n + "w_o")
        g2 = sm["ffn_norm_g"][l][None]
        h2 = _rmsnorm_fwd(x1, g2, scale=1.0, out_dtype=BF16, name=n + "ffnnorm")
        gu = _matmul(h2, wf["w_gate_up"][l], name=n + "w_gate_up")
        act = _swiglu_fwd(gu, name=n + "swiglu")
        x2 = _matmul(act, wf["w_down"][l], res=x1, name=n + "w_down")
        saved.append(dict(x=x, h=h, P=P, cw=cw, cb=cb, ya=ya, wm=wm, bT=bT, lng=lng, lnb=lnb, yb=yb,
                          gq=gq, gk=gk, qn=qn, kn=kn, vh=vh, o=o, yc=yc, bg=bg,
                          gate_col0=gate_col0, merged=merged, x1=x1, g1=g1, g2=g2, h2=h2, gu=gu, act=act))
        x = x2

    dx, lpart, dxb = _loss_grad(x, target, name="loss")
    grads = {k: [None] * L for k in WEIGHTS}
    chunked = lambda k, g: g if k == "w_in" else _reshard(g, SHARD_AXIS[k] - 1)
    for l in reversed(range(L)):
        n = f"l{l}_b_"
        sv = saved[l]
        grads["w_down"][l] = _matmul_tn(sv["act"], dxb, name=n + "g_w_down", out_dtype=BF16)
        dact = _matmul(dxb, wf["w_down"][l].T, name=n + "d_act")
        dgu = _swiglu_bwd(sv["gu"], dact, name=n + "swiglu")
        grads["w_gate_up"][l] = _matmul_tn(sv["h2"], dgu, name=n + "g_w_gate_up", out_dtype=BF16)
        dh2 = _matmul(dgu, wf["w_gate_up"][l].T, name=n + "d_h2")
        dx1, dg2, dx1b = _rmsnorm_bwd(sv["x1"], sv["g2"], dh2, scale=1.0, dres=dx, name=n + "ffnnorm",
                                      bf16_copy=True)
        grads["ffn_norm_g"][l] = dg2[0]
        grads["w_o"][l] = _matmul_tn(sv["merged"], dx1b, name=n + "g_w_o", out_dtype=BF16)
        dmerged = _matmul(dx1b, wf["w_o"][l].T, name=n + "d_merged")
        ys = (sv["ya"], sv["yb"], sv["yc"])
        wb = wf["w_branch_out"][l]
        dgates, *dyd, dya, dyb, do, dbg = _merge_bwd(ys, wb, wb.transpose(0, 2, 1), sv["P"], sv["bg"], dmerged,
                                                     gate_col0=sv["gate_col0"], hd=hd, name=n + "merge")
        grads["b_gate"][l] = dbg[0]
        grads["w_branch_out"][l] = jnp.stack(
            [_matmul_tn(ys[i], dyd[i], name=n + f"g_w_branch{i}", out_dtype=BF16) for i in range(3)])
        dconv, dcw = _conv_bwd(sv["P"], dya, sv["cw"], sv["cb"], C=C, name=n + "conv")
        grads["conv_w"][l], grads["conv_b"][l] = dcw[0:3], dcw[3]
        wmT = sv["wm"].transpose(0, 2, 1)
        dsgu, dsw, dsb, dln = _sgu_bwd(sv["P"], dyb, sv["lng"], sv["lnb"], sv["wm"], wmT, sv["bT"], W=W,
                                       cu=(3 * C) // W, cv=(3 * C) // W + 1, name=n + "sgu")
        grads["sgu_w"][l], grads["sgu_b"][l] = dsw * tril, dsb[:, :, 0]
        grads["sgu_ln_g"][l], grads["sgu_ln_b"][l] = dln[0], dln[1]
        if l == 0:
            chunks = _pack_rows([chunked(k, grads[k][ll]) for k, ll in late], BF16, lead=(N_DEV,))
            dqn, dkn, dvh, late_recv = _attn_bwd(sv["qn"], sv["kn"], sv["vh"], sv["o"], do, umat(btk), tq=btq, tk=btk,
                                                 name=n + "attn", comm=(_exchange_schedule, chunks, chunks.shape))
        else:
            dqn, dkn, dvh = _attn_bwd(sv["qn"], sv["kn"], sv["vh"], sv["o"], do, umat(btk), tq=btq, tk=btk,
                                      name=n + "attn")
        dqkv, dgq, dgk = _qkv_heads_bwd(sv["P"], sv["gq"], sv["gk"], dqn, dkn, dvh, A=A, col0=col_q // A, hd=hd,
                                        qscale=qscale, name=n + "qkv")
        grads["q_norm_g"][l], grads["k_norm_g"][l] = dgq[0], dgk[0]
        dP = jnp.concatenate([dconv, dsgu, dqkv, dgates], axis=1)
        grads["w_in"][l] = _matmul_tn(sv["h"], dP, name=n + "g_w_in", out_dtype=BF16, shard=dP.shape[1] // N_DEV)
        if l == 0:
            chunks = _pack_rows([grads["w_in"][0]], BF16, lead=(N_DEV,))
            dh, early_recv = _matmul(dP, wf["w_in"][l].T, name=n + "d_h",
                                     comm=(_exchange_schedule, chunks, chunks.shape))
        else:
            dh = _matmul(dP, wf["w_in"][l].T, name=n + "d_h")
        dx, dg1, dxb = _rmsnorm_bwd(sv["x"], sv["g1"], dh, scale=1.0, dres=dx1, name=n + "mixnorm", bf16_copy=True)
        grads["mix_norm_g"][l] = dg1[0]
    return lpart[0, 0], dx, grads, early_recv, late_recv


def kernel(x, mix_norm_g, w_in, b_gate, conv_w, conv_b, sgu_ln_g, sgu_ln_b, sgu_w, sgu_b, q_norm_g, k_norm_g, w_branch_out, w_o, ffn_norm_g, w_gate_up, w_down, loss_target, m_mix_norm_g, m_w_in, m_b_gate, m_conv_w, m_conv_b, m_sgu_ln_g, m_sgu_ln_b, m_sgu_w, m_sgu_b, m_q_norm_g, m_k_norm_g, m_w_branch_out, m_w_o, m_ffn_norm_g, m_w_gate_up, m_w_down, v_mix_norm_g, v_w_in, v_b_gate, v_conv_w, v_conv_b, v_sgu_ln_g, v_sgu_ln_b, v_sgu_w, v_sgu_b, v_q_norm_g, v_k_norm_g, v_w_branch_out, v_w_o, v_ffn_norm_g, v_w_gate_up, v_w_down):
    w = dict(mix_norm_g=mix_norm_g, w_in=w_in, b_gate=b_gate, conv_w=conv_w, conv_b=conv_b, sgu_ln_g=sgu_ln_g,
             sgu_ln_b=sgu_ln_b, sgu_w=sgu_w, sgu_b=sgu_b, q_norm_g=q_norm_g, k_norm_g=k_norm_g,
             w_branch_out=w_branch_out, w_o=w_o, ffn_norm_g=ffn_norm_g, w_gate_up=w_gate_up, w_down=w_down)
    m = dict(mix_norm_g=m_mix_norm_g, w_in=m_w_in, b_gate=m_b_gate, conv_w=m_conv_w, conv_b=m_conv_b,
             sgu_ln_g=m_sgu_ln_g, sgu_ln_b=m_sgu_ln_b, sgu_w=m_sgu_w, sgu_b=m_sgu_b, q_norm_g=m_q_norm_g,
             k_norm_g=m_k_norm_g, w_branch_out=m_w_branch_out, w_o=m_w_o, ffn_norm_g=m_ffn_norm_g,
             w_gate_up=m_w_gate_up, w_down=m_w_down)
    v = dict(mix_norm_g=v_mix_norm_g, w_in=v_w_in, b_gate=v_b_gate, conv_w=v_conv_w, conv_b=v_conv_b,
             sgu_ln_g=v_sgu_ln_g, sgu_ln_b=v_sgu_ln_b, sgu_w=v_sgu_w, sgu_b=v_sgu_b, q_norm_g=v_q_norm_g,
             k_norm_g=v_k_norm_g, w_branch_out=v_w_branch_out, w_o=v_w_o, ffn_norm_g=v_ffn_norm_g,
             w_gate_up=v_w_gate_up, w_down=v_w_down)
    me = 4 * lax.axis_index("x") + 2 * lax.axis_index("y") + lax.axis_index("c")
    S = x.shape[1]

    L = w_in.shape[0]
    late = _late_parts(L)
    late_shapes = [w[k].shape[1:] for k, _ in late]
    w_in0 = _unshard(_all_gather(_pack_rows([w_in[0]], BF16), name="gather_w_in0").reshape((N_DEV,) + w_in.shape[1:]),
                     SHARD_AXIS["w_in"] - 1)
    late_pack = _pack_rows([w[k][l] for k, l in late], BF16)
    conv_g = _all_gather(_pack_rows([conv_w], F32), name="gather_conv_w")
    sm = {k: w[k] for k in SMALL}
    sm["conv_w"] = _unshard(_unpack_rows(conv_g, [conv_w.shape], lead=(N_DEV,))[0], 2)

    lpart, dx, grads, early_recv, late_recv = _local_step(x[0], loss_target[0], w_in0, late_pack, late_shapes, sm)
    loss = lax.psum(lpart, ("x", "y", "c"))

    two_d = lambda a: a.reshape(-1, a.shape[-1])
    slots_of = lambda recv, r0, r1, width: lax.slice_in_dim(recv, r0, r1, axis=1).reshape(N_DEV, -1, width)
    out, off = {}, 0
    for k in BIG:
        n = math.prod(w[k].shape[1:]) // LANES
        if k == "w_in":
            gs_list = [slots_of(early_recv, 0, n, w[k].shape[-1])]
            if L > 1:
                gs_list.append(slots_of(late_recv, 0, (L - 1) * n, w[k].shape[-1]))
            off += (L - 1) * n
        else:
            gs_list = [slots_of(late_recv, off, off + L * n, w[k].shape[-1])]
            off += L * n
        res = _adamw(two_d(w[k]), two_d(m[k]), two_d(v[k]), gs_list, name="adamw_" + k)
        for nm, r in zip(("grad", "delta", "new_m", "new_v"), res):
            out[nm, k] = r.reshape(w[k].shape)

    small_grads = [jnp.stack(grads[k]) for k in SMALL]
    small_shapes = [g.shape for g in small_grads]
    sg = _all_gather(_pack_rows(small_grads, F32), name="gather_small_grads")
    gsum = _slot_sum(sg, name="sum_small_grads")
    gsmall = dict(zip(SMALL, _unpack_rows(gsum, small_shapes)))
    cshard = conv_w.shape[2]
    gsmall["conv_w"] = lax.dynamic_slice_in_dim(gsmall["conv_w"], me * cshard, cshard, axis=2)
    own_shapes = [w[k].shape for k in SMALL]
    gs_, ds_, ms_, vs_ = _adamw(_pack_rows([w[k] for k in SMALL], F32), _pack_rows([m[k] for k in SMALL], F32),
                                _pack_rows([v[k] for k in SMALL], F32),
                                [_pack_rows([gsmall[k] for k in SMALL], F32)[None]], name="adamw_small")
    for nm, packed in (("grad", gs_), ("delta", ds_), ("new_m", ms_), ("new_v", vs_)):
        for k, a in zip(SMALL, _unpack_rows(packed, own_shapes)):
            out[nm, k] = a

    res = [loss, dx[None]]
    for nm in ("grad", "delta", "new_m", "new_v"):
        res += [out[nm, k] for k in WEIGHTS]
    return tuple(res)
```

```python
import functools
import math

import jax
import jax.numpy as jnp
from jax import lax
from jax.experimental import pallas as pl
from jax.experimental.pallas import tpu as pltpu

F32 = jnp.float32
BF16 = jnp.bfloat16
MESH = pl.DeviceIdType.MESH

N_DEV = 8
LANES = 128
VMEM_LIMIT_BYTES = 56 * 1024 * 1024
EPS = 1e-6
ADAM_LR, ADAM_B1, ADAM_B2, ADAM_EPS, ADAM_WD, ADAM_STEP = 0.001, 0.9, 0.999, 1e-08, 0.01, 10
ATTN_FWD_TILES = (512, 512)
ATTN_BWD_TILES = (1024, 256)
ATTN_FWD_UNROLL = 4
ATTN_BWD_UNROLL = 4
BIG = ("w_in", "w_branch_out", "w_o", "w_gate_up", "w_down")
SMALL = ("mix_norm_g", "b_gate", "conv_w", "conv_b", "sgu_ln_g", "sgu_ln_b", "sgu_w", "sgu_b",
         "q_norm_g", "k_norm_g", "ffn_norm_g")
WEIGHTS = ("mix_norm_g", "w_in", "b_gate", "conv_w", "conv_b", "sgu_ln_g", "sgu_ln_b", "sgu_w", "sgu_b",
           "q_norm_g", "k_norm_g", "w_branch_out", "w_o", "ffn_norm_g", "w_gate_up", "w_down")
SHARD_AXIS = {"w_in": 2, "w_branch_out": 3, "w_o": 1, "w_gate_up": 2, "w_down": 1}


def _tile(n, cap, mult):
    best = None
    for t in range(mult, min(n, cap) + 1, mult):
        if n % t == 0:
            best = t
    return best if best is not None else n


def _params(*sem):
    return pltpu.CompilerParams(dimension_semantics=sem if sem else None, vmem_limit_bytes=VMEM_LIMIT_BYTES)


def _erf(x):
    return lax.erf(x)


def _gelu(x):
    return 0.5 * x * (1.0 + _erf(x * (1.0 / math.sqrt(2.0))))


def _gelu_grad(x):
    return 0.5 * (1.0 + _erf(x * (1.0 / math.sqrt(2.0)))) + x * jnp.exp(-0.5 * x * x) * (1.0 / math.sqrt(2.0 * math.pi))


def _sigmoid(x):
    return 1.0 / (1.0 + jnp.exp(-x))


def _matmul(a, b, *, name, res=None, out_dtype=F32, comm=None):
    M, K = a.shape
    _, N = b.shape
    tm, tn, tk = _tile(M, 1024, 8), _tile(N, 1536, LANES), _tile(K, 1536, LANES)
    nk = K // tk
    grid = (M // tm, N // tn, nk)
    has_res = res is not None

    def body(*refs):
        refs = list(refs)
        a_ref, b_ref = refs[:2]
        r_ref = refs[2] if has_res else None
        pos = 2 + has_res
        nc = len(comm[1]) if comm is not None else 0
        o_ref = refs[pos + nc]
        acc = refs[pos + 1 + 2 * nc] if nk > 1 else None
        if comm is not None:
            comm_end = _comm_begin(comm[0], refs[pos:pos + nc], refs[pos + nc + 1:pos + 2 * nc + 1],
                                   refs[pos + 2 * nc + 1 + (nk > 1):], grid)
        k = pl.program_id(2)
        part = jnp.dot(a_ref[...], b_ref[...], preferred_element_type=F32)

        def finish(v):
            if has_res:
                v = v + r_ref[...]
            o_ref[...] = v.astype(out_dtype)

        if nk == 1:
            finish(part)
        else:
            @pl.when(k == 0)
            def _():
                acc[...] = part

            @pl.when(jnp.logical_and(k > 0, k < nk - 1))
            def _():
                acc[...] += part

            @pl.when(k == nk - 1)
            def _():
                finish(acc[...] + part)

        if comm is not None:
            comm_end()

    in_specs = [pl.BlockSpec((tm, tk), lambda i, j, k: (i, k)), pl.BlockSpec((tk, tn), lambda i, j, k: (k, j))]
    args = [a, b]
    if has_res:
        in_specs.append(pl.BlockSpec((tm, tn), lambda i, j, k: (i, j)))
        args.append(res)
    out_specs = [pl.BlockSpec((tm, tn), lambda i, j, k: (i, j))]
    out_shape = [jax.ShapeDtypeStruct((M, N), out_dtype)]
    scratch = [pltpu.VMEM((tm, tn), F32)] if nk > 1 else []
    sem = ("parallel", "parallel", "arbitrary")
    if comm is not None:
        anyspec = pl.BlockSpec(memory_space=pl.ANY)
        in_specs += [anyspec] * len(comm[1])
        args += list(comm[1])
        out_specs += [anyspec] * len(comm[1])
        out_shape += [jax.ShapeDtypeStruct(c, x.dtype) for c, x in zip(comm[2], comm[1])]
        scratch += _comm_scratch(len(comm[1]))
        sem = ("arbitrary",) * 3
    out = pl.pallas_call(body, name=name, grid=grid, in_specs=in_specs, out_specs=out_specs, out_shape=out_shape,
                         scratch_shapes=scratch, compiler_params=_params(*sem))(*args)
    return out[0] if comm is None else (out[0], out[1:])


def _matmul_tn(x, y, *, name, out_dtype=F32, shard=None):
    S, A = x.shape
    _, B = y.shape
    ta, ts = _tile(A, 1536, LANES), _tile(S, 1024, 8)
    tb = shard if shard else _tile(B, 1536, LANES)
    ns = S // ts
    direct = out_dtype == F32
    view = (lambda r: r.at[0]) if shard else (lambda r: r)

    def body(x_ref, y_ref, o_ref, *scratch):
        s = pl.program_id(2)
        out = view(o_ref)
        acc = out if direct else scratch[0]
        part = lax.dot_general(x_ref[...], y_ref[...], (((0,), (0,)), ((), ())), preferred_element_type=F32)

        @pl.when(s == 0)
        def _():
            acc[...] = part

        @pl.when(s > 0)
        def _():
            acc[...] += part

        if not direct:
            @pl.when(s == ns - 1)
            def _():
                out[...] = acc[...].astype(out_dtype)

    if shard:
        out_spec, shape = pl.BlockSpec((1, ta, tb), lambda i, j, s: (j, i, 0)), (B // tb, A, tb)
    else:
        out_spec, shape = pl.BlockSpec((ta, tb), lambda i, j, s: (i, j)), (A, B)
    return pl.pallas_call(
        body, name=name, grid=(A // ta, B // tb, ns),
        in_specs=[pl.BlockSpec((ts, ta), lambda i, j, s: (s, i)), pl.BlockSpec((ts, tb), lambda i, j, s: (s, j))],
        out_specs=out_spec, out_shape=jax.ShapeDtypeStruct(shape, out_dtype),
        scratch_shapes=[] if direct else [pltpu.VMEM((ta, tb), F32)],
        compiler_params=_params("parallel", "parallel", "arbitrary"),
    )(x, y)


def _rmsnorm_fwd(x, g, *, scale, out_dtype, name):
    R, W = x.shape
    tr = _tile(R, 512 if W >= 512 else 4096, 8)

    def body(x_ref, g_ref, o_ref):
        xv = x_ref[...]
        r = lax.rsqrt(jnp.mean(xv * xv, axis=1, keepdims=True) + EPS)
        o_ref[...] = (xv * r * (g_ref[...] * scale)).astype(out_dtype)

    return pl.pallas_call(
        body, name=name, grid=(R // tr,),
        in_specs=[pl.BlockSpec((tr, W), lambda i: (i, 0)), pl.BlockSpec((1, W), lambda i: (0, 0))],
        out_specs=pl.BlockSpec((tr, W), lambda i: (i, 0)),
        out_shape=jax.ShapeDtypeStruct((R, W), out_dtype),
        compiler_params=_params("parallel"),
    )(x, g)


def _rmsnorm_bwd(x, g, dy, *, scale, name, dres=None, out_dtype=F32, bf16_copy=False):
    R, W = x.shape
    tr = _tile(R, 512 if W >= 512 else 4096, 8)
    has_res = dres is not None

    def body(*refs):
        refs = list(refs)
        dxb_ref = refs.pop() if bf16_copy else None
        if has_res:
            x_ref, g_ref, dy_ref, dres_ref, dx_ref, dg_ref = refs
        else:
            x_ref, g_ref, dy_ref, dx_ref, dg_ref = refs
        i = pl.program_id(0)
        xv = x_ref[...]
        dyv = dy_ref[...].astype(F32) * scale
        r = lax.rsqrt(jnp.mean(xv * xv, axis=1, keepdims=True) + EPS)
        u = dyv * g_ref[...]
        dx = r * u - xv * (r * r * r * jnp.mean(u * xv, axis=1, keepdims=True))
        if has_res:
            dx = dx + dres_ref[...]
        dx_ref[...] = dx.astype(out_dtype)
        if bf16_copy:
            dxb_ref[...] = dx.astype(BF16)
        part = jnp.sum(dyv * xv * r, axis=0, keepdims=True)

        @pl.when(i == 0)
        def _():
            dg_ref[...] = part

        @pl.when(i > 0)
        def _():
            dg_ref[...] += part

    row = pl.BlockSpec((tr, W), lambda i: (i, 0))
    one = pl.BlockSpec((1, W), lambda i: (0, 0))
    in_specs = [row, one, row] + ([row] if has_res else [])
    args = [x, g, dy] + ([dres] if has_res else [])
    extra = bool(bf16_copy)
    return pl.pallas_call(
        body, name=name, grid=(R // tr,), in_specs=in_specs, out_specs=[row, one] + [row] * extra,
        out_shape=[jax.ShapeDtypeStruct((R, W), out_dtype), jax.ShapeDtypeStruct((1, W), F32)]
        + [jax.ShapeDtypeStruct((R, W), BF16)] * extra,
        compiler_params=_params("arbitrary"),
    )(*args)


def _shift_down(u, prev, n):
    ts = u.shape[0]
    out = pltpu.roll(u, n, 0)
    row = lax.broadcasted_iota(jnp.int32, u.shape, 0)
    for r in range(n):
        out = jnp.where(row == r, prev[8 - n + r:8 - n + r + 1, :], out)
    return out


def _shift_up(u, nxt, n):
    ts = u.shape[0]
    out = pltpu.roll(u, ts - n, 0)
    row = lax.broadcasted_iota(jnp.int32, u.shape, 0)
    for r in range(n):
        out = jnp.where(row == ts - n + r, nxt[r:r + 1, :], out)
    return out


def _conv_fwd(P, conv_w, conv_b, *, C, name):
    S = P.shape[0]
    ts = _tile(S, 512, 8)
    hb = ts // 8

    def body(ab_ref, ac_ref, ax_ref, pc_ref, px_ref, w_ref, b_ref, o_ref):
        i = pl.program_id(0)
        u = ac_ref[...] * ax_ref[...]
        prev = pc_ref[...] * px_ref[...] * (i > 0).astype(F32)
        w = w_ref[...]
        y = b_ref[...] + w[0:1, :] * _shift_down(u, prev, 2) + w[1:2, :] * _shift_down(u, prev, 1) + w[2:3, :] * u
        o_ref[...] = (ab_ref[...] * y).astype(BF16)

    cur = lambda c: pl.BlockSpec((ts, C), lambda i: (i, c))
    prv = lambda c: pl.BlockSpec((8, C), lambda i: (jnp.maximum(i * hb - 1, 0), c))
    return pl.pallas_call(
        body, name=name, grid=(S // ts,),
        in_specs=[cur(0), cur(1), cur(2), prv(1), prv(2),
                  pl.BlockSpec((8, C), lambda i: (0, 0)), pl.BlockSpec((1, C), lambda i: (0, 0))],
        out_specs=pl.BlockSpec((ts, C), lambda i: (i, 0)),
        out_shape=jax.ShapeDtypeStruct((S, C), BF16),
        compiler_params=_params("parallel"),
    )(P, P, P, P, P, conv_w, conv_b)


def _conv_bwd(P, dya, conv_w, conv_b, *, C, name):
    S = P.shape[0]
    ts = _tile(S, 512, 8)
    hb = ts // 8
    last = S // 8 - 1
    n = S // ts

    def body(ab_ref, ac_ref, ax_ref, pc_ref, px_ref, dy_ref, nab_ref, ndy_ref, w_ref, b_ref, o_ref, dw_ref):
        i = pl.program_id(0)
        ab, ac, ax = ab_ref[...], ac_ref[...], ax_ref[...]
        u = ac * ax
        prev = pc_ref[...] * px_ref[...] * (i > 0).astype(F32)
        w = w_ref[...]
        u1, u2 = _shift_down(u, prev, 1), _shift_down(u, prev, 2)
        y = b_ref[...] + w[0:1, :] * u2 + w[1:2, :] * u1 + w[2:3, :] * u
        dya_v = dy_ref[...]
        dyp = dya_v * ab
        nxt = ndy_ref[...] * nab_ref[...] * (i < n - 1).astype(F32)
        du = w[2:3, :] * dyp + w[1:2, :] * _shift_up(dyp, nxt, 1) + w[0:1, :] * _shift_up(dyp, nxt, 2)
        o_ref[:, 0:C] = (dya_v * y).astype(BF16)
        o_ref[:, C:2 * C] = (du * ax).astype(BF16)
        o_ref[:, 2 * C:3 * C] = (du * ac).astype(BF16)
        part = jnp.concatenate([
            jnp.sum(dyp * u2, axis=0, keepdims=True), jnp.sum(dyp * u1, axis=0, keepdims=True),
            jnp.sum(dyp * u, axis=0, keepdims=True), jnp.sum(dyp, axis=0, keepdims=True),
            jnp.zeros((4, C), F32)], axis=0)

        @pl.when(i == 0)
        def _():
            dw_ref[...] = part

        @pl.when(i > 0)
        def _():
            dw_ref[...] += part

    cur = lambda c: pl.BlockSpec((ts, C), lambda i: (i, c))
    prv = lambda c: pl.BlockSpec((8, C), lambda i: (jnp.maximum(i * hb - 1, 0), c))
    nxt = lambda c: pl.BlockSpec((8, C), lambda i: (jnp.minimum((i + 1) * hb, last), c))
    return pl.pallas_call(
        body, name=name, grid=(n,),
        in_specs=[cur(0), cur(1), cur(2), prv(1), prv(2), cur(0), nxt(0), nxt(0),
                  pl.BlockSpec((8, C), lambda i: (0, 0)), pl.BlockSpec((1, C), lambda i: (0, 0))],
        out_specs=[pl.BlockSpec((ts, 3 * C), lambda i: (i, 0)), pl.BlockSpec((8, C), lambda i: (0, 0))],
        out_shape=[jax.ShapeDtypeStruct((S, 3 * C), BF16), jax.ShapeDtypeStruct((8, C), F32)],
        compiler_params=_params("arbitrary"),
    )(P, P, P, P, P, dya, P, dya, conv_w, conv_b)


def _sgu_fwd(P, ln_g, ln_b, wm, bT, *, W, cu, cv, name):
    S = P.shape[0]
    G, CH, _ = wm.shape
    gw = W // G
    ts = _tile(S, 512, CH)

    def body(u_ref, v_ref, g_ref, b_ref, wm_ref, bT_ref, o_ref):
        gv = _gelu(v_ref[...])
        mu = jnp.mean(gv, axis=1, keepdims=True)
        xc = gv - mu
        vn = (xc * lax.rsqrt(jnp.mean(xc * xc, axis=1, keepdims=True) + EPS) * g_ref[...] + b_ref[...]).astype(BF16)
        bT_v = bT_ref[...]
        for c in range(ts // CH):
            rows = slice(c * CH, (c + 1) * CH)
            for g in range(G):
                cols = slice(g * gw, (g + 1) * gw)
                mixed = jnp.dot(wm_ref[g], vn[rows, cols], preferred_element_type=F32) + bT_v[:, g:g + 1]
                o_ref[rows, cols] = (_gelu(u_ref[rows, cols]) * mixed).astype(BF16)

    full = lambda shp: pl.BlockSpec(shp, lambda i: (0,) * len(shp))
    return pl.pallas_call(
        body, name=name, grid=(S // ts,),
        in_specs=[pl.BlockSpec((ts, W), lambda i: (i, cu)), pl.BlockSpec((ts, W), lambda i: (i, cv)),
                  full((1, W)), full((1, W)), full((G, CH, CH)), full((CH, G))],
        out_specs=pl.BlockSpec((ts, W), lambda i: (i, 0)),
        out_shape=jax.ShapeDtypeStruct((S, W), BF16),
        compiler_params=_params("parallel"),
    )(P, P, ln_g, ln_b, wm, bT)


def _sgu_bwd(P, dyb, ln_g, ln_b, wm, wmT, bT, *, W, cu, cv, name):
    S = P.shape[0]
    G, CH, _ = wm.shape
    gw = W // G
    ts = _tile(S, 512, CH)

    def body(u_ref, v_ref, dy_ref, g_ref, b_ref, wm_ref, wmT_ref, bT_ref, o_ref, dw_ref, db_ref, dln_ref, dvn_ref):
        i = pl.program_id(0)

        @pl.when(i == 0)
        def _():
            dw_ref[...] = jnp.zeros_like(dw_ref)
            db_ref[...] = jnp.zeros_like(db_ref)
            dln_ref[...] = jnp.zeros_like(dln_ref)

        sv = v_ref[...]
        gv = _gelu(sv)
        mu = jnp.mean(gv, axis=1, keepdims=True)
        xc = gv - mu
        rstd = lax.rsqrt(jnp.mean(xc * xc, axis=1, keepdims=True) + EPS)
        xhat = xc * rstd
        lg = g_ref[...]
        vn = (xhat * lg + b_ref[...]).astype(BF16)
        bT_v = bT_ref[...]
        for c in range(ts // CH):
            rows = slice(c * CH, (c + 1) * CH)
            for g in range(G):
                cols = slice(g * gw, (g + 1) * gw)
                vn_cg = vn[rows, cols]
                mixed = jnp.dot(wm_ref[g], vn_cg, preferred_element_type=F32) + bT_v[:, g:g + 1]
                su = u_ref[rows, cols]
                dyv = dy_ref[rows, cols]
                dmix = dyv * _gelu(su)
                o_ref[rows, cols] = (dyv * mixed * _gelu_grad(su)).astype(BF16)
                dmix_b = dmix.astype(BF16)
                dw_ref[g] += lax.dot_general(dmix_b, vn_cg, (((1,), (1,)), ((), ())), preferred_element_type=F32)
                db_ref[g] += jnp.broadcast_to(jnp.sum(dmix, axis=1, keepdims=True), (CH, CH))
                dvn_ref[rows, cols] = jnp.dot(wmT_ref[g], dmix_b, preferred_element_type=F32)
        dvn = dvn_ref[...]
        dxh = dvn * lg
        dgv = rstd * (dxh - jnp.mean(dxh, axis=1, keepdims=True) - xhat * jnp.mean(dxh * xhat, axis=1, keepdims=True))
        o_ref[:, W:2 * W] = (dgv * _gelu_grad(sv)).astype(BF16)
        dln_ref[0:1, :] += jnp.sum(dvn * xhat, axis=0, keepdims=True)
        dln_ref[1:2, :] += jnp.sum(dvn, axis=0, keepdims=True)

    full = lambda shp: pl.BlockSpec(shp, lambda i: (0,) * len(shp))
    return pl.pallas_call(
        body, name=name, grid=(S // ts,),
        in_specs=[pl.BlockSpec((ts, W), lambda i: (i, cu)), pl.BlockSpec((ts, W), lambda i: (i, cv)),
                  pl.BlockSpec((ts, W), lambda i: (i, 0)),
                  full((1, W)), full((1, W)), full((G, CH, CH)), full((G, CH, CH)), full((CH, G))],
        out_specs=[pl.BlockSpec((ts, 2 * W), lambda i: (i, 0)), full((G, CH, CH)), full((G, CH, CH)), full((8, W))],
        out_shape=[jax.ShapeDtypeStruct((S, 2 * W), BF16), jax.ShapeDtypeStruct((G, CH, CH), F32),
                   jax.ShapeDtypeStruct((G, CH, CH), F32), jax.ShapeDtypeStruct((8, W), F32)],
        scratch_shapes=[pltpu.VMEM((ts, W), F32)],
        compiler_params=_params("arbitrary"),
    )(P, P, dyb, ln_g, ln_b, wm, wmT, bT)


def _block_sums(x, u, parts=1):
    hi = x.astype(BF16)
    out = jnp.dot(hi, u, preferred_element_type=F32)
    if parts == 2:
        lo = (x - hi.astype(F32)).astype(BF16)
        out = out + jnp.dot(lo, u, preferred_element_type=F32)
    return out


_NT = (((1,), (1,)), ((), ()))
_TN = (((0,), (0,)), ((), ()))


def _left_blocks(step, jd, carry, unroll, jd_multiple):
    rem = 0
    if jd_multiple % unroll:
        rem = jd % unroll
        carry = lax.fori_loop(0, rem, lambda t, c: step(jd - 1 - t, c, False), carry)

    def trip(t, c):
        for s in range(unroll):
            c = step(jd - rem - 1 - s - unroll * t, c, False)
        return c

    return lax.fori_loop(0, jd // unroll, trip, carry)


def _diag_step(step, j, carry, row0):
    if row0 == 0:
        return step(j, carry, True)
    tail = step(j, tuple(c[row0:] for c in carry), True, row0)
    return tuple(jnp.concatenate([c[:row0], t], axis=0) for c, t in zip(carry, tail))


def _qkv_heads_fwd(P, gq, gk, *, A, col0, hd, qscale, name):
    S = P.shape[0]
    H = A // hd
    ts = _tile(S, 512, 8)

    def body(q_ref, k_ref, v_ref, gq_ref, gk_ref, qn_ref, kn_ref, vh_ref):
        for h in range(H):
            cols = slice(h * hd, (h + 1) * hd)
            for x_ref, g_ref, sc, o_ref in ((q_ref, gq_ref, qscale, qn_ref), (k_ref, gk_ref, 1.0, kn_ref)):
                xh = x_ref[:, cols]
                r = lax.rsqrt(jnp.mean(xh * xh, axis=1, keepdims=True) + EPS)
                o_ref[h] = (xh * r * (g_ref[...] * sc)).astype(BF16)
            vh_ref[h] = v_ref[:, cols].astype(BF16)

    col = lambda n: pl.BlockSpec((ts, A), lambda i: (i, col0 + n))
    gsp = pl.BlockSpec((1, hd), lambda i: (0, 0))
    hsp = pl.BlockSpec((H, ts, hd), lambda i: (0, i, 0))
    shp = jax.ShapeDtypeStruct((H, S, hd), BF16)
    return pl.pallas_call(
        body, name=name, grid=(S // ts,), in_specs=[col(0), col(1), col(2), gsp, gsp],
        out_specs=[hsp, hsp, hsp], out_shape=[shp, shp, shp], compiler_params=_params("parallel"),
    )(P, P, P, gq, gk)


def _qkv_heads_bwd(P, gq, gk, dqn, dkn, dvh, *, A, col0, hd, qscale, name):
    S = P.shape[0]
    H = A // hd
    ts = _tile(S, 512, 8)

    def body(q_ref, k_ref, gq_ref, gk_ref, dq_ref, dk_ref, dv_ref, o_ref, dgq_ref, dgk_ref):
        i = pl.program_id(0)
        parts = [jnp.zeros((1, hd), F32), jnp.zeros((1, hd), F32)]
        for h in range(H):
            for n, (x_ref, g_ref, sc, d_ref) in enumerate(((q_ref, gq_ref, qscale, dq_ref), (k_ref, gk_ref, 1.0, dk_ref))):
                xh = x_ref[:, h * hd:(h + 1) * hd]
                dyv = d_ref[h] * sc
                r = lax.rsqrt(jnp.mean(xh * xh, axis=1, keepdims=True) + EPS)
                u = dyv * g_ref[...]
                dx = r * u - xh * (r * r * r * jnp.mean(u * xh, axis=1, keepdims=True))
                o_ref[:, n * A + h * hd:n * A + (h + 1) * hd] = dx.astype(BF16)
                parts[n] = parts[n] + jnp.sum(dyv * xh * r, axis=0, keepdims=True)
            o_ref[:, 2 * A + h * hd:2 * A + (h + 1) * hd] = dv_ref[h].astype(BF16)

        @pl.when(i == 0)
        def _():
            dgq_ref[...] = parts[0]
            dgk_ref[...] = parts[1]

        @pl.when(i > 0)
        def _():
            dgq_ref[...] += parts[0]
            dgk_ref[...] += parts[1]

    col = lambda n: pl.BlockSpec((ts, A), lambda i: (i, col0 + n))
    gsp = pl.BlockSpec((1, hd), lambda i: (0, 0))
    hsp = pl.BlockSpec((H, ts, hd), lambda i: (0, i, 0))
    return pl.pallas_call(
        body, name=name, grid=(S // ts,), in_specs=[col(0), col(1), gsp, gsp, hsp, hsp, hsp],
        out_specs=[pl.BlockSpec((ts, 3 * A), lambda i: (i, 0)), gsp, gsp],
        out_shape=[jax.ShapeDtypeStruct((S, 3 * A), BF16), jax.ShapeDtypeStruct((1, hd), F32),
                   jax.ShapeDtypeStruct((1, hd), F32)],
        compiler_params=_params("arbitrary"),
    )(P, P, gq, gk, dqn, dkn, dvh)


def _comm_begin(schedule, cin_refs, cout_refs, sems, grid):
    begin, end = _comm_phases(schedule, cin_refs, cout_refs, *sems)
    ids = [pl.program_id(d) for d in range(len(grid))]
    pl.when(functools.reduce(jnp.logical_and, [p == 0 for p in ids]))(begin)

    def comm_end():
        pl.when(functools.reduce(jnp.logical_and, [p == g - 1 for p, g in zip(ids, grid)]))(end)

    return comm_end


def _call_with_comm(body, name, grid, in_specs, out_specs, out_shape, args, comm):
    if comm is None:
        return pl.pallas_call(body, name=name, grid=grid, in_specs=in_specs, out_specs=out_specs, out_shape=out_shape,
                              compiler_params=_params("parallel", "arbitrary"))(*args)
    _, xs, cshapes = comm
    anyspec = pl.BlockSpec(memory_space=pl.ANY)
    return pl.pallas_call(
        body, name=name, grid=grid, in_specs=in_specs + [anyspec] * len(xs), out_specs=out_specs + [anyspec] * len(xs),
        out_shape=out_shape + [jax.ShapeDtypeStruct(c, x.dtype) for c, x in zip(cshapes, xs)],
        scratch_shapes=_comm_scratch(len(xs)), compiler_params=_params("arbitrary", "arbitrary"),
    )(*args, *xs)


def _attn_fwd(q, k, v, umat, *, tq, tk, name, comm=None):
    H, S, hd = q.shape

    def body(*refs):
        if comm is None:
            q_ref, k_ref, v_ref, u_ref, o_ref = refs
        else:
            nc = len(comm[1])
            q_ref, k_ref, v_ref, u_ref = refs[:4]
            o_ref = refs[4 + nc]
            comm_end = _comm_begin(comm[0], refs[4:4 + nc], refs[5 + nc:5 + 2 * nc], refs[5 + 2 * nc:], (H, S // tq))
        i = pl.program_id(1)
        qb = q_ref[0]
        um = u_ref[...]
        qpos = lax.broadcasted_iota(jnp.int32, (tq, tk), 0) + i * tq
        kloc = lax.broadcasted_iota(jnp.int32, (tq, tk), 1)

        def step(j, carry, masked, row0=0):
            r, acc = carry
            ks = pl.multiple_of(j * tk, tk)
            kb = k_ref[0, pl.ds(ks, tk), :]
            vb = v_ref[0, pl.ds(ks, tk), :]
            z = lax.dot_general(qb[row0:], kb, _NT, preferred_element_type=F32)
            lb = jnp.minimum(z, 0.0) - jnp.log(1.0 + jnp.exp(-jnp.abs(z)))
            lm = lb - z
            if masked:
                m = (kloc[row0:] + j * tk) < qpos[row0:]
                lm = jnp.where(m, lm, 0.0)
            a = jnp.exp(lb + _block_sums(lm, um) + r)
            if masked:
                a = jnp.where(m, a, 0.0)
            acc = acc + jnp.dot(a.astype(BF16), vb, preferred_element_type=F32)
            return r + jnp.sum(lm, axis=1, keepdims=True), acc

        jd = (i * tq) // tk
        carry = (jnp.zeros((tq, 1), F32), jnp.zeros((tq, hd), F32))
        for dd in reversed(range(max(1, tq // tk))):
            carry = _diag_step(step, jd + dd, carry, dd * tk if tq > tk else 0)
        carry = _left_blocks(step, jd, carry, ATTN_FWD_UNROLL, tq // tk)
        o_ref[0] = carry[1]
        if comm is not None:
            comm_end()

    blk = pl.BlockSpec((1, tq, hd), lambda h, i: (h, i, 0))
    whole = pl.BlockSpec((1, S, hd), lambda h, i: (h, 0, 0))
    in_specs = [blk, whole, whole, pl.BlockSpec((tk, tk), lambda h, i: (0, 0))]
    out_specs, out_shape = [blk], [jax.ShapeDtypeStruct((H, S, hd), F32)]
    res = _call_with_comm(body, name, (H, S // tq), in_specs, out_specs, out_shape, (q, k, v, umat), comm)
    return res[0] if comm is None else (res[0], res[1:])


def _attn_bwd(q, k, v, o, do, umat, *, tq, tk, name, comm=None):
    H, S, hd = q.shape

    def body(*refs):
        if comm is None:
            q_ref, k_ref, v_ref, o_ref, do_ref, u_ref, dq_ref, dk_ref, dv_ref = refs
        else:
            nc = len(comm[1])
            q_ref, k_ref, v_ref, o_ref, do_ref, u_ref = refs[:6]
            dq_ref, dk_ref, dv_ref = refs[6 + nc:9 + nc]
            comm_end = _comm_begin(comm[0], refs[6:6 + nc], refs[9 + nc:9 + 2 * nc], refs[9 + 2 * nc:], (H, S // tq))
        i = pl.program_id(1)

        @pl.when(i == 0)
        def _():
            dk_ref[...] = jnp.zeros_like(dk_ref)
            dv_ref[...] = jnp.zeros_like(dv_ref)

        qb = q_ref[0]
        do32 = do_ref[0]
        dob = do32.astype(BF16)
        tot = jnp.sum(dob.astype(F32) * o_ref[0], axis=1, keepdims=True)
        um = u_ref[...]
        qpos = lax.broadcasted_iota(jnp.int32, (tq, tk), 0) + i * tq
        kloc = lax.broadcasted_iota(jnp.int32, (tq, tk), 1)

        def step(j, carry, masked, row0=0):
            r, gs, dq = carry
            ks = pl.multiple_of(j * tk, tk)
            kb = k_ref[0, pl.ds(ks, tk), :]
            vb = v_ref[0, pl.ds(ks, tk), :]
            qs, dos = qb[row0:], dob[row0:]
            z = lax.dot_general(qs, kb, _NT, preferred_element_type=F32)
            lb = jnp.minimum(z, 0.0) - jnp.log(1.0 + jnp.exp(-jnp.abs(z)))
            sig = jnp.exp(lb)
            lm = lb - z
            if masked:
                m = (kloc[row0:] + j * tk) < qpos[row0:]
                lm = jnp.where(m, lm, 0.0)
            a = jnp.exp(lb + _block_sums(lm, um) + r)
            if masked:
                a = jnp.where(m, a, 0.0)
            ab = a.astype(BF16)
            g = lax.dot_general(dos, vb, _NT, preferred_element_type=F32) * ab.astype(F32)
            dz = g - sig * ((tot[row0:] - gs) - _block_sums(g, um, parts=2))
            if masked:
                dz = jnp.where(m, dz, 0.0)
            dzb = dz.astype(BF16)
            dq = dq + jnp.dot(dzb, kb, preferred_element_type=F32)
            dk_ref[0, pl.ds(ks, tk), :] += lax.dot_general(dzb, qs, _TN, preferred_element_type=F32)
            dv_ref[0, pl.ds(ks, tk), :] += lax.dot_general(ab, dos, _TN, preferred_element_type=F32)
            return r + jnp.sum(lm, axis=1, keepdims=True), gs + jnp.sum(g, axis=1, keepdims=True), dq

        jd = (i * tq) // tk
        zero = jnp.zeros((tq, 1), F32)
        carry = (zero, zero, jnp.zeros((tq, hd), F32))
        for dd in reversed(range(max(1, tq // tk))):
            carry = _diag_step(step, jd + dd, carry, dd * tk if tq > tk else 0)
        carry = _left_blocks(step, jd, carry, ATTN_BWD_UNROLL, tq // tk)
        dq_ref[0] = carry[2]
        if comm is not None:
            comm_end()

    blk = pl.BlockSpec((1, tq, hd), lambda h, i: (h, i, 0))
    whole = pl.BlockSpec((1, S, hd), lambda h, i: (h, 0, 0))
    shp = jax.ShapeDtypeStruct((H, S, hd), F32)
    in_specs = [blk, whole, whole, blk, blk, pl.BlockSpec((tk, tk), lambda h, i: (0, 0))]
    res = _call_with_comm(body, name, (H, S // tq), in_specs, [blk, whole, whole], [shp, shp, shp],
                          (q, k, v, o, do, umat), comm)
    return res if comm is None else (*res[:3], res[3:])


def _merge_fwd(ya, yb, o, wb, P, b_gate, *, gate_col0, name):
    S, C = ya.shape
    H, _, hd = o.shape
    D = wb.shape[2]
    ts = _tile(S, 512, 8)

    def body(y0, y1, oh_ref, wb_ref, g0, g1, g2, bg_ref, m_ref, yc_ref):
        for h in range(H):
            yc_ref[:, h * hd:(h + 1) * hd] = oh_ref[h].astype(BF16)
        acc = jnp.zeros((ts, D), F32)
        for n, (y_ref, g_ref) in enumerate(((y0, g0), (y1, g1), (yc_ref, g2))):
            yd = jnp.dot(y_ref[...], wb_ref[n], preferred_element_type=F32)
            acc = acc + _sigmoid(g_ref[...] + bg_ref[:, n * D:(n + 1) * D]) * yd
        m_ref[...] = acc.astype(BF16)

    ysp = pl.BlockSpec((ts, C), lambda i: (i, 0))
    gsp = lambda n: pl.BlockSpec((ts, D), lambda i: (i, gate_col0 + n))
    return pl.pallas_call(
        body, name=name, grid=(S // ts,),
        in_specs=[ysp, ysp, pl.BlockSpec((H, ts, hd), lambda i: (0, i, 0)), pl.BlockSpec((3, C, D), lambda i: (0, 0, 0)),
                  gsp(0), gsp(1), gsp(2), pl.BlockSpec((1, 3 * D), lambda i: (0, 0))],
        out_specs=[pl.BlockSpec((ts, D), lambda i: (i, 0)), ysp],
        out_shape=[jax.ShapeDtypeStruct((S, D), BF16), jax.ShapeDtypeStruct((S, C), BF16)],
        compiler_params=_params("parallel"),
    )(ya, yb, o, wb, P, P, P, b_gate)


def _merge_bwd(ys, wb, wbT, P, b_gate, dmerged, *, gate_col0, hd, name):
    S, C = ys[0].shape
    D = wb.shape[2]
    H = C // hd
    ts = _tile(S, 256, 8)

    def body(y0, y1, y2, wb_ref, wbT_ref, g0, g1, g2, bg_ref, dm_ref,
             dg_ref, dyd0, dyd1, dyd2, dya_ref, dyb_ref, do_ref, dbg_ref):
        i = pl.program_id(0)
        dm = dm_ref[...]
        parts = []
        for n, (y_ref, g_ref, dyd_ref) in enumerate(((y0, g0, dyd0), (y1, g1, dyd1), (y2, g2, dyd2))):
            yd = jnp.dot(y_ref[...], wb_ref[n], preferred_element_type=F32)
            sg = _sigmoid(g_ref[...] + bg_ref[:, n * D:(n + 1) * D])
            dgate = dm * yd * sg * (1.0 - sg)
            dg_ref[:, n * D:(n + 1) * D] = dgate.astype(BF16)
            parts.append(jnp.sum(dgate, axis=0, keepdims=True))
            dyd = (dm * sg).astype(BF16)
            dyd_ref[...] = dyd
            dy = jnp.dot(dyd, wbT_ref[n], preferred_element_type=F32)
            if n == 0:
                dya_ref[...] = dy
            elif n == 1:
                dyb_ref[...] = dy
            else:
                for h in range(H):
                    do_ref[h] = dy[:, h * hd:(h + 1) * hd]
        part = jnp.concatenate(parts, axis=1)

        @pl.when(i == 0)
        def _():
            dbg_ref[...] = part

        @pl.when(i > 0)
        def _():
            dbg_ref[...] += part

    ysp = pl.BlockSpec((ts, C), lambda i: (i, 0))
    dsp = pl.BlockSpec((ts, D), lambda i: (i, 0))
    gsp = lambda n: pl.BlockSpec((ts, D), lambda i: (i, gate_col0 + n))
    dshp = jax.ShapeDtypeStruct((S, D), BF16)
    yshp = jax.ShapeDtypeStruct((S, C), F32)
    return pl.pallas_call(
        body, name=name, grid=(S // ts,),
        in_specs=[ysp, ysp, ysp, pl.BlockSpec((3, C, D), lambda i: (0, 0, 0)),
                  pl.BlockSpec((3, D, C), lambda i: (0, 0, 0)), gsp(0), gsp(1), gsp(2),
                  pl.BlockSpec((1, 3 * D), lambda i: (0, 0)), dsp],
        out_specs=[pl.BlockSpec((ts, 3 * D), lambda i: (i, 0)), dsp, dsp, dsp, ysp, ysp,
                   pl.BlockSpec((H, ts, hd), lambda i: (0, i, 0)), pl.BlockSpec((1, 3 * D), lambda i: (0, 0))],
        out_shape=[jax.ShapeDtypeStruct((S, 3 * D), BF16), dshp, dshp, dshp, yshp, yshp,
                   jax.ShapeDtypeStruct((H, S, hd), F32), jax.ShapeDtypeStruct((1, 3 * D), F32)],
        compiler_params=_params("arbitrary"),
    )(*ys, wb, wbT, P, P, P, b_gate, dmerged)


def _swiglu_fwd(gu, *, name):
    S, F2 = gu.shape
    F = F2 // 2
    ts, tf = _tile(S, 512, 8), _tile(F, 1536, LANES)
    nf = F // tf

    def body(g_ref, u_ref, o_ref):
        gt = g_ref[...]
        o_ref[...] = (gt * _sigmoid(gt) * u_ref[...]).astype(BF16)

    return pl.pallas_call(
        body, name=name, grid=(S // ts, nf),
        in_specs=[pl.BlockSpec((ts, tf), lambda i, j: (i, j)), pl.BlockSpec((ts, tf), lambda i, j: (i, j + nf))],
        out_specs=pl.BlockSpec((ts, tf), lambda i, j: (i, j)),
        out_shape=jax.ShapeDtypeStruct((S, F), BF16),
        compiler_params=_params("parallel", "parallel"),
    )(gu, gu)


def _swiglu_bwd(gu, dact, *, name):
    S, F2 = gu.shape
    F = F2 // 2
    ts = _tile(S, 256, 8)

    def body(gu_ref, d_ref, o_ref):
        gt, up, da = gu_ref[:, 0:F], gu_ref[:, F:F2], d_ref[...]
        sg = _sigmoid(gt)
        o_ref[:, 0:F] = (da * up * sg * (1.0 + gt * (1.0 - sg))).astype(BF16)
        o_ref[:, F:F2] = (da * gt * sg).astype(BF16)

    return pl.pallas_call(
        body, name=name, grid=(S // ts,),
        in_specs=[pl.BlockSpec((ts, F2), lambda i: (i, 0)), pl.BlockSpec((ts, F), lambda i: (i, 0))],
        out_specs=pl.BlockSpec((ts, F2), lambda i: (i, 0)),
        out_shape=jax.ShapeDtypeStruct((S, F2), BF16),
        compiler_params=_params("parallel"),
    )(gu, dact)


def _loss_grad(y, target, *, name):
    S, D = y.shape
    ts = _tile(S, 512, 8)

    def body(y_ref, t_ref, dy_ref, l_ref, dyb_ref):
        i = pl.program_id(0)
        err = y_ref[...] - t_ref[...]
        dy_ref[...] = err * (1.0 / D)
        dyb_ref[...] = (err * (1.0 / D)).astype(BF16)
        part = jnp.broadcast_to(jnp.sum(jnp.sum(err * err, axis=1, keepdims=True), axis=0, keepdims=True) * (0.5 / D),
                                (1, LANES))

        @pl.when(i == 0)
        def _():
            l_ref[...] = part

        @pl.when(i > 0)
        def _():
            l_ref[...] += part

    row = pl.BlockSpec((ts, D), lambda i: (i, 0))
    return pl.pallas_call(
        body, name=name, grid=(S // ts,), in_specs=[row, row],
        out_specs=[row, pl.BlockSpec((1, LANES), lambda i: (0, 0)), row],
        out_shape=[jax.ShapeDtypeStruct((S, D), F32), jax.ShapeDtypeStruct((1, LANES), F32),
                   jax.ShapeDtypeStruct((S, D), BF16)],
        compiler_params=_params("arbitrary"),
    )(y, target)


def _adamw(w, m, v, gs_list, *, name):
    R, W = w.shape
    rows = [g.shape[1] for g in gs_list]
    tr = _tile(math.gcd(*rows), max(16, (2048 * LANES // W) // 16 * 16), 16)
    first = [sum(rows[:t]) // tr for t in range(len(rows))]
    c1 = 1.0 / (1.0 - ADAM_B1 ** ADAM_STEP)
    c2 = 1.0 / (1.0 - ADAM_B2 ** ADAM_STEP)

    def body(w_ref, m_ref, v_ref, *refs):
        gs_refs, (g_ref, d_ref, nm_ref, nv_ref) = refs[:len(rows)], refs[len(rows):]
        i = pl.program_id(0)
        g = None
        for t, gs_ref in enumerate(gs_refs):
            gt = gs_ref[0].astype(F32)
            for s in range(1, gs_ref.shape[0]):
                gt = gt + gs_ref[s].astype(F32)
            g = gt if g is None else jnp.where(i >= first[t], gt, g)
        nm = ADAM_B1 * m_ref[...] + (1.0 - ADAM_B1) * g
        nv = ADAM_B2 * v_ref[...] + (1.0 - ADAM_B2) * (g * g)
        g_ref[...] = g
        nm_ref[...] = nm
        nv_ref[...] = nv
        d_ref[...] = -ADAM_LR * ((nm * c1) / (jnp.sqrt(nv * c2) + ADAM_EPS) + ADAM_WD * w_ref[...])

    row = pl.BlockSpec((tr, W), lambda i: (i, 0))
    slots = [pl.BlockSpec((g.shape[0], tr, W),
                          lambda i, b0=first[t], nb=rows[t] // tr: (0, jnp.clip(i - b0, 0, nb - 1), 0))
             for t, g in enumerate(gs_list)]
    shp = jax.ShapeDtypeStruct((R, W), F32)
    return pl.pallas_call(
        body, name=name, grid=(R // tr,), in_specs=[row, row, row] + slots,
        out_specs=[row, row, row, row], out_shape=[shp, shp, shp, shp],
        compiler_params=_params("parallel"),
    )(w, m, v, *gs_list)


def _slot_sum(gs, *, name):
    ns, R, _ = gs.shape
    tr = _tile(R, 2048, 16)

    def body(gs_ref, o_ref):
        g = gs_ref[0]
        for s in range(1, ns):
            g = g + gs_ref[s]
        o_ref[...] = g

    return pl.pallas_call(
        body, name=name, grid=(R // tr,),
        in_specs=[pl.BlockSpec((ns, tr, LANES), lambda i: (0, i, 0))],
        out_specs=pl.BlockSpec((tr, LANES), lambda i: (i, 0)),
        out_shape=jax.ShapeDtypeStruct((R, LANES), F32),
        compiler_params=_params("parallel"),
    )(gs)


def _comm_scratch(n):
    return [pltpu.SemaphoreType.DMA((7 * n,)), pltpu.SemaphoreType.DMA((7 * n,)), pltpu.SemaphoreType.DMA((n,))]


def _gather_schedule(x_ref, out_ref, send_sems, recv_sems, local_sem, base):
    x, y, c = lax.axis_index("x"), lax.axis_index("y"), lax.axis_index("c")
    me, sibling = (x, y, c), (x, y, 1 - c)
    chips = [(1 - x, y), (x, 1 - y), (1 - x, 1 - y)]

    def slot(px, py, pc):
        return out_ref.at[4 * px + 2 * py + pc]

    def copy(k, block, to, src=None):
        return pltpu.make_async_remote_copy(
            src_ref=slot(*block) if src is None else src, dst_ref=slot(*block),
            send_sem=send_sems.at[base + k], recv_sem=recv_sems.at[base + k], device_id=to, device_id_type=MESH)

    mine = pltpu.make_async_copy(x_ref, slot(*me), local_sem)
    first = [copy(0, me, sibling, src=x_ref)]
    first += [copy(1 + j, me, (*chip, c), src=x_ref) for j, chip in enumerate(chips)]
    passed = [copy(4 + j, (*chip, c), sibling) for j, chip in enumerate(chips)]

    def begin():
        mine.start()
        for cp in first:
            cp.start()

    def forward():
        for j, chip in enumerate(chips):
            copy(1 + j, (*chip, c), me).wait_recv()
            passed[j].start()

    def finish():
        copy(0, sibling, me).wait_recv()
        for j, chip in enumerate(chips):
            copy(4 + j, (*chip, 1 - c), me).wait_recv()
        for cp in first + passed:
            cp.wait_send()
        mine.wait()

    return begin, forward, finish


def _exchange_schedule(x_ref, out_ref, send_sems, recv_sems, local_sem, base):
    x, y, c = lax.axis_index("x"), lax.axis_index("y"), lax.axis_index("c")
    me = 4 * x + 2 * y + c
    mine = pltpu.make_async_copy(x_ref.at[me], out_ref.at[me], local_sem)
    sends, recvs = [], []
    for k in range(1, N_DEV):
        px = 1 - x if k & 4 else x
        py = 1 - y if k & 2 else y
        pc = 1 - c if k & 1 else c
        peer = 4 * px + 2 * py + pc
        sems = dict(send_sem=send_sems.at[base + k - 1], recv_sem=recv_sems.at[base + k - 1],
                    device_id=(px, py, pc), device_id_type=MESH)
        sends.append(pltpu.make_async_remote_copy(src_ref=x_ref.at[peer], dst_ref=out_ref.at[me], **sems))
        recvs.append(pltpu.make_async_remote_copy(src_ref=x_ref.at[me], dst_ref=out_ref.at[peer], **sems))

    def begin():
        mine.start()
        for cp in sends:
            cp.start()

    def forward():
        pass

    def finish():
        for cp in recvs:
            cp.wait_recv()
        for cp in sends:
            cp.wait_send()
        mine.wait()

    return begin, forward, finish


def _comm_phases(schedule, x_refs, out_refs, send_sems, recv_sems, local_sems):
    parts = [schedule(x, o, send_sems, recv_sems, local_sems.at[t], 7 * t)
             for t, (x, o) in enumerate(zip(x_refs, out_refs))]

    def begin():
        for b, _, _ in parts:
            b()

    def end():
        for _, fw, _ in parts:
            fw()
        for _, _, fin in parts:
            fin()

    return begin, end


def _all_gather(xs, *, name):
    def body(x_ref, out_ref, send_sems, recv_sems, local_sems):
        begin, end = _comm_phases(_gather_schedule, [x_ref], [out_ref], send_sems, recv_sems, local_sems)
        begin()
        end()

    return pl.pallas_call(
        body, name=name,
        in_specs=[pl.BlockSpec(memory_space=pl.ANY)], out_specs=pl.BlockSpec(memory_space=pl.ANY),
        out_shape=jax.ShapeDtypeStruct((N_DEV,) + xs.shape, xs.dtype), scratch_shapes=_comm_scratch(1),
        compiler_params=pltpu.CompilerParams(has_side_effects=True),
    )(xs)


PACK_ROWS = 16


def _pack_rows(parts, dtype, lead=()):
    rows = []
    for p in parts:
        r = p.reshape(lead + (-1, LANES)).astype(dtype)
        pad = (-r.shape[-2]) % PACK_ROWS
        rows.append(jnp.pad(r, [(0, 0)] * len(lead) + [(0, pad), (0, 0)]) if pad else r)
    return jnp.concatenate(rows, axis=len(lead))


def _unpack_rows(packed, shapes, lead=()):
    out, off = [], 0
    for shp in shapes:
        n = math.prod(shp) // LANES
        out.append(lax.slice_in_dim(packed, off, off + n, axis=len(lead)).reshape(lead + tuple(shp)))
        off += n + (-n) % PACK_ROWS
    return out


def _unshard(gathered, axis):
    g = jnp.moveaxis(gathered, 0, axis)
    shp = list(g.shape)
    shp[axis:axis + 2] = [shp[axis] * shp[axis + 1]]
    return g.reshape(shp)


def _reshard(full, axis):
    shp = list(full.shape)
    shp[axis:axis + 1] = [N_DEV, shp[axis] // N_DEV]
    return jnp.moveaxis(full.reshape(shp), axis, 0)


def _late_parts(L):
    return [(k, l) for k in BIG for l in range(L) if (k, l) != ("w_in", 0)]


def _late_groups(L, shard_shape):
    groups = {}
    for k, l in _late_parts(L):
        groups.setdefault(shard_shape[k][-1], []).append((k, l))
    return groups


def _local_step(x, target, w_in0, late_local, groups, shard_shape, sm):
    S, D = x.shape
    L = sm["mix_norm_g"].shape[0]
    wf = {k: [None] * L for k in BIG}
    wf["w_in"][0] = w_in0
    C = sm["conv_b"].shape[1]
    W = sm["sgu_ln_g"].shape[1]
    hd = sm["q_norm_g"].shape[1]
    G, CH = sm["sgu_w"].shape[1], sm["sgu_w"].shape[2]
    A = w_in0.shape[1] - (3 * C + 2 * W + 3 * D)
    A = A // 3
    H = A // hd
    col_q = 3 * C + 2 * W
    qscale = 1.0 / math.sqrt(hd)
    tril = jnp.tril(jnp.ones((CH, CH), F32))
    (ftq, ftk), (btq, btk) = [(_tile(S, a, LANES), _tile(S, b, LANES)) for a, b in (ATTN_FWD_TILES, ATTN_BWD_TILES)]
    umat = lambda t: (lax.broadcasted_iota(jnp.int32, (t, t), 0) > lax.broadcasted_iota(jnp.int32, (t, t), 1)).astype(BF16)

    saved = []
    for l in range(L):
        n = f"l{l}_"
        g1 = sm["mix_norm_g"][l][None]
        h = _rmsnorm_fwd(x, g1, scale=1.0, out_dtype=BF16, name=n + "mixnorm")
        P = _matmul(h, wf["w_in"][l], name=n + "w_in")
        cw = jnp.pad(sm["conv_w"][l], ((0, 5), (0, 0)))
        cb = sm["conv_b"][l][None]
        ya = _conv_fwd(P, cw, cb, C=C, name=n + "conv")
        wm = (sm["sgu_w"][l] * tril).astype(BF16)
        bT = sm["sgu_b"][l].T
        lng, lnb = sm["sgu_ln_g"][l][None], sm["sgu_ln_b"][l][None]
        yb = _sgu_fwd(P, lng, lnb, wm, bT, W=W, cu=(3 * C) // W, cv=(3 * C) // W + 1, name=n + "sgu")
        gq, gk = sm["q_norm_g"][l][None], sm["k_norm_g"][l][None]
        qn, kn, vh = _qkv_heads_fwd(P, gq, gk, A=A, col0=col_q // A, hd=hd, qscale=qscale, name=n + "qkv")
        if l == 0:
            o, gathered = _attn_fwd(qn, kn, vh, umat(ftk), tq=ftq, tk=ftk, name=n + "attn",
                                    comm=(_gather_schedule, late_local, [(N_DEV,) + a.shape for a in late_local]))
            for (width, parts), g in zip(groups.items(), gathered):
                r = 0
                for k, ll in parts:
                    rows = math.prod(shard_shape[k]) // width
                    part = lax.slice_in_dim(g, r, r + rows, axis=1).reshape((N_DEV,) + tuple(shard_shape[k]))
                    wf[k][ll] = _unshard(part, SHARD_AXIS[k] - 1)
                    r += rows
        else:
            o = _attn_fwd(qn, kn, vh, umat(ftk), tq=ftq, tk=ftk, name=n + "attn")
        bg = sm["b_gate"][l][None]
        gate_col0 = (col_q + 3 * A) // D
        merged, yc = _merge_fwd(ya, yb, o, wf["w_branch_out"][l], P, bg, gate_col0=gate_col0, name=n + "merge")
        x1 = _matmul(merged, wf["w_o"][l], res=x, name=n + "w_o")
        g2 = sm["ffn_norm_g"][l][None]
        h2 = _rmsnorm_fwd(x1, g2, scale=1.0, out_dtype=BF16, name=n + "ffnnorm")
        gu = _matmul(h2, wf["w_gate_up"][l], name=n + "w_gate_up")
        act = _swiglu_fwd(gu, name=n + "swiglu")
        x2 = _matmul(act, wf["w_down"][l], res=x1, name=n + "w_down")
        saved.append(dict(x=x, h=h, P=P, cw=cw, cb=cb, ya=ya, wm=wm, bT=bT, lng=lng, lnb=lnb, yb=yb,
                          gq=gq, gk=gk, qn=qn, kn=kn, vh=vh, o=o, yc=yc, bg=bg,
                          gate_col0=gate_col0, merged=merged, x1=x1, g1=g1, g2=g2, h2=h2, gu=gu, act=act))
        x = x2

    dx, lpart, dxb = _loss_grad(x, target, name="loss")
    grads = {k: [None] * L for k in WEIGHTS}
    chunked = lambda k, g: g if k == "w_in" else _reshard(g, SHARD_AXIS[k] - 1)
    for l in reversed(range(L)):
        n = f"l{l}_b_"
        sv = saved[l]
        grads["w_down"][l] = _matmul_tn(sv["act"], dxb, name=n + "g_w_down", out_dtype=BF16)
        dact = _matmul(dxb, wf["w_down"][l].T, name=n + "d_act")
        dgu = _swiglu_bwd(sv["gu"], dact, name=n + "swiglu")
        grads["w_gate_up"][l] = _matmul_tn(sv["h2"], dgu, name=n + "g_w_gate_up", out_dtype=BF16)
        dh2 = _matmul(dgu, wf["w_gate_up"][l].T, name=n + "d_h2")
        dx1, dg2, dx1b = _rmsnorm_bwd(sv["x1"], sv["g2"], dh2, scale=1.0, dres=dx, name=n + "ffnnorm",
                                      bf16_copy=True)
        grads["ffn_norm_g"][l] = dg2[0]
        grads["w_o"][l] = _matmul_tn(sv["merged"], dx1b, name=n + "g_w_o", out_dtype=BF16)
        dmerged = _matmul(dx1b, wf["w_o"][l].T, name=n + "d_merged")
        ys = (sv["ya"], sv["yb"], sv["yc"])
        wb = wf["w_branch_out"][l]
        dgates, *dyd, dya, dyb, do, dbg = _merge_bwd(ys, wb, wb.transpose(0, 2, 1), sv["P"], sv["bg"], dmerged,
                                                     gate_col0=sv["gate_col0"], hd=hd, name=n + "merge")
        grads["b_gate"][l] = dbg[0]
        grads["w_branch_out"][l] = jnp.stack(
            [_matmul_tn(ys[i], dyd[i], name=n + f"g_w_branch{i}", out_dtype=BF16) for i in range(3)])
        dconv, dcw = _conv_bwd(sv["P"], dya, sv["cw"], sv["cb"], C=C, name=n + "conv")
        grads["conv_w"][l], grads["conv_b"][l] = dcw[0:3], dcw[3]
        wmT = sv["wm"].transpose(0, 2, 1)
        dsgu, dsw, dsb, dln = _sgu_bwd(sv["P"], dyb, sv["lng"], sv["lnb"], sv["wm"], wmT, sv["bT"], W=W,
                                       cu=(3 * C) // W, cv=(3 * C) // W + 1, name=n + "sgu")
        grads["sgu_w"][l], grads["sgu_b"][l] = dsw * tril, dsb[:, :, 0]
        grads["sgu_ln_g"][l], grads["sgu_ln_b"][l] = dln[0], dln[1]
        if l == 0:
            chunks = [jnp.concatenate([chunked(k, grads[k][ll]).reshape(N_DEV, -1, width) for k, ll in parts], axis=1)
                      for width, parts in groups.items()]
            dqn, dkn, dvh, late_recv = _attn_bwd(sv["qn"], sv["kn"], sv["vh"], sv["o"], do, umat(btk), tq=btq, tk=btk,
                                                 name=n + "attn",
                                                 comm=(_exchange_schedule, chunks, [c.shape for c in chunks]))
        else:
            dqn, dkn, dvh = _attn_bwd(sv["qn"], sv["kn"], sv["vh"], sv["o"], do, umat(btk), tq=btq, tk=btk,
                                      name=n + "attn")
        dqkv, dgq, dgk = _qkv_heads_bwd(sv["P"], sv["gq"], sv["gk"], dqn, dkn, dvh, A=A, col0=col_q // A, hd=hd,
                                        qscale=qscale, name=n + "qkv")
        grads["q_norm_g"][l], grads["k_norm_g"][l] = dgq[0], dgk[0]
        dP = jnp.concatenate([dconv, dsgu, dqkv, dgates], axis=1)
        grads["w_in"][l] = _matmul_tn(sv["h"], dP, name=n + "g_w_in", out_dtype=BF16, shard=dP.shape[1] // N_DEV)
        if l == 0:
            dh, early_recv = _matmul(dP, wf["w_in"][l].T, name=n + "d_h",
                                     comm=(_exchange_schedule, [grads["w_in"][0]], [grads["w_in"][0].shape]))
        else:
            dh = _matmul(dP, wf["w_in"][l].T, name=n + "d_h")
        dx, dg1, dxb = _rmsnorm_bwd(sv["x"], sv["g1"], dh, scale=1.0, dres=dx1, name=n + "mixnorm", bf16_copy=True)
        grads["mix_norm_g"][l] = dg1[0]
    return lpart[0, 0], dx, grads, early_recv, late_recv


def kernel(x, mix_norm_g, w_in, b_gate, conv_w, conv_b, sgu_ln_g, sgu_ln_b, sgu_w, sgu_b, q_norm_g, k_norm_g, w_branch_out, w_o, ffn_norm_g, w_gate_up, w_down, loss_target, m_mix_norm_g, m_w_in, m_b_gate, m_conv_w, m_conv_b, m_sgu_ln_g, m_sgu_ln_b, m_sgu_w, m_sgu_b, m_q_norm_g, m_k_norm_g, m_w_branch_out, m_w_o, m_ffn_norm_g, m_w_gate_up, m_w_down, v_mix_norm_g, v_w_in, v_b_gate, v_conv_w, v_conv_b, v_sgu_ln_g, v_sgu_ln_b, v_sgu_w, v_sgu_b, v_q_norm_g, v_k_norm_g, v_w_branch_out, v_w_o, v_ffn_norm_g, v_w_gate_up, v_w_down):
    w = dict(mix_norm_g=mix_norm_g, w_in=w_in, b_gate=b_gate, conv_w=conv_w, conv_b=conv_b, sgu_ln_g=sgu_ln_g,
             sgu_ln_b=sgu_ln_b, sgu_w=sgu_w, sgu_b=sgu_b, q_norm_g=q_norm_g, k_norm_g=k_norm_g,
             w_branch_out=w_branch_out, w_o=w_o, ffn_norm_g=ffn_norm_g, w_gate_up=w_gate_up, w_down=w_down)
    m = dict(mix_norm_g=m_mix_norm_g, w_in=m_w_in, b_gate=m_b_gate, conv_w=m_conv_w, conv_b=m_conv_b,
             sgu_ln_g=m_sgu_ln_g, sgu_ln_b=m_sgu_ln_b, sgu_w=m_sgu_w, sgu_b=m_sgu_b, q_norm_g=m_q_norm_g,
             k_norm_g=m_k_norm_g, w_branch_out=m_w_branch_out, w_o=m_w_o, ffn_norm_g=m_ffn_norm_g,
             w_gate_up=m_w_gate_up, w_down=m_w_down)
    v = dict(mix_norm_g=v_mix_norm_g, w_in=v_w_in, b_gate=v_b_gate, conv_w=v_conv_w, conv_b=v_conv_b,
             sgu_ln_g=v_sgu_ln_g, sgu_ln_b=v_sgu_ln_b, sgu_w=v_sgu_w, sgu_b=v_sgu_b, q_norm_g=v_q_norm_g,
             k_norm_g=v_k_norm_g, w_branch_out=v_w_branch_out, w_o=v_w_o, ffn_norm_g=v_ffn_norm_g,
             w_gate_up=v_w_gate_up, w_down=v_w_down)
    me = 4 * lax.axis_index("x") + 2 * lax.axis_index("y") + lax.axis_index("c")
    S = x.shape[1]

    L = w_in.shape[0]
    two_d = lambda a: a.reshape(-1, a.shape[-1])
    shard_shape = {k: w[k].shape[1:] for k in BIG}
    groups = _late_groups(L, shard_shape)
    w_in0 = _unshard(_all_gather(w_in[0].astype(BF16), name="gather_w_in0"), SHARD_AXIS["w_in"] - 1)
    late_local = [jnp.concatenate([two_d(w[k][l]).astype(BF16) for k, l in parts], axis=0) for parts in groups.values()]
    conv_g = _all_gather(_pack_rows([conv_w], F32), name="gather_conv_w")
    sm = {k: w[k] for k in SMALL}
    sm["conv_w"] = _unshard(_unpack_rows(conv_g, [conv_w.shape], lead=(N_DEV,))[0], 2)

    lpart, dx, grads, early_recv, late_recv = _local_step(x[0], loss_target[0], w_in0, late_local, groups,
                                                          shard_shape, sm)
    loss = lax.psum(lpart, ("x", "y", "c"))

    where = {("w_in", 0): (early_recv[0], 0)}
    for (width, parts), recv in zip(groups.items(), late_recv):
        r = 0
        for k, l in parts:
            where[k, l] = (recv, r)
            r += math.prod(shard_shape[k]) // width
    out = {}
    for k in BIG:
        rows = math.prod(shard_shape[k]) // shard_shape[k][-1]
        gs_list = []
        for l in range(L):
            recv, r = where[k, l]
            if gs_list and gs_list[-1][0] is recv and gs_list[-1][2] == r:
                gs_list[-1][2] = r + rows
            else:
                gs_list.append([recv, r, r + rows])
        gs_list = [recv if (r0, r1) == (0, recv.shape[1]) else lax.slice_in_dim(recv, r0, r1, axis=1)
                   for recv, r0, r1 in gs_list]
        res = _adamw(two_d(w[k]), two_d(m[k]), two_d(v[k]), gs_list, name="adamw_" + k)
        for nm, r in zip(("grad", "delta", "new_m", "new_v"), res):
            out[nm, k] = r.reshape(w[k].shape)

    small_grads = [jnp.stack(grads[k]) for k in SMALL]
    small_shapes = [g.shape for g in small_grads]
    sg = _all_gather(_pack_rows(small_grads, F32), name="gather_small_grads")
    gsum = _slot_sum(sg, name="sum_small_grads")
    gsmall = dict(zip(SMALL, _unpack_rows(gsum, small_shapes)))
    cshard = conv_w.shape[2]
    gsmall["conv_w"] = lax.dynamic_slice_in_dim(gsmall["conv_w"], me * cshard, cshard, axis=2)
    own_shapes = [w[k].shape for k in SMALL]
    gs_, ds_, ms_, vs_ = _adamw(_pack_rows([w[k] for k in SMALL], F32), _pack_rows([m[k] for k in SMALL], F32),
                                _pack_rows([v[k] for k in SMALL], F32),
                                [_pack_rows([gsmall[k] for k in SMALL], F32)[None]], name="adamw_small")
    for nm, packed in (("grad", gs_), ("delta", ds_), ("new_m", ms_), ("new_v", vs_)):
        for k, a in zip(SMALL, _unpack_rows(packed, own_shapes)):
            out[nm, k] = a

    res = [loss, dx[None]]
    for nm in ("grad", "delta", "new_m", "new_v"):
        res += [out[nm, k] for k in WEIGHTS]
    return tuple(res)
```

```python
import functools
import math

import jax
import jax.numpy as jnp
from jax import lax
from jax.experimental import pallas as pl
from jax.experimental.pallas import tpu as pltpu

F32 = jnp.float32
BF16 = jnp.bfloat16
MESH = pl.DeviceIdType.MESH

N_DEV = 8
LANES = 128
VMEM_LIMIT_BYTES = 56 * 1024 * 1024
EPS = 1e-6
ADAM_LR, ADAM_B1, ADAM_B2, ADAM_EPS, ADAM_WD, ADAM_STEP = 0.001, 0.9, 0.999, 1e-08, 0.01, 10
ATTN_FWD_TILES = (512, 512)
ATTN_BWD_TILES = (1024, 256)
ATTN_FWD_UNROLL = 4
ATTN_BWD_UNROLL = 4
BIG = ("w_in", "w_branch_out", "w_o", "w_gate_up", "w_down")
SMALL = ("mix_norm_g", "b_gate", "conv_w", "conv_b", "sgu_ln_g", "sgu_ln_b", "sgu_w", "sgu_b",
         "q_norm_g", "k_norm_g", "ffn_norm_g")
WEIGHTS = ("mix_norm_g", "w_in", "b_gate", "conv_w", "conv_b", "sgu_ln_g", "sgu_ln_b", "sgu_w", "sgu_b",
           "q_norm_g", "k_norm_g", "w_branch_out", "w_o", "ffn_norm_g", "w_gate_up", "w_down")
SHARD_AXIS = {"w_in": 2, "w_branch_out": 3, "w_o": 1, "w_gate_up": 2, "w_down": 1}


def _tile(n, cap, mult):
    best = None
    for t in range(mult, min(n, cap) + 1, mult):
        if n % t == 0:
            best = t
    return best if best is not None else n


def _params(*sem):
    return pltpu.CompilerParams(dimension_semantics=sem if sem else None, vmem_limit_bytes=VMEM_LIMIT_BYTES)


def _erf(x):
    return lax.erf(x)


def _gelu(x):
    return 0.5 * x * (1.0 + _erf(x * (1.0 / math.sqrt(2.0))))


def _gelu_grad(x):
    return 0.5 * (1.0 + _erf(x * (1.0 / math.sqrt(2.0)))) + x * jnp.exp(-0.5 * x * x) * (1.0 / math.sqrt(2.0 * math.pi))


def _sigmoid(x):
    return 1.0 / (1.0 + jnp.exp(-x))


def _matmul(a, b, *, name, res=None, out_dtype=F32, comm=None):
    M, K = a.shape
    _, N = b.shape
    tm, tn, tk = _tile(M, 1024, 8), _tile(N, 1536, LANES), _tile(K, 1536, LANES)
    nk = K // tk
    grid = (M // tm, N // tn, nk)
    has_res = res is not None

    def body(*refs):
        refs = list(refs)
        a_ref, b_ref = refs[:2]
        r_ref = refs[2] if has_res else None
        pos = 2 + has_res
        nc = len(comm[1]) if comm is not None else 0
        o_ref = refs[pos + nc]
        acc = refs[pos + 1 + 2 * nc] if nk > 1 else None
        if comm is not None:
            comm_end = _comm_begin(comm[0], refs[pos:pos + nc], refs[pos + nc + 1:pos + 2 * nc + 1],
                                   refs[pos + 2 * nc + 1 + (nk > 1):], grid)
        k = pl.program_id(2)
        part = jnp.dot(a_ref[...], b_ref[...], preferred_element_type=F32)

        def finish(v):
            if has_res:
                v = v + r_ref[...]
            o_ref[...] = v.astype(out_dtype)

        if nk == 1:
            finish(part)
        else:
            @pl.when(k == 0)
            def _():
                acc[...] = part

            @pl.when(jnp.logical_and(k > 0, k < nk - 1))
            def _():
                acc[...] += part

            @pl.when(k == nk - 1)
            def _():
                finish(acc[...] + part)

        if comm is not None:
            comm_end()

    in_specs = [pl.BlockSpec((tm, tk), lambda i, j, k: (i, k)), pl.BlockSpec((tk, tn), lambda i, j, k: (k, j))]
    args = [a, b]
    if has_res:
        in_specs.append(pl.BlockSpec((tm, tn), lambda i, j, k: (i, j)))
        args.append(res)
    out_specs = [pl.BlockSpec((tm, tn), lambda i, j, k: (i, j))]
    out_shape = [jax.ShapeDtypeStruct((M, N), out_dtype)]
    scratch = [pltpu.VMEM((tm, tn), F32)] if nk > 1 else []
    sem = ("parallel", "parallel", "arbitrary")
    if comm is not None:
        anyspec = pl.BlockSpec(memory_space=pl.ANY)
        in_specs += [anyspec] * len(comm[1])
        args += list(comm[1])
        out_specs += [anyspec] * len(comm[1])
        out_shape += [jax.ShapeDtypeStruct(c, x.dtype) for c, x in zip(comm[2], comm[1])]
        scratch += _comm_scratch(len(comm[1]))
        sem = ("arbitrary",) * 3
    out = pl.pallas_call(body, name=name, grid=grid, in_specs=in_specs, out_specs=out_specs, out_shape=out_shape,
                         scratch_shapes=scratch, compiler_params=_params(*sem))(*args)
    return out[0] if comm is None else (out[0], out[1:])


def _matmul_tn(x, y, *, name, out_dtype=F32, shard=None):
    S, A = x.shape
    _, B = y.shape
    ta, ts = _tile(A, 1536, LANES), _tile(S, 1024, 8)
    tb = shard if shard else _tile(B, 1536, LANES)
    ns = S // ts
    direct = out_dtype == F32
    view = (lambda r: r.at[0]) if shard else (lambda r: r)

    def body(x_ref, y_ref, o_ref, *scratch):
        s = pl.program_id(2)
        out = view(o_ref)
        acc = out if direct else scratch[0]
        part = lax.dot_general(x_ref[...], y_ref[...], (((0,), (0,)), ((), ())), preferred_element_type=F32)

        @pl.when(s == 0)
        def _():
            acc[...] = part

        @pl.when(s > 0)
        def _():
            acc[...] += part

        if not direct:
            @pl.when(s == ns - 1)
            def _():
                out[...] = acc[...].astype(out_dtype)

    if shard:
        out_spec, shape = pl.BlockSpec((1, ta, tb), lambda i, j, s: (j, i, 0)), (B // tb, A, tb)
    else:
        out_spec, shape = pl.BlockSpec((ta, tb), lambda i, j, s: (i, j)), (A, B)
    return pl.pallas_call(
        body, name=name, grid=(A // ta, B // tb, ns),
        in_specs=[pl.BlockSpec((ts, ta), lambda i, j, s: (s, i)), pl.BlockSpec((ts, tb), lambda i, j, s: (s, j))],
        out_specs=out_spec, out_shape=jax.ShapeDtypeStruct(shape, out_dtype),
        scratch_shapes=[] if direct else [pltpu.VMEM((ta, tb), F32)],
        compiler_params=_params("parallel", "parallel", "arbitrary"),
    )(x, y)


def _rmsnorm_fwd(x, g, *, scale, out_dtype, name):
    R, W = x.shape
    tr = _tile(R, 512 if W >= 512 else 4096, 8)

    def body(x_ref, g_ref, o_ref):
        xv = x_ref[...]
        r = lax.rsqrt(jnp.mean(xv * xv, axis=1, keepdims=True) + EPS)
        o_ref[...] = (xv * r * (g_ref[...] * scale)).astype(out_dtype)

    return pl.pallas_call(
        body, name=name, grid=(R // tr,),
        in_specs=[pl.BlockSpec((tr, W), lambda i: (i, 0)), pl.BlockSpec((1, W), lambda i: (0, 0))],
        out_specs=pl.BlockSpec((tr, W), lambda i: (i, 0)),
        out_shape=jax.ShapeDtypeStruct((R, W), out_dtype),
        compiler_params=_params("parallel"),
    )(x, g)


def _rmsnorm_bwd(x, g, dy, *, scale, name, dres=None, out_dtype=F32, bf16_copy=False):
    R, W = x.shape
    tr = _tile(R, 512 if W >= 512 else 4096, 8)
    has_res = dres is not None

    def body(*refs):
        refs = list(refs)
        dxb_ref = refs.pop() if bf16_copy else None
        if has_res:
            x_ref, g_ref, dy_ref, dres_ref, dx_ref, dg_ref = refs
        else:
            x_ref, g_ref, dy_ref, dx_ref, dg_ref = refs
        i = pl.program_id(0)
        xv = x_ref[...]
        dyv = dy_ref[...].astype(F32) * scale
        r = lax.rsqrt(jnp.mean(xv * xv, axis=1, keepdims=True) + EPS)
        u = dyv * g_ref[...]
        dx = r * u - xv * (r * r * r * jnp.mean(u * xv, axis=1, keepdims=True))
        if has_res:
            dx = dx + dres_ref[...]
        dx_ref[...] = dx.astype(out_dtype)
        if bf16_copy:
            dxb_ref[...] = dx.astype(BF16)
        part = jnp.sum(dyv * xv * r, axis=0, keepdims=True)

        @pl.when(i == 0)
        def _():
            dg_ref[...] = part

        @pl.when(i > 0)
        def _():
            dg_ref[...] += part

    row = pl.BlockSpec((tr, W), lambda i: (i, 0))
    one = pl.BlockSpec((1, W), lambda i: (0, 0))
    in_specs = [row, one, row] + ([row] if has_res else [])
    args = [x, g, dy] + ([dres] if has_res else [])
    extra = bool(bf16_copy)
    return pl.pallas_call(
        body, name=name, grid=(R // tr,), in_specs=in_specs, out_specs=[row, one] + [row] * extra,
        out_shape=[jax.ShapeDtypeStruct((R, W), out_dtype), jax.ShapeDtypeStruct((1, W), F32)]
        + [jax.ShapeDtypeStruct((R, W), BF16)] * extra,
        compiler_params=_params("arbitrary"),
    )(*args)


def _shift_down(u, prev, n):
    ts = u.shape[0]
    out = pltpu.roll(u, n, 0)
    row = lax.broadcasted_iota(jnp.int32, u.shape, 0)
    for r in range(n):
        out = jnp.where(row == r, prev[8 - n + r:8 - n + r + 1, :], out)
    return out


def _shift_up(u, nxt, n):
    ts = u.shape[0]
    out = pltpu.roll(u, ts - n, 0)
    row = lax.broadcasted_iota(jnp.int32, u.shape, 0)
    for r in range(n):
        out = jnp.where(row == ts - n + r, nxt[r:r + 1, :], out)
    return out


def _conv_fwd(P, conv_w, conv_b, *, C, name):
    S = P.shape[0]
    ts = _tile(S, 512, 8)
    hb = ts // 8

    def body(ab_ref, ac_ref, ax_ref, pc_ref, px_ref, w_ref, b_ref, o_ref):
        i = pl.program_id(0)
        u = ac_ref[...] * ax_ref[...]
        prev = pc_ref[...] * px_ref[...] * (i > 0).astype(F32)
        w = w_ref[...]
        y = b_ref[...] + w[0:1, :] * _shift_down(u, prev, 2) + w[1:2, :] * _shift_down(u, prev, 1) + w[2:3, :] * u
        o_ref[...] = (ab_ref[...] * y).astype(BF16)

    cur = lambda c: pl.BlockSpec((ts, C), lambda i: (i, c))
    prv = lambda c: pl.BlockSpec((8, C), lambda i: (jnp.maximum(i * hb - 1, 0), c))
    return pl.pallas_call(
        body, name=name, grid=(S // ts,),
        in_specs=[cur(0), cur(1), cur(2), prv(1), prv(2),
                  pl.BlockSpec((8, C), lambda i: (0, 0)), pl.BlockSpec((1, C), lambda i: (0, 0))],
        out_specs=pl.BlockSpec((ts, C), lambda i: (i, 0)),
        out_shape=jax.ShapeDtypeStruct((S, C), BF16),
        compiler_params=_params("parallel"),
    )(P, P, P, P, P, conv_w, conv_b)


def _conv_bwd(P, dya, conv_w, conv_b, *, C, name):
    S = P.shape[0]
    ts = _tile(S, 512, 8)
    hb = ts // 8
    last = S // 8 - 1
    n = S // ts

    def body(ab_ref, ac_ref, ax_ref, pc_ref, px_ref, dy_ref, nab_ref, ndy_ref, w_ref, b_ref, o_ref, dw_ref):
        i = pl.program_id(0)
        ab, ac, ax = ab_ref[...], ac_ref[...], ax_ref[...]
        u = ac * ax
        prev = pc_ref[...] * px_ref[...] * (i > 0).astype(F32)
        w = w_ref[...]
        u1, u2 = _shift_down(u, prev, 1), _shift_down(u, prev, 2)
        y = b_ref[...] + w[0:1, :] * u2 + w[1:2, :] * u1 + w[2:3, :] * u
        dya_v = dy_ref[...]
        dyp = dya_v * ab
        nxt = ndy_ref[...] * nab_ref[...] * (i < n - 1).astype(F32)
        du = w[2:3, :] * dyp + w[1:2, :] * _shift_up(dyp, nxt, 1) + w[0:1, :] * _shift_up(dyp, nxt, 2)
        o_ref[:, 0:C] = (dya_v * y).astype(BF16)
        o_ref[:, C:2 * C] = (du * ax).astype(BF16)
        o_ref[:, 2 * C:3 * C] = (du * ac).astype(BF16)
        part = jnp.concatenate([
            jnp.sum(dyp * u2, axis=0, keepdims=True), jnp.sum(dyp * u1, axis=0, keepdims=True),
            jnp.sum(dyp * u, axis=0, keepdims=True), jnp.sum(dyp, axis=0, keepdims=True),
            jnp.zeros((4, C), F32)], axis=0)

        @pl.when(i == 0)
        def _():
            dw_ref[...] = part

        @pl.when(i > 0)
        def _():
            dw_ref[...] += part

    cur = lambda c: pl.BlockSpec((ts, C), lambda i: (i, c))
    prv = lambda c: pl.BlockSpec((8, C), lambda i: (jnp.maximum(i * hb - 1, 0), c))
    nxt = lambda c: pl.BlockSpec((8, C), lambda i: (jnp.minimum((i + 1) * hb, last), c))
    return pl.pallas_call(
        body, name=name, grid=(n,),
        in_specs=[cur(0), cur(1), cur(2), prv(1), prv(2), cur(0), nxt(0), nxt(0),
                  pl.BlockSpec((8, C), lambda i: (0, 0)), pl.BlockSpec((1, C), lambda i: (0, 0))],
        out_specs=[pl.BlockSpec((ts, 3 * C), lambda i: (i, 0)), pl.BlockSpec((8, C), lambda i: (0, 0))],
        out_shape=[jax.ShapeDtypeStruct((S, 3 * C), BF16), jax.ShapeDtypeStruct((8, C), F32)],
        compiler_params=_params("arbitrary"),
    )(P, P, P, P, P, dya, P, dya, conv_w, conv_b)


def _sgu_fwd(P, ln_g, ln_b, wm, bT, *, W, cu, cv, name):
    S = P.shape[0]
    G, CH, _ = wm.shape
    gw = W // G
    ts = _tile(S, 512, CH)

    def body(u_ref, v_ref, g_ref, b_ref, wm_ref, bT_ref, o_ref):
        gv = _gelu(v_ref[...])
        mu = jnp.mean(gv, axis=1, keepdims=True)
        xc = gv - mu
        vn = (xc * lax.rsqrt(jnp.mean(xc * xc, axis=1, keepdims=True) + EPS) * g_ref[...] + b_ref[...]).astype(BF16)
        bT_v = bT_ref[...]
        for c in range(ts // CH):
            rows = slice(c * CH, (c + 1) * CH)
            for g in range(G):
                cols = slice(g * gw, (g + 1) * gw)
                mixed = jnp.dot(wm_ref[g], vn[rows, cols], preferred_element_type=F32) + bT_v[:, g:g + 1]
                o_ref[rows, cols] = (_gelu(u_ref[rows, cols]) * mixed).astype(BF16)

    full = lambda shp: pl.BlockSpec(shp, lambda i: (0,) * len(shp))
    return pl.pallas_call(
        body, name=name, grid=(S // ts,),
        in_specs=[pl.BlockSpec((ts, W), lambda i: (i, cu)), pl.BlockSpec((ts, W), lambda i: (i, cv)),
                  full((1, W)), full((1, W)), full((G, CH, CH)), full((CH, G))],
        out_specs=pl.BlockSpec((ts, W), lambda i: (i, 0)),
        out_shape=jax.ShapeDtypeStruct((S, W), BF16),
        compiler_params=_params("parallel"),
    )(P, P, ln_g, ln_b, wm, bT)


def _sgu_bwd(P, dyb, ln_g, ln_b, wm, wmT, bT, *, W, cu, cv, name):
    S = P.shape[0]
    G, CH, _ = wm.shape
    gw = W // G
    ts = _tile(S, 512, CH)

    def body(u_ref, v_ref, dy_ref, g_ref, b_ref, wm_ref, wmT_ref, bT_ref, o_ref, dw_ref, db_ref, dln_ref, dvn_ref):
        i = pl.program_id(0)

        @pl.when(i == 0)
        def _():
            dw_ref[...] = jnp.zeros_like(dw_ref)
            db_ref[...] = jnp.zeros_like(db_ref)
            dln_ref[...] = jnp.zeros_like(dln_ref)

        sv = v_ref[...]
        gv = _gelu(sv)
        mu = jnp.mean(gv, axis=1, keepdims=True)
        xc = gv - mu
        rstd = lax.rsqrt(jnp.mean(xc * xc, axis=1, keepdims=True) + EPS)
        xhat = xc * rstd
        lg = g_ref[...]
        vn = (xhat * lg + b_ref[...]).astype(BF16)
        bT_v = bT_ref[...]
        for c in range(ts // CH):
            rows = slice(c * CH, (c + 1) * CH)
            for g in range(G):
                cols = slice(g * gw, (g + 1) * gw)
                vn_cg = vn[rows, cols]
                mixed = jnp.dot(wm_ref[g], vn_cg, preferred_element_type=F32) + bT_v[:, g:g + 1]
                su = u_ref[rows, cols]
                dyv = dy_ref[rows, cols]
                dmix = dyv * _gelu(su)
                o_ref[rows, cols] = (dyv * mixed * _gelu_grad(su)).astype(BF16)
                dmix_b = dmix.astype(BF16)
                dw_ref[g] += lax.dot_general(dmix_b, vn_cg, (((1,), (1,)), ((), ())), preferred_element_type=F32)
                db_ref[g] += jnp.broadcast_to(jnp.sum(dmix, axis=1, keepdims=True), (CH, CH))
                dvn_ref[rows, cols] = jnp.dot(wmT_ref[g], dmix_b, preferred_element_type=F32)
        dvn = dvn_ref[...]
        dxh = dvn * lg
        dgv = rstd * (dxh - jnp.mean(dxh, axis=1, keepdims=True) - xhat * jnp.mean(dxh * xhat, axis=1, keepdims=True))
        o_ref[:, W:2 * W] = (dgv * _gelu_grad(sv)).astype(BF16)
        dln_ref[0:1, :] += jnp.sum(dvn * xhat, axis=0, keepdims=True)
        dln_ref[1:2, :] += jnp.sum(dvn, axis=0, keepdims=True)

    full = lambda shp: pl.BlockSpec(shp, lambda i: (0,) * len(shp))
    return pl.pallas_call(
        body, name=name, grid=(S // ts,),
        in_specs=[pl.BlockSpec((ts, W), lambda i: (i, cu)), pl.BlockSpec((ts, W), lambda i: (i, cv)),
                  pl.BlockSpec((ts, W), lambda i: (i, 0)),
                  full((1, W)), full((1, W)), full((G, CH, CH)), full((G, CH, CH)), full((CH, G))],
        out_specs=[pl.BlockSpec((ts, 2 * W), lambda i: (i, 0)), full((G, CH, CH)), full((G, CH, CH)), full((8, W))],
        out_shape=[jax.ShapeDtypeStruct((S, 2 * W), BF16), jax.ShapeDtypeStruct((G, CH, CH), F32),
                   jax.ShapeDtypeStruct((G, CH, CH), F32), jax.ShapeDtypeStruct((8, W), F32)],
        scratch_shapes=[pltpu.VMEM((ts, W), F32)],
        compiler_params=_params("arbitrary"),
    )(P, P, dyb, ln_g, ln_b, wm, wmT, bT)


def _block_sums(x, u, parts=1):
    hi = x.astype(BF16)
    out = jnp.dot(hi, u, preferred_element_type=F32)
    if parts == 2:
        lo = (x - hi.astype(F32)).astype(BF16)
        out = out + jnp.dot(lo, u, preferred_element_type=F32)
    return out


_NT = (((1,), (1,)), ((), ()))
_TN = (((0,), (0,)), ((), ()))


def _left_blocks(step, jd, carry, unroll, jd_multiple):
    rem = 0
    if jd_multiple % unroll:
        rem = jd % unroll
        carry = lax.fori_loop(0, rem, lambda t, c: step(jd - 1 - t, c, False), carry)

    def trip(t, c):
        for s in range(unroll):
            c = step(jd - rem - 1 - s - unroll * t, c, False)
        return c

    return lax.fori_loop(0, jd // unroll, trip, carry)


def _diag_step(step, j, carry, row0):
    if row0 == 0:
        return step(j, carry, True)
    tail = step(j, tuple(c[row0:] for c in carry), True, row0)
    return tuple(jnp.concatenate([c[:row0], t], axis=0) for c, t in zip(carry, tail))


def _qkv_heads_fwd(P, gq, gk, *, A, col0, hd, qscale, name):
    S = P.shape[0]
    H = A // hd
    ts = _tile(S, 512, 8)

    def body(q_ref, k_ref, v_ref, gq_ref, gk_ref, qn_ref, kn_ref, vh_ref):
        for h in range(H):
            cols = slice(h * hd, (h + 1) * hd)
            for x_ref, g_ref, sc, o_ref in ((q_ref, gq_ref, qscale, qn_ref), (k_ref, gk_ref, 1.0, kn_ref)):
                xh = x_ref[:, cols]
                r = lax.rsqrt(jnp.mean(xh * xh, axis=1, keepdims=True) + EPS)
                o_ref[h] = (xh * r * (g_ref[...] * sc)).astype(BF16)
            vh_ref[h] = v_ref[:, cols].astype(BF16)

    col = lambda n: pl.BlockSpec((ts, A), lambda i: (i, col0 + n))
    gsp = pl.BlockSpec((1, hd), lambda i: (0, 0))
    hsp = pl.BlockSpec((H, ts, hd), lambda i: (0, i, 0))
    shp = jax.ShapeDtypeStruct((H, S, hd), BF16)
    return pl.pallas_call(
        body, name=name, grid=(S // ts,), in_specs=[col(0), col(1), col(2), gsp, gsp],
        out_specs=[hsp, hsp, hsp], out_shape=[shp, shp, shp], compiler_params=_params("parallel"),
    )(P, P, P, gq, gk)


def _qkv_heads_bwd(P, gq, gk, dqn, dkn, dvh, *, A, col0, hd, qscale, name):
    S = P.shape[0]
    H = A // hd
    ts = _tile(S, 512, 8)

    def body(q_ref, k_ref, gq_ref, gk_ref, dq_ref, dk_ref, dv_ref, o_ref, dgq_ref, dgk_ref):
        i = pl.program_id(0)
        parts = [jnp.zeros((1, hd), F32), jnp.zeros((1, hd), F32)]
        for h in range(H):
            for n, (x_ref, g_ref, sc, d_ref) in enumerate(((q_ref, gq_ref, qscale, dq_ref), (k_ref, gk_ref, 1.0, dk_ref))):
                xh = x_ref[:, h * hd:(h + 1) * hd]
                dyv = d_ref[h] * sc
                r = lax.rsqrt(jnp.mean(xh * xh, axis=1, keepdims=True) + EPS)
                u = dyv * g_ref[...]
                dx = r * u - xh * (r * r * r * jnp.mean(u * xh, axis=1, keepdims=True))
                o_ref[:, n * A + h * hd:n * A + (h + 1) * hd] = dx.astype(BF16)
                parts[n] = parts[n] + jnp.sum(dyv * xh * r, axis=0, keepdims=True)
            o_ref[:, 2 * A + h * hd:2 * A + (h + 1) * hd] = dv_ref[h].astype(BF16)

        @pl.when(i == 0)
        def _():
            dgq_ref[...] = parts[0]
            dgk_ref[...] = parts[1]

        @pl.when(i > 0)
        def _():
            dgq_ref[...] += parts[0]
            dgk_ref[...] += parts[1]

    col = lambda n: pl.BlockSpec((ts, A), lambda i: (i, col0 + n))
    gsp = pl.BlockSpec((1, hd), lambda i: (0, 0))
    hsp = pl.BlockSpec((H, ts, hd), lambda i: (0, i, 0))
    return pl.pallas_call(
        body, name=name, grid=(S // ts,), in_specs=[col(0), col(1), gsp, gsp, hsp, hsp, hsp],
        out_specs=[pl.BlockSpec((ts, 3 * A), lambda i: (i, 0)), gsp, gsp],
        out_shape=[jax.ShapeDtypeStruct((S, 3 * A), BF16), jax.ShapeDtypeStruct((1, hd), F32),
                   jax.ShapeDtypeStruct((1, hd), F32)],
        compiler_params=_params("arbitrary"),
    )(P, P, gq, gk, dqn, dkn, dvh)


def _comm_begin(schedule, cin_refs, cout_refs, sems, grid):
    begin, end = _comm_phases(schedule, cin_refs, cout_refs, *sems)
    ids = [pl.program_id(d) for d in range(len(grid))]
    pl.when(functools.reduce(jnp.logical_and, [p == 0 for p in ids]))(begin)

    def comm_end():
        pl.when(functools.reduce(jnp.logical_and, [p == g - 1 for p, g in zip(ids, grid)]))(end)

    return comm_end


def _call_with_comm(body, name, grid, in_specs, out_specs, out_shape, args, comm):
    if comm is None:
        return pl.pallas_call(body, name=name, grid=grid, in_specs=in_specs, out_specs=out_specs, out_shape=out_shape,
                              compiler_params=_params("parallel", "arbitrary"))(*args)
    _, xs, cshapes = comm
    anyspec = pl.BlockSpec(memory_space=pl.ANY)
    return pl.pallas_call(
        body, name=name, grid=grid, in_specs=in_specs + [anyspec] * len(xs), out_specs=out_specs + [anyspec] * len(xs),
        out_shape=out_shape + [jax.ShapeDtypeStruct(c, x.dtype) for c, x in zip(cshapes, xs)],
        scratch_shapes=_comm_scratch(len(xs)), compiler_params=_params("arbitrary", "arbitrary"),
    )(*args, *xs)


def _attn_fwd(q, k, v, umat, *, tq, tk, name, comm=None):
    H, S, hd = q.shape

    def body(*refs):
        if comm is None:
            q_ref, k_ref, v_ref, u_ref, o_ref = refs
        else:
            nc = len(comm[1])
            q_ref, k_ref, v_ref, u_ref = refs[:4]
            o_ref = refs[4 + nc]
            comm_end = _comm_begin(comm[0], refs[4:4 + nc], refs[5 + nc:5 + 2 * nc], refs[5 + 2 * nc:], (H, S // tq))
        i = pl.program_id(1)
        qb = q_ref[0]
        um = u_ref[...]
        qpos = lax.broadcasted_iota(jnp.int32, (tq, tk), 0) + i * tq
        kloc = lax.broadcasted_iota(jnp.int32, (tq, tk), 1)

        def step(j, carry, masked, row0=0):
            r, acc = carry
            ks = pl.multiple_of(j * tk, tk)
            kb = k_ref[0, pl.ds(ks, tk), :]
            vb = v_ref[0, pl.ds(ks, tk), :]
            z = lax.dot_general(qb[row0:], kb, _NT, preferred_element_type=F32)
            lb = jnp.minimum(z, 0.0) - jnp.log(1.0 + jnp.exp(-jnp.abs(z)))
            lm = lb - z
            if masked:
                m = (kloc[row0:] + j * tk) < qpos[row0:]
                lm = jnp.where(m, lm, 0.0)
            a = jnp.exp(lb + _block_sums(lm, um) + r)
            if masked:
                a = jnp.where(m, a, 0.0)
            acc = acc + jnp.dot(a.astype(BF16), vb, preferred_element_type=F32)
            return r + jnp.sum(lm, axis=1, keepdims=True), acc

        jd = (i * tq) // tk
        carry = (jnp.zeros((tq, 1), F32), jnp.zeros((tq, hd), F32))
        for dd in reversed(range(max(1, tq // tk))):
            carry = _diag_step(step, jd + dd, carry, dd * tk if tq > tk else 0)
        carry = _left_blocks(step, jd, carry, ATTN_FWD_UNROLL, tq // tk)
        o_ref[0] = carry[1]
        if comm is not None:
            comm_end()

    blk = pl.BlockSpec((1, tq, hd), lambda h, i: (h, i, 0))
    whole = pl.BlockSpec((1, S, hd), lambda h, i: (h, 0, 0))
    in_specs = [blk, whole, whole, pl.BlockSpec((tk, tk), lambda h, i: (0, 0))]
    out_specs, out_shape = [blk], [jax.ShapeDtypeStruct((H, S, hd), F32)]
    res = _call_with_comm(body, name, (H, S // tq), in_specs, out_specs, out_shape, (q, k, v, umat), comm)
    return res[0] if comm is None else (res[0], res[1:])


def _attn_bwd(q, k, v, o, do, umat, *, tq, tk, name, comm=None):
    H, S, hd = q.shape

    def body(*refs):
        if comm is None:
            q_ref, k_ref, v_ref, o_ref, do_ref, u_ref, dq_ref, dk_ref, dv_ref = refs
        else:
            nc = len(comm[1])
            q_ref, k_ref, v_ref, o_ref, do_ref, u_ref = refs[:6]
            dq_ref, dk_ref, dv_ref = refs[6 + nc:9 + nc]
            comm_end = _comm_begin(comm[0], refs[6:6 + nc], refs[9 + nc:9 + 2 * nc], refs[9 + 2 * nc:], (H, S // tq))
        i = pl.program_id(1)

        @pl.when(i == 0)
        def _():
            dk_ref[...] = jnp.zeros_like(dk_ref)
            dv_ref[...] = jnp.zeros_like(dv_ref)

        qb = q_ref[0]
        do32 = do_ref[0]
        dob = do32.astype(BF16)
        tot = jnp.sum(dob.astype(F32) * o_ref[0], axis=1, keepdims=True)
        um = u_ref[...]
        qpos = lax.broadcasted_iota(jnp.int32, (tq, tk), 0) + i * tq
        kloc = lax.broadcasted_iota(jnp.int32, (tq, tk), 1)

        def step(j, carry, masked, row0=0):
            r, gs, dq = carry
            ks = pl.multiple_of(j * tk, tk)
            kb = k_ref[0, pl.ds(ks, tk), :]
            vb = v_ref[0, pl.ds(ks, tk), :]
            qs, dos = qb[row0:], dob[row0:]
            z = lax.dot_general(qs, kb, _NT, preferred_element_type=F32)
            lb = jnp.minimum(z, 0.0) - jnp.log(1.0 + jnp.exp(-jnp.abs(z)))
            sig = jnp.exp(lb)
            lm = lb - z
            if masked:
                m = (kloc[row0:] + j * tk) < qpos[row0:]
                lm = jnp.where(m, lm, 0.0)
            a = jnp.exp(lb + _block_sums(lm, um) + r)
            if masked:
                a = jnp.where(m, a, 0.0)
            ab = a.astype(BF16)
            g = lax.dot_general(dos, vb, _NT, preferred_element_type=F32) * ab.astype(F32)
            dz = g - sig * ((tot[row0:] - gs) - _block_sums(g, um, parts=2))
            if masked:
                dz = jnp.where(m, dz, 0.0)
            dzb = dz.astype(BF16)
            dq = dq + jnp.dot(dzb, kb, preferred_element_type=F32)
            dk_ref[0, pl.ds(ks, tk), :] += lax.dot_general(dzb, qs, _TN, preferred_element_type=F32)
            dv_ref[0, pl.ds(ks, tk), :] += lax.dot_general(ab, dos, _TN, preferred_element_type=F32)
            return r + jnp.sum(lm, axis=1, keepdims=True), gs + jnp.sum(g, axis=1, keepdims=True), dq

        jd = (i * tq) // tk
        zero = jnp.zeros((tq, 1), F32)
        carry = (zero, zero, jnp.zeros((tq, hd), F32))
        for dd in reversed(range(max(1, tq // tk))):
            carry = _diag_step(step, jd + dd, carry, dd * tk if tq > tk else 0)
        carry = _left_blocks(step, jd, carry, ATTN_BWD_UNROLL, tq // tk)
        dq_ref[0] = carry[2]
        if comm is not None:
            comm_end()

    blk = pl.BlockSpec((1, tq, hd), lambda h, i: (h, i, 0))
    whole = pl.BlockSpec((1, S, hd), lambda h, i: (h, 0, 0))
    shp = jax.ShapeDtypeStruct((H, S, hd), F32)
    in_specs = [blk, whole, whole, blk, blk, pl.BlockSpec((tk, tk), lambda h, i: (0, 0))]
    res = _call_with_comm(body, name, (H, S // tq), in_specs, [blk, whole, whole], [shp, shp, shp],
                          (q, k, v, o, do, umat), comm)
    return res if comm is None else (*res[:3], res[3:])


def _merge_fwd(ya, yb, o, wb, P, b_gate, *, gate_col0, name):
    S, C = ya.shape
    H, _, hd = o.shape
    D = wb.shape[2]
    ts = _tile(S, 512, 8)

    def body(y0, y1, oh_ref, wb_ref, g0, g1, g2, bg_ref, m_ref, yc_ref):
        for h in range(H):
            yc_ref[:, h * hd:(h + 1) * hd] = oh_ref[h].astype(BF16)
        acc = jnp.zeros((ts, D), F32)
        for n, (y_ref, g_ref) in enumerate(((y0, g0), (y1, g1), (yc_ref, g2))):
            yd = jnp.dot(y_ref[...], wb_ref[n], preferred_element_type=F32)
            acc = acc + _sigmoid(g_ref[...] + bg_ref[:, n * D:(n + 1) * D]) * yd
        m_ref[...] = acc.astype(BF16)

    ysp = pl.BlockSpec((ts, C), lambda i: (i, 0))
    gsp = lambda n: pl.BlockSpec((ts, D), lambda i: (i, gate_col0 + n))
    return pl.pallas_call(
        body, name=name, grid=(S // ts,),
        in_specs=[ysp, ysp, pl.BlockSpec((H, ts, hd), lambda i: (0, i, 0)), pl.BlockSpec((3, C, D), lambda i: (0, 0, 0)),
                  gsp(0), gsp(1), gsp(2), pl.BlockSpec((1, 3 * D), lambda i: (0, 0))],
        out_specs=[pl.BlockSpec((ts, D), lambda i: (i, 0)), ysp],
        out_shape=[jax.ShapeDtypeStruct((S, D), BF16), jax.ShapeDtypeStruct((S, C), BF16)],
        compiler_params=_params("parallel"),
    )(ya, yb, o, wb, P, P, P, b_gate)


def _merge_bwd(ys, wb, wbT, P, b_gate, dmerged, *, gate_col0, hd, name):
    S, C = ys[0].shape
    D = wb.shape[2]
    H = C // hd
    ts = _tile(S, 256, 8)

    def body(y0, y1, y2, wb_ref, wbT_ref, g0, g1, g2, bg_ref, dm_ref,
             dg_ref, dyd0, dyd1, dyd2, dya_ref, dyb_ref, do_ref, dbg_ref):
        i = pl.program_id(0)
        dm = dm_ref[...].astype(F32)
        parts = []
        for n, (y_ref, g_ref, dyd_ref) in enumerate(((y0, g0, dyd0), (y1, g1, dyd1), (y2, g2, dyd2))):
            yd = jnp.dot(y_ref[...], wb_ref[n], preferred_element_type=F32)
            sg = _sigmoid(g_ref[...] + bg_ref[:, n * D:(n + 1) * D])
            dgate = dm * yd * sg * (1.0 - sg)
            dg_ref[:, n * D:(n + 1) * D] = dgate.astype(BF16)
            parts.append(jnp.sum(dgate, axis=0, keepdims=True))
            dyd = (dm * sg).astype(BF16)
            dyd_ref[...] = dyd
            dy = jnp.dot(dyd, wbT_ref[n], preferred_element_type=F32)
            if n == 0:
                dya_ref[...] = dy
            elif n == 1:
                dyb_ref[...] = dy
            else:
                for h in range(H):
                    do_ref[h] = dy[:, h * hd:(h + 1) * hd]
        part = jnp.concatenate(parts, axis=1)

        @pl.when(i == 0)
        def _():
            dbg_ref[...] = part

        @pl.when(i > 0)
        def _():
            dbg_ref[...] += part

    ysp = pl.BlockSpec((ts, C), lambda i: (i, 0))
    dsp = pl.BlockSpec((ts, D), lambda i: (i, 0))
    gsp = lambda n: pl.BlockSpec((ts, D), lambda i: (i, gate_col0 + n))
    dshp = jax.ShapeDtypeStruct((S, D), BF16)
    yshp = jax.ShapeDtypeStruct((S, C), F32)
    return pl.pallas_call(
        body, name=name, grid=(S // ts,),
        in_specs=[ysp, ysp, ysp, pl.BlockSpec((3, C, D), lambda i: (0, 0, 0)),
                  pl.BlockSpec((3, D, C), lambda i: (0, 0, 0)), gsp(0), gsp(1), gsp(2),
                  pl.BlockSpec((1, 3 * D), lambda i: (0, 0)), dsp],
        out_specs=[pl.BlockSpec((ts, 3 * D), lambda i: (i, 0)), dsp, dsp, dsp, ysp, ysp,
                   pl.BlockSpec((H, ts, hd), lambda i: (0, i, 0)), pl.BlockSpec((1, 3 * D), lambda i: (0, 0))],
        out_shape=[jax.ShapeDtypeStruct((S, 3 * D), BF16), dshp, dshp, dshp, yshp, yshp,
                   jax.ShapeDtypeStruct((H, S, hd), F32), jax.ShapeDtypeStruct((1, 3 * D), F32)],
        compiler_params=_params("arbitrary"),
    )(*ys, wb, wbT, P, P, P, b_gate, dmerged)


def _swiglu_fwd(gu, *, name):
    S, F2 = gu.shape
    F = F2 // 2
    ts, tf = _tile(S, 512, 8), _tile(F, 1536, LANES)
    nf = F // tf

    def body(g_ref, u_ref, o_ref):
        gt = g_ref[...].astype(F32)
        o_ref[...] = (gt * _sigmoid(gt) * u_ref[...].astype(F32)).astype(BF16)

    return pl.pallas_call(
        body, name=name, grid=(S // ts, nf),
        in_specs=[pl.BlockSpec((ts, tf), lambda i, j: (i, j)), pl.BlockSpec((ts, tf), lambda i, j: (i, j + nf))],
        out_specs=pl.BlockSpec((ts, tf), lambda i, j: (i, j)),
        out_shape=jax.ShapeDtypeStruct((S, F), BF16),
        compiler_params=_params("parallel", "parallel"),
    )(gu, gu)


def _swiglu_bwd(gu, dact, *, name):
    S, F2 = gu.shape
    F = F2 // 2
    ts = _tile(S, 256, 8)

    def body(gu_ref, d_ref, o_ref):
        gt, up, da = gu_ref[:, 0:F].astype(F32), gu_ref[:, F:F2].astype(F32), d_ref[...].astype(F32)
        sg = _sigmoid(gt)
        o_ref[:, 0:F] = (da * up * sg * (1.0 + gt * (1.0 - sg))).astype(BF16)
        o_ref[:, F:F2] = (da * gt * sg).astype(BF16)

    return pl.pallas_call(
        body, name=name, grid=(S // ts,),
        in_specs=[pl.BlockSpec((ts, F2), lambda i: (i, 0)), pl.BlockSpec((ts, F), lambda i: (i, 0))],
        out_specs=pl.BlockSpec((ts, F2), lambda i: (i, 0)),
        out_shape=jax.ShapeDtypeStruct((S, F2), BF16),
        compiler_params=_params("parallel"),
    )(gu, dact)


def _loss_grad(y, target, *, name):
    S, D = y.shape
    ts = _tile(S, 512, 8)

    def body(y_ref, t_ref, dy_ref, l_ref, dyb_ref):
        i = pl.program_id(0)
        err = y_ref[...] - t_ref[...]
        dy_ref[...] = err * (1.0 / D)
        dyb_ref[...] = (err * (1.0 / D)).astype(BF16)
        part = jnp.broadcast_to(jnp.sum(jnp.sum(err * err, axis=1, keepdims=True), axis=0, keepdims=True) * (0.5 / D),
                                (1, LANES))

        @pl.when(i == 0)
        def _():
            l_ref[...] = part

        @pl.when(i > 0)
        def _():
            l_ref[...] += part

    row = pl.BlockSpec((ts, D), lambda i: (i, 0))
    return pl.pallas_call(
        body, name=name, grid=(S // ts,), in_specs=[row, row],
        out_specs=[row, pl.BlockSpec((1, LANES), lambda i: (0, 0)), row],
        out_shape=[jax.ShapeDtypeStruct((S, D), F32), jax.ShapeDtypeStruct((1, LANES), F32),
                   jax.ShapeDtypeStruct((S, D), BF16)],
        compiler_params=_params("arbitrary"),
    )(y, target)


def _adamw(w, m, v, gs_list, *, name):
    R, W = w.shape
    rows = [g.shape[1] for g in gs_list]
    tr = _tile(math.gcd(*rows), max(16, (2048 * LANES // W) // 16 * 16), 16)
    first = [sum(rows[:t]) // tr for t in range(len(rows))]
    c1 = 1.0 / (1.0 - ADAM_B1 ** ADAM_STEP)
    c2 = 1.0 / (1.0 - ADAM_B2 ** ADAM_STEP)

    def body(w_ref, m_ref, v_ref, *refs):
        gs_refs, (g_ref, d_ref, nm_ref, nv_ref) = refs[:len(rows)], refs[len(rows):]
        i = pl.program_id(0)
        g = None
        for t, gs_ref in enumerate(gs_refs):
            gt = gs_ref[0].astype(F32)
            for s in range(1, gs_ref.shape[0]):
                gt = gt + gs_ref[s].astype(F32)
            g = gt if g is None else jnp.where(i >= first[t], gt, g)
        nm = ADAM_B1 * m_ref[...] + (1.0 - ADAM_B1) * g
        nv = ADAM_B2 * v_ref[...] + (1.0 - ADAM_B2) * (g * g)
        g_ref[...] = g
        nm_ref[...] = nm
        nv_ref[...] = nv
        d_ref[...] = -ADAM_LR * ((nm * c1) / (jnp.sqrt(nv * c2) + ADAM_EPS) + ADAM_WD * w_ref[...])

    row = pl.BlockSpec((tr, W), lambda i: (i, 0))
    slots = [pl.BlockSpec((g.shape[0], tr, W),
                          lambda i, b0=first[t], nb=rows[t] // tr: (0, jnp.clip(i - b0, 0, nb - 1), 0))
             for t, g in enumerate(gs_list)]
    shp = jax.ShapeDtypeStruct((R, W), F32)
    return pl.pallas_call(
        body, name=name, grid=(R // tr,), in_specs=[row, row, row] + slots,
        out_specs=[row, row, row, row], out_shape=[shp, shp, shp, shp],
        compiler_params=_params("parallel"),
    )(w, m, v, *gs_list)


def _slot_sum(gs, *, name):
    ns, R, _ = gs.shape
    tr = _tile(R, 2048, 16)

    def body(gs_ref, o_ref):
        g = gs_ref[0]
        for s in range(1, ns):
            g = g + gs_ref[s]
        o_ref[...] = g

    return pl.pallas_call(
        body, name=name, grid=(R // tr,),
        in_specs=[pl.BlockSpec((ns, tr, LANES), lambda i: (0, i, 0))],
        out_specs=pl.BlockSpec((tr, LANES), lambda i: (i, 0)),
        out_shape=jax.ShapeDtypeStruct((R, LANES), F32),
        compiler_params=_params("parallel"),
    )(gs)


def _comm_scratch(n):
    return [pltpu.SemaphoreType.DMA((7 * n,)), pltpu.SemaphoreType.DMA((7 * n,)), pltpu.SemaphoreType.DMA((n,))]


def _gather_schedule(x_ref, out_ref, send_sems, recv_sems, local_sem, base):
    x, y, c = lax.axis_index("x"), lax.axis_index("y"), lax.axis_index("c")
    me, sibling = (x, y, c), (x, y, 1 - c)
    chips = [(1 - x, y), (x, 1 - y), (1 - x, 1 - y)]

    def slot(px, py, pc):
        return out_ref.at[4 * px + 2 * py + pc]

    def copy(k, block, to, src=None):
        return pltpu.make_async_remote_copy(
            src_ref=slot(*block) if src is None else src, dst_ref=slot(*block),
            send_sem=send_sems.at[base + k], recv_sem=recv_sems.at[base + k], device_id=to, device_id_type=MESH)

    mine = pltpu.make_async_copy(x_ref, slot(*me), local_sem)
    first = [copy(0, me, sibling, src=x_ref)]
    first += [copy(1 + j, me, (*chip, c), src=x_ref) for j, chip in enumerate(chips)]
    passed = [copy(4 + j, (*chip, c), sibling) for j, chip in enumerate(chips)]

    def begin():
        mine.start()
        for cp in first:
            cp.start()

    def forward():
        for j, chip in enumerate(chips):
            copy(1 + j, (*chip, c), me).wait_recv()
            passed[j].start()

    def finish():
        copy(0, sibling, me).wait_recv()
        for j, chip in enumerate(chips):
            copy(4 + j, (*chip, 1 - c), me).wait_recv()
        for cp in first + passed:
            cp.wait_send()
        mine.wait()

    return begin, forward, finish


def _exchange_schedule(x_ref, out_ref, send_sems, recv_sems, local_sem, base):
    x, y, c = lax.axis_index("x"), lax.axis_index("y"), lax.axis_index("c")
    me = 4 * x + 2 * y + c
    mine = pltpu.make_async_copy(x_ref.at[me], out_ref.at[me], local_sem)
    sends, recvs = [], []
    for k in range(1, N_DEV):
        px = 1 - x if k & 4 else x
        py = 1 - y if k & 2 else y
        pc = 1 - c if k & 1 else c
        peer = 4 * px + 2 * py + pc
        sems = dict(send_sem=send_sems.at[base + k - 1], recv_sem=recv_sems.at[base + k - 1],
                    device_id=(px, py, pc), device_id_type=MESH)
        sends.append(pltpu.make_async_remote_copy(src_ref=x_ref.at[peer], dst_ref=out_ref.at[me], **sems))
        recvs.append(pltpu.make_async_remote_copy(src_ref=x_ref.at[me], dst_ref=out_ref.at[peer], **sems))

    def begin():
        mine.start()
        for cp in sends:
            cp.start()

    def forward():
        pass

    def finish():
        for cp in recvs:
            cp.wait_recv()
        for cp in sends:
            cp.wait_send()
        mine.wait()

    return begin, forward, finish


def _comm_phases(schedule, x_refs, out_refs, send_sems, recv_sems, local_sems):
    parts = [schedule(x, o, send_sems, recv_sems, local_sems.at[t], 7 * t)
             for t, (x, o) in enumerate(zip(x_refs, out_refs))]

    def begin():
        for b, _, _ in parts:
            b()

    def end():
        for _, fw, _ in parts:
            fw()
        for _, _, fin in parts:
            fin()

    return begin, end


def _all_gather(xs, *, name):
    def body(x_ref, out_ref, send_sems, recv_sems, local_sems):
        begin, end = _comm_phases(_gather_schedule, [x_ref], [out_ref], send_sems, recv_sems, local_sems)
        begin()
        end()

    return pl.pallas_call(
        body, name=name,
        in_specs=[pl.BlockSpec(memory_space=pl.ANY)], out_specs=pl.BlockSpec(memory_space=pl.ANY),
        out_shape=jax.ShapeDtypeStruct((N_DEV,) + xs.shape, xs.dtype), scratch_shapes=_comm_scratch(1),
        compiler_params=pltpu.CompilerParams(has_side_effects=True),
    )(xs)


PACK_ROWS = 16


def _pack_rows(parts, dtype, lead=()):
    rows = []
    for p in parts:
        r = p.reshape(lead + (-1, LANES)).astype(dtype)
        pad = (-r.shape[-2]) % PACK_ROWS
        rows.append(jnp.pad(r, [(0, 0)] * len(lead) + [(0, pad), (0, 0)]) if pad else r)
    return jnp.concatenate(rows, axis=len(lead))


def _unpack_rows(packed, shapes, lead=()):
    out, off = [], 0
    for shp in shapes:
        n = math.prod(shp) // LANES
        out.append(lax.slice_in_dim(packed, off, off + n, axis=len(lead)).reshape(lead + tuple(shp)))
        off += n + (-n) % PACK_ROWS
    return out


def _unshard(gathered, axis):
    g = jnp.moveaxis(gathered, 0, axis)
    shp = list(g.shape)
    shp[axis:axis + 2] = [shp[axis] * shp[axis + 1]]
    return g.reshape(shp)


def _reshard(full, axis):
    shp = list(full.shape)
    shp[axis:axis + 1] = [N_DEV, shp[axis] // N_DEV]
    return jnp.moveaxis(full.reshape(shp), axis, 0)


def _late_parts(L):
    return [(k, l) for k in BIG for l in range(L) if (k, l) != ("w_in", 0)]


def _late_groups(L, shard_shape):
    groups = {}
    for k, l in _late_parts(L):
        groups.setdefault(shard_shape[k][-1], []).append((k, l))
    return groups


def _local_step(x, target, w_in0, late_local, groups, shard_shape, sm):
    S, D = x.shape
    L = sm["mix_norm_g"].shape[0]
    wf = {k: [None] * L for k in BIG}
    wf["w_in"][0] = w_in0
    C = sm["conv_b"].shape[1]
    W = sm["sgu_ln_g"].shape[1]
    hd = sm["q_norm_g"].shape[1]
    G, CH = sm["sgu_w"].shape[1], sm["sgu_w"].shape[2]
    A = w_in0.shape[1] - (3 * C + 2 * W + 3 * D)
    A = A // 3
    H = A // hd
    col_q = 3 * C + 2 * W
    qscale = 1.0 / math.sqrt(hd)
    tril = jnp.tril(jnp.ones((CH, CH), F32))
    (ftq, ftk), (btq, btk) = [(_tile(S, a, LANES), _tile(S, b, LANES)) for a, b in (ATTN_FWD_TILES, ATTN_BWD_TILES)]
    umat = lambda t: (lax.broadcasted_iota(jnp.int32, (t, t), 0) > lax.broadcasted_iota(jnp.int32, (t, t), 1)).astype(BF16)

    saved = []
    for l in range(L):
        n = f"l{l}_"
        g1 = sm["mix_norm_g"][l][None]
        h = _rmsnorm_fwd(x, g1, scale=1.0, out_dtype=BF16, name=n + "mixnorm")
        P = _matmul(h, wf["w_in"][l], name=n + "w_in")
        cw = jnp.pad(sm["conv_w"][l], ((0, 5), (0, 0)))
        cb = sm["conv_b"][l][None]
        ya = _conv_fwd(P, cw, cb, C=C, name=n + "conv")
        wm = (sm["sgu_w"][l] * tril).astype(BF16)
        bT = sm["sgu_b"][l].T
        lng, lnb = sm["sgu_ln_g"][l][None], sm["sgu_ln_b"][l][None]
        yb = _sgu_fwd(P, lng, lnb, wm, bT, W=W, cu=(3 * C) // W, cv=(3 * C) // W + 1, name=n + "sgu")
        gq, gk = sm["q_norm_g"][l][None], sm["k_norm_g"][l][None]
        qn, kn, vh = _qkv_heads_fwd(P, gq, gk, A=A, col0=col_q // A, hd=hd, qscale=qscale, name=n + "qkv")
        if l == 0:
            o, gathered = _attn_fwd(qn, kn, vh, umat(ftk), tq=ftq, tk=ftk, name=n + "attn",
                                    comm=(_gather_schedule, late_local, [(N_DEV,) + a.shape for a in late_local]))
            for (width, parts), g in zip(groups.items(), gathered):
                r = 0
                for k, ll in parts:
                    rows = math.prod(shard_shape[k]) // width
                    part = lax.slice_in_dim(g, r, r + rows, axis=1).reshape((N_DEV,) + tuple(shard_shape[k]))
                    wf[k][ll] = _unshard(part, SHARD_AXIS[k] - 1)
                    r += rows
        else:
            o = _attn_fwd(qn, kn, vh, umat(ftk), tq=ftq, tk=ftk, name=n + "attn")
        bg = sm["b_gate"][l][None]
        gate_col0 = (col_q + 3 * A) // D
        merged, yc = _merge_fwd(ya, yb, o, wf["w_branch_out"][l], P, bg, gate_col0=gate_col0, name=n + "merge")
        x1 = _matmul(merged, wf["w_o"][l], res=x, name=n + "w_o")
        g2 = sm["ffn_norm_g"][l][None]
        h2 = _rmsnorm_fwd(x1, g2, scale=1.0, out_dtype=BF16, name=n + "ffnnorm")
        gu = _matmul(h2, wf["w_gate_up"][l], name=n + "w_gate_up", out_dtype=BF16)
        act = _swiglu_fwd(gu, name=n + "swiglu")
        x2 = _matmul(act, wf["w_down"][l], res=x1, name=n + "w_down")
        saved.append(dict(x=x, h=h, P=P, cw=cw, cb=cb, ya=ya, wm=wm, bT=bT, lng=lng, lnb=lnb, yb=yb,
                          gq=gq, gk=gk, qn=qn, kn=kn, vh=vh, o=o, yc=yc, bg=bg,
                          gate_col0=gate_col0, merged=merged, x1=x1, g1=g1, g2=g2, h2=h2, gu=gu, act=act))
        x = x2

    dx, lpart, dxb = _loss_grad(x, target, name="loss")
    grads = {k: [None] * L for k in WEIGHTS}
    chunked = lambda k, g: g if k == "w_in" else _reshard(g, SHARD_AXIS[k] - 1)
    for l in reversed(range(L)):
        n = f"l{l}_b_"
        sv = saved[l]
        grads["w_down"][l] = _matmul_tn(sv["act"], dxb, name=n + "g_w_down", out_dtype=BF16)
        dact = _matmul(dxb, wf["w_down"][l].T, name=n + "d_act", out_dtype=BF16)
        dgu = _swiglu_bwd(sv["gu"], dact, name=n + "swiglu")
        grads["w_gate_up"][l] = _matmul_tn(sv["h2"], dgu, name=n + "g_w_gate_up", out_dtype=BF16)
        dh2 = _matmul(dgu, wf["w_gate_up"][l].T, name=n + "d_h2", out_dtype=BF16)
        dx1, dg2, dx1b = _rmsnorm_bwd(sv["x1"], sv["g2"], dh2, scale=1.0, dres=dx, name=n + "ffnnorm",
                                      bf16_copy=True)
        grads["ffn_norm_g"][l] = dg2[0]
        grads["w_o"][l] = _matmul_tn(sv["merged"], dx1b, name=n + "g_w_o", out_dtype=BF16)
        dmerged = _matmul(dx1b, wf["w_o"][l].T, name=n + "d_merged", out_dtype=BF16)
        ys = (sv["ya"], sv["yb"], sv["yc"])
        wb = wf["w_branch_out"][l]
        dgates, *dyd, dya, dyb, do, dbg = _merge_bwd(ys, wb, wb.transpose(0, 2, 1), sv["P"], sv["bg"], dmerged,
                                                     gate_col0=sv["gate_col0"], hd=hd, name=n + "merge")
        grads["b_gate"][l] = dbg[0]
        grads["w_branch_out"][l] = jnp.stack(
            [_matmul_tn(ys[i], dyd[i], name=n + f"g_w_branch{i}", out_dtype=BF16) for i in range(3)])
        dconv, dcw = _conv_bwd(sv["P"], dya, sv["cw"], sv["cb"], C=C, name=n + "conv")
        grads["conv_w"][l], grads["conv_b"][l] = dcw[0:3], dcw[3]
        wmT = sv["wm"].transpose(0, 2, 1)
        dsgu, dsw, dsb, dln = _sgu_bwd(sv["P"], dyb, sv["lng"], sv["lnb"], sv["wm"], wmT, sv["bT"], W=W,
                                       cu=(3 * C) // W, cv=(3 * C) // W + 1, name=n + "sgu")
        grads["sgu_w"][l], grads["sgu_b"][l] = dsw * tril, dsb[:, :, 0]
        grads["sgu_ln_g"][l], grads["sgu_ln_b"][l] = dln[0], dln[1]
        if l == 0:
            chunks = [jnp.concatenate([chunked(k, grads[k][ll]).reshape(N_DEV, -1, width) for k, ll in parts], axis=1)
                      for width, parts in groups.items()]
            dqn, dkn, dvh, late_recv = _attn_bwd(sv["qn"], sv["kn"], sv["vh"], sv["o"], do, umat(btk), tq=btq, tk=btk,
                                                 name=n + "attn",
                                                 comm=(_exchange_schedule, chunks, [c.shape for c in chunks]))
        else:
            dqn, dkn, dvh = _attn_bwd(sv["qn"], sv["kn"], sv["vh"], sv["o"], do, umat(btk), tq=btq, tk=btk,
                                      name=n + "attn")
        dqkv, dgq, dgk = _qkv_heads_bwd(sv["P"], sv["gq"], sv["gk"], dqn, dkn, dvh, A=A, col0=col_q // A, hd=hd,
                                        qscale=qscale, name=n + "qkv")
        grads["q_norm_g"][l], grads["k_norm_g"][l] = dgq[0], dgk[0]
        dP = jnp.concatenate([dconv, dsgu, dqkv, dgates], axis=1)
        grads["w_in"][l] = _matmul_tn(sv["h"], dP, name=n + "g_w_in", out_dtype=BF16, shard=dP.shape[1] // N_DEV)
        if l == 0:
            dh, early_recv = _matmul(dP, wf["w_in"][l].T, name=n + "d_h", out_dtype=BF16,
                                     comm=(_exchange_schedule, [grads["w_in"][0]], [grads["w_in"][0].shape]))
        else:
            dh = _matmul(dP, wf["w_in"][l].T, name=n + "d_h", out_dtype=BF16)
        dx, dg1, dxb = _rmsnorm_bwd(sv["x"], sv["g1"], dh, scale=1.0, dres=dx1, name=n + "mixnorm", bf16_copy=True)
        grads["mix_norm_g"][l] = dg1[0]
    return lpart[0, 0], dx, grads, early_recv, late_recv


def kernel(x, mix_norm_g, w_in, b_gate, conv_w, conv_b, sgu_ln_g, sgu_ln_b, sgu_w, sgu_b, q_norm_g, k_norm_g, w_branch_out, w_o, ffn_norm_g, w_gate_up, w_down, loss_target, m_mix_norm_g, m_w_in, m_b_gate, m_conv_w, m_conv_b, m_sgu_ln_g, m_sgu_ln_b, m_sgu_w, m_sgu_b, m_q_norm_g, m_k_norm_g, m_w_branch_out, m_w_o, m_ffn_norm_g, m_w_gate_up, m_w_down, v_mix_norm_g, v_w_in, v_b_gate, v_conv_w, v_conv_b, v_sgu_ln_g, v_sgu_ln_b, v_sgu_w, v_sgu_b, v_q_norm_g, v_k_norm_g, v_w_branch_out, v_w_o, v_ffn_norm_g, v_w_gate_up, v_w_down):
    w = dict(mix_norm_g=mix_norm_g, w_in=w_in, b_gate=b_gate, conv_w=conv_w, conv_b=conv_b, sgu_ln_g=sgu_ln_g,
             sgu_ln_b=sgu_ln_b, sgu_w=sgu_w, sgu_b=sgu_b, q_norm_g=q_norm_g, k_norm_g=k_norm_g,
             w_branch_out=w_branch_out, w_o=w_o, ffn_norm_g=ffn_norm_g, w_gate_up=w_gate_up, w_down=w_down)
    m = dict(mix_norm_g=m_mix_norm_g, w_in=m_w_in, b_gate=m_b_gate, conv_w=m_conv_w, conv_b=m_conv_b,
             sgu_ln_g=m_sgu_ln_g, sgu_ln_b=m_sgu_ln_b, sgu_w=m_sgu_w, sgu_b=m_sgu_b, q_norm_g=m_q_norm_g,
             k_norm_g=m_k_norm_g, w_branch_out=m_w_branch_out, w_o=m_w_o, ffn_norm_g=m_ffn_norm_g,
             w_gate_up=m_w_gate_up, w_down=m_w_down)
    v = dict(mix_norm_g=v_mix_norm_g, w_in=v_w_in, b_gate=v_b_gate, conv_w=v_conv_w, conv_b=v_conv_b,
             sgu_ln_g=v_sgu_ln_g, sgu_ln_b=v_sgu_ln_b, sgu_w=v_sgu_w, sgu_b=v_sgu_b, q_norm_g=v_q_norm_g,
             k_norm_g=v_k_norm_g, w_branch_out=v_w_branch_out, w_o=v_w_o, ffn_norm_g=v_ffn_norm_g,
             w_gate_up=v_w_gate_up, w_down=v_w_down)
    me = 4 * lax.axis_index("x") + 2 * lax.axis_index("y") + lax.axis_index("c")
    S = x.shape[1]

    L = w_in.shape[0]
    two_d = lambda a: a.reshape(-1, a.shape[-1])
    shard_shape = {k: w[k].shape[1:] for k in BIG}
    groups = _late_groups(L, shard_shape)
    w_in0 = _unshard(_all_gather(w_in[0].astype(BF16), name="gather_w_in0"), SHARD_AXIS["w_in"] - 1)
    late_local = [jnp.concatenate([two_d(w[k][l]).astype(BF16) for k, l in parts], axis=0) for parts in groups.values()]
    conv_g = _all_gather(_pack_rows([conv_w], F32), name="gather_conv_w")
    sm = {k: w[k] for k in SMALL}
    sm["conv_w"] = _unshard(_unpack_rows(conv_g, [conv_w.shape], lead=(N_DEV,))[0], 2)

    lpart, dx, grads, early_recv, late_recv = _local_step(x[0], loss_target[0], w_in0, late_local, groups,
                                                          shard_shape, sm)
    loss = lax.psum(lpart, ("x", "y", "c"))

    where = {("w_in", 0): (early_recv[0], 0)}
    for (width, parts), recv in zip(groups.items(), late_recv):
        r = 0
        for k, l in parts:
            where[k, l] = (recv, r)
            r += math.prod(shard_shape[k]) // width
    out = {}
    for k in BIG:
        rows = math.prod(shard_shape[k]) // shard_shape[k][-1]
        gs_list = []
        for l in range(L):
            recv, r = where[k, l]
            if gs_list and gs_list[-1][0] is recv and gs_list[-1][2] == r:
                gs_list[-1][2] = r + rows
            else:
                gs_list.append([recv, r, r + rows])
        gs_list = [recv if (r0, r1) == (0, recv.shape[1]) else lax.slice_in_dim(recv, r0, r1, axis=1)
                   for recv, r0, r1 in gs_list]
        res = _adamw(two_d(w[k]), two_d(m[k]), two_d(v[k]), gs_list, name="adamw_" + k)
        for nm, r in zip(("grad", "delta", "new_m", "new_v"), res):
            out[nm, k] = r.reshape(w[k].shape)

    small_grads = [jnp.stack(grads[k]) for k in SMALL]
    small_shapes = [g.shape for g in small_grads]
    sg = _all_gather(_pack_rows(small_grads, F32), name="gather_small_grads")
    gsum = _slot_sum(sg, name="sum_small_grads")
    gsmall = dict(zip(SMALL, _unpack_rows(gsum, small_shapes)))
    cshard = conv_w.shape[2]
    gsmall["conv_w"] = lax.dynamic_slice_in_dim(gsmall["conv_w"], me * cshard, cshard, axis=2)
    own_shapes = [w[k].shape for k in SMALL]
    gs_, ds_, ms_, vs_ = _adamw(_pack_rows([w[k] for k in SMALL], F32), _pack_rows([m[k] for k in SMALL], F32),
                                _pack_rows([v[k] for k in SMALL], F32),
                                [_pack_rows([gsmall[k] for k in SMALL], F32)[None]], name="adamw_small")
    for nm, packed in (("grad", gs_), ("delta", ds_), ("new_m", ms_), ("new_v", vs_)):
        for k, a in zip(SMALL, _unpack_rows(packed, own_shapes)):
            out[nm, k] = a

    res = [loss, dx[None]]
    for nm in ("grad", "delta", "new_m", "new_v"):
        res += [out[nm, k] for k in WEIGHTS]
    return tuple(res)
```

```python
import functools
import math

import jax
import jax.numpy as jnp
from jax import lax
from jax.experimental import pallas as pl
from jax.experimental.pallas import tpu as pltpu

F32 = jnp.float32
BF16 = jnp.bfloat16
MESH = pl.DeviceIdType.MESH

N_DEV = 8
LANES = 128
VMEM_LIMIT_BYTES = 56 * 1024 * 1024
EPS = 1e-6
ADAM_LR, ADAM_B1, ADAM_B2, ADAM_EPS, ADAM_WD, ADAM_STEP = 0.001, 0.9, 0.999, 1e-08, 0.01, 10
ATTN_FWD_TILES = (512, 512)
ATTN_BWD_TILES = (1024, 256)
ATTN_FWD_UNROLL = 4
ATTN_BWD_UNROLL = 4
BIG = ("w_in", "w_branch_out", "w_o", "w_gate_up", "w_down")
SMALL = ("mix_norm_g", "b_gate", "conv_w", "conv_b", "sgu_ln_g", "sgu_ln_b", "sgu_w", "sgu_b",
         "q_norm_g", "k_norm_g", "ffn_norm_g")
WEIGHTS = ("mix_norm_g", "w_in", "b_gate", "conv_w", "conv_b", "sgu_ln_g", "sgu_ln_b", "sgu_w", "sgu_b",
           "q_norm_g", "k_norm_g", "w_branch_out", "w_o", "ffn_norm_g", "w_gate_up", "w_down")
SHARD_AXIS = {"w_in": 2, "w_branch_out": 3, "w_o": 1, "w_gate_up": 2, "w_down": 1}


def _tile(n, cap, mult):
    best = None
    for t in range(mult, min(n, cap) + 1, mult):
        if n % t == 0:
            best = t
    return best if best is not None else n


def _params(*sem):
    return pltpu.CompilerParams(dimension_semantics=sem if sem else None, vmem_limit_bytes=VMEM_LIMIT_BYTES)


def _erf(x):
    return lax.erf(x)


def _gelu(x):
    return 0.5 * x * (1.0 + _erf(x * (1.0 / math.sqrt(2.0))))


def _gelu_grad(x):
    return 0.5 * (1.0 + _erf(x * (1.0 / math.sqrt(2.0)))) + x * jnp.exp(-0.5 * x * x) * (1.0 / math.sqrt(2.0 * math.pi))


def _sigmoid(x):
    return 1.0 / (1.0 + jnp.exp(-x))


def _matmul(a, b, *, name, res=None, out_dtype=F32, comm=None):
    M, K = a.shape
    _, N = b.shape
    tm, tn, tk = _tile(M, 1024, 8), _tile(N, 1536, LANES), _tile(K, 1536, LANES)
    nk = K // tk
    grid = (M // tm, N // tn, nk)
    has_res = res is not None

    def body(*refs):
        refs = list(refs)
        a_ref, b_ref = refs[:2]
        r_ref = refs[2] if has_res else None
        pos = 2 + has_res
        nc = len(comm[1]) if comm is not None else 0
        o_ref = refs[pos + nc]
        acc = refs[pos + 1 + 2 * nc] if nk > 1 else None
        if comm is not None:
            comm_end = _comm_begin(comm[0], refs[pos:pos + nc], refs[pos + nc + 1:pos + 2 * nc + 1],
                                   refs[pos + 2 * nc + 1 + (nk > 1):], grid)
        k = pl.program_id(2)
        part = jnp.dot(a_ref[...], b_ref[...], preferred_element_type=F32)

        def finish(v):
            if has_res:
                v = v + r_ref[...]
            o_ref[...] = v.astype(out_dtype)

        if nk == 1:
            finish(part)
        else:
            @pl.when(k == 0)
            def _():
                acc[...] = part

            @pl.when(jnp.logical_and(k > 0, k < nk - 1))
            def _():
                acc[...] += part

            @pl.when(k == nk - 1)
            def _():
                finish(acc[...] + part)

        if comm is not None:
            comm_end()

    in_specs = [pl.BlockSpec((tm, tk), lambda i, j, k: (i, k)), pl.BlockSpec((tk, tn), lambda i, j, k: (k, j))]
    args = [a, b]
    if has_res:
        in_specs.append(pl.BlockSpec((tm, tn), lambda i, j, k: (i, j)))
        args.append(res)
    out_specs = [pl.BlockSpec((tm, tn), lambda i, j, k: (i, j))]
    out_shape = [jax.ShapeDtypeStruct((M, N), out_dtype)]
    scratch = [pltpu.VMEM((tm, tn), F32)] if nk > 1 else []
    sem = ("parallel", "parallel", "arbitrary")
    if comm is not None:
        anyspec = pl.BlockSpec(memory_space=pl.ANY)
        in_specs += [anyspec] * len(comm[1])
        args += list(comm[1])
        out_specs += [anyspec] * len(comm[1])
        out_shape += [jax.ShapeDtypeStruct(c, x.dtype) for c, x in zip(comm[2], comm[1])]
        scratch += _comm_scratch(len(comm[1]))
        sem = ("arbitrary",) * 3
    out = pl.pallas_call(body, name=name, grid=grid, in_specs=in_specs, out_specs=out_specs, out_shape=out_shape,
                         scratch_shapes=scratch, compiler_params=_params(*sem))(*args)
    return out[0] if comm is None else (out[0], out[1:])


def _matmul_tn(x, y, *, name, out_dtype=F32, shard=None):
    S, A = x.shape
    _, B = y.shape
    ta, ts = _tile(A, 1536, LANES), _tile(S, 1024, 8)
    tb = shard if shard else _tile(B, 1536, LANES)
    ns = S // ts
    direct = out_dtype == F32
    view = (lambda r: r.at[0]) if shard else (lambda r: r)

    def body(x_ref, y_ref, o_ref, *scratch):
        s = pl.program_id(2)
        out = view(o_ref)
        acc = out if direct else scratch[0]
        part = lax.dot_general(x_ref[...], y_ref[...], (((0,), (0,)), ((), ())), preferred_element_type=F32)

        @pl.when(s == 0)
        def _():
            acc[...] = part

        @pl.when(s > 0)
        def _():
            acc[...] += part

        if not direct:
            @pl.when(s == ns - 1)
            def _():
                out[...] = acc[...].astype(out_dtype)

    if shard:
        out_spec, shape = pl.BlockSpec((1, ta, tb), lambda i, j, s: (j, i, 0)), (B // tb, A, tb)
    else:
        out_spec, shape = pl.BlockSpec((ta, tb), lambda i, j, s: (i, j)), (A, B)
    return pl.pallas_call(
        body, name=name, grid=(A // ta, B // tb, ns),
        in_specs=[pl.BlockSpec((ts, ta), lambda i, j, s: (s, i)), pl.BlockSpec((ts, tb), lambda i, j, s: (s, j))],
        out_specs=out_spec, out_shape=jax.ShapeDtypeStruct(shape, out_dtype),
        scratch_shapes=[] if direct else [pltpu.VMEM((ta, tb), F32)],
        compiler_params=_params("parallel", "parallel", "arbitrary"),
    )(x, y)


def _rmsnorm_fwd(x, g, *, scale, out_dtype, name):
    R, W = x.shape
    tr = _tile(R, 512 if W >= 512 else 4096, 8)

    def body(x_ref, g_ref, o_ref):
        xv = x_ref[...]
        r = lax.rsqrt(jnp.mean(xv * xv, axis=1, keepdims=True) + EPS)
        o_ref[...] = (xv * r * (g_ref[...] * scale)).astype(out_dtype)

    return pl.pallas_call(
        body, name=name, grid=(R // tr,),
        in_specs=[pl.BlockSpec((tr, W), lambda i: (i, 0)), pl.BlockSpec((1, W), lambda i: (0, 0))],
        out_specs=pl.BlockSpec((tr, W), lambda i: (i, 0)),
        out_shape=jax.ShapeDtypeStruct((R, W), out_dtype),
        compiler_params=_params("parallel"),
    )(x, g)


def _rmsnorm_bwd(x, g, dy, *, scale, name, dres=None, out_dtype=F32, bf16_copy=False):
    R, W = x.shape
    tr = _tile(R, 512 if W >= 512 else 4096, 8)
    has_res = dres is not None

    def body(*refs):
        refs = list(refs)
        dxb_ref = refs.pop() if bf16_copy else None
        if has_res:
            x_ref, g_ref, dy_ref, dres_ref, dx_ref, dg_ref = refs
        else:
            x_ref, g_ref, dy_ref, dx_ref, dg_ref = refs
        i = pl.program_id(0)
        xv = x_ref[...]
        dyv = dy_ref[...].astype(F32) * scale
        r = lax.rsqrt(jnp.mean(xv * xv, axis=1, keepdims=True) + EPS)
        u = dyv * g_ref[...]
        dx = r * u - xv * (r * r * r * jnp.mean(u * xv, axis=1, keepdims=True))
        if has_res:
            dx = dx + dres_ref[...]
        dx_ref[...] = dx.astype(out_dtype)
        if bf16_copy:
            dxb_ref[...] = dx.astype(BF16)
        part = jnp.sum(dyv * xv * r, axis=0, keepdims=True)

        @pl.when(i == 0)
        def _():
            dg_ref[...] = part

        @pl.when(i > 0)
        def _():
            dg_ref[...] += part

    row = pl.BlockSpec((tr, W), lambda i: (i, 0))
    one = pl.BlockSpec((1, W), lambda i: (0, 0))
    in_specs = [row, one, row] + ([row] if has_res else [])
    args = [x, g, dy] + ([dres] if has_res else [])
    extra = bool(bf16_copy)
    return pl.pallas_call(
        body, name=name, grid=(R // tr,), in_specs=in_specs, out_specs=[row, one] + [row] * extra,
        out_shape=[jax.ShapeDtypeStruct((R, W), out_dtype), jax.ShapeDtypeStruct((1, W), F32)]
        + [jax.ShapeDtypeStruct((R, W), BF16)] * extra,
        compiler_params=_params("arbitrary"),
    )(*args)


HALO = 16


def _shift_down(u, prev, n):
    out = pltpu.roll(u, n, 0)
    row = lax.broadcasted_iota(jnp.int32, u.shape, 0)
    for r in range(n):
        out = jnp.where(row == r, prev[HALO - n + r:HALO - n + r + 1, :], out)
    return out


def _shift_up(u, nxt, n):
    ts = u.shape[0]
    out = pltpu.roll(u, ts - n, 0)
    row = lax.broadcasted_iota(jnp.int32, u.shape, 0)
    for r in range(n):
        out = jnp.where(row == ts - n + r, nxt[r:r + 1, :], out)
    return out


def _conv_fwd(P, conv_w, conv_b, *, C, name):
    S = P.shape[0]
    ts = _tile(S, 512, HALO)
    hb = ts // HALO

    def body(ab_ref, ac_ref, ax_ref, pc_ref, px_ref, w_ref, b_ref, o_ref):
        i = pl.program_id(0)
        u = ac_ref[...].astype(F32) * ax_ref[...].astype(F32)
        prev = pc_ref[...].astype(F32) * px_ref[...].astype(F32) * (i > 0).astype(F32)
        w = w_ref[...]
        y = b_ref[...] + w[0:1, :] * _shift_down(u, prev, 2) + w[1:2, :] * _shift_down(u, prev, 1) + w[2:3, :] * u
        o_ref[...] = (ab_ref[...].astype(F32) * y).astype(BF16)

    cur = lambda c: pl.BlockSpec((ts, C), lambda i: (i, c))
    prv = lambda c: pl.BlockSpec((HALO, C), lambda i: (jnp.maximum(i * hb - 1, 0), c))
    return pl.pallas_call(
        body, name=name, grid=(S // ts,),
        in_specs=[cur(0), cur(1), cur(2), prv(1), prv(2),
                  pl.BlockSpec((8, C), lambda i: (0, 0)), pl.BlockSpec((1, C), lambda i: (0, 0))],
        out_specs=pl.BlockSpec((ts, C), lambda i: (i, 0)),
        out_shape=jax.ShapeDtypeStruct((S, C), BF16),
        compiler_params=_params("parallel"),
    )(P, P, P, P, P, conv_w, conv_b)


def _conv_bwd(P, dya, conv_w, conv_b, *, C, name):
    S = P.shape[0]
    ts = _tile(S, 512, HALO)
    hb = ts // HALO
    last = S // HALO - 1
    n = S // ts

    def body(ab_ref, ac_ref, ax_ref, pc_ref, px_ref, dy_ref, nab_ref, ndy_ref, w_ref, b_ref, o_ref, dw_ref):
        i = pl.program_id(0)
        ab, ac, ax = ab_ref[...].astype(F32), ac_ref[...].astype(F32), ax_ref[...].astype(F32)
        u = ac * ax
        prev = pc_ref[...].astype(F32) * px_ref[...].astype(F32) * (i > 0).astype(F32)
        w = w_ref[...]
        u1, u2 = _shift_down(u, prev, 1), _shift_down(u, prev, 2)
        y = b_ref[...] + w[0:1, :] * u2 + w[1:2, :] * u1 + w[2:3, :] * u
        dya_v = dy_ref[...]
        dyp = dya_v * ab
        nxt = ndy_ref[...] * nab_ref[...].astype(F32) * (i < n - 1).astype(F32)
        du = w[2:3, :] * dyp + w[1:2, :] * _shift_up(dyp, nxt, 1) + w[0:1, :] * _shift_up(dyp, nxt, 2)
        o_ref[:, 0:C] = (dya_v * y).astype(BF16)
        o_ref[:, C:2 * C] = (du * ax).astype(BF16)
        o_ref[:, 2 * C:3 * C] = (du * ac).astype(BF16)
        part = jnp.concatenate([
            jnp.sum(dyp * u2, axis=0, keepdims=True), jnp.sum(dyp * u1, axis=0, keepdims=True),
            jnp.sum(dyp * u, axis=0, keepdims=True), jnp.sum(dyp, axis=0, keepdims=True),
            jnp.zeros((4, C), F32)], axis=0)

        @pl.when(i == 0)
        def _():
            dw_ref[...] = part

        @pl.when(i > 0)
        def _():
            dw_ref[...] += part

    cur = lambda c: pl.BlockSpec((ts, C), lambda i: (i, c))
    prv = lambda c: pl.BlockSpec((HALO, C), lambda i: (jnp.maximum(i * hb - 1, 0), c))
    nxt = lambda c: pl.BlockSpec((HALO, C), lambda i: (jnp.minimum((i + 1) * hb, last), c))
    return pl.pallas_call(
        body, name=name, grid=(n,),
        in_specs=[cur(0), cur(1), cur(2), prv(1), prv(2), cur(0), nxt(0), nxt(0),
                  pl.BlockSpec((8, C), lambda i: (0, 0)), pl.BlockSpec((1, C), lambda i: (0, 0))],
        out_specs=[pl.BlockSpec((ts, 3 * C), lambda i: (i, 0)), pl.BlockSpec((8, C), lambda i: (0, 0))],
        out_shape=[jax.ShapeDtypeStruct((S, 3 * C), BF16), jax.ShapeDtypeStruct((8, C), F32)],
        compiler_params=_params("arbitrary"),
    )(P, P, P, P, P, dya, P, dya, conv_w, conv_b)


def _sgu_fwd(P, ln_g, ln_b, wm, bT, *, W, cu, cv, name):
    S = P.shape[0]
    G, CH, _ = wm.shape
    gw = W // G
    ts = _tile(S, 512, CH)

    def body(u_ref, v_ref, g_ref, b_ref, wm_ref, bT_ref, o_ref):
        gv = _gelu(v_ref[...].astype(F32))
        mu = jnp.mean(gv, axis=1, keepdims=True)
        xc = gv - mu
        vn = (xc * lax.rsqrt(jnp.mean(xc * xc, axis=1, keepdims=True) + EPS) * g_ref[...] + b_ref[...]).astype(BF16)
        bT_v = bT_ref[...]
        for c in range(ts // CH):
            rows = slice(c * CH, (c + 1) * CH)
            for g in range(G):
                cols = slice(g * gw, (g + 1) * gw)
                mixed = jnp.dot(wm_ref[g], vn[rows, cols], preferred_element_type=F32) + bT_v[:, g:g + 1]
                o_ref[rows, cols] = (_gelu(u_ref[rows, cols].astype(F32)) * mixed).astype(BF16)

    full = lambda shp: pl.BlockSpec(shp, lambda i: (0,) * len(shp))
    return pl.pallas_call(
        body, name=name, grid=(S // ts,),
        in_specs=[pl.BlockSpec((ts, W), lambda i: (i, cu)), pl.BlockSpec((ts, W), lambda i: (i, cv)),
                  full((1, W)), full((1, W)), full((G, CH, CH)), full((CH, G))],
        out_specs=pl.BlockSpec((ts, W), lambda i: (i, 0)),
        out_shape=jax.ShapeDtypeStruct((S, W), BF16),
        compiler_params=_params("parallel"),
    )(P, P, ln_g, ln_b, wm, bT)


def _sgu_bwd(P, dyb, ln_g, ln_b, wm, wmT, bT, *, W, cu, cv, name):
    S = P.shape[0]
    G, CH, _ = wm.shape
    gw = W // G
    ts = _tile(S, 512, CH)

    def body(u_ref, v_ref, dy_ref, g_ref, b_ref, wm_ref, wmT_ref, bT_ref, o_ref, dw_ref, db_ref, dln_ref, dvn_ref):
        i = pl.program_id(0)

        @pl.when(i == 0)
        def _():
            dw_ref[...] = jnp.zeros_like(dw_ref)
            db_ref[...] = jnp.zeros_like(db_ref)
            dln_ref[...] = jnp.zeros_like(dln_ref)

        sv = v_ref[...].astype(F32)
        gv = _gelu(sv)
        mu = jnp.mean(gv, axis=1, keepdims=True)
        xc = gv - mu
        rstd = lax.rsqrt(jnp.mean(xc * xc, axis=1, keepdims=True) + EPS)
        xhat = xc * rstd
        lg = g_ref[...]
        vn = (xhat * lg + b_ref[...]).astype(BF16)
        bT_v = bT_ref[...]
        for c in range(ts // CH):
            rows = slice(c * CH, (c + 1) * CH)
            for g in range(G):
                cols = slice(g * gw, (g + 1) * gw)
                vn_cg = vn[rows, cols]
                mixed = jnp.dot(wm_ref[g], vn_cg, preferred_element_type=F32) + bT_v[:, g:g + 1]
                su = u_ref[rows, cols].astype(F32)
                dyv = dy_ref[rows, cols]
                dmix = dyv * _gelu(su)
                o_ref[rows, cols] = (dyv * mixed * _gelu_grad(su)).astype(BF16)
                dmix_b = dmix.astype(BF16)
                dw_ref[g] += lax.dot_general(dmix_b, vn_cg, (((1,), (1,)), ((), ())), preferred_element_type=F32)
                db_ref[g] += jnp.broadcast_to(jnp.sum(dmix, axis=1, keepdims=True), (CH, CH))
                dvn_ref[rows, cols] = jnp.dot(wmT_ref[g], dmix_b, preferred_element_type=F32)
        dvn = dvn_ref[...]
        dxh = dvn * lg
        dgv = rstd * (dxh - jnp.mean(dxh, axis=1, keepdims=True) - xhat * jnp.mean(dxh * xhat, axis=1, keepdims=True))
        o_ref[:, W:2 * W] = (dgv * _gelu_grad(sv)).astype(BF16)
        dln_ref[0:1, :] += jnp.sum(dvn * xhat, axis=0, keepdims=True)
        dln_ref[1:2, :] += jnp.sum(dvn, axis=0, keepdims=True)

    full = lambda shp: pl.BlockSpec(shp, lambda i: (0,) * len(shp))
    return pl.pallas_call(
        body, name=name, grid=(S // ts,),
        in_specs=[pl.BlockSpec((ts, W), lambda i: (i, cu)), pl.BlockSpec((ts, W), lambda i: (i, cv)),
                  pl.BlockSpec((ts, W), lambda i: (i, 0)),
                  full((1, W)), full((1, W)), full((G, CH, CH)), full((G, CH, CH)), full((CH, G))],
        out_specs=[pl.BlockSpec((ts, 2 * W), lambda i: (i, 0)), full((G, CH, CH)), full((G, CH, CH)), full((8, W))],
        out_shape=[jax.ShapeDtypeStruct((S, 2 * W), BF16), jax.ShapeDtypeStruct((G, CH, CH), F32),
                   jax.ShapeDtypeStruct((G, CH, CH), F32), jax.ShapeDtypeStruct((8, W), F32)],
        scratch_shapes=[pltpu.VMEM((ts, W), F32)],
        compiler_params=_params("arbitrary"),
    )(P, P, dyb, ln_g, ln_b, wm, wmT, bT)


def _block_sums(x, u, parts=1):
    hi = x.astype(BF16)
    out = jnp.dot(hi, u, preferred_element_type=F32)
    if parts == 2:
        lo = (x - hi.astype(F32)).astype(BF16)
        out = out + jnp.dot(lo, u, preferred_element_type=F32)
    return out


_NT = (((1,), (1,)), ((), ()))
_TN = (((0,), (0,)), ((), ()))


def _left_blocks(step, jd, carry, unroll, jd_multiple):
    rem = 0
    if jd_multiple % unroll:
        rem = jd % unroll
        carry = lax.fori_loop(0, rem, lambda t, c: step(jd - 1 - t, c, False), carry)

    def trip(t, c):
        for s in range(unroll):
            c = step(jd - rem - 1 - s - unroll * t, c, False)
        return c

    return lax.fori_loop(0, jd // unroll, trip, carry)


def _diag_step(step, j, carry, row0):
    if row0 == 0:
        return step(j, carry, True)
    tail = step(j, tuple(c[row0:] for c in carry), True, row0)
    return tuple(jnp.concatenate([c[:row0], t], axis=0) for c, t in zip(carry, tail))


def _qkv_heads_fwd(P, gq, gk, *, A, col0, hd, qscale, name):
    S = P.shape[0]
    H = A // hd
    ts = _tile(S, 512, 8)

    def body(q_ref, k_ref, v_ref, gq_ref, gk_ref, qn_ref, kn_ref, vh_ref):
        for h in range(H):
            cols = slice(h * hd, (h + 1) * hd)
            for x_ref, g_ref, sc, o_ref in ((q_ref, gq_ref, qscale, qn_ref), (k_ref, gk_ref, 1.0, kn_ref)):
                xh = x_ref[:, cols].astype(F32)
                r = lax.rsqrt(jnp.mean(xh * xh, axis=1, keepdims=True) + EPS)
                o_ref[h] = (xh * r * (g_ref[...] * sc)).astype(BF16)
            vh_ref[h] = v_ref[:, cols].astype(BF16)

    col = lambda n: pl.BlockSpec((ts, A), lambda i: (i, col0 + n))
    gsp = pl.BlockSpec((1, hd), lambda i: (0, 0))
    hsp = pl.BlockSpec((H, ts, hd), lambda i: (0, i, 0))
    shp = jax.ShapeDtypeStruct((H, S, hd), BF16)
    return pl.pallas_call(
        body, name=name, grid=(S // ts,), in_specs=[col(0), col(1), col(2), gsp, gsp],
        out_specs=[hsp, hsp, hsp], out_shape=[shp, shp, shp], compiler_params=_params("parallel"),
    )(P, P, P, gq, gk)


def _qkv_heads_bwd(P, gq, gk, dqn, dkn, dvh, *, A, col0, hd, qscale, name):
    S = P.shape[0]
    H = A // hd
    ts = _tile(S, 512, 8)

    def body(q_ref, k_ref, gq_ref, gk_ref, dq_ref, dk_ref, dv_ref, o_ref, dgq_ref, dgk_ref):
        i = pl.program_id(0)
        parts = [jnp.zeros((1, hd), F32), jnp.zeros((1, hd), F32)]
        for h in range(H):
            for n, (x_ref, g_ref, sc, d_ref) in enumerate(((q_ref, gq_ref, qscale, dq_ref), (k_ref, gk_ref, 1.0, dk_ref))):
                xh = x_ref[:, h * hd:(h + 1) * hd].astype(F32)
                dyv = d_ref[h] * sc
                r = lax.rsqrt(jnp.mean(xh * xh, axis=1, keepdims=True) + EPS)
                u = dyv * g_ref[...]
                dx = r * u - xh * (r * r * r * jnp.mean(u * xh, axis=1, keepdims=True))
                o_ref[:, n * A + h * hd:n * A + (h + 1) * hd] = dx.astype(BF16)
                parts[n] = parts[n] + jnp.sum(dyv * xh * r, axis=0, keepdims=True)
            o_ref[:, 2 * A + h * hd:2 * A + (h + 1) * hd] = dv_ref[h].astype(BF16)

        @pl.when(i == 0)
        def _():
            dgq_ref[...] = parts[0]
            dgk_ref[...] = parts[1]

        @pl.when(i > 0)
        def _():
            dgq_ref[...] += parts[0]
            dgk_ref[...] += parts[1]

    col = lambda n: pl.BlockSpec((ts, A), lambda i: (i, col0 + n))
    gsp = pl.BlockSpec((1, hd), lambda i: (0, 0))
    hsp = pl.BlockSpec((H, ts, hd), lambda i: (0, i, 0))
    return pl.pallas_call(
        body, name=name, grid=(S // ts,), in_specs=[col(0), col(1), gsp, gsp, hsp, hsp, hsp],
        out_specs=[pl.BlockSpec((ts, 3 * A), lambda i: (i, 0)), gsp, gsp],
        out_shape=[jax.ShapeDtypeStruct((S, 3 * A), BF16), jax.ShapeDtypeStruct((1, hd), F32),
                   jax.ShapeDtypeStruct((1, hd), F32)],
        compiler_params=_params("arbitrary"),
    )(P, P, gq, gk, dqn, dkn, dvh)


def _comm_begin(schedule, cin_refs, cout_refs, sems, grid):
    begin, end = _comm_phases(schedule, cin_refs, cout_refs, *sems)
    ids = [pl.program_id(d) for d in range(len(grid))]
    pl.when(functools.reduce(jnp.logical_and, [p == 0 for p in ids]))(begin)

    def comm_end():
        pl.when(functools.reduce(jnp.logical_and, [p == g - 1 for p, g in zip(ids, grid)]))(end)

    return comm_end


def _call_with_comm(body, name, grid, in_specs, out_specs, out_shape, args, comm):
    if comm is None:
        return pl.pallas_call(body, name=name, grid=grid, in_specs=in_specs, out_specs=out_specs, out_shape=out_shape,
                              compiler_params=_params("parallel", "arbitrary"))(*args)
    _, xs, cshapes = comm
    anyspec = pl.BlockSpec(memory_space=pl.ANY)
    return pl.pallas_call(
        body, name=name, grid=grid, in_specs=in_specs + [anyspec] * len(xs), out_specs=out_specs + [anyspec] * len(xs),
        out_shape=out_shape + [jax.ShapeDtypeStruct(c, x.dtype) for c, x in zip(cshapes, xs)],
        scratch_shapes=_comm_scratch(len(xs)), compiler_params=_params("arbitrary", "arbitrary"),
    )(*args, *xs)


def _attn_fwd(q, k, v, umat, *, tq, tk, name, comm=None):
    H, S, hd = q.shape

    def body(*refs):
        if comm is None:
            q_ref, k_ref, v_ref, u_ref, o_ref = refs
        else:
            nc = len(comm[1])
            q_ref, k_ref, v_ref, u_ref = refs[:4]
            o_ref = refs[4 + nc]
            comm_end = _comm_begin(comm[0], refs[4:4 + nc], refs[5 + nc:5 + 2 * nc], refs[5 + 2 * nc:], (H, S // tq))
        i = pl.program_id(1)
        qb = q_ref[0]
        um = u_ref[...]
        qpos = lax.broadcasted_iota(jnp.int32, (tq, tk), 0) + i * tq
        kloc = lax.broadcasted_iota(jnp.int32, (tq, tk), 1)

        def step(j, carry, masked, row0=0):
            r, acc = carry
            ks = pl.multiple_of(j * tk, tk)
            kb = k_ref[0, pl.ds(ks, tk), :]
            vb = v_ref[0, pl.ds(ks, tk), :]
            z = lax.dot_general(qb[row0:], kb, _NT, preferred_element_type=F32)
            lb = jnp.minimum(z, 0.0) - jnp.log(1.0 + jnp.exp(-jnp.abs(z)))
            lm = lb - z
            if masked:
                m = (kloc[row0:] + j * tk) < qpos[row0:]
                lm = jnp.where(m, lm, 0.0)
            a = jnp.exp(lb + _block_sums(lm, um) + r)
            if masked:
                a = jnp.where(m, a, 0.0)
            acc = acc + jnp.dot(a.astype(BF16), vb, preferred_element_type=F32)
            return r + jnp.sum(lm, axis=1, keepdims=True), acc

        jd = (i * tq) // tk
        carry = (jnp.zeros((tq, 1), F32), jnp.zeros((tq, hd), F32))
        for dd in reversed(range(max(1, tq // tk))):
            carry = _diag_step(step, jd + dd, carry, dd * tk if tq > tk else 0)
        carry = _left_blocks(step, jd, carry, ATTN_FWD_UNROLL, tq // tk)
        o_ref[0] = carry[1]
        if comm is not None:
            comm_end()

    blk = pl.BlockSpec((1, tq, hd), lambda h, i: (h, i, 0))
    whole = pl.BlockSpec((1, S, hd), lambda h, i: (h, 0, 0))
    in_specs = [blk, whole, whole, pl.BlockSpec((tk, tk), lambda h, i: (0, 0))]
    out_specs, out_shape = [blk], [jax.ShapeDtypeStruct((H, S, hd), F32)]
    res = _call_with_comm(body, name, (H, S // tq), in_specs, out_specs, out_shape, (q, k, v, umat), comm)
    return res[0] if comm is None else (res[0], res[1:])


def _attn_bwd(q, k, v, o, do, umat, *, tq, tk, name, comm=None):
    H, S, hd = q.shape

    def body(*refs):
        if comm is None:
            q_ref, k_ref, v_ref, o_ref, do_ref, u_ref, dq_ref, dk_ref, dv_ref = refs
        else:
            nc = len(comm[1])
            q_ref, k_ref, v_ref, o_ref, do_ref, u_ref = refs[:6]
            dq_ref, dk_ref, dv_ref = refs[6 + nc:9 + nc]
            comm_end = _comm_begin(comm[0], refs[6:6 + nc], refs[9 + nc:9 + 2 * nc], refs[9 + 2 * nc:], (H, S // tq))
        i = pl.program_id(1)

        @pl.when(i == 0)
        def _():
            dk_ref[...] = jnp.zeros_like(dk_ref)
            dv_ref[...] = jnp.zeros_like(dv_ref)

        qb = q_ref[0]
        do32 = do_ref[0]
        dob = do32.astype(BF16)
        tot = jnp.sum(dob.astype(F32) * o_ref[0], axis=1, keepdims=True)
        um = u_ref[...]
        qpos = lax.broadcasted_iota(jnp.int32, (tq, tk), 0) + i * tq
        kloc = lax.broadcasted_iota(jnp.int32, (tq, tk), 1)

        def step(j, carry, masked, row0=0):
            r, gs, dq = carry
            ks = pl.multiple_of(j * tk, tk)
            kb = k_ref[0, pl.ds(ks, tk), :]
            vb = v_ref[0, pl.ds(ks, tk), :]
            qs, dos = qb[row0:], dob[row0:]
            z = lax.dot_general(qs, kb, _NT, preferred_element_type=F32)
            lb = jnp.minimum(z, 0.0) - jnp.log(1.0 + jnp.exp(-jnp.abs(z)))
            sig = jnp.exp(lb)
            lm = lb - z
            if masked:
                m = (kloc[row0:] + j * tk) < qpos[row0:]
                lm = jnp.where(m, lm, 0.0)
            a = jnp.exp(lb + _block_sums(lm, um) + r)
            if masked:
                a = jnp.where(m, a, 0.0)
            ab = a.astype(BF16)
            g = lax.dot_general(dos, vb, _NT, preferred_element_type=F32) * ab.astype(F32)
            dz = g - sig * ((tot[row0:] - gs) - _block_sums(g, um, parts=2))
            if masked:
                dz = jnp.where(m, dz, 0.0)
            dzb = dz.astype(BF16)
            dq = dq + jnp.dot(dzb, kb, preferred_element_type=F32)
            dk_ref[0, pl.ds(ks, tk), :] += lax.dot_general(dzb, qs, _TN, preferred_element_type=F32)
            dv_ref[0, pl.ds(ks, tk), :] += lax.dot_general(ab, dos, _TN, preferred_element_type=F32)
            return r + jnp.sum(lm, axis=1, keepdims=True), gs + jnp.sum(g, axis=1, keepdims=True), dq

        jd = (i * tq) // tk
        zero = jnp.zeros((tq, 1), F32)
        carry = (zero, zero, jnp.zeros((tq, hd), F32))
        for dd in reversed(range(max(1, tq // tk))):
            carry = _diag_step(step, jd + dd, carry, dd * tk if tq > tk else 0)
        carry = _left_blocks(step, jd, carry, ATTN_BWD_UNROLL, tq // tk)
        dq_ref[0] = carry[2]
        if comm is not None:
            comm_end()

    blk = pl.BlockSpec((1, tq, hd), lambda h, i: (h, i, 0))
    whole = pl.BlockSpec((1, S, hd), lambda h, i: (h, 0, 0))
    shp = jax.ShapeDtypeStruct((H, S, hd), F32)
    in_specs = [blk, whole, whole, blk, blk, pl.BlockSpec((tk, tk), lambda h, i: (0, 0))]
    res = _call_with_comm(body, name, (H, S // tq), in_specs, [blk, whole, whole], [shp, shp, shp],
                          (q, k, v, o, do, umat), comm)
    return res if comm is None else (*res[:3], res[3:])


def _merge_fwd(ya, yb, o, wb, P, b_gate, *, gate_col0, name):
    S, C = ya.shape
    H, _, hd = o.shape
    D = wb.shape[2]
    ts = _tile(S, 512, 8)

    def body(y0, y1, oh_ref, wb_ref, g0, g1, g2, bg_ref, m_ref, yc_ref):
        for h in range(H):
            yc_ref[:, h * hd:(h + 1) * hd] = oh_ref[h].astype(BF16)
        acc = jnp.zeros((ts, D), F32)
        for n, (y_ref, g_ref) in enumerate(((y0, g0), (y1, g1), (yc_ref, g2))):
            yd = jnp.dot(y_ref[...], wb_ref[n], preferred_element_type=F32)
            acc = acc + _sigmoid(g_ref[...].astype(F32) + bg_ref[:, n * D:(n + 1) * D]) * yd
        m_ref[...] = acc.astype(BF16)

    ysp = pl.BlockSpec((ts, C), lambda i: (i, 0))
    gsp = lambda n: pl.BlockSpec((ts, D), lambda i: (i, gate_col0 + n))
    return pl.pallas_call(
        body, name=name, grid=(S // ts,),
        in_specs=[ysp, ysp, pl.BlockSpec((H, ts, hd), lambda i: (0, i, 0)), pl.BlockSpec((3, C, D), lambda i: (0, 0, 0)),
                  gsp(0), gsp(1), gsp(2), pl.BlockSpec((1, 3 * D), lambda i: (0, 0))],
        out_specs=[pl.BlockSpec((ts, D), lambda i: (i, 0)), ysp],
        out_shape=[jax.ShapeDtypeStruct((S, D), BF16), jax.ShapeDtypeStruct((S, C), BF16)],
        compiler_params=_params("parallel"),
    )(ya, yb, o, wb, P, P, P, b_gate)


def _merge_bwd(ys, wb, wbT, P, b_gate, dmerged, *, gate_col0, hd, name):
    S, C = ys[0].shape
    D = wb.shape[2]
    H = C // hd
    ts = _tile(S, 256, 8)

    def body(y0, y1, y2, wb_ref, wbT_ref, g0, g1, g2, bg_ref, dm_ref,
             dg_ref, dyd0, dyd1, dyd2, dya_ref, dyb_ref, do_ref, dbg_ref):
        i = pl.program_id(0)
        dm = dm_ref[...].astype(F32)
        parts = []
        for n, (y_ref, g_ref, dyd_ref) in enumerate(((y0, g0, dyd0), (y1, g1, dyd1), (y2, g2, dyd2))):
            yd = jnp.dot(y_ref[...], wb_ref[n], preferred_element_type=F32)
            sg = _sigmoid(g_ref[...].astype(F32) + bg_ref[:, n * D:(n + 1) * D])
            dgate = dm * yd * sg * (1.0 - sg)
            dg_ref[:, n * D:(n + 1) * D] = dgate.astype(BF16)
            parts.append(jnp.sum(dgate, axis=0, keepdims=True))
            dyd = (dm * sg).astype(BF16)
            dyd_ref[...] = dyd
            dy = jnp.dot(dyd, wbT_ref[n], preferred_element_type=F32)
            if n == 0:
                dya_ref[...] = dy
            elif n == 1:
                dyb_ref[...] = dy
            else:
                for h in range(H):
                    do_ref[h] = dy[:, h * hd:(h + 1) * hd]
        part = jnp.concatenate(parts, axis=1)

        @pl.when(i == 0)
        def _():
            dbg_ref[...] = part

        @pl.when(i > 0)
        def _():
            dbg_ref[...] += part

    ysp = pl.BlockSpec((ts, C), lambda i: (i, 0))
    dsp = pl.BlockSpec((ts, D), lambda i: (i, 0))
    gsp = lambda n: pl.BlockSpec((ts, D), lambda i: (i, gate_col0 + n))
    dshp = jax.ShapeDtypeStruct((S, D), BF16)
    yshp = jax.ShapeDtypeStruct((S, C), F32)
    return pl.pallas_call(
        body, name=name, grid=(S // ts,),
        in_specs=[ysp, ysp, ysp, pl.BlockSpec((3, C, D), lambda i: (0, 0, 0)),
                  pl.BlockSpec((3, D, C), lambda i: (0, 0, 0)), gsp(0), gsp(1), gsp(2),
                  pl.BlockSpec((1, 3 * D), lambda i: (0, 0)), dsp],
        out_specs=[pl.BlockSpec((ts, 3 * D), lambda i: (i, 0)), dsp, dsp, dsp, ysp, ysp,
                   pl.BlockSpec((H, ts, hd), lambda i: (0, i, 0)), pl.BlockSpec((1, 3 * D), lambda i: (0, 0))],
        out_shape=[jax.ShapeDtypeStruct((S, 3 * D), BF16), dshp, dshp, dshp, yshp, yshp,
                   jax.ShapeDtypeStruct((H, S, hd), F32), jax.ShapeDtypeStruct((1, 3 * D), F32)],
        compiler_params=_params("arbitrary"),
    )(*ys, wb, wbT, P, P, P, b_gate, dmerged)


def _swiglu_fwd(gu, *, name):
    S, F2 = gu.shape
    F = F2 // 2
    ts, tf = _tile(S, 512, 8), _tile(F, 1536, LANES)
    nf = F // tf

    def body(g_ref, u_ref, o_ref):
        gt = g_ref[...].astype(F32)
        o_ref[...] = (gt * _sigmoid(gt) * u_ref[...].astype(F32)).astype(BF16)

    return pl.pallas_call(
        body, name=name, grid=(S // ts, nf),
        in_specs=[pl.BlockSpec((ts, tf), lambda i, j: (i, j)), pl.BlockSpec((ts, tf), lambda i, j: (i, j + nf))],
        out_specs=pl.BlockSpec((ts, tf), lambda i, j: (i, j)),
        out_shape=jax.ShapeDtypeStruct((S, F), BF16),
        compiler_params=_params("parallel", "parallel"),
    )(gu, gu)


def _swiglu_bwd(gu, dact, *, name):
    S, F2 = gu.shape
    F = F2 // 2
    ts = _tile(S, 256, 8)

    def body(gu_ref, d_ref, o_ref):
        gt, up, da = gu_ref[:, 0:F].astype(F32), gu_ref[:, F:F2].astype(F32), d_ref[...].astype(F32)
        sg = _sigmoid(gt)
        o_ref[:, 0:F] = (da * up * sg * (1.0 + gt * (1.0 - sg))).astype(BF16)
        o_ref[:, F:F2] = (da * gt * sg).astype(BF16)

    return pl.pallas_call(
        body, name=name, grid=(S // ts,),
        in_specs=[pl.BlockSpec((ts, F2), lambda i: (i, 0)), pl.BlockSpec((ts, F), lambda i: (i, 0))],
        out_specs=pl.BlockSpec((ts, F2), lambda i: (i, 0)),
        out_shape=jax.ShapeDtypeStruct((S, F2), BF16),
        compiler_params=_params("parallel"),
    )(gu, dact)


def _loss_grad(y, target, *, name):
    S, D = y.shape
    ts = _tile(S, 512, 8)

    def body(y_ref, t_ref, dy_ref, l_ref, dyb_ref):
        i = pl.program_id(0)
        err = y_ref[...] - t_ref[...]
        dy_ref[...] = err * (1.0 / D)
        dyb_ref[...] = (err * (1.0 / D)).astype(BF16)
        part = jnp.broadcast_to(jnp.sum(jnp.sum(err * err, axis=1, keepdims=True), axis=0, keepdims=True) * (0.5 / D),
                                (1, LANES))

        @pl.when(i == 0)
        def _():
            l_ref[...] = part

        @pl.when(i > 0)
        def _():
            l_ref[...] += part

    row = pl.BlockSpec((ts, D), lambda i: (i, 0))
    return pl.pallas_call(
        body, name=name, grid=(S // ts,), in_specs=[row, row],
        out_specs=[row, pl.BlockSpec((1, LANES), lambda i: (0, 0)), row],
        out_shape=[jax.ShapeDtypeStruct((S, D), F32), jax.ShapeDtypeStruct((1, LANES), F32),
                   jax.ShapeDtypeStruct((S, D), BF16)],
        compiler_params=_params("arbitrary"),
    )(y, target)


def _adamw(w, m, v, gs_list, *, name):
    R, W = w.shape
    rows = [g.shape[1] for g in gs_list]
    tr = _tile(math.gcd(*rows), max(16, (2048 * LANES // W) // 16 * 16), 16)
    first = [sum(rows[:t]) // tr for t in range(len(rows))]
    c1 = 1.0 / (1.0 - ADAM_B1 ** ADAM_STEP)
    c2 = 1.0 / (1.0 - ADAM_B2 ** ADAM_STEP)

    def body(w_ref, m_ref, v_ref, *refs):
        gs_refs, (g_ref, d_ref, nm_ref, nv_ref) = refs[:len(rows)], refs[len(rows):]
        i = pl.program_id(0)
        g = None
        for t, gs_ref in enumerate(gs_refs):
            gt = gs_ref[0].astype(F32)
            for s in range(1, gs_ref.shape[0]):
                gt = gt + gs_ref[s].astype(F32)
            g = gt if g is None else jnp.where(i >= first[t], gt, g)
        nm = ADAM_B1 * m_ref[...] + (1.0 - ADAM_B1) * g
        nv = ADAM_B2 * v_ref[...] + (1.0 - ADAM_B2) * (g * g)
        g_ref[...] = g
        nm_ref[...] = nm
        nv_ref[...] = nv
        d_ref[...] = -ADAM_LR * ((nm * c1) / (jnp.sqrt(nv * c2) + ADAM_EPS) + ADAM_WD * w_ref[...])

    row = pl.BlockSpec((tr, W), lambda i: (i, 0))
    slots = [pl.BlockSpec((g.shape[0], tr, W),
                          lambda i, b0=first[t], nb=rows[t] // tr: (0, jnp.clip(i - b0, 0, nb - 1), 0))
             for t, g in enumerate(gs_list)]
    shp = jax.ShapeDtypeStruct((R, W), F32)
    return pl.pallas_call(
        body, name=name, grid=(R // tr,), in_specs=[row, row, row] + slots,
        out_specs=[row, row, row, row], out_shape=[shp, shp, shp, shp],
        compiler_params=_params("parallel"),
    )(w, m, v, *gs_list)


def _slot_sum(gs, *, name):
    ns, R, _ = gs.shape
    tr = _tile(R, 2048, 16)

    def body(gs_ref, o_ref):
        g = gs_ref[0]
        for s in range(1, ns):
            g = g + gs_ref[s]
        o_ref[...] = g

    return pl.pallas_call(
        body, name=name, grid=(R // tr,),
        in_specs=[pl.BlockSpec((ns, tr, LANES), lambda i: (0, i, 0))],
        out_specs=pl.BlockSpec((tr, LANES), lambda i: (i, 0)),
        out_shape=jax.ShapeDtypeStruct((R, LANES), F32),
        compiler_params=_params("parallel"),
    )(gs)


def _comm_scratch(n):
    return [pltpu.SemaphoreType.DMA((7 * n,)), pltpu.SemaphoreType.DMA((7 * n,)), pltpu.SemaphoreType.DMA((n,))]


def _gather_schedule(x_ref, out_ref, send_sems, recv_sems, local_sem, base):
    x, y, c = lax.axis_index("x"), lax.axis_index("y"), lax.axis_index("c")
    me, sibling = (x, y, c), (x, y, 1 - c)
    chips = [(1 - x, y), (x, 1 - y), (1 - x, 1 - y)]

    def slot(px, py, pc):
        return out_ref.at[4 * px + 2 * py + pc]

    def copy(k, block, to, src=None):
        return pltpu.make_async_remote_copy(
            src_ref=slot(*block) if src is None else src, dst_ref=slot(*block),
            send_sem=send_sems.at[base + k], recv_sem=recv_sems.at[base + k], device_id=to, device_id_type=MESH)

    mine = pltpu.make_async_copy(x_ref, slot(*me), local_sem)
    first = [copy(0, me, sibling, src=x_ref)]
    first += [copy(1 + j, me, (*chip, c), src=x_ref) for j, chip in enumerate(chips)]
    passed = [copy(4 + j, (*chip, c), sibling) for j, chip in enumerate(chips)]

    def begin():
        mine.start()
        for cp in first:
            cp.start()

    def forward():
        for j, chip in enumerate(chips):
            copy(1 + j, (*chip, c), me).wait_recv()
            passed[j].start()

    def finish():
        copy(0, sibling, me).wait_recv()
        for j, chip in enumerate(chips):
            copy(4 + j, (*chip, 1 - c), me).wait_recv()
        for cp in first + passed:
            cp.wait_send()
        mine.wait()

    return begin, forward, finish


def _exchange_schedule(x_ref, out_ref, send_sems, recv_sems, local_sem, base):
    x, y, c = lax.axis_index("x"), lax.axis_index("y"), lax.axis_index("c")
    me = 4 * x + 2 * y + c
    mine = pltpu.make_async_copy(x_ref.at[me], out_ref.at[me], local_sem)
    sends, recvs = [], []
    for k in range(1, N_DEV):
        px = 1 - x if k & 4 else x
        py = 1 - y if k & 2 else y
        pc = 1 - c if k & 1 else c
        peer = 4 * px + 2 * py + pc
        sems = dict(send_sem=send_sems.at[base + k - 1], recv_sem=recv_sems.at[base + k - 1],
                    device_id=(px, py, pc), device_id_type=MESH)
        sends.append(pltpu.make_async_remote_copy(src_ref=x_ref.at[peer], dst_ref=out_ref.at[me], **sems))
        recvs.append(pltpu.make_async_remote_copy(src_ref=x_ref.at[me], dst_ref=out_ref.at[peer], **sems))

    def begin():
        mine.start()
        for cp in sends:
            cp.start()

    def forward():
        pass

    def finish():
        for cp in recvs:
            cp.wait_recv()
        for cp in sends:
            cp.wait_send()
        mine.wait()

    return begin, forward, finish


def _comm_phases(schedule, x_refs, out_refs, send_sems, recv_sems, local_sems):
    parts = [schedule(x, o, send_sems, recv_sems, local_sems.at[t], 7 * t)
             for t, (x, o) in enumerate(zip(x_refs, out_refs))]

    def begin():
        for b, _, _ in parts:
            b()

    def end():
        for _, fw, _ in parts:
            fw()
        for _, _, fin in parts:
            fin()

    return begin, end


def _all_gather(xs, *, name):
    def body(x_ref, out_ref, send_sems, recv_sems, local_sems):
        begin, end = _comm_phases(_gather_schedule, [x_ref], [out_ref], send_sems, recv_sems, local_sems)
        begin()
        end()

    return pl.pallas_call(
        body, name=name,
        in_specs=[pl.BlockSpec(memory_space=pl.ANY)], out_specs=pl.BlockSpec(memory_space=pl.ANY),
        out_shape=jax.ShapeDtypeStruct((N_DEV,) + xs.shape, xs.dtype), scratch_shapes=_comm_scratch(1),
        compiler_params=pltpu.CompilerParams(has_side_effects=True),
    )(xs)


PACK_ROWS = 16


def _pack_rows(parts, dtype, lead=()):
    rows = []
    for p in parts:
        r = p.reshape(lead + (-1, LANES)).astype(dtype)
        pad = (-r.shape[-2]) % PACK_ROWS
        rows.append(jnp.pad(r, [(0, 0)] * len(lead) + [(0, pad), (0, 0)]) if pad else r)
    return jnp.concatenate(rows, axis=len(lead))


def _unpack_rows(packed, shapes, lead=()):
    out, off = [], 0
    for shp in shapes:
        n = math.prod(shp) // LANES
        out.append(lax.slice_in_dim(packed, off, off + n, axis=len(lead)).reshape(lead + tuple(shp)))
        off += n + (-n) % PACK_ROWS
    return out


def _unshard(gathered, axis):
    g = jnp.moveaxis(gathered, 0, axis)
    shp = list(g.shape)
    shp[axis:axis + 2] = [shp[axis] * shp[axis + 1]]
    return g.reshape(shp)


def _reshard(full, axis):
    shp = list(full.shape)
    shp[axis:axis + 1] = [N_DEV, shp[axis] // N_DEV]
    return jnp.moveaxis(full.reshape(shp), axis, 0)


def _late_parts(L):
    return [(k, l) for k in BIG for l in range(L) if (k, l) != ("w_in", 0)]


def _late_groups(L, shard_shape):
    groups = {}
    for k, l in _late_parts(L):
        groups.setdefault(shard_shape[k][-1], []).append((k, l))
    return groups


def _local_step(x, target, w_in0, late_local, groups, shard_shape, sm):
    S, D = x.shape
    L = sm["mix_norm_g"].shape[0]
    wf = {k: [None] * L for k in BIG}
    wf["w_in"][0] = w_in0
    C = sm["conv_b"].shape[1]
    W = sm["sgu_ln_g"].shape[1]
    hd = sm["q_norm_g"].shape[1]
    G, CH = sm["sgu_w"].shape[1], sm["sgu_w"].shape[2]
    A = w_in0.shape[1] - (3 * C + 2 * W + 3 * D)
    A = A // 3
    H = A // hd
    col_q = 3 * C + 2 * W
    qscale = 1.0 / math.sqrt(hd)
    tril = jnp.tril(jnp.ones((CH, CH), F32))
    (ftq, ftk), (btq, btk) = [(_tile(S, a, LANES), _tile(S, b, LANES)) for a, b in (ATTN_FWD_TILES, ATTN_BWD_TILES)]
    umat = lambda t: (lax.broadcasted_iota(jnp.int32, (t, t), 0) > lax.broadcasted_iota(jnp.int32, (t, t), 1)).astype(BF16)

    saved = []
    for l in range(L):
        n = f"l{l}_"
        g1 = sm["mix_norm_g"][l][None]
        h = _rmsnorm_fwd(x, g1, scale=1.0, out_dtype=BF16, name=n + "mixnorm")
        P = _matmul(h, wf["w_in"][l], name=n + "w_in", out_dtype=BF16)
        cw = jnp.pad(sm["conv_w"][l], ((0, 5), (0, 0)))
        cb = sm["conv_b"][l][None]
        ya = _conv_fwd(P, cw, cb, C=C, name=n + "conv")
        wm = (sm["sgu_w"][l] * tril).astype(BF16)
        bT = sm["sgu_b"][l].T
        lng, lnb = sm["sgu_ln_g"][l][None], sm["sgu_ln_b"][l][None]
        yb = _sgu_fwd(P, lng, lnb, wm, bT, W=W, cu=(3 * C) // W, cv=(3 * C) // W + 1, name=n + "sgu")
        gq, gk = sm["q_norm_g"][l][None], sm["k_norm_g"][l][None]
        qn, kn, vh = _qkv_heads_fwd(P, gq, gk, A=A, col0=col_q // A, hd=hd, qscale=qscale, name=n + "qkv")
        if l == 0:
            o, gathered = _attn_fwd(qn, kn, vh, umat(ftk), tq=ftq, tk=ftk, name=n + "attn",
                                    comm=(_gather_schedule, late_local, [(N_DEV,) + a.shape for a in late_local]))
            for (width, parts), g in zip(groups.items(), gathered):
                r = 0
                for k, ll in parts:
                    rows = math.prod(shard_shape[k]) // width
                    part = lax.slice_in_dim(g, r, r + rows, axis=1).reshape((N_DEV,) + tuple(shard_shape[k]))
                    wf[k][ll] = _unshard(part, SHARD_AXIS[k] - 1)
                    r += rows
        else:
            o = _attn_fwd(qn, kn, vh, umat(ftk), tq=ftq, tk=ftk, name=n + "attn")
        bg = sm["b_gate"][l][None]
        gate_col0 = (col_q + 3 * A) // D
        merged, yc = _merge_fwd(ya, yb, o, wf["w_branch_out"][l], P, bg, gate_col0=gate_col0, name=n + "merge")
        x1 = _matmul(merged, wf["w_o"][l], res=x, name=n + "w_o")
        g2 = sm["ffn_norm_g"][l][None]
        h2 = _rmsnorm_fwd(x1, g2, scale=1.0, out_dtype=BF16, name=n + "ffnnorm")
        gu = _matmul(h2, wf["w_gate_up"][l], name=n + "w_gate_up", out_dtype=BF16)
        act = _swiglu_fwd(gu, name=n + "swiglu")
        x2 = _matmul(act, wf["w_down"][l], res=x1, name=n + "w_down")
        saved.append(dict(x=x, h=h, P=P, cw=cw, cb=cb, ya=ya, wm=wm, bT=bT, lng=lng, lnb=lnb, yb=yb,
                          gq=gq, gk=gk, qn=qn, kn=kn, vh=vh, o=o, yc=yc, bg=bg,
                          gate_col0=gate_col0, merged=merged, x1=x1, g1=g1, g2=g2, h2=h2, gu=gu, act=act))
        x = x2

    dx, lpart, dxb = _loss_grad(x, target, name="loss")
    grads = {k: [None] * L for k in WEIGHTS}
    chunked = lambda k, g: g if k == "w_in" else _reshard(g, SHARD_AXIS[k] - 1)
    for l in reversed(range(L)):
        n = f"l{l}_b_"
        sv = saved[l]
        grads["w_down"][l] = _matmul_tn(sv["act"], dxb, name=n + "g_w_down", out_dtype=BF16)
        dact = _matmul(dxb, wf["w_down"][l].T, name=n + "d_act", out_dtype=BF16)
        dgu = _swiglu_bwd(sv["gu"], dact, name=n + "swiglu")
        grads["w_gate_up"][l] = _matmul_tn(sv["h2"], dgu, name=n + "g_w_gate_up", out_dtype=BF16)
        dh2 = _matmul(dgu, wf["w_gate_up"][l].T, name=n + "d_h2", out_dtype=BF16)
        dx1, dg2, dx1b = _rmsnorm_bwd(sv["x1"], sv["g2"], dh2, scale=1.0, dres=dx, name=n + "ffnnorm",
                                      bf16_copy=True)
        grads["ffn_norm_g"][l] = dg2[0]
        grads["w_o"][l] = _matmul_tn(sv["merged"], dx1b, name=n + "g_w_o", out_dtype=BF16)
        dmerged = _matmul(dx1b, wf["w_o"][l].T, name=n + "d_merged", out_dtype=BF16)
        ys = (sv["ya"], sv["yb"], sv["yc"])
        wb = wf["w_branch_out"][l]
        dgates, *dyd, dya, dyb, do, dbg = _merge_bwd(ys, wb, wb.transpose(0, 2, 1), sv["P"], sv["bg"], dmerged,
                                                     gate_col0=sv["gate_col0"], hd=hd, name=n + "merge")
        grads["b_gate"][l] = dbg[0]
        grads["w_branch_out"][l] = jnp.stack(
            [_matmul_tn(ys[i], dyd[i], name=n + f"g_w_branch{i}", out_dtype=BF16) for i in range(3)])
        dconv, dcw = _conv_bwd(sv["P"], dya, sv["cw"], sv["cb"], C=C, name=n + "conv")
        grads["conv_w"][l], grads["conv_b"][l] = dcw[0:3], dcw[3]
        wmT = sv["wm"].transpose(0, 2, 1)
        dsgu, dsw, dsb, dln = _sgu_bwd(sv["P"], dyb, sv["lng"], sv["lnb"], sv["wm"], wmT, sv["bT"], W=W,
                                       cu=(3 * C) // W, cv=(3 * C) // W + 1, name=n + "sgu")
        grads["sgu_w"][l], grads["sgu_b"][l] = dsw * tril, dsb[:, :, 0]
        grads["sgu_ln_g"][l], grads["sgu_ln_b"][l] = dln[0], dln[1]
        if l == 0:
            chunks = [jnp.concatenate([chunked(k, grads[k][ll]).reshape(N_DEV, -1, width) for k, ll in parts], axis=1)
                      for width, parts in groups.items()]
            dqn, dkn, dvh, late_recv = _attn_bwd(sv["qn"], sv["kn"], sv["vh"], sv["o"], do, umat(btk), tq=btq, tk=btk,
                                                 name=n + "attn",
                                                 comm=(_exchange_schedule, chunks, [c.shape for c in chunks]))
        else:
            dqn, dkn, dvh = _attn_bwd(sv["qn"], sv["kn"], sv["vh"], sv["o"], do, umat(btk), tq=btq, tk=btk,
                                      name=n + "attn")
        dqkv, dgq, dgk = _qkv_heads_bwd(sv["P"], sv["gq"], sv["gk"], dqn, dkn, dvh, A=A, col0=col_q // A, hd=hd,
                                        qscale=qscale, name=n + "qkv")
        grads["q_norm_g"][l], grads["k_norm_g"][l] = dgq[0], dgk[0]
        dP = jnp.concatenate([dconv, dsgu, dqkv, dgates], axis=1)
        grads["w_in"][l] = _matmul_tn(sv["h"], dP, name=n + "g_w_in", out_dtype=BF16, shard=dP.shape[1] // N_DEV)
        if l == 0:
            dh, early_recv = _matmul(dP, wf["w_in"][l].T, name=n + "d_h", out_dtype=BF16,
                                     comm=(_exchange_schedule, [grads["w_in"][0]], [grads["w_in"][0].shape]))
        else:
            dh = _matmul(dP, wf["w_in"][l].T, name=n + "d_h", out_dtype=BF16)
        dx, dg1, dxb = _rmsnorm_bwd(sv["x"], sv["g1"], dh, scale=1.0, dres=dx1, name=n + "mixnorm", bf16_copy=True)
        grads["mix_norm_g"][l] = dg1[0]
    return lpart[0, 0], dx, grads, early_recv, late_recv


def kernel(x, mix_norm_g, w_in, b_gate, conv_w, conv_b, sgu_ln_g, sgu_ln_b, sgu_w, sgu_b, q_norm_g, k_norm_g, w_branch_out, w_o, ffn_norm_g, w_gate_up, w_down, loss_target, m_mix_norm_g, m_w_in, m_b_gate, m_conv_w, m_conv_b, m_sgu_ln_g, m_sgu_ln_b, m_sgu_w, m_sgu_b, m_q_norm_g, m_k_norm_g, m_w_branch_out, m_w_o, m_ffn_norm_g, m_w_gate_up, m_w_down, v_mix_norm_g, v_w_in, v_b_gate, v_conv_w, v_conv_b, v_sgu_ln_g, v_sgu_ln_b, v_sgu_w, v_sgu_b, v_q_norm_g, v_k_norm_g, v_w_branch_out, v_w_o, v_ffn_norm_g, v_w_gate_up, v_w_down):
    w = dict(mix_norm_g=mix_norm_g, w_in=w_in, b_gate=b_gate, conv_w=conv_w, conv_b=conv_b, sgu_ln_g=sgu_ln_g,
             sgu_ln_b=sgu_ln_b, sgu_w=sgu_w, sgu_b=sgu_b, q_norm_g=q_norm_g, k_norm_g=k_norm_g,
             w_branch_out=w_branch_out, w_o=w_o, ffn_norm_g=ffn_norm_g, w_gate_up=w_gate_up, w_down=w_down)
    m = dict(mix_norm_g=m_mix_norm_g, w_in=m_w_in, b_gate=m_b_gate, conv_w=m_conv_w, conv_b=m_conv_b,
             sgu_ln_g=m_sgu_ln_g, sgu_ln_b=m_sgu_ln_b, sgu_w=m_sgu_w, sgu_b=m_sgu_b, q_norm_g=m_q_norm_g,
             k_norm_g=m_k_norm_g, w_branch_out=m_w_branch_out, w_o=m_w_o, ffn_norm_g=m_ffn_norm_g,
             w_gate_up=m_w_gate_up, w_down=m_w_down)
    v = dict(mix_norm_g=v_mix_norm_g, w_in=v_w_in, b_gate=v_b_gate, conv_w=v_conv_w, conv_b=v_conv_b,
             sgu_ln_g=v_sgu_ln_g, sgu_ln_b=v_sgu_ln_b, sgu_w=v_sgu_w, sgu_b=v_sgu_b, q_norm_g=v_q_norm_g,
             k_norm_g=v_k_norm_g, w_branch_out=v_w_branch_out, w_o=v_w_o, ffn_norm_g=v_ffn_norm_g,
             w_gate_up=v_w_gate_up, w_down=v_w_down)
    me = 4 * lax.axis_index("x") + 2 * lax.axis_index("y") + lax.axis_index("c")
    S = x.shape[1]

    L = w_in.shape[0]
    two_d = lambda a: a.reshape(-1, a.shape[-1])
    shard_shape = {k: w[k].shape[1:] for k in BIG}
    groups = _late_groups(L, shard_shape)
    w_in0 = _unshard(_all_gather(w_in[0].astype(BF16), name="gather_w_in0"), SHARD_AXIS["w_in"] - 1)
    late_local = [jnp.concatenate([two_d(w[k][l]).astype(BF16) for k, l in parts], axis=0) for parts in groups.values()]
    conv_g = _all_gather(_pack_rows([conv_w], F32), name="gather_conv_w")
    sm = {k: w[k] for k in SMALL}
    sm["conv_w"] = _unshard(_unpack_rows(conv_g, [conv_w.shape], lead=(N_DEV,))[0], 2)

    lpart, dx, grads, early_recv, late_recv = _local_step(x[0], loss_target[0], w_in0, late_local, groups,
                                                          shard_shape, sm)
    loss = lax.psum(lpart, ("x", "y", "c"))

    where = {("w_in", 0): (early_recv[0], 0)}
    for (width, parts), recv in zip(groups.items(), late_recv):
        r = 0
        for k, l in parts:
            where[k, l] = (recv, r)
            r += math.prod(shard_shape[k]) // width
    out = {}
    for k in BIG:
        rows = math.prod(shard_shape[k]) // shard_shape[k][-1]
        gs_list = []
        for l in range(L):
            recv, r = where[k, l]
            if gs_list and gs_list[-1][0] is recv and gs_list[-1][2] == r:
                gs_list[-1][2] = r + rows
            else:
                gs_list.append([recv, r, r + rows])
        gs_list = [recv if (r0, r1) == (0, recv.shape[1]) else lax.slice_in_dim(recv, r0, r1, axis=1)
                   for recv, r0, r1 in gs_list]
        res = _adamw(two_d(w[k]), two_d(m[k]), two_d(v[k]), gs_list, name="adamw_" + k)
        for nm, r in zip(("grad", "delta", "new_m", "new_v"), res):
            out[nm, k] = r.reshape(w[k].shape)

    small_grads = [jnp.stack(grads[k]) for k in SMALL]
    small_shapes = [g.shape for g in small_grads]
    sg = _all_gather(_pack_rows(small_grads, F32), name="gather_small_grads")
    gsum = _slot_sum(sg, name="sum_small_grads")
    gsmall = dict(zip(SMALL, _unpack_rows(gsum, small_shapes)))
    cshard = conv_w.shape[2]
    gsmall["conv_w"] = lax.dynamic_slice_in_dim(gsmall["conv_w"], me * cshard, cshard, axis=2)
    own_shapes = [w[k].shape for k in SMALL]
    gs_, ds_, ms_, vs_ = _adamw(_pack_rows([w[k] for k in SMALL], F32), _pack_rows([m[k] for k in SMALL], F32),
                                _pack_rows([v[k] for k in SMALL], F32),
                                [_pack_rows([gsmall[k] for k in SMALL], F32)[None]], name="adamw_small")
    for nm, packed in (("grad", gs_), ("delta", ds_), ("new_m", ms_), ("new_v", vs_)):
        for k, a in zip(SMALL, _unpack_rows(packed, own_shapes)):
            out[nm, k] = a

    res = [loss, dx[None]]
    for nm in ("grad", "delta", "new_m", "new_v"):
        res += [out[nm, k] for k in WEIGHTS]
    return tuple(res)
```

```python
import functools
import math

import jax
import jax.numpy as jnp
from jax import lax
from jax.experimental import pallas as pl
from jax.experimental.pallas import tpu as pltpu

F32 = jnp.float32
BF16 = jnp.bfloat16
MESH = pl.DeviceIdType.MESH

N_DEV = 8
LANES = 128
VMEM_LIMIT_BYTES = 56 * 1024 * 1024
EPS = 1e-6
ADAM_LR, ADAM_B1, ADAM_B2, ADAM_EPS, ADAM_WD, ADAM_STEP = 0.001, 0.9, 0.999, 1e-08, 0.01, 10
ATTN_FWD_TILES = (512, 512)
ATTN_BWD_TILES = (1024, 256)
ATTN_FWD_UNROLL = 4
ATTN_BWD_UNROLL = 4
BIG = ("w_in", "w_branch_out", "w_o", "w_gate_up", "w_down")
SMALL = ("mix_norm_g", "b_gate", "conv_w", "conv_b", "sgu_ln_g", "sgu_ln_b", "sgu_w", "sgu_b",
         "q_norm_g", "k_norm_g", "ffn_norm_g")
WEIGHTS = ("mix_norm_g", "w_in", "b_gate", "conv_w", "conv_b", "sgu_ln_g", "sgu_ln_b", "sgu_w", "sgu_b",
           "q_norm_g", "k_norm_g", "w_branch_out", "w_o", "ffn_norm_g", "w_gate_up", "w_down")
SHARD_AXIS = {"w_in": 2, "w_branch_out": 3, "w_o": 1, "w_gate_up": 2, "w_down": 1}


def _tile(n, cap, mult):
    best = None
    for t in range(mult, min(n, cap) + 1, mult):
        if n % t == 0:
            best = t
    return best if best is not None else n


def _params(*sem):
    return pltpu.CompilerParams(dimension_semantics=sem if sem else None, vmem_limit_bytes=VMEM_LIMIT_BYTES)


def _erf(x):
    return lax.erf(x)


def _gelu(x):
    return 0.5 * x * (1.0 + _erf(x * (1.0 / math.sqrt(2.0))))


def _gelu_grad(x):
    return 0.5 * (1.0 + _erf(x * (1.0 / math.sqrt(2.0)))) + x * jnp.exp(-0.5 * x * x) * (1.0 / math.sqrt(2.0 * math.pi))


def _sigmoid(x):
    return 1.0 / (1.0 + jnp.exp(-x))


def _matmul(a, b, *, name, res=None, out_dtype=F32, comm=None):
    M, K = a.shape
    _, N = b.shape
    tm, tn, tk = _tile(M, 1024, 8), _tile(N, 1536, LANES), _tile(K, 1536, LANES)
    nk = K // tk
    grid = (M // tm, N // tn, nk)
    has_res = res is not None

    def body(*refs):
        refs = list(refs)
        a_ref, b_ref = refs[:2]
        r_ref = refs[2] if has_res else None
        pos = 2 + has_res
        nc = len(comm[1]) if comm is not None else 0
        o_ref = refs[pos + nc]
        acc = refs[pos + 1 + 2 * nc] if nk > 1 else None
        if comm is not None:
            comm_end = _comm_begin(comm[0], refs[pos:pos + nc], refs[pos + nc + 1:pos + 2 * nc + 1],
                                   refs[pos + 2 * nc + 1 + (nk > 1):], grid)
        k = pl.program_id(2)
        part = jnp.dot(a_ref[...], b_ref[...], preferred_element_type=F32)

        def finish(v):
            if has_res:
                v = v + r_ref[...]
            o_ref[...] = v.astype(out_dtype)

        if nk == 1:
            finish(part)
        else:
            @pl.when(k == 0)
            def _():
                acc[...] = part

            @pl.when(jnp.logical_and(k > 0, k < nk - 1))
            def _():
                acc[...] += part

            @pl.when(k == nk - 1)
            def _():
                finish(acc[...] + part)

        if comm is not None:
            comm_end()

    in_specs = [pl.BlockSpec((tm, tk), lambda i, j, k: (i, k)), pl.BlockSpec((tk, tn), lambda i, j, k: (k, j))]
    args = [a, b]
    if has_res:
        in_specs.append(pl.BlockSpec((tm, tn), lambda i, j, k: (i, j)))
        args.append(res)
    out_specs = [pl.BlockSpec((tm, tn), lambda i, j, k: (i, j))]
    out_shape = [jax.ShapeDtypeStruct((M, N), out_dtype)]
    scratch = [pltpu.VMEM((tm, tn), F32)] if nk > 1 else []
    sem = ("parallel", "parallel", "arbitrary")
    if comm is not None:
        anyspec = pl.BlockSpec(memory_space=pl.ANY)
        in_specs += [anyspec] * len(comm[1])
        args += list(comm[1])
        out_specs += [anyspec] * len(comm[1])
        out_shape += [jax.ShapeDtypeStruct(c, x.dtype) for c, x in zip(comm[2], comm[1])]
        scratch += _comm_scratch(len(comm[1]))
        sem = ("arbitrary",) * 3
    out = pl.pallas_call(body, name=name, grid=grid, in_specs=in_specs, out_specs=out_specs, out_shape=out_shape,
                         scratch_shapes=scratch, compiler_params=_params(*sem))(*args)
    return out[0] if comm is None else (out[0], out[1:])


def _matmul_tn(x, y, *, name, out_dtype=F32, shard=None):
    S, A = x.shape
    _, B = y.shape
    ta, ts = _tile(A, 1536, LANES), _tile(S, 1024, 8)
    tb = shard if shard else _tile(B, 1536, LANES)
    ns = S // ts
    direct = out_dtype == F32
    view = (lambda r: r.at[0]) if shard else (lambda r: r)

    def body(x_ref, y_ref, o_ref, *scratch):
        s = pl.program_id(2)
        out = view(o_ref)
        acc = out if direct else scratch[0]
        part = lax.dot_general(x_ref[...], y_ref[...], (((0,), (0,)), ((), ())), preferred_element_type=F32)

        @pl.when(s == 0)
        def _():
            acc[...] = part

        @pl.when(s > 0)
        def _():
            acc[...] += part

        if not direct:
            @pl.when(s == ns - 1)
            def _():
                out[...] = acc[...].astype(out_dtype)

    if shard:
        out_spec, shape = pl.BlockSpec((1, ta, tb), lambda i, j, s: (j, i, 0)), (B // tb, A, tb)
    else:
        out_spec, shape = pl.BlockSpec((ta, tb), lambda i, j, s: (i, j)), (A, B)
    return pl.pallas_call(
        body, name=name, grid=(A // ta, B // tb, ns),
        in_specs=[pl.BlockSpec((ts, ta), lambda i, j, s: (s, i)), pl.BlockSpec((ts, tb), lambda i, j, s: (s, j))],
        out_specs=out_spec, out_shape=jax.ShapeDtypeStruct(shape, out_dtype),
        scratch_shapes=[] if direct else [pltpu.VMEM((ta, tb), F32)],
        compiler_params=_params("parallel", "parallel", "arbitrary"),
    )(x, y)


def _rmsnorm_fwd(x, g, *, scale, out_dtype, name):
    R, W = x.shape
    tr = _tile(R, 512 if W >= 512 else 4096, 8)

    def body(x_ref, g_ref, o_ref):
        xv = x_ref[...]
        r = lax.rsqrt(jnp.mean(xv * xv, axis=1, keepdims=True) + EPS)
        o_ref[...] = (xv * r * (g_ref[...] * scale)).astype(out_dtype)

    return pl.pallas_call(
        body, name=name, grid=(R // tr,),
        in_specs=[pl.BlockSpec((tr, W), lambda i: (i, 0)), pl.BlockSpec((1, W), lambda i: (0, 0))],
        out_specs=pl.BlockSpec((tr, W), lambda i: (i, 0)),
        out_shape=jax.ShapeDtypeStruct((R, W), out_dtype),
        compiler_params=_params("parallel"),
    )(x, g)


def _rmsnorm_bwd(x, g, dy, *, scale, name, dres=None, out_dtype=F32, bf16_copy=False):
    R, W = x.shape
    tr = _tile(R, 512 if W >= 512 else 4096, 8)
    has_res = dres is not None

    def body(*refs):
        refs = list(refs)
        dxb_ref = refs.pop() if bf16_copy else None
        if has_res:
            x_ref, g_ref, dy_ref, dres_ref, dx_ref, dg_ref = refs
        else:
            x_ref, g_ref, dy_ref, dx_ref, dg_ref = refs
        i = pl.program_id(0)
        xv = x_ref[...]
        dyv = dy_ref[...].astype(F32) * scale
        r = lax.rsqrt(jnp.mean(xv * xv, axis=1, keepdims=True) + EPS)
        u = dyv * g_ref[...]
        dx = r * u - xv * (r * r * r * jnp.mean(u * xv, axis=1, keepdims=True))
        if has_res:
            dx = dx + dres_ref[...]
        dx_ref[...] = dx.astype(out_dtype)
        if bf16_copy:
            dxb_ref[...] = dx.astype(BF16)
        part = jnp.sum(dyv * xv * r, axis=0, keepdims=True)

        @pl.when(i == 0)
        def _():
            dg_ref[...] = part

        @pl.when(i > 0)
        def _():
            dg_ref[...] += part

    row = pl.BlockSpec((tr, W), lambda i: (i, 0))
    one = pl.BlockSpec((1, W), lambda i: (0, 0))
    in_specs = [row, one, row] + ([row] if has_res else [])
    args = [x, g, dy] + ([dres] if has_res else [])
    extra = bool(bf16_copy)
    return pl.pallas_call(
        body, name=name, grid=(R // tr,), in_specs=in_specs, out_specs=[row, one] + [row] * extra,
        out_shape=[jax.ShapeDtypeStruct((R, W), out_dtype), jax.ShapeDtypeStruct((1, W), F32)]
        + [jax.ShapeDtypeStruct((R, W), BF16)] * extra,
        compiler_params=_params("arbitrary"),
    )(*args)


HALO = 16


def _shift_down(u, prev, n):
    out = pltpu.roll(u, n, 0)
    row = lax.broadcasted_iota(jnp.int32, u.shape, 0)
    for r in range(n):
        out = jnp.where(row == r, prev[HALO - n + r:HALO - n + r + 1, :], out)
    return out


def _shift_up(u, nxt, n):
    ts = u.shape[0]
    out = pltpu.roll(u, ts - n, 0)
    row = lax.broadcasted_iota(jnp.int32, u.shape, 0)
    for r in range(n):
        out = jnp.where(row == ts - n + r, nxt[r:r + 1, :], out)
    return out


def _conv_fwd(P, conv_w, conv_b, *, C, name):
    S = P.shape[0]
    ts = _tile(S, 512, HALO)
    hb = ts // HALO

    def body(ab_ref, ac_ref, ax_ref, pc_ref, px_ref, w_ref, b_ref, o_ref):
        i = pl.program_id(0)
        u = ac_ref[...].astype(F32) * ax_ref[...].astype(F32)
        prev = pc_ref[...].astype(F32) * px_ref[...].astype(F32) * (i > 0).astype(F32)
        w = w_ref[...]
        y = b_ref[...] + w[0:1, :] * _shift_down(u, prev, 2) + w[1:2, :] * _shift_down(u, prev, 1) + w[2:3, :] * u
        o_ref[...] = (ab_ref[...].astype(F32) * y).astype(BF16)

    cur = lambda c: pl.BlockSpec((ts, C), lambda i: (i, c))
    prv = lambda c: pl.BlockSpec((HALO, C), lambda i: (jnp.maximum(i * hb - 1, 0), c))
    return pl.pallas_call(
        body, name=name, grid=(S // ts,),
        in_specs=[cur(0), cur(1), cur(2), prv(1), prv(2),
                  pl.BlockSpec((8, C), lambda i: (0, 0)), pl.BlockSpec((1, C), lambda i: (0, 0))],
        out_specs=pl.BlockSpec((ts, C), lambda i: (i, 0)),
        out_shape=jax.ShapeDtypeStruct((S, C), BF16),
        compiler_params=_params("parallel"),
    )(P, P, P, P, P, conv_w, conv_b)


def _conv_bwd(P, dya, conv_w, conv_b, *, C, name):
    S = P.shape[0]
    ts = _tile(S, 512, HALO)
    hb = ts // HALO
    last = S // HALO - 1
    n = S // ts

    def body(ab_ref, ac_ref, ax_ref, pc_ref, px_ref, dy_ref, nab_ref, ndy_ref, w_ref, b_ref, o_ref, dw_ref):
        i = pl.program_id(0)
        ab, ac, ax = ab_ref[...].astype(F32), ac_ref[...].astype(F32), ax_ref[...].astype(F32)
        u = ac * ax
        prev = pc_ref[...].astype(F32) * px_ref[...].astype(F32) * (i > 0).astype(F32)
        w = w_ref[...]
        u1, u2 = _shift_down(u, prev, 1), _shift_down(u, prev, 2)
        y = b_ref[...] + w[0:1, :] * u2 + w[1:2, :] * u1 + w[2:3, :] * u
        dya_v = dy_ref[...]
        dyp = dya_v * ab
        nxt = ndy_ref[...] * nab_ref[...].astype(F32) * (i < n - 1).astype(F32)
        du = w[2:3, :] * dyp + w[1:2, :] * _shift_up(dyp, nxt, 1) + w[0:1, :] * _shift_up(dyp, nxt, 2)
        o_ref[:, 0:C] = (dya_v * y).astype(BF16)
        o_ref[:, C:2 * C] = (du * ax).astype(BF16)
        o_ref[:, 2 * C:3 * C] = (du * ac).astype(BF16)
        part = jnp.concatenate([
            jnp.sum(dyp * u2, axis=0, keepdims=True), jnp.sum(dyp * u1, axis=0, keepdims=True),
            jnp.sum(dyp * u, axis=0, keepdims=True), jnp.sum(dyp, axis=0, keepdims=True),
            jnp.zeros((4, C), F32)], axis=0)

        @pl.when(i == 0)
        def _():
            dw_ref[...] = part

        @pl.when(i > 0)
        def _():
            dw_ref[...] += part

    cur = lambda c: pl.BlockSpec((ts, C), lambda i: (i, c))
    prv = lambda c: pl.BlockSpec((HALO, C), lambda i: (jnp.maximum(i * hb - 1, 0), c))
    nxt = lambda c: pl.BlockSpec((HALO, C), lambda i: (jnp.minimum((i + 1) * hb, last), c))
    return pl.pallas_call(
        body, name=name, grid=(n,),
        in_specs=[cur(0), cur(1), cur(2), prv(1), prv(2), cur(0), nxt(0), nxt(0),
                  pl.BlockSpec((8, C), lambda i: (0, 0)), pl.BlockSpec((1, C), lambda i: (0, 0))],
        out_specs=[pl.BlockSpec((ts, 3 * C), lambda i: (i, 0)), pl.BlockSpec((8, C), lambda i: (0, 0))],
        out_shape=[jax.ShapeDtypeStruct((S, 3 * C), BF16), jax.ShapeDtypeStruct((8, C), F32)],
        compiler_params=_params("arbitrary"),
    )(P, P, P, P, P, dya, P, dya, conv_w, conv_b)


def _sgu_fwd(P, ln_g, ln_b, wm, bT, *, W, cu, cv, name):
    S = P.shape[0]
    G, CH, _ = wm.shape
    gw = W // G
    ts = _tile(S, 512, CH)

    def body(u_ref, v_ref, g_ref, b_ref, wm_ref, bT_ref, o_ref):
        gv = _gelu(v_ref[...].astype(F32))
        mu = jnp.mean(gv, axis=1, keepdims=True)
        xc = gv - mu
        vn = (xc * lax.rsqrt(jnp.mean(xc * xc, axis=1, keepdims=True) + EPS) * g_ref[...] + b_ref[...]).astype(BF16)
        bT_v = bT_ref[...]
        for c in range(ts // CH):
            rows = slice(c * CH, (c + 1) * CH)
            for g in range(G):
                cols = slice(g * gw, (g + 1) * gw)
                mixed = jnp.dot(wm_ref[g], vn[rows, cols], preferred_element_type=F32) + bT_v[:, g:g + 1]
                o_ref[rows, cols] = (_gelu(u_ref[rows, cols].astype(F32)) * mixed).astype(BF16)

    full = lambda shp: pl.BlockSpec(shp, lambda i: (0,) * len(shp))
    return pl.pallas_call(
        body, name=name, grid=(S // ts,),
        in_specs=[pl.BlockSpec((ts, W), lambda i: (i, cu)), pl.BlockSpec((ts, W), lambda i: (i, cv)),
                  full((1, W)), full((1, W)), full((G, CH, CH)), full((CH, G))],
        out_specs=pl.BlockSpec((ts, W), lambda i: (i, 0)),
        out_shape=jax.ShapeDtypeStruct((S, W), BF16),
        compiler_params=_params("parallel"),
    )(P, P, ln_g, ln_b, wm, bT)


def _sgu_bwd(P, dyb, ln_g, ln_b, wm, wmT, bT, *, W, cu, cv, name):
    S = P.shape[0]
    G, CH, _ = wm.shape
    gw = W // G
    ts = _tile(S, 512, CH)

    def body(u_ref, v_ref, dy_ref, g_ref, b_ref, wm_ref, wmT_ref, bT_ref, o_ref, dw_ref, db_ref, dln_ref, dvn_ref):
        i = pl.program_id(0)

        @pl.when(i == 0)
        def _():
            dw_ref[...] = jnp.zeros_like(dw_ref)
            db_ref[...] = jnp.zeros_like(db_ref)
            dln_ref[...] = jnp.zeros_like(dln_ref)

        sv = v_ref[...].astype(F32)
        gv = _gelu(sv)
        mu = jnp.mean(gv, axis=1, keepdims=True)
        xc = gv - mu
        rstd = lax.rsqrt(jnp.mean(xc * xc, axis=1, keepdims=True) + EPS)
        xhat = xc * rstd
        lg = g_ref[...]
        vn = (xhat * lg + b_ref[...]).astype(BF16)
        bT_v = bT_ref[...]
        for c in range(ts // CH):
            rows = slice(c * CH, (c + 1) * CH)
            for g in range(G):
                cols = slice(g * gw, (g + 1) * gw)
                vn_cg = vn[rows, cols]
                mixed = jnp.dot(wm_ref[g], vn_cg, preferred_element_type=F32) + bT_v[:, g:g + 1]
                su = u_ref[rows, cols].astype(F32)
                dyv = dy_ref[rows, cols]
                dmix = dyv * _gelu(su)
                o_ref[rows, cols] = (dyv * mixed * _gelu_grad(su)).astype(BF16)
                dmix_b = dmix.astype(BF16)
                dw_ref[g] += lax.dot_general(dmix_b, vn_cg, (((1,), (1,)), ((), ())), preferred_element_type=F32)
                db_ref[g] += jnp.broadcast_to(jnp.sum(dmix, axis=1, keepdims=True), (CH, CH))
                dvn_ref[rows, cols] = jnp.dot(wmT_ref[g], dmix_b, preferred_element_type=F32)
        dvn = dvn_ref[...]
        dxh = dvn * lg
        dgv = rstd * (dxh - jnp.mean(dxh, axis=1, keepdims=True) - xhat * jnp.mean(dxh * xhat, axis=1, keepdims=True))
        o_ref[:, W:2 * W] = (dgv * _gelu_grad(sv)).astype(BF16)
        dln_ref[0:1, :] += jnp.sum(dvn * xhat, axis=0, keepdims=True)
        dln_ref[1:2, :] += jnp.sum(dvn, axis=0, keepdims=True)

    full = lambda shp: pl.BlockSpec(shp, lambda i: (0,) * len(shp))
    return pl.pallas_call(
        body, name=name, grid=(S // ts,),
        in_specs=[pl.BlockSpec((ts, W), lambda i: (i, cu)), pl.BlockSpec((ts, W), lambda i: (i, cv)),
                  pl.BlockSpec((ts, W), lambda i: (i, 0)),
                  full((1, W)), full((1, W)), full((G, CH, CH)), full((G, CH, CH)), full((CH, G))],
        out_specs=[pl.BlockSpec((ts, 2 * W), lambda i: (i, 0)), full((G, CH, CH)), full((G, CH, CH)), full((8, W))],
        out_shape=[jax.ShapeDtypeStruct((S, 2 * W), BF16), jax.ShapeDtypeStruct((G, CH, CH), F32),
                   jax.ShapeDtypeStruct((G, CH, CH), F32), jax.ShapeDtypeStruct((8, W), F32)],
        scratch_shapes=[pltpu.VMEM((ts, W), F32)],
        compiler_params=_params("arbitrary"),
    )(P, P, dyb, ln_g, ln_b, wm, wmT, bT)


def _block_sums(x, u, parts=1):
    hi = x.astype(BF16)
    out = jnp.dot(hi, u, preferred_element_type=F32)
    if parts == 2:
        lo = (x - hi.astype(F32)).astype(BF16)
        out = out + jnp.dot(lo, u, preferred_element_type=F32)
    return out


_NT = (((1,), (1,)), ((), ()))
_TN = (((0,), (0,)), ((), ()))


def _left_blocks(step, jd, carry, unroll, jd_multiple):
    rem = 0
    if jd_multiple % unroll:
        rem = jd % unroll
        carry = lax.fori_loop(0, rem, lambda t, c: step(jd - 1 - t, c, False), carry)

    def trip(t, c):
        for s in range(unroll):
            c = step(jd - rem - 1 - s - unroll * t, c, False)
        return c

    return lax.fori_loop(0, jd // unroll, trip, carry)


def _diag_step(step, j, carry, row0):
    if row0 == 0:
        return step(j, carry, True)
    tail = step(j, tuple(c[row0:] for c in carry), True, row0)
    return tuple(jnp.concatenate([c[:row0], t], axis=0) for c, t in zip(carry, tail))


def _qkv_heads_fwd(P, gq, gk, *, A, col0, hd, qscale, name):
    S = P.shape[0]
    H = A // hd
    ts = _tile(S, 512, 8)

    def body(q_ref, k_ref, v_ref, gq_ref, gk_ref, qn_ref, kn_ref, vh_ref):
        for h in range(H):
            cols = slice(h * hd, (h + 1) * hd)
            for x_ref, g_ref, sc, o_ref in ((q_ref, gq_ref, qscale, qn_ref), (k_ref, gk_ref, 1.0, kn_ref)):
                xh = x_ref[:, cols].astype(F32)
                r = lax.rsqrt(jnp.mean(xh * xh, axis=1, keepdims=True) + EPS)
                o_ref[h] = (xh * r * (g_ref[...] * sc)).astype(BF16)
            vh_ref[h] = v_ref[:, cols].astype(BF16)

    col = lambda n: pl.BlockSpec((ts, A), lambda i: (i, col0 + n))
    gsp = pl.BlockSpec((1, hd), lambda i: (0, 0))
    hsp = pl.BlockSpec((H, ts, hd), lambda i: (0, i, 0))
    shp = jax.ShapeDtypeStruct((H, S, hd), BF16)
    return pl.pallas_call(
        body, name=name, grid=(S // ts,), in_specs=[col(0), col(1), col(2), gsp, gsp],
        out_specs=[hsp, hsp, hsp], out_shape=[shp, shp, shp], compiler_params=_params("parallel"),
    )(P, P, P, gq, gk)


def _qkv_heads_bwd(P, gq, gk, dqn, dkn, dvh, *, A, col0, hd, qscale, name):
    S = P.shape[0]
    H = A // hd
    ts = _tile(S, 512, 8)

    def body(q_ref, k_ref, gq_ref, gk_ref, dq_ref, dk_ref, dv_ref, o_ref, dgq_ref, dgk_ref):
        i = pl.program_id(0)
        parts = [jnp.zeros((1, hd), F32), jnp.zeros((1, hd), F32)]
        for h in range(H):
            for n, (x_ref, g_ref, sc, d_ref) in enumerate(((q_ref, gq_ref, qscale, dq_ref), (k_ref, gk_ref, 1.0, dk_ref))):
                xh = x_ref[:, h * hd:(h + 1) * hd].astype(F32)
                dyv = d_ref[h] * sc
                r = lax.rsqrt(jnp.mean(xh * xh, axis=1, keepdims=True) + EPS)
                u = dyv * g_ref[...]
                dx = r * u - xh * (r * r * r * jnp.mean(u * xh, axis=1, keepdims=True))
                o_ref[:, n * A + h * hd:n * A + (h + 1) * hd] = dx.astype(BF16)
                parts[n] = parts[n] + jnp.sum(dyv * xh * r, axis=0, keepdims=True)
            o_ref[:, 2 * A + h * hd:2 * A + (h + 1) * hd] = dv_ref[h].astype(BF16)

        @pl.when(i == 0)
        def _():
            dgq_ref[...] = parts[0]
            dgk_ref[...] = parts[1]

        @pl.when(i > 0)
        def _():
            dgq_ref[...] += parts[0]
            dgk_ref[...] += parts[1]

    col = lambda n: pl.BlockSpec((ts, A), lambda i: (i, col0 + n))
    gsp = pl.BlockSpec((1, hd), lambda i: (0, 0))
    hsp = pl.BlockSpec((H, ts, hd), lambda i: (0, i, 0))
    return pl.pallas_call(
        body, name=name, grid=(S // ts,), in_specs=[col(0), col(1), gsp, gsp, hsp, hsp, hsp],
        out_specs=[pl.BlockSpec((ts, 3 * A), lambda i: (i, 0)), gsp, gsp],
        out_shape=[jax.ShapeDtypeStruct((S, 3 * A), BF16), jax.ShapeDtypeStruct((1, hd), F32),
                   jax.ShapeDtypeStruct((1, hd), F32)],
        compiler_params=_params("arbitrary"),
    )(P, P, gq, gk, dqn, dkn, dvh)


def _comm_begin(schedule, cin_refs, cout_refs, sems, grid):
    begin, end = _comm_phases(schedule, cin_refs, cout_refs, *sems)
    ids = [pl.program_id(d) for d in range(len(grid))]
    pl.when(functools.reduce(jnp.logical_and, [p == 0 for p in ids]))(begin)

    def comm_end():
        pl.when(functools.reduce(jnp.logical_and, [p == g - 1 for p, g in zip(ids, grid)]))(end)

    return comm_end


def _call_with_comm(body, name, grid, in_specs, out_specs, out_shape, args, comm):
    if comm is None:
        return pl.pallas_call(body, name=name, grid=grid, in_specs=in_specs, out_specs=out_specs, out_shape=out_shape,
                              compiler_params=_params("parallel", "arbitrary"))(*args)
    _, xs, cshapes = comm
    anyspec = pl.BlockSpec(memory_space=pl.ANY)
    return pl.pallas_call(
        body, name=name, grid=grid, in_specs=in_specs + [anyspec] * len(xs), out_specs=out_specs + [anyspec] * len(xs),
        out_shape=out_shape + [jax.ShapeDtypeStruct(c, x.dtype) for c, x in zip(cshapes, xs)],
        scratch_shapes=_comm_scratch(len(xs)), compiler_params=_params("arbitrary", "arbitrary"),
    )(*args, *xs)


def _attn_fwd(q, k, v, umat, *, tq, tk, name, comm=None):
    H, S, hd = q.shape

    def body(*refs):
        if comm is None:
            q_ref, k_ref, v_ref, u_ref, o_ref = refs
        else:
            nc = len(comm[1])
            q_ref, k_ref, v_ref, u_ref = refs[:4]
            o_ref = refs[4 + nc]
            comm_end = _comm_begin(comm[0], refs[4:4 + nc], refs[5 + nc:5 + 2 * nc], refs[5 + 2 * nc:], (H, S // tq))
        i = pl.program_id(1)
        qb = q_ref[0]
        um = u_ref[...]
        qpos = lax.broadcasted_iota(jnp.int32, (tq, tk), 0) + i * tq
        kloc = lax.broadcasted_iota(jnp.int32, (tq, tk), 1)

        def step(j, carry, masked, row0=0):
            r, acc = carry
            ks = pl.multiple_of(j * tk, tk)
            kb = k_ref[0, pl.ds(ks, tk), :]
            vb = v_ref[0, pl.ds(ks, tk), :]
            z = lax.dot_general(qb[row0:], kb, _NT, preferred_element_type=F32)
            lb = jnp.minimum(z, 0.0) - jnp.log(1.0 + jnp.exp(-jnp.abs(z)))
            lm = lb - z
            if masked:
                m = (kloc[row0:] + j * tk) < qpos[row0:]
                lm = jnp.where(m, lm, 0.0)
            a = jnp.exp(lb + _block_sums(lm, um) + r)
            if masked:
                a = jnp.where(m, a, 0.0)
            acc = acc + jnp.dot(a.astype(BF16), vb, preferred_element_type=F32)
            return r + jnp.sum(lm, axis=1, keepdims=True), acc

        jd = (i * tq) // tk
        carry = (jnp.zeros((tq, 1), F32), jnp.zeros((tq, hd), F32))
        for dd in reversed(range(max(1, tq // tk))):
            carry = _diag_step(step, jd + dd, carry, dd * tk if tq > tk else 0)
        carry = _left_blocks(step, jd, carry, ATTN_FWD_UNROLL, tq // tk)
        o_ref[0] = carry[1]
        if comm is not None:
            comm_end()

    blk = pl.BlockSpec((1, tq, hd), lambda h, i: (h, i, 0))
    whole = pl.BlockSpec((1, S, hd), lambda h, i: (h, 0, 0))
    in_specs = [blk, whole, whole, pl.BlockSpec((tk, tk), lambda h, i: (0, 0))]
    out_specs, out_shape = [blk], [jax.ShapeDtypeStruct((H, S, hd), F32)]
    res = _call_with_comm(body, name, (H, S // tq), in_specs, out_specs, out_shape, (q, k, v, umat), comm)
    return res[0] if comm is None else (res[0], res[1:])


def _attn_bwd(q, k, v, o, do, umat, *, tq, tk, name, comm=None):
    H, S, hd = q.shape

    def body(*refs):
        if comm is None:
            q_ref, k_ref, v_ref, o_ref, do_ref, u_ref, dq_ref, dk_ref, dv_ref = refs
        else:
            nc = len(comm[1])
            q_ref, k_ref, v_ref, o_ref, do_ref, u_ref = refs[:6]
            dq_ref, dk_ref, dv_ref = refs[6 + nc:9 + nc]
            comm_end = _comm_begin(comm[0], refs[6:6 + nc], refs[9 + nc:9 + 2 * nc], refs[9 + 2 * nc:], (H, S // tq))
        i = pl.program_id(1)

        @pl.when(i == 0)
        def _():
            dk_ref[...] = jnp.zeros_like(dk_ref)
            dv_ref[...] = jnp.zeros_like(dv_ref)

        qb = q_ref[0]
        do32 = do_ref[0]
        dob = do32.astype(BF16)
        tot = jnp.sum(dob.astype(F32) * o_ref[0], axis=1, keepdims=True)
        um = u_ref[...]
        qpos = lax.broadcasted_iota(jnp.int32, (tq, tk), 0) + i * tq
        kloc = lax.broadcasted_iota(jnp.int32, (tq, tk), 1)

        def step(j, carry, masked, row0=0):
            r, gs, dq = carry
            ks = pl.multiple_of(j * tk, tk)
            kb = k_ref[0, pl.ds(ks, tk), :]
            vb = v_ref[0, pl.ds(ks, tk), :]
            qs, dos = qb[row0:], dob[row0:]
            z = lax.dot_general(qs, kb, _NT, preferred_element_type=F32)
            lb = jnp.minimum(z, 0.0) - jnp.log(1.0 + jnp.exp(-jnp.abs(z)))
            sig = jnp.exp(lb)
            lm = lb - z
            if masked:
                m = (kloc[row0:] + j * tk) < qpos[row0:]
                lm = jnp.where(m, lm, 0.0)
            a = jnp.exp(lb + _block_sums(lm, um) + r)
            if masked:
                a = jnp.where(m, a, 0.0)
            ab = a.astype(BF16)
            g = lax.dot_general(dos, vb, _NT, preferred_element_type=F32) * ab.astype(F32)
            dz = g - sig * ((tot[row0:] - gs) - _block_sums(g, um, parts=2))
            if masked:
                dz = jnp.where(m, dz, 0.0)
            dzb = dz.astype(BF16)
            dq = dq + jnp.dot(dzb, kb, preferred_element_type=F32)
            dk_ref[0, pl.ds(ks, tk), :] += lax.dot_general(dzb, qs, _TN, preferred_element_type=F32)
            dv_ref[0, pl.ds(ks, tk), :] += lax.dot_general(ab, dos, _TN, preferred_element_type=F32)
            return r + jnp.sum(lm, axis=1, keepdims=True), gs + jnp.sum(g, axis=1, keepdims=True), dq

        jd = (i * tq) // tk
        zero = jnp.zeros((tq, 1), F32)
        carry = (zero, zero, jnp.zeros((tq, hd), F32))
        for dd in reversed(range(max(1, tq // tk))):
            carry = _diag_step(step, jd + dd, carry, dd * tk if tq > tk else 0)
        carry = _left_blocks(step, jd, carry, ATTN_BWD_UNROLL, tq // tk)
        dq_ref[0] = carry[2]
        if comm is not None:
            comm_end()

    blk = pl.BlockSpec((1, tq, hd), lambda h, i: (h, i, 0))
    whole = pl.BlockSpec((1, S, hd), lambda h, i: (h, 0, 0))
    shp = jax.ShapeDtypeStruct((H, S, hd), F32)
    in_specs = [blk, whole, whole, blk, blk, pl.BlockSpec((tk, tk), lambda h, i: (0, 0))]
    res = _call_with_comm(body, name, (H, S // tq), in_specs, [blk, whole, whole], [shp, shp, shp],
                          (q, k, v, o, do, umat), comm)
    return res if comm is None else (*res[:3], res[3:])


def _merge_fwd(ya, yb, o, wb, P, b_gate, *, gate_col0, name):
    S, C = ya.shape
    H, _, hd = o.shape
    D = wb.shape[2]
    ts = _tile(S, 512, 8)

    def body(y0, y1, oh_ref, wb_ref, g0, g1, g2, bg_ref, m_ref, yc_ref):
        for h in range(H):
            yc_ref[:, h * hd:(h + 1) * hd] = oh_ref[h].astype(BF16)
        acc = jnp.zeros((ts, D), F32)
        for n, (y_ref, g_ref) in enumerate(((y0, g0), (y1, g1), (yc_ref, g2))):
            yd = jnp.dot(y_ref[...], wb_ref[n], preferred_element_type=F32)
            acc = acc + _sigmoid(g_ref[...].astype(F32) + bg_ref[:, n * D:(n + 1) * D]) * yd
        m_ref[...] = acc.astype(BF16)

    ysp = pl.BlockSpec((ts, C), lambda i: (i, 0))
    gsp = lambda n: pl.BlockSpec((ts, D), lambda i: (i, gate_col0 + n))
    return pl.pallas_call(
        body, name=name, grid=(S // ts,),
        in_specs=[ysp, ysp, pl.BlockSpec((H, ts, hd), lambda i: (0, i, 0)), pl.BlockSpec((3, C, D), lambda i: (0, 0, 0)),
                  gsp(0), gsp(1), gsp(2), pl.BlockSpec((1, 3 * D), lambda i: (0, 0))],
        out_specs=[pl.BlockSpec((ts, D), lambda i: (i, 0)), ysp],
        out_shape=[jax.ShapeDtypeStruct((S, D), BF16), jax.ShapeDtypeStruct((S, C), BF16)],
        compiler_params=_params("parallel"),
    )(ya, yb, o, wb, P, P, P, b_gate)


def _merge_bwd(ys, wb, wbT, P, b_gate, dmerged, *, gate_col0, hd, name):
    S, C = ys[0].shape
    D = wb.shape[2]
    H = C // hd
    ts = _tile(S, 512, 8)

    def body(y0, y1, y2, wb_ref, wbT_ref, g0, g1, g2, bg_ref, dm_ref,
             dg_ref, dyd0, dyd1, dyd2, dya_ref, dyb_ref, do_ref, dbg_ref):
        i = pl.program_id(0)
        dm = dm_ref[...].astype(F32)
        parts = []
        for n, (y_ref, g_ref, dyd_ref) in enumerate(((y0, g0, dyd0), (y1, g1, dyd1), (y2, g2, dyd2))):
            yd = jnp.dot(y_ref[...], wb_ref[n], preferred_element_type=F32)
            sg = _sigmoid(g_ref[...].astype(F32) + bg_ref[:, n * D:(n + 1) * D])
            dgate = dm * yd * sg * (1.0 - sg)
            dg_ref[:, n * D:(n + 1) * D] = dgate.astype(BF16)
            parts.append(jnp.sum(dgate, axis=0, keepdims=True))
            dyd = (dm * sg).astype(BF16)
            dyd_ref[...] = dyd
            dy = jnp.dot(dyd, wbT_ref[n], preferred_element_type=F32)
            if n == 0:
                dya_ref[...] = dy
            elif n == 1:
                dyb_ref[...] = dy
            else:
                for h in range(H):
                    do_ref[h] = dy[:, h * hd:(h + 1) * hd]
        part = jnp.concatenate(parts, axis=1)

        @pl.when(i == 0)
        def _():
            dbg_ref[...] = part

        @pl.when(i > 0)
        def _():
            dbg_ref[...] += part

    ysp = pl.BlockSpec((ts, C), lambda i: (i, 0))
    dsp = pl.BlockSpec((ts, D), lambda i: (i, 0))
    gsp = lambda n: pl.BlockSpec((ts, D), lambda i: (i, gate_col0 + n))
    dshp = jax.ShapeDtypeStruct((S, D), BF16)
    yshp = jax.ShapeDtypeStruct((S, C), F32)
    return pl.pallas_call(
        body, name=name, grid=(S // ts,),
        in_specs=[ysp, ysp, ysp, pl.BlockSpec((3, C, D), lambda i: (0, 0, 0)),
                  pl.BlockSpec((3, D, C), lambda i: (0, 0, 0)), gsp(0), gsp(1), gsp(2),
                  pl.BlockSpec((1, 3 * D), lambda i: (0, 0)), dsp],
        out_specs=[pl.BlockSpec((ts, 3 * D), lambda i: (i, 0)), dsp, dsp, dsp, ysp, ysp,
                   pl.BlockSpec((H, ts, hd), lambda i: (0, i, 0)), pl.BlockSpec((1, 3 * D), lambda i: (0, 0))],
        out_shape=[jax.ShapeDtypeStruct((S, 3 * D), BF16), dshp, dshp, dshp, yshp, yshp,
                   jax.ShapeDtypeStruct((H, S, hd), F32), jax.ShapeDtypeStruct((1, 3 * D), F32)],
        compiler_params=_params("arbitrary"),
    )(*ys, wb, wbT, P, P, P, b_gate, dmerged)


def _swiglu_fwd(gu, *, name):
    S, F2 = gu.shape
    F = F2 // 2
    ts, tf = _tile(S, 512, 8), _tile(F, 1536, LANES)
    nf = F // tf

    def body(g_ref, u_ref, o_ref):
        gt = g_ref[...].astype(F32)
        o_ref[...] = (gt * _sigmoid(gt) * u_ref[...].astype(F32)).astype(BF16)

    return pl.pallas_call(
        body, name=name, grid=(S // ts, nf),
        in_specs=[pl.BlockSpec((ts, tf), lambda i, j: (i, j)), pl.BlockSpec((ts, tf), lambda i, j: (i, j + nf))],
        out_specs=pl.BlockSpec((ts, tf), lambda i, j: (i, j)),
        out_shape=jax.ShapeDtypeStruct((S, F), BF16),
        compiler_params=_params("parallel", "parallel"),
    )(gu, gu)


def _swiglu_bwd(gu, dact, *, name):
    S, F2 = gu.shape
    F = F2 // 2
    ts = _tile(S, 512, 8)

    def body(gu_ref, d_ref, o_ref):
        gt, up, da = gu_ref[:, 0:F].astype(F32), gu_ref[:, F:F2].astype(F32), d_ref[...].astype(F32)
        sg = _sigmoid(gt)
        o_ref[:, 0:F] = (da * up * sg * (1.0 + gt * (1.0 - sg))).astype(BF16)
        o_ref[:, F:F2] = (da * gt * sg).astype(BF16)

    return pl.pallas_call(
        body, name=name, grid=(S // ts,),
        in_specs=[pl.BlockSpec((ts, F2), lambda i: (i, 0)), pl.BlockSpec((ts, F), lambda i: (i, 0))],
        out_specs=pl.BlockSpec((ts, F2), lambda i: (i, 0)),
        out_shape=jax.ShapeDtypeStruct((S, F2), BF16),
        compiler_params=_params("parallel"),
    )(gu, dact)


def _loss_grad(y, target, *, name):
    S, D = y.shape
    ts = _tile(S, 512, 8)

    def body(y_ref, t_ref, dy_ref, l_ref, dyb_ref):
        i = pl.program_id(0)
        err = y_ref[...] - t_ref[...]
        dy_ref[...] = err * (1.0 / D)
        dyb_ref[...] = (err * (1.0 / D)).astype(BF16)
        part = jnp.broadcast_to(jnp.sum(jnp.sum(err * err, axis=1, keepdims=True), axis=0, keepdims=True) * (0.5 / D),
                                (1, LANES))

        @pl.when(i == 0)
        def _():
            l_ref[...] = part

        @pl.when(i > 0)
        def _():
            l_ref[...] += part

    row = pl.BlockSpec((ts, D), lambda i: (i, 0))
    return pl.pallas_call(
        body, name=name, grid=(S // ts,), in_specs=[row, row],
        out_specs=[row, pl.BlockSpec((1, LANES), lambda i: (0, 0)), row],
        out_shape=[jax.ShapeDtypeStruct((S, D), F32), jax.ShapeDtypeStruct((1, LANES), F32),
                   jax.ShapeDtypeStruct((S, D), BF16)],
        compiler_params=_params("arbitrary"),
    )(y, target)


def _adamw(w, m, v, gs_list, *, name):
    R, W = w.shape
    rows = [g.shape[1] for g in gs_list]
    tr = _tile(math.gcd(*rows), max(16, (2048 * LANES // W) // 16 * 16), 16)
    first = [sum(rows[:t]) // tr for t in range(len(rows))]
    c1 = 1.0 / (1.0 - ADAM_B1 ** ADAM_STEP)
    c2 = 1.0 / (1.0 - ADAM_B2 ** ADAM_STEP)

    def body(w_ref, m_ref, v_ref, *refs):
        gs_refs, (g_ref, d_ref, nm_ref, nv_ref) = refs[:len(rows)], refs[len(rows):]
        i = pl.program_id(0)
        g = None
        for t, gs_ref in enumerate(gs_refs):
            gt = gs_ref[0].astype(F32)
            for s in range(1, gs_ref.shape[0]):
                gt = gt + gs_ref[s].astype(F32)
            g = gt if g is None else jnp.where(i >= first[t], gt, g)
        nm = ADAM_B1 * m_ref[...] + (1.0 - ADAM_B1) * g
        nv = ADAM_B2 * v_ref[...] + (1.0 - ADAM_B2) * (g * g)
        g_ref[...] = g
        nm_ref[...] = nm
        nv_ref[...] = nv
        d_ref[...] = -ADAM_LR * ((nm * c1) / (jnp.sqrt(nv * c2) + ADAM_EPS) + ADAM_WD * w_ref[...])

    row = pl.BlockSpec((tr, W), lambda i: (i, 0))
    slots = [pl.BlockSpec((g.shape[0], tr, W),
                          lambda i, b0=first[t], nb=rows[t] // tr: (0, jnp.clip(i - b0, 0, nb - 1), 0))
             for t, g in enumerate(gs_list)]
    shp = jax.ShapeDtypeStruct((R, W), F32)
    return pl.pallas_call(
        body, name=name, grid=(R // tr,), in_specs=[row, row, row] + slots,
        out_specs=[row, row, row, row], out_shape=[shp, shp, shp, shp],
        compiler_params=_params("parallel"),
    )(w, m, v, *gs_list)


def _slot_sum(gs, *, name):
    ns, R, _ = gs.shape
    tr = _tile(R, 2048, 16)

    def body(gs_ref, o_ref):
        g = gs_ref[0]
        for s in range(1, ns):
            g = g + gs_ref[s]
        o_ref[...] = g

    return pl.pallas_call(
        body, name=name, grid=(R // tr,),
        in_specs=[pl.BlockSpec((ns, tr, LANES), lambda i: (0, i, 0))],
        out_specs=pl.BlockSpec((tr, LANES), lambda i: (i, 0)),
        out_shape=jax.ShapeDtypeStruct((R, LANES), F32),
        compiler_params=_params("parallel"),
    )(gs)


def _comm_scratch(n):
    return [pltpu.SemaphoreType.DMA((7 * n,)), pltpu.SemaphoreType.DMA((7 * n,)), pltpu.SemaphoreType.DMA((n,))]


def _gather_schedule(x_ref, out_ref, send_sems, recv_sems, local_sem, base):
    x, y, c = lax.axis_index("x"), lax.axis_index("y"), lax.axis_index("c")
    me, sibling = (x, y, c), (x, y, 1 - c)
    chips = [(1 - x, y), (x, 1 - y), (1 - x, 1 - y)]

    def slot(px, py, pc):
        return out_ref.at[4 * px + 2 * py + pc]

    def copy(k, block, to, src=None):
        return pltpu.make_async_remote_copy(
            src_ref=slot(*block) if src is None else src, dst_ref=slot(*block),
            send_sem=send_sems.at[base + k], recv_sem=recv_sems.at[base + k], device_id=to, device_id_type=MESH)

    mine = pltpu.make_async_copy(x_ref, slot(*me), local_sem)
    first = [copy(0, me, sibling, src=x_ref)]
    first += [copy(1 + j, me, (*chip, c), src=x_ref) for j, chip in enumerate(chips)]
    passed = [copy(4 + j, (*chip, c), sibling) for j, chip in enumerate(chips)]

    def begin():
        mine.start()
        for cp in first:
            cp.start()

    def forward():
        for j, chip in enumerate(chips):
            copy(1 + j, (*chip, c), me).wait_recv()
            passed[j].start()

    def finish():
        copy(0, sibling, me).wait_recv()
        for j, chip in enumerate(chips):
            copy(4 + j, (*chip, 1 - c), me).wait_recv()
        for cp in first + passed:
            cp.wait_send()
        mine.wait()

    return begin, forward, finish


def _exchange_schedule(x_ref, out_ref, send_sems, recv_sems, local_sem, base):
    x, y, c = lax.axis_index("x"), lax.axis_index("y"), lax.axis_index("c")
    me = 4 * x + 2 * y + c
    mine = pltpu.make_async_copy(x_ref.at[me], out_ref.at[me], local_sem)
    sends, recvs = [], []
    for k in range(1, N_DEV):
        px = 1 - x if k & 4 else x
        py = 1 - y if k & 2 else y
        pc = 1 - c if k & 1 else c
        peer = 4 * px + 2 * py + pc
        sems = dict(send_sem=send_sems.at[base + k - 1], recv_sem=recv_sems.at[base + k - 1],
                    device_id=(px, py, pc), device_id_type=MESH)
        sends.append(pltpu.make_async_remote_copy(src_ref=x_ref.at[peer], dst_ref=out_ref.at[me], **sems))
        recvs.append(pltpu.make_async_remote_copy(src_ref=x_ref.at[me], dst_ref=out_ref.at[peer], **sems))

    def begin():
        mine.start()
        for cp in sends:
            cp.start()

    def forward():
        pass

    def finish():
        for cp in recvs:
            cp.wait_recv()
        for cp in sends:
            cp.wait_send()
        mine.wait()

    return begin, forward, finish


def _comm_phases(schedule, x_refs, out_refs, send_sems, recv_sems, local_sems):
    parts = [schedule(x, o, send_sems, recv_sems, local_sems.at[t], 7 * t)
             for t, (x, o) in enumerate(zip(x_refs, out_refs))]

    def begin():
        for b, _, _ in parts:
            b()

    def end():
        for _, fw, _ in parts:
            fw()
        for _, _, fin in parts:
            fin()

    return begin, end


def _all_gather(xs, *, name):
    def body(x_ref, out_ref, send_sems, recv_sems, local_sems):
        begin, end = _comm_phases(_gather_schedule, [x_ref], [out_ref], send_sems, recv_sems, local_sems)
        begin()
        end()

    return pl.pallas_call(
        body, name=name,
        in_specs=[pl.BlockSpec(memory_space=pl.ANY)], out_specs=pl.BlockSpec(memory_space=pl.ANY),
        out_shape=jax.ShapeDtypeStruct((N_DEV,) + xs.shape, xs.dtype), scratch_shapes=_comm_scratch(1),
        compiler_params=pltpu.CompilerParams(has_side_effects=True),
    )(xs)


PACK_ROWS = 16


def _pack_rows(parts, dtype, lead=()):
    rows = []
    for p in parts:
        r = p.reshape(lead + (-1, LANES)).astype(dtype)
        pad = (-r.shape[-2]) % PACK_ROWS
        rows.append(jnp.pad(r, [(0, 0)] * len(lead) + [(0, pad), (0, 0)]) if pad else r)
    return jnp.concatenate(rows, axis=len(lead))


def _unpack_rows(packed, shapes, lead=()):
    out, off = [], 0
    for shp in shapes:
        n = math.prod(shp) // LANES
        out.append(lax.slice_in_dim(packed, off, off + n, axis=len(lead)).reshape(lead + tuple(shp)))
        off += n + (-n) % PACK_ROWS
    return out


def _unshard(gathered, axis):
    g = jnp.moveaxis(gathered, 0, axis)
    shp = list(g.shape)
    shp[axis:axis + 2] = [shp[axis] * shp[axis + 1]]
    return g.reshape(shp)


def _reshard(full, axis):
    shp = list(full.shape)
    shp[axis:axis + 1] = [N_DEV, shp[axis] // N_DEV]
    return jnp.moveaxis(full.reshape(shp), axis, 0)


def _late_parts(L):
    return [(k, l) for k in BIG for l in range(L) if (k, l) != ("w_in", 0)]


def _late_groups(L, shard_shape):
    groups = {}
    for k, l in _late_parts(L):
        groups.setdefault(shard_shape[k][-1], []).append((k, l))
    return groups


def _local_step(x, target, w_in0, late_local, groups, shard_shape, sm):
    S, D = x.shape
    L = sm["mix_norm_g"].shape[0]
    wf = {k: [None] * L for k in BIG}
    wf["w_in"][0] = w_in0
    C = sm["conv_b"].shape[1]
    W = sm["sgu_ln_g"].shape[1]
    hd = sm["q_norm_g"].shape[1]
    CH = sm["sgu_w"].shape[2]
    A = w_in0.shape[1] - (3 * C + 2 * W + 3 * D)
    A = A // 3
    H = A // hd
    col_q = 3 * C + 2 * W
    qscale = 1.0 / math.sqrt(hd)
    tril = jnp.tril(jnp.ones((CH, CH), F32))
    (ftq, ftk), (btq, btk) = [(_tile(S, a, LANES), _tile(S, b, LANES)) for a, b in (ATTN_FWD_TILES, ATTN_BWD_TILES)]
    umat = lambda t: (lax.broadcasted_iota(jnp.int32, (t, t), 0) > lax.broadcasted_iota(jnp.int32, (t, t), 1)).astype(BF16)

    saved = []
    for l in range(L):
        n = f"l{l}_"
        g1 = sm["mix_norm_g"][l][None]
        h = _rmsnorm_fwd(x, g1, scale=1.0, out_dtype=BF16, name=n + "mixnorm")
        P = _matmul(h, wf["w_in"][l], name=n + "w_in", out_dtype=BF16)
        cw = jnp.pad(sm["conv_w"][l], ((0, 5), (0, 0)))
        cb = sm["conv_b"][l][None]
        ya = _conv_fwd(P, cw, cb, C=C, name=n + "conv")
        wm = (sm["sgu_w"][l] * tril).astype(BF16)
        bT = sm["sgu_b"][l].T
        lng, lnb = sm["sgu_ln_g"][l][None], sm["sgu_ln_b"][l][None]
        yb = _sgu_fwd(P, lng, lnb, wm, bT, W=W, cu=(3 * C) // W, cv=(3 * C) // W + 1, name=n + "sgu")
        gq, gk = sm["q_norm_g"][l][None], sm["k_norm_g"][l][None]
        qn, kn, vh = _qkv_heads_fwd(P, gq, gk, A=A, col0=col_q // A, hd=hd, qscale=qscale, name=n + "qkv")
        if l == 0:
            o, gathered = _attn_fwd(qn, kn, vh, umat(ftk), tq=ftq, tk=ftk, name=n + "attn",
                                    comm=(_gather_schedule, late_local, [(N_DEV,) + a.shape for a in late_local]))
            for (width, parts), g in zip(groups.items(), gathered):
                r = 0
                for k, ll in parts:
                    rows = math.prod(shard_shape[k]) // width
                    part = lax.slice_in_dim(g, r, r + rows, axis=1).reshape((N_DEV,) + tuple(shard_shape[k]))
                    wf[k][ll] = _unshard(part, SHARD_AXIS[k] - 1)
                    r += rows
        else:
            o = _attn_fwd(qn, kn, vh, umat(ftk), tq=ftq, tk=ftk, name=n + "attn")
        bg = sm["b_gate"][l][None]
        gate_col0 = (col_q + 3 * A) // D
        merged, yc = _merge_fwd(ya, yb, o, wf["w_branch_out"][l], P, bg, gate_col0=gate_col0, name=n + "merge")
        x1 = _matmul(merged, wf["w_o"][l], res=x, name=n + "w_o")
        g2 = sm["ffn_norm_g"][l][None]
        h2 = _rmsnorm_fwd(x1, g2, scale=1.0, out_dtype=BF16, name=n + "ffnnorm")
        gu = _matmul(h2, wf["w_gate_up"][l], name=n + "w_gate_up", out_dtype=BF16)
        act = _swiglu_fwd(gu, name=n + "swiglu")
        x2 = _matmul(act, wf["w_down"][l], res=x1, name=n + "w_down")
        saved.append(dict(x=x, h=h, P=P, cw=cw, cb=cb, ya=ya, wm=wm, bT=bT, lng=lng, lnb=lnb, yb=yb,
                          gq=gq, gk=gk, qn=qn, kn=kn, vh=vh, o=o, yc=yc, bg=bg,
                          gate_col0=gate_col0, merged=merged, x1=x1, g1=g1, g2=g2, h2=h2, gu=gu, act=act))
        x = x2

    dx, lpart, dxb = _loss_grad(x, target, name="loss")
    grads = {k: [None] * L for k in WEIGHTS}
    chunked = lambda k, g: g if k == "w_in" else _reshard(g, SHARD_AXIS[k] - 1)
    for l in reversed(range(L)):
        n = f"l{l}_b_"
        sv = saved[l]
        grads["w_down"][l] = _matmul_tn(sv["act"], dxb, name=n + "g_w_down", out_dtype=BF16)
        dact = _matmul(dxb, wf["w_down"][l].T, name=n + "d_act", out_dtype=BF16)
        dgu = _swiglu_bwd(sv["gu"], dact, name=n + "swiglu")
        grads["w_gate_up"][l] = _matmul_tn(sv["h2"], dgu, name=n + "g_w_gate_up", out_dtype=BF16)
        dh2 = _matmul(dgu, wf["w_gate_up"][l].T, name=n + "d_h2", out_dtype=BF16)
        dx1, dg2, dx1b = _rmsnorm_bwd(sv["x1"], sv["g2"], dh2, scale=1.0, dres=dx, name=n + "ffnnorm",
                                      bf16_copy=True)
        grads["ffn_norm_g"][l] = dg2[0]
        grads["w_o"][l] = _matmul_tn(sv["merged"], dx1b, name=n + "g_w_o", out_dtype=BF16)
        dmerged = _matmul(dx1b, wf["w_o"][l].T, name=n + "d_merged", out_dtype=BF16)
        ys = (sv["ya"], sv["yb"], sv["yc"])
        wb = wf["w_branch_out"][l]
        dgates, *dyd, dya, dyb, do, dbg = _merge_bwd(ys, wb, wb.transpose(0, 2, 1), sv["P"], sv["bg"], dmerged,
                                                     gate_col0=sv["gate_col0"], hd=hd, name=n + "merge")
        grads["b_gate"][l] = dbg[0]
        grads["w_branch_out"][l] = jnp.stack(
            [_matmul_tn(ys[i], dyd[i], name=n + f"g_w_branch{i}", out_dtype=BF16) for i in range(3)])
        dconv, dcw = _conv_bwd(sv["P"], dya, sv["cw"], sv["cb"], C=C, name=n + "conv")
        grads["conv_w"][l], grads["conv_b"][l] = dcw[0:3], dcw[3]
        wmT = sv["wm"].transpose(0, 2, 1)
        dsgu, dsw, dsb, dln = _sgu_bwd(sv["P"], dyb, sv["lng"], sv["lnb"], sv["wm"], wmT, sv["bT"], W=W,
                                       cu=(3 * C) // W, cv=(3 * C) // W + 1, name=n + "sgu")
        grads["sgu_w"][l], grads["sgu_b"][l] = dsw * tril, dsb[:, :, 0]
        grads["sgu_ln_g"][l], grads["sgu_ln_b"][l] = dln[0], dln[1]
        if l == 0:
            chunks = [jnp.concatenate([chunked(k, grads[k][ll]).reshape(N_DEV, -1, width) for k, ll in parts], axis=1)
                      for width, parts in groups.items()]
            dqn, dkn, dvh, late_recv = _attn_bwd(sv["qn"], sv["kn"], sv["vh"], sv["o"], do, umat(btk), tq=btq, tk=btk,
                                                 name=n + "attn",
                                                 comm=(_exchange_schedule, chunks, [c.shape for c in chunks]))
        else:
            dqn, dkn, dvh = _attn_bwd(sv["qn"], sv["kn"], sv["vh"], sv["o"], do, umat(btk), tq=btq, tk=btk,
                                      name=n + "attn")
        dqkv, dgq, dgk = _qkv_heads_bwd(sv["P"], sv["gq"], sv["gk"], dqn, dkn, dvh, A=A, col0=col_q // A, hd=hd,
                                        qscale=qscale, name=n + "qkv")
        grads["q_norm_g"][l], grads["k_norm_g"][l] = dgq[0], dgk[0]
        dP = jnp.concatenate([dconv, dsgu, dqkv, dgates], axis=1)
        grads["w_in"][l] = _matmul_tn(sv["h"], dP, name=n + "g_w_in", out_dtype=BF16, shard=dP.shape[1] // N_DEV)
        if l == 0:
            dh, early_recv = _matmul(dP, wf["w_in"][l].T, name=n + "d_h", out_dtype=BF16,
                                     comm=(_exchange_schedule, [grads["w_in"][0]], [grads["w_in"][0].shape]))
        else:
            dh = _matmul(dP, wf["w_in"][l].T, name=n + "d_h", out_dtype=BF16)
        dx, dg1, dxb = _rmsnorm_bwd(sv["x"], sv["g1"], dh, scale=1.0, dres=dx1, name=n + "mixnorm", bf16_copy=True)
        grads["mix_norm_g"][l] = dg1[0]
    return lpart[0, 0], dx, grads, early_recv, late_recv


def kernel(x, mix_norm_g, w_in, b_gate, conv_w, conv_b, sgu_ln_g, sgu_ln_b, sgu_w, sgu_b, q_norm_g, k_norm_g, w_branch_out, w_o, ffn_norm_g, w_gate_up, w_down, loss_target, m_mix_norm_g, m_w_in, m_b_gate, m_conv_w, m_conv_b, m_sgu_ln_g, m_sgu_ln_b, m_sgu_w, m_sgu_b, m_q_norm_g, m_k_norm_g, m_w_branch_out, m_w_o, m_ffn_norm_g, m_w_gate_up, m_w_down, v_mix_norm_g, v_w_in, v_b_gate, v_conv_w, v_conv_b, v_sgu_ln_g, v_sgu_ln_b, v_sgu_w, v_sgu_b, v_q_norm_g, v_k_norm_g, v_w_branch_out, v_w_o, v_ffn_norm_g, v_w_gate_up, v_w_down):
    w = dict(mix_norm_g=mix_norm_g, w_in=w_in, b_gate=b_gate, conv_w=conv_w, conv_b=conv_b, sgu_ln_g=sgu_ln_g,
             sgu_ln_b=sgu_ln_b, sgu_w=sgu_w, sgu_b=sgu_b, q_norm_g=q_norm_g, k_norm_g=k_norm_g,
             w_branch_out=w_branch_out, w_o=w_o, ffn_norm_g=ffn_norm_g, w_gate_up=w_gate_up, w_down=w_down)
    m = dict(mix_norm_g=m_mix_norm_g, w_in=m_w_in, b_gate=m_b_gate, conv_w=m_conv_w, conv_b=m_conv_b,
             sgu_ln_g=m_sgu_ln_g, sgu_ln_b=m_sgu_ln_b, sgu_w=m_sgu_w, sgu_b=m_sgu_b, q_norm_g=m_q_norm_g,
             k_norm_g=m_k_norm_g, w_branch_out=m_w_branch_out, w_o=m_w_o, ffn_norm_g=m_ffn_norm_g,
             w_gate_up=m_w_gate_up, w_down=m_w_down)
    v = dict(mix_norm_g=v_mix_norm_g, w_in=v_w_in, b_gate=v_b_gate, conv_w=v_conv_w, conv_b=v_conv_b,
             sgu_ln_g=v_sgu_ln_g, sgu_ln_b=v_sgu_ln_b, sgu_w=v_sgu_w, sgu_b=v_sgu_b, q_norm_g=v_q_norm_g,
             k_norm_g=v_k_norm_g, w_branch_out=v_w_branch_out, w_o=v_w_o, ffn_norm_g=v_ffn_norm_g,
             w_gate_up=v_w_gate_up, w_down=v_w_down)
    me = 4 * lax.axis_index("x") + 2 * lax.axis_index("y") + lax.axis_index("c")

    L = w_in.shape[0]
    two_d = lambda a: a.reshape(-1, a.shape[-1])
    shard_shape = {k: w[k].shape[1:] for k in BIG}
    groups = _late_groups(L, shard_shape)
    w_in0 = _unshard(_all_gather(w_in[0].astype(BF16), name="gather_w_in0"), SHARD_AXIS["w_in"] - 1)
    late_local = [jnp.concatenate([two_d(w[k][l]).astype(BF16) for k, l in parts], axis=0) for parts in groups.values()]
    conv_g = _all_gather(_pack_rows([conv_w], F32), name="gather_conv_w")
    sm = {k: w[k] for k in SMALL}
    sm["conv_w"] = _unshard(_unpack_rows(conv_g, [conv_w.shape], lead=(N_DEV,))[0], 2)

    lpart, dx, grads, early_recv, late_recv = _local_step(x[0], loss_target[0], w_in0, late_local, groups,
                                                          shard_shape, sm)
    loss = lax.psum(lpart, ("x", "y", "c"))

    where = {("w_in", 0): (early_recv[0], 0)}
    for (width, parts), recv in zip(groups.items(), late_recv):
        r = 0
        for k, l in parts:
            where[k, l] = (recv, r)
            r += math.prod(shard_shape[k]) // width
    out = {}
    for k in BIG:
        rows = math.prod(shard_shape[k]) // shard_shape[k][-1]
        gs_list = []
        for l in range(L):
            recv, r = where[k, l]
            if gs_list and gs_list[-1][0] is recv and gs_list[-1][2] == r:
                gs_list[-1][2] = r + rows
            else:
                gs_list.append([recv, r, r + rows])
        gs_list = [recv if (r0, r1) == (0, recv.shape[1]) else lax.slice_in_dim(recv, r0, r1, axis=1)
                   for recv, r0, r1 in gs_list]
        res = _adamw(two_d(w[k]), two_d(m[k]), two_d(v[k]), gs_list, name="adamw_" + k)
        for nm, r in zip(("grad", "delta", "new_m", "new_v"), res):
            out[nm, k] = r.reshape(w[k].shape)

    small_grads = [jnp.stack(grads[k]) for k in SMALL]
    small_shapes = [g.shape for g in small_grads]
    sg = _all_gather(_pack_rows(small_grads, F32), name="gather_small_grads")
    gsum = _slot_sum(sg, name="sum_small_grads")
    gsmall = dict(zip(SMALL, _unpack_rows(gsum, small_shapes)))
    cshard = conv_w.shape[2]
    gsmall["conv_w"] = lax.dynamic_slice_in_dim(gsmall["conv_w"], me * cshard, cshard, axis=2)
    own_shapes = [w[k].shape for k in SMALL]
    gs_, ds_, ms_, vs_ = _adamw(_pack_rows([w[k] for k in SMALL], F32), _pack_rows([m[k] for k in SMALL], F32),
                                _pack_rows([v[k] for k in SMALL], F32),
                                [_pack_rows([gsmall[k] for k in SMALL], F32)[None]], name="adamw_small")
    for nm, packed in (("grad", gs_), ("delta", ds_), ("new_m", ms_), ("new_v", vs_)):
        for k, a in zip(SMALL, _unpack_rows(packed, own_shapes)):
            out[nm, k] = a

    res = [loss, dx[None]]
    for nm in ("grad", "delta", "new_m", "new_v"):
        res += [out[nm, k] for k in WEIGHTS]
    return tuple(res)
```

```python
import functools
import math

import jax
import jax.numpy as jnp
from jax import lax
from jax.experimental import pallas as pl
from jax.experimental.pallas import tpu as pltpu

F32 = jnp.float32
BF16 = jnp.bfloat16
MESH = pl.DeviceIdType.MESH

N_DEV = 8
LANES = 128
VMEM_LIMIT_BYTES = 56 * 1024 * 1024
EPS = 1e-6
ADAM_LR, ADAM_B1, ADAM_B2, ADAM_EPS, ADAM_WD, ADAM_STEP = 0.001, 0.9, 0.999, 1e-08, 0.01, 10
ATTN_FWD_TILES = (512, 512)
ATTN_BWD_TILES = (1024, 256)
ATTN_FWD_UNROLL = 4
ATTN_BWD_UNROLL = 4
BIG = ("w_in", "w_branch_out", "w_o", "w_gate_up", "w_down")
SMALL = ("mix_norm_g", "b_gate", "conv_w", "conv_b", "sgu_ln_g", "sgu_ln_b", "sgu_w", "sgu_b",
         "q_norm_g", "k_norm_g", "ffn_norm_g")
WEIGHTS = ("mix_norm_g", "w_in", "b_gate", "conv_w", "conv_b", "sgu_ln_g", "sgu_ln_b", "sgu_w", "sgu_b",
           "q_norm_g", "k_norm_g", "w_branch_out", "w_o", "ffn_norm_g", "w_gate_up", "w_down")
SHARD_AXIS = {"w_in": 2, "w_branch_out": 3, "w_o": 1, "w_gate_up": 2, "w_down": 1}


def _tile(n, cap, mult):
    best = None
    for t in range(mult, min(n, cap) + 1, mult):
        if n % t == 0:
            best = t
    return best if best is not None else n


def _params(*sem):
    return pltpu.CompilerParams(dimension_semantics=sem if sem else None, vmem_limit_bytes=VMEM_LIMIT_BYTES)


def _erf(x):
    return lax.erf(x)


def _gelu(x):
    return 0.5 * x * (1.0 + _erf(x * (1.0 / math.sqrt(2.0))))


def _gelu_grad(x):
    return 0.5 * (1.0 + _erf(x * (1.0 / math.sqrt(2.0)))) + x * jnp.exp(-0.5 * x * x) * (1.0 / math.sqrt(2.0 * math.pi))


def _sigmoid(x):
    return 1.0 / (1.0 + jnp.exp(-x))


def _matmul(a, b, *, name, res=None, out_dtype=F32, comm=None):
    M, K = a.shape
    _, N = b.shape
    tm, tn, tk = _tile(M, 1024, 8), _tile(N, 1536, LANES), _tile(K, 1536, LANES)
    nk = K // tk
    grid = (M // tm, N // tn, nk)
    has_res = res is not None

    def body(*refs):
        refs = list(refs)
        a_ref, b_ref = refs[:2]
        r_ref = refs[2] if has_res else None
        pos = 2 + has_res
        nc = len(comm[1]) if comm is not None else 0
        o_ref = refs[pos + nc]
        acc = refs[pos + 1 + 2 * nc] if nk > 1 else None
        if comm is not None:
            comm_end = _comm_begin(comm[0], refs[pos:pos + nc], refs[pos + nc + 1:pos + 2 * nc + 1],
                                   refs[pos + 2 * nc + 1 + (nk > 1):], grid)
        k = pl.program_id(2)
        part = jnp.dot(a_ref[...], b_ref[...], preferred_element_type=F32)

        def finish(v):
            if has_res:
                v = v + r_ref[...]
            o_ref[...] = v.astype(out_dtype)

        if nk == 1:
            finish(part)
        else:
            @pl.when(k == 0)
            def _():
                acc[...] = part

            @pl.when(jnp.logical_and(k > 0, k < nk - 1))
            def _():
                acc[...] += part

            @pl.when(k == nk - 1)
            def _():
                finish(acc[...] + part)

        if comm is not None:
            comm_end()

    in_specs = [pl.BlockSpec((tm, tk), lambda i, j, k: (i, k)), pl.BlockSpec((tk, tn), lambda i, j, k: (k, j))]
    args = [a, b]
    if has_res:
        in_specs.append(pl.BlockSpec((tm, tn), lambda i, j, k: (i, j)))
        args.append(res)
    out_specs = [pl.BlockSpec((tm, tn), lambda i, j, k: (i, j))]
    out_shape = [jax.ShapeDtypeStruct((M, N), out_dtype)]
    scratch = [pltpu.VMEM((tm, tn), F32)] if nk > 1 else []
    sem = ("parallel", "parallel", "arbitrary")
    if comm is not None:
        anyspec = pl.BlockSpec(memory_space=pl.ANY)
        in_specs += [anyspec] * len(comm[1])
        args += list(comm[1])
        out_specs += [anyspec] * len(comm[1])
        out_shape += [jax.ShapeDtypeStruct(c, x.dtype) for c, x in zip(comm[2], comm[1])]
        scratch += _comm_scratch(len(comm[1]))
        sem = ("arbitrary",) * 3
    out = pl.pallas_call(body, name=name, grid=grid, in_specs=in_specs, out_specs=out_specs, out_shape=out_shape,
                         scratch_shapes=scratch, compiler_params=_params(*sem))(*args)
    return out[0] if comm is None else (out[0], out[1:])


def _matmul_tn(x, y, *, name, out_dtype=F32, shard=None):
    S, A = x.shape
    _, B = y.shape
    ta, ts = _tile(A, 1536, LANES), _tile(S, 1024, 8)
    tb = shard if shard else _tile(B, 1536, LANES)
    ns = S // ts
    direct = out_dtype == F32
    view = (lambda r: r.at[0]) if shard else (lambda r: r)

    def body(x_ref, y_ref, o_ref, *scratch):
        s = pl.program_id(2)
        out = view(o_ref)
        acc = out if direct else scratch[0]
        part = lax.dot_general(x_ref[...], y_ref[...], (((0,), (0,)), ((), ())), preferred_element_type=F32)

        @pl.when(s == 0)
        def _():
            acc[...] = part

        @pl.when(s > 0)
        def _():
            acc[...] += part

        if not direct:
            @pl.when(s == ns - 1)
            def _():
                out[...] = acc[...].astype(out_dtype)

    if shard:
        out_spec, shape = pl.BlockSpec((1, ta, tb), lambda i, j, s: (j, i, 0)), (B // tb, A, tb)
    else:
        out_spec, shape = pl.BlockSpec((ta, tb), lambda i, j, s: (i, j)), (A, B)
    return pl.pallas_call(
        body, name=name, grid=(A // ta, B // tb, ns),
        in_specs=[pl.BlockSpec((ts, ta), lambda i, j, s: (s, i)), pl.BlockSpec((ts, tb), lambda i, j, s: (s, j))],
        out_specs=out_spec, out_shape=jax.ShapeDtypeStruct(shape, out_dtype),
        scratch_shapes=[] if direct else [pltpu.VMEM((ta, tb), F32)],
        compiler_params=_params("parallel", "parallel", "arbitrary"),
    )(x, y)


def _rmsnorm_fwd(x, g, *, scale, out_dtype, name):
    R, W = x.shape
    tr = _tile(R, 512 if W >= 512 else 4096, 8)

    def body(x_ref, g_ref, o_ref):
        xv = x_ref[...]
        r = lax.rsqrt(jnp.mean(xv * xv, axis=1, keepdims=True) + EPS)
        o_ref[...] = (xv * r * (g_ref[...] * scale)).astype(out_dtype)

    return pl.pallas_call(
        body, name=name, grid=(R // tr,),
        in_specs=[pl.BlockSpec((tr, W), lambda i: (i, 0)), pl.BlockSpec((1, W), lambda i: (0, 0))],
        out_specs=pl.BlockSpec((tr, W), lambda i: (i, 0)),
        out_shape=jax.ShapeDtypeStruct((R, W), out_dtype),
        compiler_params=_params("parallel"),
    )(x, g)


def _rmsnorm_bwd(x, g, dy, *, scale, name, dres=None, out_dtype=F32, bf16_copy=False):
    R, W = x.shape
    tr = _tile(R, 512 if W >= 512 else 4096, 8)
    has_res = dres is not None

    def body(*refs):
        refs = list(refs)
        dxb_ref = refs.pop() if bf16_copy else None
        if has_res:
            x_ref, g_ref, dy_ref, dres_ref, dx_ref, dg_ref = refs
        else:
            x_ref, g_ref, dy_ref, dx_ref, dg_ref = refs
        i = pl.program_id(0)
        xv = x_ref[...]
        dyv = dy_ref[...].astype(F32) * scale
        r = lax.rsqrt(jnp.mean(xv * xv, axis=1, keepdims=True) + EPS)
        u = dyv * g_ref[...]
        dx = r * u - xv * (r * r * r * jnp.mean(u * xv, axis=1, keepdims=True))
        if has_res:
            dx = dx + dres_ref[...]
        dx_ref[...] = dx.astype(out_dtype)
        if bf16_copy:
            dxb_ref[...] = dx.astype(BF16)
        part = jnp.sum(dyv * xv * r, axis=0, keepdims=True)

        @pl.when(i == 0)
        def _():
            dg_ref[...] = part

        @pl.when(i > 0)
        def _():
            dg_ref[...] += part

    row = pl.BlockSpec((tr, W), lambda i: (i, 0))
    one = pl.BlockSpec((1, W), lambda i: (0, 0))
    in_specs = [row, one, row] + ([row] if has_res else [])
    args = [x, g, dy] + ([dres] if has_res else [])
    extra = bool(bf16_copy)
    return pl.pallas_call(
        body, name=name, grid=(R // tr,), in_specs=in_specs, out_specs=[row, one] + [row] * extra,
        out_shape=[jax.ShapeDtypeStruct((R, W), out_dtype), jax.ShapeDtypeStruct((1, W), F32)]
        + [jax.ShapeDtypeStruct((R, W), BF16)] * extra,
        compiler_params=_params("arbitrary"),
    )(*args)


HALO = 16


def _shift_down(u, prev, n):
    out = pltpu.roll(u, n, 0)
    row = lax.broadcasted_iota(jnp.int32, u.shape, 0)
    for r in range(n):
        out = jnp.where(row == r, prev[HALO - n + r:HALO - n + r + 1, :], out)
    return out


def _shift_up(u, nxt, n):
    ts = u.shape[0]
    out = pltpu.roll(u, ts - n, 0)
    row = lax.broadcasted_iota(jnp.int32, u.shape, 0)
    for r in range(n):
        out = jnp.where(row == ts - n + r, nxt[r:r + 1, :], out)
    return out


def _conv_fwd(P, conv_w, conv_b, *, C, name):
    S = P.shape[0]
    ts = _tile(S, 512, HALO)
    hb = ts // HALO

    def body(ab_ref, ac_ref, ax_ref, pc_ref, px_ref, w_ref, b_ref, o_ref):
        i = pl.program_id(0)
        u = ac_ref[...].astype(F32) * ax_ref[...].astype(F32)
        prev = pc_ref[...].astype(F32) * px_ref[...].astype(F32) * (i > 0).astype(F32)
        w = w_ref[...]
        y = b_ref[...] + w[0:1, :] * _shift_down(u, prev, 2) + w[1:2, :] * _shift_down(u, prev, 1) + w[2:3, :] * u
        o_ref[...] = (ab_ref[...].astype(F32) * y).astype(BF16)

    cur = lambda c: pl.BlockSpec((ts, C), lambda i: (i, c))
    prv = lambda c: pl.BlockSpec((HALO, C), lambda i: (jnp.maximum(i * hb - 1, 0), c))
    return pl.pallas_call(
        body, name=name, grid=(S // ts,),
        in_specs=[cur(0), cur(1), cur(2), prv(1), prv(2),
                  pl.BlockSpec((8, C), lambda i: (0, 0)), pl.BlockSpec((1, C), lambda i: (0, 0))],
        out_specs=pl.BlockSpec((ts, C), lambda i: (i, 0)),
        out_shape=jax.ShapeDtypeStruct((S, C), BF16),
        compiler_params=_params("parallel"),
    )(P, P, P, P, P, conv_w, conv_b)


def _conv_bwd(P, dya, conv_w, conv_b, *, C, name):
    S = P.shape[0]
    ts = _tile(S, 512, HALO)
    hb = ts // HALO
    last = S // HALO - 1
    n = S // ts

    def body(ab_ref, ac_ref, ax_ref, pc_ref, px_ref, dy_ref, nab_ref, ndy_ref, w_ref, b_ref, o_ref, dw_ref):
        i = pl.program_id(0)
        ab, ac, ax = ab_ref[...].astype(F32), ac_ref[...].astype(F32), ax_ref[...].astype(F32)
        u = ac * ax
        prev = pc_ref[...].astype(F32) * px_ref[...].astype(F32) * (i > 0).astype(F32)
        w = w_ref[...]
        u1, u2 = _shift_down(u, prev, 1), _shift_down(u, prev, 2)
        y = b_ref[...] + w[0:1, :] * u2 + w[1:2, :] * u1 + w[2:3, :] * u
        dya_v = dy_ref[...]
        dyp = dya_v * ab
        nxt = ndy_ref[...] * nab_ref[...].astype(F32) * (i < n - 1).astype(F32)
        du = w[2:3, :] * dyp + w[1:2, :] * _shift_up(dyp, nxt, 1) + w[0:1, :] * _shift_up(dyp, nxt, 2)
        o_ref[:, 0:C] = (dya_v * y).astype(BF16)
        o_ref[:, C:2 * C] = (du * ax).astype(BF16)
        o_ref[:, 2 * C:3 * C] = (du * ac).astype(BF16)
        part = jnp.concatenate([
            jnp.sum(dyp * u2, axis=0, keepdims=True), jnp.sum(dyp * u1, axis=0, keepdims=True),
            jnp.sum(dyp * u, axis=0, keepdims=True), jnp.sum(dyp, axis=0, keepdims=True),
            jnp.zeros((4, C), F32)], axis=0)

        @pl.when(i == 0)
        def _():
            dw_ref[...] = part

        @pl.when(i > 0)
        def _():
            dw_ref[...] += part

    cur = lambda c: pl.BlockSpec((ts, C), lambda i: (i, c))
    prv = lambda c: pl.BlockSpec((HALO, C), lambda i: (jnp.maximum(i * hb - 1, 0), c))
    nxt = lambda c: pl.BlockSpec((HALO, C), lambda i: (jnp.minimum((i + 1) * hb, last), c))
    return pl.pallas_call(
        body, name=name, grid=(n,),
        in_specs=[cur(0), cur(1), cur(2), prv(1), prv(2), cur(0), nxt(0), nxt(0),
                  pl.BlockSpec((8, C), lambda i: (0, 0)), pl.BlockSpec((1, C), lambda i: (0, 0))],
        out_specs=[pl.BlockSpec((ts, 3 * C), lambda i: (i, 0)), pl.BlockSpec((8, C), lambda i: (0, 0))],
        out_shape=[jax.ShapeDtypeStruct((S, 3 * C), BF16), jax.ShapeDtypeStruct((8, C), F32)],
        compiler_params=_params("arbitrary"),
    )(P, P, P, P, P, dya, P, dya, conv_w, conv_b)


def _sgu_fwd(P, ln_g, ln_b, wm, bT, *, W, cu, cv, name):
    S = P.shape[0]
    G, CH, _ = wm.shape
    gw = W // G
    ts = _tile(S, 512, CH)

    def body(u_ref, v_ref, g_ref, b_ref, wm_ref, bT_ref, o_ref):
        gv = _gelu(v_ref[...].astype(F32))
        mu = jnp.mean(gv, axis=1, keepdims=True)
        xc = gv - mu
        vn = (xc * lax.rsqrt(jnp.mean(xc * xc, axis=1, keepdims=True) + EPS) * g_ref[...] + b_ref[...]).astype(BF16)
        bT_v = bT_ref[...]
        for c in range(ts // CH):
            rows = slice(c * CH, (c + 1) * CH)
            for g in range(G):
                cols = slice(g * gw, (g + 1) * gw)
                mixed = jnp.dot(wm_ref[g], vn[rows, cols], preferred_element_type=F32) + bT_v[:, g:g + 1]
                o_ref[rows, cols] = (_gelu(u_ref[rows, cols].astype(F32)) * mixed).astype(BF16)

    full = lambda shp: pl.BlockSpec(shp, lambda i: (0,) * len(shp))
    return pl.pallas_call(
        body, name=name, grid=(S // ts,),
        in_specs=[pl.BlockSpec((ts, W), lambda i: (i, cu)), pl.BlockSpec((ts, W), lambda i: (i, cv)),
                  full((1, W)), full((1, W)), full((G, CH, CH)), full((CH, G))],
        out_specs=pl.BlockSpec((ts, W), lambda i: (i, 0)),
        out_shape=jax.ShapeDtypeStruct((S, W), BF16),
        compiler_params=_params("parallel"),
    )(P, P, ln_g, ln_b, wm, bT)


def _sgu_bwd(P, dyb, ln_g, ln_b, wm, wmT, bT, *, W, cu, cv, name):
    S = P.shape[0]
    G, CH, _ = wm.shape
    gw = W // G
    ts = _tile(S, 512, CH)

    def body(u_ref, v_ref, dy_ref, g_ref, b_ref, wm_ref, wmT_ref, bT_ref, o_ref, dw_ref, db_ref, dln_ref, dvn_ref):
        i = pl.program_id(0)

        @pl.when(i == 0)
        def _():
            dw_ref[...] = jnp.zeros_like(dw_ref)
            db_ref[...] = jnp.zeros_like(db_ref)
            dln_ref[...] = jnp.zeros_like(dln_ref)

        sv = v_ref[...].astype(F32)
        gv = _gelu(sv)
        mu = jnp.mean(gv, axis=1, keepdims=True)
        xc = gv - mu
        rstd = lax.rsqrt(jnp.mean(xc * xc, axis=1, keepdims=True) + EPS)
        xhat = xc * rstd
        lg = g_ref[...]
        vn = (xhat * lg + b_ref[...]).astype(BF16)
        bT_v = bT_ref[...]
        for c in range(ts // CH):
            rows = slice(c * CH, (c + 1) * CH)
            for g in range(G):
                cols = slice(g * gw, (g + 1) * gw)
                vn_cg = vn[rows, cols]
                mixed = jnp.dot(wm_ref[g], vn_cg, preferred_element_type=F32) + bT_v[:, g:g + 1]
                su = u_ref[rows, cols].astype(F32)
                dyv = dy_ref[rows, cols]
                dmix = dyv * _gelu(su)
                o_ref[rows, cols] = (dyv * mixed * _gelu_grad(su)).astype(BF16)
                dmix_b = dmix.astype(BF16)
                dw_ref[g] += lax.dot_general(dmix_b, vn_cg, (((1,), (1,)), ((), ())), preferred_element_type=F32)
                db_ref[g] += jnp.broadcast_to(jnp.sum(dmix, axis=1, keepdims=True), (CH, CH))
                dvn_ref[rows, cols] = jnp.dot(wmT_ref[g], dmix_b, preferred_element_type=F32)
        dvn = dvn_ref[...]
        dxh = dvn * lg
        dgv = rstd * (dxh - jnp.mean(dxh, axis=1, keepdims=True) - xhat * jnp.mean(dxh * xhat, axis=1, keepdims=True))
        o_ref[:, W:2 * W] = (dgv * _gelu_grad(sv)).astype(BF16)
        dln_ref[0:1, :] += jnp.sum(dvn * xhat, axis=0, keepdims=True)
        dln_ref[1:2, :] += jnp.sum(dvn, axis=0, keepdims=True)

    full = lambda shp: pl.BlockSpec(shp, lambda i: (0,) * len(shp))
    return pl.pallas_call(
        body, name=name, grid=(S // ts,),
        in_specs=[pl.BlockSpec((ts, W), lambda i: (i, cu)), pl.BlockSpec((ts, W), lambda i: (i, cv)),
                  pl.BlockSpec((ts, W), lambda i: (i, 0)),
                  full((1, W)), full((1, W)), full((G, CH, CH)), full((G, CH, CH)), full((CH, G))],
        out_specs=[pl.BlockSpec((ts, 2 * W), lambda i: (i, 0)), full((G, CH, CH)), full((G, CH, CH)), full((8, W))],
        out_shape=[jax.ShapeDtypeStruct((S, 2 * W), BF16), jax.ShapeDtypeStruct((G, CH, CH), F32),
                   jax.ShapeDtypeStruct((G, CH, CH), F32), jax.ShapeDtypeStruct((8, W), F32)],
        scratch_shapes=[pltpu.VMEM((ts, W), F32)],
        compiler_params=_params("arbitrary"),
    )(P, P, dyb, ln_g, ln_b, wm, wmT, bT)


def _block_sums(x, u, parts=1):
    hi = x.astype(BF16)
    out = jnp.dot(hi, u, preferred_element_type=F32)
    if parts == 2:
        lo = (x - hi.astype(F32)).astype(BF16)
        out = out + jnp.dot(lo, u, preferred_element_type=F32)
    return out


_NT = (((1,), (1,)), ((), ()))
_TN = (((0,), (0,)), ((), ()))


def _left_blocks(step, jd, carry, unroll, jd_multiple):
    rem = 0
    if jd_multiple % unroll:
        rem = jd % unroll
        carry = lax.fori_loop(0, rem, lambda t, c: step(jd - 1 - t, c, False), carry)

    def trip(t, c):
        for s in range(unroll):
            c = step(jd - rem - 1 - s - unroll * t, c, False)
        return c

    return lax.fori_loop(0, jd // unroll, trip, carry)


def _diag_step(step, j, carry, row0):
    if row0 == 0:
        return step(j, carry, True)
    tail = step(j, tuple(c[row0:] for c in carry), True, row0)
    return tuple(jnp.concatenate([c[:row0], t], axis=0) for c, t in zip(carry, tail))


def _qkv_heads_fwd(P, gq, gk, *, A, col0, hd, qscale, name):
    S = P.shape[0]
    H = A // hd
    ts = _tile(S, 512, 8)

    def body(q_ref, k_ref, v_ref, gq_ref, gk_ref, qn_ref, kn_ref, vh_ref):
        for h in range(H):
            cols = slice(h * hd, (h + 1) * hd)
            for x_ref, g_ref, sc, o_ref in ((q_ref, gq_ref, qscale, qn_ref), (k_ref, gk_ref, 1.0, kn_ref)):
                xh = x_ref[:, cols].astype(F32)
                r = lax.rsqrt(jnp.mean(xh * xh, axis=1, keepdims=True) + EPS)
                o_ref[h] = (xh * r * (g_ref[...] * sc)).astype(BF16)
            vh_ref[h] = v_ref[:, cols].astype(BF16)

    col = lambda n: pl.BlockSpec((ts, A), lambda i: (i, col0 + n))
    gsp = pl.BlockSpec((1, hd), lambda i: (0, 0))
    hsp = pl.BlockSpec((H, ts, hd), lambda i: (0, i, 0))
    shp = jax.ShapeDtypeStruct((H, S, hd), BF16)
    return pl.pallas_call(
        body, name=name, grid=(S // ts,), in_specs=[col(0), col(1), col(2), gsp, gsp],
        out_specs=[hsp, hsp, hsp], out_shape=[shp, shp, shp], compiler_params=_params("parallel"),
    )(P, P, P, gq, gk)


def _qkv_heads_bwd(P, gq, gk, dqn, dkn, dvh, *, A, col0, hd, qscale, name):
    S = P.shape[0]
    H = A // hd
    ts = _tile(S, 512, 8)

    def body(q_ref, k_ref, gq_ref, gk_ref, dq_ref, dk_ref, dv_ref, o_ref, dgq_ref, dgk_ref):
        i = pl.program_id(0)
        parts = [jnp.zeros((1, hd), F32), jnp.zeros((1, hd), F32)]
        for h in range(H):
            for n, (x_ref, g_ref, sc, d_ref) in enumerate(((q_ref, gq_ref, qscale, dq_ref), (k_ref, gk_ref, 1.0, dk_ref))):
                xh = x_ref[:, h * hd:(h + 1) * hd].astype(F32)
                dyv = d_ref[h] * sc
                r = lax.rsqrt(jnp.mean(xh * xh, axis=1, keepdims=True) + EPS)
                u = dyv * g_ref[...]
                dx = r * u - xh * (r * r * r * jnp.mean(u * xh, axis=1, keepdims=True))
                o_ref[:, n * A + h * hd:n * A + (h + 1) * hd] = dx.astype(BF16)
                parts[n] = parts[n] + jnp.sum(dyv * xh * r, axis=0, keepdims=True)
            o_ref[:, 2 * A + h * hd:2 * A + (h + 1) * hd] = dv_ref[h].astype(BF16)

        @pl.when(i == 0)
        def _():
            dgq_ref[...] = parts[0]
            dgk_ref[...] = parts[1]

        @pl.when(i > 0)
        def _():
            dgq_ref[...] += parts[0]
            dgk_ref[...] += parts[1]

    col = lambda n: pl.BlockSpec((ts, A), lambda i: (i, col0 + n))
    gsp = pl.BlockSpec((1, hd), lambda i: (0, 0))
    hsp = pl.BlockSpec((H, ts, hd), lambda i: (0, i, 0))
    return pl.pallas_call(
        body, name=name, grid=(S // ts,), in_specs=[col(0), col(1), gsp, gsp, hsp, hsp, hsp],
        out_specs=[pl.BlockSpec((ts, 3 * A), lambda i: (i, 0)), gsp, gsp],
        out_shape=[jax.ShapeDtypeStruct((S, 3 * A), BF16), jax.ShapeDtypeStruct((1, hd), F32),
                   jax.ShapeDtypeStruct((1, hd), F32)],
        compiler_params=_params("arbitrary"),
    )(P, P, gq, gk, dqn, dkn, dvh)


def _comm_begin(schedule, cin_refs, cout_refs, sems, grid):
    begin, forward, finish = _comm_phases(schedule, cin_refs, cout_refs, *sems)
    ids = [pl.program_id(d) for d in range(len(grid))]
    at = lambda where: functools.reduce(jnp.logical_and, [p == w for p, w in zip(ids, where)])
    pl.when(at([0] * len(grid)))(begin)

    def comm_end():
        pl.when(at([grid[0] // 2] + [0] * (len(grid) - 1)))(forward)
        pl.when(at([g - 1 for g in grid]))(finish)

    return comm_end


def _call_with_comm(body, name, grid, in_specs, out_specs, out_shape, args, comm):
    if comm is None:
        return pl.pallas_call(body, name=name, grid=grid, in_specs=in_specs, out_specs=out_specs, out_shape=out_shape,
                              compiler_params=_params("parallel", "arbitrary"))(*args)
    _, xs, cshapes = comm
    anyspec = pl.BlockSpec(memory_space=pl.ANY)
    return pl.pallas_call(
        body, name=name, grid=grid, in_specs=in_specs + [anyspec] * len(xs), out_specs=out_specs + [anyspec] * len(xs),
        out_shape=out_shape + [jax.ShapeDtypeStruct(c, x.dtype) for c, x in zip(cshapes, xs)],
        scratch_shapes=_comm_scratch(len(xs)), compiler_params=_params("arbitrary", "arbitrary"),
    )(*args, *xs)


def _attn_fwd(q, k, v, umat, *, tq, tk, name, comm=None):
    H, S, hd = q.shape

    def body(*refs):
        if comm is None:
            q_ref, k_ref, v_ref, u_ref, o_ref = refs
        else:
            nc = len(comm[1])
            q_ref, k_ref, v_ref, u_ref = refs[:4]
            o_ref = refs[4 + nc]
            comm_end = _comm_begin(comm[0], refs[4:4 + nc], refs[5 + nc:5 + 2 * nc], refs[5 + 2 * nc:], (H, S // tq))
        i = pl.program_id(1)
        qb = q_ref[0]
        um = u_ref[...]
        qpos = lax.broadcasted_iota(jnp.int32, (tq, tk), 0) + i * tq
        kloc = lax.broadcasted_iota(jnp.int32, (tq, tk), 1)

        def step(j, carry, masked, row0=0):
            r, acc = carry
            ks = pl.multiple_of(j * tk, tk)
            kb = k_ref[0, pl.ds(ks, tk), :]
            vb = v_ref[0, pl.ds(ks, tk), :]
            z = lax.dot_general(qb[row0:], kb, _NT, preferred_element_type=F32)
            lb = jnp.minimum(z, 0.0) - jnp.log(1.0 + jnp.exp(-jnp.abs(z)))
            lm = lb - z
            if masked:
                m = (kloc[row0:] + j * tk) < qpos[row0:]
                lm = jnp.where(m, lm, 0.0)
            a = jnp.exp(lb + _block_sums(lm, um) + r)
            if masked:
                a = jnp.where(m, a, 0.0)
            acc = acc + jnp.dot(a.astype(BF16), vb, preferred_element_type=F32)
            return r + jnp.sum(lm, axis=1, keepdims=True), acc

        jd = (i * tq) // tk
        carry = (jnp.zeros((tq, 1), F32), jnp.zeros((tq, hd), F32))
        for dd in reversed(range(max(1, tq // tk))):
            carry = _diag_step(step, jd + dd, carry, dd * tk if tq > tk else 0)
        carry = _left_blocks(step, jd, carry, ATTN_FWD_UNROLL, tq // tk)
        o_ref[0] = carry[1]
        if comm is not None:
            comm_end()

    blk = pl.BlockSpec((1, tq, hd), lambda h, i: (h, i, 0))
    whole = pl.BlockSpec((1, S, hd), lambda h, i: (h, 0, 0))
    in_specs = [blk, whole, whole, pl.BlockSpec((tk, tk), lambda h, i: (0, 0))]
    out_specs, out_shape = [blk], [jax.ShapeDtypeStruct((H, S, hd), F32)]
    res = _call_with_comm(body, name, (H, S // tq), in_specs, out_specs, out_shape, (q, k, v, umat), comm)
    return res[0] if comm is None else (res[0], res[1:])


def _attn_bwd(q, k, v, o, do, umat, *, tq, tk, name, comm=None):
    H, S, hd = q.shape

    def body(*refs):
        if comm is None:
            q_ref, k_ref, v_ref, o_ref, do_ref, u_ref, dq_ref, dk_ref, dv_ref = refs
        else:
            nc = len(comm[1])
            q_ref, k_ref, v_ref, o_ref, do_ref, u_ref = refs[:6]
            dq_ref, dk_ref, dv_ref = refs[6 + nc:9 + nc]
            comm_end = _comm_begin(comm[0], refs[6:6 + nc], refs[9 + nc:9 + 2 * nc], refs[9 + 2 * nc:], (H, S // tq))
        i = pl.program_id(1)

        @pl.when(i == 0)
        def _():
            dk_ref[...] = jnp.zeros_like(dk_ref)
            dv_ref[...] = jnp.zeros_like(dv_ref)

        qb = q_ref[0]
        do32 = do_ref[0]
        dob = do32.astype(BF16)
        tot = jnp.sum(dob.astype(F32) * o_ref[0], axis=1, keepdims=True)
        um = u_ref[...]
        qpos = lax.broadcasted_iota(jnp.int32, (tq, tk), 0) + i * tq
        kloc = lax.broadcasted_iota(jnp.int32, (tq, tk), 1)

        def step(j, carry, masked, row0=0):
            r, gs, dq = carry
            ks = pl.multiple_of(j * tk, tk)
            kb = k_ref[0, pl.ds(ks, tk), :]
            vb = v_ref[0, pl.ds(ks, tk), :]
            qs, dos = qb[row0:], dob[row0:]
            z = lax.dot_general(qs, kb, _NT, preferred_element_type=F32)
            lb = jnp.minimum(z, 0.0) - jnp.log(1.0 + jnp.exp(-jnp.abs(z)))
            sig = jnp.exp(lb)
            lm = lb - z
            if masked:
                m = (kloc[row0:] + j * tk) < qpos[row0:]
                lm = jnp.where(m, lm, 0.0)
            a = jnp.exp(lb + _block_sums(lm, um) + r)
            if masked:
                a = jnp.where(m, a, 0.0)
            ab = a.astype(BF16)
            g = lax.dot_general(dos, vb, _NT, preferred_element_type=F32) * ab.astype(F32)
            dz = g - sig * ((tot[row0:] - gs) - _block_sums(g, um, parts=2))
            if masked:
                dz = jnp.where(m, dz, 0.0)
            dzb = dz.astype(BF16)
            dq = dq + jnp.dot(dzb, kb, preferred_element_type=F32)
            dk_ref[0, pl.ds(ks, tk), :] += lax.dot_general(dzb, qs, _TN, preferred_element_type=F32)
            dv_ref[0, pl.ds(ks, tk), :] += lax.dot_general(ab, dos, _TN, preferred_element_type=F32)
            return r + jnp.sum(lm, axis=1, keepdims=True), gs + jnp.sum(g, axis=1, keepdims=True), dq

        jd = (i * tq) // tk
        zero = jnp.zeros((tq, 1), F32)
        carry = (zero, zero, jnp.zeros((tq, hd), F32))
        for dd in reversed(range(max(1, tq // tk))):
            carry = _diag_step(step, jd + dd, carry, dd * tk if tq > tk else 0)
        carry = _left_blocks(step, jd, carry, ATTN_BWD_UNROLL, tq // tk)
        dq_ref[0] = carry[2]
        if comm is not None:
            comm_end()

    blk = pl.BlockSpec((1, tq, hd), lambda h, i: (h, i, 0))
    whole = pl.BlockSpec((1, S, hd), lambda h, i: (h, 0, 0))
    shp = jax.ShapeDtypeStruct((H, S, hd), F32)
    in_specs = [blk, whole, whole, blk, blk, pl.BlockSpec((tk, tk), lambda h, i: (0, 0))]
    res = _call_with_comm(body, name, (H, S // tq), in_specs, [blk, whole, whole], [shp, shp, shp],
                          (q, k, v, o, do, umat), comm)
    return res if comm is None else (*res[:3], res[3:])


def _merge_fwd(ya, yb, o, wb, P, b_gate, *, gate_col0, name):
    S, C = ya.shape
    H, _, hd = o.shape
    D = wb.shape[2]
    ts = _tile(S, 512, 8)

    def body(y0, y1, oh_ref, wb_ref, g0, g1, g2, bg_ref, m_ref, yc_ref):
        for h in range(H):
            yc_ref[:, h * hd:(h + 1) * hd] = oh_ref[h].astype(BF16)
        acc = jnp.zeros((ts, D), F32)
        for n, (y_ref, g_ref) in enumerate(((y0, g0), (y1, g1), (yc_ref, g2))):
            yd = jnp.dot(y_ref[...], wb_ref[n], preferred_element_type=F32)
            acc = acc + _sigmoid(g_ref[...].astype(F32) + bg_ref[:, n * D:(n + 1) * D]) * yd
        m_ref[...] = acc.astype(BF16)

    ysp = pl.BlockSpec((ts, C), lambda i: (i, 0))
    gsp = lambda n: pl.BlockSpec((ts, D), lambda i: (i, gate_col0 + n))
    return pl.pallas_call(
        body, name=name, grid=(S // ts,),
        in_specs=[ysp, ysp, pl.BlockSpec((H, ts, hd), lambda i: (0, i, 0)), pl.BlockSpec((3, C, D), lambda i: (0, 0, 0)),
                  gsp(0), gsp(1), gsp(2), pl.BlockSpec((1, 3 * D), lambda i: (0, 0))],
        out_specs=[pl.BlockSpec((ts, D), lambda i: (i, 0)), ysp],
        out_shape=[jax.ShapeDtypeStruct((S, D), BF16), jax.ShapeDtypeStruct((S, C), BF16)],
        compiler_params=_params("parallel"),
    )(ya, yb, o, wb, P, P, P, b_gate)


def _merge_bwd(ys, wb, wbT, P, b_gate, dmerged, *, gate_col0, hd, name):
    S, C = ys[0].shape
    D = wb.shape[2]
    H = C // hd
    ts = _tile(S, 512, 8)

    def body(y0, y1, y2, wb_ref, wbT_ref, g0, g1, g2, bg_ref, dm_ref,
             dg_ref, dyd0, dyd1, dyd2, dya_ref, dyb_ref, do_ref, dbg_ref):
        i = pl.program_id(0)
        dm = dm_ref[...].astype(F32)
        parts = []
        for n, (y_ref, g_ref, dyd_ref) in enumerate(((y0, g0, dyd0), (y1, g1, dyd1), (y2, g2, dyd2))):
            yd = jnp.dot(y_ref[...], wb_ref[n], preferred_element_type=F32)
            sg = _sigmoid(g_ref[...].astype(F32) + bg_ref[:, n * D:(n + 1) * D])
            dgate = dm * yd * sg * (1.0 - sg)
            dg_ref[:, n * D:(n + 1) * D] = dgate.astype(BF16)
            parts.append(jnp.sum(dgate, axis=0, keepdims=True))
            dyd = (dm * sg).astype(BF16)
            dyd_ref[...] = dyd
            dy = jnp.dot(dyd, wbT_ref[n], preferred_element_type=F32)
            if n == 0:
                dya_ref[...] = dy
            elif n == 1:
                dyb_ref[...] = dy
            else:
                for h in range(H):
                    do_ref[h] = dy[:, h * hd:(h + 1) * hd]
        part = jnp.concatenate(parts, axis=1)

        @pl.when(i == 0)
        def _():
            dbg_ref[...] = part

        @pl.when(i > 0)
        def _():
            dbg_ref[...] += part

    ysp = pl.BlockSpec((ts, C), lambda i: (i, 0))
    dsp = pl.BlockSpec((ts, D), lambda i: (i, 0))
    gsp = lambda n: pl.BlockSpec((ts, D), lambda i: (i, gate_col0 + n))
    dshp = jax.ShapeDtypeStruct((S, D), BF16)
    yshp = jax.ShapeDtypeStruct((S, C), F32)
    return pl.pallas_call(
        body, name=name, grid=(S // ts,),
        in_specs=[ysp, ysp, ysp, pl.BlockSpec((3, C, D), lambda i: (0, 0, 0)),
                  pl.BlockSpec((3, D, C), lambda i: (0, 0, 0)), gsp(0), gsp(1), gsp(2),
                  pl.BlockSpec((1, 3 * D), lambda i: (0, 0)), dsp],
        out_specs=[pl.BlockSpec((ts, 3 * D), lambda i: (i, 0)), dsp, dsp, dsp, ysp, ysp,
                   pl.BlockSpec((H, ts, hd), lambda i: (0, i, 0)), pl.BlockSpec((1, 3 * D), lambda i: (0, 0))],
        out_shape=[jax.ShapeDtypeStruct((S, 3 * D), BF16), dshp, dshp, dshp, yshp, yshp,
                   jax.ShapeDtypeStruct((H, S, hd), F32), jax.ShapeDtypeStruct((1, 3 * D), F32)],
        compiler_params=_params("arbitrary"),
    )(*ys, wb, wbT, P, P, P, b_gate, dmerged)


def _swiglu_fwd(gu, *, name):
    S, F2 = gu.shape
    F = F2 // 2
    ts, tf = _tile(S, 512, 8), _tile(F, 1536, LANES)
    nf = F // tf

    def body(g_ref, u_ref, o_ref):
        gt = g_ref[...].astype(F32)
        o_ref[...] = (gt * _sigmoid(gt) * u_ref[...].astype(F32)).astype(BF16)

    return pl.pallas_call(
        body, name=name, grid=(S // ts, nf),
        in_specs=[pl.BlockSpec((ts, tf), lambda i, j: (i, j)), pl.BlockSpec((ts, tf), lambda i, j: (i, j + nf))],
        out_specs=pl.BlockSpec((ts, tf), lambda i, j: (i, j)),
        out_shape=jax.ShapeDtypeStruct((S, F), BF16),
        compiler_params=_params("parallel", "parallel"),
    )(gu, gu)


def _swiglu_bwd(gu, dact, *, name):
    S, F2 = gu.shape
    F = F2 // 2
    ts = _tile(S, 512, 8)

    def body(gu_ref, d_ref, o_ref):
        gt, up, da = gu_ref[:, 0:F].astype(F32), gu_ref[:, F:F2].astype(F32), d_ref[...].astype(F32)
        sg = _sigmoid(gt)
        o_ref[:, 0:F] = (da * up * sg * (1.0 + gt * (1.0 - sg))).astype(BF16)
        o_ref[:, F:F2] = (da * gt * sg).astype(BF16)

    return pl.pallas_call(
        body, name=name, grid=(S // ts,),
        in_specs=[pl.BlockSpec((ts, F2), lambda i: (i, 0)), pl.BlockSpec((ts, F), lambda i: (i, 0))],
        out_specs=pl.BlockSpec((ts, F2), lambda i: (i, 0)),
        out_shape=jax.ShapeDtypeStruct((S, F2), BF16),
        compiler_params=_params("parallel"),
    )(gu, dact)


def _loss_grad(y, target, *, name):
    S, D = y.shape
    ts = _tile(S, 512, 8)

    def body(y_ref, t_ref, dy_ref, l_ref, dyb_ref):
        i = pl.program_id(0)
        err = y_ref[...] - t_ref[...]
        dy_ref[...] = err * (1.0 / D)
        dyb_ref[...] = (err * (1.0 / D)).astype(BF16)
        part = jnp.broadcast_to(jnp.sum(jnp.sum(err * err, axis=1, keepdims=True), axis=0, keepdims=True) * (0.5 / D),
                                (1, LANES))

        @pl.when(i == 0)
        def _():
            l_ref[...] = part

        @pl.when(i > 0)
        def _():
            l_ref[...] += part

    row = pl.BlockSpec((ts, D), lambda i: (i, 0))
    return pl.pallas_call(
        body, name=name, grid=(S // ts,), in_specs=[row, row],
        out_specs=[row, pl.BlockSpec((1, LANES), lambda i: (0, 0)), row],
        out_shape=[jax.ShapeDtypeStruct((S, D), F32), jax.ShapeDtypeStruct((1, LANES), F32),
                   jax.ShapeDtypeStruct((S, D), BF16)],
        compiler_params=_params("arbitrary"),
    )(y, target)


def _adamw(w, m, v, gs_list, *, name):
    R, W = w.shape
    rows = [g.shape[1] for g in gs_list]
    tr = _tile(math.gcd(*rows), max(16, (2048 * LANES // W) // 16 * 16), 16)
    first = [sum(rows[:t]) // tr for t in range(len(rows))]
    c1 = 1.0 / (1.0 - ADAM_B1 ** ADAM_STEP)
    c2 = 1.0 / (1.0 - ADAM_B2 ** ADAM_STEP)

    def body(w_ref, m_ref, v_ref, *refs):
        gs_refs, (g_ref, d_ref, nm_ref, nv_ref) = refs[:len(rows)], refs[len(rows):]
        i = pl.program_id(0)
        g = None
        for t, gs_ref in enumerate(gs_refs):
            gt = gs_ref[0].astype(F32)
            for s in range(1, gs_ref.shape[0]):
                gt = gt + gs_ref[s].astype(F32)
            g = gt if g is None else jnp.where(i >= first[t], gt, g)
        nm = ADAM_B1 * m_ref[...] + (1.0 - ADAM_B1) * g
        nv = ADAM_B2 * v_ref[...] + (1.0 - ADAM_B2) * (g * g)
        g_ref[...] = g
        nm_ref[...] = nm
        nv_ref[...] = nv
        d_ref[...] = -ADAM_LR * ((nm * c1) / (jnp.sqrt(nv * c2) + ADAM_EPS) + ADAM_WD * w_ref[...])

    row = pl.BlockSpec((tr, W), lambda i: (i, 0))
    slots = [pl.BlockSpec((g.shape[0], tr, W),
                          lambda i, b0=first[t], nb=rows[t] // tr: (0, jnp.clip(i - b0, 0, nb - 1), 0))
             for t, g in enumerate(gs_list)]
    shp = jax.ShapeDtypeStruct((R, W), F32)
    return pl.pallas_call(
        body, name=name, grid=(R // tr,), in_specs=[row, row, row] + slots,
        out_specs=[row, row, row, row], out_shape=[shp, shp, shp, shp],
        compiler_params=_params("parallel"),
    )(w, m, v, *gs_list)


def _slot_sum(gs, *, name):
    ns, R, _ = gs.shape
    tr = _tile(R, 2048, 16)

    def body(gs_ref, o_ref):
        g = gs_ref[0]
        for s in range(1, ns):
            g = g + gs_ref[s]
        o_ref[...] = g

    return pl.pallas_call(
        body, name=name, grid=(R // tr,),
        in_specs=[pl.BlockSpec((ns, tr, LANES), lambda i: (0, i, 0))],
        out_specs=pl.BlockSpec((tr, LANES), lambda i: (i, 0)),
        out_shape=jax.ShapeDtypeStruct((R, LANES), F32),
        compiler_params=_params("parallel"),
    )(gs)


def _comm_scratch(n):
    return [pltpu.SemaphoreType.DMA((7 * n,)), pltpu.SemaphoreType.DMA((7 * n,)), pltpu.SemaphoreType.DMA((n,))]


def _gather_schedule(x_ref, out_ref, send_sems, recv_sems, local_sem, base):
    x, y, c = lax.axis_index("x"), lax.axis_index("y"), lax.axis_index("c")
    me, sibling = (x, y, c), (x, y, 1 - c)
    chips = [(1 - x, y), (x, 1 - y), (1 - x, 1 - y)]

    def slot(px, py, pc):
        return out_ref.at[4 * px + 2 * py + pc]

    def copy(k, block, to, src=None):
        return pltpu.make_async_remote_copy(
            src_ref=slot(*block) if src is None else src, dst_ref=slot(*block),
            send_sem=send_sems.at[base + k], recv_sem=recv_sems.at[base + k], device_id=to, device_id_type=MESH)

    mine = pltpu.make_async_copy(x_ref, slot(*me), local_sem)
    first = [copy(0, me, sibling, src=x_ref)]
    first += [copy(1 + j, me, (*chip, c), src=x_ref) for j, chip in enumerate(chips)]
    passed = [copy(4 + j, (*chip, c), sibling) for j, chip in enumerate(chips)]

    def begin():
        mine.start()
        for cp in first:
            cp.start()

    def forward():
        for j, chip in enumerate(chips):
            copy(1 + j, (*chip, c), me).wait_recv()
            passed[j].start()

    def finish():
        copy(0, sibling, me).wait_recv()
        for j, chip in enumerate(chips):
            copy(4 + j, (*chip, 1 - c), me).wait_recv()
        for cp in first + passed:
            cp.wait_send()
        mine.wait()

    return begin, forward, finish


def _exchange_schedule(x_ref, out_ref, send_sems, recv_sems, local_sem, base):
    x, y, c = lax.axis_index("x"), lax.axis_index("y"), lax.axis_index("c")
    me = 4 * x + 2 * y + c
    mine = pltpu.make_async_copy(x_ref.at[me], out_ref.at[me], local_sem)
    sends, recvs = [], []
    for k in range(1, N_DEV):
        px = 1 - x if k & 4 else x
        py = 1 - y if k & 2 else y
        pc = 1 - c if k & 1 else c
        peer = 4 * px + 2 * py + pc
        sems = dict(send_sem=send_sems.at[base + k - 1], recv_sem=recv_sems.at[base + k - 1],
                    device_id=(px, py, pc), device_id_type=MESH)
        sends.append(pltpu.make_async_remote_copy(src_ref=x_ref.at[peer], dst_ref=out_ref.at[me], **sems))
        recvs.append(pltpu.make_async_remote_copy(src_ref=x_ref.at[me], dst_ref=out_ref.at[peer], **sems))

    def begin():
        mine.start()
        for cp in sends:
            cp.start()

    def forward():
        pass

    def finish():
        for cp in recvs:
            cp.wait_recv()
        for cp in sends:
            cp.wait_send()
        mine.wait()

    return begin, forward, finish


def _comm_phases(schedule, x_refs, out_refs, send_sems, recv_sems, local_sems):
    parts = [schedule(x, o, send_sems, recv_sems, local_sems.at[t], 7 * t)
             for t, (x, o) in enumerate(zip(x_refs, out_refs))]

    def every(phase):
        def go():
            for p in parts:
                p[phase]()
        return go

    return every(0), every(1), every(2)


def _all_gather(xs, *, name):
    def body(x_ref, out_ref, send_sems, recv_sems, local_sems):
        for phase in _comm_phases(_gather_schedule, [x_ref], [out_ref], send_sems, recv_sems, local_sems):
            phase()

    return pl.pallas_call(
        body, name=name,
        in_specs=[pl.BlockSpec(memory_space=pl.ANY)], out_specs=pl.BlockSpec(memory_space=pl.ANY),
        out_shape=jax.ShapeDtypeStruct((N_DEV,) + xs.shape, xs.dtype), scratch_shapes=_comm_scratch(1),
        compiler_params=pltpu.CompilerParams(has_side_effects=True),
    )(xs)


PACK_ROWS = 16


def _pack_rows(parts, dtype, lead=()):
    rows = []
    for p in parts:
        r = p.reshape(lead + (-1, LANES)).astype(dtype)
        pad = (-r.shape[-2]) % PACK_ROWS
        rows.append(jnp.pad(r, [(0, 0)] * len(lead) + [(0, pad), (0, 0)]) if pad else r)
    return jnp.concatenate(rows, axis=len(lead))


def _unpack_rows(packed, shapes, lead=()):
    out, off = [], 0
    for shp in shapes:
        n = math.prod(shp) // LANES
        out.append(lax.slice_in_dim(packed, off, off + n, axis=len(lead)).reshape(lead + tuple(shp)))
        off += n + (-n) % PACK_ROWS
    return out


def _unshard(gathered, axis):
    g = jnp.moveaxis(gathered, 0, axis)
    shp = list(g.shape)
    shp[axis:axis + 2] = [shp[axis] * shp[axis + 1]]
    return g.reshape(shp)


def _reshard(full, axis):
    shp = list(full.shape)
    shp[axis:axis + 1] = [N_DEV, shp[axis] // N_DEV]
    return jnp.moveaxis(full.reshape(shp), axis, 0)


def _late_parts(L):
    return [(k, l) for k in BIG for l in range(L) if (k, l) != ("w_in", 0)]


def _late_groups(L, shard_shape):
    groups = {}
    for k, l in _late_parts(L):
        groups.setdefault(shard_shape[k][-1], []).append((k, l))
    return groups


def _local_step(x, target, w_in0, late_local, groups, shard_shape, sm):
    S, D = x.shape
    L = sm["mix_norm_g"].shape[0]
    wf = {k: [None] * L for k in BIG}
    wf["w_in"][0] = w_in0
    C = sm["conv_b"].shape[1]
    W = sm["sgu_ln_g"].shape[1]
    hd = sm["q_norm_g"].shape[1]
    CH = sm["sgu_w"].shape[2]
    A = w_in0.shape[1] - (3 * C + 2 * W + 3 * D)
    A = A // 3
    H = A // hd
    col_q = 3 * C + 2 * W
    qscale = 1.0 / math.sqrt(hd)
    tril = jnp.tril(jnp.ones((CH, CH), F32))
    (ftq, ftk), (btq, btk) = [(_tile(S, a, LANES), _tile(S, b, LANES)) for a, b in (ATTN_FWD_TILES, ATTN_BWD_TILES)]
    umat = lambda t: (lax.broadcasted_iota(jnp.int32, (t, t), 0) > lax.broadcasted_iota(jnp.int32, (t, t), 1)).astype(BF16)

    saved = []
    for l in range(L):
        n = f"l{l}_"
        g1 = sm["mix_norm_g"][l][None]
        h = _rmsnorm_fwd(x, g1, scale=1.0, out_dtype=BF16, name=n + "mixnorm")
        P = _matmul(h, wf["w_in"][l], name=n + "w_in", out_dtype=BF16)
        cw = jnp.pad(sm["conv_w"][l], ((0, 5), (0, 0)))
        cb = sm["conv_b"][l][None]
        ya = _conv_fwd(P, cw, cb, C=C, name=n + "conv")
        wm = (sm["sgu_w"][l] * tril).astype(BF16)
        bT = sm["sgu_b"][l].T
        lng, lnb = sm["sgu_ln_g"][l][None], sm["sgu_ln_b"][l][None]
        yb = _sgu_fwd(P, lng, lnb, wm, bT, W=W, cu=(3 * C) // W, cv=(3 * C) // W + 1, name=n + "sgu")
        gq, gk = sm["q_norm_g"][l][None], sm["k_norm_g"][l][None]
        qn, kn, vh = _qkv_heads_fwd(P, gq, gk, A=A, col0=col_q // A, hd=hd, qscale=qscale, name=n + "qkv")
        if l == 0:
            o, gathered = _attn_fwd(qn, kn, vh, umat(ftk), tq=ftq, tk=ftk, name=n + "attn",
                                    comm=(_gather_schedule, late_local, [(N_DEV,) + a.shape for a in late_local]))
            for (width, parts), g in zip(groups.items(), gathered):
                r = 0
                for k, ll in parts:
                    rows = math.prod(shard_shape[k]) // width
                    part = lax.slice_in_dim(g, r, r + rows, axis=1).reshape((N_DEV,) + tuple(shard_shape[k]))
                    wf[k][ll] = _unshard(part, SHARD_AXIS[k] - 1)
                    r += rows
        else:
            o = _attn_fwd(qn, kn, vh, umat(ftk), tq=ftq, tk=ftk, name=n + "attn")
        bg = sm["b_gate"][l][None]
        gate_col0 = (col_q + 3 * A) // D
        merged, yc = _merge_fwd(ya, yb, o, wf["w_branch_out"][l], P, bg, gate_col0=gate_col0, name=n + "merge")
        x1 = _matmul(merged, wf["w_o"][l], res=x, name=n + "w_o")
        g2 = sm["ffn_norm_g"][l][None]
        h2 = _rmsnorm_fwd(x1, g2, scale=1.0, out_dtype=BF16, name=n + "ffnnorm")
        gu = _matmul(h2, wf["w_gate_up"][l], name=n + "w_gate_up", out_dtype=BF16)
        act = _swiglu_fwd(gu, name=n + "swiglu")
        x2 = _matmul(act, wf["w_down"][l], res=x1, name=n + "w_down")
        saved.append(dict(x=x, h=h, P=P, cw=cw, cb=cb, ya=ya, wm=wm, bT=bT, lng=lng, lnb=lnb, yb=yb,
                          gq=gq, gk=gk, qn=qn, kn=kn, vh=vh, o=o, yc=yc, bg=bg,
                          gate_col0=gate_col0, merged=merged, x1=x1, g1=g1, g2=g2, h2=h2, gu=gu, act=act))
        x = x2

    dx, lpart, dxb = _loss_grad(x, target, name="loss")
    grads = {k: [None] * L for k in WEIGHTS}
    chunked = lambda k, g: g if k == "w_in" else _reshard(g, SHARD_AXIS[k] - 1)
    for l in reversed(range(L)):
        n = f"l{l}_b_"
        sv = saved[l]
        grads["w_down"][l] = _matmul_tn(sv["act"], dxb, name=n + "g_w_down", out_dtype=BF16)
        dact = _matmul(dxb, wf["w_down"][l].T, name=n + "d_act", out_dtype=BF16)
        dgu = _swiglu_bwd(sv["gu"], dact, name=n + "swiglu")
        grads["w_gate_up"][l] = _matmul_tn(sv["h2"], dgu, name=n + "g_w_gate_up", out_dtype=BF16)
        dh2 = _matmul(dgu, wf["w_gate_up"][l].T, name=n + "d_h2", out_dtype=BF16)
        dx1, dg2, dx1b = _rmsnorm_bwd(sv["x1"], sv["g2"], dh2, scale=1.0, dres=dx, name=n + "ffnnorm",
                                      bf16_copy=True)
        grads["ffn_norm_g"][l] = dg2[0]
        grads["w_o"][l] = _matmul_tn(sv["merged"], dx1b, name=n + "g_w_o", out_dtype=BF16)
        dmerged = _matmul(dx1b, wf["w_o"][l].T, name=n + "d_merged", out_dtype=BF16)
        ys = (sv["ya"], sv["yb"], sv["yc"])
        wb = wf["w_branch_out"][l]
        dgates, *dyd, dya, dyb, do, dbg = _merge_bwd(ys, wb, wb.transpose(0, 2, 1), sv["P"], sv["bg"], dmerged,
                                                     gate_col0=sv["gate_col0"], hd=hd, name=n + "merge")
        grads["b_gate"][l] = dbg[0]
        grads["w_branch_out"][l] = jnp.stack(
            [_matmul_tn(ys[i], dyd[i], name=n + f"g_w_branch{i}", out_dtype=BF16) for i in range(3)])
        dconv, dcw = _conv_bwd(sv["P"], dya, sv["cw"], sv["cb"], C=C, name=n + "conv")
        grads["conv_w"][l], grads["conv_b"][l] = dcw[0:3], dcw[3]
        wmT = sv["wm"].transpose(0, 2, 1)
        dsgu, dsw, dsb, dln = _sgu_bwd(sv["P"], dyb, sv["lng"], sv["lnb"], sv["wm"], wmT, sv["bT"], W=W,
                                       cu=(3 * C) // W, cv=(3 * C) // W + 1, name=n + "sgu")
        grads["sgu_w"][l], grads["sgu_b"][l] = dsw * tril, dsb[:, :, 0]
        grads["sgu_ln_g"][l], grads["sgu_ln_b"][l] = dln[0], dln[1]
        if l == 0:
            chunks = [jnp.concatenate([chunked(k, grads[k][ll]).reshape(N_DEV, -1, width) for k, ll in parts], axis=1)
                      for width, parts in groups.items()]
            dqn, dkn, dvh, late_recv = _attn_bwd(sv["qn"], sv["kn"], sv["vh"], sv["o"], do, umat(btk), tq=btq, tk=btk,
                                                 name=n + "attn",
                                                 comm=(_exchange_schedule, chunks, [c.shape for c in chunks]))
        else:
            dqn, dkn, dvh = _attn_bwd(sv["qn"], sv["kn"], sv["vh"], sv["o"], do, umat(btk), tq=btq, tk=btk,
                                      name=n + "attn")
        dqkv, dgq, dgk = _qkv_heads_bwd(sv["P"], sv["gq"], sv["gk"], dqn, dkn, dvh, A=A, col0=col_q // A, hd=hd,
                                        qscale=qscale, name=n + "qkv")
        grads["q_norm_g"][l], grads["k_norm_g"][l] = dgq[0], dgk[0]
        dP = jnp.concatenate([dconv, dsgu, dqkv, dgates], axis=1)
        grads["w_in"][l] = _matmul_tn(sv["h"], dP, name=n + "g_w_in", out_dtype=BF16, shard=dP.shape[1] // N_DEV)
        if l == 0:
            dh, early_recv = _matmul(dP, wf["w_in"][l].T, name=n + "d_h", out_dtype=BF16,
                                     comm=(_exchange_schedule, [grads["w_in"][0]], [grads["w_in"][0].shape]))
        else:
            dh = _matmul(dP, wf["w_in"][l].T, name=n + "d_h", out_dtype=BF16)
        dx, dg1, dxb = _rmsnorm_bwd(sv["x"], sv["g1"], dh, scale=1.0, dres=dx1, name=n + "mixnorm", bf16_copy=True)
        grads["mix_norm_g"][l] = dg1[0]
    return lpart[0, 0], dx, grads, early_recv, late_recv


def kernel(x, mix_norm_g, w_in, b_gate, conv_w, conv_b, sgu_ln_g, sgu_ln_b, sgu_w, sgu_b, q_norm_g, k_norm_g, w_branch_out, w_o, ffn_norm_g, w_gate_up, w_down, loss_target, m_mix_norm_g, m_w_in, m_b_gate, m_conv_w, m_conv_b, m_sgu_ln_g, m_sgu_ln_b, m_sgu_w, m_sgu_b, m_q_norm_g, m_k_norm_g, m_w_branch_out, m_w_o, m_ffn_norm_g, m_w_gate_up, m_w_down, v_mix_norm_g, v_w_in, v_b_gate, v_conv_w, v_conv_b, v_sgu_ln_g, v_sgu_ln_b, v_sgu_w, v_sgu_b, v_q_norm_g, v_k_norm_g, v_w_branch_out, v_w_o, v_ffn_norm_g, v_w_gate_up, v_w_down):
    w = dict(mix_norm_g=mix_norm_g, w_in=w_in, b_gate=b_gate, conv_w=conv_w, conv_b=conv_b, sgu_ln_g=sgu_ln_g,
             sgu_ln_b=sgu_ln_b, sgu_w=sgu_w, sgu_b=sgu_b, q_norm_g=q_norm_g, k_norm_g=k_norm_g,
             w_branch_out=w_branch_out, w_o=w_o, ffn_norm_g=ffn_norm_g, w_gate_up=w_gate_up, w_down=w_down)
    m = dict(mix_norm_g=m_mix_norm_g, w_in=m_w_in, b_gate=m_b_gate, conv_w=m_conv_w, conv_b=m_conv_b,
             sgu_ln_g=m_sgu_ln_g, sgu_ln_b=m_sgu_ln_b, sgu_w=m_sgu_w, sgu_b=m_sgu_b, q_norm_g=m_q_norm_g,
             k_norm_g=m_k_norm_g, w_branch_out=m_w_branch_out, w_o=m_w_o, ffn_norm_g=m_ffn_norm_g,
             w_gate_up=m_w_gate_up, w_down=m_w_down)
    v = dict(mix_norm_g=v_mix_norm_g, w_in=v_w_in, b_gate=v_b_gate, conv_w=v_conv_w, conv_b=v_conv_b,
             sgu_ln_g=v_sgu_ln_g, sgu_ln_b=v_sgu_ln_b, sgu_w=v_sgu_w, sgu_b=v_sgu_b, q_norm_g=v_q_norm_g,
             k_norm_g=v_k_norm_g, w_branch_out=v_w_branch_out, w_o=v_w_o, ffn_norm_g=v_ffn_norm_g,
             w_gate_up=v_w_gate_up, w_down=v_w_down)
    me = 4 * lax.axis_index("x") + 2 * lax.axis_index("y") + lax.axis_index("c")

    L = w_in.shape[0]
    two_d = lambda a: a.reshape(-1, a.shape[-1])
    shard_shape = {k: w[k].shape[1:] for k in BIG}
    groups = _late_groups(L, shard_shape)
    w_in0 = _unshard(_all_gather(w_in[0].astype(BF16), name="gather_w_in0"), SHARD_AXIS["w_in"] - 1)
    late_local = [jnp.concatenate([two_d(w[k][l]).astype(BF16) for k, l in parts], axis=0) for parts in groups.values()]
    conv_g = _all_gather(_pack_rows([conv_w], F32), name="gather_conv_w")
    sm = {k: w[k] for k in SMALL}
    sm["conv_w"] = _unshard(_unpack_rows(conv_g, [conv_w.shape], lead=(N_DEV,))[0], 2)

    lpart, dx, grads, early_recv, late_recv = _local_step(x[0], loss_target[0], w_in0, late_local, groups,
                                                          shard_shape, sm)
    loss = lax.psum(lpart, ("x", "y", "c"))

    where = {("w_in", 0): (early_recv[0], 0)}
    for (width, parts), recv in zip(groups.items(), late_recv):
        r = 0
        for k, l in parts:
            where[k, l] = (recv, r)
            r += math.prod(shard_shape[k]) // width
    out = {}
    for k in BIG:
        rows = math.prod(shard_shape[k]) // shard_shape[k][-1]
        gs_list = []
        for l in range(L):
            recv, r = where[k, l]
            if gs_list and gs_list[-1][0] is recv and gs_list[-1][2] == r:
                gs_list[-1][2] = r + rows
            else:
                gs_list.append([recv, r, r + rows])
        gs_list = [recv if (r0, r1) == (0, recv.shape[1]) else lax.slice_in_dim(recv, r0, r1, axis=1)
                   for recv, r0, r1 in gs_list]
        res = _adamw(two_d(w[k]), two_d(m[k]), two_d(v[k]), gs_list, name="adamw_" + k)
        for nm, r in zip(("grad", "delta", "new_m", "new_v"), res):
            out[nm, k] = r.reshape(w[k].shape)

    small_grads = [jnp.stack(grads[k]) for k in SMALL]
    small_shapes = [g.shape for g in small_grads]
    sg = _all_gather(_pack_rows(small_grads, F32), name="gather_small_grads")
    gsum = _slot_sum(sg, name="sum_small_grads")
    gsmall = dict(zip(SMALL, _unpack_rows(gsum, small_shapes)))
    cshard = conv_w.shape[2]
    gsmall["conv_w"] = lax.dynamic_slice_in_dim(gsmall["conv_w"], me * cshard, cshard, axis=2)
    own_shapes = [w[k].shape for k in SMALL]
    gs_, ds_, ms_, vs_ = _adamw(_pack_rows([w[k] for k in SMALL], F32), _pack_rows([m[k] for k in SMALL], F32),
                                _pack_rows([v[k] for k in SMALL], F32),
                                [_pack_rows([gsmall[k] for k in SMALL], F32)[None]], name="adamw_small")
    for nm, packed in (("grad", gs_), ("delta", ds_), ("new_m", ms_), ("new_v", vs_)):
        for k, a in zip(SMALL, _unpack_rows(packed, own_shapes)):
            out[nm, k] = a

    res = [loss, dx[None]]
    for nm in ("grad", "delta", "new_m", "new_v"):
        res += [out[nm, k] for k in WEIGHTS]
    return tuple(res)
```

```python
import functools
import math

import jax
import jax.numpy as jnp
from jax import lax
from jax.experimental import pallas as pl
from jax.experimental.pallas import tpu as pltpu

F32 = jnp.float32
BF16 = jnp.bfloat16
MESH = pl.DeviceIdType.MESH

N_DEV = 8
LANES = 128
VMEM_LIMIT_BYTES = 56 * 1024 * 1024
EPS = 1e-6
ADAM_LR, ADAM_B1, ADAM_B2, ADAM_EPS, ADAM_WD, ADAM_STEP = 0.001, 0.9, 0.999, 1e-08, 0.01, 10
ATTN_FWD_TILES = (512, 512)
ATTN_BWD_TILES = (1024, 256)
ATTN_FWD_UNROLL = 4
ATTN_BWD_UNROLL = 4
BIG = ("w_in", "w_branch_out", "w_o", "w_gate_up", "w_down")
SMALL = ("mix_norm_g", "b_gate", "conv_w", "conv_b", "sgu_ln_g", "sgu_ln_b", "sgu_w", "sgu_b",
         "q_norm_g", "k_norm_g", "ffn_norm_g")
WEIGHTS = ("mix_norm_g", "w_in", "b_gate", "conv_w", "conv_b", "sgu_ln_g", "sgu_ln_b", "sgu_w", "sgu_b",
           "q_norm_g", "k_norm_g", "w_branch_out", "w_o", "ffn_norm_g", "w_gate_up", "w_down")
SHARD_AXIS = {"w_in": 2, "w_branch_out": 3, "w_o": 1, "w_gate_up": 2, "w_down": 1}


def _tile(n, cap, mult):
    best = None
    for t in range(mult, min(n, cap) + 1, mult):
        if n % t == 0:
            best = t
    return best if best is not None else n


def _params(*sem):
    return pltpu.CompilerParams(dimension_semantics=sem if sem else None, vmem_limit_bytes=VMEM_LIMIT_BYTES)


def _erf(x):
    return lax.erf(x)


def _gelu(x):
    return 0.5 * x * (1.0 + _erf(x * (1.0 / math.sqrt(2.0))))


def _gelu_grad(x):
    return 0.5 * (1.0 + _erf(x * (1.0 / math.sqrt(2.0)))) + x * jnp.exp(-0.5 * x * x) * (1.0 / math.sqrt(2.0 * math.pi))


def _sigmoid(x):
    return 1.0 / (1.0 + jnp.exp(-x))


def _matmul(a, b, *, name, res=None, out_dtype=F32, comm=None):
    M, K = a.shape
    _, N = b.shape
    tm, tn, tk = _tile(M, 1024, 8), _tile(N, 1536, LANES), _tile(K, 1536, LANES)
    nk = K // tk
    grid = (M // tm, N // tn, nk)
    has_res = res is not None

    def body(*refs):
        refs = list(refs)
        a_ref, b_ref = refs[:2]
        r_ref = refs[2] if has_res else None
        pos = 2 + has_res
        nc = len(comm[1]) if comm is not None else 0
        o_ref = refs[pos + nc]
        acc = refs[pos + 1 + 2 * nc] if nk > 1 else None
        if comm is not None:
            comm_end = _comm_begin(comm[0], refs[pos:pos + nc], refs[pos + nc + 1:pos + 2 * nc + 1],
                                   refs[pos + 2 * nc + 1 + (nk > 1):], grid)
        k = pl.program_id(2)
        part = jnp.dot(a_ref[...], b_ref[...], preferred_element_type=F32)

        def finish(v):
            if has_res:
                v = v + r_ref[...]
            o_ref[...] = v.astype(out_dtype)

        if nk == 1:
            finish(part)
        else:
            @pl.when(k == 0)
            def _():
                acc[...] = part

            @pl.when(jnp.logical_and(k > 0, k < nk - 1))
            def _():
                acc[...] += part

            @pl.when(k == nk - 1)
            def _():
                finish(acc[...] + part)

        if comm is not None:
            comm_end()

    in_specs = [pl.BlockSpec((tm, tk), lambda i, j, k: (i, k)), pl.BlockSpec((tk, tn), lambda i, j, k: (k, j))]
    args = [a, b]
    if has_res:
        in_specs.append(pl.BlockSpec((tm, tn), lambda i, j, k: (i, j)))
        args.append(res)
    out_specs = [pl.BlockSpec((tm, tn), lambda i, j, k: (i, j))]
    out_shape = [jax.ShapeDtypeStruct((M, N), out_dtype)]
    scratch = [pltpu.VMEM((tm, tn), F32)] if nk > 1 else []
    sem = ("parallel", "parallel", "arbitrary")
    if comm is not None:
        anyspec = pl.BlockSpec(memory_space=pl.ANY)
        in_specs += [anyspec] * len(comm[1])
        args += list(comm[1])
        out_specs += [anyspec] * len(comm[1])
        out_shape += [jax.ShapeDtypeStruct(c, x.dtype) for c, x in zip(comm[2], comm[1])]
        scratch += _comm_scratch(len(comm[1]))
        sem = ("arbitrary",) * 3
    out = pl.pallas_call(body, name=name, grid=grid, in_specs=in_specs, out_specs=out_specs, out_shape=out_shape,
                         scratch_shapes=scratch, compiler_params=_params(*sem))(*args)
    return out[0] if comm is None else (out[0], out[1:])


def _matmul_tn(x, y, *, name, out_dtype=F32, shard=None):
    S, A = x.shape
    _, B = y.shape
    ta, ts = _tile(A, 1536, LANES), _tile(S, 1024, 8)
    tb = shard if shard else _tile(B, 1536, LANES)
    ns = S // ts
    direct = out_dtype == F32
    view = (lambda r: r.at[0]) if shard else (lambda r: r)

    def body(x_ref, y_ref, o_ref, *scratch):
        s = pl.program_id(2)
        out = view(o_ref)
        acc = out if direct else scratch[0]
        part = lax.dot_general(x_ref[...], y_ref[...], (((0,), (0,)), ((), ())), preferred_element_type=F32)

        @pl.when(s == 0)
        def _():
            acc[...] = part

        @pl.when(s > 0)
        def _():
            acc[...] += part

        if not direct:
            @pl.when(s == ns - 1)
            def _():
                out[...] = acc[...].astype(out_dtype)

    if shard:
        out_spec, shape = pl.BlockSpec((1, ta, tb), lambda i, j, s: (j, i, 0)), (B // tb, A, tb)
    else:
        out_spec, shape = pl.BlockSpec((ta, tb), lambda i, j, s: (i, j)), (A, B)
    return pl.pallas_call(
        body, name=name, grid=(A // ta, B // tb, ns),
        in_specs=[pl.BlockSpec((ts, ta), lambda i, j, s: (s, i)), pl.BlockSpec((ts, tb), lambda i, j, s: (s, j))],
        out_specs=out_spec, out_shape=jax.ShapeDtypeStruct(shape, out_dtype),
        scratch_shapes=[] if direct else [pltpu.VMEM((ta, tb), F32)],
        compiler_params=_params("parallel", "parallel", "arbitrary"),
    )(x, y)


def _rmsnorm_fwd(x, g, *, scale, out_dtype, name):
    R, W = x.shape
    tr = _tile(R, 512 if W >= 512 else 4096, 8)

    def body(x_ref, g_ref, o_ref):
        xv = x_ref[...]
        r = lax.rsqrt(jnp.mean(xv * xv, axis=1, keepdims=True) + EPS)
        o_ref[...] = (xv * r * (g_ref[...] * scale)).astype(out_dtype)

    return pl.pallas_call(
        body, name=name, grid=(R // tr,),
        in_specs=[pl.BlockSpec((tr, W), lambda i: (i, 0)), pl.BlockSpec((1, W), lambda i: (0, 0))],
        out_specs=pl.BlockSpec((tr, W), lambda i: (i, 0)),
        out_shape=jax.ShapeDtypeStruct((R, W), out_dtype),
        compiler_params=_params("parallel"),
    )(x, g)


def _rmsnorm_bwd(x, g, dy, *, scale, name, dres=None, out_dtype=F32, bf16_copy=False):
    R, W = x.shape
    tr = _tile(R, 512 if W >= 512 else 4096, 8)
    has_res = dres is not None

    def body(*refs):
        refs = list(refs)
        dxb_ref = refs.pop() if bf16_copy else None
        if has_res:
            x_ref, g_ref, dy_ref, dres_ref, dx_ref, dg_ref = refs
        else:
            x_ref, g_ref, dy_ref, dx_ref, dg_ref = refs
        i = pl.program_id(0)
        xv = x_ref[...]
        dyv = dy_ref[...].astype(F32) * scale
        r = lax.rsqrt(jnp.mean(xv * xv, axis=1, keepdims=True) + EPS)
        u = dyv * g_ref[...]
        dx = r * u - xv * (r * r * r * jnp.mean(u * xv, axis=1, keepdims=True))
        if has_res:
            dx = dx + dres_ref[...]
        dx_ref[...] = dx.astype(out_dtype)
        if bf16_copy:
            dxb_ref[...] = dx.astype(BF16)
        part = jnp.sum(dyv * xv * r, axis=0, keepdims=True)

        @pl.when(i == 0)
        def _():
            dg_ref[...] = part

        @pl.when(i > 0)
        def _():
            dg_ref[...] += part

    row = pl.BlockSpec((tr, W), lambda i: (i, 0))
    one = pl.BlockSpec((1, W), lambda i: (0, 0))
    in_specs = [row, one, row] + ([row] if has_res else [])
    args = [x, g, dy] + ([dres] if has_res else [])
    extra = bool(bf16_copy)
    return pl.pallas_call(
        body, name=name, grid=(R // tr,), in_specs=in_specs, out_specs=[row, one] + [row] * extra,
        out_shape=[jax.ShapeDtypeStruct((R, W), out_dtype), jax.ShapeDtypeStruct((1, W), F32)]
        + [jax.ShapeDtypeStruct((R, W), BF16)] * extra,
        compiler_params=_params("arbitrary"),
    )(*args)


HALO = 16


def _shift_down(u, prev, n):
    out = pltpu.roll(u, n, 0)
    row = lax.broadcasted_iota(jnp.int32, u.shape, 0)
    for r in range(n):
        out = jnp.where(row == r, prev[HALO - n + r:HALO - n + r + 1, :], out)
    return out


def _shift_up(u, nxt, n):
    ts = u.shape[0]
    out = pltpu.roll(u, ts - n, 0)
    row = lax.broadcasted_iota(jnp.int32, u.shape, 0)
    for r in range(n):
        out = jnp.where(row == ts - n + r, nxt[r:r + 1, :], out)
    return out


def _conv_fwd(P, conv_w, conv_b, *, C, name):
    S = P.shape[0]
    ts = _tile(S, 512, HALO)
    hb = ts // HALO

    def body(ab_ref, ac_ref, ax_ref, pc_ref, px_ref, w_ref, b_ref, o_ref):
        i = pl.program_id(0)
        u = ac_ref[...].astype(F32) * ax_ref[...].astype(F32)
        prev = pc_ref[...].astype(F32) * px_ref[...].astype(F32) * (i > 0).astype(F32)
        w = w_ref[...]
        y = b_ref[...] + w[0:1, :] * _shift_down(u, prev, 2) + w[1:2, :] * _shift_down(u, prev, 1) + w[2:3, :] * u
        o_ref[...] = (ab_ref[...].astype(F32) * y).astype(BF16)

    cur = lambda c: pl.BlockSpec((ts, C), lambda i: (i, c))
    prv = lambda c: pl.BlockSpec((HALO, C), lambda i: (jnp.maximum(i * hb - 1, 0), c))
    return pl.pallas_call(
        body, name=name, grid=(S // ts,),
        in_specs=[cur(0), cur(1), cur(2), prv(1), prv(2),
                  pl.BlockSpec((8, C), lambda i: (0, 0)), pl.BlockSpec((1, C), lambda i: (0, 0))],
        out_specs=pl.BlockSpec((ts, C), lambda i: (i, 0)),
        out_shape=jax.ShapeDtypeStruct((S, C), BF16),
        compiler_params=_params("parallel"),
    )(P, P, P, P, P, conv_w, conv_b)


def _conv_bwd(P, dya, conv_w, conv_b, *, C, name):
    S = P.shape[0]
    ts = _tile(S, 512, HALO)
    hb = ts // HALO
    last = S // HALO - 1
    n = S // ts

    def body(ab_ref, ac_ref, ax_ref, pc_ref, px_ref, dy_ref, nab_ref, ndy_ref, w_ref, b_ref, o_ref, dw_ref):
        i = pl.program_id(0)
        ab, ac, ax = ab_ref[...].astype(F32), ac_ref[...].astype(F32), ax_ref[...].astype(F32)
        u = ac * ax
        prev = pc_ref[...].astype(F32) * px_ref[...].astype(F32) * (i > 0).astype(F32)
        w = w_ref[...]
        u1, u2 = _shift_down(u, prev, 1), _shift_down(u, prev, 2)
        y = b_ref[...] + w[0:1, :] * u2 + w[1:2, :] * u1 + w[2:3, :] * u
        dya_v = dy_ref[...]
        dyp = dya_v * ab
        nxt = ndy_ref[...] * nab_ref[...].astype(F32) * (i < n - 1).astype(F32)
        du = w[2:3, :] * dyp + w[1:2, :] * _shift_up(dyp, nxt, 1) + w[0:1, :] * _shift_up(dyp, nxt, 2)
        o_ref[:, 0:C] = (dya_v * y).astype(BF16)
        o_ref[:, C:2 * C] = (du * ax).astype(BF16)
        o_ref[:, 2 * C:3 * C] = (du * ac).astype(BF16)
        part = jnp.concatenate([
            jnp.sum(dyp * u2, axis=0, keepdims=True), jnp.sum(dyp * u1, axis=0, keepdims=True),
            jnp.sum(dyp * u, axis=0, keepdims=True), jnp.sum(dyp, axis=0, keepdims=True),
            jnp.zeros((4, C), F32)], axis=0)

        @pl.when(i == 0)
        def _():
            dw_ref[...] = part

        @pl.when(i > 0)
        def _():
            dw_ref[...] += part

    cur = lambda c: pl.BlockSpec((ts, C), lambda i: (i, c))
    prv = lambda c: pl.BlockSpec((HALO, C), lambda i: (jnp.maximum(i * hb - 1, 0), c))
    nxt = lambda c: pl.BlockSpec((HALO, C), lambda i: (jnp.minimum((i + 1) * hb, last), c))
    return pl.pallas_call(
        body, name=name, grid=(n,),
        in_specs=[cur(0), cur(1), cur(2), prv(1), prv(2), cur(0), nxt(0), nxt(0),
                  pl.BlockSpec((8, C), lambda i: (0, 0)), pl.BlockSpec((1, C), lambda i: (0, 0))],
        out_specs=[pl.BlockSpec((ts, 3 * C), lambda i: (i, 0)), pl.BlockSpec((8, C), lambda i: (0, 0))],
        out_shape=[jax.ShapeDtypeStruct((S, 3 * C), BF16), jax.ShapeDtypeStruct((8, C), F32)],
        compiler_params=_params("arbitrary"),
    )(P, P, P, P, P, dya, P, dya, conv_w, conv_b)


def _sgu_fwd(P, ln_g, ln_b, wm, bT, *, W, cu, cv, name):
    S = P.shape[0]
    G, CH, _ = wm.shape
    gw = W // G
    ts = _tile(S, 512, CH)

    def body(u_ref, v_ref, g_ref, b_ref, wm_ref, bT_ref, o_ref):
        gv = _gelu(v_ref[...].astype(F32))
        mu = jnp.mean(gv, axis=1, keepdims=True)
        xc = gv - mu
        vn = (xc * lax.rsqrt(jnp.mean(xc * xc, axis=1, keepdims=True) + EPS) * g_ref[...] + b_ref[...]).astype(BF16)
        bT_v = bT_ref[...]
        for c in range(ts // CH):
            rows = slice(c * CH, (c + 1) * CH)
            for g in range(G):
                cols = slice(g * gw, (g + 1) * gw)
                mixed = jnp.dot(wm_ref[g], vn[rows, cols], preferred_element_type=F32) + bT_v[:, g:g + 1]
                o_ref[rows, cols] = (_gelu(u_ref[rows, cols].astype(F32)) * mixed).astype(BF16)

    full = lambda shp: pl.BlockSpec(shp, lambda i: (0,) * len(shp))
    return pl.pallas_call(
        body, name=name, grid=(S // ts,),
        in_specs=[pl.BlockSpec((ts, W), lambda i: (i, cu)), pl.BlockSpec((ts, W), lambda i: (i, cv)),
                  full((1, W)), full((1, W)), full((G, CH, CH)), full((CH, G))],
        out_specs=pl.BlockSpec((ts, W), lambda i: (i, 0)),
        out_shape=jax.ShapeDtypeStruct((S, W), BF16),
        compiler_params=_params("parallel"),
    )(P, P, ln_g, ln_b, wm, bT)


def _sgu_bwd(P, dyb, ln_g, ln_b, wm, wmT, bT, *, W, cu, cv, name):
    S = P.shape[0]
    G, CH, _ = wm.shape
    gw = W // G
    ts = _tile(S, 512, CH)

    def body(u_ref, v_ref, dy_ref, g_ref, b_ref, wm_ref, wmT_ref, bT_ref, o_ref, dw_ref, db_ref, dln_ref, dvn_ref):
        i = pl.program_id(0)

        @pl.when(i == 0)
        def _():
            dw_ref[...] = jnp.zeros_like(dw_ref)
            db_ref[...] = jnp.zeros_like(db_ref)
            dln_ref[...] = jnp.zeros_like(dln_ref)

        sv = v_ref[...].astype(F32)
        gv = _gelu(sv)
        mu = jnp.mean(gv, axis=1, keepdims=True)
        xc = gv - mu
        rstd = lax.rsqrt(jnp.mean(xc * xc, axis=1, keepdims=True) + EPS)
        xhat = xc * rstd
        lg = g_ref[...]
        vn = (xhat * lg + b_ref[...]).astype(BF16)
        bT_v = bT_ref[...]
        for c in range(ts // CH):
            rows = slice(c * CH, (c + 1) * CH)
            for g in range(G):
                cols = slice(g * gw, (g + 1) * gw)
                vn_cg = vn[rows, cols]
                mixed = jnp.dot(wm_ref[g], vn_cg, preferred_element_type=F32) + bT_v[:, g:g + 1]
                su = u_ref[rows, cols].astype(F32)
                dyv = dy_ref[rows, cols]
                dmix = dyv * _gelu(su)
                o_ref[rows, cols] = (dyv * mixed * _gelu_grad(su)).astype(BF16)
                dmix_b = dmix.astype(BF16)
                dw_ref[g] += lax.dot_general(dmix_b, vn_cg, (((1,), (1,)), ((), ())), preferred_element_type=F32)
                db_ref[g] += jnp.broadcast_to(jnp.sum(dmix, axis=1, keepdims=True), (CH, CH))
                dvn_ref[rows, cols] = jnp.dot(wmT_ref[g], dmix_b, preferred_element_type=F32)
        dvn = dvn_ref[...]
        dxh = dvn * lg
        dgv = rstd * (dxh - jnp.mean(dxh, axis=1, keepdims=True) - xhat * jnp.mean(dxh * xhat, axis=1, keepdims=True))
        o_ref[:, W:2 * W] = (dgv * _gelu_grad(sv)).astype(BF16)
        dln_ref[0:1, :] += jnp.sum(dvn * xhat, axis=0, keepdims=True)
        dln_ref[1:2, :] += jnp.sum(dvn, axis=0, keepdims=True)

    full = lambda shp: pl.BlockSpec(shp, lambda i: (0,) * len(shp))
    return pl.pallas_call(
        body, name=name, grid=(S // ts,),
        in_specs=[pl.BlockSpec((ts, W), lambda i: (i, cu)), pl.BlockSpec((ts, W), lambda i: (i, cv)),
                  pl.BlockSpec((ts, W), lambda i: (i, 0)),
                  full((1, W)), full((1, W)), full((G, CH, CH)), full((G, CH, CH)), full((CH, G))],
        out_specs=[pl.BlockSpec((ts, 2 * W), lambda i: (i, 0)), full((G, CH, CH)), full((G, CH, CH)), full((8, W))],
        out_shape=[jax.ShapeDtypeStruct((S, 2 * W), BF16), jax.ShapeDtypeStruct((G, CH, CH), F32),
                   jax.ShapeDtypeStruct((G, CH, CH), F32), jax.ShapeDtypeStruct((8, W), F32)],
        scratch_shapes=[pltpu.VMEM((ts, W), F32)],
        compiler_params=_params("arbitrary"),
    )(P, P, dyb, ln_g, ln_b, wm, wmT, bT)


def _block_sums(x, u, parts=1):
    hi = x.astype(BF16)
    out = jnp.dot(hi, u, preferred_element_type=F32)
    if parts == 2:
        lo = (x - hi.astype(F32)).astype(BF16)
        out = out + jnp.dot(lo, u, preferred_element_type=F32)
    return out


_NT = (((1,), (1,)), ((), ()))
_TN = (((0,), (0,)), ((), ()))


def _left_blocks(step, jd, carry, unroll, jd_multiple):
    rem = 0
    if jd_multiple % unroll:
        rem = jd % unroll
        carry = lax.fori_loop(0, rem, lambda t, c: step(jd - 1 - t, c, False), carry)

    def trip(t, c):
        for s in range(unroll):
            c = step(jd - rem - 1 - s - unroll * t, c, False)
        return c

    return lax.fori_loop(0, jd // unroll, trip, carry)


def _diag_step(step, j, carry, row0):
    if row0 == 0:
        return step(j, carry, True)
    tail = step(j, tuple(c[row0:] for c in carry), True, row0)
    return tuple(jnp.concatenate([c[:row0], t], axis=0) for c, t in zip(carry, tail))


def _qkv_heads_fwd(P, gq, gk, *, A, col0, hd, qscale, name):
    S = P.shape[0]
    H = A // hd
    ts = _tile(S, 512, 8)

    def body(q_ref, k_ref, v_ref, gq_ref, gk_ref, qn_ref, kn_ref, vh_ref):
        for h in range(H):
            cols = slice(h * hd, (h + 1) * hd)
            for x_ref, g_ref, sc, o_ref in ((q_ref, gq_ref, qscale, qn_ref), (k_ref, gk_ref, 1.0, kn_ref)):
                xh = x_ref[:, cols].astype(F32)
                r = lax.rsqrt(jnp.mean(xh * xh, axis=1, keepdims=True) + EPS)
                o_ref[h] = (xh * r * (g_ref[...] * sc)).astype(BF16)
            vh_ref[h] = v_ref[:, cols].astype(BF16)

    col = lambda n: pl.BlockSpec((ts, A), lambda i: (i, col0 + n))
    gsp = pl.BlockSpec((1, hd), lambda i: (0, 0))
    hsp = pl.BlockSpec((H, ts, hd), lambda i: (0, i, 0))
    shp = jax.ShapeDtypeStruct((H, S, hd), BF16)
    return pl.pallas_call(
        body, name=name, grid=(S // ts,), in_specs=[col(0), col(1), col(2), gsp, gsp],
        out_specs=[hsp, hsp, hsp], out_shape=[shp, shp, shp], compiler_params=_params("parallel"),
    )(P, P, P, gq, gk)


def _qkv_heads_bwd(P, gq, gk, dqn, dkn, dvh, *, A, col0, hd, qscale, name):
    S = P.shape[0]
    H = A // hd
    ts = _tile(S, 512, 8)

    def body(q_ref, k_ref, gq_ref, gk_ref, dq_ref, dk_ref, dv_ref, o_ref, dgq_ref, dgk_ref):
        i = pl.program_id(0)
        parts = [jnp.zeros((1, hd), F32), jnp.zeros((1, hd), F32)]
        for h in range(H):
            for n, (x_ref, g_ref, sc, d_ref) in enumerate(((q_ref, gq_ref, qscale, dq_ref), (k_ref, gk_ref, 1.0, dk_ref))):
                xh = x_ref[:, h * hd:(h + 1) * hd].astype(F32)
                dyv = d_ref[h] * sc
                r = lax.rsqrt(jnp.mean(xh * xh, axis=1, keepdims=True) + EPS)
                u = dyv * g_ref[...]
                dx = r * u - xh * (r * r * r * jnp.mean(u * xh, axis=1, keepdims=True))
                o_ref[:, n * A + h * hd:n * A + (h + 1) * hd] = dx.astype(BF16)
                parts[n] = parts[n] + jnp.sum(dyv * xh * r, axis=0, keepdims=True)
            o_ref[:, 2 * A + h * hd:2 * A + (h + 1) * hd] = dv_ref[h].astype(BF16)

        @pl.when(i == 0)
        def _():
            dgq_ref[...] = parts[0]
            dgk_ref[...] = parts[1]

        @pl.when(i > 0)
        def _():
            dgq_ref[...] += parts[0]
            dgk_ref[...] += parts[1]

    col = lambda n: pl.BlockSpec((ts, A), lambda i: (i, col0 + n))
    gsp = pl.BlockSpec((1, hd), lambda i: (0, 0))
    hsp = pl.BlockSpec((H, ts, hd), lambda i: (0, i, 0))
    return pl.pallas_call(
        body, name=name, grid=(S // ts,), in_specs=[col(0), col(1), gsp, gsp, hsp, hsp, hsp],
        out_specs=[pl.BlockSpec((ts, 3 * A), lambda i: (i, 0)), gsp, gsp],
        out_shape=[jax.ShapeDtypeStruct((S, 3 * A), BF16), jax.ShapeDtypeStruct((1, hd), F32),
                   jax.ShapeDtypeStruct((1, hd), F32)],
        compiler_params=_params("arbitrary"),
    )(P, P, gq, gk, dqn, dkn, dvh)


def _comm_begin(schedule, cin_refs, cout_refs, sems, grid):
    begin, forward, finish = _comm_phases(schedule, cin_refs, cout_refs, *sems)
    ids = [pl.program_id(d) for d in range(len(grid))]
    at = lambda where: functools.reduce(jnp.logical_and, [p == w for p, w in zip(ids, where)])
    pl.when(at([0] * len(grid)))(begin)

    def comm_end():
        pl.when(at([grid[0] // 2] + [0] * (len(grid) - 1)))(forward)
        pl.when(at([g - 1 for g in grid]))(finish)

    return comm_end


def _call_with_comm(body, name, grid, in_specs, out_specs, out_shape, args, comm, scratch=()):
    if comm is None:
        return pl.pallas_call(body, name=name, grid=grid, in_specs=in_specs, out_specs=out_specs, out_shape=out_shape,
                              scratch_shapes=list(scratch), compiler_params=_params("parallel", "arbitrary"))(*args)
    _, xs, cshapes = comm
    anyspec = pl.BlockSpec(memory_space=pl.ANY)
    return pl.pallas_call(
        body, name=name, grid=grid, in_specs=in_specs + [anyspec] * len(xs), out_specs=out_specs + [anyspec] * len(xs),
        out_shape=out_shape + [jax.ShapeDtypeStruct(c, x.dtype) for c, x in zip(cshapes, xs)],
        scratch_shapes=list(scratch) + _comm_scratch(len(xs)), compiler_params=_params("arbitrary", "arbitrary"),
    )(*args, *xs)


def _attn_fwd(q, k, v, umat, *, tq, tk, name, comm=None):
    H, S, hd = q.shape

    def body(*refs):
        if comm is None:
            q_ref, k_ref, v_ref, u_ref, o_ref = refs
        else:
            nc = len(comm[1])
            q_ref, k_ref, v_ref, u_ref = refs[:4]
            o_ref = refs[4 + nc]
            comm_end = _comm_begin(comm[0], refs[4:4 + nc], refs[5 + nc:5 + 2 * nc], refs[5 + 2 * nc:], (H, S // tq))
        i = pl.program_id(1)
        qpos = lax.broadcasted_iota(jnp.int32, (tq, tk), 0) + i * tq
        kloc = lax.broadcasted_iota(jnp.int32, (tq, tk), 1)

        def step(j, carry, masked, row0=0):
            r, acc = carry
            ks = pl.multiple_of(j * tk, tk)
            kb = k_ref[0, pl.ds(ks, tk), :]
            vb = v_ref[0, pl.ds(ks, tk), :]
            z = lax.dot_general(q_ref[0, row0:, :], kb, _NT, preferred_element_type=F32)
            lb = jnp.minimum(z, 0.0) - jnp.log(1.0 + jnp.exp(-jnp.abs(z)))
            lm = lb - z
            if masked:
                m = (kloc[row0:] + j * tk) < qpos[row0:]
                lm = jnp.where(m, lm, 0.0)
            a = jnp.exp(lb + _block_sums(lm, u_ref[...]) + r)
            if masked:
                a = jnp.where(m, a, 0.0)
            acc = acc + jnp.dot(a.astype(BF16), vb, preferred_element_type=F32)
            return r + jnp.sum(lm, axis=1, keepdims=True), acc

        jd = (i * tq) // tk
        carry = (jnp.zeros((tq, 1), F32), jnp.zeros((tq, hd), F32))
        for dd in reversed(range(max(1, tq // tk))):
            carry = _diag_step(step, jd + dd, carry, dd * tk if tq > tk else 0)
        carry = _left_blocks(step, jd, carry, ATTN_FWD_UNROLL, tq // tk)
        o_ref[0] = carry[1]
        if comm is not None:
            comm_end()

    blk = pl.BlockSpec((1, tq, hd), lambda h, i: (h, i, 0))
    whole = pl.BlockSpec((1, S, hd), lambda h, i: (h, 0, 0))
    in_specs = [blk, whole, whole, pl.BlockSpec((tk, tk), lambda h, i: (0, 0))]
    out_specs, out_shape = [blk], [jax.ShapeDtypeStruct((H, S, hd), F32)]
    res = _call_with_comm(body, name, (H, S // tq), in_specs, out_specs, out_shape, (q, k, v, umat), comm)
    return res[0] if comm is None else (res[0], res[1:])


def _attn_bwd(q, k, v, o, do, umat, *, tq, tk, name, comm=None):
    H, S, hd = q.shape

    def body(*refs):
        if comm is None:
            q_ref, k_ref, v_ref, o_ref, do_ref, u_ref, dq_ref, dk_ref, dv_ref, tot_ref, dob_ref = refs
        else:
            nc = len(comm[1])
            q_ref, k_ref, v_ref, o_ref, do_ref, u_ref = refs[:6]
            dq_ref, dk_ref, dv_ref = refs[6 + nc:9 + nc]
            tot_ref, dob_ref = refs[9 + 2 * nc:11 + 2 * nc]
            comm_end = _comm_begin(comm[0], refs[6:6 + nc], refs[9 + nc:9 + 2 * nc], refs[11 + 2 * nc:], (H, S // tq))
        i = pl.program_id(1)

        @pl.when(i == 0)
        def _():
            dk_ref[...] = jnp.zeros_like(dk_ref)
            dv_ref[...] = jnp.zeros_like(dv_ref)

        dob_ref[...] = do_ref[0].astype(BF16)
        tot_ref[...] = jnp.sum(dob_ref[...].astype(F32) * o_ref[0], axis=1, keepdims=True)
        qpos = lax.broadcasted_iota(jnp.int32, (tq, tk), 0) + i * tq
        kloc = lax.broadcasted_iota(jnp.int32, (tq, tk), 1)

        def step(j, carry, masked, row0=0):
            r, gs, dq = carry
            ks = pl.multiple_of(j * tk, tk)
            kb = k_ref[0, pl.ds(ks, tk), :]
            vb = v_ref[0, pl.ds(ks, tk), :]
            qs, dos, um = q_ref[0, row0:, :], dob_ref[row0:, :], u_ref[...]
            z = lax.dot_general(qs, kb, _NT, preferred_element_type=F32)
            lb = jnp.minimum(z, 0.0) - jnp.log(1.0 + jnp.exp(-jnp.abs(z)))
            sig = jnp.exp(lb)
            lm = lb - z
            if masked:
                m = (kloc[row0:] + j * tk) < qpos[row0:]
                lm = jnp.where(m, lm, 0.0)
            a = jnp.exp(lb + _block_sums(lm, um) + r)
            if masked:
                a = jnp.where(m, a, 0.0)
            ab = a.astype(BF16)
            g = lax.dot_general(dos, vb, _NT, preferred_element_type=F32) * ab.astype(F32)
            dz = g - sig * ((tot_ref[row0:, :] - gs) - _block_sums(g, um, parts=2))
            if masked:
                dz = jnp.where(m, dz, 0.0)
            dzb = dz.astype(BF16)
            dq = dq + jnp.dot(dzb, kb, preferred_element_type=F32)
            dk_ref[0, pl.ds(ks, tk), :] += lax.dot_general(dzb, qs, _TN, preferred_element_type=F32)
            dv_ref[0, pl.ds(ks, tk), :] += lax.dot_general(ab, dos, _TN, preferred_element_type=F32)
            return r + jnp.sum(lm, axis=1, keepdims=True), gs + jnp.sum(g, axis=1, keepdims=True), dq

        jd = (i * tq) // tk
        zero = jnp.zeros((tq, 1), F32)
        carry = (zero, zero, jnp.zeros((tq, hd), F32))
        for dd in reversed(range(max(1, tq // tk))):
            carry = _diag_step(step, jd + dd, carry, dd * tk if tq > tk else 0)
        carry = _left_blocks(step, jd, carry, ATTN_BWD_UNROLL, tq // tk)
        dq_ref[0] = carry[2]
        if comm is not None:
            comm_end()

    blk = pl.BlockSpec((1, tq, hd), lambda h, i: (h, i, 0))
    whole = pl.BlockSpec((1, S, hd), lambda h, i: (h, 0, 0))
    shp = jax.ShapeDtypeStruct((H, S, hd), F32)
    in_specs = [blk, whole, whole, blk, blk, pl.BlockSpec((tk, tk), lambda h, i: (0, 0))]
    res = _call_with_comm(body, name, (H, S // tq), in_specs, [blk, whole, whole], [shp, shp, shp],
                          (q, k, v, o, do, umat), comm, scratch=[pltpu.VMEM((tq, 1), F32), pltpu.VMEM((tq, hd), BF16)])
    return res if comm is None else (*res[:3], res[3:])


def _merge_fwd(ya, yb, o, wb, P, b_gate, *, gate_col0, name):
    S, C = ya.shape
    H, _, hd = o.shape
    D = wb.shape[2]
    ts = _tile(S, 512, 8)

    def body(y0, y1, oh_ref, wb_ref, g0, g1, g2, bg_ref, m_ref, yc_ref):
        for h in range(H):
            yc_ref[:, h * hd:(h + 1) * hd] = oh_ref[h].astype(BF16)
        acc = jnp.zeros((ts, D), F32)
        for n, (y_ref, g_ref) in enumerate(((y0, g0), (y1, g1), (yc_ref, g2))):
            yd = jnp.dot(y_ref[...], wb_ref[n], preferred_element_type=F32)
            acc = acc + _sigmoid(g_ref[...].astype(F32) + bg_ref[:, n * D:(n + 1) * D]) * yd
        m_ref[...] = acc.astype(BF16)

    ysp = pl.BlockSpec((ts, C), lambda i: (i, 0))
    gsp = lambda n: pl.BlockSpec((ts, D), lambda i: (i, gate_col0 + n))
    return pl.pallas_call(
        body, name=name, grid=(S // ts,),
        in_specs=[ysp, ysp, pl.BlockSpec((H, ts, hd), lambda i: (0, i, 0)), pl.BlockSpec((3, C, D), lambda i: (0, 0, 0)),
                  gsp(0), gsp(1), gsp(2), pl.BlockSpec((1, 3 * D), lambda i: (0, 0))],
        out_specs=[pl.BlockSpec((ts, D), lambda i: (i, 0)), ysp],
        out_shape=[jax.ShapeDtypeStruct((S, D), BF16), jax.ShapeDtypeStruct((S, C), BF16)],
        compiler_params=_params("parallel"),
    )(ya, yb, o, wb, P, P, P, b_gate)


def _merge_bwd(ys, wb, wbT, P, b_gate, dmerged, *, gate_col0, hd, name):
    S, C = ys[0].shape
    D = wb.shape[2]
    H = C // hd
    ts = _tile(S, 512, 8)

    def body(y0, y1, y2, wb_ref, wbT_ref, g0, g1, g2, bg_ref, dm_ref,
             dg_ref, dyd0, dyd1, dyd2, dya_ref, dyb_ref, do_ref, dbg_ref):
        i = pl.program_id(0)
        dm = dm_ref[...].astype(F32)
        parts = []
        for n, (y_ref, g_ref, dyd_ref) in enumerate(((y0, g0, dyd0), (y1, g1, dyd1), (y2, g2, dyd2))):
            yd = jnp.dot(y_ref[...], wb_ref[n], preferred_element_type=F32)
            sg = _sigmoid(g_ref[...].astype(F32) + bg_ref[:, n * D:(n + 1) * D])
            dgate = dm * yd * sg * (1.0 - sg)
            dg_ref[:, n * D:(n + 1) * D] = dgate.astype(BF16)
            parts.append(jnp.sum(dgate, axis=0, keepdims=True))
            dyd = (dm * sg).astype(BF16)
            dyd_ref[...] = dyd
            dy = jnp.dot(dyd, wbT_ref[n], preferred_element_type=F32)
            if n == 0:
                dya_ref[...] = dy
            elif n == 1:
                dyb_ref[...] = dy
            else:
                for h in range(H):
                    do_ref[h] = dy[:, h * hd:(h + 1) * hd]
        part = jnp.concatenate(parts, axis=1)

        @pl.when(i == 0)
        def _():
            dbg_ref[...] = part

        @pl.when(i > 0)
        def _():
            dbg_ref[...] += part

    ysp = pl.BlockSpec((ts, C), lambda i: (i, 0))
    dsp = pl.BlockSpec((ts, D), lambda i: (i, 0))
    gsp = lambda n: pl.BlockSpec((ts, D), lambda i: (i, gate_col0 + n))
    dshp = jax.ShapeDtypeStruct((S, D), BF16)
    yshp = jax.ShapeDtypeStruct((S, C), F32)
    return pl.pallas_call(
        body, name=name, grid=(S // ts,),
        in_specs=[ysp, ysp, ysp, pl.BlockSpec((3, C, D), lambda i: (0, 0, 0)),
                  pl.BlockSpec((3, D, C), lambda i: (0, 0, 0)), gsp(0), gsp(1), gsp(2),
                  pl.BlockSpec((1, 3 * D), lambda i: (0, 0)), dsp],
        out_specs=[pl.BlockSpec((ts, 3 * D), lambda i: (i, 0)), dsp, dsp, dsp, ysp, ysp,
                   pl.BlockSpec((H, ts, hd), lambda i: (0, i, 0)), pl.BlockSpec((1, 3 * D), lambda i: (0, 0))],
        out_shape=[jax.ShapeDtypeStruct((S, 3 * D), BF16), dshp, dshp, dshp, yshp, yshp,
                   jax.ShapeDtypeStruct((H, S, hd), F32), jax.ShapeDtypeStruct((1, 3 * D), F32)],
        compiler_params=_params("arbitrary"),
    )(*ys, wb, wbT, P, P, P, b_gate, dmerged)


def _swiglu_fwd(gu, *, name):
    S, F2 = gu.shape
    F = F2 // 2
    ts, tf = _tile(S, 512, 8), _tile(F, 1536, LANES)
    nf = F // tf

    def body(g_ref, u_ref, o_ref):
        gt = g_ref[...].astype(F32)
        o_ref[...] = (gt * _sigmoid(gt) * u_ref[...].astype(F32)).astype(BF16)

    return pl.pallas_call(
        body, name=name, grid=(S // ts, nf),
        in_specs=[pl.BlockSpec((ts, tf), lambda i, j: (i, j)), pl.BlockSpec((ts, tf), lambda i, j: (i, j + nf))],
        out_specs=pl.BlockSpec((ts, tf), lambda i, j: (i, j)),
        out_shape=jax.ShapeDtypeStruct((S, F), BF16),
        compiler_params=_params("parallel", "parallel"),
    )(gu, gu)


def _swiglu_bwd(gu, dact, *, name):
    S, F2 = gu.shape
    F = F2 // 2
    ts = _tile(S, 512, 8)

    def body(gu_ref, d_ref, o_ref):
        gt, up, da = gu_ref[:, 0:F].astype(F32), gu_ref[:, F:F2].astype(F32), d_ref[...].astype(F32)
        sg = _sigmoid(gt)
        o_ref[:, 0:F] = (da * up * sg * (1.0 + gt * (1.0 - sg))).astype(BF16)
        o_ref[:, F:F2] = (da * gt * sg).astype(BF16)

    return pl.pallas_call(
        body, name=name, grid=(S // ts,),
        in_specs=[pl.BlockSpec((ts, F2), lambda i: (i, 0)), pl.BlockSpec((ts, F), lambda i: (i, 0))],
        out_specs=pl.BlockSpec((ts, F2), lambda i: (i, 0)),
        out_shape=jax.ShapeDtypeStruct((S, F2), BF16),
        compiler_params=_params("parallel"),
    )(gu, dact)


def _loss_grad(y, target, *, name):
    S, D = y.shape
    ts = _tile(S, 512, 8)

    def body(y_ref, t_ref, dy_ref, l_ref, dyb_ref):
        i = pl.program_id(0)
        err = y_ref[...] - t_ref[...]
        dy_ref[...] = err * (1.0 / D)
        dyb_ref[...] = (err * (1.0 / D)).astype(BF16)
        part = jnp.broadcast_to(jnp.sum(jnp.sum(err * err, axis=1, keepdims=True), axis=0, keepdims=True) * (0.5 / D),
                                (1, LANES))

        @pl.when(i == 0)
        def _():
            l_ref[...] = part

        @pl.when(i > 0)
        def _():
            l_ref[...] += part

    row = pl.BlockSpec((ts, D), lambda i: (i, 0))
    return pl.pallas_call(
        body, name=name, grid=(S // ts,), in_specs=[row, row],
        out_specs=[row, pl.BlockSpec((1, LANES), lambda i: (0, 0)), row],
        out_shape=[jax.ShapeDtypeStruct((S, D), F32), jax.ShapeDtypeStruct((1, LANES), F32),
                   jax.ShapeDtypeStruct((S, D), BF16)],
        compiler_params=_params("arbitrary"),
    )(y, target)


def _adamw(w, m, v, gs_list, *, name):
    R, W = w.shape
    rows = [g.shape[1] for g in gs_list]
    tr = _tile(math.gcd(*rows), max(16, (2048 * LANES // W) // 16 * 16), 16)
    first = [sum(rows[:t]) // tr for t in range(len(rows))]
    c1 = 1.0 / (1.0 - ADAM_B1 ** ADAM_STEP)
    c2 = 1.0 / (1.0 - ADAM_B2 ** ADAM_STEP)

    def body(w_ref, m_ref, v_ref, *refs):
        gs_refs, (g_ref, d_ref, nm_ref, nv_ref) = refs[:len(rows)], refs[len(rows):]
        i = pl.program_id(0)
        g = None
        for t, gs_ref in enumerate(gs_refs):
            gt = gs_ref[0].astype(F32)
            for s in range(1, gs_ref.shape[0]):
                gt = gt + gs_ref[s].astype(F32)
            g = gt if g is None else jnp.where(i >= first[t], gt, g)
        nm = ADAM_B1 * m_ref[...] + (1.0 - ADAM_B1) * g
        nv = ADAM_B2 * v_ref[...] + (1.0 - ADAM_B2) * (g * g)
        g_ref[...] = g
        nm_ref[...] = nm
        nv_ref[...] = nv
        d_ref[...] = -ADAM_LR * ((nm * c1) / (jnp.sqrt(nv * c2) + ADAM_EPS) + ADAM_WD * w_ref[...])

    row = pl.BlockSpec((tr, W), lambda i: (i, 0))
    slots = [pl.BlockSpec((g.shape[0], tr, W),
                          lambda i, b0=first[t], nb=rows[t] // tr: (0, jnp.clip(i - b0, 0, nb - 1), 0))
             for t, g in enumerate(gs_list)]
    shp = jax.ShapeDtypeStruct((R, W), F32)
    return pl.pallas_call(
        body, name=name, grid=(R // tr,), in_specs=[row, row, row] + slots,
        out_specs=[row, row, row, row], out_shape=[shp, shp, shp, shp],
        compiler_params=_params("parallel"),
    )(w, m, v, *gs_list)


def _slot_sum(gs, *, name):
    ns, R, _ = gs.shape
    tr = _tile(R, 2048, 16)

    def body(gs_ref, o_ref):
        g = gs_ref[0]
        for s in range(1, ns):
            g = g + gs_ref[s]
        o_ref[...] = g

    return pl.pallas_call(
        body, name=name, grid=(R // tr,),
        in_specs=[pl.BlockSpec((ns, tr, LANES), lambda i: (0, i, 0))],
        out_specs=pl.BlockSpec((tr, LANES), lambda i: (i, 0)),
        out_shape=jax.ShapeDtypeStruct((R, LANES), F32),
        compiler_params=_params("parallel"),
    )(gs)


def _comm_scratch(n):
    return [pltpu.SemaphoreType.DMA((7 * n,)), pltpu.SemaphoreType.DMA((7 * n,)), pltpu.SemaphoreType.DMA((n,))]


def _gather_schedule(x_ref, out_ref, send_sems, recv_sems, local_sem, base):
    x, y, c = lax.axis_index("x"), lax.axis_index("y"), lax.axis_index("c")
    me, sibling = (x, y, c), (x, y, 1 - c)
    chips = [(1 - x, y), (x, 1 - y), (1 - x, 1 - y)]

    def slot(px, py, pc):
        return out_ref.at[4 * px + 2 * py + pc]

    def copy(k, block, to, src=None):
        return pltpu.make_async_remote_copy(
            src_ref=slot(*block) if src is None else src, dst_ref=slot(*block),
            send_sem=send_sems.at[base + k], recv_sem=recv_sems.at[base + k], device_id=to, device_id_type=MESH)

    mine = pltpu.make_async_copy(x_ref, slot(*me), local_sem)
    first = [copy(0, me, sibling, src=x_ref)]
    first += [copy(1 + j, me, (*chip, c), src=x_ref) for j, chip in enumerate(chips)]
    passed = [copy(4 + j, (*chip, c), sibling) for j, chip in enumerate(chips)]

    def begin():
        mine.start()
        for cp in first:
            cp.start()

    def forward():
        for j, chip in enumerate(chips):
            copy(1 + j, (*chip, c), me).wait_recv()
            passed[j].start()

    def finish():
        copy(0, sibling, me).wait_recv()
        for j, chip in enumerate(chips):
            copy(4 + j, (*chip, 1 - c), me).wait_recv()
        for cp in first + passed:
            cp.wait_send()
        mine.wait()

    return begin, forward, finish


def _exchange_schedule(x_ref, out_ref, send_sems, recv_sems, local_sem, base):
    x, y, c = lax.axis_index("x"), lax.axis_index("y"), lax.axis_index("c")
    me = 4 * x + 2 * y + c
    mine = pltpu.make_async_copy(x_ref.at[me], out_ref.at[me], local_sem)
    sends, recvs = [], []
    for k in range(1, N_DEV):
        px = 1 - x if k & 4 else x
        py = 1 - y if k & 2 else y
        pc = 1 - c if k & 1 else c
        peer = 4 * px + 2 * py + pc
        sems = dict(send_sem=send_sems.at[base + k - 1], recv_sem=recv_sems.at[base + k - 1],
                    device_id=(px, py, pc), device_id_type=MESH)
        sends.append(pltpu.make_async_remote_copy(src_ref=x_ref.at[peer], dst_ref=out_ref.at[me], **sems))
        recvs.append(pltpu.make_async_remote_copy(src_ref=x_ref.at[me], dst_ref=out_ref.at[peer], **sems))

    def begin():
        mine.start()
        for cp in sends:
            cp.start()

    def forward():
        pass

    def finish():
        for cp in recvs:
            cp.wait_recv()
        for cp in sends:
            cp.wait_send()
        mine.wait()

    return begin, forward, finish


def _comm_phases(schedule, x_refs, out_refs, send_sems, recv_sems, local_sems):
    parts = [schedule(x, o, send_sems, recv_sems, local_sems.at[t], 7 * t)
             for t, (x, o) in enumerate(zip(x_refs, out_refs))]

    def every(phase):
        def go():
            for p in parts:
                p[phase]()
        return go

    return every(0), every(1), every(2)


def _all_gather(xs, *, name):
    def body(x_ref, out_ref, send_sems, recv_sems, local_sems):
        for phase in _comm_phases(_gather_schedule, [x_ref], [out_ref], send_sems, recv_sems, local_sems):
            phase()

    return pl.pallas_call(
        body, name=name,
        in_specs=[pl.BlockSpec(memory_space=pl.ANY)], out_specs=pl.BlockSpec(memory_space=pl.ANY),
        out_shape=jax.ShapeDtypeStruct((N_DEV,) + xs.shape, xs.dtype), scratch_shapes=_comm_scratch(1),
        compiler_params=pltpu.CompilerParams(has_side_effects=True),
    )(xs)


PACK_ROWS = 16


def _pack_rows(parts, dtype, lead=()):
    rows = []
    for p in parts:
        r = p.reshape(lead + (-1, LANES)).astype(dtype)
        pad = (-r.shape[-2]) % PACK_ROWS
        rows.append(jnp.pad(r, [(0, 0)] * len(lead) + [(0, pad), (0, 0)]) if pad else r)
    return jnp.concatenate(rows, axis=len(lead))


def _unpack_rows(packed, shapes, lead=()):
    out, off = [], 0
    for shp in shapes:
        n = math.prod(shp) // LANES
        out.append(lax.slice_in_dim(packed, off, off + n, axis=len(lead)).reshape(lead + tuple(shp)))
        off += n + (-n) % PACK_ROWS
    return out


def _unshard(gathered, axis):
    g = jnp.moveaxis(gathered, 0, axis)
    shp = list(g.shape)
    shp[axis:axis + 2] = [shp[axis] * shp[axis + 1]]
    return g.reshape(shp)


def _reshard(full, axis):
    shp = list(full.shape)
    shp[axis:axis + 1] = [N_DEV, shp[axis] // N_DEV]
    return jnp.moveaxis(full.reshape(shp), axis, 0)


def _late_parts(L):
    return [(k, l) for k in BIG for l in range(L) if (k, l) != ("w_in", 0)]


def _late_groups(L, shard_shape):
    groups = {}
    for k, l in _late_parts(L):
        groups.setdefault(shard_shape[k][-1], []).append((k, l))
    return groups


def _local_step(x, target, w_in0, late_local, groups, shard_shape, sm):
    S, D = x.shape
    L = sm["mix_norm_g"].shape[0]
    wf = {k: [None] * L for k in BIG}
    wf["w_in"][0] = w_in0
    C = sm["conv_b"].shape[1]
    W = sm["sgu_ln_g"].shape[1]
    hd = sm["q_norm_g"].shape[1]
    CH = sm["sgu_w"].shape[2]
    A = w_in0.shape[1] - (3 * C + 2 * W + 3 * D)
    A = A // 3
    H = A // hd
    col_q = 3 * C + 2 * W
    qscale = 1.0 / math.sqrt(hd)
    tril = jnp.tril(jnp.ones((CH, CH), F32))
    (ftq, ftk), (btq, btk) = [(_tile(S, a, LANES), _tile(S, b, LANES)) for a, b in (ATTN_FWD_TILES, ATTN_BWD_TILES)]
    umat = lambda t: (lax.broadcasted_iota(jnp.int32, (t, t), 0) > lax.broadcasted_iota(jnp.int32, (t, t), 1)).astype(BF16)

    saved = []
    for l in range(L):
        n = f"l{l}_"
        g1 = sm["mix_norm_g"][l][None]
        h = _rmsnorm_fwd(x, g1, scale=1.0, out_dtype=BF16, name=n + "mixnorm")
        P = _matmul(h, wf["w_in"][l], name=n + "w_in", out_dtype=BF16)
        cw = jnp.pad(sm["conv_w"][l], ((0, 5), (0, 0)))
        cb = sm["conv_b"][l][None]
        ya = _conv_fwd(P, cw, cb, C=C, name=n + "conv")
        wm = (sm["sgu_w"][l] * tril).astype(BF16)
        bT = sm["sgu_b"][l].T
        lng, lnb = sm["sgu_ln_g"][l][None], sm["sgu_ln_b"][l][None]
        yb = _sgu_fwd(P, lng, lnb, wm, bT, W=W, cu=(3 * C) // W, cv=(3 * C) // W + 1, name=n + "sgu")
        gq, gk = sm["q_norm_g"][l][None], sm["k_norm_g"][l][None]
        qn, kn, vh = _qkv_heads_fwd(P, gq, gk, A=A, col0=col_q // A, hd=hd, qscale=qscale, name=n + "qkv")
        if l == 0:
            o, gathered = _attn_fwd(qn, kn, vh, umat(ftk), tq=ftq, tk=ftk, name=n + "attn",
                                    comm=(_gather_schedule, late_local, [(N_DEV,) + a.shape for a in late_local]))
            for (width, parts), g in zip(groups.items(), gathered):
                r = 0
                for k, ll in parts:
                    rows = math.prod(shard_shape[k]) // width
                    part = lax.slice_in_dim(g, r, r + rows, axis=1).reshape((N_DEV,) + tuple(shard_shape[k]))
                    wf[k][ll] = _unshard(part, SHARD_AXIS[k] - 1)
                    r += rows
        else:
            o = _attn_fwd(qn, kn, vh, umat(ftk), tq=ftq, tk=ftk, name=n + "attn")
        bg = sm["b_gate"][l][None]
        gate_col0 = (col_q + 3 * A) // D
        merged, yc = _merge_fwd(ya, yb, o, wf["w_branch_out"][l], P, bg, gate_col0=gate_col0, name=n + "merge")
        x1 = _matmul(merged, wf["w_o"][l], res=x, name=n + "w_o")
        g2 = sm["ffn_norm_g"][l][None]
        h2 = _rmsnorm_fwd(x1, g2, scale=1.0, out_dtype=BF16, name=n + "ffnnorm")
        gu = _matmul(h2, wf["w_gate_up"][l], name=n + "w_gate_up", out_dtype=BF16)
        act = _swiglu_fwd(gu, name=n + "swiglu")
        x2 = _matmul(act, wf["w_down"][l], res=x1, name=n + "w_down")
        saved.append(dict(x=x, h=h, P=P, cw=cw, cb=cb, ya=ya, wm=wm, bT=bT, lng=lng, lnb=lnb, yb=yb,
                          gq=gq, gk=gk, qn=qn, kn=kn, vh=vh, o=o, yc=yc, bg=bg,
                          gate_col0=gate_col0, merged=merged, x1=x1, g1=g1, g2=g2, h2=h2, gu=gu, act=act))
        x = x2

    dx, lpart, dxb = _loss_grad(x, target, name="loss")
    grads = {k: [None] * L for k in WEIGHTS}
    chunked = lambda k, g: g if k == "w_in" else _reshard(g, SHARD_AXIS[k] - 1)
    for l in reversed(range(L)):
        n = f"l{l}_b_"
        sv = saved[l]
        grads["w_down"][l] = _matmul_tn(sv["act"], dxb, name=n + "g_w_down", out_dtype=BF16)
        dact = _matmul(dxb, wf["w_down"][l].T, name=n + "d_act", out_dtype=BF16)
        dgu = _swiglu_bwd(sv["gu"], dact, name=n + "swiglu")
        grads["w_gate_up"][l] = _matmul_tn(sv["h2"], dgu, name=n + "g_w_gate_up", out_dtype=BF16)
        dh2 = _matmul(dgu, wf["w_gate_up"][l].T, name=n + "d_h2", out_dtype=BF16)
        dx1, dg2, dx1b = _rmsnorm_bwd(sv["x1"], sv["g2"], dh2, scale=1.0, dres=dx, name=n + "ffnnorm",
                                      bf16_copy=True)
        grads["ffn_norm_g"][l] = dg2[0]
        grads["w_o"][l] = _matmul_tn(sv["merged"], dx1b, name=n + "g_w_o", out_dtype=BF16)
        dmerged = _matmul(dx1b, wf["w_o"][l].T, name=n + "d_merged", out_dtype=BF16)
        ys = (sv["ya"], sv["yb"], sv["yc"])
        wb = wf["w_branch_out"][l]
        dgates, *dyd, dya, dyb, do, dbg = _merge_bwd(ys, wb, wb.transpose(0, 2, 1), sv["P"], sv["bg"], dmerged,
                                                     gate_col0=sv["gate_col0"], hd=hd, name=n + "merge")
        grads["b_gate"][l] = dbg[0]
        grads["w_branch_out"][l] = jnp.stack(
            [_matmul_tn(ys[i], dyd[i], name=n + f"g_w_branch{i}", out_dtype=BF16) for i in range(3)])
        dconv, dcw = _conv_bwd(sv["P"], dya, sv["cw"], sv["cb"], C=C, name=n + "conv")
        grads["conv_w"][l], grads["conv_b"][l] = dcw[0:3], dcw[3]
        wmT = sv["wm"].transpose(0, 2, 1)
        dsgu, dsw, dsb, dln = _sgu_bwd(sv["P"], dyb, sv["lng"], sv["lnb"], sv["wm"], wmT, sv["bT"], W=W,
                                       cu=(3 * C) // W, cv=(3 * C) // W + 1, name=n + "sgu")
        grads["sgu_w"][l], grads["sgu_b"][l] = dsw * tril, dsb[:, :, 0]
        grads["sgu_ln_g"][l], grads["sgu_ln_b"][l] = dln[0], dln[1]
        if l == 0:
            chunks = [jnp.concatenate([chunked(k, grads[k][ll]).reshape(N_DEV, -1, width) for k, ll in parts], axis=1)
                      for width, parts in groups.items()]
            dqn, dkn, dvh, late_recv = _attn_bwd(sv["qn"], sv["kn"], sv["vh"], sv["o"], do, umat(btk), tq=btq, tk=btk,
                                                 name=n + "attn",
                                                 comm=(_exchange_schedule, chunks, [c.shape for c in chunks]))
        else:
            dqn, dkn, dvh = _attn_bwd(sv["qn"], sv["kn"], sv["vh"], sv["o"], do, umat(btk), tq=btq, tk=btk,
                                      name=n + "attn")
        dqkv, dgq, dgk = _qkv_heads_bwd(sv["P"], sv["gq"], sv["gk"], dqn, dkn, dvh, A=A, col0=col_q // A, hd=hd,
                                        qscale=qscale, name=n + "qkv")
        grads["q_norm_g"][l], grads["k_norm_g"][l] = dgq[0], dgk[0]
        dP = jnp.concatenate([dconv, dsgu, dqkv, dgates], axis=1)
        grads["w_in"][l] = _matmul_tn(sv["h"], dP, name=n + "g_w_in", out_dtype=BF16, shard=dP.shape[1] // N_DEV)
        if l == 0:
            dh, early_recv = _matmul(dP, wf["w_in"][l].T, name=n + "d_h", out_dtype=BF16,
                                     comm=(_exchange_schedule, [grads["w_in"][0]], [grads["w_in"][0].shape]))
        else:
            dh = _matmul(dP, wf["w_in"][l].T, name=n + "d_h", out_dtype=BF16)
        dx, dg1, dxb = _rmsnorm_bwd(sv["x"], sv["g1"], dh, scale=1.0, dres=dx1, name=n + "mixnorm", bf16_copy=True)
        grads["mix_norm_g"][l] = dg1[0]
    return lpart[0, 0], dx, grads, early_recv, late_recv


def kernel(x, mix_norm_g, w_in, b_gate, conv_w, conv_b, sgu_ln_g, sgu_ln_b, sgu_w, sgu_b, q_norm_g, k_norm_g, w_branch_out, w_o, ffn_norm_g, w_gate_up, w_down, loss_target, m_mix_norm_g, m_w_in, m_b_gate, m_conv_w, m_conv_b, m_sgu_ln_g, m_sgu_ln_b, m_sgu_w, m_sgu_b, m_q_norm_g, m_k_norm_g, m_w_branch_out, m_w_o, m_ffn_norm_g, m_w_gate_up, m_w_down, v_mix_norm_g, v_w_in, v_b_gate, v_conv_w, v_conv_b, v_sgu_ln_g, v_sgu_ln_b, v_sgu_w, v_sgu_b, v_q_norm_g, v_k_norm_g, v_w_branch_out, v_w_o, v_ffn_norm_g, v_w_gate_up, v_w_down):
    w = dict(mix_norm_g=mix_norm_g, w_in=w_in, b_gate=b_gate, conv_w=conv_w, conv_b=conv_b, sgu_ln_g=sgu_ln_g,
             sgu_ln_b=sgu_ln_b, sgu_w=sgu_w, sgu_b=sgu_b, q_norm_g=q_norm_g, k_norm_g=k_norm_g,
             w_branch_out=w_branch_out, w_o=w_o, ffn_norm_g=ffn_norm_g, w_gate_up=w_gate_up, w_down=w_down)
    m = dict(mix_norm_g=m_mix_norm_g, w_in=m_w_in, b_gate=m_b_gate, conv_w=m_conv_w, conv_b=m_conv_b,
             sgu_ln_g=m_sgu_ln_g, sgu_ln_b=m_sgu_ln_b, sgu_w=m_sgu_w, sgu_b=m_sgu_b, q_norm_g=m_q_norm_g,
             k_norm_g=m_k_norm_g, w_branch_out=m_w_branch_out, w_o=m_w_o, ffn_norm_g=m_ffn_norm_g,
             w_gate_up=m_w_gate_up, w_down=m_w_down)
    v = dict(mix_norm_g=v_mix_norm_g, w_in=v_w_in, b_gate=v_b_gate, conv_w=v_conv_w, conv_b=v_conv_b,
             sgu_ln_g=v_sgu_ln_g, sgu_ln_b=v_sgu_ln_b, sgu_w=v_sgu_w, sgu_b=v_sgu_b, q_norm_g=v_q_norm_g,
             k_norm_g=v_k_norm_g, w_branch_out=v_w_branch_out, w_o=v_w_o, ffn_norm_g=v_ffn_norm_g,
             w_gate_up=v_w_gate_up, w_down=v_w_down)
    me = 4 * lax.axis_index("x") + 2 * lax.axis_index("y") + lax.axis_index("c")

    L = w_in.shape[0]
    two_d = lambda a: a.reshape(-1, a.shape[-1])
    shard_shape = {k: w[k].shape[1:] for k in BIG}
    groups = _late_groups(L, shard_shape)
    w_in0 = _unshard(_all_gather(w_in[0].astype(BF16), name="gather_w_in0"), SHARD_AXIS["w_in"] - 1)
    late_local = [jnp.concatenate([two_d(w[k][l]).astype(BF16) for k, l in parts], axis=0) for parts in groups.values()]
    conv_g = _all_gather(_pack_rows([conv_w], F32), name="gather_conv_w")
    sm = {k: w[k] for k in SMALL}
    sm["conv_w"] = _unshard(_unpack_rows(conv_g, [conv_w.shape], lead=(N_DEV,))[0], 2)

    lpart, dx, grads, early_recv, late_recv = _local_step(x[0], loss_target[0], w_in0, late_local, groups,
                                                          shard_shape, sm)
    loss = lax.psum(lpart, ("x", "y", "c"))

    where = {("w_in", 0): (early_recv[0], 0)}
    for (width, parts), recv in zip(groups.items(), late_recv):
        r = 0
        for k, l in parts:
            where[k, l] = (recv, r)
            r += math.prod(shard_shape[k]) // width
    out = {}
    for k in BIG:
        rows = math.prod(shard_shape[k]) // shard_shape[k][-1]
        gs_list = []
        for l in range(L):
            recv, r = where[k, l]
            if gs_list and gs_list[-1][0] is recv and gs_list[-1][2] == r:
                gs_list[-1][2] = r + rows
            else:
                gs_list.append([recv, r, r + rows])
        gs_list = [recv if (r0, r1) == (0, recv.shape[1]) else lax.slice_in_dim(recv, r0, r1, axis=1)
                   for recv, r0, r1 in gs_list]
        res = _adamw(two_d(w[k]), two_d(m[k]), two_d(v[k]), gs_list, name="adamw_" + k)
        for nm, r in zip(("grad", "delta", "new_m", "new_v"), res):
            out[nm, k] = r.reshape(w[k].shape)

    small_grads = [jnp.stack(grads[k]) for k in SMALL]
    small_shapes = [g.shape for g in small_grads]
    sg = _all_gather(_pack_rows(small_grads, F32), name="gather_small_grads")
    gsum = _slot_sum(sg, name="sum_small_grads")
    gsmall = dict(zip(SMALL, _unpack_rows(gsum, small_shapes)))
    cshard = conv_w.shape[2]
    gsmall["conv_w"] = lax.dynamic_slice_in_dim(gsmall["conv_w"], me * cshard, cshard, axis=2)
    own_shapes = [w[k].shape for k in SMALL]
    gs_, ds_, ms_, vs_ = _adamw(_pack_rows([w[k] for k in SMALL], F32), _pack_rows([m[k] for k in SMALL], F32),
                                _pack_rows([v[k] for k in SMALL], F32),
                                [_pack_rows([gsmall[k] for k in SMALL], F32)[None]], name="adamw_small")
    for nm, packed in (("grad", gs_), ("delta", ds_), ("new_m", ms_), ("new_v", vs_)):
        for k, a in zip(SMALL, _unpack_rows(packed, own_shapes)):
            out[nm, k] = a

    res = [loss, dx[None]]
    for nm in ("grad", "delta", "new_m", "new_v"):
        res += [out[nm, k] for k in WEIGHTS]
    return tuple(res)
```
